```python
import jax, jax.numpy as jnp
from jax import lax
import numpy as np

D_MODEL = 1024
BATCH = 8
SEQ = 16384
DEPTH = 1

N_META = 16
EPS = 1e-6
ATT_HEADS = 8
Q_LORA = 384
KV_LORA = 256
QK_NOPE = 128
QK_ROPE = 64
V_HEAD = 128
ROPE_THETA = 10000.0
ATT_WIDTH = ATT_HEADS * V_HEAD
Q_BLOCK = 128
SSM_HEADS = 16
SSM_HEAD_DIM = 64
SSM_WIDTH = SSM_HEADS * SSM_HEAD_DIM
SSM_GROUPS = 2
SSM_HEADS_PER_GROUP = SSM_HEADS // SSM_GROUPS
SSM_STATE = 128
CONV_K = 4
CHUNK = 128
CONV_DIM = SSM_WIDTH + 2 * SSM_GROUPS * SSM_STATE
MIX_WIDTH = ATT_WIDTH + SSM_WIDTH
D_FF = 4 * D_MODEL
IN_SPLITS = (Q_LORA, KV_LORA, QK_ROPE, SSM_WIDTH, CONV_DIM, SSM_HEADS)
IN_WIDTH = sum(IN_SPLITS)
IN_OFFSETS = tuple(int(v) for v in np.cumsum(IN_SPLITS)[:-1])

kernel_name = "hymba_mla_ssd_sandwich_layer"


def rmsnorm(x, w):
    x32 = x.astype(jnp.float32)
    y = x32 * lax.rsqrt(jnp.mean(jnp.square(x32), axis=-1, keepdims=True) + EPS)
    return y.astype(x.dtype) * w


def gated_group_rmsnorm(y, z, w):
    g = (y * jax.nn.silu(z)).astype(jnp.float32)
    shp = g.shape
    g = g.reshape(*shp[:-1], SSM_GROUPS, shp[-1] // SSM_GROUPS)
    g = g * lax.rsqrt(jnp.mean(jnp.square(g), axis=-1, keepdims=True) + EPS)
    return g.reshape(shp).astype(z.dtype) * w


def rope_tables(length, dtype):
    inv_freq = ROPE_THETA ** (-jnp.arange(0, QK_ROPE, 2, dtype=jnp.float32) / QK_ROPE)
    ang = jnp.arange(length, dtype=jnp.float32)[:, None] * inv_freq[None, :]
    return jnp.cos(ang).astype(dtype), jnp.sin(ang).astype(dtype)


def apply_rope(x, cos, sin):
    x1, x2 = jnp.split(x, 2, axis=-1)
    return jnp.concatenate([x1 * cos - x2 * sin, x2 * cos + x1 * sin], axis=-1)


def mla_attention(q_nope, q_rope, k_nope, k_rope, v):
    bsz, length = q_nope.shape[:2]
    scale = (QK_NOPE + QK_ROPE) ** -0.5
    pos = jnp.arange(length)

    def attend(qn, qr, qpos, kn, kr, vv, kpos):
        s = jnp.einsum('bqhd,bkhd->bhqk', qn, kn) + jnp.einsum('bqhd,bkd->bhqk', qr, kr)
        s = s.astype(jnp.float32) * scale
        s = jnp.where(kpos[None, None, None, :] <= qpos[None, None, :, None], s, -jnp.inf)
        p = jax.nn.softmax(s, axis=-1).astype(vv.dtype)
        return jnp.einsum('bhqk,bkhd->bqhd', p, vv)

    out_meta = attend(q_nope[:, :N_META], q_rope[:, :N_META], pos[:N_META],
                      k_nope[:, :N_META], k_rope[:, :N_META], v[:, :N_META], pos[:N_META])
    n_blocks = (length - N_META) // Q_BLOCK

    def to_blocks(t):
        t = t[:, N_META:]
        return jnp.moveaxis(t.reshape(bsz, n_blocks, Q_BLOCK, *t.shape[2:]), 1, 0)

    def block_fn(args):
        qn, qr, qp = args
        return attend(qn, qr, qp, k_nope, k_rope, v, pos)

    out_real = lax.map(block_fn, (to_blocks(q_nope), to_blocks(q_rope), pos[N_META:].reshape(n_blocks, Q_BLOCK)))
    out_real = jnp.moveaxis(out_real, 0, 1).reshape(bsz, length - N_META, ATT_HEADS, V_HEAD)
    return jnp.concatenate([out_meta, out_real], axis=1)


def causal_depthwise_conv(u, w, bias):
    out = lax.conv_general_dilated(u, w[:, None, :], window_strides=(1,), padding=((CONV_K - 1, 0),),
                                   dimension_numbers=('NWC', 'WIO', 'NWC'), feature_group_count=u.shape[-1])
    return out + bias


def ssd_chunked(x, dt, a, b_mat, c_mat, state0, chunk):
    bsz, total = x.shape[:2]
    nc = total // chunk

    def chunks(t):
        return t.reshape(bsz, nc, chunk, *t.shape[2:])

    x, dt, b_mat, c_mat = chunks(x), chunks(dt), chunks(b_mat), chunks(c_mat)
    a_cum = jnp.cumsum(dt * a, axis=2)
    seg = a_cum[:, :, :, None] - a_cum[:, :, None, :]
    causal = jnp.tril(jnp.ones((chunk, chunk), dtype=bool))[:, :, None, None]
    decay = jnp.where(causal, jnp.exp(jnp.where(causal, seg, 0.0)), 0.0)
    cb = jnp.einsum('bclgn,bcsgn->bclsg', c_mat, b_mat)
    w_ls = cb[..., None] * decay * dt[:, :, None]
    y_diag = jnp.einsum('bclsgr,bcsgrp->bclgrp', w_ls, x)
    decay_to_end = jnp.exp(a_cum[:, :, -1:] - a_cum)
    states = jnp.einsum('bcsgn,bcsgr,bcsgrp->bcgrpn', b_mat, decay_to_end * dt, x)
    chunk_decay = jnp.exp(a_cum[:, :, -1])

    def step(h, inp):
        dec, st = inp
        return dec[..., None, None] * h + st, h

    final, prev = lax.scan(step, state0, (jnp.moveaxis(chunk_decay, 1, 0), jnp.moveaxis(states, 1, 0)))
    prev = jnp.moveaxis(prev, 0, 1)
    y_off = jnp.einsum('bclgn,bcgrpn,bclgr->bclgrp', c_mat, prev, jnp.exp(a_cum))
    y = (y_diag + y_off).reshape(bsz, total, *x.shape[3:])
    return y, final


def hybrid_mixer(h, w_in, q_a_norm, w_q_up, kv_a_norm, w_kv_up, conv_w, conv_b,
                 dt_bias, a_log, d_skip, ssm_norm, w_out, cos, sin):
    bsz, length, _ = h.shape
    proj = h @ w_in
    c_q, c_kv, k_rope, z, xbc, dt_raw = jnp.split(proj, IN_OFFSETS, axis=-1)

    q = (rmsnorm(c_q, q_a_norm) @ w_q_up).reshape(bsz, length, ATT_HEADS, QK_NOPE + QK_ROPE)
    q_nope, q_rope = q[..., :QK_NOPE], q[..., QK_NOPE:]
    kv = (rmsnorm(c_kv, kv_a_norm) @ w_kv_up).reshape(bsz, length, ATT_HEADS, QK_NOPE + V_HEAD)
    k_nope, v = kv[..., :QK_NOPE], kv[..., QK_NOPE:]
    q_rope = apply_rope(q_rope, cos[None, :, None, :], sin[None, :, None, :])
    k_rope = apply_rope(k_rope, cos[None], sin[None])
    att = mla_attention(q_nope, q_rope, k_nope, k_rope, v).reshape(bsz, length, ATT_WIDTH)

    xbc = jax.nn.silu(causal_depthwise_conv(xbc, conv_w, conv_b))
    xs, bm, cm = jnp.split(xbc, (SSM_WIDTH, SSM_WIDTH + SSM_GROUPS * SSM_STATE), axis=-1)
    xs32 = xs.astype(jnp.float32).reshape(bsz, length, SSM_GROUPS, SSM_HEADS_PER_GROUP, SSM_HEAD_DIM)
    bm = bm.astype(jnp.float32).reshape(bsz, length, SSM_GROUPS, SSM_STATE)
    cm = cm.astype(jnp.float32).reshape(bsz, length, SSM_GROUPS, SSM_STATE)
    dt = jax.nn.softplus(dt_raw.astype(jnp.float32) + dt_bias.astype(jnp.float32))
    dt = dt.reshape(bsz, length, SSM_GROUPS, SSM_HEADS_PER_GROUP)
    a = -jnp.exp(a_log.astype(jnp.float32)).reshape(SSM_GROUPS, SSM_HEADS_PER_GROUP)
    state0 = jnp.zeros((bsz, SSM_GROUPS, SSM_HEADS_PER_GROUP, SSM_HEAD_DIM, SSM_STATE), jnp.float32)
    y_meta, st = ssd_chunked(xs32[:, :N_META], dt[:, :N_META], a, bm[:, :N_META], cm[:, :N_META], state0, N_META)
    y_real, _ = ssd_chunked(xs32[:, N_META:], dt[:, N_META:], a, bm[:, N_META:], cm[:, N_META:], st, CHUNK)
    y = jnp.concatenate([y_meta, y_real], axis=1)
    y = y + d_skip.astype(jnp.float32).reshape(SSM_GROUPS, SSM_HEADS_PER_GROUP)[:, :, None] * xs32
    y = y.reshape(bsz, length, SSM_WIDTH).astype(h.dtype)
    ssm = gated_group_rmsnorm(y, z, ssm_norm)

    return jnp.concatenate([att, ssm], axis=-1) @ w_out


def _fwd_setup_inputs(seed: int = 0) -> dict:
    key = jax.random.key(seed)
    ks = jax.random.split(key, 24)
    f32 = jnp.float32

    def nrm(k, shape, fan_in):
        return jax.random.normal(k, shape, f32) * (fan_in ** -0.5)

    def gain(k, shape):
        return 1.0 + 0.05 * jax.random.normal(k, shape, f32)

    dt0 = jnp.exp(jax.random.uniform(ks[10], (DEPTH, SSM_HEADS), f32) * (jnp.log(0.1) - jnp.log(0.001)) + jnp.log(0.001))
    return {
        "x": jax.random.normal(ks[0], (BATCH, SEQ, D_MODEL), f32),
        "meta_tokens": jax.random.normal(ks[1], (N_META, D_MODEL), f32),
        "norm_mix_pre": gain(ks[2], (DEPTH, D_MODEL)),
        "w_in": nrm(ks[3], (DEPTH, D_MODEL, IN_WIDTH), D_MODEL),
        "q_a_norm": gain(ks[4], (DEPTH, Q_LORA)),
        "w_q_up": nrm(ks[5], (DEPTH, Q_LORA, ATT_HEADS * (QK_NOPE + QK_ROPE)), Q_LORA),
        "kv_a_norm": gain(ks[6], (DEPTH, KV_LORA)),
        "w_kv_up": nrm(ks[7], (DEPTH, KV_LORA, ATT_HEADS * (QK_NOPE + V_HEAD)), KV_LORA),
        "conv_w": nrm(ks[8], (DEPTH, CONV_K, CONV_DIM), CONV_K),
        "conv_b": 0.01 * jax.random.normal(ks[9], (DEPTH, CONV_DIM), f32),
        "dt_bias": dt0 + jnp.log(-jnp.expm1(-dt0)),
        "a_log": jnp.log(jax.random.uniform(ks[11], (DEPTH, SSM_HEADS), f32, 1.0, 16.0)),
        "d_skip": gain(ks[12], (DEPTH, SSM_HEADS)),
        "ssm_norm": gain(ks[13], (DEPTH, SSM_WIDTH)),
        "w_out": nrm(ks[14], (DEPTH, MIX_WIDTH, D_MODEL), MIX_WIDTH),
        "norm_mix_post": gain(ks[15], (DEPTH, D_MODEL)),
        "norm_mlp_pre": gain(ks[16], (DEPTH, D_MODEL)),
        "w_mlp_up": nrm(ks[17], (DEPTH, D_MODEL, D_FF), D_MODEL),
        "w_mlp_down": nrm(ks[18], (DEPTH, D_FF, D_MODEL), D_FF),
        "norm_mlp_post": gain(ks[19], (DEPTH, D_MODEL)),
    }


def _fwd_reference(x, meta_tokens, norm_mix_pre, w_in, q_a_norm, w_q_up, kv_a_norm, w_kv_up, conv_w, conv_b,
              dt_bias, a_log, d_skip, ssm_norm, w_out, norm_mix_post, norm_mlp_pre, w_mlp_up, w_mlp_down,
              norm_mlp_post):
    bsz = x.shape[0]
    h = jnp.concatenate([jnp.broadcast_to(meta_tokens[None].astype(x.dtype), (bsz, N_META, D_MODEL)), x], axis=1)
    cos, sin = rope_tables(h.shape[1], h.dtype)
    for layer in range(DEPTH):
        mix = hybrid_mixer(rmsnorm(h, norm_mix_pre[layer]), w_in[layer], q_a_norm[layer], w_q_up[layer],
                           kv_a_norm[layer], w_kv_up[layer], conv_w[layer], conv_b[layer], dt_bias[layer],
                           a_log[layer], d_skip[layer], ssm_norm[layer], w_out[layer], cos, sin)
        h = h + rmsnorm(mix, norm_mix_post[layer])
        f = jnp.square(jax.nn.relu(rmsnorm(h, norm_mlp_pre[layer]) @ w_mlp_up[layer])) @ w_mlp_down[layer]
        h = h + rmsnorm(f, norm_mlp_post[layer])
    return h[:, N_META:]


import jax as _jax
import jax.numpy as _jnp

TWIN_FORMAT = 'train_step'
FWD_PARAMS = ['x', 'meta_tokens', 'norm_mix_pre', 'w_in', 'q_a_norm', 'w_q_up', 'kv_a_norm', 'w_kv_up', 'conv_w', 'conv_b', 'dt_bias', 'a_log', 'd_skip', 'ssm_norm', 'w_out', 'norm_mix_post', 'norm_mlp_pre', 'w_mlp_up', 'w_mlp_down', 'norm_mlp_post']
TWIN_WEIGHTS = ['meta_tokens', 'norm_mix_pre', 'w_in', 'q_a_norm', 'w_q_up', 'kv_a_norm', 'w_kv_up', 'conv_w', 'conv_b', 'dt_bias', 'a_log', 'd_skip', 'ssm_norm', 'w_out', 'norm_mix_post', 'norm_mlp_pre', 'w_mlp_up', 'w_mlp_down', 'norm_mlp_post']
TWIN_DIFF_INPUT = 'x'
TWIN_INPUTS = ['x', 'meta_tokens', 'norm_mix_pre', 'w_in', 'q_a_norm', 'w_q_up', 'kv_a_norm', 'w_kv_up', 'conv_w', 'conv_b', 'dt_bias', 'a_log', 'd_skip', 'ssm_norm', 'w_out', 'norm_mix_post', 'norm_mlp_pre', 'w_mlp_up', 'w_mlp_down', 'norm_mlp_post', 'loss_target', 'm_meta_tokens', 'm_norm_mix_pre', 'm_w_in', 'm_q_a_norm', 'm_w_q_up', 'm_kv_a_norm', 'm_w_kv_up', 'm_conv_w', 'm_conv_b', 'm_dt_bias', 'm_a_log', 'm_d_skip', 'm_ssm_norm', 'm_w_out', 'm_norm_mix_post', 'm_norm_mlp_pre', 'm_w_mlp_up', 'm_w_mlp_down', 'm_norm_mlp_post', 'v_meta_tokens', 'v_norm_mix_pre', 'v_w_in', 'v_q_a_norm', 'v_w_q_up', 'v_kv_a_norm', 'v_w_kv_up', 'v_conv_w', 'v_conv_b', 'v_dt_bias', 'v_a_log', 'v_d_skip', 'v_ssm_norm', 'v_w_out', 'v_norm_mix_post', 'v_norm_mlp_pre', 'v_w_mlp_up', 'v_w_mlp_down', 'v_norm_mlp_post']
TWIN_OUTPUTS = ['loss', 'grad_x', 'grad_meta_tokens', 'grad_norm_mix_pre', 'grad_w_in', 'grad_q_a_norm', 'grad_w_q_up', 'grad_kv_a_norm', 'grad_w_kv_up', 'grad_conv_w', 'grad_conv_b', 'grad_dt_bias', 'grad_a_log', 'grad_d_skip', 'grad_ssm_norm', 'grad_w_out', 'grad_norm_mix_post', 'grad_norm_mlp_pre', 'grad_w_mlp_up', 'grad_w_mlp_down', 'grad_norm_mlp_post', 'delta_meta_tokens', 'delta_norm_mix_pre', 'delta_w_in', 'delta_q_a_norm', 'delta_w_q_up', 'delta_kv_a_norm', 'delta_w_kv_up', 'delta_conv_w', 'delta_conv_b', 'delta_dt_bias', 'delta_a_log', 'delta_d_skip', 'delta_ssm_norm', 'delta_w_out', 'delta_norm_mix_post', 'delta_norm_mlp_pre', 'delta_w_mlp_up', 'delta_w_mlp_down', 'delta_norm_mlp_post', 'new_m_meta_tokens', 'new_m_norm_mix_pre', 'new_m_w_in', 'new_m_q_a_norm', 'new_m_w_q_up', 'new_m_kv_a_norm', 'new_m_w_kv_up', 'new_m_conv_w', 'new_m_conv_b', 'new_m_dt_bias', 'new_m_a_log', 'new_m_d_skip', 'new_m_ssm_norm', 'new_m_w_out', 'new_m_norm_mix_post', 'new_m_norm_mlp_pre', 'new_m_w_mlp_up', 'new_m_w_mlp_down', 'new_m_norm_mlp_post', 'new_v_meta_tokens', 'new_v_norm_mix_pre', 'new_v_w_in', 'new_v_q_a_norm', 'new_v_w_q_up', 'new_v_kv_a_norm', 'new_v_w_kv_up', 'new_v_conv_w', 'new_v_conv_b', 'new_v_dt_bias', 'new_v_a_log', 'new_v_d_skip', 'new_v_ssm_norm', 'new_v_w_out', 'new_v_norm_mix_post', 'new_v_norm_mlp_pre', 'new_v_w_mlp_up', 'new_v_w_mlp_down', 'new_v_norm_mlp_post']
TWIN_LEAF_KINDS = {'loss': 'loss', 'grad_x': 'grad_x', 'grad_meta_tokens': 'grad_w', 'grad_norm_mix_pre': 'grad_w', 'grad_w_in': 'grad_w', 'grad_q_a_norm': 'grad_w', 'grad_w_q_up': 'grad_w', 'grad_kv_a_norm': 'grad_w', 'grad_w_kv_up': 'grad_w', 'grad_conv_w': 'grad_w', 'grad_conv_b': 'grad_w', 'grad_dt_bias': 'grad_w', 'grad_a_log': 'grad_w', 'grad_d_skip': 'grad_w', 'grad_ssm_norm': 'grad_w', 'grad_w_out': 'grad_w', 'grad_norm_mix_post': 'grad_w', 'grad_norm_mlp_pre': 'grad_w', 'grad_w_mlp_up': 'grad_w', 'grad_w_mlp_down': 'grad_w', 'grad_norm_mlp_post': 'grad_w', 'delta_meta_tokens': 'delta_w', 'delta_norm_mix_pre': 'delta_w', 'delta_w_in': 'delta_w', 'delta_q_a_norm': 'delta_w', 'delta_w_q_up': 'delta_w', 'delta_kv_a_norm': 'delta_w', 'delta_w_kv_up': 'delta_w', 'delta_conv_w': 'delta_w', 'delta_conv_b': 'delta_w', 'delta_dt_bias': 'delta_w', 'delta_a_log': 'delta_w', 'delta_d_skip': 'delta_w', 'delta_ssm_norm': 'delta_w', 'delta_w_out': 'delta_w', 'delta_norm_mix_post': 'delta_w', 'delta_norm_mlp_pre': 'delta_w', 'delta_w_mlp_up': 'delta_w', 'delta_w_mlp_down': 'delta_w', 'delta_norm_mlp_post': 'delta_w', 'new_m_meta_tokens': 'new_m', 'new_m_norm_mix_pre': 'new_m', 'new_m_w_in': 'new_m', 'new_m_q_a_norm': 'new_m', 'new_m_w_q_up': 'new_m', 'new_m_kv_a_norm': 'new_m', 'new_m_w_kv_up': 'new_m', 'new_m_conv_w': 'new_m', 'new_m_conv_b': 'new_m', 'new_m_dt_bias': 'new_m', 'new_m_a_log': 'new_m', 'new_m_d_skip': 'new_m', 'new_m_ssm_norm': 'new_m', 'new_m_w_out': 'new_m', 'new_m_norm_mix_post': 'new_m', 'new_m_norm_mlp_pre': 'new_m', 'new_m_w_mlp_up': 'new_m', 'new_m_w_mlp_down': 'new_m', 'new_m_norm_mlp_post': 'new_m', 'new_v_meta_tokens': 'new_v', 'new_v_norm_mix_pre': 'new_v', 'new_v_w_in': 'new_v', 'new_v_q_a_norm': 'new_v', 'new_v_w_q_up': 'new_v', 'new_v_kv_a_norm': 'new_v', 'new_v_w_kv_up': 'new_v', 'new_v_conv_w': 'new_v', 'new_v_conv_b': 'new_v', 'new_v_dt_bias': 'new_v', 'new_v_a_log': 'new_v', 'new_v_d_skip': 'new_v', 'new_v_ssm_norm': 'new_v', 'new_v_w_out': 'new_v', 'new_v_norm_mix_post': 'new_v', 'new_v_norm_mlp_pre': 'new_v', 'new_v_w_mlp_up': 'new_v', 'new_v_w_mlp_down': 'new_v', 'new_v_norm_mlp_post': 'new_v'}


def _forward(args):
    return _fwd_reference(*[args[k] for k in FWD_PARAMS])


def _output_shape():
    def fwd():
        inp = _fwd_setup_inputs(0)
        return _fwd_reference(*[inp[k] for k in FWD_PARAMS])
    out = _jax.eval_shape(fwd)
    return out.shape, out.dtype

N_MICROBATCH = 1
ADAM_LR = 0.001
ADAM_B1 = 0.9
ADAM_B2 = 0.999
ADAM_EPS = 1e-08
ADAM_WD = 0.01
ADAM_STEP = 10
PER_EXAMPLE_BATCH_AXIS = {'x': 0, 'loss_target': 0}
SHARED_INPUTS = []
_WEIGHT_DTYPES = {'meta_tokens': _jnp.float32, 'norm_mix_pre': _jnp.float32, 'w_in': _jnp.float32, 'q_a_norm': _jnp.float32, 'w_q_up': _jnp.float32, 'kv_a_norm': _jnp.float32, 'w_kv_up': _jnp.float32, 'conv_w': _jnp.float32, 'conv_b': _jnp.float32, 'dt_bias': _jnp.float32, 'a_log': _jnp.float32, 'd_skip': _jnp.float32, 'ssm_norm': _jnp.float32, 'w_out': _jnp.float32, 'norm_mix_post': _jnp.float32, 'norm_mlp_pre': _jnp.float32, 'w_mlp_up': _jnp.float32, 'w_mlp_down': _jnp.float32, 'norm_mlp_post': _jnp.float32}
MOMENT_SCALE = {'meta_tokens': 8.146269e-02, 'norm_mix_pre': 1.583866e+00, 'w_in': 9.313748e-01, 'q_a_norm': 2.305117e-01, 'w_q_up': 1.095301e-01, 'kv_a_norm': 6.642343e-01, 'w_kv_up': 2.727724e-01, 'conv_w': 5.581626e+00, 'conv_b': 2.029916e+01, 'dt_bias': 2.660722e+00, 'a_log': 2.169770e+01, 'd_skip': 4.669727e+01, 'ssm_norm': 1.183362e+01, 'w_out': 9.862816e+00, 'norm_mix_post': 1.293368e+02, 'norm_mlp_pre': 3.484188e+00, 'w_mlp_up': 1.819960e+00, 'w_mlp_down': 9.911731e+00, 'norm_mlp_post': 1.331515e+02}


def _to_microbatches(a, axis):
    t = _jnp.moveaxis(a, axis, 0)
    t = t.reshape((N_MICROBATCH, t.shape[0] // N_MICROBATCH) + t.shape[1:])
    return _jnp.moveaxis(t, 1, axis + 1)


def setup_inputs(seed: int = 0) -> dict:
    inp = _fwd_setup_inputs(seed)
    key = _jax.random.fold_in(_jax.random.key(seed), 7919)
    shape, _ = _output_shape()
    out = dict(inp)
    out["loss_target"] = _jax.random.normal(_jax.random.fold_in(key, 0), shape, _jnp.float32)
    for i, name in enumerate(TWIN_WEIGHTS):
        w = inp[name].astype(_jnp.float32)
        if MOMENT_SCALE is None:
            s = _jnp.sqrt(_jnp.mean(_jnp.square(w)) + 1e-30)
        else:
            s = MOMENT_SCALE[name]
        km, kv = _jax.random.split(_jax.random.fold_in(key, i + 1))
        out[name] = w
        out["m_" + name] = s * _jax.random.normal(km, w.shape, _jnp.float32)
        out["v_" + name] = (s * s) * _jax.random.uniform(kv, w.shape, _jnp.float32, 0.5, 1.5)
    if N_MICROBATCH > 1:
        for name, axis in PER_EXAMPLE_BATCH_AXIS.items():
            out[name] = _to_microbatches(out[name], axis)
    return {'x': out['x'], 'meta_tokens': out['meta_tokens'], 'norm_mix_pre': out['norm_mix_pre'], 'w_in': out['w_in'], 'q_a_norm': out['q_a_norm'], 'w_q_up': out['w_q_up'], 'kv_a_norm': out['kv_a_norm'], 'w_kv_up': out['w_kv_up'], 'conv_w': out['conv_w'], 'conv_b': out['conv_b'], 'dt_bias': out['dt_bias'], 'a_log': out['a_log'], 'd_skip': out['d_skip'], 'ssm_norm': out['ssm_norm'], 'w_out': out['w_out'], 'norm_mix_post': out['norm_mix_post'], 'norm_mlp_pre': out['norm_mlp_pre'], 'w_mlp_up': out['w_mlp_up'], 'w_mlp_down': out['w_mlp_down'], 'norm_mlp_post': out['norm_mlp_post'], 'loss_target': out['loss_target'], 'm_meta_tokens': out['m_meta_tokens'], 'm_norm_mix_pre': out['m_norm_mix_pre'], 'm_w_in': out['m_w_in'], 'm_q_a_norm': out['m_q_a_norm'], 'm_w_q_up': out['m_w_q_up'], 'm_kv_a_norm': out['m_kv_a_norm'], 'm_w_kv_up': out['m_w_kv_up'], 'm_conv_w': out['m_conv_w'], 'm_conv_b': out['m_conv_b'], 'm_dt_bias': out['m_dt_bias'], 'm_a_log': out['m_a_log'], 'm_d_skip': out['m_d_skip'], 'm_ssm_norm': out['m_ssm_norm'], 'm_w_out': out['m_w_out'], 'm_norm_mix_post': out['m_norm_mix_post'], 'm_norm_mlp_pre': out['m_norm_mlp_pre'], 'm_w_mlp_up': out['m_w_mlp_up'], 'm_w_mlp_down': out['m_w_mlp_down'], 'm_norm_mlp_post': out['m_norm_mlp_post'], 'v_meta_tokens': out['v_meta_tokens'], 'v_norm_mix_pre': out['v_norm_mix_pre'], 'v_w_in': out['v_w_in'], 'v_q_a_norm': out['v_q_a_norm'], 'v_w_q_up': out['v_w_q_up'], 'v_kv_a_norm': out['v_kv_a_norm'], 'v_w_kv_up': out['v_w_kv_up'], 'v_conv_w': out['v_conv_w'], 'v_conv_b': out['v_conv_b'], 'v_dt_bias': out['v_dt_bias'], 'v_a_log': out['v_a_log'], 'v_d_skip': out['v_d_skip'], 'v_ssm_norm': out['v_ssm_norm'], 'v_w_out': out['v_w_out'], 'v_norm_mix_post': out['v_norm_mix_post'], 'v_norm_mlp_pre': out['v_norm_mlp_pre'], 'v_w_mlp_up': out['v_w_mlp_up'], 'v_w_mlp_down': out['v_w_mlp_down'], 'v_norm_mlp_post': out['v_norm_mlp_post']}


def _loss(weights, diff, rest, loss_target):
    with _jax.named_scope("forward"):
        args = {**rest, TWIN_DIFF_INPUT: diff, **{k: w.astype(_WEIGHT_DTYPES[k]) for k, w in weights.items()}}
        y = _forward(args)
    with _jax.named_scope("loss_head"):
        err = _jnp.square(y.astype(_jnp.float32) - loss_target)
        return 0.5 * _jnp.sum(_jnp.mean(err, axis=-1)) if err.ndim else 0.5 * err


def _adamw(w, g, m, v):
    m = ADAM_B1 * m + (1.0 - ADAM_B1) * g
    v = ADAM_B2 * v + (1.0 - ADAM_B2) * _jnp.square(g)
    m_hat = m / (1.0 - ADAM_B1 ** ADAM_STEP)
    v_hat = v / (1.0 - ADAM_B2 ** ADAM_STEP)
    delta = -ADAM_LR * (m_hat / (_jnp.sqrt(v_hat) + ADAM_EPS) + ADAM_WD * w)
    return delta, m, v


def reference(x, meta_tokens, norm_mix_pre, w_in, q_a_norm, w_q_up, kv_a_norm, w_kv_up, conv_w, conv_b, dt_bias, a_log, d_skip, ssm_norm, w_out, norm_mix_post, norm_mlp_pre, w_mlp_up, w_mlp_down, norm_mlp_post, loss_target, m_meta_tokens, m_norm_mix_pre, m_w_in, m_q_a_norm, m_w_q_up, m_kv_a_norm, m_w_kv_up, m_conv_w, m_conv_b, m_dt_bias, m_a_log, m_d_skip, m_ssm_norm, m_w_out, m_norm_mix_post, m_norm_mlp_pre, m_w_mlp_up, m_w_mlp_down, m_norm_mlp_post, v_meta_tokens, v_norm_mix_pre, v_w_in, v_q_a_norm, v_w_q_up, v_kv_a_norm, v_w_kv_up, v_conv_w, v_conv_b, v_dt_bias, v_a_log, v_d_skip, v_ssm_norm, v_w_out, v_norm_mix_post, v_norm_mlp_pre, v_w_mlp_up, v_w_mlp_down, v_norm_mlp_post):
    given = dict(x=x, meta_tokens=meta_tokens, norm_mix_pre=norm_mix_pre, w_in=w_in, q_a_norm=q_a_norm, w_q_up=w_q_up, kv_a_norm=kv_a_norm, w_kv_up=w_kv_up, conv_w=conv_w, conv_b=conv_b, dt_bias=dt_bias, a_log=a_log, d_skip=d_skip, ssm_norm=ssm_norm, w_out=w_out, norm_mix_post=norm_mix_post, norm_mlp_pre=norm_mlp_pre, w_mlp_up=w_mlp_up, w_mlp_down=w_mlp_down, norm_mlp_post=norm_mlp_post, loss_target=loss_target, m_meta_tokens=m_meta_tokens, m_norm_mix_pre=m_norm_mix_pre, m_w_in=m_w_in, m_q_a_norm=m_q_a_norm, m_w_q_up=m_w_q_up, m_kv_a_norm=m_kv_a_norm, m_w_kv_up=m_w_kv_up, m_conv_w=m_conv_w, m_conv_b=m_conv_b, m_dt_bias=m_dt_bias, m_a_log=m_a_log, m_d_skip=m_d_skip, m_ssm_norm=m_ssm_norm, m_w_out=m_w_out, m_norm_mix_post=m_norm_mix_post, m_norm_mlp_pre=m_norm_mlp_pre, m_w_mlp_up=m_w_mlp_up, m_w_mlp_down=m_w_mlp_down, m_norm_mlp_post=m_norm_mlp_post, v_meta_tokens=v_meta_tokens, v_norm_mix_pre=v_norm_mix_pre, v_w_in=v_w_in, v_q_a_norm=v_q_a_norm, v_w_q_up=v_w_q_up, v_kv_a_norm=v_kv_a_norm, v_w_kv_up=v_w_kv_up, v_conv_w=v_conv_w, v_conv_b=v_conv_b, v_dt_bias=v_dt_bias, v_a_log=v_a_log, v_d_skip=v_d_skip, v_ssm_norm=v_ssm_norm, v_w_out=v_w_out, v_norm_mix_post=v_norm_mix_post, v_norm_mlp_pre=v_norm_mlp_pre, v_w_mlp_up=v_w_mlp_up, v_w_mlp_down=v_w_mlp_down, v_norm_mlp_post=v_norm_mlp_post)
    weights = {n: given[n] for n in TWIN_WEIGHTS}
    shared = {n: given[n] for n in SHARED_INPUTS}
    per_example = {n: given[n] for n in ['x']}
    grad_fn = _jax.value_and_grad(_loss, argnums=(0, 1))

    def one_microbatch(ex, loss_target):
        ex = dict(ex)
        diff = ex.pop(TWIN_DIFF_INPUT)
        return grad_fn(weights, diff, {**shared, **ex}, loss_target)

    if N_MICROBATCH == 1:
        loss, (grad_w, grad_x) = one_microbatch(per_example, given["loss_target"])
    else:
        def body(carry, xs):
            loss_sum, grad_sum = carry
            l_k, (gw_k, gx_k) = one_microbatch(xs[0], xs[1])
            with _jax.named_scope("update"):
                return (loss_sum + l_k, _jax.tree.map(_jnp.add, grad_sum, gw_k)), gx_k

        init = (_jnp.zeros((), _jnp.float32), _jax.tree.map(_jnp.zeros_like, weights))
        (loss, grad_w), grad_x = _jax.lax.scan(body, init, (per_example, given["loss_target"]))
    with _jax.named_scope("update"):
        delta_w, new_m, new_v = {}, {}, {}
        for n in TWIN_WEIGHTS:
            delta_w[n], new_m[n], new_v[n] = _adamw(weights[n], grad_w[n], given["m_" + n], given["v_" + n])
    return (loss, grad_x, *[grad_w[n] for n in TWIN_WEIGHTS], *[delta_w[n] for n in TWIN_WEIGHTS],
            *[new_m[n] for n in TWIN_WEIGHTS], *[new_v[n] for n in TWIN_WEIGHTS])
```

```python
import functools
import math

import jax
import jax.numpy as jnp
import numpy as np
from jax import lax
from jax.experimental import pallas as pl
from jax.experimental.pallas import tpu as pltpu

F32 = jnp.float32
BF16 = jnp.bfloat16

D_MODEL = 1024
N_META = 16
EPS = 1e-6
ATT_HEADS = 8
Q_LORA = 384
KV_LORA = 256
QK_NOPE = 128
QK_ROPE = 64
V_HEAD = 128
ROPE_THETA = 10000.0
SSM_HEADS = 16
SSM_HEAD_DIM = 64
SSM_WIDTH = 1024
SSM_GROUPS = 2
SSM_STATE = 128
CONV_K = 4
CHUNK = 128
CONV_DIM = 1536
D_FF = 4096
IN_SPLITS = (Q_LORA, KV_LORA, QK_ROPE, SSM_WIDTH, CONV_DIM, SSM_HEADS)
IN_WIDTH = sum(IN_SPLITS)
ADAM_LR, ADAM_B1, ADAM_B2, ADAM_EPS, ADAM_WD, ADAM_STEP = 0.001, 0.9, 0.999, 1e-08, 0.01, 10

LANE = 128
PADF = CHUNK - N_META
ROWB = 512
HEADW = 256
PC_Q, PC_KV, PC_KR, PC_Z, PC_XBC, PC_DT, PROJ_W = 0, 384, 640, 896, 1920, 3456, 3584
NEG = -1e30
N_DEV = 8
VMEM_LIMIT = 56 * 1024 * 1024
MESH = pl.DeviceIdType.MESH


def _cp(sem, vmem=VMEM_LIMIT, **kw):
    return pltpu.CompilerParams(dimension_semantics=sem, vmem_limit_bytes=vmem, **kw)


def _dot(a, b, dims=((1,), (0,)), prec=None):
    return lax.dot_general(a, b, (dims, ((), ())), preferred_element_type=F32, precision=prec)


def _bdot(a, b, dims=((1,), (0,))):
    return _dot(a.astype(BF16), b.astype(BF16), dims)


NT = ((1,), (1,))
HI = lax.Precision.HIGHEST


def _rms_fwd(x, w):
    r = lax.rsqrt(jnp.mean(x * x, axis=-1, keepdims=True) + EPS)
    return (x * r) * w


def _rms_bwd(x, w, dy):
    r = lax.rsqrt(jnp.mean(x * x, axis=-1, keepdims=True) + EPS)
    xh = x * r
    g = dy * w
    dx = r * (g - xh * jnp.mean(g * xh, axis=-1, keepdims=True))
    return dx, dy * xh


def _sigmoid(x):
    return 1.0 / (1.0 + jnp.exp(-x))


def _colsum8(x):
    t, c = x.shape
    return jnp.sum(x.reshape(t // 8, 8, c), axis=0)


def _rowspec(t, c, cb=0):
    return pl.BlockSpec((t, c), lambda i: (i, cb))


def _fullspec(shape):
    n = len(shape)
    return pl.BlockSpec(shape, lambda i: (0,) * n)


def _sds(shape, dt):
    return jax.ShapeDtypeStruct(shape, dt)


def _acc(ref, val):
    @pl.when(pl.program_id(0) == 0)
    def _():
        ref[...] = jnp.zeros_like(ref)

    ref[...] += val


def _norm_in_proj(h, g, w_all):
    lp = h.shape[0]

    def body(h_ref, g_ref, w_ref, hn_ref, cq_ref, ckv_ref, kr_ref, z_ref, xbc_ref, dt_ref):
        hn = _rms_fwd(h_ref[...], g_ref[...]).astype(BF16)
        hn_ref[...] = hn
        p = _dot(hn, w_ref[...])
        cq_ref[...] = p[:, PC_Q:PC_KV]
        ckv_ref[...] = p[:, PC_KV:PC_KR]
        kr_ref[...] = p[:, PC_KR:PC_Z]
        z_ref[...] = p[:, PC_Z:PC_XBC]
        xbc_ref[...] = p[:, PC_XBC:PC_DT]
        dt_ref[...] = p[:, PC_DT:PROJ_W]

    widths = (Q_LORA, KV_LORA, HEADW, SSM_WIDTH, CONV_DIM, LANE)
    return pl.pallas_call(
        body, name="norm_in_proj", grid=(lp // ROWB,),
        in_specs=[_rowspec(ROWB, D_MODEL), _fullspec((1, D_MODEL)), _fullspec((D_MODEL, PROJ_W))],
        out_specs=[_rowspec(ROWB, D_MODEL)] + [_rowspec(ROWB, w) for w in widths],
        out_shape=[_sds((lp, D_MODEL), BF16)] + [_sds((lp, w), F32) for w in widths],
        compiler_params=_cp(("arbitrary",)),
    )(h, g, w_all)


def _rope(x, cos, sa, sb):
    w = x.shape[1]
    return x * cos + pltpu.roll(x, w - 32, 1) * sa + pltpu.roll(x, 32, 1) * sb


def _rope_t(dy, cos, sa, sb):
    w = dy.shape[1]
    return dy * cos + pltpu.roll(dy * sa, 32, 1) + pltpu.roll(dy * sb, w - 32, 1)


def _tile8(t):
    return jnp.concatenate([t] * ATT_HEADS, axis=1)


def _qkv(cq, ckv, kr, cos, sa, sb, gq, gkv, wq, wk, wv):
    lp = cq.shape[0]
    qw = ATT_HEADS * HEADW

    def body(cq_ref, ckv_ref, kr_ref, cos_ref, sa_ref, sb_ref, gq_ref, gkv_ref, wq_ref, wk_ref, wv_ref,
             q_ref, k_ref, v_ref, cqn_ref, ckvn_ref):
        cos_, sa_, sb_ = cos_ref[...], sa_ref[...], sb_ref[...]
        cqn = _rms_fwd(cq_ref[...], gq_ref[...]).astype(BF16)
        ckvn = _rms_fwd(ckv_ref[...], gkv_ref[...]).astype(BF16)
        cqn_ref[...] = cqn
        ckvn_ref[...] = ckvn
        q = _dot(cqn, wq_ref[...])
        q_ref[...] = _rope(q, _tile8(cos_), _tile8(sa_), _tile8(sb_)).astype(BF16)
        k = _dot(ckvn, wk_ref[...]) + _tile8(_rope(kr_ref[...], cos_, sa_, sb_))
        k_ref[...] = k.astype(BF16)
        v_ref[...] = _dot(ckvn, wv_ref[...]).astype(BF16)

    return pl.pallas_call(
        body, name="qkv", grid=(lp // ROWB,),
        in_specs=[_rowspec(ROWB, Q_LORA), _rowspec(ROWB, KV_LORA), _rowspec(ROWB, HEADW)]
        + [_rowspec(ROWB, HEADW)] * 3
        + [_fullspec((1, Q_LORA)), _fullspec((1, KV_LORA)), _fullspec((Q_LORA, qw)), _fullspec((KV_LORA, qw)),
           _fullspec((KV_LORA, ATT_HEADS * V_HEAD))],
        out_specs=[_rowspec(ROWB, qw), _rowspec(ROWB, qw), _rowspec(ROWB, ATT_HEADS * V_HEAD),
                   _rowspec(ROWB, Q_LORA), _rowspec(ROWB, KV_LORA)],
        out_shape=[_sds((lp, qw), BF16), _sds((lp, qw), BF16), _sds((lp, ATT_HEADS * V_HEAD), BF16),
                   _sds((lp, Q_LORA), BF16), _sds((lp, KV_LORA), BF16)],
        compiler_params=_cp(("arbitrary",)),
    )(cq, ckv, kr, cos, sa, sb, gq, gkv, wq, wk, wv)


ATT_SCALE = (QK_NOPE + QK_ROPE) ** -0.5


def _att_ok(qrow, krow):
    return (krow <= qrow) & ((krow >= PADF) | (qrow < PADF))


def _flash_fwd(q, k, v):
    lp = q.shape[0]
    nq = lp // ROWB

    def body(q_ref, k_ref, v_ref, o_ref, lse_ref, acc, m_s, l_s):
        i = pl.program_id(1)
        qb = q_ref[...]
        m_s[...] = jnp.full_like(m_s, NEG)
        l_s[...] = jnp.zeros_like(l_s)
        acc[...] = jnp.zeros_like(acc)

        def tile(j, masked):
            off = pl.multiple_of(j * ROWB, ROWB)
            kb = k_ref[pl.ds(off, ROWB), :]
            vb = v_ref[pl.ds(off, ROWB), :]
            s = _dot(qb, kb, NT) * ATT_SCALE
            if masked:
                qrow = i * ROWB + lax.broadcasted_iota(jnp.int32, s.shape, 0)
                krow = j * ROWB + lax.broadcasted_iota(jnp.int32, s.shape, 1)
                s = jnp.where(_att_ok(qrow, krow), s, NEG)
            m_prev = m_s[...]
            m_new = jnp.maximum(m_prev, jnp.max(s, axis=1, keepdims=True))
            alpha = jnp.exp(m_prev - m_new)
            p = jnp.exp(s - m_new)
            l_s[...] = alpha * l_s[...] + jnp.sum(p, axis=1, keepdims=True)
            acc[...] = alpha * acc[...] + _dot(p.astype(BF16), vb)
            m_s[...] = m_new

        tile(0, True)

        def loop(j, c):
            tile(j, False)
            return c

        lax.fori_loop(1, i, loop, 0)

        @pl.when(i > 0)
        def _():
            tile(i, True)

        l = l_s[...]
        o_ref[...] = (acc[...] / l).astype(BF16)
        lse_ref[0] = jnp.broadcast_to(m_s[...] + jnp.log(l), (ROWB, LANE))

    return pl.pallas_call(
        body, name="flash_fwd", grid=(ATT_HEADS, nq),
        in_specs=[pl.BlockSpec((ROWB, HEADW), lambda h, i: (i, h)),
                  pl.BlockSpec((lp, HEADW), lambda h, i: (0, h)),
                  pl.BlockSpec((lp, V_HEAD), lambda h, i: (0, h))],
        out_specs=[pl.BlockSpec((ROWB, V_HEAD), lambda h, i: (i, h)),
                   pl.BlockSpec((1, ROWB, LANE), lambda h, i: (h, i, 0))],
        out_shape=[_sds((lp, ATT_HEADS * V_HEAD), BF16), _sds((ATT_HEADS, lp, LANE), F32)],
        scratch_shapes=[pltpu.VMEM((ROWB, V_HEAD), F32), pltpu.VMEM((ROWB, 1), F32), pltpu.VMEM((ROWB, 1), F32)],
        compiler_params=_cp(("arbitrary", "arbitrary")),
    )(q, k, v)


def _silu(x):
    return x * _sigmoid(x)


def _conv_fwd(xbc, cw, cb):
    lp, c = xbc.shape
    t8 = ROWB // 8

    def body(x_ref, prev_ref, w_ref, b_ref, o_ref, buf):
        i = pl.program_id(0)
        buf[pl.ds(0, 8), :] = jnp.where(i > 0, prev_ref[...], 0.0)
        buf[pl.ds(8, ROWB), :] = x_ref[...]
        w = w_ref[...]
        pre = b_ref[...] + sum(w[kk:kk + 1, :] * buf[pl.ds(8 - (CONV_K - 1) + kk, ROWB), :] for kk in range(CONV_K))
        o_ref[...] = _silu(pre)

    return pl.pallas_call(
        body, name="conv_fwd", grid=(lp // ROWB,),
        in_specs=[_rowspec(ROWB, c), pl.BlockSpec((8, c), lambda i: (jnp.maximum(i * t8 - 1, 0), 0)),
                  _fullspec((8, c)), _fullspec((1, c))],
        out_specs=_rowspec(ROWB, c), out_shape=_sds((lp, c), F32),
        scratch_shapes=[pltpu.VMEM((ROWB + 8, c), F32)],
        compiler_params=_cp(("arbitrary",)),
    )(xbc, xbc, cw, cb)


def _expand_mat():
    r = np.arange(LANE)[:, None]
    c = np.arange(SSM_WIDTH)[None, :]
    return jnp.asarray((c // SSM_HEAD_DIM == r).astype(np.float32))


def _tri_mat():
    i = np.arange(CHUNK)
    return jnp.asarray((i[:, None] >= i[None, :]).astype(np.float32))


def _ssd_prep(dtr_ref, bias_ref, alog_ref, tri, c, seq_rows):
    rows = c * CHUNK + lax.broadcasted_iota(jnp.int32, (CHUNK, LANE), 0)
    lanes = lax.broadcasted_iota(jnp.int32, (CHUNK, LANE), 1)
    valid = (rows >= PADF) & (rows < PADF + seq_rows) & (lanes < SSM_HEADS)
    dtr = dtr_ref[...] + bias_ref[...]
    sp = jnp.maximum(dtr, 0.0) + jnp.log(1.0 + jnp.exp(-jnp.abs(dtr)))
    dt = jnp.where(valid, sp, 0.0)
    a = -jnp.exp(alog_ref[...])
    acol = _dot(tri, dt * a, prec=HI)
    return dt, a, acol, valid, dtr


def _ssd_fwd(xbc_act, dtr, dt_bias, a_log, seq_rows):
    lp = xbc_act.shape[0]
    nc = lp // CHUNK
    gw = SSM_WIDTH // SSM_GROUPS
    hpg = SSM_HEADS // SSM_GROUPS

    def body(x_ref, b_ref, c_ref, dtr_ref, bias_ref, alog_ref, tri_ref, ex_ref, y_ref, hp_ref, h_s):
        c = pl.program_id(0)

        @pl.when(c == 0)
        def _():
            h_s[...] = jnp.zeros_like(h_s)

        tri = tri_ref[...]
        ex = ex_ref[...]
        dt, a, acol, _, _ = _ssd_prep(dtr_ref, bias_ref, alog_ref, tri, c, seq_rows)
        arow = acol.T
        dtrow = dt.T
        alast = acol[CHUNK - 1:CHUNK, :]
        e_all = _dot(jnp.exp(acol), ex, prec=HI)
        wx_all = _dot(jnp.exp(alast - acol) * dt, ex, prec=HI)
        dec_all = _dot(jnp.broadcast_to(jnp.exp(alast), (8, LANE)), ex, prec=HI)[0:1, :]
        causal = tri > 0.5
        hp_ref[0] = h_s[...]
        for g in range(SSM_GROUPS):
            gs = slice(g * gw, (g + 1) * gw)
            bg = b_ref[:, g * SSM_STATE:(g + 1) * SSM_STATE]
            cg = c_ref[:, g * SSM_STATE:(g + 1) * SSM_STATE].astype(BF16)
            xg = x_ref[:, gs]
            hg = h_s[:, gs]
            gm = _bdot(cg, bg, NT)
            y_off = _bdot(cg, hg) * e_all[:, gs]
            for r in range(hpg):
                hd = g * hpg + r
                seg = acol[:, hd:hd + 1] - arow[hd:hd + 1, :]
                lm = jnp.where(causal, jnp.exp(jnp.where(causal, seg, 0.0)), 0.0)
                w = gm * lm * dtrow[hd:hd + 1, :]
                cs = slice(r * SSM_HEAD_DIM, (r + 1) * SSM_HEAD_DIM)
                y_ref[:, pl.ds(hd * SSM_HEAD_DIM, SSM_HEAD_DIM)] = _bdot(w, xg[:, cs]) + y_off[:, cs]
            st = _bdot(bg.T, xg * wx_all[:, gs])
            h_s[:, gs] = hg * dec_all[:, gs] + st

    xs_spec = pl.BlockSpec((CHUNK, SSM_WIDTH), lambda c: (c, 0))
    b_spec = pl.BlockSpec((CHUNK, 2 * SSM_STATE), lambda c: (c, SSM_WIDTH // (2 * SSM_STATE)))
    c_spec = pl.BlockSpec((CHUNK, 2 * SSM_STATE), lambda c: (c, SSM_WIDTH // (2 * SSM_STATE) + 1))
    return pl.pallas_call(
        body, name="ssd_fwd", grid=(nc,),
        in_specs=[xs_spec, b_spec, c_spec, pl.BlockSpec((CHUNK, LANE), lambda c: (c, 0)),
                  _fullspec((1, LANE)), _fullspec((1, LANE)), _fullspec((CHUNK, CHUNK)), _fullspec((LANE, SSM_WIDTH))],
        out_specs=[xs_spec, pl.BlockSpec((1, SSM_STATE, SSM_WIDTH), lambda c: (c, 0, 0))],
        out_shape=[_sds((lp, SSM_WIDTH), F32), _sds((nc, SSM_STATE, SSM_WIDTH), F32)],
        scratch_shapes=[pltpu.VMEM((SSM_STATE, SSM_WIDTH), F32)],
        compiler_params=_cp(("arbitrary",)),
    )(xbc_act, xbc_act, xbc_act, dtr, dt_bias, a_log, _tri_mat(), _expand_mat())


def _group_mean(x):
    gw = SSM_WIDTH // SSM_GROUPS
    parts = [jnp.broadcast_to(jnp.mean(x[:, g * gw:(g + 1) * gw], axis=-1, keepdims=True), (x.shape[0], gw))
             for g in range(SSM_GROUPS)]
    return jnp.concatenate(parts, axis=1)


def _ssd_post(y, xbc_act, z, dskip, gnorm):
    lp = y.shape[0]

    def body(y_ref, x_ref, z_ref, d_ref, g_ref, o_ref):
        z_ = z_ref[...]
        gt = (y_ref[...] + d_ref[...] * x_ref[...]) * _silu(z_)
        r = lax.rsqrt(_group_mean(gt * gt) + EPS)
        o_ref[...] = ((gt * r) * g_ref[...]).astype(BF16)

    return pl.pallas_call(
        body, name="ssd_post", grid=(lp // ROWB,),
        in_specs=[_rowspec(ROWB, SSM_WIDTH)] * 3 + [_fullspec((1, SSM_WIDTH))] * 2,
        out_specs=_rowspec(ROWB, SSM_WIDTH), out_shape=_sds((lp, SSM_WIDTH), BF16),
        compiler_params=_cp(("arbitrary",)),
    )(y, xbc_act, z, dskip, gnorm)


def _out_proj(att, ssm, h, w_out, g_post):
    lp = h.shape[0]

    def body(a_ref, s_ref, h_ref, w_ref, g_ref, mix_ref, h1_ref):
        mix = _dot(a_ref[...], w_ref[pl.ds(0, 1024), :]) + _dot(s_ref[...], w_ref[pl.ds(1024, 1024), :])
        mix_ref[...] = mix
        h1_ref[...] = h_ref[...] + _rms_fwd(mix, g_ref[...])

    return pl.pallas_call(
        body, name="out_proj", grid=(lp // ROWB,),
        in_specs=[_rowspec(ROWB, 1024)] * 3 + [_fullspec((2048, D_MODEL)), _fullspec((1, D_MODEL))],
        out_specs=[_rowspec(ROWB, D_MODEL)] * 2, out_shape=[_sds((lp, D_MODEL), F32)] * 2,
        compiler_params=_cp(("arbitrary",)),
    )(att, ssm, h, w_out, g_post)


def _resident(w_hbm, w_vmem, sem):
    @pl.when(pl.program_id(0) == 0)
    def _():
        cp = pltpu.make_async_copy(w_hbm, w_vmem, sem)
        cp.start()
        cp.wait()


ANY = pl.BlockSpec(memory_space=pl.ANY)


def _mlp_fwd(h1, tgt, w_up, w_down, g_pre, g_post, seq_rows):
    lp = h1.shape[0]

    def body(h1_ref, t_ref, wu_hbm, wd_hbm, gpre_ref, gpost_ref, hn2_ref, f_ref, dh2_ref, loss_ref, wu, wd, sems):
        _resident(wu_hbm, wu, sems.at[0])
        _resident(wd_hbm, wd, sems.at[1])
        i = pl.program_id(0)
        h1_ = h1_ref[...]
        hn2 = _rms_fwd(h1_, gpre_ref[...]).astype(BF16)
        hn2_ref[...] = hn2
        u = jnp.maximum(_dot(hn2, wu[...]), 0.0)
        f = _dot((u * u).astype(BF16), wd[...])
        f_ref[...] = f
        h2 = h1_ + _rms_fwd(f, gpost_ref[...])
        rows = i * ROWB + lax.broadcasted_iota(jnp.int32, (ROWB, 1), 0)
        real = (rows >= PADF + N_META) & (rows < PADF + seq_rows)
        err = jnp.where(real, h2 - t_ref[...], 0.0)
        dh2_ref[...] = err * (1.0 / D_MODEL)
        _acc(loss_ref, _colsum8(err * err))

    return pl.pallas_call(
        body, name="mlp_fwd", grid=(lp // ROWB,),
        in_specs=[_rowspec(ROWB, D_MODEL)] * 2 + [ANY, ANY] + [_fullspec((1, D_MODEL))] * 2,
        out_specs=[_rowspec(ROWB, D_MODEL)] * 3 + [_fullspec((8, D_MODEL))],
        out_shape=[_sds((lp, D_MODEL), BF16), _sds((lp, D_MODEL), F32), _sds((lp, D_MODEL), F32), _sds((8, D_MODEL), F32)],
        scratch_shapes=[pltpu.VMEM((D_MODEL, D_FF), BF16), pltpu.VMEM((D_FF, D_MODEL), BF16), pltpu.SemaphoreType.DMA((2,))],
        compiler_params=_cp(("arbitrary",)),
    )(h1, tgt, w_up, w_down, g_pre, g_post)


def _pad_cols(w, width):
    return jnp.pad(w, ((0, 0), (0, width - w.shape[1])))


def _layout_weights(w_in, w_q_up, w_kv_up):
    o = np.cumsum((0,) + IN_SPLITS)
    pieces = [w_in[:, o[k]:o[k + 1]] for k in range(6)]
    kr = jnp.pad(pieces[2], ((0, 0), (QK_NOPE, HEADW - QK_NOPE - QK_ROPE)))
    w_all = jnp.concatenate([pieces[0], pieces[1], kr, pieces[3], pieces[4], _pad_cols(pieces[5], LANE)], axis=1)
    wq = jnp.pad(w_q_up.reshape(Q_LORA, ATT_HEADS, QK_NOPE + QK_ROPE), ((0, 0), (0, 0), (0, HEADW - QK_NOPE - QK_ROPE)))
    wkv = w_kv_up.reshape(KV_LORA, ATT_HEADS, QK_NOPE + V_HEAD)
    wk = jnp.pad(wkv[:, :, :QK_NOPE], ((0, 0), (0, 0), (0, HEADW - QK_NOPE)))
    wv = wkv[:, :, QK_NOPE:]
    return w_all, wq.reshape(Q_LORA, -1), wk.reshape(KV_LORA, -1), wv.reshape(KV_LORA, -1)


def _rope_tables(lp):
    pos = jnp.maximum(jnp.arange(lp, dtype=jnp.int32) - PADF, 0).astype(F32)
    inv_freq = ROPE_THETA ** (-jnp.arange(0, QK_ROPE, 2, dtype=F32) / QK_ROPE)
    ang = pos[:, None] * inv_freq[None, :]
    cos, sin = jnp.cos(ang), jnp.sin(ang)
    one, zero = jnp.ones((lp, QK_NOPE), F32), jnp.zeros((lp, QK_NOPE), F32)
    z32, z64 = jnp.zeros((lp, 32), F32), jnp.zeros((lp, 64), F32)
    cos_t = jnp.concatenate([one, cos, cos, jnp.ones((lp, 64), F32)], axis=1)
    sa = jnp.concatenate([zero, -sin, z32, z64], axis=1)
    sb = jnp.concatenate([zero, z32, sin, z64], axis=1)
    return cos_t, sa, sb


def _row1(v, width=None):
    v = v.reshape(1, -1).astype(F32)
    return v if width is None else _pad_cols(v, width)


def _local_forward(h, tgt, p, seq_rows):
    lp = h.shape[0]
    f = {"seq_rows": seq_rows}
    w_all, wq, wk, wv = _layout_weights(p["w_in"], p["w_q_up"], p["w_kv_up"])
    f.update(w_all=w_all, wq=wq, wk=wk, wv=wv)
    f["hn"], cq, ckv, kr, f["z"], f["xbc"], f["dtr"] = _norm_in_proj(h, _row1(p["norm_mix_pre"]), w_all)
    f.update(cq=cq, ckv=ckv)
    f["rope"] = _rope_tables(lp)
    f["q"], f["k"], f["v"], f["cqn"], f["ckvn"] = _qkv(cq, ckv, kr, *f["rope"], _row1(p["q_a_norm"]),
                                                   _row1(p["kv_a_norm"]), wq, wk, wv)
    f["att"], f["lse"] = _flash_fwd(f["q"], f["k"], f["v"])
    f["cw"] = jnp.pad(p["conv_w"].astype(F32), ((0, 8 - CONV_K), (0, 0)))
    f["xact"] = _conv_fwd(f["xbc"], f["cw"], _row1(p["conv_b"]))
    f["dt_bias"], f["a_log"] = _row1(p["dt_bias"], LANE), _row1(p["a_log"], LANE)
    f["y"], f["hprev"] = _ssd_fwd(f["xact"], f["dtr"], f["dt_bias"], f["a_log"], seq_rows)
    f["dskip"] = jnp.repeat(p["d_skip"].reshape(-1).astype(F32), SSM_HEAD_DIM).reshape(1, SSM_WIDTH)
    f["ssm"] = _ssd_post(f["y"], f["xact"], f["z"], f["dskip"], _row1(p["ssm_norm"]))
    f["mix"], f["h1"] = _out_proj(f["att"], f["ssm"], h, p["w_out"], _row1(p["norm_mix_post"]))
    f["hn2"], f["f"], f["dh2"], loss8 = _mlp_fwd(f["h1"], tgt, p["w_mlp_up"], p["w_mlp_down"],
                                                 _row1(p["norm_mlp_pre"]), _row1(p["norm_mlp_post"]), seq_rows)
    f["loss"] = 0.5 * jnp.sum(loss8) / D_MODEL
    return f


MLPB = 256


def _mlp_bwd(dh2, f, h1, hn2, w_up, w_down, g_pre, g_post):
    lp = h1.shape[0]

    def body(dh2_ref, f_ref, h1_ref, hn2_ref, wu_hbm, wd_hbm, gpre_ref, gpost_ref,
             dh1_ref, du_ref, a_ref, df_ref, dgpre_ref, dgpost_ref, wu, wd, sems):
        _resident(wu_hbm, wu, sems.at[0])
        _resident(wd_hbm, wd, sems.at[1])
        dh2_ = dh2_ref[...]
        df, dgp = _rms_bwd(f_ref[...], gpost_ref[...], dh2_)
        dfb = df.astype(BF16)
        df_ref[...] = dfb
        da = _dot(dfb, wd[...], NT)
        u = jnp.maximum(_dot(hn2_ref[...], wu[...]), 0.0)
        a_ref[...] = (u * u).astype(BF16)
        du = (da * (2.0 * u)).astype(BF16)
        du_ref[...] = du
        dhn2 = _dot(du, wu[...], NT)
        dx, dgq = _rms_bwd(h1_ref[...], gpre_ref[...], dhn2)
        dh1_ref[...] = dh2_ + dx
        _acc(dgpre_ref, _colsum8(dgq))
        _acc(dgpost_ref, _colsum8(dgp))

    return pl.pallas_call(
        body, name="mlp_bwd", grid=(lp // MLPB,),
        in_specs=[_rowspec(MLPB, D_MODEL)] * 4 + [ANY, ANY] + [_fullspec((1, D_MODEL))] * 2,
        out_specs=[_rowspec(MLPB, D_MODEL), _rowspec(MLPB, D_FF), _rowspec(MLPB, D_FF), _rowspec(MLPB, D_MODEL),
                   _fullspec((8, D_MODEL)), _fullspec((8, D_MODEL))],
        out_shape=[_sds((lp, D_MODEL), F32), _sds((lp, D_FF), BF16), _sds((lp, D_FF), BF16), _sds((lp, D_MODEL), BF16),
                   _sds((8, D_MODEL), F32), _sds((8, D_MODEL), F32)],
        scratch_shapes=[pltpu.VMEM((D_MODEL, D_FF), BF16), pltpu.VMEM((D_FF, D_MODEL), BF16), pltpu.SemaphoreType.DMA((2,))],
        compiler_params=_cp(("arbitrary",)),
    )(dh2, f, h1, hn2, w_up, w_down, g_pre, g_post)


def _out_bwd(dh1, mix, w_out, g_post):
    lp = dh1.shape[0]

    def body(dh1_ref, mix_ref, w_ref, g_ref, dmix_ref, datt_ref, dssm_ref, dg_ref):
        dmix, dg = _rms_bwd(mix_ref[...], g_ref[...], dh1_ref[...])
        dmb = dmix.astype(BF16)
        dmix_ref[...] = dmb
        datt_ref[...] = _dot(dmb, w_ref[pl.ds(0, 1024), :], NT).astype(BF16)
        dssm_ref[...] = _dot(dmb, w_ref[pl.ds(1024, 1024), :], NT)
        _acc(dg_ref, _colsum8(dg))

    return pl.pallas_call(
        body, name="out_bwd", grid=(lp // ROWB,),
        in_specs=[_rowspec(ROWB, D_MODEL)] * 2 + [_fullspec((2048, D_MODEL)), _fullspec((1, D_MODEL))],
        out_specs=[_rowspec(ROWB, D_MODEL)] * 3 + [_fullspec((8, D_MODEL))],
        out_shape=[_sds((lp, D_MODEL), BF16), _sds((lp, 1024), BF16), _sds((lp, 1024), F32), _sds((8, D_MODEL), F32)],
        compiler_params=_cp(("arbitrary",)),
    )(dh1, mix, w_out, g_post)


def _ssd_post_bwd(dssm, y, xact, z, dskip, gnorm):
    lp = y.shape[0]

    def body(do_ref, y_ref, x_ref, z_ref, d_ref, g_ref, dy_ref, dz_ref, dg_ref, dd_ref):
        z_, x_ = z_ref[...], x_ref[...]
        sg = _sigmoid(z_)
        sz = z_ * sg
        y2 = y_ref[...] + d_ref[...] * x_
        gt = y2 * sz
        r = lax.rsqrt(_group_mean(gt * gt) + EPS)
        gh = gt * r
        do = do_ref[...]
        dgh = do * g_ref[...]
        dgt = r * (dgh - gh * _group_mean(dgh * gh))
        dy2 = dgt * sz
        dy_ref[...] = dy2
        dz_ref[...] = (dgt * y2 * (sg * (1.0 + z_ * (1.0 - sg)))).astype(BF16)
        _acc(dg_ref, _colsum8(do * gh))
        _acc(dd_ref, _colsum8(dy2 * x_))

    return pl.pallas_call(
        body, name="ssd_post_bwd", grid=(lp // ROWB,),
        in_specs=[_rowspec(ROWB, SSM_WIDTH)] * 4 + [_fullspec((1, SSM_WIDTH))] * 2,
        out_specs=[_rowspec(ROWB, SSM_WIDTH)] * 2 + [_fullspec((8, SSM_WIDTH))] * 2,
        out_shape=[_sds((lp, SSM_WIDTH), F32), _sds((lp, SSM_WIDTH), BF16), _sds((8, SSM_WIDTH), F32), _sds((8, SSM_WIDTH), F32)],
        compiler_params=_cp(("arbitrary",)),
    )(dssm, y, xact, z, dskip, gnorm)


def _ssd_bwd(dy, xact, dtr, hprev, dt_bias, a_log, dskip, seq_rows):
    lp = xact.shape[0]
    nc = lp // CHUNK
    gw = SSM_WIDTH // SSM_GROUPS
    hpg = SSM_HEADS // SSM_GROUPS
    nb = SSM_WIDTH // (2 * SSM_STATE)

    def body(dy_ref, x_ref, b_ref, c_ref, dtr_ref, hp_ref, bias_ref, alog_ref, dsk_ref, tri_ref, ex_ref, ext_ref,
             dact_ref, ddtr_ref, da_ref, dbias_ref, dh_s):
        step = pl.program_id(0)
        c = nc - 1 - step

        @pl.when(step == 0)
        def _():
            dh_s[...] = jnp.zeros_like(dh_s)

        tri = tri_ref[...]
        ex = ex_ref[...]
        dt, a, acol, valid, dtr_ = _ssd_prep(dtr_ref, bias_ref, alog_ref, tri, c, seq_rows)
        arow = acol.T
        dtrow = dt.T
        alast = acol[CHUNK - 1:CHUNK, :]
        e_all = _dot(jnp.exp(acol), ex, prec=HI)
        wgt0 = jnp.exp(alast - acol)
        wgt = wgt0 * dt
        wx_all = _dot(wgt, ex, prec=HI)
        elast = jnp.exp(alast)
        dec_all = _dot(jnp.broadcast_to(elast, (8, LANE)), ex, prec=HI)[0:1, :]
        causal = tri > 0.5
        upper = tri.T > 0.5
        lane_id = lax.broadcasted_iota(jnp.int32, (1, LANE), 1)
        sub_id = lax.broadcasted_iota(jnp.int32, (CHUNK, 1), 0)
        dacol = jnp.zeros((CHUNK, LANE), F32)
        darowf = jnp.zeros((CHUNK, LANE), F32)
        ddtrowf = jnp.zeros((CHUNK, LANE), F32)
        dwgt = jnp.zeros((CHUNK, LANE), F32)
        delast = jnp.zeros((1, LANE), F32)
        for g in range(SSM_GROUPS):
            gs = slice(g * gw, (g + 1) * gw)
            ext_g = ext_ref[pl.ds(g * gw, gw), :]
            bg = b_ref[:, g * SSM_STATE:(g + 1) * SSM_STATE]
            cg = c_ref[:, g * SSM_STATE:(g + 1) * SSM_STATE]
            bgb, cgb = bg.astype(BF16), cg.astype(BF16)
            xg = x_ref[:, gs]
            dyg = dy_ref[:, gs]
            hg = hp_ref[0, :, gs]
            dhg = dh_s[:, gs]
            hgb, dhgb = hg.astype(BF16), dhg.astype(BF16)
            gm = _dot(cgb, bgb, NT)
            gmt = _dot(bgb, cgb, NT)
            y_off = _dot(cgb, hgb) * e_all[:, gs]
            dy0 = (dyg * e_all[:, gs]).astype(BF16)
            dcg = _dot(dy0, hgb, NT)
            dh_in = _dot(cg.T.astype(BF16), dy0) + dhg * dec_all[:, gs]
            dacol = dacol + _dot(dyg * y_off, ext_g, prec=HI)
            xw = xg * wx_all[:, gs]
            dxw = _dot(bgb, dhgb)
            dx_state = dxw * wx_all[:, gs]
            dwgt = dwgt + _dot(dxw * xg, ext_g, prec=HI)
            dbt = _dot(dhgb, xw.astype(BF16), NT)
            delast = delast + jnp.sum(_dot(_colsum8(dhg * hg), ext_g, prec=HI), axis=0, keepdims=True)
            dgm = jnp.zeros((CHUNK, CHUNK), F32)
            for r in range(hpg):
                hd = g * hpg + r
                cs = slice(r * SSM_HEAD_DIM, (r + 1) * SSM_HEAD_DIM)
                acol_r, arow_r = acol[:, hd:hd + 1], arow[hd:hd + 1, :]
                dtrow_r, dtcol_r = dtrow[hd:hd + 1, :], dt[:, hd:hd + 1]
                lm = jnp.where(causal, jnp.exp(jnp.where(causal, acol_r - arow_r, 0.0)), 0.0)
                lmt = jnp.where(upper, jnp.exp(jnp.where(upper, arow_r - acol_r, 0.0)), 0.0)
                wt = gmt * lmt * dtcol_r
                dy_r = dyg[:, cs].astype(BF16)
                dx_r = _dot(wt.astype(BF16), dy_r)
                dw = _dot(dy_r, xg[:, cs].astype(BF16), NT)
                t1 = dw * lm
                dgm = dgm + t1 * dtrow_r
                q1 = t1 * gm
                m = q1 * dtrow_r
                dacol = dacol + jnp.sum(m, axis=1, keepdims=True) * (lane_id == hd).astype(F32)
                darowf = darowf - (sub_id == hd).astype(F32) * jnp.sum(m, axis=0, keepdims=True)
                ddtrowf = ddtrowf + (sub_id == hd).astype(F32) * jnp.sum(q1, axis=0, keepdims=True)
                dact_ref[:, pl.ds(hd * SSM_HEAD_DIM, SSM_HEAD_DIM)] = (
                    dx_r + dx_state[:, cs] + dyg[:, cs] * dsk_ref[:, pl.ds(hd * SSM_HEAD_DIM, SSM_HEAD_DIM)])
            dgmb = dgm.astype(BF16)
            dact_ref[:, pl.ds(SSM_WIDTH + g * SSM_STATE, SSM_STATE)] = dbt.T + _dot(dgm.T.astype(BF16), cgb)
            dact_ref[:, pl.ds(SSM_WIDTH + 2 * SSM_STATE + g * SSM_STATE, SSM_STATE)] = dcg + _dot(dgmb, bgb)
            dh_s[:, gs] = dh_in
        t = dwgt * wgt
        dalast = jnp.sum(t, axis=0, keepdims=True) + delast * elast
        dacol_tot = dacol - t + darowf.T + (sub_id == CHUNK - 1).astype(F32) * dalast
        dda = _dot(tri.T, dacol_tot, prec=HI)
        ddt = dwgt * wgt0 + ddtrowf.T + dda * a
        ddtr = jnp.where(valid, ddt * _sigmoid(dtr_), 0.0)
        ddtr_ref[...] = ddtr
        _acc(da_ref, _colsum8(dda * dt) * a)
        _acc(dbias_ref, _colsum8(ddtr))

    rev = lambda c: nc - 1 - c
    xs_spec = pl.BlockSpec((CHUNK, SSM_WIDTH), lambda c: (rev(c), 0))
    return pl.pallas_call(
        body, name="ssd_bwd", grid=(nc,),
        in_specs=[xs_spec, xs_spec,
                  pl.BlockSpec((CHUNK, 2 * SSM_STATE), lambda c: (rev(c), nb)),
                  pl.BlockSpec((CHUNK, 2 * SSM_STATE), lambda c: (rev(c), nb + 1)),
                  pl.BlockSpec((CHUNK, LANE), lambda c: (rev(c), 0)),
                  pl.BlockSpec((1, SSM_STATE, SSM_WIDTH), lambda c: (rev(c), 0, 0)),
                  _fullspec((1, LANE)), _fullspec((1, LANE)), _fullspec((1, SSM_WIDTH)),
                  _fullspec((CHUNK, CHUNK)), _fullspec((LANE, SSM_WIDTH)), _fullspec((SSM_WIDTH, LANE))],
        out_specs=[pl.BlockSpec((CHUNK, CONV_DIM), lambda c: (rev(c), 0)), pl.BlockSpec((CHUNK, LANE), lambda c: (rev(c), 0)),
                   _fullspec((8, LANE)), _fullspec((8, LANE))],
        out_shape=[_sds((lp, CONV_DIM), F32), _sds((lp, LANE), F32), _sds((8, LANE), F32), _sds((8, LANE), F32)],
        scratch_shapes=[pltpu.VMEM((SSM_STATE, SSM_WIDTH), F32)],
        compiler_params=_cp(("arbitrary",)),
    )(dy, xact, xact, xact, dtr, hprev, dt_bias, a_log, dskip, _tri_mat(), _expand_mat(), _expand_mat().T)


def _conv_bwd(dact, xbc, cw, cb):
    lp, c = xbc.shape
    t8 = ROWB // 8
    nb = lp // ROWB
    ext = ROWB + 8

    def body(d_ref, dnext_ref, x_ref, prev_ref, next_ref, w_ref, b_ref, dx_ref, dw_ref, db_ref, xb, dp):
        i = pl.program_id(0)
        last = i == nb - 1
        xb[pl.ds(0, 8), :] = jnp.where(i > 0, prev_ref[...], 0.0)
        xb[pl.ds(8, ROWB), :] = x_ref[...]
        xb[pl.ds(8 + ROWB, 8), :] = jnp.where(last, 0.0, next_ref[...])
        w = w_ref[...]
        pre = b_ref[...] + sum(w[kk:kk + 1, :] * xb[pl.ds(8 - (CONV_K - 1) + kk, ext), :] for kk in range(CONV_K))
        sg = _sigmoid(pre)
        dsilu = sg * (1.0 + pre * (1.0 - sg))
        dp[pl.ds(0, ROWB), :] = d_ref[...] * dsilu[0:ROWB]
        dp[pl.ds(ROWB, 8), :] = jnp.where(last, 0.0, dnext_ref[...]) * dsilu[ROWB:ext]
        dx_ref[...] = sum(w[kk:kk + 1, :] * dp[pl.ds(CONV_K - 1 - kk, ROWB), :] for kk in range(CONV_K)).astype(BF16)
        dpre = dp[pl.ds(0, ROWB), :]
        sub = lax.broadcasted_iota(jnp.int32, (8, 1), 0)
        dwv = jnp.zeros((8, c), F32)
        for kk in range(CONV_K):
            part = jnp.sum(_colsum8(dpre * xb[pl.ds(8 - (CONV_K - 1) + kk, ROWB), :]), axis=0, keepdims=True)
            dwv = dwv + jnp.where(sub == kk, part, 0.0)
        _acc(dw_ref, dwv)
        _acc(db_ref, _colsum8(dpre))

    nxt = lambda i: (jnp.minimum((i + 1) * t8, lp // 8 - 1), 0)
    prv = lambda i: (jnp.maximum(i * t8 - 1, 0), 0)
    return pl.pallas_call(
        body, name="conv_bwd", grid=(nb,),
        in_specs=[_rowspec(ROWB, c), pl.BlockSpec((8, c), nxt), _rowspec(ROWB, c), pl.BlockSpec((8, c), prv),
                  pl.BlockSpec((8, c), nxt), _fullspec((8, c)), _fullspec((1, c))],
        out_specs=[_rowspec(ROWB, c), _fullspec((8, c)), _fullspec((8, c))],
        out_shape=[_sds((lp, c), BF16), _sds((8, c), F32), _sds((8, c), F32)],
        scratch_shapes=[pltpu.VMEM((ROWB + 16, c), F32), pltpu.VMEM((ROWB + 8, c), F32)],
        compiler_params=_cp(("arbitrary",)),
    )(dact, dact, xbc, xbc, xbc, cw, cb)


def _attn_delta(datt, att):
    lp = att.shape[0]

    def body(do_ref, o_ref, d_ref):
        d = jnp.sum(do_ref[...].astype(F32) * o_ref[...].astype(F32), axis=1, keepdims=True)
        d_ref[0] = jnp.broadcast_to(d, (ROWB, LANE))

    blk = pl.BlockSpec((ROWB, V_HEAD), lambda h, i: (i, h))
    return pl.pallas_call(
        body, name="attn_delta", grid=(ATT_HEADS, lp // ROWB), in_specs=[blk, blk],
        out_specs=pl.BlockSpec((1, ROWB, LANE), lambda h, i: (h, i, 0)),
        out_shape=_sds((ATT_HEADS, lp, LANE), F32), compiler_params=_cp(("arbitrary", "arbitrary")),
    )(datt, att)


def _flash_dq(q, k, v, datt, lse, delta):
    lp = q.shape[0]
    nq = lp // ROWB

    def body(q_ref, do_ref, lse_ref, dl_ref, k_ref, v_ref, dq_ref, acc):
        i = pl.program_id(1)
        qb, dob = q_ref[...], do_ref[...]
        lse_c, dl_c = lse_ref[0][:, 0:1], dl_ref[0][:, 0:1]
        acc[...] = jnp.zeros_like(acc)

        def tile(j, masked):
            off = pl.multiple_of(j * ROWB, ROWB)
            kb = k_ref[pl.ds(off, ROWB), :]
            vb = v_ref[pl.ds(off, ROWB), :]
            s = _dot(qb, kb, NT) * ATT_SCALE
            if masked:
                qrow = i * ROWB + lax.broadcasted_iota(jnp.int32, s.shape, 0)
                krow = j * ROWB + lax.broadcasted_iota(jnp.int32, s.shape, 1)
                s = jnp.where(_att_ok(qrow, krow), s, NEG)
            p = jnp.exp(s - lse_c)
            dp = _dot(dob, vb, NT)
            ds = p * (dp - dl_c) * ATT_SCALE
            acc[...] += _dot(ds.astype(BF16), kb)

        tile(0, True)

        def loop(j, c):
            tile(j, False)
            return c

        lax.fori_loop(1, i, loop, 0)

        @pl.when(i > 0)
        def _():
            tile(i, True)

        dq_ref[...] = acc[...]

    stat = pl.BlockSpec((1, ROWB, LANE), lambda h, i: (h, i, 0))
    return pl.pallas_call(
        body, name="flash_dq", grid=(ATT_HEADS, nq),
        in_specs=[pl.BlockSpec((ROWB, HEADW), lambda h, i: (i, h)), pl.BlockSpec((ROWB, V_HEAD), lambda h, i: (i, h)),
                  stat, stat,
                  pl.BlockSpec((lp, HEADW), lambda h, i: (0, h)), pl.BlockSpec((lp, V_HEAD), lambda h, i: (0, h))],
        out_specs=pl.BlockSpec((ROWB, HEADW), lambda h, i: (i, h)),
        out_shape=_sds((lp, ATT_HEADS * HEADW), F32),
        scratch_shapes=[pltpu.VMEM((ROWB, HEADW), F32)],
        compiler_params=_cp(("arbitrary", "arbitrary")),
    )(q, datt, lse, delta, k, v)


def _flash_dkv(q, k, v, datt, lse_row, delta_row):
    lp = q.shape[0]
    nk = lp // ROWB

    def body(k_ref, v_ref, q_ref, do_ref, lse_ref, dl_ref, dk_ref, dv_ref, dk_acc, dv_acc):
        j = pl.program_id(1)
        kb, vb = k_ref[...], v_ref[...]
        dk_acc[...] = jnp.zeros_like(dk_acc)
        dv_acc[...] = jnp.zeros_like(dv_acc)

        def tile(i, masked):
            off = pl.multiple_of(i * ROWB, ROWB)
            qb = q_ref[pl.ds(off, ROWB), :]
            dob = do_ref[pl.ds(off, ROWB), :]
            lse_r = lse_ref[0, :, pl.ds(off, ROWB)]
            dl_r = dl_ref[0, :, pl.ds(off, ROWB)]
            st = _dot(kb, qb, NT) * ATT_SCALE
            if masked:
                krow = j * ROWB + lax.broadcasted_iota(jnp.int32, st.shape, 0)
                qrow = i * ROWB + lax.broadcasted_iota(jnp.int32, st.shape, 1)
                st = jnp.where(_att_ok(qrow, krow), st, NEG)
            pt = jnp.exp(st - lse_r)
            dv_acc[...] += _dot(pt.astype(BF16), dob)
            dpt = _dot(vb, dob, NT)
            dst = pt * (dpt - dl_r) * ATT_SCALE
            dk_acc[...] += _dot(dst.astype(BF16), qb)

        tile(j, True)

        def loop_m(i, c):
            tile(i, True)
            return c

        def loop_u(i, c):
            tile(i, False)
            return c

        @pl.when(j == 0)
        def _():
            lax.fori_loop(1, nk, loop_m, 0)

        @pl.when(j > 0)
        def _():
            lax.fori_loop(j + 1, nk, loop_u, 0)

        dk_ref[...] = dk_acc[...]
        dv_ref[...] = dv_acc[...].astype(BF16)

    stat = pl.BlockSpec((1, 1, lp), lambda h, j: (h, 0, 0))
    return pl.pallas_call(
        body, name="flash_dkv", grid=(ATT_HEADS, nk),
        in_specs=[pl.BlockSpec((ROWB, HEADW), lambda h, j: (j, h)), pl.BlockSpec((ROWB, V_HEAD), lambda h, j: (j, h)),
                  pl.BlockSpec((lp, HEADW), lambda h, j: (0, h)), pl.BlockSpec((lp, V_HEAD), lambda h, j: (0, h)),
                  stat, stat],
        out_specs=[pl.BlockSpec((ROWB, HEADW), lambda h, j: (j, h)), pl.BlockSpec((ROWB, V_HEAD), lambda h, j: (j, h))],
        out_shape=[_sds((lp, ATT_HEADS * HEADW), F32), _sds((lp, ATT_HEADS * V_HEAD), BF16)],
        scratch_shapes=[pltpu.VMEM((ROWB, HEADW), F32), pltpu.VMEM((ROWB, V_HEAD), F32)],
        compiler_params=_cp(("arbitrary", "arbitrary")),
    )(k, v, q, datt, lse_row, delta_row)


def _qkv_bwd(dq, dk, dv, cq, ckv, cos, sa, sb, gq, gkv, wq, wk, wv):
    lp = cq.shape[0]
    qw = ATT_HEADS * HEADW

    def body(dq_ref, dk_ref, dv_ref, cq_ref, ckv_ref, cos_ref, sa_ref, sb_ref, gq_ref, gkv_ref, wq_ref, wk_ref, wv_ref,
             dcq_ref, dckv_ref, dkr_ref, dqp_ref, dkb_ref, dgq_ref, dgkv_ref):
        cos_, sa_, sb_ = cos_ref[...], sa_ref[...], sb_ref[...]
        dqp = _rope_t(dq_ref[...], _tile8(cos_), _tile8(sa_), _tile8(sb_)).astype(BF16)
        dqp_ref[...] = dqp
        dcq, dgq = _rms_bwd(cq_ref[...], gq_ref[...], _dot(dqp, wq_ref[...], NT))
        dcq_ref[...] = dcq.astype(BF16)
        dk_ = dk_ref[...]
        dkb = dk_.astype(BF16)
        dkb_ref[...] = dkb
        dksum = sum(dk_[:, hh * HEADW:(hh + 1) * HEADW] for hh in range(ATT_HEADS))
        dkr_ref[...] = _rope_t(dksum, cos_, sa_, sb_).astype(BF16)
        dckvn = _dot(dkb, wk_ref[...], NT) + _dot(dv_ref[...], wv_ref[...], NT)
        dckv, dgkv = _rms_bwd(ckv_ref[...], gkv_ref[...], dckvn)
        dckv_ref[...] = dckv.astype(BF16)
        _acc(dgq_ref, _colsum8(dgq))
        _acc(dgkv_ref, _colsum8(dgkv))

    return pl.pallas_call(
        body, name="qkv_bwd", grid=(lp // ROWB,),
        in_specs=[_rowspec(ROWB, qw), _rowspec(ROWB, qw), _rowspec(ROWB, ATT_HEADS * V_HEAD),
                  _rowspec(ROWB, Q_LORA), _rowspec(ROWB, KV_LORA)] + [_rowspec(ROWB, HEADW)] * 3
        + [_fullspec((1, Q_LORA)), _fullspec((1, KV_LORA)), _fullspec((Q_LORA, qw)), _fullspec((KV_LORA, qw)),
           _fullspec((KV_LORA, ATT_HEADS * V_HEAD))],
        out_specs=[_rowspec(ROWB, Q_LORA), _rowspec(ROWB, KV_LORA), _rowspec(ROWB, HEADW), _rowspec(ROWB, qw),
                   _rowspec(ROWB, qw), _fullspec((8, Q_LORA)), _fullspec((8, KV_LORA))],
        out_shape=[_sds((lp, Q_LORA), BF16), _sds((lp, KV_LORA), BF16), _sds((lp, HEADW), BF16), _sds((lp, qw), BF16),
                   _sds((lp, qw), BF16), _sds((8, Q_LORA), F32), _sds((8, KV_LORA), F32)],
        compiler_params=_cp(("arbitrary",)),
    )(dq, dk, dv, cq, ckv, cos, sa, sb, gq, gkv, wq, wk, wv)


def _in_bwd(dproj, h, dh1, g, w_all):
    lp = h.shape[0]

    def body(dp_ref, h_ref, dh1_ref, g_ref, w_ref, dh_ref, dg_ref):
        dx, dg = _rms_bwd(h_ref[...], g_ref[...], _dot(dp_ref[...], w_ref[...], NT))
        dh_ref[...] = dh1_ref[...] + dx
        _acc(dg_ref, _colsum8(dg))

    return pl.pallas_call(
        body, name="in_bwd", grid=(lp // ROWB,),
        in_specs=[_rowspec(ROWB, PROJ_W), _rowspec(ROWB, D_MODEL), _rowspec(ROWB, D_MODEL), _fullspec((1, D_MODEL)),
                  _fullspec((D_MODEL, PROJ_W))],
        out_specs=[_rowspec(ROWB, D_MODEL), _fullspec((8, D_MODEL))],
        out_shape=[_sds((lp, D_MODEL), F32), _sds((8, D_MODEL), F32)],
        compiler_params=_cp(("arbitrary",)),
    )(dproj, h, dh1, g, w_all)


def _tile_of(n, cap=1024):
    return max(t for t in range(LANE, min(n, cap) + 1, LANE) if n % t == 0)


def _matmul_tn(name, a, b):
    rows, kd = a.shape
    nd = b.shape[1]
    tk, tn = _tile_of(kd), _tile_of(nd)

    def body(a_ref, b_ref, o_ref):
        @pl.when(pl.program_id(2) == 0)
        def _():
            o_ref[...] = jnp.zeros_like(o_ref)

        o_ref[...] += _dot(a_ref[...], b_ref[...], ((0,), (0,)))

    return pl.pallas_call(
        body, name=name, grid=(kd // tk, nd // tn, rows // ROWB),
        in_specs=[pl.BlockSpec((ROWB, tk), lambda i, j, r: (r, i)), pl.BlockSpec((ROWB, tn), lambda i, j, r: (r, j))],
        out_specs=pl.BlockSpec((tk, tn), lambda i, j, r: (i, j)), out_shape=_sds((kd, nd), F32),
        compiler_params=_cp(("arbitrary", "arbitrary", "arbitrary")),
    )(a, b)


def _local_backward(h, f, p):
    g = {}
    row = lambda v: _row1(v)
    s8 = lambda v: jnp.sum(v, axis=0)
    dh1, du, a_, df, dgpre, dgpost = _mlp_bwd(f["dh2"], f["f"], f["h1"], f["hn2"], p["w_mlp_up"], p["w_mlp_down"],
                                              row(p["norm_mlp_pre"]), row(p["norm_mlp_post"]))
    g["norm_mlp_pre"], g["norm_mlp_post"] = s8(dgpre), s8(dgpost)
    g["w_mlp_up"] = _matmul_tn("dw_mlp_up", f["hn2"], du)
    g["w_mlp_down"] = _matmul_tn("dw_mlp_down", a_, df)
    dmix, datt, dssm, dgmp = _out_bwd(dh1, f["mix"], p["w_out"], row(p["norm_mix_post"]))
    g["norm_mix_post"] = s8(dgmp)
    g["w_out"] = jnp.concatenate([_matmul_tn("dw_out_att", f["att"], dmix), _matmul_tn("dw_out_ssm", f["ssm"], dmix)], axis=0)
    dy, dz, dgn, dd = _ssd_post_bwd(dssm, f["y"], f["xact"], f["z"], f["dskip"], row(p["ssm_norm"]))
    g["ssm_norm"] = s8(dgn)
    g["d_skip"] = s8(dd).reshape(SSM_HEADS, SSM_HEAD_DIM).sum(axis=1)
    dact, ddtr, da8, dbias8 = _ssd_bwd(dy, f["xact"], f["dtr"], f["hprev"], f["dt_bias"], f["a_log"], f["dskip"], f["seq_rows"])
    g["a_log"], g["dt_bias"] = s8(da8)[:SSM_HEADS], s8(dbias8)[:SSM_HEADS]
    dxbc, dcw8, dcb8 = _conv_bwd(dact, f["xbc"], f["cw"], row(p["conv_b"]))
    g["conv_w"], g["conv_b"] = dcw8[:CONV_K], s8(dcb8)
    delta = _attn_delta(datt, f["att"])
    lp = h.shape[0]
    to_row = lambda t: t[:, :, 0].reshape(ATT_HEADS, 1, lp)
    dq = _flash_dq(f["q"], f["k"], f["v"], datt, f["lse"], delta)
    dk, dv = _flash_dkv(f["q"], f["k"], f["v"], datt, to_row(f["lse"]), to_row(delta))
    dcq, dckv, dkr, dqp, dkb, dgq, dgkv = _qkv_bwd(dq, dk, dv, f["cq"], f["ckv"], *f["rope"], row(p["q_a_norm"]),
                                                   row(p["kv_a_norm"]), f["wq"], f["wk"], f["wv"])
    g["q_a_norm"], g["kv_a_norm"] = s8(dgq), s8(dgkv)
    dwq = _matmul_tn("dw_q_up", f["cqn"], dqp).reshape(Q_LORA, ATT_HEADS, HEADW)
    g["w_q_up"] = dwq[:, :, :QK_NOPE + QK_ROPE].reshape(Q_LORA, -1)
    dwk = _matmul_tn("dw_k_up", f["ckvn"], dkb).reshape(KV_LORA, ATT_HEADS, HEADW)[:, :, :QK_NOPE]
    dwv = _matmul_tn("dw_v_up", f["ckvn"], dv).reshape(KV_LORA, ATT_HEADS, V_HEAD)
    g["w_kv_up"] = jnp.concatenate([dwk, dwv], axis=2).reshape(KV_LORA, -1)
    dproj = jnp.concatenate([dcq, dckv, dkr, dz, dxbc, ddtr.astype(BF16)], axis=1)
    dh, dgin = _in_bwd(dproj, h, dh1, row(p["norm_mix_pre"]), f["w_all"])
    g["norm_mix_pre"] = s8(dgin)
    dwa = _matmul_tn("dw_in", f["hn"], dproj)
    g["w_in"] = jnp.concatenate([dwa[:, PC_Q:PC_KR], dwa[:, PC_KR + QK_NOPE:PC_KR + QK_NOPE + QK_ROPE],
                                 dwa[:, PC_Z:PC_DT + SSM_HEADS]], axis=1)
    return dh, g


BIG = {"w_in": ((D_MODEL, IN_WIDTH), 1), "w_q_up": ((Q_LORA, ATT_HEADS * (QK_NOPE + QK_ROPE)), 1),
       "w_kv_up": ((KV_LORA, ATT_HEADS * (QK_NOPE + V_HEAD)), 1), "w_out": ((2 * D_MODEL, D_MODEL), 0),
       "w_mlp_up": ((D_MODEL, D_FF), 1), "w_mlp_down": ((D_FF, D_MODEL), 0), "conv_w": ((CONV_K, CONV_DIM), 1),
       "meta_tokens": ((N_META, D_MODEL), 1)}
SMALL = {"norm_mix_pre": D_MODEL, "q_a_norm": Q_LORA, "kv_a_norm": KV_LORA, "conv_b": CONV_DIM, "dt_bias": SSM_HEADS,
         "a_log": SSM_HEADS, "d_skip": SSM_HEADS, "ssm_norm": SSM_WIDTH, "norm_mix_post": D_MODEL,
         "norm_mlp_pre": D_MODEL, "norm_mlp_post": D_MODEL}
WEIGHT_ORDER = ("meta_tokens", "norm_mix_pre", "w_in", "q_a_norm", "w_q_up", "kv_a_norm", "w_kv_up", "conv_w", "conv_b",
                "dt_bias", "a_log", "d_skip", "ssm_norm", "w_out", "norm_mix_post", "norm_mlp_pre", "w_mlp_up",
                "w_mlp_down", "norm_mlp_post")
PACK_BLOCK = 512


def _shard_shape(name):
    shape, ax = BIG[name]
    return tuple(d // N_DEV if a == ax else d for a, d in enumerate(shape))


BIG_ROWS = -(-sum(math.prod(_shard_shape(n)) for n in BIG) // (LANE * PACK_BLOCK)) * PACK_BLOCK
SMALL_ROWS = -(-sum(SMALL.values()) // (LANE * 8)) * 8


def _pack(flats, rows):
    v = jnp.concatenate([f.reshape(-1) for f in flats])
    return jnp.pad(v, (0, rows * LANE - v.shape[0])).reshape(rows, LANE)


def _unpack(packed, shapes):
    v = packed.reshape(-1)
    out, o = [], 0
    for s in shapes:
        n = math.prod(s)
        out.append(v[o:o + n].reshape(s))
        o += n
    return out


def _to_chunks(name, full):
    shape, ax = BIG[name]
    if ax == 0:
        return full.reshape(N_DEV, -1)
    k, n = shape
    return full.reshape(k, N_DEV, n // N_DEV).transpose(1, 0, 2).reshape(N_DEV, -1)


def _from_shards(name, shards):
    shape, ax = BIG[name]
    if ax == 0:
        return shards.reshape(shape)
    return shards.transpose(1, 0, 2).reshape(shape)


def _peer(k):
    x, y, c = lax.axis_index("x"), lax.axis_index("y"), lax.axis_index("c")
    px = 1 - x if k & 4 else x
    py = 1 - y if k & 2 else y
    pc = 1 - c if k & 1 else c
    return (px, py, pc), 4 * px + 2 * py + pc


def _all_gather(shard):
    rows = shard.shape[0]

    def body(x_ref, out_ref, send_sems, recv_sems, local_sem):
        _, me = _peer(0)
        sibling, _ = _peer(1)
        chips = (4, 2, 6)

        def copy(sem, block, to, src=None):
            return pltpu.make_async_remote_copy(
                src_ref=out_ref.at[block] if src is None else src, dst_ref=out_ref.at[block],
                send_sem=send_sems.at[sem], recv_sem=recv_sems.at[sem], device_id=to, device_id_type=MESH)

        mine = pltpu.make_async_copy(x_ref, out_ref.at[me], local_sem)
        mine.start()
        first = [copy(0, me, sibling, src=x_ref)]
        first += [copy(1 + n, me, _peer(k)[0], src=x_ref) for n, k in enumerate(chips)]
        for cp in first:
            cp.start()
        passed = [copy(4 + n, _peer(k)[1], sibling) for n, k in enumerate(chips)]
        for n, k in enumerate(chips):
            copy(1 + n, _peer(k)[1], sibling).wait_recv()
            passed[n].start()
        copy(0, _peer(1)[1], sibling).wait_recv()
        for n, k in enumerate(chips):
            copy(4 + n, _peer(k | 1)[1], sibling).wait_recv()
        for cp in first + passed:
            cp.wait_send()
        mine.wait()

    return pl.pallas_call(
        body, name="all_gather_weights", out_shape=_sds((N_DEV, rows, LANE), shard.dtype),
        in_specs=[ANY], out_specs=ANY,
        scratch_shapes=[pltpu.SemaphoreType.DMA((7,)), pltpu.SemaphoreType.DMA((7,)), pltpu.SemaphoreType.DMA],
    )(shard)


def _exchange(big, small):
    def body(big_ref, small_ref, obig_ref, osmall_ref, send_sems, recv_sems, local_sems):
        _, me = _peer(0)
        own = [pltpu.make_async_copy(big_ref.at[me], obig_ref.at[me], local_sems.at[0]),
               pltpu.make_async_copy(small_ref, osmall_ref.at[me], local_sems.at[1])]
        for cp in own:
            cp.start()

        def copies(k):
            to, idx = _peer(k)
            kw = dict(device_id=to, device_id_type=MESH)
            return [pltpu.make_async_remote_copy(src_ref=big_ref.at[idx], dst_ref=obig_ref.at[me],
                                                 send_sem=send_sems.at[2 * k - 2], recv_sem=recv_sems.at[2 * k - 2], **kw),
                    pltpu.make_async_remote_copy(src_ref=small_ref, dst_ref=osmall_ref.at[me],
                                                 send_sem=send_sems.at[2 * k - 1], recv_sem=recv_sems.at[2 * k - 1], **kw)]

        def arrivals(k):
            to, idx = _peer(k)
            kw = dict(device_id=to, device_id_type=MESH)
            return [pltpu.make_async_remote_copy(src_ref=big_ref.at[me], dst_ref=obig_ref.at[idx],
                                                 send_sem=send_sems.at[2 * k - 2], recv_sem=recv_sems.at[2 * k - 2], **kw),
                    pltpu.make_async_remote_copy(src_ref=small_ref, dst_ref=osmall_ref.at[idx],
                                                 send_sem=send_sems.at[2 * k - 1], recv_sem=recv_sems.at[2 * k - 1], **kw)]

        sent = [cp for k in range(1, N_DEV) for cp in copies(k)]
        for cp in sent:
            cp.start()
        for k in range(1, N_DEV):
            for cp in arrivals(k):
                cp.wait_recv()
        for cp in sent:
            cp.wait_send()
        for cp in own:
            cp.wait()

    n = 2 * (N_DEV - 1)
    return pl.pallas_call(
        body, name="exchange_grads",
        out_shape=[_sds(big.shape, big.dtype), _sds((N_DEV,) + small.shape, small.dtype)],
        in_specs=[ANY, ANY], out_specs=[ANY, ANY],
        scratch_shapes=[pltpu.SemaphoreType.DMA((n,)), pltpu.SemaphoreType.DMA((n,)), pltpu.SemaphoreType.DMA((2,))],
    )(big, small)


def _reduce_adamw(recv, w, m, v):
    rows = w.shape[0]
    blk = min(PACK_BLOCK, rows)
    c1 = 1.0 - ADAM_B1 ** ADAM_STEP
    c2 = 1.0 - ADAM_B2 ** ADAM_STEP

    def body(r_ref, w_ref, m_ref, v_ref, g_ref, d_ref, nm_ref, nv_ref):
        g = r_ref[0].astype(F32)
        for s in range(1, N_DEV):
            g = g + r_ref[s].astype(F32)
        g_ref[...] = g
        m_ = ADAM_B1 * m_ref[...] + (1.0 - ADAM_B1) * g
        v_ = ADAM_B2 * v_ref[...] + (1.0 - ADAM_B2) * (g * g)
        nm_ref[...] = m_
        nv_ref[...] = v_
        d_ref[...] = -ADAM_LR * ((m_ / c1) / (jnp.sqrt(v_ / c2) + ADAM_EPS) + ADAM_WD * w_ref[...])

    spec = _rowspec(blk, LANE)
    return pl.pallas_call(
        body, name="reduce_adamw", grid=(rows // blk,),
        in_specs=[pl.BlockSpec((N_DEV, blk, LANE), lambda i: (0, i, 0)), spec, spec, spec],
        out_specs=[spec] * 4, out_shape=[_sds((rows, LANE), F32)] * 4,
        compiler_params=_cp(("arbitrary",)),
    )(recv, w, m, v)


def kernel(x, meta_tokens, norm_mix_pre, w_in, q_a_norm, w_q_up, kv_a_norm, w_kv_up, conv_w, conv_b, dt_bias, a_log, d_skip, ssm_norm, w_out, norm_mix_post, norm_mlp_pre, w_mlp_up, w_mlp_down, norm_mlp_post, loss_target, m_meta_tokens, m_norm_mix_pre, m_w_in, m_q_a_norm, m_w_q_up, m_kv_a_norm, m_w_kv_up, m_conv_w, m_conv_b, m_dt_bias, m_a_log, m_d_skip, m_ssm_norm, m_w_out, m_norm_mix_post, m_norm_mlp_pre, m_w_mlp_up, m_w_mlp_down, m_norm_mlp_post, v_meta_tokens, v_norm_mix_pre, v_w_in, v_q_a_norm, v_w_q_up, v_kv_a_norm, v_w_kv_up, v_conv_w, v_conv_b, v_dt_bias, v_a_log, v_d_skip, v_ssm_norm, v_w_out, v_norm_mix_post, v_norm_mlp_pre, v_w_mlp_up, v_w_mlp_down, v_norm_mlp_post):
    w = dict(meta_tokens=meta_tokens, norm_mix_pre=norm_mix_pre, w_in=w_in, q_a_norm=q_a_norm, w_q_up=w_q_up,
             kv_a_norm=kv_a_norm, w_kv_up=w_kv_up, conv_w=conv_w, conv_b=conv_b, dt_bias=dt_bias, a_log=a_log,
             d_skip=d_skip, ssm_norm=ssm_norm, w_out=w_out, norm_mix_post=norm_mix_post, norm_mlp_pre=norm_mlp_pre,
             w_mlp_up=w_mlp_up, w_mlp_down=w_mlp_down, norm_mlp_post=norm_mlp_post)
    m = dict(meta_tokens=m_meta_tokens, norm_mix_pre=m_norm_mix_pre, w_in=m_w_in, q_a_norm=m_q_a_norm, w_q_up=m_w_q_up,
             kv_a_norm=m_kv_a_norm, w_kv_up=m_w_kv_up, conv_w=m_conv_w, conv_b=m_conv_b, dt_bias=m_dt_bias,
             a_log=m_a_log, d_skip=m_d_skip, ssm_norm=m_ssm_norm, w_out=m_w_out, norm_mix_post=m_norm_mix_post,
             norm_mlp_pre=m_norm_mlp_pre, w_mlp_up=m_w_mlp_up, w_mlp_down=m_w_mlp_down, norm_mlp_post=m_norm_mlp_post)
    v = dict(meta_tokens=v_meta_tokens, norm_mix_pre=v_norm_mix_pre, w_in=v_w_in, q_a_norm=v_q_a_norm, w_q_up=v_w_q_up,
             kv_a_norm=v_kv_a_norm, w_kv_up=v_w_kv_up, conv_w=v_conv_w, conv_b=v_conv_b, dt_bias=v_dt_bias,
             a_log=v_a_log, d_skip=v_d_skip, ssm_norm=v_ssm_norm, w_out=v_w_out, norm_mix_post=v_norm_mix_post,
             norm_mlp_pre=v_norm_mlp_pre, w_mlp_up=v_w_mlp_up, w_mlp_down=v_w_mlp_down, norm_mlp_post=v_norm_mlp_post)
    seq = x.shape[1]
    seq_rows = N_META + seq
    lp = -(-(PADF + seq_rows) // ROWB) * ROWB
    big_names = [n for n in WEIGHT_ORDER if n in BIG]
    small_names = [n for n in WEIGHT_ORDER if n in SMALL]
    shard_shapes = [_shard_shape(n) for n in big_names]

    f32_names = ("conv_w", "meta_tokens")
    mm_names = [n for n in big_names if n not in f32_names]
    bits = [lax.bitcast_convert_type(w[n].astype(F32), BF16) for n in f32_names]
    payload = _pack([w[n].astype(BF16) for n in mm_names] + bits, BIG_ROWS)
    flat = _all_gather(payload).reshape(N_DEV, -1)
    p = {n: w[n].reshape(-1) for n in small_names}
    o = 0
    for n in mm_names:
        s = _shard_shape(n)
        p[n] = _from_shards(n, flat[:, o:o + math.prod(s)].reshape((N_DEV,) + s))
        o += math.prod(s)
    for n in f32_names:
        s = _shard_shape(n)
        shards = lax.bitcast_convert_type(flat[:, o:o + 2 * math.prod(s)].reshape((N_DEV,) + s + (2,)), F32)
        p[n] = _from_shards(n, shards)
        o += 2 * math.prod(s)
    meta_full = p["meta_tokens"]

    zeros = lambda r: jnp.zeros((r, D_MODEL), F32)
    h = jnp.concatenate([zeros(PADF), meta_full, x[0], zeros(lp - PADF - seq_rows)], axis=0)
    tgt = jnp.concatenate([zeros(PADF + N_META), loss_target[0], zeros(lp - PADF - seq_rows)], axis=0)

    f = _local_forward(h, tgt, p, seq_rows)
    dh, g = _local_backward(h, f, p)
    g["meta_tokens"] = dh[PADF:PADF + N_META]
    grad_x = dh[PADF + N_META:PADF + seq_rows][None]
    loss = lax.psum(f["loss"], ("x", "y", "c"))

    chunks = jnp.concatenate([_to_chunks(n, g[n]) for n in big_names], axis=1).astype(BF16)
    chunks = jnp.pad(chunks, ((0, 0), (0, BIG_ROWS * LANE - chunks.shape[1]))).reshape(N_DEV, BIG_ROWS, LANE)
    small = _pack([g[n] for n in small_names], SMALL_ROWS)
    recv_big, recv_small = _exchange(chunks, small)

    outs = {}
    for names, recv, rows, shapes in ((big_names, recv_big, BIG_ROWS, shard_shapes),
                                      (small_names, recv_small, SMALL_ROWS, [(SMALL[n],) for n in small_names])):
        packed = [_pack([d[n].astype(F32) for n in names], rows) for d in (w, m, v)]
        res = _reduce_adamw(recv, *packed)
        for kind, arr in zip(("grad", "delta", "new_m", "new_v"), res):
            for n, val in zip(names, _unpack(arr, shapes)):
                outs[kind, n] = val.reshape(w[n].shape)
    return (loss, grad_x) + tuple(outs[kind, n] for kind in ("grad", "delta", "new_m", "new_v") for n in WEIGHT_ORDER)
```

```python
import functools
import math

import jax
import jax.numpy as jnp
import numpy as np
from jax import lax
from jax.experimental import pallas as pl
from jax.experimental.pallas import tpu as pltpu

F32 = jnp.float32
BF16 = jnp.bfloat16

D_MODEL = 1024
N_META = 16
EPS = 1e-6
ATT_HEADS = 8
Q_LORA = 384
KV_LORA = 256
QK_NOPE = 128
QK_ROPE = 64
V_HEAD = 128
ROPE_THETA = 10000.0
SSM_HEADS = 16
SSM_HEAD_DIM = 64
SSM_WIDTH = 1024
SSM_GROUPS = 2
SSM_STATE = 128
CONV_K = 4
CHUNK = 128
CONV_DIM = 1536
D_FF = 4096
IN_SPLITS = (Q_LORA, KV_LORA, QK_ROPE, SSM_WIDTH, CONV_DIM, SSM_HEADS)
IN_WIDTH = sum(IN_SPLITS)
ADAM_LR, ADAM_B1, ADAM_B2, ADAM_EPS, ADAM_WD, ADAM_STEP = 0.001, 0.9, 0.999, 1e-08, 0.01, 10

LANE = 128
PADF = CHUNK - N_META
ROWB = 512
HEADW = 256
PC_Q, PC_KV, PC_KR, PC_Z, PC_XBC, PC_DT, PROJ_W = 0, 384, 640, 896, 1920, 3456, 3584
NEG = -1e30
N_DEV = 8
VMEM_LIMIT = 56 * 1024 * 1024
MESH = pl.DeviceIdType.MESH


def _cp(sem, vmem=VMEM_LIMIT, **kw):
    return pltpu.CompilerParams(dimension_semantics=sem, vmem_limit_bytes=vmem, **kw)


def _dot(a, b, dims=((1,), (0,)), prec=None):
    return lax.dot_general(a, b, (dims, ((), ())), preferred_element_type=F32, precision=prec)


def _bdot(a, b, dims=((1,), (0,))):
    return _dot(a.astype(BF16), b.astype(BF16), dims)


NT = ((1,), (1,))
HI = lax.Precision.HIGHEST


def _rms_fwd(x, w):
    r = lax.rsqrt(jnp.mean(x * x, axis=-1, keepdims=True) + EPS)
    return (x * r) * w


def _rms_bwd(x, w, dy):
    r = lax.rsqrt(jnp.mean(x * x, axis=-1, keepdims=True) + EPS)
    xh = x * r
    g = dy * w
    dx = r * (g - xh * jnp.mean(g * xh, axis=-1, keepdims=True))
    return dx, dy * xh


def _sigmoid(x):
    return 1.0 / (1.0 + jnp.exp(-x))


def _colsum8(x):
    t, c = x.shape
    return jnp.sum(x.reshape(t // 8, 8, c), axis=0)


def _rowspec(t, c, cb=0):
    return pl.BlockSpec((t, c), lambda i: (i, cb))


def _fullspec(shape):
    n = len(shape)
    return pl.BlockSpec(shape, lambda i: (0,) * n)


def _sds(shape, dt):
    return jax.ShapeDtypeStruct(shape, dt)


def _acc(ref, val):
    @pl.when(pl.program_id(0) == 0)
    def _():
        ref[...] = jnp.zeros_like(ref)

    ref[...] += val


def _norm_in_proj(h, g, w_all):
    lp = h.shape[0]

    def body(h_ref, g_ref, w_ref, hn_ref, cq_ref, ckv_ref, kr_ref, z_ref, xbc_ref, dt_ref):
        hn = _rms_fwd(h_ref[...], g_ref[...]).astype(BF16)
        hn_ref[...] = hn
        p = _dot(hn, w_ref[...])
        cq_ref[...] = p[:, PC_Q:PC_KV]
        ckv_ref[...] = p[:, PC_KV:PC_KR]
        kr_ref[...] = p[:, PC_KR:PC_Z]
        z_ref[...] = p[:, PC_Z:PC_XBC]
        xbc_ref[...] = p[:, PC_XBC:PC_DT]
        dt_ref[...] = p[:, PC_DT:PROJ_W]

    widths = (Q_LORA, KV_LORA, HEADW, SSM_WIDTH, CONV_DIM, LANE)
    return pl.pallas_call(
        body, name="norm_in_proj", grid=(lp // ROWB,),
        in_specs=[_rowspec(ROWB, D_MODEL), _fullspec((1, D_MODEL)), _fullspec((D_MODEL, PROJ_W))],
        out_specs=[_rowspec(ROWB, D_MODEL)] + [_rowspec(ROWB, w) for w in widths],
        out_shape=[_sds((lp, D_MODEL), BF16)] + [_sds((lp, w), F32) for w in widths],
        compiler_params=_cp(("arbitrary",)),
    )(h, g, w_all)


def _rope(x, cos, sa, sb):
    w = x.shape[1]
    return x * cos + pltpu.roll(x, w - 32, 1) * sa + pltpu.roll(x, 32, 1) * sb


def _rope_t(dy, cos, sa, sb):
    w = dy.shape[1]
    return dy * cos + pltpu.roll(dy * sa, 32, 1) + pltpu.roll(dy * sb, w - 32, 1)


def _tile8(t):
    return jnp.concatenate([t] * ATT_HEADS, axis=1)


def _qkv(cq, ckv, kr, cos, sa, sb, gq, gkv, wq, wk, wv):
    lp = cq.shape[0]
    qw = ATT_HEADS * HEADW

    def body(cq_ref, ckv_ref, kr_ref, cos_ref, sa_ref, sb_ref, gq_ref, gkv_ref, wq_ref, wk_ref, wv_ref,
             q_ref, k_ref, v_ref, cqn_ref, ckvn_ref):
        cos_, sa_, sb_ = cos_ref[...], sa_ref[...], sb_ref[...]
        cqn = _rms_fwd(cq_ref[...], gq_ref[...]).astype(BF16)
        ckvn = _rms_fwd(ckv_ref[...], gkv_ref[...]).astype(BF16)
        cqn_ref[...] = cqn
        ckvn_ref[...] = ckvn
        q = _dot(cqn, wq_ref[...])
        q_ref[...] = (_rope(q, _tile8(cos_), _tile8(sa_), _tile8(sb_)) * Q_PRESCALE).astype(BF16)
        k = _dot(ckvn, wk_ref[...]) + _tile8(_rope(kr_ref[...], cos_, sa_, sb_))
        k_ref[...] = k.astype(BF16)
        lanes = lax.broadcasted_iota(jnp.int32, (1, qw), 1)
        ones = ((lanes % HEADW) >= V_HEAD).astype(F32)
        v_ref[...] = (_dot(ckvn, wv_ref[...]) + ones).astype(BF16)

    return pl.pallas_call(
        body, name="qkv", grid=(lp // ROWB,),
        in_specs=[_rowspec(ROWB, Q_LORA), _rowspec(ROWB, KV_LORA), _rowspec(ROWB, HEADW)]
        + [_rowspec(ROWB, HEADW)] * 3
        + [_fullspec((1, Q_LORA)), _fullspec((1, KV_LORA)), _fullspec((Q_LORA, qw)), _fullspec((KV_LORA, qw)),
           _fullspec((KV_LORA, qw))],
        out_specs=[_rowspec(ROWB, qw), _rowspec(ROWB, qw), _rowspec(ROWB, qw),
                   _rowspec(ROWB, Q_LORA), _rowspec(ROWB, KV_LORA)],
        out_shape=[_sds((lp, qw), BF16), _sds((lp, qw), BF16), _sds((lp, qw), BF16),
                   _sds((lp, Q_LORA), BF16), _sds((lp, KV_LORA), BF16)],
        compiler_params=_cp(("arbitrary",)),
    )(cq, ckv, kr, cos, sa, sb, gq, gkv, wq, wk, wv)


ATT_SCALE = (QK_NOPE + QK_ROPE) ** -0.5
LOG2E = 1.4426950408889634
LN2 = 0.6931471805599453
Q_PRESCALE = ATT_SCALE * LOG2E
KVB = 512


def _att_ok(qrow, krow):
    return (krow <= qrow) & ((krow >= PADF) | (qrow < PADF))


def _lanes(x, n):
    return x if n == 1 else jnp.concatenate([x] * n, axis=1)


def _pair_loop(lo, hi, tile, u=2):
    n = jnp.maximum(hi - lo, 0)

    def many(t, c):
        for d in range(u):
            tile(lo + u * t + d)
        return c

    lax.fori_loop(0, n // u, many, 0)

    def one(j, c):
        tile(j)
        return c

    lax.fori_loop(lo + (n // u) * u, hi, one, 0)


def _flash_fwd(q, k, v):
    lp = q.shape[0]
    nq = lp // ROWB
    rep = KVB // LANE

    def body(q_ref, k_ref, v_ref, o_ref, lse_ref, acc, m_s):
        i = pl.program_id(1)
        qb = q_ref[...]
        m_s[...] = jnp.full_like(m_s, NEG)
        acc[...] = jnp.zeros_like(acc)

        def tile(j, masked):
            off = pl.multiple_of(j * KVB, KVB)
            kb = k_ref[pl.ds(off, KVB), :]
            vb = v_ref[pl.ds(off, KVB), :]
            s = _dot(qb, kb, NT)
            if masked:
                qrow = i * ROWB + lax.broadcasted_iota(jnp.int32, s.shape, 0)
                krow = j * KVB + lax.broadcasted_iota(jnp.int32, s.shape, 1)
                s = jnp.where(_att_ok(qrow, krow), s, NEG)
            m_prev = m_s[...]
            m_new = jnp.maximum(m_prev, jnp.max(s, axis=1, keepdims=True))
            alpha = jnp.exp2(m_prev - m_new)
            p = jnp.exp2(s - _lanes(m_new, rep))
            acc[...] = _lanes(alpha, 2) * acc[...] + _dot(p.astype(BF16), vb)
            m_s[...] = m_new

        tile(0, True)
        _pair_loop(1, i, lambda j: tile(j, False), 4)

        @pl.when(i > 0)
        def _():
            tile(i, True)

        l = acc[:, V_HEAD:]
        o_ref[...] = (acc[:, :V_HEAD] / l).astype(BF16)
        lse_ref[0] = m_s[...] + jnp.log2(l)

    return pl.pallas_call(
        body, name="flash_fwd", grid=(ATT_HEADS, nq),
        in_specs=[pl.BlockSpec((ROWB, HEADW), lambda h, i: (i, h)),
                  pl.BlockSpec((lp, HEADW), lambda h, i: (0, h)),
                  pl.BlockSpec((lp, HEADW), lambda h, i: (0, h))],
        out_specs=[pl.BlockSpec((ROWB, V_HEAD), lambda h, i: (i, h)),
                   pl.BlockSpec((1, ROWB, LANE), lambda h, i: (h, i, 0))],
        out_shape=[_sds((lp, ATT_HEADS * V_HEAD), BF16), _sds((ATT_HEADS, lp, LANE), F32)],
        scratch_shapes=[pltpu.VMEM((ROWB, HEADW), F32), pltpu.VMEM((ROWB, LANE), F32)],
        compiler_params=_cp(("arbitrary", "arbitrary")),
    )(q, k, v)


def _silu(x):
    return x * _sigmoid(x)


def _conv_fwd(xbc, cw, cb):
    lp, c = xbc.shape
    t8 = ROWB // 8

    def body(x_ref, prev_ref, w_ref, b_ref, o_ref, buf):
        i = pl.program_id(0)
        buf[pl.ds(0, 8), :] = jnp.where(i > 0, prev_ref[...], 0.0)
        buf[pl.ds(8, ROWB), :] = x_ref[...]
        w = w_ref[...]
        pre = b_ref[...] + sum(w[kk:kk + 1, :] * buf[pl.ds(8 - (CONV_K - 1) + kk, ROWB), :] for kk in range(CONV_K))
        o_ref[...] = _silu(pre)

    return pl.pallas_call(
        body, name="conv_fwd", grid=(lp // ROWB,),
        in_specs=[_rowspec(ROWB, c), pl.BlockSpec((8, c), lambda i: (jnp.maximum(i * t8 - 1, 0), 0)),
                  _fullspec((8, c)), _fullspec((1, c))],
        out_specs=_rowspec(ROWB, c), out_shape=_sds((lp, c), F32),
        scratch_shapes=[pltpu.VMEM((ROWB + 8, c), F32)],
        compiler_params=_cp(("arbitrary",)),
    )(xbc, xbc, cw, cb)


def _expand_mat():
    r = np.arange(LANE)[:, None]
    c = np.arange(SSM_WIDTH)[None, :]
    return jnp.asarray((c // SSM_HEAD_DIM == r).astype(np.float32))


def _tri_mat():
    i = np.arange(CHUNK)
    return jnp.asarray((i[:, None] >= i[None, :]).astype(np.float32))


def _ssd_prep(dtr_ref, bias_ref, alog_ref, tri, c, seq_rows):
    rows = c * CHUNK + lax.broadcasted_iota(jnp.int32, (CHUNK, LANE), 0)
    lanes = lax.broadcasted_iota(jnp.int32, (CHUNK, LANE), 1)
    valid = (rows >= PADF) & (rows < PADF + seq_rows) & (lanes < SSM_HEADS)
    dtr = dtr_ref[...] + bias_ref[...]
    sp = jnp.maximum(dtr, 0.0) + jnp.log(1.0 + jnp.exp(-jnp.abs(dtr)))
    dt = jnp.where(valid, sp, 0.0)
    a = -jnp.exp(alog_ref[...])
    acol = _dot(tri, dt * a, prec=HI)
    return dt, a, acol, valid, dtr


def _ssd_fwd(xbc_act, dtr, dt_bias, a_log, seq_rows):
    lp = xbc_act.shape[0]
    nc = lp // CHUNK
    gw = SSM_WIDTH // SSM_GROUPS
    hpg = SSM_HEADS // SSM_GROUPS

    def body(x_ref, b_ref, c_ref, dtr_ref, bias_ref, alog_ref, tri_ref, ex_ref, y_ref, hp_ref, h_s):
        c = pl.program_id(0)

        @pl.when(c == 0)
        def _():
            h_s[...] = jnp.zeros_like(h_s)

        tri = tri_ref[...]
        ex = ex_ref[...]
        dt, a, acol, _, _ = _ssd_prep(dtr_ref, bias_ref, alog_ref, tri, c, seq_rows)
        arow = acol.T
        dtrow = dt.T
        alast = acol[CHUNK - 1:CHUNK, :]
        e_all = _dot(jnp.exp(acol), ex, prec=HI)
        wx_all = _dot(jnp.exp(alast - acol) * dt, ex, prec=HI)
        dec_all = _dot(jnp.broadcast_to(jnp.exp(alast), (8, LANE)), ex, prec=HI)[0:1, :]
        causal = tri > 0.5
        hp_ref[0] = h_s[...]
        for g in range(SSM_GROUPS):
            gs = slice(g * gw, (g + 1) * gw)
            bg = b_ref[:, g * SSM_STATE:(g + 1) * SSM_STATE]
            cg = c_ref[:, g * SSM_STATE:(g + 1) * SSM_STATE].astype(BF16)
            xg = x_ref[:, gs]
            hg = h_s[:, gs]
            gm = _bdot(cg, bg, NT)
            y_off = _bdot(cg, hg) * e_all[:, gs]
            for r in range(hpg):
                hd = g * hpg + r
                seg = acol[:, hd:hd + 1] - arow[hd:hd + 1, :]
                lm = jnp.where(causal, jnp.exp(jnp.where(causal, seg, 0.0)), 0.0)
                w = gm * lm * dtrow[hd:hd + 1, :]
                cs = slice(r * SSM_HEAD_DIM, (r + 1) * SSM_HEAD_DIM)
                y_ref[:, pl.ds(hd * SSM_HEAD_DIM, SSM_HEAD_DIM)] = _bdot(w, xg[:, cs]) + y_off[:, cs]
            st = _bdot(bg.T, xg * wx_all[:, gs])
            h_s[:, gs] = hg * dec_all[:, gs] + st

    xs_spec = pl.BlockSpec((CHUNK, SSM_WIDTH), lambda c: (c, 0))
    b_spec = pl.BlockSpec((CHUNK, 2 * SSM_STATE), lambda c: (c, SSM_WIDTH // (2 * SSM_STATE)))
    c_spec = pl.BlockSpec((CHUNK, 2 * SSM_STATE), lambda c: (c, SSM_WIDTH // (2 * SSM_STATE) + 1))
    return pl.pallas_call(
        body, name="ssd_fwd", grid=(nc,),
        in_specs=[xs_spec, b_spec, c_spec, pl.BlockSpec((CHUNK, LANE), lambda c: (c, 0)),
                  _fullspec((1, LANE)), _fullspec((1, LANE)), _fullspec((CHUNK, CHUNK)), _fullspec((LANE, SSM_WIDTH))],
        out_specs=[xs_spec, pl.BlockSpec((1, SSM_STATE, SSM_WIDTH), lambda c: (c, 0, 0))],
        out_shape=[_sds((lp, SSM_WIDTH), F32), _sds((nc, SSM_STATE, SSM_WIDTH), F32)],
        scratch_shapes=[pltpu.VMEM((SSM_STATE, SSM_WIDTH), F32)],
        compiler_params=_cp(("arbitrary",)),
    )(xbc_act, xbc_act, xbc_act, dtr, dt_bias, a_log, _tri_mat(), _expand_mat())


def _group_mean(x):
    gw = SSM_WIDTH // SSM_GROUPS
    parts = [jnp.broadcast_to(jnp.mean(x[:, g * gw:(g + 1) * gw], axis=-1, keepdims=True), (x.shape[0], gw))
             for g in range(SSM_GROUPS)]
    return jnp.concatenate(parts, axis=1)


def _ssd_post(y, xbc_act, z, dskip, gnorm):
    lp = y.shape[0]

    def body(y_ref, x_ref, z_ref, d_ref, g_ref, o_ref):
        z_ = z_ref[...]
        gt = (y_ref[...] + d_ref[...] * x_ref[...]) * _silu(z_)
        r = lax.rsqrt(_group_mean(gt * gt) + EPS)
        o_ref[...] = ((gt * r) * g_ref[...]).astype(BF16)

    return pl.pallas_call(
        body, name="ssd_post", grid=(lp // ROWB,),
        in_specs=[_rowspec(ROWB, SSM_WIDTH)] * 3 + [_fullspec((1, SSM_WIDTH))] * 2,
        out_specs=_rowspec(ROWB, SSM_WIDTH), out_shape=_sds((lp, SSM_WIDTH), BF16),
        compiler_params=_cp(("arbitrary",)),
    )(y, xbc_act, z, dskip, gnorm)


def _out_proj(att, ssm, h, w_out, g_post):
    lp = h.shape[0]

    def body(a_ref, s_ref, h_ref, w_ref, g_ref, mix_ref, h1_ref):
        mix = _dot(a_ref[...], w_ref[pl.ds(0, 1024), :]) + _dot(s_ref[...], w_ref[pl.ds(1024, 1024), :])
        mix_ref[...] = mix
        h1_ref[...] = h_ref[...] + _rms_fwd(mix, g_ref[...])

    return pl.pallas_call(
        body, name="out_proj", grid=(lp // ROWB,),
        in_specs=[_rowspec(ROWB, 1024)] * 3 + [_fullspec((2048, D_MODEL)), _fullspec((1, D_MODEL))],
        out_specs=[_rowspec(ROWB, D_MODEL)] * 2, out_shape=[_sds((lp, D_MODEL), F32)] * 2,
        compiler_params=_cp(("arbitrary",)),
    )(att, ssm, h, w_out, g_post)


def _resident(w_hbm, w_vmem, sem):
    @pl.when(pl.program_id(0) == 0)
    def _():
        cp = pltpu.make_async_copy(w_hbm, w_vmem, sem)
        cp.start()
        cp.wait()


ANY = pl.BlockSpec(memory_space=pl.ANY)


def _mlp_fwd(h1, tgt, w_up, w_down, g_pre, g_post, seq_rows):
    lp = h1.shape[0]

    def body(h1_ref, t_ref, wu_hbm, wd_hbm, gpre_ref, gpost_ref, hn2_ref, f_ref, dh2_ref, loss_ref, wu, wd, sems):
        _resident(wu_hbm, wu, sems.at[0])
        _resident(wd_hbm, wd, sems.at[1])
        i = pl.program_id(0)
        h1_ = h1_ref[...]
        hn2 = _rms_fwd(h1_, gpre_ref[...]).astype(BF16)
        hn2_ref[...] = hn2
        u = jnp.maximum(_dot(hn2, wu[...]), 0.0)
        f = _dot((u * u).astype(BF16), wd[...])
        f_ref[...] = f
        h2 = h1_ + _rms_fwd(f, gpost_ref[...])
        rows = i * ROWB + lax.broadcasted_iota(jnp.int32, (ROWB, 1), 0)
        real = (rows >= PADF + N_META) & (rows < PADF + seq_rows)
        err = jnp.where(real, h2 - t_ref[...], 0.0)
        dh2_ref[...] = err * (1.0 / D_MODEL)
        _acc(loss_ref, _colsum8(err * err))

    return pl.pallas_call(
        body, name="mlp_fwd", grid=(lp // ROWB,),
        in_specs=[_rowspec(ROWB, D_MODEL)] * 2 + [ANY, ANY] + [_fullspec((1, D_MODEL))] * 2,
        out_specs=[_rowspec(ROWB, D_MODEL)] * 3 + [_fullspec((8, D_MODEL))],
        out_shape=[_sds((lp, D_MODEL), BF16), _sds((lp, D_MODEL), F32), _sds((lp, D_MODEL), F32), _sds((8, D_MODEL), F32)],
        scratch_shapes=[pltpu.VMEM((D_MODEL, D_FF), BF16), pltpu.VMEM((D_FF, D_MODEL), BF16), pltpu.SemaphoreType.DMA((2,))],
        compiler_params=_cp(("arbitrary",)),
    )(h1, tgt, w_up, w_down, g_pre, g_post)


def _pad_cols(w, width):
    return jnp.pad(w, ((0, 0), (0, width - w.shape[1])))


def _layout_weights(w_in, w_q_up, w_kv_up):
    o = np.cumsum((0,) + IN_SPLITS)
    pieces = [w_in[:, o[k]:o[k + 1]] for k in range(6)]
    kr = jnp.pad(pieces[2], ((0, 0), (QK_NOPE, HEADW - QK_NOPE - QK_ROPE)))
    w_all = jnp.concatenate([pieces[0], pieces[1], kr, pieces[3], pieces[4], _pad_cols(pieces[5], LANE)], axis=1)
    wq = jnp.pad(w_q_up.reshape(Q_LORA, ATT_HEADS, QK_NOPE + QK_ROPE), ((0, 0), (0, 0), (0, HEADW - QK_NOPE - QK_ROPE)))
    wkv = w_kv_up.reshape(KV_LORA, ATT_HEADS, QK_NOPE + V_HEAD)
    wk = jnp.pad(wkv[:, :, :QK_NOPE], ((0, 0), (0, 0), (0, HEADW - QK_NOPE)))
    wv = wkv[:, :, QK_NOPE:]
    wvp = jnp.pad(wv, ((0, 0), (0, 0), (0, HEADW - V_HEAD)))
    return (w_all, wq.reshape(Q_LORA, -1), wk.reshape(KV_LORA, -1), wv.reshape(KV_LORA, -1),
            wvp.reshape(KV_LORA, -1))


def _rope_tables(lp):
    pos = jnp.maximum(jnp.arange(lp, dtype=jnp.int32) - PADF, 0).astype(F32)
    inv_freq = ROPE_THETA ** (-jnp.arange(0, QK_ROPE, 2, dtype=F32) / QK_ROPE)
    ang = pos[:, None] * inv_freq[None, :]
    cos, sin = jnp.cos(ang), jnp.sin(ang)
    one, zero = jnp.ones((lp, QK_NOPE), F32), jnp.zeros((lp, QK_NOPE), F32)
    z32, z64 = jnp.zeros((lp, 32), F32), jnp.zeros((lp, 64), F32)
    cos_t = jnp.concatenate([one, cos, cos, jnp.ones((lp, 64), F32)], axis=1)
    sa = jnp.concatenate([zero, -sin, z32, z64], axis=1)
    sb = jnp.concatenate([zero, z32, sin, z64], axis=1)
    return cos_t, sa, sb


def _row1(v, width=None):
    v = v.reshape(1, -1).astype(F32)
    return v if width is None else _pad_cols(v, width)


def _local_forward(h, tgt, p, seq_rows):
    lp = h.shape[0]
    f = {"seq_rows": seq_rows}
    w_all, wq, wk, wv, wvp = _layout_weights(p["w_in"], p["w_q_up"], p["w_kv_up"])
    f.update(w_all=w_all, wq=wq, wk=wk, wv=wv)
    f["hn"], cq, ckv, kr, f["z"], f["xbc"], f["dtr"] = _norm_in_proj(h, _row1(p["norm_mix_pre"]), w_all)
    f.update(cq=cq, ckv=ckv)
    f["rope"] = _rope_tables(lp)
    f["q"], f["k"], f["v"], f["cqn"], f["ckvn"] = _qkv(cq, ckv, kr, *f["rope"], _row1(p["q_a_norm"]),
                                                   _row1(p["kv_a_norm"]), wq, wk, wvp)
    f["att"], f["lse"] = _flash_fwd(f["q"], f["k"], f["v"])
    f["cw"] = jnp.pad(p["conv_w"].astype(F32), ((0, 8 - CONV_K), (0, 0)))
    f["xact"] = _conv_fwd(f["xbc"], f["cw"], _row1(p["conv_b"]))
    f["dt_bias"], f["a_log"] = _row1(p["dt_bias"], LANE), _row1(p["a_log"], LANE)
    f["y"], f["hprev"] = _ssd_fwd(f["xact"], f["dtr"], f["dt_bias"], f["a_log"], seq_rows)
    f["dskip"] = jnp.repeat(p["d_skip"].reshape(-1).astype(F32), SSM_HEAD_DIM).reshape(1, SSM_WIDTH)
    f["ssm"] = _ssd_post(f["y"], f["xact"], f["z"], f["dskip"], _row1(p["ssm_norm"]))
    f["mix"], f["h1"] = _out_proj(f["att"], f["ssm"], h, p["w_out"], _row1(p["norm_mix_post"]))
    f["hn2"], f["f"], f["dh2"], loss8 = _mlp_fwd(f["h1"], tgt, p["w_mlp_up"], p["w_mlp_down"],
                                                 _row1(p["norm_mlp_pre"]), _row1(p["norm_mlp_post"]), seq_rows)
    f["loss"] = 0.5 * jnp.sum(loss8) / D_MODEL
    return f


MLPB = 256


def _mlp_bwd(dh2, f, h1, hn2, w_up, w_down, g_pre, g_post):
    lp = h1.shape[0]

    def body(dh2_ref, f_ref, h1_ref, hn2_ref, wu_hbm, wd_hbm, gpre_ref, gpost_ref,
             dh1_ref, du_ref, a_ref, df_ref, dgpre_ref, dgpost_ref, wu, wd, sems):
        _resident(wu_hbm, wu, sems.at[0])
        _resident(wd_hbm, wd, sems.at[1])
        dh2_ = dh2_ref[...]
        df, dgp = _rms_bwd(f_ref[...], gpost_ref[...], dh2_)
        dfb = df.astype(BF16)
        df_ref[...] = dfb
        da = _dot(dfb, wd[...], NT)
        u = jnp.maximum(_dot(hn2_ref[...], wu[...]), 0.0)
        a_ref[...] = (u * u).astype(BF16)
        du = (da * (2.0 * u)).astype(BF16)
        du_ref[...] = du
        dhn2 = _dot(du, wu[...], NT)
        dx, dgq = _rms_bwd(h1_ref[...], gpre_ref[...], dhn2)
        dh1_ref[...] = dh2_ + dx
        _acc(dgpre_ref, _colsum8(dgq))
        _acc(dgpost_ref, _colsum8(dgp))

    return pl.pallas_call(
        body, name="mlp_bwd", grid=(lp // MLPB,),
        in_specs=[_rowspec(MLPB, D_MODEL)] * 4 + [ANY, ANY] + [_fullspec((1, D_MODEL))] * 2,
        out_specs=[_rowspec(MLPB, D_MODEL), _rowspec(MLPB, D_FF), _rowspec(MLPB, D_FF), _rowspec(MLPB, D_MODEL),
                   _fullspec((8, D_MODEL)), _fullspec((8, D_MODEL))],
        out_shape=[_sds((lp, D_MODEL), F32), _sds((lp, D_FF), BF16), _sds((lp, D_FF), BF16), _sds((lp, D_MODEL), BF16),
                   _sds((8, D_MODEL), F32), _sds((8, D_MODEL), F32)],
        scratch_shapes=[pltpu.VMEM((D_MODEL, D_FF), BF16), pltpu.VMEM((D_FF, D_MODEL), BF16), pltpu.SemaphoreType.DMA((2,))],
        compiler_params=_cp(("arbitrary",)),
    )(dh2, f, h1, hn2, w_up, w_down, g_pre, g_post)


def _out_bwd(dh1, mix, w_out, g_post):
    lp = dh1.shape[0]

    def body(dh1_ref, mix_ref, w_ref, g_ref, dmix_ref, datt_ref, dssm_ref, dg_ref):
        dmix, dg = _rms_bwd(mix_ref[...], g_ref[...], dh1_ref[...])
        dmb = dmix.astype(BF16)
        dmix_ref[...] = dmb
        datt_ref[...] = _dot(dmb, w_ref[pl.ds(0, 1024), :], NT).astype(BF16)
        dssm_ref[...] = _dot(dmb, w_ref[pl.ds(1024, 1024), :], NT)
        _acc(dg_ref, _colsum8(dg))

    return pl.pallas_call(
        body, name="out_bwd", grid=(lp // ROWB,),
        in_specs=[_rowspec(ROWB, D_MODEL)] * 2 + [_fullspec((2048, D_MODEL)), _fullspec((1, D_MODEL))],
        out_specs=[_rowspec(ROWB, D_MODEL)] * 3 + [_fullspec((8, D_MODEL))],
        out_shape=[_sds((lp, D_MODEL), BF16), _sds((lp, 1024), BF16), _sds((lp, 1024), F32), _sds((8, D_MODEL), F32)],
        compiler_params=_cp(("arbitrary",)),
    )(dh1, mix, w_out, g_post)


def _ssd_post_bwd(dssm, y, xact, z, dskip, gnorm):
    lp = y.shape[0]

    def body(do_ref, y_ref, x_ref, z_ref, d_ref, g_ref, dy_ref, dz_ref, dg_ref, dd_ref):
        z_, x_ = z_ref[...], x_ref[...]
        sg = _sigmoid(z_)
        sz = z_ * sg
        y2 = y_ref[...] + d_ref[...] * x_
        gt = y2 * sz
        r = lax.rsqrt(_group_mean(gt * gt) + EPS)
        gh = gt * r
        do = do_ref[...]
        dgh = do * g_ref[...]
        dgt = r * (dgh - gh * _group_mean(dgh * gh))
        dy2 = dgt * sz
        dy_ref[...] = dy2
        dz_ref[...] = (dgt * y2 * (sg * (1.0 + z_ * (1.0 - sg)))).astype(BF16)
        _acc(dg_ref, _colsum8(do * gh))
        _acc(dd_ref, _colsum8(dy2 * x_))

    return pl.pallas_call(
        body, name="ssd_post_bwd", grid=(lp // ROWB,),
        in_specs=[_rowspec(ROWB, SSM_WIDTH)] * 4 + [_fullspec((1, SSM_WIDTH))] * 2,
        out_specs=[_rowspec(ROWB, SSM_WIDTH)] * 2 + [_fullspec((8, SSM_WIDTH))] * 2,
        out_shape=[_sds((lp, SSM_WIDTH), F32), _sds((lp, SSM_WIDTH), BF16), _sds((8, SSM_WIDTH), F32), _sds((8, SSM_WIDTH), F32)],
        compiler_params=_cp(("arbitrary",)),
    )(dssm, y, xact, z, dskip, gnorm)


def _ssd_bwd(dy, xact, dtr, hprev, dt_bias, a_log, dskip, seq_rows):
    lp = xact.shape[0]
    nc = lp // CHUNK
    gw = SSM_WIDTH // SSM_GROUPS
    hpg = SSM_HEADS // SSM_GROUPS
    nb = SSM_WIDTH // (2 * SSM_STATE)

    def body(dy_ref, x_ref, b_ref, c_ref, dtr_ref, hp_ref, bias_ref, alog_ref, dsk_ref, tri_ref, ex_ref, ext_ref,
             dact_ref, ddtr_ref, da_ref, dbias_ref, dh_s):
        step = pl.program_id(0)
        c = nc - 1 - step

        @pl.when(step == 0)
        def _():
            dh_s[...] = jnp.zeros_like(dh_s)

        tri = tri_ref[...]
        ex = ex_ref[...]
        dt, a, acol, valid, dtr_ = _ssd_prep(dtr_ref, bias_ref, alog_ref, tri, c, seq_rows)
        arow = acol.T
        dtrow = dt.T
        alast = acol[CHUNK - 1:CHUNK, :]
        e_all = _dot(jnp.exp(acol), ex, prec=HI)
        wgt0 = jnp.exp(alast - acol)
        wgt = wgt0 * dt
        wx_all = _dot(wgt, ex, prec=HI)
        elast = jnp.exp(alast)
        dec_all = _dot(jnp.broadcast_to(elast, (8, LANE)), ex, prec=HI)[0:1, :]
        causal = tri > 0.5
        upper = tri.T > 0.5
        lane_id = lax.broadcasted_iota(jnp.int32, (1, LANE), 1)
        sub_id = lax.broadcasted_iota(jnp.int32, (CHUNK, 1), 0)
        dacol = jnp.zeros((CHUNK, LANE), F32)
        darowf = jnp.zeros((CHUNK, LANE), F32)
        ddtrowf = jnp.zeros((CHUNK, LANE), F32)
        dwgt = jnp.zeros((CHUNK, LANE), F32)
        delast = jnp.zeros((1, LANE), F32)
        for g in range(SSM_GROUPS):
            gs = slice(g * gw, (g + 1) * gw)
            ext_g = ext_ref[pl.ds(g * gw, gw), :]
            bg = b_ref[:, g * SSM_STATE:(g + 1) * SSM_STATE]
            cg = c_ref[:, g * SSM_STATE:(g + 1) * SSM_STATE]
            bgb, cgb = bg.astype(BF16), cg.astype(BF16)
            xg = x_ref[:, gs]
            dyg = dy_ref[:, gs]
            hg = hp_ref[0, :, gs]
            dhg = dh_s[:, gs]
            hgb, dhgb = hg.astype(BF16), dhg.astype(BF16)
            gm = _dot(cgb, bgb, NT)
            gmt = _dot(bgb, cgb, NT)
            y_off = _dot(cgb, hgb) * e_all[:, gs]
            dy0 = (dyg * e_all[:, gs]).astype(BF16)
            dcg = _dot(dy0, hgb, NT)
            dh_in = _dot(cg.T.astype(BF16), dy0) + dhg * dec_all[:, gs]
            dacol = dacol + _dot(dyg * y_off, ext_g, prec=HI)
            xw = xg * wx_all[:, gs]
            dxw = _dot(bgb, dhgb)
            dx_state = dxw * wx_all[:, gs]
            dwgt = dwgt + _dot(dxw * xg, ext_g, prec=HI)
            dbt = _dot(dhgb, xw.astype(BF16), NT)
            delast = delast + jnp.sum(_dot(_colsum8(dhg * hg), ext_g, prec=HI), axis=0, keepdims=True)
            dgm = jnp.zeros((CHUNK, CHUNK), F32)
            for r in range(hpg):
                hd = g * hpg + r
                cs = slice(r * SSM_HEAD_DIM, (r + 1) * SSM_HEAD_DIM)
                acol_r, arow_r = acol[:, hd:hd + 1], arow[hd:hd + 1, :]
                dtrow_r, dtcol_r = dtrow[hd:hd + 1, :], dt[:, hd:hd + 1]
                lm = jnp.where(causal, jnp.exp(jnp.where(causal, acol_r - arow_r, 0.0)), 0.0)
                lmt = jnp.where(upper, jnp.exp(jnp.where(upper, arow_r - acol_r, 0.0)), 0.0)
                wt = gmt * lmt * dtcol_r
                dy_r = dyg[:, cs].astype(BF16)
                dx_r = _dot(wt.astype(BF16), dy_r)
                dw = _dot(dy_r, xg[:, cs].astype(BF16), NT)
                t1 = dw * lm
                dgm = dgm + t1 * dtrow_r
                q1 = t1 * gm
                m = q1 * dtrow_r
                dacol = dacol + jnp.sum(m, axis=1, keepdims=True) * (lane_id == hd).astype(F32)
                darowf = darowf - (sub_id == hd).astype(F32) * jnp.sum(m, axis=0, keepdims=True)
                ddtrowf = ddtrowf + (sub_id == hd).astype(F32) * jnp.sum(q1, axis=0, keepdims=True)
                dact_ref[:, pl.ds(hd * SSM_HEAD_DIM, SSM_HEAD_DIM)] = (
                    dx_r + dx_state[:, cs] + dyg[:, cs] * dsk_ref[:, pl.ds(hd * SSM_HEAD_DIM, SSM_HEAD_DIM)])
            dgmb = dgm.astype(BF16)
            dact_ref[:, pl.ds(SSM_WIDTH + g * SSM_STATE, SSM_STATE)] = dbt.T + _dot(dgm.T.astype(BF16), cgb)
            dact_ref[:, pl.ds(SSM_WIDTH + 2 * SSM_STATE + g * SSM_STATE, SSM_STATE)] = dcg + _dot(dgmb, bgb)
            dh_s[:, gs] = dh_in
        t = dwgt * wgt
        dalast = jnp.sum(t, axis=0, keepdims=True) + delast * elast
        dacol_tot = dacol - t + darowf.T + (sub_id == CHUNK - 1).astype(F32) * dalast
        dda = _dot(tri.T, dacol_tot, prec=HI)
        ddt = dwgt * wgt0 + ddtrowf.T + dda * a
        ddtr = jnp.where(valid, ddt * _sigmoid(dtr_), 0.0)
        ddtr_ref[...] = ddtr
        _acc(da_ref, _colsum8(dda * dt) * a)
        _acc(dbias_ref, _colsum8(ddtr))

    rev = lambda c: nc - 1 - c
    xs_spec = pl.BlockSpec((CHUNK, SSM_WIDTH), lambda c: (rev(c), 0))
    return pl.pallas_call(
        body, name="ssd_bwd", grid=(nc,),
        in_specs=[xs_spec, xs_spec,
                  pl.BlockSpec((CHUNK, 2 * SSM_STATE), lambda c: (rev(c), nb)),
                  pl.BlockSpec((CHUNK, 2 * SSM_STATE), lambda c: (rev(c), nb + 1)),
                  pl.BlockSpec((CHUNK, LANE), lambda c: (rev(c), 0)),
                  pl.BlockSpec((1, SSM_STATE, SSM_WIDTH), lambda c: (rev(c), 0, 0)),
                  _fullspec((1, LANE)), _fullspec((1, LANE)), _fullspec((1, SSM_WIDTH)),
                  _fullspec((CHUNK, CHUNK)), _fullspec((LANE, SSM_WIDTH)), _fullspec((SSM_WIDTH, LANE))],
        out_specs=[pl.BlockSpec((CHUNK, CONV_DIM), lambda c: (rev(c), 0)), pl.BlockSpec((CHUNK, LANE), lambda c: (rev(c), 0)),
                   _fullspec((8, LANE)), _fullspec((8, LANE))],
        out_shape=[_sds((lp, CONV_DIM), F32), _sds((lp, LANE), F32), _sds((8, LANE), F32), _sds((8, LANE), F32)],
        scratch_shapes=[pltpu.VMEM((SSM_STATE, SSM_WIDTH), F32)],
        compiler_params=_cp(("arbitrary",)),
    )(dy, xact, xact, xact, dtr, hprev, dt_bias, a_log, dskip, _tri_mat(), _expand_mat(), _expand_mat().T)


def _conv_bwd(dact, xbc, cw, cb):
    lp, c = xbc.shape
    t8 = ROWB // 8
    nb = lp // ROWB
    ext = ROWB + 8

    def body(d_ref, dnext_ref, x_ref, prev_ref, next_ref, w_ref, b_ref, dx_ref, dw_ref, db_ref, xb, dp):
        i = pl.program_id(0)
        last = i == nb - 1
        xb[pl.ds(0, 8), :] = jnp.where(i > 0, prev_ref[...], 0.0)
        xb[pl.ds(8, ROWB), :] = x_ref[...]
        xb[pl.ds(8 + ROWB, 8), :] = jnp.where(last, 0.0, next_ref[...])
        w = w_ref[...]
        pre = b_ref[...] + sum(w[kk:kk + 1, :] * xb[pl.ds(8 - (CONV_K - 1) + kk, ext), :] for kk in range(CONV_K))
        sg = _sigmoid(pre)
        dsilu = sg * (1.0 + pre * (1.0 - sg))
        dp[pl.ds(0, ROWB), :] = d_ref[...] * dsilu[0:ROWB]
        dp[pl.ds(ROWB, 8), :] = jnp.where(last, 0.0, dnext_ref[...]) * dsilu[ROWB:ext]
        dx_ref[...] = sum(w[kk:kk + 1, :] * dp[pl.ds(CONV_K - 1 - kk, ROWB), :] for kk in range(CONV_K)).astype(BF16)
        dpre = dp[pl.ds(0, ROWB), :]
        sub = lax.broadcasted_iota(jnp.int32, (8, 1), 0)
        dwv = jnp.zeros((8, c), F32)
        for kk in range(CONV_K):
            part = jnp.sum(_colsum8(dpre * xb[pl.ds(8 - (CONV_K - 1) + kk, ROWB), :]), axis=0, keepdims=True)
            dwv = dwv + jnp.where(sub == kk, part, 0.0)
        _acc(dw_ref, dwv)
        _acc(db_ref, _colsum8(dpre))

    nxt = lambda i: (jnp.minimum((i + 1) * t8, lp // 8 - 1), 0)
    prv = lambda i: (jnp.maximum(i * t8 - 1, 0), 0)
    return pl.pallas_call(
        body, name="conv_bwd", grid=(nb,),
        in_specs=[_rowspec(ROWB, c), pl.BlockSpec((8, c), nxt), _rowspec(ROWB, c), pl.BlockSpec((8, c), prv),
                  pl.BlockSpec((8, c), nxt), _fullspec((8, c)), _fullspec((1, c))],
        out_specs=[_rowspec(ROWB, c), _fullspec((8, c)), _fullspec((8, c))],
        out_shape=[_sds((lp, c), BF16), _sds((8, c), F32), _sds((8, c), F32)],
        scratch_shapes=[pltpu.VMEM((ROWB + 16, c), F32), pltpu.VMEM((ROWB + 8, c), F32)],
        compiler_params=_cp(("arbitrary",)),
    )(dact, dact, xbc, xbc, xbc, cw, cb)


def _attn_delta(datt, att):
    lp = att.shape[0]

    def body(do_ref, o_ref, d_ref):
        d = jnp.sum(do_ref[...].astype(F32) * o_ref[...].astype(F32), axis=1, keepdims=True)
        d_ref[0] = jnp.broadcast_to(d, (ROWB, LANE))

    blk = pl.BlockSpec((ROWB, V_HEAD), lambda h, i: (i, h))
    return pl.pallas_call(
        body, name="attn_delta", grid=(ATT_HEADS, lp // ROWB), in_specs=[blk, blk],
        out_specs=pl.BlockSpec((1, ROWB, LANE), lambda h, i: (h, i, 0)),
        out_shape=_sds((ATT_HEADS, lp, LANE), F32), compiler_params=_cp(("arbitrary", "arbitrary")),
    )(datt, att)


def _flash_dq(q, k, v, datt, lse, delta):
    lp = q.shape[0]
    nq = lp // ROWB

    def body(q_ref, do_ref, lse_ref, dl_ref, k_ref, v_ref, dq_ref, acc):
        i = pl.program_id(1)
        qb, dob = q_ref[...], do_ref[...]
        rep = KVB // LANE
        lse_c, dl_c = _lanes(lse_ref[0], rep), _lanes(dl_ref[0], rep)
        acc[...] = jnp.zeros_like(acc)

        def tile(j, masked):
            off = pl.multiple_of(j * KVB, KVB)
            kb = k_ref[pl.ds(off, KVB), :]
            vb = v_ref[pl.ds(off, KVB), :]
            s = _dot(qb, kb, NT)
            if masked:
                qrow = i * ROWB + lax.broadcasted_iota(jnp.int32, s.shape, 0)
                krow = j * KVB + lax.broadcasted_iota(jnp.int32, s.shape, 1)
                s = jnp.where(_att_ok(qrow, krow), s, NEG)
            p = jnp.exp2(s - lse_c)
            dp = _dot(dob, vb, NT)
            ds = p * (dp - dl_c)
            acc[...] += _dot(ds.astype(BF16), kb)

        tile(0, True)
        _pair_loop(1, i, lambda j: tile(j, False), 4)

        @pl.when(i > 0)
        def _():
            tile(i, True)

        dq_ref[...] = acc[...] * ATT_SCALE

    stat = pl.BlockSpec((1, ROWB, LANE), lambda h, i: (h, i, 0))
    return pl.pallas_call(
        body, name="flash_dq", grid=(ATT_HEADS, nq),
        in_specs=[pl.BlockSpec((ROWB, HEADW), lambda h, i: (i, h)), pl.BlockSpec((ROWB, V_HEAD), lambda h, i: (i, h)),
                  stat, stat,
                  pl.BlockSpec((lp, HEADW), lambda h, i: (0, h)), pl.BlockSpec((lp, V_HEAD), lambda h, i: (0, 2 * h))],
        out_specs=pl.BlockSpec((ROWB, HEADW), lambda h, i: (i, h)),
        out_shape=_sds((lp, ATT_HEADS * HEADW), F32),
        scratch_shapes=[pltpu.VMEM((ROWB, HEADW), F32)],
        compiler_params=_cp(("arbitrary", "arbitrary")),
    )(q, datt, lse, delta, k, v)


def _flash_dkv(q, k, v, datt, lse_row, delta_row):
    lp = q.shape[0]
    nk = lp // ROWB

    def body(k_ref, v_ref, q_ref, do_ref, lse_ref, dl_ref, dk_ref, dv_ref, dk_acc, dv_acc):
        j = pl.program_id(1)
        kb, vb = k_ref[...], v_ref[...]
        dk_acc[...] = jnp.zeros_like(dk_acc)
        dv_acc[...] = jnp.zeros_like(dv_acc)

        def tile(i, masked):
            off = pl.multiple_of(i * ROWB, ROWB)
            qb = q_ref[pl.ds(off, ROWB), :]
            dob = do_ref[pl.ds(off, ROWB), :]
            lse_r = lse_ref[0, :, pl.ds(off, ROWB)]
            dl_r = dl_ref[0, :, pl.ds(off, ROWB)]
            st = _dot(kb, qb, NT)
            if masked:
                krow = j * ROWB + lax.broadcasted_iota(jnp.int32, st.shape, 0)
                qrow = i * ROWB + lax.broadcasted_iota(jnp.int32, st.shape, 1)
                st = jnp.where(_att_ok(qrow, krow), st, NEG)
            pt = jnp.exp2(st - lse_r)
            dv_acc[...] += _dot(pt.astype(BF16), dob)
            dpt = _dot(vb, dob, NT)
            dst = pt * (dpt - dl_r)
            dk_acc[...] += _dot(dst.astype(BF16), qb)

        tile(j, True)

        @pl.when(j == 0)
        def _():
            _pair_loop(1, nk, lambda i: tile(i, True), 2)

        @pl.when(j > 0)
        def _():
            _pair_loop(j + 1, nk, lambda i: tile(i, False), 4)

        dk_ref[...] = dk_acc[...] * LN2
        dv_ref[...] = dv_acc[...].astype(BF16)

    stat = pl.BlockSpec((1, 1, lp), lambda h, j: (h, 0, 0))
    return pl.pallas_call(
        body, name="flash_dkv", grid=(ATT_HEADS, nk),
        in_specs=[pl.BlockSpec((ROWB, HEADW), lambda h, j: (j, h)), pl.BlockSpec((ROWB, V_HEAD), lambda h, j: (j, 2 * h)),
                  pl.BlockSpec((lp, HEADW), lambda h, j: (0, h)), pl.BlockSpec((lp, V_HEAD), lambda h, j: (0, h)),
                  stat, stat],
        out_specs=[pl.BlockSpec((ROWB, HEADW), lambda h, j: (j, h)), pl.BlockSpec((ROWB, V_HEAD), lambda h, j: (j, h))],
        out_shape=[_sds((lp, ATT_HEADS * HEADW), F32), _sds((lp, ATT_HEADS * V_HEAD), BF16)],
        scratch_shapes=[pltpu.VMEM((ROWB, HEADW), F32), pltpu.VMEM((ROWB, V_HEAD), F32)],
        compiler_params=_cp(("arbitrary", "arbitrary")),
    )(k, v, q, datt, lse_row, delta_row)


def _qkv_bwd(dq, dk, dv, cq, ckv, cos, sa, sb, gq, gkv, wq, wk, wv):
    lp = cq.shape[0]
    qw = ATT_HEADS * HEADW

    def body(dq_ref, dk_ref, dv_ref, cq_ref, ckv_ref, cos_ref, sa_ref, sb_ref, gq_ref, gkv_ref, wq_ref, wk_ref, wv_ref,
             dcq_ref, dckv_ref, dkr_ref, dqp_ref, dkb_ref, dgq_ref, dgkv_ref):
        cos_, sa_, sb_ = cos_ref[...], sa_ref[...], sb_ref[...]
        dqp = _rope_t(dq_ref[...], _tile8(cos_), _tile8(sa_), _tile8(sb_)).astype(BF16)
        dqp_ref[...] = dqp
        dcq, dgq = _rms_bwd(cq_ref[...], gq_ref[...], _dot(dqp, wq_ref[...], NT))
        dcq_ref[...] = dcq.astype(BF16)
        dk_ = dk_ref[...]
        dkb = dk_.astype(BF16)
        dkb_ref[...] = dkb
        dksum = sum(dk_[:, hh * HEADW:(hh + 1) * HEADW] for hh in range(ATT_HEADS))
        dkr_ref[...] = _rope_t(dksum, cos_, sa_, sb_).astype(BF16)
        dckvn = _dot(dkb, wk_ref[...], NT) + _dot(dv_ref[...], wv_ref[...], NT)
        dckv, dgkv = _rms_bwd(ckv_ref[...], gkv_ref[...], dckvn)
        dckv_ref[...] = dckv.astype(BF16)
        _acc(dgq_ref, _colsum8(dgq))
        _acc(dgkv_ref, _colsum8(dgkv))

    return pl.pallas_call(
        body, name="qkv_bwd", grid=(lp // ROWB,),
        in_specs=[_rowspec(ROWB, qw), _rowspec(ROWB, qw), _rowspec(ROWB, ATT_HEADS * V_HEAD),
                  _rowspec(ROWB, Q_LORA), _rowspec(ROWB, KV_LORA)] + [_rowspec(ROWB, HEADW)] * 3
        + [_fullspec((1, Q_LORA)), _fullspec((1, KV_LORA)), _fullspec((Q_LORA, qw)), _fullspec((KV_LORA, qw)),
           _fullspec((KV_LORA, ATT_HEADS * V_HEAD))],
        out_specs=[_rowspec(ROWB, Q_LORA), _rowspec(ROWB, KV_LORA), _rowspec(ROWB, HEADW), _rowspec(ROWB, qw),
                   _rowspec(ROWB, qw), _fullspec((8, Q_LORA)), _fullspec((8, KV_LORA))],
        out_shape=[_sds((lp, Q_LORA), BF16), _sds((lp, KV_LORA), BF16), _sds((lp, HEADW), BF16), _sds((lp, qw), BF16),
                   _sds((lp, qw), BF16), _sds((8, Q_LORA), F32), _sds((8, KV_LORA), F32)],
        compiler_params=_cp(("arbitrary",)),
    )(dq, dk, dv, cq, ckv, cos, sa, sb, gq, gkv, wq, wk, wv)


def _in_bwd(dproj, h, dh1, g, w_all):
    lp = h.shape[0]

    def body(dp_ref, h_ref, dh1_ref, g_ref, w_ref, dh_ref, dg_ref):
        dx, dg = _rms_bwd(h_ref[...], g_ref[...], _dot(dp_ref[...], w_ref[...], NT))
        dh_ref[...] = dh1_ref[...] + dx
        _acc(dg_ref, _colsum8(dg))

    return pl.pallas_call(
        body, name="in_bwd", grid=(lp // ROWB,),
        in_specs=[_rowspec(ROWB, PROJ_W), _rowspec(ROWB, D_MODEL), _rowspec(ROWB, D_MODEL), _fullspec((1, D_MODEL)),
                  _fullspec((D_MODEL, PROJ_W))],
        out_specs=[_rowspec(ROWB, D_MODEL), _fullspec((8, D_MODEL))],
        out_shape=[_sds((lp, D_MODEL), F32), _sds((8, D_MODEL), F32)],
        compiler_params=_cp(("arbitrary",)),
    )(dproj, h, dh1, g, w_all)


def _tile_of(n, cap=1024):
    return max(t for t in range(LANE, min(n, cap) + 1, LANE) if n % t == 0)


def _matmul_tn(name, a, b):
    rows, kd = a.shape
    nd = b.shape[1]
    tk, tn = _tile_of(kd), _tile_of(nd)

    def body(a_ref, b_ref, o_ref):
        @pl.when(pl.program_id(2) == 0)
        def _():
            o_ref[...] = jnp.zeros_like(o_ref)

        o_ref[...] += _dot(a_ref[...], b_ref[...], ((0,), (0,)))

    return pl.pallas_call(
        body, name=name, grid=(kd // tk, nd // tn, rows // ROWB),
        in_specs=[pl.BlockSpec((ROWB, tk), lambda i, j, r: (r, i)), pl.BlockSpec((ROWB, tn), lambda i, j, r: (r, j))],
        out_specs=pl.BlockSpec((tk, tn), lambda i, j, r: (i, j)), out_shape=_sds((kd, nd), F32),
        compiler_params=_cp(("arbitrary", "arbitrary", "arbitrary")),
    )(a, b)


def _local_backward(h, f, p):
    g = {}
    row = lambda v: _row1(v)
    s8 = lambda v: jnp.sum(v, axis=0)
    dh1, du, a_, df, dgpre, dgpost = _mlp_bwd(f["dh2"], f["f"], f["h1"], f["hn2"], p["w_mlp_up"], p["w_mlp_down"],
                                              row(p["norm_mlp_pre"]), row(p["norm_mlp_post"]))
    g["norm_mlp_pre"], g["norm_mlp_post"] = s8(dgpre), s8(dgpost)
    g["w_mlp_up"] = _matmul_tn("dw_mlp_up", f["hn2"], du)
    g["w_mlp_down"] = _matmul_tn("dw_mlp_down", a_, df)
    dmix, datt, dssm, dgmp = _out_bwd(dh1, f["mix"], p["w_out"], row(p["norm_mix_post"]))
    g["norm_mix_post"] = s8(dgmp)
    g["w_out"] = jnp.concatenate([_matmul_tn("dw_out_att", f["att"], dmix), _matmul_tn("dw_out_ssm", f["ssm"], dmix)], axis=0)
    dy, dz, dgn, dd = _ssd_post_bwd(dssm, f["y"], f["xact"], f["z"], f["dskip"], row(p["ssm_norm"]))
    g["ssm_norm"] = s8(dgn)
    g["d_skip"] = s8(dd).reshape(SSM_HEADS, SSM_HEAD_DIM).sum(axis=1)
    dact, ddtr, da8, dbias8 = _ssd_bwd(dy, f["xact"], f["dtr"], f["hprev"], f["dt_bias"], f["a_log"], f["dskip"], f["seq_rows"])
    g["a_log"], g["dt_bias"] = s8(da8)[:SSM_HEADS], s8(dbias8)[:SSM_HEADS]
    dxbc, dcw8, dcb8 = _conv_bwd(dact, f["xbc"], f["cw"], row(p["conv_b"]))
    g["conv_w"], g["conv_b"] = dcw8[:CONV_K], s8(dcb8)
    delta = _attn_delta(datt, f["att"])
    lp = h.shape[0]
    to_row = lambda t: t[:, :, 0].reshape(ATT_HEADS, 1, lp)
    dq = _flash_dq(f["q"], f["k"], f["v"], datt, f["lse"], delta)
    dk, dv = _flash_dkv(f["q"], f["k"], f["v"], datt, to_row(f["lse"]), to_row(delta))
    dcq, dckv, dkr, dqp, dkb, dgq, dgkv = _qkv_bwd(dq, dk, dv, f["cq"], f["ckv"], *f["rope"], row(p["q_a_norm"]),
                                                   row(p["kv_a_norm"]), f["wq"], f["wk"], f["wv"])
    g["q_a_norm"], g["kv_a_norm"] = s8(dgq), s8(dgkv)
    dwq = _matmul_tn("dw_q_up", f["cqn"], dqp).reshape(Q_LORA, ATT_HEADS, HEADW)
    g["w_q_up"] = dwq[:, :, :QK_NOPE + QK_ROPE].reshape(Q_LORA, -1)
    dwk = _matmul_tn("dw_k_up", f["ckvn"], dkb).reshape(KV_LORA, ATT_HEADS, HEADW)[:, :, :QK_NOPE]
    dwv = _matmul_tn("dw_v_up", f["ckvn"], dv).reshape(KV_LORA, ATT_HEADS, V_HEAD)
    g["w_kv_up"] = jnp.concatenate([dwk, dwv], axis=2).reshape(KV_LORA, -1)
    dproj = jnp.concatenate([dcq, dckv, dkr, dz, dxbc, ddtr.astype(BF16)], axis=1)
    dh, dgin = _in_bwd(dproj, h, dh1, row(p["norm_mix_pre"]), f["w_all"])
    g["norm_mix_pre"] = s8(dgin)
    dwa = _matmul_tn("dw_in", f["hn"], dproj)
    g["w_in"] = jnp.concatenate([dwa[:, PC_Q:PC_KR], dwa[:, PC_KR + QK_NOPE:PC_KR + QK_NOPE + QK_ROPE],
                                 dwa[:, PC_Z:PC_DT + SSM_HEADS]], axis=1)
    return dh, g


BIG = {"w_in": ((D_MODEL, IN_WIDTH), 1), "w_q_up": ((Q_LORA, ATT_HEADS * (QK_NOPE + QK_ROPE)), 1),
       "w_kv_up": ((KV_LORA, ATT_HEADS * (QK_NOPE + V_HEAD)), 1), "w_out": ((2 * D_MODEL, D_MODEL), 0),
       "w_mlp_up": ((D_MODEL, D_FF), 1), "w_mlp_down": ((D_FF, D_MODEL), 0), "conv_w": ((CONV_K, CONV_DIM), 1),
       "meta_tokens": ((N_META, D_MODEL), 1)}
SMALL = {"norm_mix_pre": D_MODEL, "q_a_norm": Q_LORA, "kv_a_norm": KV_LORA, "conv_b": CONV_DIM, "dt_bias": SSM_HEADS,
         "a_log": SSM_HEADS, "d_skip": SSM_HEADS, "ssm_norm": SSM_WIDTH, "norm_mix_post": D_MODEL,
         "norm_mlp_pre": D_MODEL, "norm_mlp_post": D_MODEL}
WEIGHT_ORDER = ("meta_tokens", "norm_mix_pre", "w_in", "q_a_norm", "w_q_up", "kv_a_norm", "w_kv_up", "conv_w", "conv_b",
                "dt_bias", "a_log", "d_skip", "ssm_norm", "w_out", "norm_mix_post", "norm_mlp_pre", "w_mlp_up",
                "w_mlp_down", "norm_mlp_post")
PACK_BLOCK = 512


def _shard_shape(name):
    shape, ax = BIG[name]
    return tuple(d // N_DEV if a == ax else d for a, d in enumerate(shape))


BIG_ROWS = -(-sum(math.prod(_shard_shape(n)) for n in BIG) // (LANE * PACK_BLOCK)) * PACK_BLOCK
SMALL_ROWS = -(-sum(SMALL.values()) // (LANE * 8)) * 8


def _pack(flats, rows):
    v = jnp.concatenate([f.reshape(-1) for f in flats])
    return jnp.pad(v, (0, rows * LANE - v.shape[0])).reshape(rows, LANE)


def _unpack(packed, shapes):
    v = packed.reshape(-1)
    out, o = [], 0
    for s in shapes:
        n = math.prod(s)
        out.append(v[o:o + n].reshape(s))
        o += n
    return out


def _to_chunks(name, full):
    shape, ax = BIG[name]
    if ax == 0:
        return full.reshape(N_DEV, -1)
    k, n = shape
    return full.reshape(k, N_DEV, n // N_DEV).transpose(1, 0, 2).reshape(N_DEV, -1)


def _from_shards(name, shards):
    shape, ax = BIG[name]
    if ax == 0:
        return shards.reshape(shape)
    return shards.transpose(1, 0, 2).reshape(shape)


def _peer(k):
    x, y, c = lax.axis_index("x"), lax.axis_index("y"), lax.axis_index("c")
    px = 1 - x if k & 4 else x
    py = 1 - y if k & 2 else y
    pc = 1 - c if k & 1 else c
    return (px, py, pc), 4 * px + 2 * py + pc


def _all_gather(shard):
    rows = shard.shape[0]

    def body(x_ref, out_ref, send_sems, recv_sems, local_sem):
        _, me = _peer(0)
        sibling, _ = _peer(1)
        chips = (4, 2, 6)

        def copy(sem, block, to, src=None):
            return pltpu.make_async_remote_copy(
                src_ref=out_ref.at[block] if src is None else src, dst_ref=out_ref.at[block],
                send_sem=send_sems.at[sem], recv_sem=recv_sems.at[sem], device_id=to, device_id_type=MESH)

        mine = pltpu.make_async_copy(x_ref, out_ref.at[me], local_sem)
        mine.start()
        first = [copy(0, me, sibling, src=x_ref)]
        first += [copy(1 + n, me, _peer(k)[0], src=x_ref) for n, k in enumerate(chips)]
        for cp in first:
            cp.start()
        passed = [copy(4 + n, _peer(k)[1], sibling) for n, k in enumerate(chips)]
        for n, k in enumerate(chips):
            copy(1 + n, _peer(k)[1], sibling).wait_recv()
            passed[n].start()
        copy(0, _peer(1)[1], sibling).wait_recv()
        for n, k in enumerate(chips):
            copy(4 + n, _peer(k | 1)[1], sibling).wait_recv()
        for cp in first + passed:
            cp.wait_send()
        mine.wait()

    return pl.pallas_call(
        body, name="all_gather_weights", out_shape=_sds((N_DEV, rows, LANE), shard.dtype),
        in_specs=[ANY], out_specs=ANY,
        scratch_shapes=[pltpu.SemaphoreType.DMA((7,)), pltpu.SemaphoreType.DMA((7,)), pltpu.SemaphoreType.DMA],
    )(shard)


def _exchange(big, small):
    def body(big_ref, small_ref, obig_ref, osmall_ref, send_sems, recv_sems, local_sems):
        _, me = _peer(0)
        own = [pltpu.make_async_copy(big_ref.at[me], obig_ref.at[me], local_sems.at[0]),
               pltpu.make_async_copy(small_ref, osmall_ref.at[me], local_sems.at[1])]
        for cp in own:
            cp.start()

        def copies(k):
            to, idx = _peer(k)
            kw = dict(device_id=to, device_id_type=MESH)
            return [pltpu.make_async_remote_copy(src_ref=big_ref.at[idx], dst_ref=obig_ref.at[me],
                                                 send_sem=send_sems.at[2 * k - 2], recv_sem=recv_sems.at[2 * k - 2], **kw),
                    pltpu.make_async_remote_copy(src_ref=small_ref, dst_ref=osmall_ref.at[me],
                                                 send_sem=send_sems.at[2 * k - 1], recv_sem=recv_sems.at[2 * k - 1], **kw)]

        def arrivals(k):
            to, idx = _peer(k)
            kw = dict(device_id=to, device_id_type=MESH)
            return [pltpu.make_async_remote_copy(src_ref=big_ref.at[me], dst_ref=obig_ref.at[idx],
                                                 send_sem=send_sems.at[2 * k - 2], recv_sem=recv_sems.at[2 * k - 2], **kw),
                    pltpu.make_async_remote_copy(src_ref=small_ref, dst_ref=osmall_ref.at[idx],
                                                 send_sem=send_sems.at[2 * k - 1], recv_sem=recv_sems.at[2 * k - 1], **kw)]

        sent = [cp for k in range(1, N_DEV) for cp in copies(k)]
        for cp in sent:
            cp.start()
        for k in range(1, N_DEV):
            for cp in arrivals(k):
                cp.wait_recv()
        for cp in sent:
            cp.wait_send()
        for cp in own:
            cp.wait()

    n = 2 * (N_DEV - 1)
    return pl.pallas_call(
        body, name="exchange_grads",
        out_shape=[_sds(big.shape, big.dtype), _sds((N_DEV,) + small.shape, small.dtype)],
        in_specs=[ANY, ANY], out_specs=[ANY, ANY],
        scratch_shapes=[pltpu.SemaphoreType.DMA((n,)), pltpu.SemaphoreType.DMA((n,)), pltpu.SemaphoreType.DMA((2,))],
    )(big, small)


def _reduce_adamw(recv, w, m, v):
    rows = w.shape[0]
    blk = min(PACK_BLOCK, rows)
    c1 = 1.0 - ADAM_B1 ** ADAM_STEP
    c2 = 1.0 - ADAM_B2 ** ADAM_STEP

    def body(r_ref, w_ref, m_ref, v_ref, g_ref, d_ref, nm_ref, nv_ref):
        g = r_ref[0].astype(F32)
        for s in range(1, N_DEV):
            g = g + r_ref[s].astype(F32)
        g_ref[...] = g
        m_ = ADAM_B1 * m_ref[...] + (1.0 - ADAM_B1) * g
        v_ = ADAM_B2 * v_ref[...] + (1.0 - ADAM_B2) * (g * g)
        nm_ref[...] = m_
        nv_ref[...] = v_
        d_ref[...] = -ADAM_LR * ((m_ / c1) / (jnp.sqrt(v_ / c2) + ADAM_EPS) + ADAM_WD * w_ref[...])

    spec = _rowspec(blk, LANE)
    return pl.pallas_call(
        body, name="reduce_adamw", grid=(rows // blk,),
        in_specs=[pl.BlockSpec((N_DEV, blk, LANE), lambda i: (0, i, 0)), spec, spec, spec],
        out_specs=[spec] * 4, out_shape=[_sds((rows, LANE), F32)] * 4,
        compiler_params=_cp(("arbitrary",)),
    )(recv, w, m, v)


def kernel(x, meta_tokens, norm_mix_pre, w_in, q_a_norm, w_q_up, kv_a_norm, w_kv_up, conv_w, conv_b, dt_bias, a_log, d_skip, ssm_norm, w_out, norm_mix_post, norm_mlp_pre, w_mlp_up, w_mlp_down, norm_mlp_post, loss_target, m_meta_tokens, m_norm_mix_pre, m_w_in, m_q_a_norm, m_w_q_up, m_kv_a_norm, m_w_kv_up, m_conv_w, m_conv_b, m_dt_bias, m_a_log, m_d_skip, m_ssm_norm, m_w_out, m_norm_mix_post, m_norm_mlp_pre, m_w_mlp_up, m_w_mlp_down, m_norm_mlp_post, v_meta_tokens, v_norm_mix_pre, v_w_in, v_q_a_norm, v_w_q_up, v_kv_a_norm, v_w_kv_up, v_conv_w, v_conv_b, v_dt_bias, v_a_log, v_d_skip, v_ssm_norm, v_w_out, v_norm_mix_post, v_norm_mlp_pre, v_w_mlp_up, v_w_mlp_down, v_norm_mlp_post):
    w = dict(meta_tokens=meta_tokens, norm_mix_pre=norm_mix_pre, w_in=w_in, q_a_norm=q_a_norm, w_q_up=w_q_up,
             kv_a_norm=kv_a_norm, w_kv_up=w_kv_up, conv_w=conv_w, conv_b=conv_b, dt_bias=dt_bias, a_log=a_log,
             d_skip=d_skip, ssm_norm=ssm_norm, w_out=w_out, norm_mix_post=norm_mix_post, norm_mlp_pre=norm_mlp_pre,
             w_mlp_up=w_mlp_up, w_mlp_down=w_mlp_down, norm_mlp_post=norm_mlp_post)
    m = dict(meta_tokens=m_meta_tokens, norm_mix_pre=m_norm_mix_pre, w_in=m_w_in, q_a_norm=m_q_a_norm, w_q_up=m_w_q_up,
             kv_a_norm=m_kv_a_norm, w_kv_up=m_w_kv_up, conv_w=m_conv_w, conv_b=m_conv_b, dt_bias=m_dt_bias,
             a_log=m_a_log, d_skip=m_d_skip, ssm_norm=m_ssm_norm, w_out=m_w_out, norm_mix_post=m_norm_mix_post,
             norm_mlp_pre=m_norm_mlp_pre, w_mlp_up=m_w_mlp_up, w_mlp_down=m_w_mlp_down, norm_mlp_post=m_norm_mlp_post)
    v = dict(meta_tokens=v_meta_tokens, norm_mix_pre=v_norm_mix_pre, w_in=v_w_in, q_a_norm=v_q_a_norm, w_q_up=v_w_q_up,
             kv_a_norm=v_kv_a_norm, w_kv_up=v_w_kv_up, conv_w=v_conv_w, conv_b=v_conv_b, dt_bias=v_dt_bias,
             a_log=v_a_log, d_skip=v_d_skip, ssm_norm=v_ssm_norm, w_out=v_w_out, norm_mix_post=v_norm_mix_post,
             norm_mlp_pre=v_norm_mlp_pre, w_mlp_up=v_w_mlp_up, w_mlp_down=v_w_mlp_down, norm_mlp_post=v_norm_mlp_post)
    seq = x.shape[1]
    seq_rows = N_META + seq
    lp = -(-(PADF + seq_rows) // ROWB) * ROWB
    big_names = [n for n in WEIGHT_ORDER if n in BIG]
    small_names = [n for n in WEIGHT_ORDER if n in SMALL]
    shard_shapes = [_shard_shape(n) for n in big_names]

    f32_names = ("conv_w", "meta_tokens")
    mm_names = [n for n in big_names if n not in f32_names]
    bits = [lax.bitcast_convert_type(w[n].astype(F32), BF16) for n in f32_names]
    payload = _pack([w[n].astype(BF16) for n in mm_names] + bits, BIG_ROWS)
    flat = _all_gather(payload).reshape(N_DEV, -1)
    p = {n: w[n].reshape(-1) for n in small_names}
    o = 0
    for n in mm_names:
        s = _shard_shape(n)
        p[n] = _from_shards(n, flat[:, o:o + math.prod(s)].reshape((N_DEV,) + s))
        o += math.prod(s)
    for n in f32_names:
        s = _shard_shape(n)
        shards = lax.bitcast_convert_type(flat[:, o:o + 2 * math.prod(s)].reshape((N_DEV,) + s + (2,)), F32)
        p[n] = _from_shards(n, shards)
        o += 2 * math.prod(s)
    meta_full = p["meta_tokens"]

    zeros = lambda r: jnp.zeros((r, D_MODEL), F32)
    h = jnp.concatenate([zeros(PADF), meta_full, x[0], zeros(lp - PADF - seq_rows)], axis=0)
    tgt = jnp.concatenate([zeros(PADF + N_META), loss_target[0], zeros(lp - PADF - seq_rows)], axis=0)

    f = _local_forward(h, tgt, p, seq_rows)
    dh, g = _local_backward(h, f, p)
    g["meta_tokens"] = dh[PADF:PADF + N_META]
    grad_x = dh[PADF + N_META:PADF + seq_rows][None]
    loss = lax.psum(f["loss"], ("x", "y", "c"))

    chunks = jnp.concatenate([_to_chunks(n, g[n]) for n in big_names], axis=1).astype(BF16)
    chunks = jnp.pad(chunks, ((0, 0), (0, BIG_ROWS * LANE - chunks.shape[1]))).reshape(N_DEV, BIG_ROWS, LANE)
    small = _pack([g[n] for n in small_names], SMALL_ROWS)
    recv_big, recv_small = _exchange(chunks, small)

    outs = {}
    for names, recv, rows, shapes in ((big_names, recv_big, BIG_ROWS, shard_shapes),
                                      (small_names, recv_small, SMALL_ROWS, [(SMALL[n],) for n in small_names])):
        packed = [_pack([d[n].astype(F32) for n in names], rows) for d in (w, m, v)]
        res = _reduce_adamw(recv, *packed)
        for kind, arr in zip(("grad", "delta", "new_m", "new_v"), res):
            for n, val in zip(names, _unpack(arr, shapes)):
                outs[kind, n] = val.reshape(w[n].shape)
    return (loss, grad_x) + tuple(outs[kind, n] for kind in ("grad", "delta", "new_m", "new_v") for n in WEIGHT_ORDER)
```

```python
import functools
import math

import jax
import jax.numpy as jnp
import numpy as np
from jax import lax
from jax.experimental import pallas as pl
from jax.experimental.pallas import tpu as pltpu

F32 = jnp.float32
BF16 = jnp.bfloat16

D_MODEL = 1024
N_META = 16
EPS = 1e-6
ATT_HEADS = 8
Q_LORA = 384
KV_LORA = 256
QK_NOPE = 128
QK_ROPE = 64
V_HEAD = 128
ROPE_THETA = 10000.0
SSM_HEADS = 16
SSM_HEAD_DIM = 64
SSM_WIDTH = 1024
SSM_GROUPS = 2
SSM_STATE = 128
CONV_K = 4
CHUNK = 128
CONV_DIM = 1536
D_FF = 4096
IN_SPLITS = (Q_LORA, KV_LORA, QK_ROPE, SSM_WIDTH, CONV_DIM, SSM_HEADS)
IN_WIDTH = sum(IN_SPLITS)
ADAM_LR, ADAM_B1, ADAM_B2, ADAM_EPS, ADAM_WD, ADAM_STEP = 0.001, 0.9, 0.999, 1e-08, 0.01, 10

LANE = 128
ROWB = 512
PADF = ROWB - N_META
HEADW = 256
PC_Q, PC_KV, PC_KR, PC_Z, PC_XBC, PC_DT, PROJ_W = 0, 384, 640, 896, 1920, 3456, 3584
NEG = -1e30
N_DEV = 8
VMEM_LIMIT = 56 * 1024 * 1024
MESH = pl.DeviceIdType.MESH


def _cp(sem, vmem=VMEM_LIMIT, **kw):
    return pltpu.CompilerParams(dimension_semantics=sem, vmem_limit_bytes=vmem, **kw)


def _dot(a, b, dims=((1,), (0,)), prec=None):
    return lax.dot_general(a, b, (dims, ((), ())), preferred_element_type=F32, precision=prec)


def _bdot(a, b, dims=((1,), (0,))):
    return _dot(a.astype(BF16), b.astype(BF16), dims)


NT = ((1,), (1,))
HI = lax.Precision.HIGHEST


def _rms_fwd(x, w):
    r = lax.rsqrt(jnp.mean(x * x, axis=-1, keepdims=True) + EPS)
    return (x * r) * w


def _rms_bwd(x, w, dy):
    r = lax.rsqrt(jnp.mean(x * x, axis=-1, keepdims=True) + EPS)
    xh = x * r
    g = dy * w
    dx = r * (g - xh * jnp.mean(g * xh, axis=-1, keepdims=True))
    return dx, dy * xh


def _sigmoid(x):
    return 1.0 / (1.0 + jnp.exp(-x))


def _colsum8(x):
    t, c = x.shape
    return jnp.sum(x.reshape(t // 8, 8, c), axis=0)


def _rowspec(t, c, cb=0):
    return pl.BlockSpec((t, c), lambda i: (i, cb))


def _fullspec(shape):
    n = len(shape)
    return pl.BlockSpec(shape, lambda i: (0,) * n)


def _sds(shape, dt):
    return jax.ShapeDtypeStruct(shape, dt)


def _acc(ref, val):
    @pl.when(pl.program_id(0) == 0)
    def _():
        ref[...] = jnp.zeros_like(ref)

    ref[...] += val


def _xspec():
    return pl.BlockSpec((ROWB, D_MODEL), lambda i: (jnp.maximum(i - 1, 0), 0))


def _h_block(head_ref, x_ref):
    return jnp.where(pl.program_id(0) == 0, head_ref[...], x_ref[...])


def _norm_in_proj(head, x, g, w_all):
    lp = head.shape[0] + x.shape[0]

    def body(head_ref, x_ref, g_ref, w_ref, hn_ref, cq_ref, ckv_ref, kr_ref, z_ref, xbc_ref, dt_ref):
        hn = _rms_fwd(_h_block(head_ref, x_ref), g_ref[...]).astype(BF16)
        hn_ref[...] = hn
        p = _dot(hn, w_ref[...])
        cq_ref[...] = p[:, PC_Q:PC_KV]
        ckv_ref[...] = p[:, PC_KV:PC_KR]
        kr_ref[...] = p[:, PC_KR:PC_Z]
        z_ref[...] = p[:, PC_Z:PC_XBC]
        xbc_ref[...] = p[:, PC_XBC:PC_DT]
        dt_ref[...] = p[:, PC_DT:PROJ_W]

    widths = (Q_LORA, KV_LORA, HEADW, SSM_WIDTH, CONV_DIM, LANE)
    return pl.pallas_call(
        body, name="norm_in_proj", grid=(lp // ROWB,),
        in_specs=[_fullspec((ROWB, D_MODEL)), _xspec(), _fullspec((1, D_MODEL)), _fullspec((D_MODEL, PROJ_W))],
        out_specs=[_rowspec(ROWB, D_MODEL)] + [_rowspec(ROWB, w) for w in widths],
        out_shape=[_sds((lp, D_MODEL), BF16)] + [_sds((lp, w), F32) for w in widths],
        compiler_params=_cp(("arbitrary",)),
    )(head, x, g, w_all)


def _rope(x, cos, sa, sb):
    w = x.shape[1]
    return x * cos + pltpu.roll(x, w - 32, 1) * sa + pltpu.roll(x, 32, 1) * sb


def _rope_t(dy, cos, sa, sb):
    w = dy.shape[1]
    return dy * cos + pltpu.roll(dy * sa, 32, 1) + pltpu.roll(dy * sb, w - 32, 1)


def _tile8(t):
    return jnp.concatenate([t] * ATT_HEADS, axis=1)


def _qkv(cq, ckv, kr, cos, sa, sb, gq, gkv, wq, wk, wv):
    lp = cq.shape[0]
    qw = ATT_HEADS * HEADW

    def body(cq_ref, ckv_ref, kr_ref, cos_ref, sa_ref, sb_ref, gq_ref, gkv_ref, wq_ref, wk_ref, wv_ref,
             q_ref, k_ref, v_ref, cqn_ref, ckvn_ref):
        cos_, sa_, sb_ = cos_ref[...], sa_ref[...], sb_ref[...]
        cqn = _rms_fwd(cq_ref[...], gq_ref[...]).astype(BF16)
        ckvn = _rms_fwd(ckv_ref[...], gkv_ref[...]).astype(BF16)
        cqn_ref[...] = cqn
        ckvn_ref[...] = ckvn
        q = _dot(cqn, wq_ref[...])
        q_ref[...] = (_rope(q, _tile8(cos_), _tile8(sa_), _tile8(sb_)) * Q_PRESCALE).astype(BF16)
        k = _dot(ckvn, wk_ref[...]) + _tile8(_rope(kr_ref[...], cos_, sa_, sb_))
        k_ref[...] = k.astype(BF16)
        lanes = lax.broadcasted_iota(jnp.int32, (1, qw), 1)
        ones = ((lanes % HEADW) >= V_HEAD).astype(F32)
        v_ref[...] = (_dot(ckvn, wv_ref[...]) + ones).astype(BF16)

    return pl.pallas_call(
        body, name="qkv", grid=(lp // ROWB,),
        in_specs=[_rowspec(ROWB, Q_LORA), _rowspec(ROWB, KV_LORA), _rowspec(ROWB, HEADW)]
        + [_rowspec(ROWB, HEADW)] * 3
        + [_fullspec((1, Q_LORA)), _fullspec((1, KV_LORA)), _fullspec((Q_LORA, qw)), _fullspec((KV_LORA, qw)),
           _fullspec((KV_LORA, qw))],
        out_specs=[_rowspec(ROWB, qw), _rowspec(ROWB, qw), _rowspec(ROWB, qw),
                   _rowspec(ROWB, Q_LORA), _rowspec(ROWB, KV_LORA)],
        out_shape=[_sds((lp, qw), BF16), _sds((lp, qw), BF16), _sds((lp, qw), BF16),
                   _sds((lp, Q_LORA), BF16), _sds((lp, KV_LORA), BF16)],
        compiler_params=_cp(("arbitrary",)),
    )(cq, ckv, kr, cos, sa, sb, gq, gkv, wq, wk, wv)


ATT_SCALE = (QK_NOPE + QK_ROPE) ** -0.5
LOG2E = 1.4426950408889634
LN2 = 0.6931471805599453
Q_PRESCALE = ATT_SCALE * LOG2E
KVB = 512


def _att_ok(qrow, krow):
    return (krow <= qrow) & ((krow >= PADF) | (qrow < PADF))


def _lanes(x, n):
    return x if n == 1 else jnp.concatenate([x] * n, axis=1)


def _pair_loop(lo, hi, tile, unrolls=(2,)):
    for u in tuple(unrolls) + (1,):
        n = jnp.maximum(hi - lo, 0)
        trips = n // u

        def many(t, c, u=u, lo=lo):
            for d in range(u):
                tile(lo + u * t + d)
            return c

        lax.fori_loop(0, trips, many, 0)
        lo = lo + trips * u


def _flash_fwd(q, k, v):
    lp = q.shape[0]
    nq = lp // ROWB
    rep = KVB // LANE

    def body(q_ref, k_ref, v_ref, o_ref, lse_ref, acc, m_s):
        i = pl.program_id(1)
        qb = q_ref[...]
        m_s[...] = jnp.full_like(m_s, NEG)
        acc[...] = jnp.zeros_like(acc)

        def tile(j, masked):
            off = pl.multiple_of(j * KVB, KVB)
            kb = k_ref[pl.ds(off, KVB), :]
            vb = v_ref[pl.ds(off, KVB), :]
            s = _dot(qb, kb, NT)
            if masked:
                qrow = i * ROWB + lax.broadcasted_iota(jnp.int32, s.shape, 0)
                krow = j * KVB + lax.broadcasted_iota(jnp.int32, s.shape, 1)
                s = jnp.where(_att_ok(qrow, krow), s, NEG)
            m_prev = m_s[...]
            m_new = jnp.maximum(m_prev, jnp.max(s, axis=1, keepdims=True))
            alpha = jnp.exp2(m_prev - m_new)
            p = jnp.exp2(s - _lanes(m_new, rep))
            acc[...] = _lanes(alpha, 2) * acc[...] + _dot(p.astype(BF16), vb)
            m_s[...] = m_new

        tile(0, True)
        _pair_loop(1, i, lambda j: tile(j, False), (8, 2))

        @pl.when(i > 0)
        def _():
            tile(i, True)

        l = acc[:, V_HEAD:]
        o_ref[...] = (acc[:, :V_HEAD] / l).astype(BF16)
        lse_ref[0] = (m_s[...] + jnp.log2(l)).T[0:1, :]

    return pl.pallas_call(
        body, name="flash_fwd", grid=(ATT_HEADS, nq),
        in_specs=[pl.BlockSpec((ROWB, HEADW), lambda h, i: (i, h)),
                  pl.BlockSpec((lp, HEADW), lambda h, i: (0, h)),
                  pl.BlockSpec((lp, HEADW), lambda h, i: (0, h))],
        out_specs=[pl.BlockSpec((ROWB, V_HEAD), lambda h, i: (i, h)),
                   pl.BlockSpec((1, 1, ROWB), lambda h, i: (h, 0, i))],
        out_shape=[_sds((lp, ATT_HEADS * V_HEAD), BF16), _sds((ATT_HEADS, 1, lp), F32)],
        scratch_shapes=[pltpu.VMEM((ROWB, HEADW), F32), pltpu.VMEM((ROWB, LANE), F32)],
        compiler_params=_cp(("arbitrary", "arbitrary")),
    )(q, k, v)


def _silu(x):
    return x * _sigmoid(x)


def _conv_fwd(xbc, cw, cb):
    lp, c = xbc.shape
    t8 = ROWB // 8

    def body(x_ref, prev_ref, w_ref, b_ref, o_ref, buf):
        i = pl.program_id(0)
        buf[pl.ds(0, 8), :] = jnp.where(i > 0, prev_ref[...], 0.0)
        buf[pl.ds(8, ROWB), :] = x_ref[...]
        w = w_ref[...]
        pre = b_ref[...] + sum(w[kk:kk + 1, :] * buf[pl.ds(8 - (CONV_K - 1) + kk, ROWB), :] for kk in range(CONV_K))
        o_ref[...] = _silu(pre)

    return pl.pallas_call(
        body, name="conv_fwd", grid=(lp // ROWB,),
        in_specs=[_rowspec(ROWB, c), pl.BlockSpec((8, c), lambda i: (jnp.maximum(i * t8 - 1, 0), 0)),
                  _fullspec((8, c)), _fullspec((1, c))],
        out_specs=_rowspec(ROWB, c), out_shape=_sds((lp, c), F32),
        scratch_shapes=[pltpu.VMEM((ROWB + 8, c), F32)],
        compiler_params=_cp(("arbitrary",)),
    )(xbc, xbc, cw, cb)


def _expand_mat():
    r = np.arange(LANE)[:, None]
    c = np.arange(SSM_WIDTH)[None, :]
    return jnp.asarray((c // SSM_HEAD_DIM == r).astype(np.float32))


def _tri_mat():
    i = np.arange(CHUNK)
    return jnp.asarray((i[:, None] >= i[None, :]).astype(np.float32))


def _ssd_prep(dtr_ref, bias_ref, alog_ref, tri, c, seq_rows):
    rows = c * CHUNK + lax.broadcasted_iota(jnp.int32, (CHUNK, LANE), 0)
    lanes = lax.broadcasted_iota(jnp.int32, (CHUNK, LANE), 1)
    valid = (rows >= PADF) & (rows < PADF + seq_rows) & (lanes < SSM_HEADS)
    dtr = dtr_ref[...] + bias_ref[...]
    sp = jnp.maximum(dtr, 0.0) + jnp.log(1.0 + jnp.exp(-jnp.abs(dtr)))
    dt = jnp.where(valid, sp, 0.0)
    a = -jnp.exp(alog_ref[...])
    acol = _dot(tri, dt * a, prec=HI)
    return dt, a, acol, valid, dtr


def _ssd_fwd(xbc_act, dtr, dt_bias, a_log, seq_rows):
    lp = xbc_act.shape[0]
    nc = lp // CHUNK
    gw = SSM_WIDTH // SSM_GROUPS
    hpg = SSM_HEADS // SSM_GROUPS

    def body(x_ref, b_ref, c_ref, dtr_ref, bias_ref, alog_ref, tri_ref, ex_ref, y_ref, hp_ref, h_s):
        c = pl.program_id(0)

        @pl.when(c == 0)
        def _():
            h_s[...] = jnp.zeros_like(h_s)

        tri = tri_ref[...]
        ex = ex_ref[...]
        dt, a, acol, _, _ = _ssd_prep(dtr_ref, bias_ref, alog_ref, tri, c, seq_rows)
        arow = acol.T
        dtrow = dt.T
        alast = acol[CHUNK - 1:CHUNK, :]
        e_all = _dot(jnp.exp(acol), ex, prec=HI)
        wx_all = _dot(jnp.exp(alast - acol) * dt, ex, prec=HI)
        dec_all = _dot(jnp.broadcast_to(jnp.exp(alast), (8, LANE)), ex, prec=HI)[0:1, :]
        causal = tri > 0.5
        hp_ref[0] = h_s[...]
        for g in range(SSM_GROUPS):
            gs = slice(g * gw, (g + 1) * gw)
            bg = b_ref[:, g * SSM_STATE:(g + 1) * SSM_STATE]
            cg = c_ref[:, g * SSM_STATE:(g + 1) * SSM_STATE].astype(BF16)
            xg = x_ref[:, gs]
            hg = h_s[:, gs]
            gm = _bdot(cg, bg, NT)
            y_off = _bdot(cg, hg) * e_all[:, gs]
            for r in range(hpg):
                hd = g * hpg + r
                seg = acol[:, hd:hd + 1] - arow[hd:hd + 1, :]
                lm = jnp.where(causal, jnp.exp(jnp.where(causal, seg, 0.0)), 0.0)
                w = gm * lm * dtrow[hd:hd + 1, :]
                cs = slice(r * SSM_HEAD_DIM, (r + 1) * SSM_HEAD_DIM)
                y_ref[:, pl.ds(hd * SSM_HEAD_DIM, SSM_HEAD_DIM)] = _bdot(w, xg[:, cs]) + y_off[:, cs]
            st = _bdot(bg.T, xg * wx_all[:, gs])
            h_s[:, gs] = hg * dec_all[:, gs] + st

    xs_spec = pl.BlockSpec((CHUNK, SSM_WIDTH), lambda c: (c, 0))
    b_spec = pl.BlockSpec((CHUNK, 2 * SSM_STATE), lambda c: (c, SSM_WIDTH // (2 * SSM_STATE)))
    c_spec = pl.BlockSpec((CHUNK, 2 * SSM_STATE), lambda c: (c, SSM_WIDTH // (2 * SSM_STATE) + 1))
    return pl.pallas_call(
        body, name="ssd_fwd", grid=(nc,),
        in_specs=[xs_spec, b_spec, c_spec, pl.BlockSpec((CHUNK, LANE), lambda c: (c, 0)),
                  _fullspec((1, LANE)), _fullspec((1, LANE)), _fullspec((CHUNK, CHUNK)), _fullspec((LANE, SSM_WIDTH))],
        out_specs=[xs_spec, pl.BlockSpec((1, SSM_STATE, SSM_WIDTH), lambda c: (c, 0, 0))],
        out_shape=[_sds((lp, SSM_WIDTH), F32), _sds((nc, SSM_STATE, SSM_WIDTH), F32)],
        scratch_shapes=[pltpu.VMEM((SSM_STATE, SSM_WIDTH), F32)],
        compiler_params=_cp(("arbitrary",)),
    )(xbc_act, xbc_act, xbc_act, dtr, dt_bias, a_log, _tri_mat(), _expand_mat())


def _group_mean(x):
    gw = SSM_WIDTH // SSM_GROUPS
    parts = [jnp.broadcast_to(jnp.mean(x[:, g * gw:(g + 1) * gw], axis=-1, keepdims=True), (x.shape[0], gw))
             for g in range(SSM_GROUPS)]
    return jnp.concatenate(parts, axis=1)


def _ssd_post(y, xbc_act, z, dskip, gnorm):
    lp = y.shape[0]

    def body(y_ref, x_ref, z_ref, d_ref, g_ref, o_ref):
        z_ = z_ref[...]
        gt = (y_ref[...] + d_ref[...] * x_ref[...]) * _silu(z_)
        r = lax.rsqrt(_group_mean(gt * gt) + EPS)
        o_ref[...] = ((gt * r) * g_ref[...]).astype(BF16)

    return pl.pallas_call(
        body, name="ssd_post", grid=(lp // ROWB,),
        in_specs=[_rowspec(ROWB, SSM_WIDTH)] * 3 + [_fullspec((1, SSM_WIDTH))] * 2,
        out_specs=_rowspec(ROWB, SSM_WIDTH), out_shape=_sds((lp, SSM_WIDTH), BF16),
        compiler_params=_cp(("arbitrary",)),
    )(y, xbc_act, z, dskip, gnorm)


def _out_proj(att, ssm, head, x, w_out, g_post):
    lp = att.shape[0]

    def body(a_ref, s_ref, head_ref, x_ref, w_ref, g_ref, mix_ref, h1_ref):
        mix = _dot(a_ref[...], w_ref[pl.ds(0, 1024), :]) + _dot(s_ref[...], w_ref[pl.ds(1024, 1024), :])
        mix_ref[...] = mix
        h1_ref[...] = _h_block(head_ref, x_ref) + _rms_fwd(mix, g_ref[...])

    return pl.pallas_call(
        body, name="out_proj", grid=(lp // ROWB,),
        in_specs=[_rowspec(ROWB, 1024)] * 2 + [_fullspec((ROWB, D_MODEL)), _xspec(), _fullspec((2048, D_MODEL)),
                                               _fullspec((1, D_MODEL))],
        out_specs=[_rowspec(ROWB, D_MODEL)] * 2, out_shape=[_sds((lp, D_MODEL), F32)] * 2,
        compiler_params=_cp(("arbitrary",)),
    )(att, ssm, head, x, w_out, g_post)


def _resident(w_hbm, w_vmem, sem):
    @pl.when(pl.program_id(0) == 0)
    def _():
        cp = pltpu.make_async_copy(w_hbm, w_vmem, sem)
        cp.start()
        cp.wait()


ANY = pl.BlockSpec(memory_space=pl.ANY)


def _mlp_fwd(h1, tgt, w_up, w_down, g_pre, g_post, seq_rows):
    lp = h1.shape[0]

    def body(h1_ref, t_ref, wu_hbm, wd_hbm, gpre_ref, gpost_ref, hn2_ref, f_ref, dh2_ref, loss_ref, wu, wd, sems):
        _resident(wu_hbm, wu, sems.at[0])
        _resident(wd_hbm, wd, sems.at[1])
        i = pl.program_id(0)
        h1_ = h1_ref[...]
        hn2 = _rms_fwd(h1_, gpre_ref[...]).astype(BF16)
        hn2_ref[...] = hn2
        u = jnp.maximum(_dot(hn2, wu[...]), 0.0)
        f = _dot((u * u).astype(BF16), wd[...])
        f_ref[...] = f
        h2 = h1_ + _rms_fwd(f, gpost_ref[...])
        rows = i * ROWB + lax.broadcasted_iota(jnp.int32, (ROWB, 1), 0)
        real = (rows >= PADF + N_META) & (rows < PADF + seq_rows)
        err = jnp.where(real, h2 - t_ref[...], 0.0)
        dh2_ref[...] = err * (1.0 / D_MODEL)
        _acc(loss_ref, _colsum8(err * err))

    return pl.pallas_call(
        body, name="mlp_fwd", grid=(lp // ROWB,),
        in_specs=[_rowspec(ROWB, D_MODEL), _xspec()] + [ANY, ANY] + [_fullspec((1, D_MODEL))] * 2,
        out_specs=[_rowspec(ROWB, D_MODEL)] * 3 + [_fullspec((8, D_MODEL))],
        out_shape=[_sds((lp, D_MODEL), BF16), _sds((lp, D_MODEL), F32), _sds((lp, D_MODEL), F32), _sds((8, D_MODEL), F32)],
        scratch_shapes=[pltpu.VMEM((D_MODEL, D_FF), BF16), pltpu.VMEM((D_FF, D_MODEL), BF16), pltpu.SemaphoreType.DMA((2,))],
        compiler_params=_cp(("arbitrary",)),
    )(h1, tgt, w_up, w_down, g_pre, g_post)


def _pad_cols(w, width):
    return jnp.pad(w, ((0, 0), (0, width - w.shape[1])))


def _layout_weights(w_in, w_q_up, w_kv_up):
    o = np.cumsum((0,) + IN_SPLITS)
    pieces = [w_in[:, o[k]:o[k + 1]] for k in range(6)]
    kr = jnp.pad(pieces[2], ((0, 0), (QK_NOPE, HEADW - QK_NOPE - QK_ROPE)))
    w_all = jnp.concatenate([pieces[0], pieces[1], kr, pieces[3], pieces[4], _pad_cols(pieces[5], LANE)], axis=1)
    wq = jnp.pad(w_q_up.reshape(Q_LORA, ATT_HEADS, QK_NOPE + QK_ROPE), ((0, 0), (0, 0), (0, HEADW - QK_NOPE - QK_ROPE)))
    wkv = w_kv_up.reshape(KV_LORA, ATT_HEADS, QK_NOPE + V_HEAD)
    wk = jnp.pad(wkv[:, :, :QK_NOPE], ((0, 0), (0, 0), (0, HEADW - QK_NOPE)))
    wv = wkv[:, :, QK_NOPE:]
    wvp = jnp.pad(wv, ((0, 0), (0, 0), (0, HEADW - V_HEAD)))
    return (w_all, wq.reshape(Q_LORA, -1), wk.reshape(KV_LORA, -1), wv.reshape(KV_LORA, -1),
            wvp.reshape(KV_LORA, -1))


def _rope_tables(lp):
    pos = jnp.maximum(jnp.arange(lp, dtype=jnp.int32) - PADF, 0).astype(F32)
    inv_freq = ROPE_THETA ** (-jnp.arange(0, QK_ROPE, 2, dtype=F32) / QK_ROPE)
    ang = pos[:, None] * inv_freq[None, :]
    cos, sin = jnp.cos(ang), jnp.sin(ang)
    one, zero = jnp.ones((lp, QK_NOPE), F32), jnp.zeros((lp, QK_NOPE), F32)
    z32, z64 = jnp.zeros((lp, 32), F32), jnp.zeros((lp, 64), F32)
    cos_t = jnp.concatenate([one, cos, cos, jnp.ones((lp, 64), F32)], axis=1)
    sa = jnp.concatenate([zero, -sin, z32, z64], axis=1)
    sb = jnp.concatenate([zero, z32, sin, z64], axis=1)
    return cos_t, sa, sb


def _row1(v, width=None):
    v = v.reshape(1, -1).astype(F32)
    return v if width is None else _pad_cols(v, width)


def _local_forward(head, x, tgt, p):
    assert head.shape[0] == ROWB and x.shape[0] % ROWB == 0
    lp = ROWB + x.shape[0]
    seq_rows = N_META + x.shape[0]
    f = {"seq_rows": seq_rows}
    w_all, wq, wk, wv, wvp = _layout_weights(p["w_in"], p["w_q_up"], p["w_kv_up"])
    f.update(w_all=w_all, wq=wq, wk=wk, wv=wv)
    f["hn"], cq, ckv, kr, f["z"], f["xbc"], f["dtr"] = _norm_in_proj(head, x, _row1(p["norm_mix_pre"]), w_all)
    f.update(cq=cq, ckv=ckv)
    f["rope"] = _rope_tables(lp)
    f["q"], f["k"], f["v"], f["cqn"], f["ckvn"] = _qkv(cq, ckv, kr, *f["rope"], _row1(p["q_a_norm"]),
                                                   _row1(p["kv_a_norm"]), wq, wk, wvp)
    f["att"], f["lse"] = _flash_fwd(f["q"], f["k"], f["v"])
    f["cw"] = jnp.pad(p["conv_w"].astype(F32), ((0, 8 - CONV_K), (0, 0)))
    f["xact"] = _conv_fwd(f["xbc"], f["cw"], _row1(p["conv_b"]))
    f["dt_bias"], f["a_log"] = _row1(p["dt_bias"], LANE), _row1(p["a_log"], LANE)
    f["y"], f["hprev"] = _ssd_fwd(f["xact"], f["dtr"], f["dt_bias"], f["a_log"], seq_rows)
    f["dskip"] = jnp.repeat(p["d_skip"].reshape(-1).astype(F32), SSM_HEAD_DIM).reshape(1, SSM_WIDTH)
    f["ssm"] = _ssd_post(f["y"], f["xact"], f["z"], f["dskip"], _row1(p["ssm_norm"]))
    f["mix"], f["h1"] = _out_proj(f["att"], f["ssm"], head, x, p["w_out"], _row1(p["norm_mix_post"]))
    f["hn2"], f["f"], f["dh2"], loss8 = _mlp_fwd(f["h1"], tgt, p["w_mlp_up"], p["w_mlp_down"],
                                                 _row1(p["norm_mlp_pre"]), _row1(p["norm_mlp_post"]), seq_rows)
    f["loss"] = 0.5 * jnp.sum(loss8) / D_MODEL
    return f


MLPB = 256


def _mlp_bwd(dh2, f, h1, hn2, w_up, w_down, g_pre, g_post):
    lp = h1.shape[0]

    def body(dh2_ref, f_ref, h1_ref, hn2_ref, wu_hbm, wd_hbm, gpre_ref, gpost_ref,
             dh1_ref, du_ref, a_ref, df_ref, dgpre_ref, dgpost_ref, wu, wd, sems):
        _resident(wu_hbm, wu, sems.at[0])
        _resident(wd_hbm, wd, sems.at[1])
        dh2_ = dh2_ref[...]
        df, dgp = _rms_bwd(f_ref[...], gpost_ref[...], dh2_)
        dfb = df.astype(BF16)
        df_ref[...] = dfb
        da = _dot(dfb, wd[...], NT)
        u = jnp.maximum(_dot(hn2_ref[...], wu[...]), 0.0)
        a_ref[...] = (u * u).astype(BF16)
        du = (da * (2.0 * u)).astype(BF16)
        du_ref[...] = du
        dhn2 = _dot(du, wu[...], NT)
        dx, dgq = _rms_bwd(h1_ref[...], gpre_ref[...], dhn2)
        dh1_ref[...] = dh2_ + dx
        _acc(dgpre_ref, _colsum8(dgq))
        _acc(dgpost_ref, _colsum8(dgp))

    return pl.pallas_call(
        body, name="mlp_bwd", grid=(lp // MLPB,),
        in_specs=[_rowspec(MLPB, D_MODEL)] * 4 + [ANY, ANY] + [_fullspec((1, D_MODEL))] * 2,
        out_specs=[_rowspec(MLPB, D_MODEL), _rowspec(MLPB, D_FF), _rowspec(MLPB, D_FF), _rowspec(MLPB, D_MODEL),
                   _fullspec((8, D_MODEL)), _fullspec((8, D_MODEL))],
        out_shape=[_sds((lp, D_MODEL), F32), _sds((lp, D_FF), BF16), _sds((lp, D_FF), BF16), _sds((lp, D_MODEL), BF16),
                   _sds((8, D_MODEL), F32), _sds((8, D_MODEL), F32)],
        scratch_shapes=[pltpu.VMEM((D_MODEL, D_FF), BF16), pltpu.VMEM((D_FF, D_MODEL), BF16), pltpu.SemaphoreType.DMA((2,))],
        compiler_params=_cp(("arbitrary",)),
    )(dh2, f, h1, hn2, w_up, w_down, g_pre, g_post)


def _out_bwd(dh1, mix, w_out, g_post):
    lp = dh1.shape[0]

    def body(dh1_ref, mix_ref, w_ref, g_ref, dmix_ref, datt_ref, dssm_ref, dg_ref):
        dmix, dg = _rms_bwd(mix_ref[...], g_ref[...], dh1_ref[...])
        dmb = dmix.astype(BF16)
        dmix_ref[...] = dmb
        datt_ref[...] = _dot(dmb, w_ref[pl.ds(0, 1024), :], NT).astype(BF16)
        dssm_ref[...] = _dot(dmb, w_ref[pl.ds(1024, 1024), :], NT)
        _acc(dg_ref, _colsum8(dg))

    return pl.pallas_call(
        body, name="out_bwd", grid=(lp // ROWB,),
        in_specs=[_rowspec(ROWB, D_MODEL)] * 2 + [_fullspec((2048, D_MODEL)), _fullspec((1, D_MODEL))],
        out_specs=[_rowspec(ROWB, D_MODEL)] * 3 + [_fullspec((8, D_MODEL))],
        out_shape=[_sds((lp, D_MODEL), BF16), _sds((lp, 1024), BF16), _sds((lp, 1024), F32), _sds((8, D_MODEL), F32)],
        compiler_params=_cp(("arbitrary",)),
    )(dh1, mix, w_out, g_post)


def _ssd_post_bwd(dssm, y, xact, z, dskip, gnorm):
    lp = y.shape[0]

    def body(do_ref, y_ref, x_ref, z_ref, d_ref, g_ref, dy_ref, dz_ref, dg_ref, dd_ref):
        z_, x_ = z_ref[...], x_ref[...]
        sg = _sigmoid(z_)
        sz = z_ * sg
        y2 = y_ref[...] + d_ref[...] * x_
        gt = y2 * sz
        r = lax.rsqrt(_group_mean(gt * gt) + EPS)
        gh = gt * r
        do = do_ref[...]
        dgh = do * g_ref[...]
        dgt = r * (dgh - gh * _group_mean(dgh * gh))
        dy2 = dgt * sz
        dy_ref[...] = dy2
        dz_ref[...] = (dgt * y2 * (sg * (1.0 + z_ * (1.0 - sg)))).astype(BF16)
        _acc(dg_ref, _colsum8(do * gh))
        _acc(dd_ref, _colsum8(dy2 * x_))

    return pl.pallas_call(
        body, name="ssd_post_bwd", grid=(lp // ROWB,),
        in_specs=[_rowspec(ROWB, SSM_WIDTH)] * 4 + [_fullspec((1, SSM_WIDTH))] * 2,
        out_specs=[_rowspec(ROWB, SSM_WIDTH)] * 2 + [_fullspec((8, SSM_WIDTH))] * 2,
        out_shape=[_sds((lp, SSM_WIDTH), F32), _sds((lp, SSM_WIDTH), BF16), _sds((8, SSM_WIDTH), F32), _sds((8, SSM_WIDTH), F32)],
        compiler_params=_cp(("arbitrary",)),
    )(dssm, y, xact, z, dskip, gnorm)


def _ssd_bwd(dy, xact, dtr, hprev, dt_bias, a_log, dskip, seq_rows):
    lp = xact.shape[0]
    nc = lp // CHUNK
    gw = SSM_WIDTH // SSM_GROUPS
    hpg = SSM_HEADS // SSM_GROUPS
    nb = SSM_WIDTH // (2 * SSM_STATE)

    def body(dy_ref, x_ref, b_ref, c_ref, dtr_ref, hp_ref, bias_ref, alog_ref, dsk_ref, tri_ref, ex_ref, ext_ref,
             dact_ref, ddtr_ref, da_ref, dbias_ref, dh_s):
        step = pl.program_id(0)
        c = nc - 1 - step

        @pl.when(step == 0)
        def _():
            dh_s[...] = jnp.zeros_like(dh_s)

        tri = tri_ref[...]
        ex = ex_ref[...]
        dt, a, acol, valid, dtr_ = _ssd_prep(dtr_ref, bias_ref, alog_ref, tri, c, seq_rows)
        arow = acol.T
        dtrow = dt.T
        alast = acol[CHUNK - 1:CHUNK, :]
        e_all = _dot(jnp.exp(acol), ex, prec=HI)
        wgt0 = jnp.exp(alast - acol)
        wgt = wgt0 * dt
        wx_all = _dot(wgt, ex, prec=HI)
        elast = jnp.exp(alast)
        dec_all = _dot(jnp.broadcast_to(elast, (8, LANE)), ex, prec=HI)[0:1, :]
        causal = tri > 0.5
        upper = tri.T > 0.5
        lane_id = lax.broadcasted_iota(jnp.int32, (1, LANE), 1)
        sub_id = lax.broadcasted_iota(jnp.int32, (CHUNK, 1), 0)
        dacol = jnp.zeros((CHUNK, LANE), F32)
        darowf = jnp.zeros((CHUNK, LANE), F32)
        ddtrowf = jnp.zeros((CHUNK, LANE), F32)
        dwgt = jnp.zeros((CHUNK, LANE), F32)
        delast = jnp.zeros((1, LANE), F32)
        for g in range(SSM_GROUPS):
            gs = slice(g * gw, (g + 1) * gw)
            ext_g = ext_ref[pl.ds(g * gw, gw), :]
            bg = b_ref[:, g * SSM_STATE:(g + 1) * SSM_STATE]
            cg = c_ref[:, g * SSM_STATE:(g + 1) * SSM_STATE]
            bgb, cgb = bg.astype(BF16), cg.astype(BF16)
            xg = x_ref[:, gs]
            dyg = dy_ref[:, gs]
            hg = hp_ref[0, :, gs]
            dhg = dh_s[:, gs]
            hgb, dhgb = hg.astype(BF16), dhg.astype(BF16)
            gm = _dot(cgb, bgb, NT)
            gmt = _dot(bgb, cgb, NT)
            y_off = _dot(cgb, hgb) * e_all[:, gs]
            dy0 = (dyg * e_all[:, gs]).astype(BF16)
            dcg = _dot(dy0, hgb, NT)
            dh_in = _dot(cg.T.astype(BF16), dy0) + dhg * dec_all[:, gs]
            dacol = dacol + _dot(dyg * y_off, ext_g, prec=HI)
            xw = xg * wx_all[:, gs]
            dxw = _dot(bgb, dhgb)
            dx_state = dxw * wx_all[:, gs]
            dwgt = dwgt + _dot(dxw * xg, ext_g, prec=HI)
            dbt = _dot(dhgb, xw.astype(BF16), NT)
            delast = delast + jnp.sum(_dot(_colsum8(dhg * hg), ext_g, prec=HI), axis=0, keepdims=True)
            dgm = jnp.zeros((CHUNK, CHUNK), F32)
            for r in range(hpg):
                hd = g * hpg + r
                cs = slice(r * SSM_HEAD_DIM, (r + 1) * SSM_HEAD_DIM)
                acol_r, arow_r = acol[:, hd:hd + 1], arow[hd:hd + 1, :]
                dtrow_r, dtcol_r = dtrow[hd:hd + 1, :], dt[:, hd:hd + 1]
                lm = jnp.where(causal, jnp.exp(jnp.where(causal, acol_r - arow_r, 0.0)), 0.0)
                lmt = jnp.where(upper, jnp.exp(jnp.where(upper, arow_r - acol_r, 0.0)), 0.0)
                wt = gmt * lmt * dtcol_r
                dy_r = dyg[:, cs].astype(BF16)
                dx_r = _dot(wt.astype(BF16), dy_r)
                dw = _dot(dy_r, xg[:, cs].astype(BF16), NT)
                t1 = dw * lm
                dgm = dgm + t1 * dtrow_r
                q1 = t1 * gm
                m = q1 * dtrow_r
                dacol = dacol + jnp.sum(m, axis=1, keepdims=True) * (lane_id == hd).astype(F32)
                darowf = darowf - (sub_id == hd).astype(F32) * jnp.sum(m, axis=0, keepdims=True)
                ddtrowf = ddtrowf + (sub_id == hd).astype(F32) * jnp.sum(q1, axis=0, keepdims=True)
                dact_ref[:, pl.ds(hd * SSM_HEAD_DIM, SSM_HEAD_DIM)] = (
                    dx_r + dx_state[:, cs] + dyg[:, cs] * dsk_ref[:, pl.ds(hd * SSM_HEAD_DIM, SSM_HEAD_DIM)])
            dgmb = dgm.astype(BF16)
            dact_ref[:, pl.ds(SSM_WIDTH + g * SSM_STATE, SSM_STATE)] = dbt.T + _dot(dgm.T.astype(BF16), cgb)
            dact_ref[:, pl.ds(SSM_WIDTH + 2 * SSM_STATE + g * SSM_STATE, SSM_STATE)] = dcg + _dot(dgmb, bgb)
            dh_s[:, gs] = dh_in
        t = dwgt * wgt
        dalast = jnp.sum(t, axis=0, keepdims=True) + delast * elast
        dacol_tot = dacol - t + darowf.T + (sub_id == CHUNK - 1).astype(F32) * dalast
        dda = _dot(tri.T, dacol_tot, prec=HI)
        ddt = dwgt * wgt0 + ddtrowf.T + dda * a
        ddtr = jnp.where(valid, ddt * _sigmoid(dtr_), 0.0)
        ddtr_ref[...] = ddtr
        _acc(da_ref, _colsum8(dda * dt) * a)
        _acc(dbias_ref, _colsum8(ddtr))

    rev = lambda c: nc - 1 - c
    xs_spec = pl.BlockSpec((CHUNK, SSM_WIDTH), lambda c: (rev(c), 0))
    return pl.pallas_call(
        body, name="ssd_bwd", grid=(nc,),
        in_specs=[xs_spec, xs_spec,
                  pl.BlockSpec((CHUNK, 2 * SSM_STATE), lambda c: (rev(c), nb)),
                  pl.BlockSpec((CHUNK, 2 * SSM_STATE), lambda c: (rev(c), nb + 1)),
                  pl.BlockSpec((CHUNK, LANE), lambda c: (rev(c), 0)),
                  pl.BlockSpec((1, SSM_STATE, SSM_WIDTH), lambda c: (rev(c), 0, 0)),
                  _fullspec((1, LANE)), _fullspec((1, LANE)), _fullspec((1, SSM_WIDTH)),
                  _fullspec((CHUNK, CHUNK)), _fullspec((LANE, SSM_WIDTH)), _fullspec((SSM_WIDTH, LANE))],
        out_specs=[pl.BlockSpec((CHUNK, CONV_DIM), lambda c: (rev(c), 0)), pl.BlockSpec((CHUNK, LANE), lambda c: (rev(c), 0)),
                   _fullspec((8, LANE)), _fullspec((8, LANE))],
        out_shape=[_sds((lp, CONV_DIM), F32), _sds((lp, LANE), F32), _sds((8, LANE), F32), _sds((8, LANE), F32)],
        scratch_shapes=[pltpu.VMEM((SSM_STATE, SSM_WIDTH), F32)],
        compiler_params=_cp(("arbitrary",)),
    )(dy, xact, xact, xact, dtr, hprev, dt_bias, a_log, dskip, _tri_mat(), _expand_mat(), _expand_mat().T)


def _conv_bwd(dact, xbc, cw, cb):
    lp, c = xbc.shape
    t8 = ROWB // 8
    nb = lp // ROWB
    ext = ROWB + 8

    def body(d_ref, dnext_ref, x_ref, prev_ref, next_ref, w_ref, b_ref, dx_ref, dw_ref, db_ref, xb, dp):
        i = pl.program_id(0)
        last = i == nb - 1
        xb[pl.ds(0, 8), :] = jnp.where(i > 0, prev_ref[...], 0.0)
        xb[pl.ds(8, ROWB), :] = x_ref[...]
        xb[pl.ds(8 + ROWB, 8), :] = jnp.where(last, 0.0, next_ref[...])
        w = w_ref[...]
        pre = b_ref[...] + sum(w[kk:kk + 1, :] * xb[pl.ds(8 - (CONV_K - 1) + kk, ext), :] for kk in range(CONV_K))
        sg = _sigmoid(pre)
        dsilu = sg * (1.0 + pre * (1.0 - sg))
        dp[pl.ds(0, ROWB), :] = d_ref[...] * dsilu[0:ROWB]
        dp[pl.ds(ROWB, 8), :] = jnp.where(last, 0.0, dnext_ref[...]) * dsilu[ROWB:ext]
        dx_ref[...] = sum(w[kk:kk + 1, :] * dp[pl.ds(CONV_K - 1 - kk, ROWB), :] for kk in range(CONV_K)).astype(BF16)
        dpre = dp[pl.ds(0, ROWB), :]
        sub = lax.broadcasted_iota(jnp.int32, (8, 1), 0)
        dwv = jnp.zeros((8, c), F32)
        for kk in range(CONV_K):
            part = jnp.sum(_colsum8(dpre * xb[pl.ds(8 - (CONV_K - 1) + kk, ROWB), :]), axis=0, keepdims=True)
            dwv = dwv + jnp.where(sub == kk, part, 0.0)
        _acc(dw_ref, dwv)
        _acc(db_ref, _colsum8(dpre))

    nxt = lambda i: (jnp.minimum((i + 1) * t8, lp // 8 - 1), 0)
    prv = lambda i: (jnp.maximum(i * t8 - 1, 0), 0)
    return pl.pallas_call(
        body, name="conv_bwd", grid=(nb,),
        in_specs=[_rowspec(ROWB, c), pl.BlockSpec((8, c), nxt), _rowspec(ROWB, c), pl.BlockSpec((8, c), prv),
                  pl.BlockSpec((8, c), nxt), _fullspec((8, c)), _fullspec((1, c))],
        out_specs=[_rowspec(ROWB, c), _fullspec((8, c)), _fullspec((8, c))],
        out_shape=[_sds((lp, c), BF16), _sds((8, c), F32), _sds((8, c), F32)],
        scratch_shapes=[pltpu.VMEM((ROWB + 16, c), F32), pltpu.VMEM((ROWB + 8, c), F32)],
        compiler_params=_cp(("arbitrary",)),
    )(dact, dact, xbc, xbc, xbc, cw, cb)


def _attn_delta(datt, att):
    lp = att.shape[0]

    def body(do_ref, o_ref, d_ref):
        d = jnp.sum(do_ref[...].astype(F32) * o_ref[...].astype(F32), axis=1, keepdims=True)
        d_ref[0] = jnp.broadcast_to(d, (ROWB, LANE)).T[0:1, :]

    blk = pl.BlockSpec((ROWB, V_HEAD), lambda h, i: (i, h))
    return pl.pallas_call(
        body, name="attn_delta", grid=(ATT_HEADS, lp // ROWB), in_specs=[blk, blk],
        out_specs=pl.BlockSpec((1, 1, ROWB), lambda h, i: (h, 0, i)),
        out_shape=_sds((ATT_HEADS, 1, lp), F32), compiler_params=_cp(("arbitrary", "arbitrary")),
    )(datt, att)


def _flash_bwd(q, k, v, datt, lse_row, delta_row):
    lp = q.shape[0]
    nk = lp // ROWB

    def body(k_ref, v_ref, q_ref, do_ref, lse_ref, dl_ref, dq_ref, dk_ref, dv_ref, dq_acc, dk_acc, dv_acc):
        j = pl.program_id(1)

        @pl.when(j == 0)
        def _():
            dq_acc[...] = jnp.zeros_like(dq_acc)

        kb, vb = k_ref[...], v_ref[...]
        dk_acc[...] = jnp.zeros_like(dk_acc)
        dv_acc[...] = jnp.zeros_like(dv_acc)

        def tile(i, masked):
            off = pl.multiple_of(i * ROWB, ROWB)
            qb = q_ref[pl.ds(off, ROWB), :]
            dob = do_ref[pl.ds(off, ROWB), :]
            lse_r = lse_ref[0, :, pl.ds(off, ROWB)]
            dl_r = dl_ref[0, :, pl.ds(off, ROWB)]
            st = _dot(kb, qb, NT)
            if masked:
                krow = j * ROWB + lax.broadcasted_iota(jnp.int32, st.shape, 0)
                qrow = i * ROWB + lax.broadcasted_iota(jnp.int32, st.shape, 1)
                st = jnp.where(_att_ok(qrow, krow), st, NEG)
            pt = jnp.exp2(st - lse_r)
            dv_acc[...] += _dot(pt.astype(BF16), dob)
            dpt = _dot(vb, dob, NT)
            dst = (pt * (dpt - dl_r)).astype(BF16)
            dk_acc[...] += _dot(dst, qb)
            dq_acc[pl.ds(off, ROWB), :] += _dot(dst, kb, ((0,), (0,)))

        tile(j, True)

        @pl.when(j == 0)
        def _():
            _pair_loop(1, nk, lambda i: tile(i, True), (2,))

        @pl.when(j > 0)
        def _():
            _pair_loop(j + 1, nk, lambda i: tile(i, False), (4,))

        dk_ref[...] = dk_acc[...] * LN2
        dv_ref[...] = dv_acc[...].astype(BF16)
        dq_ref[...] = dq_acc[pl.ds(pl.multiple_of(j * ROWB, ROWB), ROWB), :] * ATT_SCALE

    stat = pl.BlockSpec((1, 1, lp), lambda h, j: (h, 0, 0))
    blk = pl.BlockSpec((ROWB, HEADW), lambda h, j: (j, h))
    return pl.pallas_call(
        body, name="flash_bwd", grid=(ATT_HEADS, nk),
        in_specs=[blk, pl.BlockSpec((ROWB, V_HEAD), lambda h, j: (j, 2 * h)),
                  pl.BlockSpec((lp, HEADW), lambda h, j: (0, h)), pl.BlockSpec((lp, V_HEAD), lambda h, j: (0, h)),
                  stat, stat],
        out_specs=[blk, blk, pl.BlockSpec((ROWB, V_HEAD), lambda h, j: (j, h))],
        out_shape=[_sds((lp, ATT_HEADS * HEADW), F32), _sds((lp, ATT_HEADS * HEADW), F32),
                   _sds((lp, ATT_HEADS * V_HEAD), BF16)],
        scratch_shapes=[pltpu.VMEM((lp, HEADW), F32), pltpu.VMEM((ROWB, HEADW), F32), pltpu.VMEM((ROWB, V_HEAD), F32)],
        compiler_params=_cp(("arbitrary", "arbitrary")),
    )(k, v, q, datt, lse_row, delta_row)


def _qkv_bwd(dq, dk, dv, cq, ckv, cos, sa, sb, gq, gkv, wq, wk, wv):
    lp = cq.shape[0]
    qw = ATT_HEADS * HEADW

    def body(dq_ref, dk_ref, dv_ref, cq_ref, ckv_ref, cos_ref, sa_ref, sb_ref, gq_ref, gkv_ref, wq_ref, wk_ref, wv_ref,
             dcq_ref, dckv_ref, dkr_ref, dqp_ref, dkb_ref, dgq_ref, dgkv_ref):
        cos_, sa_, sb_ = cos_ref[...], sa_ref[...], sb_ref[...]
        dqp = _rope_t(dq_ref[...], _tile8(cos_), _tile8(sa_), _tile8(sb_)).astype(BF16)
        dqp_ref[...] = dqp
        dcq, dgq = _rms_bwd(cq_ref[...], gq_ref[...], _dot(dqp, wq_ref[...], NT))
        dcq_ref[...] = dcq.astype(BF16)
        dk_ = dk_ref[...]
        dkb = dk_.astype(BF16)
        dkb_ref[...] = dkb
        dksum = sum(dk_[:, hh * HEADW:(hh + 1) * HEADW] for hh in range(ATT_HEADS))
        dkr_ref[...] = _rope_t(dksum, cos_, sa_, sb_).astype(BF16)
        dckvn = _dot(dkb, wk_ref[...], NT) + _dot(dv_ref[...], wv_ref[...], NT)
        dckv, dgkv = _rms_bwd(ckv_ref[...], gkv_ref[...], dckvn)
        dckv_ref[...] = dckv.astype(BF16)
        _acc(dgq_ref, _colsum8(dgq))
        _acc(dgkv_ref, _colsum8(dgkv))

    return pl.pallas_call(
        body, name="qkv_bwd", grid=(lp // ROWB,),
        in_specs=[_rowspec(ROWB, qw), _rowspec(ROWB, qw), _rowspec(ROWB, ATT_HEADS * V_HEAD),
                  _rowspec(ROWB, Q_LORA), _rowspec(ROWB, KV_LORA)] + [_rowspec(ROWB, HEADW)] * 3
        + [_fullspec((1, Q_LORA)), _fullspec((1, KV_LORA)), _fullspec((Q_LORA, qw)), _fullspec((KV_LORA, qw)),
           _fullspec((KV_LORA, ATT_HEADS * V_HEAD))],
        out_specs=[_rowspec(ROWB, Q_LORA), _rowspec(ROWB, KV_LORA), _rowspec(ROWB, HEADW), _rowspec(ROWB, qw),
                   _rowspec(ROWB, qw), _fullspec((8, Q_LORA)), _fullspec((8, KV_LORA))],
        out_shape=[_sds((lp, Q_LORA), BF16), _sds((lp, KV_LORA), BF16), _sds((lp, HEADW), BF16), _sds((lp, qw), BF16),
                   _sds((lp, qw), BF16), _sds((8, Q_LORA), F32), _sds((8, KV_LORA), F32)],
        compiler_params=_cp(("arbitrary",)),
    )(dq, dk, dv, cq, ckv, cos, sa, sb, gq, gkv, wq, wk, wv)


def _in_bwd(dproj, head, x, dh1, g, w_all):
    lp = dh1.shape[0]

    def body(dp_ref, head_ref, x_ref, dh1_ref, g_ref, w_ref, dx_ref, dhead_ref, dg_ref):
        dx, dg = _rms_bwd(_h_block(head_ref, x_ref), g_ref[...], _dot(dp_ref[...], w_ref[...], NT))
        dh = dh1_ref[...] + dx

        @pl.when(pl.program_id(0) == 0)
        def _():
            dhead_ref[...] = dh

        @pl.when(pl.program_id(0) > 0)
        def _():
            dx_ref[...] = dh

        _acc(dg_ref, _colsum8(dg))

    return pl.pallas_call(
        body, name="in_bwd", grid=(lp // ROWB,),
        in_specs=[_rowspec(ROWB, PROJ_W), _fullspec((ROWB, D_MODEL)), _xspec(), _rowspec(ROWB, D_MODEL),
                  _fullspec((1, D_MODEL)), _fullspec((D_MODEL, PROJ_W))],
        out_specs=[_xspec(), _fullspec((ROWB, D_MODEL)), _fullspec((8, D_MODEL))],
        out_shape=[_sds(x.shape, F32), _sds((ROWB, D_MODEL), F32), _sds((8, D_MODEL), F32)],
        compiler_params=_cp(("arbitrary",)),
    )(dproj, head, x, dh1, g, w_all)


def _tile_of(n, cap=1024):
    return max(t for t in range(LANE, min(n, cap) + 1, LANE) if n % t == 0)


def _matmul_tn(name, a, b):
    rows, kd = a.shape
    nd = b.shape[1]
    tk, tn = _tile_of(kd), _tile_of(nd)

    def body(a_ref, b_ref, o_ref):
        @pl.when(pl.program_id(2) == 0)
        def _():
            o_ref[...] = jnp.zeros_like(o_ref)

        o_ref[...] += _dot(a_ref[...], b_ref[...], ((0,), (0,)))

    return pl.pallas_call(
        body, name=name, grid=(kd // tk, nd // tn, rows // ROWB),
        in_specs=[pl.BlockSpec((ROWB, tk), lambda i, j, r: (r, i)), pl.BlockSpec((ROWB, tn), lambda i, j, r: (r, j))],
        out_specs=pl.BlockSpec((tk, tn), lambda i, j, r: (i, j)), out_shape=_sds((kd, nd), F32),
        compiler_params=_cp(("arbitrary", "arbitrary", "arbitrary")),
    )(a, b)


def _local_backward(head, x, f, p):
    g = {}
    row = lambda v: _row1(v)
    s8 = lambda v: jnp.sum(v, axis=0)
    dh1, du, a_, df, dgpre, dgpost = _mlp_bwd(f["dh2"], f["f"], f["h1"], f["hn2"], p["w_mlp_up"], p["w_mlp_down"],
                                              row(p["norm_mlp_pre"]), row(p["norm_mlp_post"]))
    g["norm_mlp_pre"], g["norm_mlp_post"] = s8(dgpre), s8(dgpost)
    g["w_mlp_up"] = _matmul_tn("dw_mlp_up", f["hn2"], du)
    g["w_mlp_down"] = _matmul_tn("dw_mlp_down", a_, df)
    dmix, datt, dssm, dgmp = _out_bwd(dh1, f["mix"], p["w_out"], row(p["norm_mix_post"]))
    g["norm_mix_post"] = s8(dgmp)
    g["w_out"] = jnp.concatenate([_matmul_tn("dw_out_att", f["att"], dmix), _matmul_tn("dw_out_ssm", f["ssm"], dmix)], axis=0)
    dy, dz, dgn, dd = _ssd_post_bwd(dssm, f["y"], f["xact"], f["z"], f["dskip"], row(p["ssm_norm"]))
    g["ssm_norm"] = s8(dgn)
    g["d_skip"] = s8(dd).reshape(SSM_HEADS, SSM_HEAD_DIM).sum(axis=1)
    dact, ddtr, da8, dbias8 = _ssd_bwd(dy, f["xact"], f["dtr"], f["hprev"], f["dt_bias"], f["a_log"], f["dskip"], f["seq_rows"])
    g["a_log"], g["dt_bias"] = s8(da8)[:SSM_HEADS], s8(dbias8)[:SSM_HEADS]
    dxbc, dcw8, dcb8 = _conv_bwd(dact, f["xbc"], f["cw"], row(p["conv_b"]))
    g["conv_w"], g["conv_b"] = dcw8[:CONV_K], s8(dcb8)
    delta = _attn_delta(datt, f["att"])
    dq, dk, dv = _flash_bwd(f["q"], f["k"], f["v"], datt, f["lse"], delta)
    dcq, dckv, dkr, dqp, dkb, dgq, dgkv = _qkv_bwd(dq, dk, dv, f["cq"], f["ckv"], *f["rope"], row(p["q_a_norm"]),
                                                   row(p["kv_a_norm"]), f["wq"], f["wk"], f["wv"])
    g["q_a_norm"], g["kv_a_norm"] = s8(dgq), s8(dgkv)
    dwq = _matmul_tn("dw_q_up", f["cqn"], dqp).reshape(Q_LORA, ATT_HEADS, HEADW)
    g["w_q_up"] = dwq[:, :, :QK_NOPE + QK_ROPE].reshape(Q_LORA, -1)
    dwk = _matmul_tn("dw_k_up", f["ckvn"], dkb).reshape(KV_LORA, ATT_HEADS, HEADW)[:, :, :QK_NOPE]
    dwv = _matmul_tn("dw_v_up", f["ckvn"], dv).reshape(KV_LORA, ATT_HEADS, V_HEAD)
    g["w_kv_up"] = jnp.concatenate([dwk, dwv], axis=2).reshape(KV_LORA, -1)
    dproj = jnp.concatenate([dcq, dckv, dkr, dz, dxbc, ddtr.astype(BF16)], axis=1)
    dx, dhead, dgin = _in_bwd(dproj, head, x, dh1, row(p["norm_mix_pre"]), f["w_all"])
    g["norm_mix_pre"] = s8(dgin)
    g["meta_tokens"] = dhead[PADF:]
    dwa = _matmul_tn("dw_in", f["hn"], dproj)
    g["w_in"] = jnp.concatenate([dwa[:, PC_Q:PC_KR], dwa[:, PC_KR + QK_NOPE:PC_KR + QK_NOPE + QK_ROPE],
                                 dwa[:, PC_Z:PC_DT + SSM_HEADS]], axis=1)
    return dx, g


BIG = {"w_in": ((D_MODEL, IN_WIDTH), 1), "w_q_up": ((Q_LORA, ATT_HEADS * (QK_NOPE + QK_ROPE)), 1),
       "w_kv_up": ((KV_LORA, ATT_HEADS * (QK_NOPE + V_HEAD)), 1), "w_out": ((2 * D_MODEL, D_MODEL), 0),
       "w_mlp_up": ((D_MODEL, D_FF), 1), "w_mlp_down": ((D_FF, D_MODEL), 0), "conv_w": ((CONV_K, CONV_DIM), 1),
       "meta_tokens": ((N_META, D_MODEL), 1)}
SMALL = {"norm_mix_pre": D_MODEL, "q_a_norm": Q_LORA, "kv_a_norm": KV_LORA, "conv_b": CONV_DIM, "dt_bias": SSM_HEADS,
         "a_log": SSM_HEADS, "d_skip": SSM_HEADS, "ssm_norm": SSM_WIDTH, "norm_mix_post": D_MODEL,
         "norm_mlp_pre": D_MODEL, "norm_mlp_post": D_MODEL}
WEIGHT_ORDER = ("meta_tokens", "norm_mix_pre", "w_in", "q_a_norm", "w_q_up", "kv_a_norm", "w_kv_up", "conv_w", "conv_b",
                "dt_bias", "a_log", "d_skip", "ssm_norm", "w_out", "norm_mix_post", "norm_mlp_pre", "w_mlp_up",
                "w_mlp_down", "norm_mlp_post")
PACK_BLOCK = 512


def _shard_shape(name):
    shape, ax = BIG[name]
    return tuple(d // N_DEV if a == ax else d for a, d in enumerate(shape))


BIG_ROWS = -(-sum(math.prod(_shard_shape(n)) for n in BIG) // (LANE * PACK_BLOCK)) * PACK_BLOCK
SMALL_ROWS = -(-sum(SMALL.values()) // (LANE * 8)) * 8


def _pack(flats, rows):
    v = jnp.concatenate([f.reshape(-1) for f in flats])
    return jnp.pad(v, (0, rows * LANE - v.shape[0])).reshape(rows, LANE)


def _unpack(packed, shapes):
    v = packed.reshape(-1)
    out, o = [], 0
    for s in shapes:
        n = math.prod(s)
        out.append(v[o:o + n].reshape(s))
        o += n
    return out


def _to_chunks(name, full):
    shape, ax = BIG[name]
    if ax == 0:
        return full.reshape(N_DEV, -1)
    k, n = shape
    return full.reshape(k, N_DEV, n // N_DEV).transpose(1, 0, 2).reshape(N_DEV, -1)


def _from_shards(name, shards):
    shape, ax = BIG[name]
    if ax == 0:
        return shards.reshape(shape)
    return shards.transpose(1, 0, 2).reshape(shape)


def _peer(k):
    x, y, c = lax.axis_index("x"), lax.axis_index("y"), lax.axis_index("c")
    px = 1 - x if k & 4 else x
    py = 1 - y if k & 2 else y
    pc = 1 - c if k & 1 else c
    return (px, py, pc), 4 * px + 2 * py + pc


def _all_gather(shard):
    rows = shard.shape[0]

    def body(x_ref, out_ref, send_sems, recv_sems, local_sem):
        _, me = _peer(0)
        sibling, _ = _peer(1)
        chips = (4, 2, 6)

        def copy(sem, block, to, src=None):
            return pltpu.make_async_remote_copy(
                src_ref=out_ref.at[block] if src is None else src, dst_ref=out_ref.at[block],
                send_sem=send_sems.at[sem], recv_sem=recv_sems.at[sem], device_id=to, device_id_type=MESH)

        mine = pltpu.make_async_copy(x_ref, out_ref.at[me], local_sem)
        mine.start()
        first = [copy(0, me, sibling, src=x_ref)]
        first += [copy(1 + n, me, _peer(k)[0], src=x_ref) for n, k in enumerate(chips)]
        for cp in first:
            cp.start()
        passed = [copy(4 + n, _peer(k)[1], sibling) for n, k in enumerate(chips)]
        for n, k in enumerate(chips):
            copy(1 + n, _peer(k)[1], sibling).wait_recv()
            passed[n].start()
        copy(0, _peer(1)[1], sibling).wait_recv()
        for n, k in enumerate(chips):
            copy(4 + n, _peer(k | 1)[1], sibling).wait_recv()
        for cp in first + passed:
            cp.wait_send()
        mine.wait()

    return pl.pallas_call(
        body, name="all_gather_weights", out_shape=_sds((N_DEV, rows, LANE), shard.dtype),
        in_specs=[ANY], out_specs=ANY,
        scratch_shapes=[pltpu.SemaphoreType.DMA((7,)), pltpu.SemaphoreType.DMA((7,)), pltpu.SemaphoreType.DMA],
    )(shard)


def _exchange(big, small):
    def body(big_ref, small_ref, obig_ref, osmall_ref, send_sems, recv_sems, local_sems):
        _, me = _peer(0)
        own = [pltpu.make_async_copy(big_ref.at[me], obig_ref.at[me], local_sems.at[0]),
               pltpu.make_async_copy(small_ref, osmall_ref.at[me], local_sems.at[1])]
        for cp in own:
            cp.start()

        def copies(k):
            to, idx = _peer(k)
            kw = dict(device_id=to, device_id_type=MESH)
            return [pltpu.make_async_remote_copy(src_ref=big_ref.at[idx], dst_ref=obig_ref.at[me],
                                                 send_sem=send_sems.at[2 * k - 2], recv_sem=recv_sems.at[2 * k - 2], **kw),
                    pltpu.make_async_remote_copy(src_ref=small_ref, dst_ref=osmall_ref.at[me],
                                                 send_sem=send_sems.at[2 * k - 1], recv_sem=recv_sems.at[2 * k - 1], **kw)]

        def arrivals(k):
            to, idx = _peer(k)
            kw = dict(device_id=to, device_id_type=MESH)
            return [pltpu.make_async_remote_copy(src_ref=big_ref.at[me], dst_ref=obig_ref.at[idx],
                                                 send_sem=send_sems.at[2 * k - 2], recv_sem=recv_sems.at[2 * k - 2], **kw),
                    pltpu.make_async_remote_copy(src_ref=small_ref, dst_ref=osmall_ref.at[idx],
                                                 send_sem=send_sems.at[2 * k - 1], recv_sem=recv_sems.at[2 * k - 1], **kw)]

        sent = [cp for k in range(1, N_DEV) for cp in copies(k)]
        for cp in sent:
            cp.start()
        for k in range(1, N_DEV):
            for cp in arrivals(k):
                cp.wait_recv()
        for cp in sent:
            cp.wait_send()
        for cp in own:
            cp.wait()

    n = 2 * (N_DEV - 1)
    return pl.pallas_call(
        body, name="exchange_grads",
        out_shape=[_sds(big.shape, big.dtype), _sds((N_DEV,) + small.shape, small.dtype)],
        in_specs=[ANY, ANY], out_specs=[ANY, ANY],
        scratch_shapes=[pltpu.SemaphoreType.DMA((n,)), pltpu.SemaphoreType.DMA((n,)), pltpu.SemaphoreType.DMA((2,))],
    )(big, small)


def _reduce_adamw(recv, w, m, v):
    rows = w.shape[0]
    blk = min(PACK_BLOCK, rows)
    c1 = 1.0 - ADAM_B1 ** ADAM_STEP
    c2 = 1.0 - ADAM_B2 ** ADAM_STEP

    def body(r_ref, w_ref, m_ref, v_ref, g_ref, d_ref, nm_ref, nv_ref):
        g = r_ref[0].astype(F32)
        for s in range(1, N_DEV):
            g = g + r_ref[s].astype(F32)
        g_ref[...] = g
        m_ = ADAM_B1 * m_ref[...] + (1.0 - ADAM_B1) * g
        v_ = ADAM_B2 * v_ref[...] + (1.0 - ADAM_B2) * (g * g)
        nm_ref[...] = m_
        nv_ref[...] = v_
        d_ref[...] = -ADAM_LR * ((m_ / c1) / (jnp.sqrt(v_ / c2) + ADAM_EPS) + ADAM_WD * w_ref[...])

    spec = _rowspec(blk, LANE)
    return pl.pallas_call(
        body, name="reduce_adamw", grid=(rows // blk,),
        in_specs=[pl.BlockSpec((N_DEV, blk, LANE), lambda i: (0, i, 0)), spec, spec, spec],
        out_specs=[spec] * 4, out_shape=[_sds((rows, LANE), F32)] * 4,
        compiler_params=_cp(("arbitrary",)),
    )(recv, w, m, v)


def kernel(x, meta_tokens, norm_mix_pre, w_in, q_a_norm, w_q_up, kv_a_norm, w_kv_up, conv_w, conv_b, dt_bias, a_log, d_skip, ssm_norm, w_out, norm_mix_post, norm_mlp_pre, w_mlp_up, w_mlp_down, norm_mlp_post, loss_target, m_meta_tokens, m_norm_mix_pre, m_w_in, m_q_a_norm, m_w_q_up, m_kv_a_norm, m_w_kv_up, m_conv_w, m_conv_b, m_dt_bias, m_a_log, m_d_skip, m_ssm_norm, m_w_out, m_norm_mix_post, m_norm_mlp_pre, m_w_mlp_up, m_w_mlp_down, m_norm_mlp_post, v_meta_tokens, v_norm_mix_pre, v_w_in, v_q_a_norm, v_w_q_up, v_kv_a_norm, v_w_kv_up, v_conv_w, v_conv_b, v_dt_bias, v_a_log, v_d_skip, v_ssm_norm, v_w_out, v_norm_mix_post, v_norm_mlp_pre, v_w_mlp_up, v_w_mlp_down, v_norm_mlp_post):
    w = dict(meta_tokens=meta_tokens, norm_mix_pre=norm_mix_pre, w_in=w_in, q_a_norm=q_a_norm, w_q_up=w_q_up,
             kv_a_norm=kv_a_norm, w_kv_up=w_kv_up, conv_w=conv_w, conv_b=conv_b, dt_bias=dt_bias, a_log=a_log,
             d_skip=d_skip, ssm_norm=ssm_norm, w_out=w_out, norm_mix_post=norm_mix_post, norm_mlp_pre=norm_mlp_pre,
             w_mlp_up=w_mlp_up, w_mlp_down=w_mlp_down, norm_mlp_post=norm_mlp_post)
    m = dict(meta_tokens=m_meta_tokens, norm_mix_pre=m_norm_mix_pre, w_in=m_w_in, q_a_norm=m_q_a_norm, w_q_up=m_w_q_up,
             kv_a_norm=m_kv_a_norm, w_kv_up=m_w_kv_up, conv_w=m_conv_w, conv_b=m_conv_b, dt_bias=m_dt_bias,
             a_log=m_a_log, d_skip=m_d_skip, ssm_norm=m_ssm_norm, w_out=m_w_out, norm_mix_post=m_norm_mix_post,
             norm_mlp_pre=m_norm_mlp_pre, w_mlp_up=m_w_mlp_up, w_mlp_down=m_w_mlp_down, norm_mlp_post=m_norm_mlp_post)
    v = dict(meta_tokens=v_meta_tokens, norm_mix_pre=v_norm_mix_pre, w_in=v_w_in, q_a_norm=v_q_a_norm, w_q_up=v_w_q_up,
             kv_a_norm=v_kv_a_norm, w_kv_up=v_w_kv_up, conv_w=v_conv_w, conv_b=v_conv_b, dt_bias=v_dt_bias,
             a_log=v_a_log, d_skip=v_d_skip, ssm_norm=v_ssm_norm, w_out=v_w_out, norm_mix_post=v_norm_mix_post,
             norm_mlp_pre=v_norm_mlp_pre, w_mlp_up=v_w_mlp_up, w_mlp_down=v_w_mlp_down, norm_mlp_post=v_norm_mlp_post)
    big_names = [n for n in WEIGHT_ORDER if n in BIG]
    small_names = [n for n in WEIGHT_ORDER if n in SMALL]
    shard_shapes = [_shard_shape(n) for n in big_names]

    f32_names = ("conv_w", "meta_tokens")
    mm_names = [n for n in big_names if n not in f32_names]
    bits = [lax.bitcast_convert_type(w[n].astype(F32), BF16) for n in f32_names]
    payload = _pack([w[n].astype(BF16) for n in mm_names] + bits, BIG_ROWS)
    flat = _all_gather(payload).reshape(N_DEV, -1)
    p = {n: w[n].reshape(-1) for n in small_names}
    o = 0
    for n in mm_names:
        s = _shard_shape(n)
        p[n] = _from_shards(n, flat[:, o:o + math.prod(s)].reshape((N_DEV,) + s))
        o += math.prod(s)
    for n in f32_names:
        s = _shard_shape(n)
        shards = lax.bitcast_convert_type(flat[:, o:o + 2 * math.prod(s)].reshape((N_DEV,) + s + (2,)), F32)
        p[n] = _from_shards(n, shards)
        o += 2 * math.prod(s)
    head = jnp.concatenate([jnp.zeros((PADF, D_MODEL), F32), p["meta_tokens"]], axis=0)
    f = _local_forward(head, x[0], loss_target[0], p)
    dx, g = _local_backward(head, x[0], f, p)
    grad_x = dx[None]
    loss = lax.psum(f["loss"], ("x", "y", "c"))

    chunks = jnp.concatenate([_to_chunks(n, g[n]) for n in big_names], axis=1).astype(BF16)
    chunks = jnp.pad(chunks, ((0, 0), (0, BIG_ROWS * LANE - chunks.shape[1]))).reshape(N_DEV, BIG_ROWS, LANE)
    small = _pack([g[n] for n in small_names], SMALL_ROWS)
    recv_big, recv_small = _exchange(chunks, small)

    outs = {}
    for names, recv, rows, shapes in ((big_names, recv_big, BIG_ROWS, shard_shapes),
                                      (small_names, recv_small, SMALL_ROWS, [(SMALL[n],) for n in small_names])):
        packed = [_pack([d[n].astype(F32) for n in names], rows) for d in (w, m, v)]
        res = _reduce_adamw(recv, *packed)
        for kind, arr in zip(("grad", "delta", "new_m", "new_v"), res):
            for n, val in zip(names, _unpack(arr, shapes)):
                outs[kind, n] = val.reshape(w[n].shape)
    return (loss, grad_x) + tuple(outs[kind, n] for kind in ("grad", "delta", "new_m", "new_v") for n in WEIGHT_ORDER)
```

```python
import math

import jax
import jax.numpy as jnp
import numpy as np
from jax import lax
from jax.experimental import pallas as pl
from jax.experimental.pallas import tpu as pltpu

F32 = jnp.float32
BF16 = jnp.bfloat16

D_MODEL = 1024
N_META = 16
EPS = 1e-6
ATT_HEADS = 8
Q_LORA = 384
KV_LORA = 256
QK_NOPE = 128
QK_ROPE = 64
V_HEAD = 128
ROPE_THETA = 10000.0
SSM_HEADS = 16
SSM_HEAD_DIM = 64
SSM_WIDTH = 1024
SSM_GROUPS = 2
SSM_STATE = 128
CONV_K = 4
CHUNK = 128
CONV_DIM = 1536
D_FF = 4096
IN_SPLITS = (Q_LORA, KV_LORA, QK_ROPE, SSM_WIDTH, CONV_DIM, SSM_HEADS)
IN_WIDTH = sum(IN_SPLITS)
ADAM_LR, ADAM_B1, ADAM_B2, ADAM_EPS, ADAM_WD, ADAM_STEP = 0.001, 0.9, 0.999, 1e-08, 0.01, 10

LANE = 128
ROWB = 512
PADF = ROWB - N_META
HEADW = 256
PC_Q, PC_KV, PC_KR, PC_Z, PC_XBC, PC_DT, PROJ_W = 0, 384, 640, 896, 1920, 3456, 3584
NEG = -1e30
N_DEV = 8
VMEM_LIMIT = 56 * 1024 * 1024
MESH = pl.DeviceIdType.MESH


def _cp(sem, vmem=VMEM_LIMIT, **kw):
    return pltpu.CompilerParams(dimension_semantics=sem, vmem_limit_bytes=vmem, **kw)


def _dot(a, b, dims=((1,), (0,))):
    return lax.dot_general(a, b, (dims, ((), ())), preferred_element_type=F32)


def _bdot(a, b, dims=((1,), (0,))):
    return _dot(a.astype(BF16), b.astype(BF16), dims)


NT = ((1,), (1,))


def _rms_fwd(x, w):
    r = lax.rsqrt(jnp.mean(x * x, axis=-1, keepdims=True) + EPS)
    return (x * r) * w


def _rms_bwd(x, w, dy):
    r = lax.rsqrt(jnp.mean(x * x, axis=-1, keepdims=True) + EPS)
    xh = x * r
    g = dy * w
    dx = r * (g - xh * jnp.mean(g * xh, axis=-1, keepdims=True))
    return dx, dy * xh


def _sigmoid(x):
    return 0.5 * jnp.tanh(0.5 * x) + 0.5


def _colsum8(x):
    t, c = x.shape
    return jnp.sum(x.reshape(t // 8, 8, c), axis=0)


def _rowspec(t, c, cb=0):
    return pl.BlockSpec((t, c), lambda i: (i, cb))


def _fullspec(shape):
    n = len(shape)
    return pl.BlockSpec(shape, lambda i: (0,) * n)


def _sds(shape, dt):
    return jax.ShapeDtypeStruct(shape, dt)


def _acc(ref, val):
    @pl.when(pl.program_id(0) == 0)
    def _():
        ref[...] = jnp.zeros_like(ref)

    ref[...] += val


def _xspec():
    return pl.BlockSpec((ROWB, D_MODEL), lambda i: (jnp.maximum(i - 1, 0), 0))


def _h_block(head_ref, x_ref):
    return jnp.where(pl.program_id(0) == 0, head_ref[...], x_ref[...])


def _norm_in_proj(head, x, g, w_all):
    lp = head.shape[0] + x.shape[0]

    def body(head_ref, x_ref, g_ref, w_ref, hn_ref, cq_ref, ckv_ref, kr_ref, z_ref, xbc_ref, dt_ref):
        hn = _rms_fwd(_h_block(head_ref, x_ref), g_ref[...]).astype(BF16)
        hn_ref[...] = hn
        p = _dot(hn, w_ref[...])
        cq_ref[...] = p[:, PC_Q:PC_KV]
        ckv_ref[...] = p[:, PC_KV:PC_KR]
        kr_ref[...] = p[:, PC_KR:PC_Z]
        z_ref[...] = p[:, PC_Z:PC_XBC]
        xbc_ref[...] = p[:, PC_XBC:PC_DT]
        dt_ref[...] = p[:, PC_DT:PROJ_W]

    widths = (Q_LORA, KV_LORA, HEADW, SSM_WIDTH, CONV_DIM, LANE)
    return pl.pallas_call(
        body, name="norm_in_proj", grid=(lp // ROWB,),
        in_specs=[_fullspec((ROWB, D_MODEL)), _xspec(), _fullspec((1, D_MODEL)), _fullspec((D_MODEL, PROJ_W))],
        out_specs=[_rowspec(ROWB, D_MODEL)] + [_rowspec(ROWB, w) for w in widths],
        out_shape=[_sds((lp, D_MODEL), BF16)] + [_sds((lp, w), F32) for w in widths],
        compiler_params=_cp(("arbitrary",)),
    )(head, x, g, w_all)


def _rope(x, cos, sa, sb):
    w = x.shape[1]
    return x * cos + pltpu.roll(x, w - 32, 1) * sa + pltpu.roll(x, 32, 1) * sb


def _rope_t(dy, cos, sa, sb):
    w = dy.shape[1]
    return dy * cos + pltpu.roll(dy * sa, 32, 1) + pltpu.roll(dy * sb, w - 32, 1)


def _tile8(t):
    return jnp.concatenate([t] * ATT_HEADS, axis=1)


def _qkv(cq, ckv, kr, cos, sa, sb, gq, gkv, wq, wk, wv):
    lp = cq.shape[0]
    qw = ATT_HEADS * HEADW

    def body(cq_ref, ckv_ref, kr_ref, cos_ref, sa_ref, sb_ref, gq_ref, gkv_ref, wq_ref, wk_ref, wv_ref,
             q_ref, k_ref, v_ref, cqn_ref, ckvn_ref):
        cos_, sa_, sb_ = cos_ref[...], sa_ref[...], sb_ref[...]
        cqn = _rms_fwd(cq_ref[...], gq_ref[...]).astype(BF16)
        ckvn = _rms_fwd(ckv_ref[...], gkv_ref[...]).astype(BF16)
        cqn_ref[...] = cqn
        ckvn_ref[...] = ckvn
        q = _dot(cqn, wq_ref[...])
        q_ref[...] = (_rope(q, _tile8(cos_), _tile8(sa_), _tile8(sb_)) * Q_PRESCALE).astype(BF16)
        k = _dot(ckvn, wk_ref[...]) + _tile8(_rope(kr_ref[...], cos_, sa_, sb_))
        k_ref[...] = k.astype(BF16)
        lanes = lax.broadcasted_iota(jnp.int32, (1, qw), 1)
        ones = ((lanes % HEADW) >= V_HEAD).astype(F32)
        v_ref[...] = (_dot(ckvn, wv_ref[...]) + ones).astype(BF16)

    return pl.pallas_call(
        body, name="qkv", grid=(lp // ROWB,),
        in_specs=[_rowspec(ROWB, Q_LORA), _rowspec(ROWB, KV_LORA), _rowspec(ROWB, HEADW)]
        + [_rowspec(ROWB, HEADW)] * 3
        + [_fullspec((1, Q_LORA)), _fullspec((1, KV_LORA)), _fullspec((Q_LORA, qw)), _fullspec((KV_LORA, qw)),
           _fullspec((KV_LORA, qw))],
        out_specs=[_rowspec(ROWB, qw), _rowspec(ROWB, qw), _rowspec(ROWB, qw),
                   _rowspec(ROWB, Q_LORA), _rowspec(ROWB, KV_LORA)],
        out_shape=[_sds((lp, qw), BF16), _sds((lp, qw), BF16), _sds((lp, qw), BF16),
                   _sds((lp, Q_LORA), BF16), _sds((lp, KV_LORA), BF16)],
        compiler_params=_cp(("arbitrary",)),
    )(cq, ckv, kr, cos, sa, sb, gq, gkv, wq, wk, wv)


ATT_SCALE = (QK_NOPE + QK_ROPE) ** -0.5
LOG2E = 1.4426950408889634
LN2 = 0.6931471805599453
Q_PRESCALE = ATT_SCALE * LOG2E
KVB = 512


def _att_ok(qrow, krow):
    return (krow <= qrow) & ((krow >= PADF) | (qrow < PADF))


def _lanes(x, n):
    return x if n == 1 else jnp.concatenate([x] * n, axis=1)


def _pair_loop(lo, hi, tile, unrolls=(2,)):
    for u in tuple(unrolls) + (1,):
        n = jnp.maximum(hi - lo, 0)
        trips = n // u

        def many(t, c, u=u, lo=lo):
            for d in range(u):
                tile(lo + u * t + d)
            return c

        lax.fori_loop(0, trips, many, 0)
        lo = lo + trips * u


def _flash_fwd(q, k, v):
    lp = q.shape[0]
    nq = lp // ROWB
    rep = KVB // LANE

    def body(q_ref, k_ref, v_ref, o_ref, lse_ref, acc, m_s):
        i = pl.program_id(1)
        qb = q_ref[...]
        m_s[...] = jnp.full_like(m_s, NEG)
        acc[...] = jnp.zeros_like(acc)

        def tile(j, masked):
            off = pl.multiple_of(j * KVB, KVB)
            kb = k_ref[pl.ds(off, KVB), :]
            vb = v_ref[pl.ds(off, KVB), :]
            s = _dot(qb, kb, NT)
            if masked:
                qrow = i * ROWB + lax.broadcasted_iota(jnp.int32, s.shape, 0)
                krow = j * KVB + lax.broadcasted_iota(jnp.int32, s.shape, 1)
                s = jnp.where(_att_ok(qrow, krow), s, NEG)
            m_prev = m_s[...]
            m_new = jnp.maximum(m_prev, jnp.max(s, axis=1, keepdims=True))
            alpha = jnp.exp2(m_prev - m_new)
            p = jnp.exp2(s - _lanes(m_new, rep))
            acc[...] = _lanes(alpha, 2) * acc[...] + _dot(p.astype(BF16), vb)
            m_s[...] = m_new

        tile(0, True)
        _pair_loop(1, i, lambda j: tile(j, False), (8, 2))

        @pl.when(i > 0)
        def _():
            tile(i, True)

        l = acc[:, V_HEAD:]
        o_ref[...] = (acc[:, :V_HEAD] / l).astype(BF16)
        lse_ref[0] = (m_s[...] + jnp.log2(l)).T[0:1, :]

    return pl.pallas_call(
        body, name="flash_fwd", grid=(ATT_HEADS, nq),
        in_specs=[pl.BlockSpec((ROWB, HEADW), lambda h, i: (i, h)),
                  pl.BlockSpec((lp, HEADW), lambda h, i: (0, h)),
                  pl.BlockSpec((lp, HEADW), lambda h, i: (0, h))],
        out_specs=[pl.BlockSpec((ROWB, V_HEAD), lambda h, i: (i, h)),
                   pl.BlockSpec((1, 1, ROWB), lambda h, i: (h, 0, i))],
        out_shape=[_sds((lp, ATT_HEADS * V_HEAD), BF16), _sds((ATT_HEADS, 1, lp), F32)],
        scratch_shapes=[pltpu.VMEM((ROWB, HEADW), F32), pltpu.VMEM((ROWB, LANE), F32)],
        compiler_params=_cp(("arbitrary", "arbitrary")),
    )(q, k, v)


def _silu(x):
    return x * _sigmoid(x)


CONV_ROWS = 64


def _conv_fwd(xbc, cw, cb):
    lp, c = xbc.shape
    t8 = ROWB // 8

    def body(x_ref, prev_ref, w_ref, b_ref, o_ref, buf):
        i = pl.program_id(0)
        buf[pl.ds(0, 8), :] = jnp.where(i > 0, prev_ref[...], 0.0)
        buf[pl.ds(8, ROWB), :] = x_ref[...]

        def strip(s, carry):
            cs = pl.ds(pl.multiple_of(s * LANE, LANE), LANE)
            w, b = w_ref[:, cs], b_ref[:, cs]
            for r0 in range(0, ROWB, CONV_ROWS):
                pre = b + sum(w[kk:kk + 1, :] * buf[pl.ds(8 - (CONV_K - 1) + kk + r0, CONV_ROWS), cs]
                              for kk in range(CONV_K))
                o_ref[pl.ds(r0, CONV_ROWS), cs] = _silu(pre)
            return carry

        lax.fori_loop(0, c // LANE, strip, 0)

    return pl.pallas_call(
        body, name="conv_fwd", grid=(lp // ROWB,),
        in_specs=[_rowspec(ROWB, c), pl.BlockSpec((8, c), lambda i: (jnp.maximum(i * t8 - 1, 0), 0)),
                  _fullspec((8, c)), _fullspec((1, c))],
        out_specs=_rowspec(ROWB, c), out_shape=_sds((lp, c), F32),
        scratch_shapes=[pltpu.VMEM((ROWB + 8, c), F32)],
        compiler_params=_cp(("arbitrary",)),
    )(xbc, xbc, cw, cb)


def _expand_mat():
    r = np.arange(LANE)[:, None]
    c = np.arange(SSM_WIDTH)[None, :]
    return jnp.asarray((c // SSM_HEAD_DIM == r).astype(np.float32))


def _tri_mat():
    i = np.arange(CHUNK)
    return jnp.asarray((i[:, None] >= i[None, :]).astype(np.float32))


def _x3(m, axis):
    return jnp.concatenate([m.astype(BF16)] * 3, axis=axis)


def _split3(x):
    hi = x.astype(BF16)
    r = x - hi.astype(F32)
    mid = r.astype(BF16)
    return hi, mid, (r - mid.astype(F32)).astype(BF16)


def _dot01_r(x, m3):
    return _dot(jnp.concatenate(_split3(x), axis=1), m3)


def _dot01_l(m3, x):
    return _dot(m3, jnp.concatenate(_split3(x), axis=0))


def _ssd_prep(dtr_ref, bias_ref, alog_ref, tri3, c, seq_rows):
    rows = c * CHUNK + lax.broadcasted_iota(jnp.int32, (CHUNK, LANE), 0)
    lanes = lax.broadcasted_iota(jnp.int32, (CHUNK, LANE), 1)
    valid = (rows >= PADF) & (rows < PADF + seq_rows) & (lanes < SSM_HEADS)
    dtr = dtr_ref[...] + bias_ref[...]
    sp = jnp.maximum(dtr, 0.0) + jnp.log(1.0 + jnp.exp(-jnp.abs(dtr)))
    dt = jnp.where(valid, sp, 0.0)
    a = -jnp.exp(alog_ref[...])
    acol = _dot01_l(tri3, dt * a)
    return dt, a, acol, valid, dtr


def _row16(v):
    return jnp.broadcast_to(v, (16, v.shape[1]))


def _ssd_fwd(xbc_act, dtr, dt_bias, a_log, seq_rows):
    lp = xbc_act.shape[0]
    nc = lp // CHUNK
    gw = SSM_WIDTH // SSM_GROUPS
    hpg = SSM_HEADS // SSM_GROUPS

    def body(x_ref, b_ref, c_ref, dtr_ref, bias_ref, alog_ref, tri_ref, tri3_ref, ex3_ref, y_ref, hp_ref, h_s):
        c = pl.program_id(0)

        @pl.when(c == 0)
        def _():
            h_s[...] = jnp.zeros_like(h_s)

        ex3 = ex3_ref[...]
        dt, a, acol, _, _ = _ssd_prep(dtr_ref, bias_ref, alog_ref, tri3_ref[...], c, seq_rows)
        arow = acol.T
        dtrow = dt.T
        alast = acol[CHUNK - 1:CHUNK, :]
        e_all = _dot01_r(jnp.exp(acol), ex3)
        wx_all = _dot01_r(jnp.exp(alast - acol) * dt, ex3)
        dec_all = _dot01_r(_row16(jnp.exp(alast)), ex3)[0:1, :]
        causal = tri_ref[...] > 0.5
        hp_ref[0] = h_s[...]
        for g in range(SSM_GROUPS):
            gs = slice(g * gw, (g + 1) * gw)
            bg = b_ref[:, g * SSM_STATE:(g + 1) * SSM_STATE]
            cg = c_ref[:, g * SSM_STATE:(g + 1) * SSM_STATE].astype(BF16)
            xg = x_ref[:, gs]
            hg = h_s[:, gs]
            gm = _bdot(cg, bg, NT)
            y_off = _bdot(cg, hg) * e_all[:, gs]
            for r in range(hpg):
                hd = g * hpg + r
                seg = acol[:, hd:hd + 1] - arow[hd:hd + 1, :]
                lm = jnp.where(causal, jnp.exp(jnp.where(causal, seg, 0.0)), 0.0)
                w = gm * lm * dtrow[hd:hd + 1, :]
                cs = slice(r * SSM_HEAD_DIM, (r + 1) * SSM_HEAD_DIM)
                y_ref[:, pl.ds(hd * SSM_HEAD_DIM, SSM_HEAD_DIM)] = _bdot(w, xg[:, cs]) + y_off[:, cs]
            st = _bdot(bg.T, xg * wx_all[:, gs])
            h_s[:, gs] = hg * dec_all[:, gs] + st

    xs_spec = pl.BlockSpec((CHUNK, SSM_WIDTH), lambda c: (c, 0))
    b_spec = pl.BlockSpec((CHUNK, 2 * SSM_STATE), lambda c: (c, SSM_WIDTH // (2 * SSM_STATE)))
    c_spec = pl.BlockSpec((CHUNK, 2 * SSM_STATE), lambda c: (c, SSM_WIDTH // (2 * SSM_STATE) + 1))
    return pl.pallas_call(
        body, name="ssd_fwd", grid=(nc,),
        in_specs=[xs_spec, b_spec, c_spec, pl.BlockSpec((CHUNK, LANE), lambda c: (c, 0)),
                  _fullspec((1, LANE)), _fullspec((1, LANE)), _fullspec((CHUNK, CHUNK)), _fullspec((CHUNK, 3 * CHUNK)),
                  _fullspec((3 * LANE, SSM_WIDTH))],
        out_specs=[xs_spec, pl.BlockSpec((1, SSM_STATE, SSM_WIDTH), lambda c: (c, 0, 0))],
        out_shape=[_sds((lp, SSM_WIDTH), F32), _sds((nc, SSM_STATE, SSM_WIDTH), F32)],
        scratch_shapes=[pltpu.VMEM((SSM_STATE, SSM_WIDTH), F32)],
        compiler_params=_cp(("arbitrary",)),
    )(xbc_act, xbc_act, xbc_act, dtr, dt_bias, a_log, _tri_mat(), _x3(_tri_mat(), 1), _x3(_expand_mat(), 0))


def _group_mean(x):
    gw = SSM_WIDTH // SSM_GROUPS
    parts = [jnp.broadcast_to(jnp.mean(x[:, g * gw:(g + 1) * gw], axis=-1, keepdims=True), (x.shape[0], gw))
             for g in range(SSM_GROUPS)]
    return jnp.concatenate(parts, axis=1)


def _ssd_post(y, xbc_act, z, dskip, gnorm):
    lp = y.shape[0]

    def body(y_ref, x_ref, z_ref, d_ref, g_ref, o_ref):
        z_ = z_ref[...]
        gt = (y_ref[...] + d_ref[...] * x_ref[...]) * _silu(z_)
        r = lax.rsqrt(_group_mean(gt * gt) + EPS)
        o_ref[...] = ((gt * r) * g_ref[...]).astype(BF16)

    return pl.pallas_call(
        body, name="ssd_post", grid=(lp // ROWB,),
        in_specs=[_rowspec(ROWB, SSM_WIDTH)] * 3 + [_fullspec((1, SSM_WIDTH))] * 2,
        out_specs=_rowspec(ROWB, SSM_WIDTH), out_shape=_sds((lp, SSM_WIDTH), BF16),
        compiler_params=_cp(("arbitrary",)),
    )(y, xbc_act, z, dskip, gnorm)


def _out_proj(att, ssm, head, x, w_out, g_post):
    lp = att.shape[0]

    def body(a_ref, s_ref, head_ref, x_ref, w_ref, g_ref, mix_ref, h1_ref):
        mix = _dot(a_ref[...], w_ref[pl.ds(0, 1024), :]) + _dot(s_ref[...], w_ref[pl.ds(1024, 1024), :])
        mix_ref[...] = mix
        h1_ref[...] = _h_block(head_ref, x_ref) + _rms_fwd(mix, g_ref[...])

    return pl.pallas_call(
        body, name="out_proj", grid=(lp // ROWB,),
        in_specs=[_rowspec(ROWB, 1024)] * 2 + [_fullspec((ROWB, D_MODEL)), _xspec(), _fullspec((2048, D_MODEL)),
                                               _fullspec((1, D_MODEL))],
        out_specs=[_rowspec(ROWB, D_MODEL)] * 2, out_shape=[_sds((lp, D_MODEL), F32)] * 2,
        compiler_params=_cp(("arbitrary",)),
    )(att, ssm, head, x, w_out, g_post)


def _resident(w_hbm, w_vmem, sem):
    @pl.when(pl.program_id(0) == 0)
    def _():
        cp = pltpu.make_async_copy(w_hbm, w_vmem, sem)
        cp.start()
        cp.wait()


ANY = pl.BlockSpec(memory_space=pl.ANY)


def _mlp_fwd(h1, tgt, w_up, w_down, g_pre, g_post, seq_rows):
    lp = h1.shape[0]

    def body(h1_ref, t_ref, wu_hbm, wd_hbm, gpre_ref, gpost_ref, hn2_ref, f_ref, dh2_ref, loss_ref, wu, wd, sems):
        _resident(wu_hbm, wu, sems.at[0])
        _resident(wd_hbm, wd, sems.at[1])
        i = pl.program_id(0)
        h1_ = h1_ref[...]
        hn2 = _rms_fwd(h1_, gpre_ref[...]).astype(BF16)
        hn2_ref[...] = hn2
        u = jnp.maximum(_dot(hn2, wu[...]), 0.0)
        f = _dot((u * u).astype(BF16), wd[...])
        f_ref[...] = f
        h2 = h1_ + _rms_fwd(f, gpost_ref[...])
        rows = i * ROWB + lax.broadcasted_iota(jnp.int32, (ROWB, 1), 0)
        real = (rows >= PADF + N_META) & (rows < PADF + seq_rows)
        err = jnp.where(real, h2 - t_ref[...], 0.0)
        dh2_ref[...] = err * (1.0 / D_MODEL)
        _acc(loss_ref, _colsum8(err * err))

    return pl.pallas_call(
        body, name="mlp_fwd", grid=(lp // ROWB,),
        in_specs=[_rowspec(ROWB, D_MODEL), _xspec()] + [ANY, ANY] + [_fullspec((1, D_MODEL))] * 2,
        out_specs=[_rowspec(ROWB, D_MODEL)] * 3 + [_fullspec((8, D_MODEL))],
        out_shape=[_sds((lp, D_MODEL), BF16), _sds((lp, D_MODEL), F32), _sds((lp, D_MODEL), F32), _sds((8, D_MODEL), F32)],
        scratch_shapes=[pltpu.VMEM((D_MODEL, D_FF), BF16), pltpu.VMEM((D_FF, D_MODEL), BF16), pltpu.SemaphoreType.DMA((2,))],
        compiler_params=_cp(("arbitrary",)),
    )(h1, tgt, w_up, w_down, g_pre, g_post)


def _pad_cols(w, width):
    return jnp.pad(w, ((0, 0), (0, width - w.shape[1])))


def _layout_weights(w_in, w_q_up, w_kv_up):
    o = np.cumsum((0,) + IN_SPLITS)
    pieces = [w_in[:, o[k]:o[k + 1]] for k in range(6)]
    kr = jnp.pad(pieces[2], ((0, 0), (QK_NOPE, HEADW - QK_NOPE - QK_ROPE)))
    w_all = jnp.concatenate([pieces[0], pieces[1], kr, pieces[3], pieces[4], _pad_cols(pieces[5], LANE)], axis=1)
    wq = jnp.pad(w_q_up.reshape(Q_LORA, ATT_HEADS, QK_NOPE + QK_ROPE), ((0, 0), (0, 0), (0, HEADW - QK_NOPE - QK_ROPE)))
    wkv = w_kv_up.reshape(KV_LORA, ATT_HEADS, QK_NOPE + V_HEAD)
    wk = jnp.pad(wkv[:, :, :QK_NOPE], ((0, 0), (0, 0), (0, HEADW - QK_NOPE)))
    wv = wkv[:, :, QK_NOPE:]
    wvp = jnp.pad(wv, ((0, 0), (0, 0), (0, HEADW - V_HEAD)))
    return (w_all, wq.reshape(Q_LORA, -1), wk.reshape(KV_LORA, -1), wv.reshape(KV_LORA, -1),
            wvp.reshape(KV_LORA, -1))


def _rope_tables(lp):
    pos = jnp.maximum(jnp.arange(lp, dtype=jnp.int32) - PADF, 0).astype(F32)
    inv_freq = ROPE_THETA ** (-jnp.arange(0, QK_ROPE, 2, dtype=F32) / QK_ROPE)
    ang = pos[:, None] * inv_freq[None, :]
    cos, sin = jnp.cos(ang), jnp.sin(ang)
    one, zero = jnp.ones((lp, QK_NOPE), F32), jnp.zeros((lp, QK_NOPE), F32)
    z32, z64 = jnp.zeros((lp, 32), F32), jnp.zeros((lp, 64), F32)
    cos_t = jnp.concatenate([one, cos, cos, jnp.ones((lp, 64), F32)], axis=1)
    sa = jnp.concatenate([zero, -sin, z32, z64], axis=1)
    sb = jnp.concatenate([zero, z32, sin, z64], axis=1)
    return cos_t, sa, sb


def _row1(v, width=None):
    v = v.reshape(1, -1).astype(F32)
    return v if width is None else _pad_cols(v, width)


def _local_forward(head, x, tgt, p):
    assert head.shape[0] == ROWB and x.shape[0] % ROWB == 0
    lp = ROWB + x.shape[0]
    seq_rows = N_META + x.shape[0]
    f = {"seq_rows": seq_rows}
    w_all, wq, wk, wv, wvp = _layout_weights(p["w_in"], p["w_q_up"], p["w_kv_up"])
    f.update(w_all=w_all, wq=wq, wk=wk, wv=wv)
    f["hn"], cq, ckv, kr, f["z"], f["xbc"], f["dtr"] = _norm_in_proj(head, x, _row1(p["norm_mix_pre"]), w_all)
    f.update(cq=cq, ckv=ckv)
    f["rope"] = _rope_tables(lp)
    f["q"], f["k"], f["v"], f["cqn"], f["ckvn"] = _qkv(cq, ckv, kr, *f["rope"], _row1(p["q_a_norm"]),
                                                   _row1(p["kv_a_norm"]), wq, wk, wvp)
    f["att"], f["lse"] = _flash_fwd(f["q"], f["k"], f["v"])
    f["cw"] = jnp.pad(p["conv_w"].astype(F32), ((0, 8 - CONV_K), (0, 0)))
    f["xact"] = _conv_fwd(f["xbc"], f["cw"], _row1(p["conv_b"]))
    f["dt_bias"], f["a_log"] = _row1(p["dt_bias"], LANE), _row1(p["a_log"], LANE)
    f["y"], f["hprev"] = _ssd_fwd(f["xact"], f["dtr"], f["dt_bias"], f["a_log"], seq_rows)
    f["dskip"] = jnp.repeat(p["d_skip"].reshape(-1).astype(F32), SSM_HEAD_DIM).reshape(1, SSM_WIDTH)
    f["ssm"] = _ssd_post(f["y"], f["xact"], f["z"], f["dskip"], _row1(p["ssm_norm"]))
    f["mix"], f["h1"] = _out_proj(f["att"], f["ssm"], head, x, p["w_out"], _row1(p["norm_mix_post"]))
    f["hn2"], f["f"], f["dh2"], loss8 = _mlp_fwd(f["h1"], tgt, p["w_mlp_up"], p["w_mlp_down"],
                                                 _row1(p["norm_mlp_pre"]), _row1(p["norm_mlp_post"]), seq_rows)
    f["loss"] = 0.5 * jnp.sum(loss8) / D_MODEL
    return f


MLPB = 256


def _mlp_bwd(dh2, f, h1, hn2, w_up, w_down, g_pre, g_post):
    lp = h1.shape[0]

    def body(dh2_ref, f_ref, h1_ref, hn2_ref, wu_hbm, wd_hbm, gpre_ref, gpost_ref,
             dh1_ref, du_ref, a_ref, df_ref, dgpre_ref, dgpost_ref, wu, wd, sems):
        _resident(wu_hbm, wu, sems.at[0])
        _resident(wd_hbm, wd, sems.at[1])
        dh2_ = dh2_ref[...]
        df, dgp = _rms_bwd(f_ref[...], gpost_ref[...], dh2_)
        dfb = df.astype(BF16)
        df_ref[...] = dfb
        da = _dot(dfb, wd[...], NT)
        u = jnp.maximum(_dot(hn2_ref[...], wu[...]), 0.0)
        a_ref[...] = (u * u).astype(BF16)
        du = (da * (2.0 * u)).astype(BF16)
        du_ref[...] = du
        dhn2 = _dot(du, wu[...], NT)
        dx, dgq = _rms_bwd(h1_ref[...], gpre_ref[...], dhn2)
        dh1_ref[...] = dh2_ + dx
        _acc(dgpre_ref, _colsum8(dgq))
        _acc(dgpost_ref, _colsum8(dgp))

    return pl.pallas_call(
        body, name="mlp_bwd", grid=(lp // MLPB,),
        in_specs=[_rowspec(MLPB, D_MODEL)] * 4 + [ANY, ANY] + [_fullspec((1, D_MODEL))] * 2,
        out_specs=[_rowspec(MLPB, D_MODEL), _rowspec(MLPB, D_FF), _rowspec(MLPB, D_FF), _rowspec(MLPB, D_MODEL),
                   _fullspec((8, D_MODEL)), _fullspec((8, D_MODEL))],
        out_shape=[_sds((lp, D_MODEL), F32), _sds((lp, D_FF), BF16), _sds((lp, D_FF), BF16), _sds((lp, D_MODEL), BF16),
                   _sds((8, D_MODEL), F32), _sds((8, D_MODEL), F32)],
        scratch_shapes=[pltpu.VMEM((D_MODEL, D_FF), BF16), pltpu.VMEM((D_FF, D_MODEL), BF16), pltpu.SemaphoreType.DMA((2,))],
        compiler_params=_cp(("arbitrary",)),
    )(dh2, f, h1, hn2, w_up, w_down, g_pre, g_post)


def _out_bwd(dh1, mix, w_out, g_post):
    lp = dh1.shape[0]

    def body(dh1_ref, mix_ref, w_ref, g_ref, dmix_ref, datt_ref, dssm_ref, dg_ref):
        dmix, dg = _rms_bwd(mix_ref[...], g_ref[...], dh1_ref[...])
        dmb = dmix.astype(BF16)
        dmix_ref[...] = dmb
        datt_ref[...] = _dot(dmb, w_ref[pl.ds(0, 1024), :], NT).astype(BF16)
        dssm_ref[...] = _dot(dmb, w_ref[pl.ds(1024, 1024), :], NT)
        _acc(dg_ref, _colsum8(dg))

    return pl.pallas_call(
        body, name="out_bwd", grid=(lp // ROWB,),
        in_specs=[_rowspec(ROWB, D_MODEL)] * 2 + [_fullspec((2048, D_MODEL)), _fullspec((1, D_MODEL))],
        out_specs=[_rowspec(ROWB, D_MODEL)] * 3 + [_fullspec((8, D_MODEL))],
        out_shape=[_sds((lp, D_MODEL), BF16), _sds((lp, 1024), BF16), _sds((lp, 1024), F32), _sds((8, D_MODEL), F32)],
        compiler_params=_cp(("arbitrary",)),
    )(dh1, mix, w_out, g_post)


def _ssd_post_bwd(dssm, y, xact, z, dskip, gnorm):
    lp = y.shape[0]

    def body(do_ref, y_ref, x_ref, z_ref, d_ref, g_ref, dy_ref, dz_ref, dg_ref, dd_ref):
        z_, x_ = z_ref[...], x_ref[...]
        sg = _sigmoid(z_)
        sz = z_ * sg
        y2 = y_ref[...] + d_ref[...] * x_
        gt = y2 * sz
        r = lax.rsqrt(_group_mean(gt * gt) + EPS)
        gh = gt * r
        do = do_ref[...]
        dgh = do * g_ref[...]
        dgt = r * (dgh - gh * _group_mean(dgh * gh))
        dy2 = dgt * sz
        dy_ref[...] = dy2
        dz_ref[...] = (dgt * y2 * (sg * (1.0 + z_ * (1.0 - sg)))).astype(BF16)
        _acc(dg_ref, _colsum8(do * gh))
        _acc(dd_ref, _colsum8(dy2 * x_))

    return pl.pallas_call(
        body, name="ssd_post_bwd", grid=(lp // ROWB,),
        in_specs=[_rowspec(ROWB, SSM_WIDTH)] * 4 + [_fullspec((1, SSM_WIDTH))] * 2,
        out_specs=[_rowspec(ROWB, SSM_WIDTH)] * 2 + [_fullspec((8, SSM_WIDTH))] * 2,
        out_shape=[_sds((lp, SSM_WIDTH), F32), _sds((lp, SSM_WIDTH), BF16), _sds((8, SSM_WIDTH), F32), _sds((8, SSM_WIDTH), F32)],
        compiler_params=_cp(("arbitrary",)),
    )(dssm, y, xact, z, dskip, gnorm)


def _ssd_bwd(dy, xact, dtr, hprev, dt_bias, a_log, dskip, seq_rows):
    lp = xact.shape[0]
    nc = lp // CHUNK
    gw = SSM_WIDTH // SSM_GROUPS
    hpg = SSM_HEADS // SSM_GROUPS
    nb = SSM_WIDTH // (2 * SSM_STATE)

    def body(dy_ref, x_ref, b_ref, c_ref, dtr_ref, hp_ref, bias_ref, alog_ref, dsk_ref, tri_ref, tri3_ref, trit3_ref,
             ex3_ref, ext3_ref, dact_ref, ddtr_ref, da_ref, dbias_ref, dh_s):
        step = pl.program_id(0)
        c = nc - 1 - step

        @pl.when(step == 0)
        def _():
            dh_s[...] = jnp.zeros_like(dh_s)

        tri = tri_ref[...]
        ex3 = ex3_ref[...]
        dt, a, acol, valid, dtr_ = _ssd_prep(dtr_ref, bias_ref, alog_ref, tri3_ref[...], c, seq_rows)
        arow = acol.T
        dtrow = dt.T
        alast = acol[CHUNK - 1:CHUNK, :]
        e_all = _dot01_r(jnp.exp(acol), ex3)
        wgt0 = jnp.exp(alast - acol)
        wgt = wgt0 * dt
        wx_all = _dot01_r(wgt, ex3)
        elast = jnp.exp(alast)
        dec_all = _dot01_r(_row16(elast), ex3)[0:1, :]
        causal = tri > 0.5
        upper = tri.T > 0.5
        lane_id = lax.broadcasted_iota(jnp.int32, (1, LANE), 1)
        sub_id = lax.broadcasted_iota(jnp.int32, (CHUNK, 1), 0)
        dacol = jnp.zeros((CHUNK, LANE), F32)
        darowf = jnp.zeros((CHUNK, LANE), F32)
        ddtrowf = jnp.zeros((CHUNK, LANE), F32)
        dwgt = jnp.zeros((CHUNK, LANE), F32)
        delast = jnp.zeros((1, LANE), F32)
        for g in range(SSM_GROUPS):
            gs = slice(g * gw, (g + 1) * gw)
            ext3_g = ext3_ref[g]
            bg = b_ref[:, g * SSM_STATE:(g + 1) * SSM_STATE]
            cg = c_ref[:, g * SSM_STATE:(g + 1) * SSM_STATE]
            bgb, cgb = bg.astype(BF16), cg.astype(BF16)
            xg = x_ref[:, gs]
            dyg = dy_ref[:, gs]
            hg = hp_ref[0, :, gs]
            dhg = dh_s[:, gs]
            hgb, dhgb = hg.astype(BF16), dhg.astype(BF16)
            gm = _dot(cgb, bgb, NT)
            gmt = _dot(bgb, cgb, NT)
            y_off = _dot(cgb, hgb) * e_all[:, gs]
            dy0 = (dyg * e_all[:, gs]).astype(BF16)
            dcg = _dot(dy0, hgb, NT)
            dh_in = _dot(cg.T.astype(BF16), dy0) + dhg * dec_all[:, gs]
            dacol = dacol + _dot01_r(dyg * y_off, ext3_g)
            xw = xg * wx_all[:, gs]
            dxw = _dot(bgb, dhgb)
            dx_state = dxw * wx_all[:, gs]
            dwgt = dwgt + _dot01_r(dxw * xg, ext3_g)
            dbt = _dot(dhgb, xw.astype(BF16), NT)
            hh = _colsum8(dhg * hg)
            hh16 = jnp.concatenate([hh, jnp.zeros_like(hh)], axis=0)
            delast = delast + jnp.sum(_dot01_r(hh16, ext3_g), axis=0, keepdims=True)
            dgm = jnp.zeros((CHUNK, CHUNK), F32)
            for r in range(hpg):
                hd = g * hpg + r
                cs = slice(r * SSM_HEAD_DIM, (r + 1) * SSM_HEAD_DIM)
                acol_r, arow_r = acol[:, hd:hd + 1], arow[hd:hd + 1, :]
                dtrow_r, dtcol_r = dtrow[hd:hd + 1, :], dt[:, hd:hd + 1]
                lm = jnp.where(causal, jnp.exp(jnp.where(causal, acol_r - arow_r, 0.0)), 0.0)
                lmt = jnp.where(upper, jnp.exp(jnp.where(upper, arow_r - acol_r, 0.0)), 0.0)
                wt = gmt * lmt * dtcol_r
                dy_r = dyg[:, cs].astype(BF16)
                dx_r = _dot(wt.astype(BF16), dy_r)
                dw = _dot(dy_r, xg[:, cs].astype(BF16), NT)
                t1 = dw * lm
                dgm = dgm + t1 * dtrow_r
                q1 = t1 * gm
                m = q1 * dtrow_r
                dacol = dacol + jnp.sum(m, axis=1, keepdims=True) * (lane_id == hd).astype(F32)
                darowf = darowf - (sub_id == hd).astype(F32) * jnp.sum(m, axis=0, keepdims=True)
                ddtrowf = ddtrowf + (sub_id == hd).astype(F32) * jnp.sum(q1, axis=0, keepdims=True)
                dact_ref[:, pl.ds(hd * SSM_HEAD_DIM, SSM_HEAD_DIM)] = (
                    dx_r + dx_state[:, cs] + dyg[:, cs] * dsk_ref[:, pl.ds(hd * SSM_HEAD_DIM, SSM_HEAD_DIM)])
            dgmb = dgm.astype(BF16)
            dact_ref[:, pl.ds(SSM_WIDTH + g * SSM_STATE, SSM_STATE)] = dbt.T + _dot(dgm.T.astype(BF16), cgb)
            dact_ref[:, pl.ds(SSM_WIDTH + 2 * SSM_STATE + g * SSM_STATE, SSM_STATE)] = dcg + _dot(dgmb, bgb)
            dh_s[:, gs] = dh_in
        t = dwgt * wgt
        dalast = jnp.sum(t, axis=0, keepdims=True) + delast * elast
        dacol_tot = dacol - t + darowf.T + (sub_id == CHUNK - 1).astype(F32) * dalast
        dda = _dot01_l(trit3_ref[...], dacol_tot)
        ddt = dwgt * wgt0 + ddtrowf.T + dda * a
        ddtr = jnp.where(valid, ddt * _sigmoid(dtr_), 0.0)
        ddtr_ref[...] = ddtr
        _acc(da_ref, _colsum8(dda * dt) * a)
        _acc(dbias_ref, _colsum8(ddtr))

    rev = lambda c: nc - 1 - c
    xs_spec = pl.BlockSpec((CHUNK, SSM_WIDTH), lambda c: (rev(c), 0))
    return pl.pallas_call(
        body, name="ssd_bwd", grid=(nc,),
        in_specs=[xs_spec, xs_spec,
                  pl.BlockSpec((CHUNK, 2 * SSM_STATE), lambda c: (rev(c), nb)),
                  pl.BlockSpec((CHUNK, 2 * SSM_STATE), lambda c: (rev(c), nb + 1)),
                  pl.BlockSpec((CHUNK, LANE), lambda c: (rev(c), 0)),
                  pl.BlockSpec((1, SSM_STATE, SSM_WIDTH), lambda c: (rev(c), 0, 0)),
                  _fullspec((1, LANE)), _fullspec((1, LANE)), _fullspec((1, SSM_WIDTH)),
                  _fullspec((CHUNK, CHUNK)), _fullspec((CHUNK, 3 * CHUNK)), _fullspec((CHUNK, 3 * CHUNK)),
                  _fullspec((3 * LANE, SSM_WIDTH)), _fullspec((SSM_GROUPS, 3 * gw, LANE))],
        out_specs=[pl.BlockSpec((CHUNK, CONV_DIM), lambda c: (rev(c), 0)), pl.BlockSpec((CHUNK, LANE), lambda c: (rev(c), 0)),
                   _fullspec((8, LANE)), _fullspec((8, LANE))],
        out_shape=[_sds((lp, CONV_DIM), F32), _sds((lp, LANE), F32), _sds((8, LANE), F32), _sds((8, LANE), F32)],
        scratch_shapes=[pltpu.VMEM((SSM_STATE, SSM_WIDTH), F32)],
        compiler_params=_cp(("arbitrary",)),
    )(dy, xact, xact, xact, dtr, hprev, dt_bias, a_log, dskip, _tri_mat(), _x3(_tri_mat(), 1), _x3(_tri_mat().T, 1),
      _x3(_expand_mat(), 0), jnp.stack([_x3(_expand_mat().T[g * gw:(g + 1) * gw], 0) for g in range(SSM_GROUPS)]))


def _conv_bwd(dact, xbc, cw, cb):
    lp, c = xbc.shape
    t8 = ROWB // 8
    nb = lp // ROWB

    def body(d_ref, dnext_ref, x_ref, prev_ref, next_ref, w_ref, b_ref, dx_ref, dw_ref, db_ref, xb, dp):
        i = pl.program_id(0)
        last = i == nb - 1
        xb[pl.ds(0, 8), :] = jnp.where(i > 0, prev_ref[...], 0.0)
        xb[pl.ds(8, ROWB), :] = x_ref[...]
        xb[pl.ds(8 + ROWB, 8), :] = jnp.where(last, 0.0, next_ref[...])

        @pl.when(i == 0)
        def _():
            dw_ref[...] = jnp.zeros_like(dw_ref)
            db_ref[...] = jnp.zeros_like(db_ref)

        sub = lax.broadcasted_iota(jnp.int32, (8, 1), 0)
        x0 = 8 - (CONV_K - 1)

        def strip(s, carry):
            cs = pl.ds(pl.multiple_of(s * LANE, LANE), LANE)
            w, b = w_ref[:, cs], b_ref[:, cs]

            def dpre_rows(r0, n, d):
                xs = [xb[pl.ds(x0 + kk + r0, n), cs] for kk in range(CONV_K)]
                pre = b + sum(w[kk:kk + 1, :] * xs[kk] for kk in range(CONV_K))
                sg = _sigmoid(pre)
                return d * (sg * (1.0 + pre * (1.0 - sg))), xs

            dws = [jnp.zeros((8, LANE), F32) for _ in range(CONV_K)]
            dbs = jnp.zeros((8, LANE), F32)
            for r0 in range(0, ROWB, CONV_ROWS):
                dpre, xs = dpre_rows(r0, CONV_ROWS, d_ref[pl.ds(r0, CONV_ROWS), cs])
                dp[pl.ds(r0, CONV_ROWS), cs] = dpre
                dbs = dbs + _colsum8(dpre)
                for kk in range(CONV_K):
                    dws[kk] = dws[kk] + _colsum8(dpre * xs[kk])
            dp[pl.ds(ROWB, 8), cs] = dpre_rows(ROWB, 8, jnp.where(last, 0.0, dnext_ref[:, cs]))[0]
            dwv = sum(jnp.where(sub == kk, jnp.sum(dws[kk], axis=0, keepdims=True), 0.0) for kk in range(CONV_K))
            dw_ref[:, cs] += dwv
            db_ref[:, cs] += dbs
            for r0 in range(0, ROWB, CONV_ROWS):
                dx = sum(w[kk:kk + 1, :] * dp[pl.ds(CONV_K - 1 - kk + r0, CONV_ROWS), cs] for kk in range(CONV_K))
                dx_ref[pl.ds(r0, CONV_ROWS), cs] = dx.astype(BF16)
            return carry

        lax.fori_loop(0, c // LANE, strip, 0)

    nxt = lambda i: (jnp.minimum((i + 1) * t8, lp // 8 - 1), 0)
    prv = lambda i: (jnp.maximum(i * t8 - 1, 0), 0)
    return pl.pallas_call(
        body, name="conv_bwd", grid=(nb,),
        in_specs=[_rowspec(ROWB, c), pl.BlockSpec((8, c), nxt), _rowspec(ROWB, c), pl.BlockSpec((8, c), prv),
                  pl.BlockSpec((8, c), nxt), _fullspec((8, c)), _fullspec((1, c))],
        out_specs=[_rowspec(ROWB, c), _fullspec((8, c)), _fullspec((8, c))],
        out_shape=[_sds((lp, c), BF16), _sds((8, c), F32), _sds((8, c), F32)],
        scratch_shapes=[pltpu.VMEM((ROWB + 16, c), F32), pltpu.VMEM((ROWB + 8, c), F32)],
        compiler_params=_cp(("arbitrary",)),
    )(dact, dact, xbc, xbc, xbc, cw, cb)


def _attn_delta(datt, att):
    lp = att.shape[0]

    def body(do_ref, o_ref, d_ref):
        d = jnp.sum(do_ref[...].astype(F32) * o_ref[...].astype(F32), axis=1, keepdims=True)
        d_ref[0] = jnp.broadcast_to(d, (ROWB, LANE)).T[0:1, :]

    blk = pl.BlockSpec((ROWB, V_HEAD), lambda h, i: (i, h))
    return pl.pallas_call(
        body, name="attn_delta", grid=(ATT_HEADS, lp // ROWB), in_specs=[blk, blk],
        out_specs=pl.BlockSpec((1, 1, ROWB), lambda h, i: (h, 0, i)),
        out_shape=_sds((ATT_HEADS, 1, lp), F32), compiler_params=_cp(("arbitrary", "arbitrary")),
    )(datt, att)


def _flash_bwd(q, k, v, datt, lse_row, delta_row):
    lp = q.shape[0]
    nk = lp // ROWB

    def body(k_ref, v_ref, q_ref, do_ref, lse_ref, dl_ref, dq_ref, dk_ref, dv_ref, dq_acc, dk_acc, dv_acc):
        j = pl.program_id(1)

        @pl.when(j == 0)
        def _():
            dq_acc[...] = jnp.zeros_like(dq_acc)

        kb, vb = k_ref[...], v_ref[...]
        dk_acc[...] = jnp.zeros_like(dk_acc)
        dv_acc[...] = jnp.zeros_like(dv_acc)

        def tile(i, masked):
            off = pl.multiple_of(i * ROWB, ROWB)
            qb = q_ref[pl.ds(off, ROWB), :]
            dob = do_ref[pl.ds(off, ROWB), :]
            lse_r = lse_ref[0, :, pl.ds(off, ROWB)]
            dl_r = dl_ref[0, :, pl.ds(off, ROWB)]
            st = _dot(kb, qb, NT)
            if masked:
                krow = j * ROWB + lax.broadcasted_iota(jnp.int32, st.shape, 0)
                qrow = i * ROWB + lax.broadcasted_iota(jnp.int32, st.shape, 1)
                st = jnp.where(_att_ok(qrow, krow), st, NEG)
            pt = jnp.exp2(st - lse_r)
            dv_acc[...] += _dot(pt.astype(BF16), dob)
            dpt = _dot(vb, dob, NT)
            dst = (pt * (dpt - dl_r)).astype(BF16)
            dk_acc[...] += _dot(dst, qb)
            dq_acc[pl.ds(off, ROWB), :] += _dot(dst, kb, ((0,), (0,)))

        tile(j, True)

        @pl.when(j == 0)
        def _():
            _pair_loop(1, nk, lambda i: tile(i, True), (2,))

        @pl.when(j > 0)
        def _():
            _pair_loop(j + 1, nk, lambda i: tile(i, False), (4,))

        dk_ref[...] = dk_acc[...] * LN2
        dv_ref[...] = dv_acc[...].astype(BF16)
        dq_ref[...] = dq_acc[pl.ds(pl.multiple_of(j * ROWB, ROWB), ROWB), :] * ATT_SCALE

    stat = pl.BlockSpec((1, 1, lp), lambda h, j: (h, 0, 0))
    blk = pl.BlockSpec((ROWB, HEADW), lambda h, j: (j, h))
    return pl.pallas_call(
        body, name="flash_bwd", grid=(ATT_HEADS, nk),
        in_specs=[blk, pl.BlockSpec((ROWB, V_HEAD), lambda h, j: (j, 2 * h)),
                  pl.BlockSpec((lp, HEADW), lambda h, j: (0, h)), pl.BlockSpec((lp, V_HEAD), lambda h, j: (0, h)),
                  stat, stat],
        out_specs=[blk, blk, pl.BlockSpec((ROWB, V_HEAD), lambda h, j: (j, h))],
        out_shape=[_sds((lp, ATT_HEADS * HEADW), F32), _sds((lp, ATT_HEADS * HEADW), F32),
                   _sds((lp, ATT_HEADS * V_HEAD), BF16)],
        scratch_shapes=[pltpu.VMEM((lp, HEADW), F32), pltpu.VMEM((ROWB, HEADW), F32), pltpu.VMEM((ROWB, V_HEAD), F32)],
        compiler_params=_cp(("arbitrary", "arbitrary")),
    )(k, v, q, datt, lse_row, delta_row)


def _qkv_bwd(dq, dk, dv, cq, ckv, cos, sa, sb, gq, gkv, wq, wk, wv):
    lp = cq.shape[0]
    qw = ATT_HEADS * HEADW

    def body(dq_ref, dk_ref, dv_ref, cq_ref, ckv_ref, cos_ref, sa_ref, sb_ref, gq_ref, gkv_ref, wq_ref, wk_ref, wv_ref,
             dcq_ref, dckv_ref, dkr_ref, dqp_ref, dkb_ref, dgq_ref, dgkv_ref):
        cos_, sa_, sb_ = cos_ref[...], sa_ref[...], sb_ref[...]
        dqp = _rope_t(dq_ref[...], _tile8(cos_), _tile8(sa_), _tile8(sb_)).astype(BF16)
        dqp_ref[...] = dqp
        dcq, dgq = _rms_bwd(cq_ref[...], gq_ref[...], _dot(dqp, wq_ref[...], NT))
        dcq_ref[...] = dcq.astype(BF16)
        dk_ = dk_ref[...]
        dkb = dk_.astype(BF16)
        dkb_ref[...] = dkb
        dksum = sum(dk_[:, hh * HEADW:(hh + 1) * HEADW] for hh in range(ATT_HEADS))
        dkr_ref[...] = _rope_t(dksum, cos_, sa_, sb_).astype(BF16)
        dckvn = _dot(dkb, wk_ref[...], NT) + _dot(dv_ref[...], wv_ref[...], NT)
        dckv, dgkv = _rms_bwd(ckv_ref[...], gkv_ref[...], dckvn)
        dckv_ref[...] = dckv.astype(BF16)
        _acc(dgq_ref, _colsum8(dgq))
        _acc(dgkv_ref, _colsum8(dgkv))

    return pl.pallas_call(
        body, name="qkv_bwd", grid=(lp // ROWB,),
        in_specs=[_rowspec(ROWB, qw), _rowspec(ROWB, qw), _rowspec(ROWB, ATT_HEADS * V_HEAD),
                  _rowspec(ROWB, Q_LORA), _rowspec(ROWB, KV_LORA)] + [_rowspec(ROWB, HEADW)] * 3
        + [_fullspec((1, Q_LORA)), _fullspec((1, KV_LORA)), _fullspec((Q_LORA, qw)), _fullspec((KV_LORA, qw)),
           _fullspec((KV_LORA, ATT_HEADS * V_HEAD))],
        out_specs=[_rowspec(ROWB, Q_LORA), _rowspec(ROWB, KV_LORA), _rowspec(ROWB, HEADW), _rowspec(ROWB, qw),
                   _rowspec(ROWB, qw), _fullspec((8, Q_LORA)), _fullspec((8, KV_LORA))],
        out_shape=[_sds((lp, Q_LORA), BF16), _sds((lp, KV_LORA), BF16), _sds((lp, HEADW), BF16), _sds((lp, qw), BF16),
                   _sds((lp, qw), BF16), _sds((8, Q_LORA), F32), _sds((8, KV_LORA), F32)],
        compiler_params=_cp(("arbitrary",)),
    )(dq, dk, dv, cq, ckv, cos, sa, sb, gq, gkv, wq, wk, wv)


def _in_bwd(dproj, head, x, dh1, g, w_all):
    lp = dh1.shape[0]

    def body(dp_ref, head_ref, x_ref, dh1_ref, g_ref, w_ref, dx_ref, dhead_ref, dg_ref):
        dx, dg = _rms_bwd(_h_block(head_ref, x_ref), g_ref[...], _dot(dp_ref[...], w_ref[...], NT))
        dh = dh1_ref[...] + dx

        @pl.when(pl.program_id(0) == 0)
        def _():
            dhead_ref[...] = dh

        @pl.when(pl.program_id(0) > 0)
        def _():
            dx_ref[...] = dh

        _acc(dg_ref, _colsum8(dg))

    return pl.pallas_call(
        body, name="in_bwd", grid=(lp // ROWB,),
        in_specs=[_rowspec(ROWB, PROJ_W), _fullspec((ROWB, D_MODEL)), _xspec(), _rowspec(ROWB, D_MODEL),
                  _fullspec((1, D_MODEL)), _fullspec((D_MODEL, PROJ_W))],
        out_specs=[_xspec(), _fullspec((ROWB, D_MODEL)), _fullspec((8, D_MODEL))],
        out_shape=[_sds(x.shape, F32), _sds((ROWB, D_MODEL), F32), _sds((8, D_MODEL), F32)],
        compiler_params=_cp(("arbitrary",)),
    )(dproj, head, x, dh1, g, w_all)


def _tile_of(n, cap=1024):
    return max(t for t in range(LANE, min(n, cap) + 1, LANE) if n % t == 0)


def _matmul_tn(name, a, b):
    rows, kd = a.shape
    nd = b.shape[1]
    tk, tn = _tile_of(kd), _tile_of(nd)

    def body(a_ref, b_ref, o_ref):
        @pl.when(pl.program_id(2) == 0)
        def _():
            o_ref[...] = jnp.zeros_like(o_ref)

        o_ref[...] += _dot(a_ref[...], b_ref[...], ((0,), (0,)))

    return pl.pallas_call(
        body, name=name, grid=(kd // tk, nd // tn, rows // ROWB),
        in_specs=[pl.BlockSpec((ROWB, tk), lambda i, j, r: (r, i)), pl.BlockSpec((ROWB, tn), lambda i, j, r: (r, j))],
        out_specs=pl.BlockSpec((tk, tn), lambda i, j, r: (i, j)), out_shape=_sds((kd, nd), F32),
        compiler_params=_cp(("arbitrary", "arbitrary", "arbitrary")),
    )(a, b)


def _local_backward(head, x, f, p):
    g = {}
    row = lambda v: _row1(v)
    s8 = lambda v: jnp.sum(v, axis=0)
    dh1, du, a_, df, dgpre, dgpost = _mlp_bwd(f["dh2"], f["f"], f["h1"], f["hn2"], p["w_mlp_up"], p["w_mlp_down"],
                                              row(p["norm_mlp_pre"]), row(p["norm_mlp_post"]))
    g["norm_mlp_pre"], g["norm_mlp_post"] = s8(dgpre), s8(dgpost)
    g["w_mlp_up"] = _matmul_tn("dw_mlp_up", f["hn2"], du)
    g["w_mlp_down"] = _matmul_tn("dw_mlp_down", a_, df)
    dmix, datt, dssm, dgmp = _out_bwd(dh1, f["mix"], p["w_out"], row(p["norm_mix_post"]))
    g["norm_mix_post"] = s8(dgmp)
    g["w_out"] = jnp.concatenate([_matmul_tn("dw_out_att", f["att"], dmix), _matmul_tn("dw_out_ssm", f["ssm"], dmix)], axis=0)
    dy, dz, dgn, dd = _ssd_post_bwd(dssm, f["y"], f["xact"], f["z"], f["dskip"], row(p["ssm_norm"]))
    g["ssm_norm"] = s8(dgn)
    g["d_skip"] = s8(dd).reshape(SSM_HEADS, SSM_HEAD_DIM).sum(axis=1)
    dact, ddtr, da8, dbias8 = _ssd_bwd(dy, f["xact"], f["dtr"], f["hprev"], f["dt_bias"], f["a_log"], f["dskip"], f["seq_rows"])
    g["a_log"], g["dt_bias"] = s8(da8)[:SSM_HEADS], s8(dbias8)[:SSM_HEADS]
    dxbc, dcw8, dcb8 = _conv_bwd(dact, f["xbc"], f["cw"], row(p["conv_b"]))
    g["conv_w"], g["conv_b"] = dcw8[:CONV_K], s8(dcb8)
    delta = _attn_delta(datt, f["att"])
    dq, dk, dv = _flash_bwd(f["q"], f["k"], f["v"], datt, f["lse"], delta)
    dcq, dckv, dkr, dqp, dkb, dgq, dgkv = _qkv_bwd(dq, dk, dv, f["cq"], f["ckv"], *f["rope"], row(p["q_a_norm"]),
                                                   row(p["kv_a_norm"]), f["wq"], f["wk"], f["wv"])
    g["q_a_norm"], g["kv_a_norm"] = s8(dgq), s8(dgkv)
    dwq = _matmul_tn("dw_q_up", f["cqn"], dqp).reshape(Q_LORA, ATT_HEADS, HEADW)
    g["w_q_up"] = dwq[:, :, :QK_NOPE + QK_ROPE].reshape(Q_LORA, -1)
    dwk = _matmul_tn("dw_k_up", f["ckvn"], dkb).reshape(KV_LORA, ATT_HEADS, HEADW)[:, :, :QK_NOPE]
    dwv = _matmul_tn("dw_v_up", f["ckvn"], dv).reshape(KV_LORA, ATT_HEADS, V_HEAD)
    g["w_kv_up"] = jnp.concatenate([dwk, dwv], axis=2).reshape(KV_LORA, -1)
    dproj = jnp.concatenate([dcq, dckv, dkr, dz, dxbc, ddtr.astype(BF16)], axis=1)
    dx, dhead, dgin = _in_bwd(dproj, head, x, dh1, row(p["norm_mix_pre"]), f["w_all"])
    g["norm_mix_pre"] = s8(dgin)
    g["meta_tokens"] = dhead[PADF:]
    dwa = _matmul_tn("dw_in", f["hn"], dproj)
    g["w_in"] = jnp.concatenate([dwa[:, PC_Q:PC_KR], dwa[:, PC_KR + QK_NOPE:PC_KR + QK_NOPE + QK_ROPE],
                                 dwa[:, PC_Z:PC_DT + SSM_HEADS]], axis=1)
    return dx, g


BIG = {"w_in": ((D_MODEL, IN_WIDTH), 1), "w_q_up": ((Q_LORA, ATT_HEADS * (QK_NOPE + QK_ROPE)), 1),
       "w_kv_up": ((KV_LORA, ATT_HEADS * (QK_NOPE + V_HEAD)), 1), "w_out": ((2 * D_MODEL, D_MODEL), 0),
       "w_mlp_up": ((D_MODEL, D_FF), 1), "w_mlp_down": ((D_FF, D_MODEL), 0), "conv_w": ((CONV_K, CONV_DIM), 1),
       "meta_tokens": ((N_META, D_MODEL), 1)}
SMALL = {"norm_mix_pre": D_MODEL, "q_a_norm": Q_LORA, "kv_a_norm": KV_LORA, "conv_b": CONV_DIM, "dt_bias": SSM_HEADS,
         "a_log": SSM_HEADS, "d_skip": SSM_HEADS, "ssm_norm": SSM_WIDTH, "norm_mix_post": D_MODEL,
         "norm_mlp_pre": D_MODEL, "norm_mlp_post": D_MODEL}
WEIGHT_ORDER = ("meta_tokens", "norm_mix_pre", "w_in", "q_a_norm", "w_q_up", "kv_a_norm", "w_kv_up", "conv_w", "conv_b",
                "dt_bias", "a_log", "d_skip", "ssm_norm", "w_out", "norm_mix_post", "norm_mlp_pre", "w_mlp_up",
                "w_mlp_down", "norm_mlp_post")
ADAM_ROWS = 256


def _shard_shape(name):
    shape, ax = BIG[name]
    return tuple(d // N_DEV if a == ax else d for a, d in enumerate(shape))


SMALL_ROWS = -(-sum(SMALL.values()) // (LANE * 8)) * 8


def _pack(flats, rows):
    v = jnp.concatenate([f.reshape(-1) for f in flats])
    return jnp.pad(v, (0, rows * LANE - v.shape[0])).reshape(rows, LANE)


def _unpack(packed, shapes):
    v = packed.reshape(-1)
    out, o = [], 0
    for s in shapes:
        n = math.prod(s)
        out.append(v[o:o + n].reshape(s))
        o += n
    return out


def _to_chunks(name, full):
    shape, ax = BIG[name]
    if ax == 0:
        return full.reshape((N_DEV,) + _shard_shape(name))
    k, n = shape
    return full.reshape(k, N_DEV, n // N_DEV).transpose(1, 0, 2)


def _from_shards(name, shards):
    shape, ax = BIG[name]
    if ax == 0:
        return shards.reshape(shape)
    return shards.transpose(1, 0, 2).reshape(shape)


def _peer(k):
    x, y, c = lax.axis_index("x"), lax.axis_index("y"), lax.axis_index("c")
    px = 1 - x if k & 4 else x
    py = 1 - y if k & 2 else y
    pc = 1 - c if k & 1 else c
    return (px, py, pc), 4 * px + 2 * py + pc


def _all_gather(shards):
    na = len(shards)

    def body(*refs):
        x_refs, out_refs = refs[:na], refs[na:2 * na]
        send_sems, recv_sems, local_sems = refs[2 * na:]
        _, me = _peer(0)
        sibling, _ = _peer(1)
        chips = (4, 2, 6)

        def copy(a, n, block, to, src=None):
            return pltpu.make_async_remote_copy(
                src_ref=out_refs[a].at[block] if src is None else src, dst_ref=out_refs[a].at[block],
                send_sem=send_sems.at[7 * a + n], recv_sem=recv_sems.at[7 * a + n], device_id=to, device_id_type=MESH)

        mine = [pltpu.make_async_copy(x_refs[a], out_refs[a].at[me], local_sems.at[a]) for a in range(na)]
        first = [copy(a, 0, me, sibling, src=x_refs[a]) for a in range(na)]
        first += [copy(a, 1 + n, me, _peer(k)[0], src=x_refs[a]) for n, k in enumerate(chips) for a in range(na)]
        for cp in mine + first:
            cp.start()
        passed = []
        for n, k in enumerate(chips):
            for a in range(na):
                copy(a, 1 + n, _peer(k)[1], sibling).wait_recv()
                passed.append(copy(a, 4 + n, _peer(k)[1], sibling))
                passed[-1].start()
        for a in range(na):
            copy(a, 0, _peer(1)[1], sibling).wait_recv()
        for n, k in enumerate(chips):
            for a in range(na):
                copy(a, 4 + n, _peer(k | 1)[1], sibling).wait_recv()
        for cp in first + passed:
            cp.wait_send()
        for cp in mine:
            cp.wait()

    return pl.pallas_call(
        body, name="all_gather_weights", out_shape=[_sds((N_DEV,) + s.shape, s.dtype) for s in shards],
        in_specs=[ANY] * na, out_specs=[ANY] * na,
        scratch_shapes=[pltpu.SemaphoreType.DMA((7 * na,)), pltpu.SemaphoreType.DMA((7 * na,)),
                        pltpu.SemaphoreType.DMA((na,))],
    )(*shards)


def _exchange(chunks, small):
    na = len(chunks) + 1

    def body(*refs):
        in_refs, out_refs = refs[:na], refs[na:2 * na]
        send_sems, recv_sems, local_sems = refs[2 * na:]
        _, me = _peer(0)

        def src(a, idx):
            return in_refs[a] if a == na - 1 else in_refs[a].at[idx]

        own = [pltpu.make_async_copy(src(a, me), out_refs[a].at[me], local_sems.at[a]) for a in range(na)]

        def copy(a, k, sending):
            to, idx = _peer(k)
            return pltpu.make_async_remote_copy(
                src_ref=src(a, idx if sending else me), dst_ref=out_refs[a].at[me if sending else idx],
                send_sem=send_sems.at[7 * a + k - 1], recv_sem=recv_sems.at[7 * a + k - 1],
                device_id=to, device_id_type=MESH)

        sent = [copy(a, k, True) for k in range(1, N_DEV) for a in range(na)]
        for cp in own + sent:
            cp.start()
        for k in range(1, N_DEV):
            for a in range(na):
                copy(a, k, False).wait_recv()
        for cp in sent:
            cp.wait_send()
        for cp in own:
            cp.wait()

    arrays = list(chunks) + [small]
    return pl.pallas_call(
        body, name="exchange_grads",
        out_shape=[_sds(c.shape, c.dtype) for c in chunks] + [_sds((N_DEV,) + small.shape, small.dtype)],
        in_specs=[ANY] * na, out_specs=[ANY] * na,
        scratch_shapes=[pltpu.SemaphoreType.DMA((7 * na,)), pltpu.SemaphoreType.DMA((7 * na,)),
                        pltpu.SemaphoreType.DMA((na,))],
    )(*arrays)


def _reduce_adamw(name, recv, w, m, v):
    rows, cols = w.shape
    blk = ADAM_ROWS if rows % ADAM_ROWS == 0 else rows
    c1 = 1.0 - ADAM_B1 ** ADAM_STEP
    c2 = 1.0 - ADAM_B2 ** ADAM_STEP

    def body(r_ref, w_ref, m_ref, v_ref, g_ref, d_ref, nm_ref, nv_ref):
        g = r_ref[0].astype(F32)
        for s in range(1, N_DEV):
            g = g + r_ref[s].astype(F32)
        g_ref[...] = g
        m_ = ADAM_B1 * m_ref[...] + (1.0 - ADAM_B1) * g
        v_ = ADAM_B2 * v_ref[...] + (1.0 - ADAM_B2) * (g * g)
        nm_ref[...] = m_
        nv_ref[...] = v_
        d_ref[...] = -ADAM_LR * ((m_ / c1) / (jnp.sqrt(v_ / c2) + ADAM_EPS) + ADAM_WD * w_ref[...])

    spec = _rowspec(blk, cols)
    return pl.pallas_call(
        body, name="reduce_adamw_" + name, grid=(rows // blk,),
        in_specs=[pl.BlockSpec((N_DEV, blk, cols), lambda i: (0, i, 0)), spec, spec, spec],
        out_specs=[spec] * 4, out_shape=[_sds((rows, cols), F32)] * 4,
        compiler_params=_cp(("arbitrary",)),
    )(recv, w, m, v)


def kernel(x, meta_tokens, norm_mix_pre, w_in, q_a_norm, w_q_up, kv_a_norm, w_kv_up, conv_w, conv_b, dt_bias, a_log, d_skip, ssm_norm, w_out, norm_mix_post, norm_mlp_pre, w_mlp_up, w_mlp_down, norm_mlp_post, loss_target, m_meta_tokens, m_norm_mix_pre, m_w_in, m_q_a_norm, m_w_q_up, m_kv_a_norm, m_w_kv_up, m_conv_w, m_conv_b, m_dt_bias, m_a_log, m_d_skip, m_ssm_norm, m_w_out, m_norm_mix_post, m_norm_mlp_pre, m_w_mlp_up, m_w_mlp_down, m_norm_mlp_post, v_meta_tokens, v_norm_mix_pre, v_w_in, v_q_a_norm, v_w_q_up, v_kv_a_norm, v_w_kv_up, v_conv_w, v_conv_b, v_dt_bias, v_a_log, v_d_skip, v_ssm_norm, v_w_out, v_norm_mix_post, v_norm_mlp_pre, v_w_mlp_up, v_w_mlp_down, v_norm_mlp_post):
    w = dict(meta_tokens=meta_tokens, norm_mix_pre=norm_mix_pre, w_in=w_in, q_a_norm=q_a_norm, w_q_up=w_q_up,
             kv_a_norm=kv_a_norm, w_kv_up=w_kv_up, conv_w=conv_w, conv_b=conv_b, dt_bias=dt_bias, a_log=a_log,
             d_skip=d_skip, ssm_norm=ssm_norm, w_out=w_out, norm_mix_post=norm_mix_post, norm_mlp_pre=norm_mlp_pre,
             w_mlp_up=w_mlp_up, w_mlp_down=w_mlp_down, norm_mlp_post=norm_mlp_post)
    m = dict(meta_tokens=m_meta_tokens, norm_mix_pre=m_norm_mix_pre, w_in=m_w_in, q_a_norm=m_q_a_norm, w_q_up=m_w_q_up,
             kv_a_norm=m_kv_a_norm, w_kv_up=m_w_kv_up, conv_w=m_conv_w, conv_b=m_conv_b, dt_bias=m_dt_bias,
             a_log=m_a_log, d_skip=m_d_skip, ssm_norm=m_ssm_norm, w_out=m_w_out, norm_mix_post=m_norm_mix_post,
             norm_mlp_pre=m_norm_mlp_pre, w_mlp_up=m_w_mlp_up, w_mlp_down=m_w_mlp_down, norm_mlp_post=m_norm_mlp_post)
    v = dict(meta_tokens=v_meta_tokens, norm_mix_pre=v_norm_mix_pre, w_in=v_w_in, q_a_norm=v_q_a_norm, w_q_up=v_w_q_up,
             kv_a_norm=v_kv_a_norm, w_kv_up=v_w_kv_up, conv_w=v_conv_w, conv_b=v_conv_b, dt_bias=v_dt_bias,
             a_log=v_a_log, d_skip=v_d_skip, ssm_norm=v_ssm_norm, w_out=v_w_out, norm_mix_post=v_norm_mix_post,
             norm_mlp_pre=v_norm_mlp_pre, w_mlp_up=v_w_mlp_up, w_mlp_down=v_w_mlp_down, norm_mlp_post=v_norm_mlp_post)
    big_names = [n for n in WEIGHT_ORDER if n in BIG]
    small_names = [n for n in WEIGHT_ORDER if n in SMALL]
    shard = lambda d, n: d[n].reshape(_shard_shape(n))

    f32_names = ("conv_w", "meta_tokens")
    gathered = _all_gather([shard(w, n).astype(F32 if n in f32_names else BF16) for n in big_names])
    p = {n: w[n].reshape(-1) for n in small_names}
    p.update({n: _from_shards(n, s) for n, s in zip(big_names, gathered)})
    head = jnp.concatenate([jnp.zeros((PADF, D_MODEL), F32), p["meta_tokens"]], axis=0)
    f = _local_forward(head, x[0], loss_target[0], p)
    dx, g = _local_backward(head, x[0], f, p)
    grad_x = dx[None]
    loss = lax.psum(f["loss"], ("x", "y", "c"))

    small = _pack([g[n] for n in small_names], SMALL_ROWS)
    *recv_big, recv_small = _exchange([_to_chunks(n, g[n]).astype(BF16) for n in big_names], small)

    outs = {}
    kinds = ("grad", "delta", "new_m", "new_v")
    for n, recv in zip(big_names, recv_big):
        for kind, arr in zip(kinds, _reduce_adamw(n, recv, shard(w, n), shard(m, n), shard(v, n))):
            outs[kind, n] = arr.reshape(w[n].shape)
    packed = [_pack([d[n] for n in small_names], SMALL_ROWS) for d in (w, m, v)]
    for kind, arr in zip(kinds, _reduce_adamw("small", recv_small, *packed)):
        for n, val in zip(small_names, _unpack(arr, [(SMALL[n],) for n in small_names])):
            outs[kind, n] = val.reshape(w[n].shape)
    return (loss, grad_x) + tuple(outs[kind, n] for kind in ("grad", "delta", "new_m", "new_v") for n in WEIGHT_ORDER)
```

```python
import math

import jax
import jax.numpy as jnp
import numpy as np
from jax import lax
from jax.experimental import pallas as pl
from jax.experimental.pallas import tpu as pltpu

F32 = jnp.float32
BF16 = jnp.bfloat16

D_MODEL = 1024
N_META = 16
EPS = 1e-6
ATT_HEADS = 8
Q_LORA = 384
KV_LORA = 256
QK_NOPE = 128
QK_ROPE = 64
V_HEAD = 128
ROPE_THETA = 10000.0
SSM_HEADS = 16
SSM_HEAD_DIM = 64
SSM_WIDTH = 1024
SSM_GROUPS = 2
SSM_STATE = 128
CONV_K = 4
CHUNK = 128
CONV_DIM = 1536
D_FF = 4096
IN_SPLITS = (Q_LORA, KV_LORA, QK_ROPE, SSM_WIDTH, CONV_DIM, SSM_HEADS)
IN_WIDTH = sum(IN_SPLITS)
ADAM_LR, ADAM_B1, ADAM_B2, ADAM_EPS, ADAM_WD, ADAM_STEP = 0.001, 0.9, 0.999, 1e-08, 0.01, 10

LANE = 128
ROWB = 512
PADF = ROWB - N_META
HEADW = 256
PC_Q, PC_KV, PC_KR, PC_Z, PC_XBC, PC_DT, PROJ_W = 0, 384, 640, 896, 1920, 3456, 3584
NEG = -1e30
N_DEV = 8
VMEM_LIMIT = 56 * 1024 * 1024
MESH = pl.DeviceIdType.MESH


def _cp(sem, vmem=VMEM_LIMIT, **kw):
    return pltpu.CompilerParams(dimension_semantics=sem, vmem_limit_bytes=vmem, **kw)


def _dot(a, b, dims=((1,), (0,))):
    return lax.dot_general(a, b, (dims, ((), ())), preferred_element_type=F32)


def _bdot(a, b, dims=((1,), (0,))):
    return _dot(a.astype(BF16), b.astype(BF16), dims)


NT = ((1,), (1,))


def _rms_fwd(x, w):
    r = lax.rsqrt(jnp.mean(x * x, axis=-1, keepdims=True) + EPS)
    return (x * r) * w


def _rms_bwd(x, w, dy):
    r = lax.rsqrt(jnp.mean(x * x, axis=-1, keepdims=True) + EPS)
    xh = x * r
    g = dy * w
    dx = r * (g - xh * jnp.mean(g * xh, axis=-1, keepdims=True))
    return dx, dy * xh


def _sigmoid(x):
    return 0.5 * jnp.tanh(0.5 * x) + 0.5


def _colsum8(x):
    t, c = x.shape
    return jnp.sum(x.reshape(t // 8, 8, c), axis=0)


def _rowspec(t, c, cb=0):
    return pl.BlockSpec((t, c), lambda i: (i, cb))


def _fullspec(shape):
    n = len(shape)
    return pl.BlockSpec(shape, lambda i: (0,) * n)


def _sds(shape, dt):
    return jax.ShapeDtypeStruct(shape, dt)


def _acc(ref, val):
    @pl.when(pl.program_id(0) == 0)
    def _():
        ref[...] = jnp.zeros_like(ref)

    ref[...] += val


def _xspec():
    return pl.BlockSpec((ROWB, D_MODEL), lambda i: (jnp.maximum(i - 1, 0), 0))


def _h_block(head_ref, x_ref):
    return jnp.where(pl.program_id(0) == 0, head_ref[...], x_ref[...])


def _norm_in_proj(head, x, g, w_all):
    lp = head.shape[0] + x.shape[0]

    def body(head_ref, x_ref, g_ref, w_ref, hn_ref, cq_ref, ckv_ref, kr_ref, z_ref, xbc_ref, dt_ref):
        hn = _rms_fwd(_h_block(head_ref, x_ref), g_ref[...]).astype(BF16)
        hn_ref[...] = hn
        p = _dot(hn, w_ref[...])
        cq_ref[...] = p[:, PC_Q:PC_KV]
        ckv_ref[...] = p[:, PC_KV:PC_KR]
        kr_ref[...] = p[:, PC_KR:PC_Z]
        z_ref[...] = p[:, PC_Z:PC_XBC]
        xbc_ref[...] = p[:, PC_XBC:PC_DT]
        dt_ref[...] = p[:, PC_DT:PROJ_W]

    widths = (Q_LORA, KV_LORA, HEADW, SSM_WIDTH, CONV_DIM, LANE)
    return pl.pallas_call(
        body, name="norm_in_proj", grid=(lp // ROWB,),
        in_specs=[_fullspec((ROWB, D_MODEL)), _xspec(), _fullspec((1, D_MODEL)), _fullspec((D_MODEL, PROJ_W))],
        out_specs=[_rowspec(ROWB, D_MODEL)] + [_rowspec(ROWB, w) for w in widths],
        out_shape=[_sds((lp, D_MODEL), BF16)] + [_sds((lp, w), F32) for w in widths],
        compiler_params=_cp(("arbitrary",)),
    )(head, x, g, w_all)


def _rope(x, cos, sa, sb):
    w = x.shape[1]
    return x * cos + pltpu.roll(x, w - 32, 1) * sa + pltpu.roll(x, 32, 1) * sb


def _rope_t(dy, cos, sa, sb):
    w = dy.shape[1]
    return dy * cos + pltpu.roll(dy * sa, 32, 1) + pltpu.roll(dy * sb, w - 32, 1)


def _tile8(t):
    return jnp.concatenate([t] * ATT_HEADS, axis=1)


def _qkv(cq, ckv, kr, cos, sa, sb, gq, gkv, wq, wk, wv):
    lp = cq.shape[0]
    qw = ATT_HEADS * HEADW

    def body(cq_ref, ckv_ref, kr_ref, cos_ref, sa_ref, sb_ref, gq_ref, gkv_ref, wq_ref, wk_ref, wv_ref,
             q_ref, k_ref, v_ref, cqn_ref, ckvn_ref):
        cos_, sa_, sb_ = cos_ref[...], sa_ref[...], sb_ref[...]
        cqn = _rms_fwd(cq_ref[...], gq_ref[...]).astype(BF16)
        ckvn = _rms_fwd(ckv_ref[...], gkv_ref[...]).astype(BF16)
        cqn_ref[...] = cqn
        ckvn_ref[...] = ckvn
        q = _dot(cqn, wq_ref[...])
        q_ref[...] = (_rope(q, _tile8(cos_), _tile8(sa_), _tile8(sb_)) * Q_PRESCALE).astype(BF16)
        k = _dot(ckvn, wk_ref[...]) + _tile8(_rope(kr_ref[...], cos_, sa_, sb_))
        k_ref[...] = k.astype(BF16)
        lanes = lax.broadcasted_iota(jnp.int32, (1, qw), 1)
        ones = ((lanes % HEADW) >= V_HEAD).astype(F32)
        v_ref[...] = (_dot(ckvn, wv_ref[...]) + ones).astype(BF16)

    return pl.pallas_call(
        body, name="qkv", grid=(lp // ROWB,),
        in_specs=[_rowspec(ROWB, Q_LORA), _rowspec(ROWB, KV_LORA), _rowspec(ROWB, HEADW)]
        + [_rowspec(ROWB, HEADW)] * 3
        + [_fullspec((1, Q_LORA)), _fullspec((1, KV_LORA)), _fullspec((Q_LORA, qw)), _fullspec((KV_LORA, qw)),
           _fullspec((KV_LORA, qw))],
        out_specs=[_rowspec(ROWB, qw), _rowspec(ROWB, qw), _rowspec(ROWB, qw),
                   _rowspec(ROWB, Q_LORA), _rowspec(ROWB, KV_LORA)],
        out_shape=[_sds((lp, qw), BF16), _sds((lp, qw), BF16), _sds((lp, qw), BF16),
                   _sds((lp, Q_LORA), BF16), _sds((lp, KV_LORA), BF16)],
        compiler_params=_cp(("arbitrary",)),
    )(cq, ckv, kr, cos, sa, sb, gq, gkv, wq, wk, wv)


ATT_SCALE = (QK_NOPE + QK_ROPE) ** -0.5
LOG2E = 1.4426950408889634
LN2 = 0.6931471805599453
Q_PRESCALE = ATT_SCALE * LOG2E
KVB = 512


def _att_ok(qrow, krow):
    return (krow <= qrow) & ((krow >= PADF) | (qrow < PADF))


def _lanes(x, n):
    return x if n == 1 else jnp.concatenate([x] * n, axis=1)


def _pair_loop(lo, hi, tile, unrolls=(2,)):
    for u in tuple(unrolls) + (1,):
        n = jnp.maximum(hi - lo, 0)
        trips = n // u

        def many(t, c, u=u, lo=lo):
            for d in range(u):
                tile(lo + u * t + d)
            return c

        lax.fori_loop(0, trips, many, 0)
        lo = lo + trips * u


def _flash_fwd(q, k, v):
    lp = q.shape[0]
    nq = lp // ROWB
    rep = KVB // LANE

    def body(q_ref, k_ref, v_ref, o_ref, lse_ref, acc, m_s):
        i = pl.program_id(1)
        qb = q_ref[...]
        m_s[...] = jnp.full_like(m_s, NEG)
        acc[...] = jnp.zeros_like(acc)

        def tile(j, masked):
            off = pl.multiple_of(j * KVB, KVB)
            kb = k_ref[pl.ds(off, KVB), :]
            vb = v_ref[pl.ds(off, KVB), :]
            s = _dot(qb, kb, NT)
            if masked:
                qrow = i * ROWB + lax.broadcasted_iota(jnp.int32, s.shape, 0)
                krow = j * KVB + lax.broadcasted_iota(jnp.int32, s.shape, 1)
                s = jnp.where(_att_ok(qrow, krow), s, NEG)
            m_prev = m_s[...]
            m_new = jnp.maximum(m_prev, jnp.max(s, axis=1, keepdims=True))
            alpha = jnp.exp2(m_prev - m_new)
            p = jnp.exp2(s - _lanes(m_new, rep))
            acc[...] = _lanes(alpha, 2) * acc[...] + _dot(p.astype(BF16), vb)
            m_s[...] = m_new

        tile(0, True)
        _pair_loop(1, i, lambda j: tile(j, False), (8, 2))

        @pl.when(i > 0)
        def _():
            tile(i, True)

        l = acc[:, V_HEAD:]
        o_ref[...] = (acc[:, :V_HEAD] / l).astype(BF16)
        lse_ref[0] = (m_s[...] + jnp.log2(l)).T[0:1, :]

    return pl.pallas_call(
        body, name="flash_fwd", grid=(ATT_HEADS, nq),
        in_specs=[pl.BlockSpec((ROWB, HEADW), lambda h, i: (i, h)),
                  pl.BlockSpec((lp, HEADW), lambda h, i: (0, h)),
                  pl.BlockSpec((lp, HEADW), lambda h, i: (0, h))],
        out_specs=[pl.BlockSpec((ROWB, V_HEAD), lambda h, i: (i, h)),
                   pl.BlockSpec((1, 1, ROWB), lambda h, i: (h, 0, i))],
        out_shape=[_sds((lp, ATT_HEADS * V_HEAD), BF16), _sds((ATT_HEADS, 1, lp), F32)],
        scratch_shapes=[pltpu.VMEM((ROWB, HEADW), F32), pltpu.VMEM((ROWB, LANE), F32)],
        compiler_params=_cp(("arbitrary", "arbitrary")),
    )(q, k, v)


def _silu(x):
    return x * _sigmoid(x)


CONV_ROWS = 64


def _conv_fwd(xbc, cw, cb):
    lp, c = xbc.shape
    t8 = ROWB // 8

    def body(x_ref, prev_ref, w_ref, b_ref, o_ref, buf):
        i = pl.program_id(0)
        buf[pl.ds(0, 8), :] = jnp.where(i > 0, prev_ref[...], 0.0)
        buf[pl.ds(8, ROWB), :] = x_ref[...]

        def strip(s, carry):
            cs = pl.ds(pl.multiple_of(s * LANE, LANE), LANE)
            w, b = w_ref[:, cs], b_ref[:, cs]
            for r0 in range(0, ROWB, CONV_ROWS):
                pre = b + sum(w[kk:kk + 1, :] * buf[pl.ds(8 - (CONV_K - 1) + kk + r0, CONV_ROWS), cs]
                              for kk in range(CONV_K))
                o_ref[pl.ds(r0, CONV_ROWS), cs] = _silu(pre)
            return carry

        lax.fori_loop(0, c // LANE, strip, 0)

    return pl.pallas_call(
        body, name="conv_fwd", grid=(lp // ROWB,),
        in_specs=[_rowspec(ROWB, c), pl.BlockSpec((8, c), lambda i: (jnp.maximum(i * t8 - 1, 0), 0)),
                  _fullspec((8, c)), _fullspec((1, c))],
        out_specs=_rowspec(ROWB, c), out_shape=_sds((lp, c), F32),
        scratch_shapes=[pltpu.VMEM((ROWB + 8, c), F32)],
        compiler_params=_cp(("arbitrary",)),
    )(xbc, xbc, cw, cb)


def _expand_mat():
    r = np.arange(LANE)[:, None]
    c = np.arange(SSM_WIDTH)[None, :]
    return jnp.asarray((c // SSM_HEAD_DIM == r).astype(np.float32))


def _tri_mat():
    i = np.arange(CHUNK)
    return jnp.asarray((i[:, None] >= i[None, :]).astype(np.float32))


def _x3(m, axis):
    return jnp.concatenate([m.astype(BF16)] * 3, axis=axis)


def _split3(x):
    hi = x.astype(BF16)
    r = x - hi.astype(F32)
    mid = r.astype(BF16)
    return hi, mid, (r - mid.astype(F32)).astype(BF16)


def _dot01_r(x, m3):
    return _dot(jnp.concatenate(_split3(x), axis=1), m3)


def _dot01_l(m3, x):
    return _dot(m3, jnp.concatenate(_split3(x), axis=0))


def _ssd_prep(dtr_ref, bias_ref, alog_ref, tri3, c, seq_rows):
    rows = c * CHUNK + lax.broadcasted_iota(jnp.int32, (CHUNK, LANE), 0)
    lanes = lax.broadcasted_iota(jnp.int32, (CHUNK, LANE), 1)
    valid = (rows >= PADF) & (rows < PADF + seq_rows) & (lanes < SSM_HEADS)
    dtr = dtr_ref[...] + bias_ref[...]
    sp = jnp.maximum(dtr, 0.0) + jnp.log(1.0 + jnp.exp(-jnp.abs(dtr)))
    dt = jnp.where(valid, sp, 0.0)
    a = -jnp.exp(alog_ref[...])
    acol = _dot01_l(tri3, dt * a)
    return dt, a, acol, valid, dtr


def _row16(v):
    return jnp.broadcast_to(v, (16, v.shape[1]))


def _ssd_fwd(xbc_act, dtr, dt_bias, a_log, seq_rows, gather=()):
    lp = xbc_act.shape[0]
    nc = lp // CHUNK
    gw = SSM_WIDTH // SSM_GROUPS
    hpg = SSM_HEADS // SSM_GROUPS

    na = len(gather)

    def body(x_ref, b_ref, c_ref, dtr_ref, bias_ref, alog_ref, tri_ref, tri3_ref, ex3_ref, *rest):
        gin, (y_ref, hp_ref), gout, h_s, sems = rest[:na], rest[na:na + 2], rest[na + 2:2 * na + 2], rest[2 * na + 2], rest[2 * na + 3:]
        c = pl.program_id(0)

        @pl.when(c == 0)
        def _():
            h_s[...] = jnp.zeros_like(h_s)

        if na:
            g_start, g_forward, g_finish = _gather_ops(gin, gout, *sems)
            pl.when(c == 0)(g_start)
            pl.when(c == nc // 2)(g_forward)

        ex3 = ex3_ref[...]
        dt, a, acol, _, _ = _ssd_prep(dtr_ref, bias_ref, alog_ref, tri3_ref[...], c, seq_rows)
        arow = acol.T
        dtrow = dt.T
        alast = acol[CHUNK - 1:CHUNK, :]
        e_all = _dot01_r(jnp.exp(acol), ex3)
        wx_all = _dot01_r(jnp.exp(alast - acol) * dt, ex3)
        dec_all = _dot01_r(_row16(jnp.exp(alast)), ex3)[0:1, :]
        causal = tri_ref[...] > 0.5
        hp_ref[0] = h_s[...]
        for g in range(SSM_GROUPS):
            gs = slice(g * gw, (g + 1) * gw)
            bg = b_ref[:, g * SSM_STATE:(g + 1) * SSM_STATE]
            cg = c_ref[:, g * SSM_STATE:(g + 1) * SSM_STATE].astype(BF16)
            xg = x_ref[:, gs]
            hg = h_s[:, gs]
            gm = _bdot(cg, bg, NT)
            y_off = _bdot(cg, hg) * e_all[:, gs]
            for r in range(hpg):
                hd = g * hpg + r
                seg = acol[:, hd:hd + 1] - arow[hd:hd + 1, :]
                lm = jnp.where(causal, jnp.exp(jnp.where(causal, seg, 0.0)), 0.0)
                w = gm * lm * dtrow[hd:hd + 1, :]
                cs = slice(r * SSM_HEAD_DIM, (r + 1) * SSM_HEAD_DIM)
                y_ref[:, pl.ds(hd * SSM_HEAD_DIM, SSM_HEAD_DIM)] = _bdot(w, xg[:, cs]) + y_off[:, cs]
            st = _bdot(bg.T, xg * wx_all[:, gs])
            h_s[:, gs] = hg * dec_all[:, gs] + st

        if na:
            pl.when(c == nc - 1)(g_finish)

    xs_spec = pl.BlockSpec((CHUNK, SSM_WIDTH), lambda c: (c, 0))
    b_spec = pl.BlockSpec((CHUNK, 2 * SSM_STATE), lambda c: (c, SSM_WIDTH // (2 * SSM_STATE)))
    c_spec = pl.BlockSpec((CHUNK, 2 * SSM_STATE), lambda c: (c, SSM_WIDTH // (2 * SSM_STATE) + 1))
    y, hprev, *gathered = pl.pallas_call(
        body, name="ssd_fwd", grid=(nc,),
        in_specs=[xs_spec, b_spec, c_spec, pl.BlockSpec((CHUNK, LANE), lambda c: (c, 0)),
                  _fullspec((1, LANE)), _fullspec((1, LANE)), _fullspec((CHUNK, CHUNK)), _fullspec((CHUNK, 3 * CHUNK)),
                  _fullspec((3 * LANE, SSM_WIDTH))] + [ANY] * na,
        out_specs=[xs_spec, pl.BlockSpec((1, SSM_STATE, SSM_WIDTH), lambda c: (c, 0, 0))] + [ANY] * na,
        out_shape=[_sds((lp, SSM_WIDTH), F32), _sds((nc, SSM_STATE, SSM_WIDTH), F32)]
        + [_sds((N_DEV,) + s.shape, s.dtype) for s in gather],
        scratch_shapes=[pltpu.VMEM((SSM_STATE, SSM_WIDTH), F32)] + (_gather_scratch(na) if na else []),
        compiler_params=_cp(("arbitrary",)),
    )(xbc_act, xbc_act, xbc_act, dtr, dt_bias, a_log, _tri_mat(), _x3(_tri_mat(), 1), _x3(_expand_mat(), 0), *gather)
    return y, hprev, gathered


def _group_mean(x):
    gw = SSM_WIDTH // SSM_GROUPS
    parts = [jnp.broadcast_to(jnp.mean(x[:, g * gw:(g + 1) * gw], axis=-1, keepdims=True), (x.shape[0], gw))
             for g in range(SSM_GROUPS)]
    return jnp.concatenate(parts, axis=1)


def _ssd_post(y, xbc_act, z, dskip, gnorm):
    lp = y.shape[0]

    def body(y_ref, x_ref, z_ref, d_ref, g_ref, o_ref):
        z_ = z_ref[...]
        gt = (y_ref[...] + d_ref[...] * x_ref[...]) * _silu(z_)
        r = lax.rsqrt(_group_mean(gt * gt) + EPS)
        o_ref[...] = ((gt * r) * g_ref[...]).astype(BF16)

    return pl.pallas_call(
        body, name="ssd_post", grid=(lp // ROWB,),
        in_specs=[_rowspec(ROWB, SSM_WIDTH)] * 3 + [_fullspec((1, SSM_WIDTH))] * 2,
        out_specs=_rowspec(ROWB, SSM_WIDTH), out_shape=_sds((lp, SSM_WIDTH), BF16),
        compiler_params=_cp(("arbitrary",)),
    )(y, xbc_act, z, dskip, gnorm)


def _out_proj(att, ssm, head, x, w_out, g_post):
    lp = att.shape[0]

    def body(a_ref, s_ref, head_ref, x_ref, w_ref, g_ref, mix_ref, h1_ref):
        mix = _dot(a_ref[...], w_ref[pl.ds(0, 1024), :]) + _dot(s_ref[...], w_ref[pl.ds(1024, 1024), :])
        mix_ref[...] = mix
        h1_ref[...] = _h_block(head_ref, x_ref) + _rms_fwd(mix, g_ref[...])

    return pl.pallas_call(
        body, name="out_proj", grid=(lp // ROWB,),
        in_specs=[_rowspec(ROWB, 1024)] * 2 + [_fullspec((ROWB, D_MODEL)), _xspec(), _fullspec((2048, D_MODEL)),
                                               _fullspec((1, D_MODEL))],
        out_specs=[_rowspec(ROWB, D_MODEL)] * 2, out_shape=[_sds((lp, D_MODEL), F32)] * 2,
        compiler_params=_cp(("arbitrary",)),
    )(att, ssm, head, x, w_out, g_post)


def _resident(w_hbm, w_vmem, sem):
    @pl.when(pl.program_id(0) == 0)
    def _():
        cp = pltpu.make_async_copy(w_hbm, w_vmem, sem)
        cp.start()
        cp.wait()


ANY = pl.BlockSpec(memory_space=pl.ANY)


def _mlp_fwd(h1, tgt, w_up, w_down, g_pre, g_post, seq_rows):
    lp = h1.shape[0]

    def body(h1_ref, t_ref, wu_hbm, wd_hbm, gpre_ref, gpost_ref, hn2_ref, f_ref, dh2_ref, loss_ref, wu, wd, sems):
        _resident(wu_hbm, wu, sems.at[0])
        _resident(wd_hbm, wd, sems.at[1])
        i = pl.program_id(0)
        h1_ = h1_ref[...]
        hn2 = _rms_fwd(h1_, gpre_ref[...]).astype(BF16)
        hn2_ref[...] = hn2
        u = jnp.maximum(_dot(hn2, wu[...]), 0.0)
        f = _dot((u * u).astype(BF16), wd[...])
        f_ref[...] = f
        h2 = h1_ + _rms_fwd(f, gpost_ref[...])
        rows = i * ROWB + lax.broadcasted_iota(jnp.int32, (ROWB, 1), 0)
        real = (rows >= PADF + N_META) & (rows < PADF + seq_rows)
        err = jnp.where(real, h2 - t_ref[...], 0.0)
        dh2_ref[...] = err * (1.0 / D_MODEL)
        _acc(loss_ref, _colsum8(err * err))

    return pl.pallas_call(
        body, name="mlp_fwd", grid=(lp // ROWB,),
        in_specs=[_rowspec(ROWB, D_MODEL), _xspec()] + [ANY, ANY] + [_fullspec((1, D_MODEL))] * 2,
        out_specs=[_rowspec(ROWB, D_MODEL)] * 3 + [_fullspec((8, D_MODEL))],
        out_shape=[_sds((lp, D_MODEL), BF16), _sds((lp, D_MODEL), F32), _sds((lp, D_MODEL), F32), _sds((8, D_MODEL), F32)],
        scratch_shapes=[pltpu.VMEM((D_MODEL, D_FF), BF16), pltpu.VMEM((D_FF, D_MODEL), BF16), pltpu.SemaphoreType.DMA((2,))],
        compiler_params=_cp(("arbitrary",)),
    )(h1, tgt, w_up, w_down, g_pre, g_post)


def _pad_cols(w, width):
    return jnp.pad(w, ((0, 0), (0, width - w.shape[1])))


def _layout_weights(w_in, w_q_up, w_kv_up):
    o = np.cumsum((0,) + IN_SPLITS)
    pieces = [w_in[:, o[k]:o[k + 1]] for k in range(6)]
    kr = jnp.pad(pieces[2], ((0, 0), (QK_NOPE, HEADW - QK_NOPE - QK_ROPE)))
    w_all = jnp.concatenate([pieces[0], pieces[1], kr, pieces[3], pieces[4], _pad_cols(pieces[5], LANE)], axis=1)
    wq = jnp.pad(w_q_up.reshape(Q_LORA, ATT_HEADS, QK_NOPE + QK_ROPE), ((0, 0), (0, 0), (0, HEADW - QK_NOPE - QK_ROPE)))
    wkv = w_kv_up.reshape(KV_LORA, ATT_HEADS, QK_NOPE + V_HEAD)
    wk = jnp.pad(wkv[:, :, :QK_NOPE], ((0, 0), (0, 0), (0, HEADW - QK_NOPE)))
    wv = wkv[:, :, QK_NOPE:]
    wvp = jnp.pad(wv, ((0, 0), (0, 0), (0, HEADW - V_HEAD)))
    return (w_all, wq.reshape(Q_LORA, -1), wk.reshape(KV_LORA, -1), wv.reshape(KV_LORA, -1),
            wvp.reshape(KV_LORA, -1))


def _rope_tables(lp):
    pos = jnp.maximum(jnp.arange(lp, dtype=jnp.int32) - PADF, 0).astype(F32)
    inv_freq = ROPE_THETA ** (-jnp.arange(0, QK_ROPE, 2, dtype=F32) / QK_ROPE)
    ang = pos[:, None] * inv_freq[None, :]
    cos, sin = jnp.cos(ang), jnp.sin(ang)
    one, zero = jnp.ones((lp, QK_NOPE), F32), jnp.zeros((lp, QK_NOPE), F32)
    z32, z64 = jnp.zeros((lp, 32), F32), jnp.zeros((lp, 64), F32)
    cos_t = jnp.concatenate([one, cos, cos, jnp.ones((lp, 64), F32)], axis=1)
    sa = jnp.concatenate([zero, -sin, z32, z64], axis=1)
    sb = jnp.concatenate([zero, z32, sin, z64], axis=1)
    return cos_t, sa, sb


def _row1(v, width=None):
    v = v.reshape(1, -1).astype(F32)
    return v if width is None else _pad_cols(v, width)


LATE = ("w_out", "w_mlp_up", "w_mlp_down")


def _local_forward(head, x, tgt, p, late=None):
    assert head.shape[0] == ROWB and x.shape[0] % ROWB == 0
    lp = ROWB + x.shape[0]
    seq_rows = N_META + x.shape[0]
    f = {"seq_rows": seq_rows}
    w_all, wq, wk, wv, wvp = _layout_weights(p["w_in"], p["w_q_up"], p["w_kv_up"])
    f.update(w_all=w_all, wq=wq, wk=wk, wv=wv)
    f["hn"], cq, ckv, kr, f["z"], f["xbc"], f["dtr"] = _norm_in_proj(head, x, _row1(p["norm_mix_pre"]), w_all)
    f.update(cq=cq, ckv=ckv)
    f["rope"] = _rope_tables(lp)
    f["q"], f["k"], f["v"], f["cqn"], f["ckvn"] = _qkv(cq, ckv, kr, *f["rope"], _row1(p["q_a_norm"]),
                                                   _row1(p["kv_a_norm"]), wq, wk, wvp)
    f["att"], f["lse"] = _flash_fwd(f["q"], f["k"], f["v"])
    f["cw"] = jnp.pad(p["conv_w"].astype(F32), ((0, 8 - CONV_K), (0, 0)))
    f["xact"] = _conv_fwd(f["xbc"], f["cw"], _row1(p["conv_b"]))
    f["dt_bias"], f["a_log"] = _row1(p["dt_bias"], LANE), _row1(p["a_log"], LANE)
    f["y"], f["hprev"], gathered = _ssd_fwd(f["xact"], f["dtr"], f["dt_bias"], f["a_log"], seq_rows,
                                            gather=[late[n] for n in LATE] if late else ())
    p = {**p, **{n: _from_shards(n, s) for n, s in zip(LATE, gathered)}}
    f["p"] = p
    f["dskip"] = jnp.repeat(p["d_skip"].reshape(-1).astype(F32), SSM_HEAD_DIM).reshape(1, SSM_WIDTH)
    f["ssm"] = _ssd_post(f["y"], f["xact"], f["z"], f["dskip"], _row1(p["ssm_norm"]))
    f["mix"], f["h1"] = _out_proj(f["att"], f["ssm"], head, x, p["w_out"], _row1(p["norm_mix_post"]))
    f["hn2"], f["f"], f["dh2"], loss8 = _mlp_fwd(f["h1"], tgt, p["w_mlp_up"], p["w_mlp_down"],
                                                 _row1(p["norm_mlp_pre"]), _row1(p["norm_mlp_post"]), seq_rows)
    f["loss"] = 0.5 * jnp.sum(loss8) / D_MODEL
    return f


MLPB = 256


def _mlp_bwd(dh2, f, h1, hn2, w_up, w_down, g_pre, g_post):
    lp = h1.shape[0]

    def body(dh2_ref, f_ref, h1_ref, hn2_ref, wu_hbm, wd_hbm, gpre_ref, gpost_ref,
             dh1_ref, du_ref, a_ref, df_ref, dgpre_ref, dgpost_ref, wu, wd, sems):
        _resident(wu_hbm, wu, sems.at[0])
        _resident(wd_hbm, wd, sems.at[1])
        dh2_ = dh2_ref[...]
        df, dgp = _rms_bwd(f_ref[...], gpost_ref[...], dh2_)
        dfb = df.astype(BF16)
        df_ref[...] = dfb
        da = _dot(dfb, wd[...], NT)
        u = jnp.maximum(_dot(hn2_ref[...], wu[...]), 0.0)
        a_ref[...] = (u * u).astype(BF16)
        du = (da * (2.0 * u)).astype(BF16)
        du_ref[...] = du
        dhn2 = _dot(du, wu[...], NT)
        dx, dgq = _rms_bwd(h1_ref[...], gpre_ref[...], dhn2)
        dh1_ref[...] = dh2_ + dx
        _acc(dgpre_ref, _colsum8(dgq))
        _acc(dgpost_ref, _colsum8(dgp))

    return pl.pallas_call(
        body, name="mlp_bwd", grid=(lp // MLPB,),
        in_specs=[_rowspec(MLPB, D_MODEL)] * 4 + [ANY, ANY] + [_fullspec((1, D_MODEL))] * 2,
        out_specs=[_rowspec(MLPB, D_MODEL), _rowspec(MLPB, D_FF), _rowspec(MLPB, D_FF), _rowspec(MLPB, D_MODEL),
                   _fullspec((8, D_MODEL)), _fullspec((8, D_MODEL))],
        out_shape=[_sds((lp, D_MODEL), F32), _sds((lp, D_FF), BF16), _sds((lp, D_FF), BF16), _sds((lp, D_MODEL), BF16),
                   _sds((8, D_MODEL), F32), _sds((8, D_MODEL), F32)],
        scratch_shapes=[pltpu.VMEM((D_MODEL, D_FF), BF16), pltpu.VMEM((D_FF, D_MODEL), BF16), pltpu.SemaphoreType.DMA((2,))],
        compiler_params=_cp(("arbitrary",)),
    )(dh2, f, h1, hn2, w_up, w_down, g_pre, g_post)


def _out_bwd(dh1, mix, att, w_out, g_post):
    lp = dh1.shape[0]

    def body(dh1_ref, mix_ref, att_ref, w_ref, g_ref, dmix_ref, datt_ref, dssm_ref, dg_ref, dl_ref):
        dmix, dg = _rms_bwd(mix_ref[...], g_ref[...], dh1_ref[...])
        dmb = dmix.astype(BF16)
        dmix_ref[...] = dmb
        datt = _dot(dmb, w_ref[pl.ds(0, 1024), :], NT).astype(BF16)
        datt_ref[...] = datt
        dssm_ref[...] = _dot(dmb, w_ref[pl.ds(1024, 1024), :], NT)
        _acc(dg_ref, _colsum8(dg))
        prod = datt.astype(F32) * att_ref[...].astype(F32)
        for hh in range(ATT_HEADS):
            d = jnp.sum(prod[:, hh * V_HEAD:(hh + 1) * V_HEAD], axis=1, keepdims=True)
            dl_ref[hh] = jnp.broadcast_to(d, (ROWB, LANE)).T[0:1, :]

    return pl.pallas_call(
        body, name="out_bwd", grid=(lp // ROWB,),
        in_specs=[_rowspec(ROWB, D_MODEL)] * 3 + [_fullspec((2048, D_MODEL)), _fullspec((1, D_MODEL))],
        out_specs=[_rowspec(ROWB, D_MODEL)] * 3 + [_fullspec((8, D_MODEL)),
                                                   pl.BlockSpec((ATT_HEADS, 1, ROWB), lambda i: (0, 0, i))],
        out_shape=[_sds((lp, D_MODEL), BF16), _sds((lp, 1024), BF16), _sds((lp, 1024), F32), _sds((8, D_MODEL), F32),
                   _sds((ATT_HEADS, 1, lp), F32)],
        compiler_params=_cp(("arbitrary",)),
    )(dh1, mix, att, w_out, g_post)


def _ssd_post_bwd(dssm, y, xact, z, dskip, gnorm):
    lp = y.shape[0]

    def body(do_ref, y_ref, x_ref, z_ref, d_ref, g_ref, dy_ref, dz_ref, dg_ref, dd_ref):
        z_, x_ = z_ref[...], x_ref[...]
        sg = _sigmoid(z_)
        sz = z_ * sg
        y2 = y_ref[...] + d_ref[...] * x_
        gt = y2 * sz
        r = lax.rsqrt(_group_mean(gt * gt) + EPS)
        gh = gt * r
        do = do_ref[...]
        dgh = do * g_ref[...]
        dgt = r * (dgh - gh * _group_mean(dgh * gh))
        dy2 = dgt * sz
        dy_ref[...] = dy2
        dz_ref[...] = (dgt * y2 * (sg * (1.0 + z_ * (1.0 - sg)))).astype(BF16)
        _acc(dg_ref, _colsum8(do * gh))
        _acc(dd_ref, _colsum8(dy2 * x_))

    return pl.pallas_call(
        body, name="ssd_post_bwd", grid=(lp // ROWB,),
        in_specs=[_rowspec(ROWB, SSM_WIDTH)] * 4 + [_fullspec((1, SSM_WIDTH))] * 2,
        out_specs=[_rowspec(ROWB, SSM_WIDTH)] * 2 + [_fullspec((8, SSM_WIDTH))] * 2,
        out_shape=[_sds((lp, SSM_WIDTH), F32), _sds((lp, SSM_WIDTH), BF16), _sds((8, SSM_WIDTH), F32), _sds((8, SSM_WIDTH), F32)],
        compiler_params=_cp(("arbitrary",)),
    )(dssm, y, xact, z, dskip, gnorm)


def _ssd_bwd(dy, xact, dtr, hprev, dt_bias, a_log, dskip, seq_rows, exchange=()):
    lp = xact.shape[0]
    nc = lp // CHUNK
    gw = SSM_WIDTH // SSM_GROUPS
    hpg = SSM_HEADS // SSM_GROUPS
    nb = SSM_WIDTH // (2 * SSM_STATE)
    na = len(exchange)

    def body(dy_ref, x_ref, b_ref, c_ref, dtr_ref, hp_ref, bias_ref, alog_ref, dsk_ref, tri_ref, tri3_ref, trit3_ref,
             ex3_ref, ext3_ref, *rest):
        xin, (dact_ref, ddtr_ref, da_ref, dbias_ref), xout = rest[:na], rest[na:na + 4], rest[na + 4:2 * na + 4]
        dh_s, sems = rest[2 * na + 4], rest[2 * na + 5:]
        step = pl.program_id(0)
        c = nc - 1 - step

        @pl.when(step == 0)
        def _():
            dh_s[...] = jnp.zeros_like(dh_s)

        if na:
            x_start, x_finish = _exchange_ops(xin, xout, *sems)
            pl.when(step == 0)(x_start)

        tri = tri_ref[...]
        ex3 = ex3_ref[...]
        dt, a, acol, valid, dtr_ = _ssd_prep(dtr_ref, bias_ref, alog_ref, tri3_ref[...], c, seq_rows)
        arow = acol.T
        dtrow = dt.T
        alast = acol[CHUNK - 1:CHUNK, :]
        e_all = _dot01_r(jnp.exp(acol), ex3)
        wgt0 = jnp.exp(alast - acol)
        wgt = wgt0 * dt
        wx_all = _dot01_r(wgt, ex3)
        elast = jnp.exp(alast)
        dec_all = _dot01_r(_row16(elast), ex3)[0:1, :]
        causal = tri > 0.5
        upper = tri.T > 0.5
        lane_id = lax.broadcasted_iota(jnp.int32, (1, LANE), 1)
        sub_id = lax.broadcasted_iota(jnp.int32, (CHUNK, 1), 0)
        dacol = jnp.zeros((CHUNK, LANE), F32)
        darowf = jnp.zeros((CHUNK, LANE), F32)
        ddtrowf = jnp.zeros((CHUNK, LANE), F32)
        dwgt = jnp.zeros((CHUNK, LANE), F32)
        delast = jnp.zeros((1, LANE), F32)
        for g in range(SSM_GROUPS):
            gs = slice(g * gw, (g + 1) * gw)
            ext3_g = ext3_ref[g]
            bg = b_ref[:, g * SSM_STATE:(g + 1) * SSM_STATE]
            cg = c_ref[:, g * SSM_STATE:(g + 1) * SSM_STATE]
            bgb, cgb = bg.astype(BF16), cg.astype(BF16)
            xg = x_ref[:, gs]
            dyg = dy_ref[:, gs]
            hg = hp_ref[0, :, gs]
            dhg = dh_s[:, gs]
            hgb, dhgb = hg.astype(BF16), dhg.astype(BF16)
            gm = _dot(cgb, bgb, NT)
            gmt = _dot(bgb, cgb, NT)
            y_off = _dot(cgb, hgb) * e_all[:, gs]
            dy0 = (dyg * e_all[:, gs]).astype(BF16)
            dcg = _dot(dy0, hgb, NT)
            dh_in = _dot(cg.T.astype(BF16), dy0) + dhg * dec_all[:, gs]
            dacol = dacol + _dot01_r(dyg * y_off, ext3_g)
            xw = xg * wx_all[:, gs]
            dxw = _dot(bgb, dhgb)
            dx_state = dxw * wx_all[:, gs]
            dwgt = dwgt + _dot01_r(dxw * xg, ext3_g)
            dbt = _dot(dhgb, xw.astype(BF16), NT)
            hh = _colsum8(dhg * hg)
            hh16 = jnp.concatenate([hh, jnp.zeros_like(hh)], axis=0)
            delast = delast + jnp.sum(_dot01_r(hh16, ext3_g), axis=0, keepdims=True)
            dgm = jnp.zeros((CHUNK, CHUNK), F32)
            for r in range(hpg):
                hd = g * hpg + r
                cs = slice(r * SSM_HEAD_DIM, (r + 1) * SSM_HEAD_DIM)
                acol_r, arow_r = acol[:, hd:hd + 1], arow[hd:hd + 1, :]
                dtrow_r, dtcol_r = dtrow[hd:hd + 1, :], dt[:, hd:hd + 1]
                lm = jnp.where(causal, jnp.exp(jnp.where(causal, acol_r - arow_r, 0.0)), 0.0)
                lmt = jnp.where(upper, jnp.exp(jnp.where(upper, arow_r - acol_r, 0.0)), 0.0)
                wt = gmt * lmt * dtcol_r
                dy_r = dyg[:, cs].astype(BF16)
                dx_r = _dot(wt.astype(BF16), dy_r)
                dw = _dot(dy_r, xg[:, cs].astype(BF16), NT)
                t1 = dw * lm
                dgm = dgm + t1 * dtrow_r
                q1 = t1 * gm
                m = q1 * dtrow_r
                dacol = dacol + jnp.sum(m, axis=1, keepdims=True) * (lane_id == hd).astype(F32)
                darowf = darowf - (sub_id == hd).astype(F32) * jnp.sum(m, axis=0, keepdims=True)
                ddtrowf = ddtrowf + (sub_id == hd).astype(F32) * jnp.sum(q1, axis=0, keepdims=True)
                dact_ref[:, pl.ds(hd * SSM_HEAD_DIM, SSM_HEAD_DIM)] = (
                    dx_r + dx_state[:, cs] + dyg[:, cs] * dsk_ref[:, pl.ds(hd * SSM_HEAD_DIM, SSM_HEAD_DIM)])
            dgmb = dgm.astype(BF16)
            dact_ref[:, pl.ds(SSM_WIDTH + g * SSM_STATE, SSM_STATE)] = dbt.T + _dot(dgm.T.astype(BF16), cgb)
            dact_ref[:, pl.ds(SSM_WIDTH + 2 * SSM_STATE + g * SSM_STATE, SSM_STATE)] = dcg + _dot(dgmb, bgb)
            dh_s[:, gs] = dh_in
        t = dwgt * wgt
        dalast = jnp.sum(t, axis=0, keepdims=True) + delast * elast
        dacol_tot = dacol - t + darowf.T + (sub_id == CHUNK - 1).astype(F32) * dalast
        dda = _dot01_l(trit3_ref[...], dacol_tot)
        ddt = dwgt * wgt0 + ddtrowf.T + dda * a
        ddtr = jnp.where(valid, ddt * _sigmoid(dtr_), 0.0)
        ddtr_ref[...] = ddtr
        _acc(da_ref, _colsum8(dda * dt) * a)
        _acc(dbias_ref, _colsum8(ddtr))
        if na:
            pl.when(step == nc - 1)(x_finish)

    rev = lambda c: nc - 1 - c
    xs_spec = pl.BlockSpec((CHUNK, SSM_WIDTH), lambda c: (rev(c), 0))
    dact, ddtr, da8, dbias8, *received = pl.pallas_call(
        body, name="ssd_bwd", grid=(nc,),
        in_specs=[xs_spec, xs_spec,
                  pl.BlockSpec((CHUNK, 2 * SSM_STATE), lambda c: (rev(c), nb)),
                  pl.BlockSpec((CHUNK, 2 * SSM_STATE), lambda c: (rev(c), nb + 1)),
                  pl.BlockSpec((CHUNK, LANE), lambda c: (rev(c), 0)),
                  pl.BlockSpec((1, SSM_STATE, SSM_WIDTH), lambda c: (rev(c), 0, 0)),
                  _fullspec((1, LANE)), _fullspec((1, LANE)), _fullspec((1, SSM_WIDTH)),
                  _fullspec((CHUNK, CHUNK)), _fullspec((CHUNK, 3 * CHUNK)), _fullspec((CHUNK, 3 * CHUNK)),
                  _fullspec((3 * LANE, SSM_WIDTH)), _fullspec((SSM_GROUPS, 3 * gw, LANE))] + [ANY] * na,
        out_specs=[pl.BlockSpec((CHUNK, CONV_DIM), lambda c: (rev(c), 0)), pl.BlockSpec((CHUNK, LANE), lambda c: (rev(c), 0)),
                   _fullspec((8, LANE)), _fullspec((8, LANE))] + [ANY] * na,
        out_shape=[_sds((lp, CONV_DIM), F32), _sds((lp, LANE), F32), _sds((8, LANE), F32), _sds((8, LANE), F32)]
        + [_sds(e.shape, e.dtype) for e in exchange],
        scratch_shapes=[pltpu.VMEM((SSM_STATE, SSM_WIDTH), F32)] + (_gather_scratch(na) if na else []),
        compiler_params=_cp(("arbitrary",)),
    )(dy, xact, xact, xact, dtr, hprev, dt_bias, a_log, dskip, _tri_mat(), _x3(_tri_mat(), 1), _x3(_tri_mat().T, 1),
      _x3(_expand_mat(), 0), jnp.stack([_x3(_expand_mat().T[g * gw:(g + 1) * gw], 0) for g in range(SSM_GROUPS)]),
      *exchange)
    return dact, ddtr, da8, dbias8, received


def _conv_bwd(dact, xbc, cw, cb):
    lp, c = xbc.shape
    t8 = ROWB // 8
    nb = lp // ROWB

    def body(d_ref, dnext_ref, x_ref, prev_ref, next_ref, w_ref, b_ref, dx_ref, dw_ref, db_ref, xb, dp):
        i = pl.program_id(0)
        last = i == nb - 1
        xb[pl.ds(0, 8), :] = jnp.where(i > 0, prev_ref[...], 0.0)
        xb[pl.ds(8, ROWB), :] = x_ref[...]
        xb[pl.ds(8 + ROWB, 8), :] = jnp.where(last, 0.0, next_ref[...])

        @pl.when(i == 0)
        def _():
            dw_ref[...] = jnp.zeros_like(dw_ref)
            db_ref[...] = jnp.zeros_like(db_ref)

        sub = lax.broadcasted_iota(jnp.int32, (8, 1), 0)
        x0 = 8 - (CONV_K - 1)

        def strip(s, carry):
            cs = pl.ds(pl.multiple_of(s * LANE, LANE), LANE)
            w, b = w_ref[:, cs], b_ref[:, cs]

            def dpre_rows(r0, n, d):
                xs = [xb[pl.ds(x0 + kk + r0, n), cs] for kk in range(CONV_K)]
                pre = b + sum(w[kk:kk + 1, :] * xs[kk] for kk in range(CONV_K))
                sg = _sigmoid(pre)
                return d * (sg * (1.0 + pre * (1.0 - sg))), xs

            dws = [jnp.zeros((8, LANE), F32) for _ in range(CONV_K)]
            dbs = jnp.zeros((8, LANE), F32)
            for r0 in range(0, ROWB, CONV_ROWS):
                dpre, xs = dpre_rows(r0, CONV_ROWS, d_ref[pl.ds(r0, CONV_ROWS), cs])
                dp[pl.ds(r0, CONV_ROWS), cs] = dpre
                dbs = dbs + _colsum8(dpre)
                for kk in range(CONV_K):
                    dws[kk] = dws[kk] + _colsum8(dpre * xs[kk])
            dp[pl.ds(ROWB, 8), cs] = dpre_rows(ROWB, 8, jnp.where(last, 0.0, dnext_ref[:, cs]))[0]
            dwv = sum(jnp.where(sub == kk, jnp.sum(dws[kk], axis=0, keepdims=True), 0.0) for kk in range(CONV_K))
            dw_ref[:, cs] += dwv
            db_ref[:, cs] += dbs
            for r0 in range(0, ROWB, CONV_ROWS):
                dx = sum(w[kk:kk + 1, :] * dp[pl.ds(CONV_K - 1 - kk + r0, CONV_ROWS), cs] for kk in range(CONV_K))
                dx_ref[pl.ds(r0, CONV_ROWS), cs] = dx.astype(BF16)
            return carry

        lax.fori_loop(0, c // LANE, strip, 0)

    nxt = lambda i: (jnp.minimum((i + 1) * t8, lp // 8 - 1), 0)
    prv = lambda i: (jnp.maximum(i * t8 - 1, 0), 0)
    return pl.pallas_call(
        body, name="conv_bwd", grid=(nb,),
        in_specs=[_rowspec(ROWB, c), pl.BlockSpec((8, c), nxt), _rowspec(ROWB, c), pl.BlockSpec((8, c), prv),
                  pl.BlockSpec((8, c), nxt), _fullspec((8, c)), _fullspec((1, c))],
        out_specs=[_rowspec(ROWB, c), _fullspec((8, c)), _fullspec((8, c))],
        out_shape=[_sds((lp, c), BF16), _sds((8, c), F32), _sds((8, c), F32)],
        scratch_shapes=[pltpu.VMEM((ROWB + 16, c), F32), pltpu.VMEM((ROWB + 8, c), F32)],
        compiler_params=_cp(("arbitrary",)),
    )(dact, dact, xbc, xbc, xbc, cw, cb)


def _flash_bwd(q, k, v, datt, lse_row, delta_row):
    lp = q.shape[0]
    nk = lp // ROWB

    def body(k_ref, v_ref, q_ref, do_ref, lse_ref, dl_ref, dq_ref, dk_ref, dv_ref, dq_acc, dk_acc, dv_acc):
        j = pl.program_id(1)

        @pl.when(j == 0)
        def _():
            dq_acc[...] = jnp.zeros_like(dq_acc)

        kb, vb = k_ref[...], v_ref[...]
        dk_acc[...] = jnp.zeros_like(dk_acc)
        dv_acc[...] = jnp.zeros_like(dv_acc)

        def tile(i, masked):
            off = pl.multiple_of(i * ROWB, ROWB)
            qb = q_ref[pl.ds(off, ROWB), :]
            dob = do_ref[pl.ds(off, ROWB), :]
            lse_r = lse_ref[0, :, pl.ds(off, ROWB)]
            dl_r = dl_ref[0, :, pl.ds(off, ROWB)]
            st = _dot(kb, qb, NT)
            if masked:
                krow = j * ROWB + lax.broadcasted_iota(jnp.int32, st.shape, 0)
                qrow = i * ROWB + lax.broadcasted_iota(jnp.int32, st.shape, 1)
                st = jnp.where(_att_ok(qrow, krow), st, NEG)
            pt = jnp.exp2(st - lse_r)
            dv_acc[...] += _dot(pt.astype(BF16), dob)
            dpt = _dot(vb, dob, NT)
            dst = (pt * (dpt - dl_r)).astype(BF16)
            dk_acc[...] += _dot(dst, qb)
            dq_acc[pl.ds(off, ROWB), :] += _dot(dst, kb, ((0,), (0,)))

        tile(j, True)

        @pl.when(j == 0)
        def _():
            _pair_loop(1, nk, lambda i: tile(i, True), (2,))

        @pl.when(j > 0)
        def _():
            _pair_loop(j + 1, nk, lambda i: tile(i, False), (4,))

        dk_ref[...] = dk_acc[...] * LN2
        dv_ref[...] = dv_acc[...].astype(BF16)
        dq_ref[...] = dq_acc[pl.ds(pl.multiple_of(j * ROWB, ROWB), ROWB), :] * ATT_SCALE

    stat = pl.BlockSpec((1, 1, lp), lambda h, j: (h, 0, 0))
    blk = pl.BlockSpec((ROWB, HEADW), lambda h, j: (j, h))
    return pl.pallas_call(
        body, name="flash_bwd", grid=(ATT_HEADS, nk),
        in_specs=[blk, pl.BlockSpec((ROWB, V_HEAD), lambda h, j: (j, 2 * h)),
                  pl.BlockSpec((lp, HEADW), lambda h, j: (0, h)), pl.BlockSpec((lp, V_HEAD), lambda h, j: (0, h)),
                  stat, stat],
        out_specs=[blk, blk, pl.BlockSpec((ROWB, V_HEAD), lambda h, j: (j, h))],
        out_shape=[_sds((lp, ATT_HEADS * HEADW), F32), _sds((lp, ATT_HEADS * HEADW), F32),
                   _sds((lp, ATT_HEADS * V_HEAD), BF16)],
        scratch_shapes=[pltpu.VMEM((lp, HEADW), F32), pltpu.VMEM((ROWB, HEADW), F32), pltpu.VMEM((ROWB, V_HEAD), F32)],
        compiler_params=_cp(("arbitrary", "arbitrary")),
    )(k, v, q, datt, lse_row, delta_row)


def _qkv_bwd(dq, dk, dv, cq, ckv, cos, sa, sb, gq, gkv, wq, wk, wv):
    lp = cq.shape[0]
    qw = ATT_HEADS * HEADW

    def body(dq_ref, dk_ref, dv_ref, cq_ref, ckv_ref, cos_ref, sa_ref, sb_ref, gq_ref, gkv_ref, wq_ref, wk_ref, wv_ref,
             dcq_ref, dckv_ref, dkr_ref, dqp_ref, dkb_ref, dgq_ref, dgkv_ref):
        cos_, sa_, sb_ = cos_ref[...], sa_ref[...], sb_ref[...]
        dqp = _rope_t(dq_ref[...], _tile8(cos_), _tile8(sa_), _tile8(sb_)).astype(BF16)
        dqp_ref[...] = dqp
        dcq, dgq = _rms_bwd(cq_ref[...], gq_ref[...], _dot(dqp, wq_ref[...], NT))
        dcq_ref[...] = dcq.astype(BF16)
        dk_ = dk_ref[...]
        dkb = dk_.astype(BF16)
        dkb_ref[...] = dkb
        dksum = sum(dk_[:, hh * HEADW:(hh + 1) * HEADW] for hh in range(ATT_HEADS))
        dkr_ref[...] = _rope_t(dksum, cos_, sa_, sb_).astype(BF16)
        dckvn = _dot(dkb, wk_ref[...], NT) + _dot(dv_ref[...], wv_ref[...], NT)
        dckv, dgkv = _rms_bwd(ckv_ref[...], gkv_ref[...], dckvn)
        dckv_ref[...] = dckv.astype(BF16)
        _acc(dgq_ref, _colsum8(dgq))
        _acc(dgkv_ref, _colsum8(dgkv))

    return pl.pallas_call(
        body, name="qkv_bwd", grid=(lp // ROWB,),
        in_specs=[_rowspec(ROWB, qw), _rowspec(ROWB, qw), _rowspec(ROWB, ATT_HEADS * V_HEAD),
                  _rowspec(ROWB, Q_LORA), _rowspec(ROWB, KV_LORA)] + [_rowspec(ROWB, HEADW)] * 3
        + [_fullspec((1, Q_LORA)), _fullspec((1, KV_LORA)), _fullspec((Q_LORA, qw)), _fullspec((KV_LORA, qw)),
           _fullspec((KV_LORA, ATT_HEADS * V_HEAD))],
        out_specs=[_rowspec(ROWB, Q_LORA), _rowspec(ROWB, KV_LORA), _rowspec(ROWB, HEADW), _rowspec(ROWB, qw),
                   _rowspec(ROWB, qw), _fullspec((8, Q_LORA)), _fullspec((8, KV_LORA))],
        out_shape=[_sds((lp, Q_LORA), BF16), _sds((lp, KV_LORA), BF16), _sds((lp, HEADW), BF16), _sds((lp, qw), BF16),
                   _sds((lp, qw), BF16), _sds((8, Q_LORA), F32), _sds((8, KV_LORA), F32)],
        compiler_params=_cp(("arbitrary",)),
    )(dq, dk, dv, cq, ckv, cos, sa, sb, gq, gkv, wq, wk, wv)


def _in_bwd(dproj, head, x, dh1, g, w_all):
    lp = dh1.shape[0]

    def body(dp_ref, head_ref, x_ref, dh1_ref, g_ref, w_ref, dx_ref, dhead_ref, dg_ref):
        dx, dg = _rms_bwd(_h_block(head_ref, x_ref), g_ref[...], _dot(dp_ref[...], w_ref[...], NT))
        dh = dh1_ref[...] + dx

        @pl.when(pl.program_id(0) == 0)
        def _():
            dhead_ref[...] = dh

        @pl.when(pl.program_id(0) > 0)
        def _():
            dx_ref[...] = dh

        _acc(dg_ref, _colsum8(dg))

    return pl.pallas_call(
        body, name="in_bwd", grid=(lp // ROWB,),
        in_specs=[_rowspec(ROWB, PROJ_W), _fullspec((ROWB, D_MODEL)), _xspec(), _rowspec(ROWB, D_MODEL),
                  _fullspec((1, D_MODEL)), _fullspec((D_MODEL, PROJ_W))],
        out_specs=[_xspec(), _fullspec((ROWB, D_MODEL)), _fullspec((8, D_MODEL))],
        out_shape=[_sds(x.shape, F32), _sds((ROWB, D_MODEL), F32), _sds((8, D_MODEL), F32)],
        compiler_params=_cp(("arbitrary",)),
    )(dproj, head, x, dh1, g, w_all)


def _tile_of(n, cap=1024):
    return max(t for t in range(LANE, min(n, cap) + 1, LANE) if n % t == 0)


def _matmul_tn(name, a, b):
    rows, kd = a.shape
    nd = b.shape[1]
    tk, tn = _tile_of(kd), _tile_of(nd)
    rb = 3 * ROWB if rows % (3 * ROWB) == 0 else ROWB

    def body(a_ref, b_ref, o_ref):
        @pl.when(pl.program_id(2) == 0)
        def _():
            o_ref[...] = jnp.zeros_like(o_ref)

        o_ref[...] += _dot(a_ref[...], b_ref[...], ((0,), (0,)))

    return pl.pallas_call(
        body, name=name, grid=(kd // tk, nd // tn, rows // rb),
        in_specs=[pl.BlockSpec((rb, tk), lambda i, j, r: (r, i)), pl.BlockSpec((rb, tn), lambda i, j, r: (r, j))],
        out_specs=pl.BlockSpec((tk, tn), lambda i, j, r: (i, j)), out_shape=_sds((kd, nd), F32),
        compiler_params=_cp(("arbitrary", "arbitrary", "arbitrary")),
    )(a, b)


def _local_backward(head, x, f, exchange_late=False):
    p = f["p"]
    g = {}
    row = lambda v: _row1(v)
    s8 = lambda v: jnp.sum(v, axis=0)
    dh1, du, a_, df, dgpre, dgpost = _mlp_bwd(f["dh2"], f["f"], f["h1"], f["hn2"], p["w_mlp_up"], p["w_mlp_down"],
                                              row(p["norm_mlp_pre"]), row(p["norm_mlp_post"]))
    g["norm_mlp_pre"], g["norm_mlp_post"] = s8(dgpre), s8(dgpost)
    g["w_mlp_up"] = _matmul_tn("dw_mlp_up", f["hn2"], du)
    g["w_mlp_down"] = _matmul_tn("dw_mlp_down", a_, df)
    dmix, datt, dssm, dgmp, delta = _out_bwd(dh1, f["mix"], f["att"], p["w_out"], row(p["norm_mix_post"]))
    g["norm_mix_post"] = s8(dgmp)
    g["w_out"] = jnp.concatenate([_matmul_tn("dw_out_att", f["att"], dmix), _matmul_tn("dw_out_ssm", f["ssm"], dmix)], axis=0)
    dy, dz, dgn, dd = _ssd_post_bwd(dssm, f["y"], f["xact"], f["z"], f["dskip"], row(p["ssm_norm"]))
    g["ssm_norm"] = s8(dgn)
    g["d_skip"] = s8(dd).reshape(SSM_HEADS, SSM_HEAD_DIM).sum(axis=1)
    dact, ddtr, da8, dbias8, received = _ssd_bwd(
        dy, f["xact"], f["dtr"], f["hprev"], f["dt_bias"], f["a_log"], f["dskip"], f["seq_rows"],
        exchange=[_to_chunks(n, g[n]).astype(BF16) for n in LATE] if exchange_late else ())
    g["a_log"], g["dt_bias"] = s8(da8)[:SSM_HEADS], s8(dbias8)[:SSM_HEADS]
    dxbc, dcw8, dcb8 = _conv_bwd(dact, f["xbc"], f["cw"], row(p["conv_b"]))
    g["conv_w"], g["conv_b"] = dcw8[:CONV_K], s8(dcb8)
    dq, dk, dv = _flash_bwd(f["q"], f["k"], f["v"], datt, f["lse"], delta)
    dcq, dckv, dkr, dqp, dkb, dgq, dgkv = _qkv_bwd(dq, dk, dv, f["cq"], f["ckv"], *f["rope"], row(p["q_a_norm"]),
                                                   row(p["kv_a_norm"]), f["wq"], f["wk"], f["wv"])
    g["q_a_norm"], g["kv_a_norm"] = s8(dgq), s8(dgkv)
    dwq = _matmul_tn("dw_q_up", f["cqn"], dqp).reshape(Q_LORA, ATT_HEADS, HEADW)
    g["w_q_up"] = dwq[:, :, :QK_NOPE + QK_ROPE].reshape(Q_LORA, -1)
    dwk = _matmul_tn("dw_k_up", f["ckvn"], dkb).reshape(KV_LORA, ATT_HEADS, HEADW)[:, :, :QK_NOPE]
    dwv = _matmul_tn("dw_v_up", f["ckvn"], dv).reshape(KV_LORA, ATT_HEADS, V_HEAD)
    g["w_kv_up"] = jnp.concatenate([dwk, dwv], axis=2).reshape(KV_LORA, -1)
    dproj = jnp.concatenate([dcq, dckv, dkr, dz, dxbc, ddtr.astype(BF16)], axis=1)
    dx, dhead, dgin = _in_bwd(dproj, head, x, dh1, row(p["norm_mix_pre"]), f["w_all"])
    g["norm_mix_pre"] = s8(dgin)
    g["meta_tokens"] = dhead[PADF:]
    dwa = _matmul_tn("dw_in", f["hn"], dproj)
    g["w_in"] = jnp.concatenate([dwa[:, PC_Q:PC_KR], dwa[:, PC_KR + QK_NOPE:PC_KR + QK_NOPE + QK_ROPE],
                                 dwa[:, PC_Z:PC_DT + SSM_HEADS]], axis=1)
    return dx, g, received


BIG = {"w_in": ((D_MODEL, IN_WIDTH), 1), "w_q_up": ((Q_LORA, ATT_HEADS * (QK_NOPE + QK_ROPE)), 1),
       "w_kv_up": ((KV_LORA, ATT_HEADS * (QK_NOPE + V_HEAD)), 1), "w_out": ((2 * D_MODEL, D_MODEL), 0),
       "w_mlp_up": ((D_MODEL, D_FF), 1), "w_mlp_down": ((D_FF, D_MODEL), 0), "conv_w": ((CONV_K, CONV_DIM), 1),
       "meta_tokens": ((N_META, D_MODEL), 1)}
SMALL = {"norm_mix_pre": D_MODEL, "q_a_norm": Q_LORA, "kv_a_norm": KV_LORA, "conv_b": CONV_DIM, "dt_bias": SSM_HEADS,
         "a_log": SSM_HEADS, "d_skip": SSM_HEADS, "ssm_norm": SSM_WIDTH, "norm_mix_post": D_MODEL,
         "norm_mlp_pre": D_MODEL, "norm_mlp_post": D_MODEL}
WEIGHT_ORDER = ("meta_tokens", "norm_mix_pre", "w_in", "q_a_norm", "w_q_up", "kv_a_norm", "w_kv_up", "conv_w", "conv_b",
                "dt_bias", "a_log", "d_skip", "ssm_norm", "w_out", "norm_mix_post", "norm_mlp_pre", "w_mlp_up",
                "w_mlp_down", "norm_mlp_post")
ADAM_ROWS = 256


def _shard_shape(name):
    shape, ax = BIG[name]
    return tuple(d // N_DEV if a == ax else d for a, d in enumerate(shape))


SMALL_ROWS = -(-sum(SMALL.values()) // (LANE * 8)) * 8


def _pack(flats, rows):
    v = jnp.concatenate([f.reshape(-1) for f in flats])
    return jnp.pad(v, (0, rows * LANE - v.shape[0])).reshape(rows, LANE)


def _unpack(packed, shapes):
    v = packed.reshape(-1)
    out, o = [], 0
    for s in shapes:
        n = math.prod(s)
        out.append(v[o:o + n].reshape(s))
        o += n
    return out


def _to_chunks(name, full):
    shape, ax = BIG[name]
    if ax == 0:
        return full.reshape((N_DEV,) + _shard_shape(name))
    k, n = shape
    return full.reshape(k, N_DEV, n // N_DEV).transpose(1, 0, 2)


def _from_shards(name, shards):
    shape, ax = BIG[name]
    if ax == 0:
        return shards.reshape(shape)
    return shards.transpose(1, 0, 2).reshape(shape)


def _peer(k):
    x, y, c = lax.axis_index("x"), lax.axis_index("y"), lax.axis_index("c")
    px = 1 - x if k & 4 else x
    py = 1 - y if k & 2 else y
    pc = 1 - c if k & 1 else c
    return (px, py, pc), 4 * px + 2 * py + pc


def _gather_ops(x_refs, out_refs, send_sems, recv_sems, local_sems):
    na = len(x_refs)
    chips = (4, 2, 6)

    def copy(a, n, block, to, src=None):
        return pltpu.make_async_remote_copy(
            src_ref=out_refs[a].at[block] if src is None else src, dst_ref=out_refs[a].at[block],
            send_sem=send_sems.at[7 * a + n], recv_sem=recv_sems.at[7 * a + n], device_id=to, device_id_type=MESH)

    def mine():
        me = _peer(0)[1]
        return [pltpu.make_async_copy(x_refs[a], out_refs[a].at[me], local_sems.at[a]) for a in range(na)]

    def first():
        me, sibling = _peer(0)[1], _peer(1)[0]
        out = [copy(a, 0, me, sibling, src=x_refs[a]) for a in range(na)]
        return out + [copy(a, 1 + n, me, _peer(k)[0], src=x_refs[a]) for n, k in enumerate(chips) for a in range(na)]

    def passed():
        sibling = _peer(1)[0]
        return [copy(a, 4 + n, _peer(k)[1], sibling) for n, k in enumerate(chips) for a in range(na)]

    def start():
        for cp in mine() + first():
            cp.start()

    def forward():
        sibling = _peer(1)[0]
        fwd = passed()
        for n, k in enumerate(chips):
            for a in range(na):
                copy(a, 1 + n, _peer(k)[1], sibling).wait_recv()
                fwd[n * na + a].start()

    def finish():
        sibling = _peer(1)[0]
        for a in range(na):
            copy(a, 0, _peer(1)[1], sibling).wait_recv()
        for n, k in enumerate(chips):
            for a in range(na):
                copy(a, 4 + n, _peer(k | 1)[1], sibling).wait_recv()
        for cp in first() + passed():
            cp.wait_send()
        for cp in mine():
            cp.wait()

    return start, forward, finish


def _gather_scratch(na):
    return [pltpu.SemaphoreType.DMA((7 * na,)), pltpu.SemaphoreType.DMA((7 * na,)), pltpu.SemaphoreType.DMA((na,))]


def _all_gather(shards):
    na = len(shards)

    def body(*refs):
        for step in _gather_ops(refs[:na], refs[na:2 * na], *refs[2 * na:]):
            step()

    return pl.pallas_call(
        body, name="all_gather_weights", out_shape=[_sds((N_DEV,) + s.shape, s.dtype) for s in shards],
        in_specs=[ANY] * na, out_specs=[ANY] * na, scratch_shapes=_gather_scratch(na),
    )(*shards)


def _exchange(chunks, small):
    na = len(chunks) + 1

    def body(*refs):
        for step in _exchange_ops(refs[:na], refs[na:2 * na], *refs[2 * na:], whole=(na - 1,)):
            step()

    arrays = list(chunks) + [small]
    return pl.pallas_call(
        body, name="exchange_grads",
        out_shape=[_sds(c.shape, c.dtype) for c in chunks] + [_sds((N_DEV,) + small.shape, small.dtype)],
        in_specs=[ANY] * na, out_specs=[ANY] * na, scratch_shapes=_gather_scratch(na),
    )(*arrays)


def _exchange_ops(in_refs, out_refs, send_sems, recv_sems, local_sems, whole=()):
    na = len(in_refs)

    def src(a, idx):
        return in_refs[a] if a in whole else in_refs[a].at[idx]

    def own():
        me = _peer(0)[1]
        return [pltpu.make_async_copy(src(a, me), out_refs[a].at[me], local_sems.at[a]) for a in range(na)]

    def copy(a, k, sending):
        me = _peer(0)[1]
        to, idx = _peer(k)
        return pltpu.make_async_remote_copy(
            src_ref=src(a, idx if sending else me), dst_ref=out_refs[a].at[me if sending else idx],
            send_sem=send_sems.at[7 * a + k - 1], recv_sem=recv_sems.at[7 * a + k - 1],
            device_id=to, device_id_type=MESH)

    def sent():
        return [copy(a, k, True) for k in range(1, N_DEV) for a in range(na)]

    def start():
        for cp in own() + sent():
            cp.start()

    def finish():
        for k in range(1, N_DEV):
            for a in range(na):
                copy(a, k, False).wait_recv()
        for cp in sent():
            cp.wait_send()
        for cp in own():
            cp.wait()

    return start, finish


def _reduce_adamw(name, recv, w, m, v):
    rows, cols = w.shape
    blk = ADAM_ROWS if rows % ADAM_ROWS == 0 else rows
    c1 = 1.0 - ADAM_B1 ** ADAM_STEP
    c2 = 1.0 - ADAM_B2 ** ADAM_STEP

    def body(r_ref, w_ref, m_ref, v_ref, g_ref, d_ref, nm_ref, nv_ref):
        g = r_ref[0].astype(F32)
        for s in range(1, N_DEV):
            g = g + r_ref[s].astype(F32)
        g_ref[...] = g
        m_ = ADAM_B1 * m_ref[...] + (1.0 - ADAM_B1) * g
        v_ = ADAM_B2 * v_ref[...] + (1.0 - ADAM_B2) * (g * g)
        nm_ref[...] = m_
        nv_ref[...] = v_
        d_ref[...] = -ADAM_LR * ((m_ / c1) / (jnp.sqrt(v_ / c2) + ADAM_EPS) + ADAM_WD * w_ref[...])

    spec = _rowspec(blk, cols)
    return pl.pallas_call(
        body, name="reduce_adamw_" + name, grid=(rows // blk,),
        in_specs=[pl.BlockSpec((N_DEV, blk, cols), lambda i: (0, i, 0)), spec, spec, spec],
        out_specs=[spec] * 4, out_shape=[_sds((rows, cols), F32)] * 4,
        compiler_params=_cp(("arbitrary",)),
    )(recv, w, m, v)


def kernel(x, meta_tokens, norm_mix_pre, w_in, q_a_norm, w_q_up, kv_a_norm, w_kv_up, conv_w, conv_b, dt_bias, a_log, d_skip, ssm_norm, w_out, norm_mix_post, norm_mlp_pre, w_mlp_up, w_mlp_down, norm_mlp_post, loss_target, m_meta_tokens, m_norm_mix_pre, m_w_in, m_q_a_norm, m_w_q_up, m_kv_a_norm, m_w_kv_up, m_conv_w, m_conv_b, m_dt_bias, m_a_log, m_d_skip, m_ssm_norm, m_w_out, m_norm_mix_post, m_norm_mlp_pre, m_w_mlp_up, m_w_mlp_down, m_norm_mlp_post, v_meta_tokens, v_norm_mix_pre, v_w_in, v_q_a_norm, v_w_q_up, v_kv_a_norm, v_w_kv_up, v_conv_w, v_conv_b, v_dt_bias, v_a_log, v_d_skip, v_ssm_norm, v_w_out, v_norm_mix_post, v_norm_mlp_pre, v_w_mlp_up, v_w_mlp_down, v_norm_mlp_post):
    w = dict(meta_tokens=meta_tokens, norm_mix_pre=norm_mix_pre, w_in=w_in, q_a_norm=q_a_norm, w_q_up=w_q_up,
             kv_a_norm=kv_a_norm, w_kv_up=w_kv_up, conv_w=conv_w, conv_b=conv_b, dt_bias=dt_bias, a_log=a_log,
             d_skip=d_skip, ssm_norm=ssm_norm, w_out=w_out, norm_mix_post=norm_mix_post, norm_mlp_pre=norm_mlp_pre,
             w_mlp_up=w_mlp_up, w_mlp_down=w_mlp_down, norm_mlp_post=norm_mlp_post)
    m = dict(meta_tokens=m_meta_tokens, norm_mix_pre=m_norm_mix_pre, w_in=m_w_in, q_a_norm=m_q_a_norm, w_q_up=m_w_q_up,
             kv_a_norm=m_kv_a_norm, w_kv_up=m_w_kv_up, conv_w=m_conv_w, conv_b=m_conv_b, dt_bias=m_dt_bias,
             a_log=m_a_log, d_skip=m_d_skip, ssm_norm=m_ssm_norm, w_out=m_w_out, norm_mix_post=m_norm_mix_post,
             norm_mlp_pre=m_norm_mlp_pre, w_mlp_up=m_w_mlp_up, w_mlp_down=m_w_mlp_down, norm_mlp_post=m_norm_mlp_post)
    v = dict(meta_tokens=v_meta_tokens, norm_mix_pre=v_norm_mix_pre, w_in=v_w_in, q_a_norm=v_q_a_norm, w_q_up=v_w_q_up,
             kv_a_norm=v_kv_a_norm, w_kv_up=v_w_kv_up, conv_w=v_conv_w, conv_b=v_conv_b, dt_bias=v_dt_bias,
             a_log=v_a_log, d_skip=v_d_skip, ssm_norm=v_ssm_norm, w_out=v_w_out, norm_mix_post=v_norm_mix_post,
             norm_mlp_pre=v_norm_mlp_pre, w_mlp_up=v_w_mlp_up, w_mlp_down=v_w_mlp_down, norm_mlp_post=v_norm_mlp_post)
    big_names = [n for n in WEIGHT_ORDER if n in BIG]
    small_names = [n for n in WEIGHT_ORDER if n in SMALL]
    shard = lambda d, n: d[n].reshape(_shard_shape(n))

    f32_names = ("conv_w", "meta_tokens")
    early = [n for n in big_names if n not in LATE]
    gathered = _all_gather([shard(w, n).astype(F32 if n in f32_names else BF16) for n in early])
    p = {n: w[n].reshape(-1) for n in small_names}
    p.update({n: _from_shards(n, s) for n, s in zip(early, gathered)})
    head = jnp.concatenate([jnp.zeros((PADF, D_MODEL), F32), p["meta_tokens"]], axis=0)
    f = _local_forward(head, x[0], loss_target[0], p, late={n: shard(w, n).astype(BF16) for n in LATE})
    dx, g, recv_late = _local_backward(head, x[0], f, exchange_late=True)
    grad_x = dx[None]
    loss = lax.psum(f["loss"], ("x", "y", "c"))

    small = _pack([g[n] for n in small_names], SMALL_ROWS)
    *recv_early, recv_small = _exchange([_to_chunks(n, g[n]).astype(BF16) for n in early], small)
    recv_of = {**dict(zip(early, recv_early)), **dict(zip(LATE, recv_late))}

    outs = {}
    kinds = ("grad", "delta", "new_m", "new_v")
    for n, recv in ((n, recv_of[n]) for n in big_names):
        for kind, arr in zip(kinds, _reduce_adamw(n, recv, shard(w, n), shard(m, n), shard(v, n))):
            outs[kind, n] = arr.reshape(w[n].shape)
    packed = [_pack([d[n] for n in small_names], SMALL_ROWS) for d in (w, m, v)]
    for kind, arr in zip(kinds, _reduce_adamw("small", recv_small, *packed)):
        for n, val in zip(small_names, _unpack(arr, [(SMALL[n],) for n in small_names])):
            outs[kind, n] = val.reshape(w[n].shape)
    return (loss, grad_x) + tuple(outs[kind, n] for kind in ("grad", "delta", "new_m", "new_v") for n in WEIGHT_ORDER)
```

```python
import math

import jax
import jax.numpy as jnp
import numpy as np
from jax import lax
from jax.experimental import pallas as pl
from jax.experimental.pallas import tpu as pltpu

F32 = jnp.float32
BF16 = jnp.bfloat16

D_MODEL = 1024
N_META = 16
EPS = 1e-6
ATT_HEADS = 8
Q_LORA = 384
KV_LORA = 256
QK_NOPE = 128
QK_ROPE = 64
V_HEAD = 128
ROPE_THETA = 10000.0
SSM_HEADS = 16
SSM_HEAD_DIM = 64
SSM_WIDTH = 1024
SSM_GROUPS = 2
SSM_STATE = 128
CONV_K = 4
CHUNK = 128
CONV_DIM = 1536
D_FF = 4096
IN_SPLITS = (Q_LORA, KV_LORA, QK_ROPE, SSM_WIDTH, CONV_DIM, SSM_HEADS)
IN_WIDTH = sum(IN_SPLITS)
ADAM_LR, ADAM_B1, ADAM_B2, ADAM_EPS, ADAM_WD, ADAM_STEP = 0.001, 0.9, 0.999, 1e-08, 0.01, 10

LANE = 128
ROWB = 512
PADF = ROWB - N_META
HEADW = 256
PC_Q, PC_KV, PC_KR, PC_Z, PC_XBC, PC_DT, PROJ_W = 0, 384, 640, 896, 1920, 3456, 3584
NEG = -1e30
N_DEV = 8
VMEM_LIMIT = 56 * 1024 * 1024
MESH = pl.DeviceIdType.MESH


def _cp(sem, vmem=VMEM_LIMIT, **kw):
    return pltpu.CompilerParams(dimension_semantics=sem, vmem_limit_bytes=vmem, **kw)


def _dot(a, b, dims=((1,), (0,))):
    return lax.dot_general(a, b, (dims, ((), ())), preferred_element_type=F32)


def _bdot(a, b, dims=((1,), (0,))):
    return _dot(a.astype(BF16), b.astype(BF16), dims)


NT = ((1,), (1,))


def _rms_fwd(x, w):
    r = lax.rsqrt(jnp.mean(x * x, axis=-1, keepdims=True) + EPS)
    return (x * r) * w


def _rms_bwd(x, w, dy):
    r = lax.rsqrt(jnp.mean(x * x, axis=-1, keepdims=True) + EPS)
    xh = x * r
    g = dy * w
    dx = r * (g - xh * jnp.mean(g * xh, axis=-1, keepdims=True))
    return dx, dy * xh


def _sigmoid(x):
    return 0.5 * jnp.tanh(0.5 * x) + 0.5


def _colsum8(x):
    t, c = x.shape
    return jnp.sum(x.reshape(t // 8, 8, c), axis=0)


def _rowspec(t, c, cb=0):
    return pl.BlockSpec((t, c), lambda i: (i, cb))


def _fullspec(shape):
    n = len(shape)
    return pl.BlockSpec(shape, lambda i: (0,) * n)


def _sds(shape, dt):
    return jax.ShapeDtypeStruct(shape, dt)


def _acc(ref, val):
    @pl.when(pl.program_id(0) == 0)
    def _():
        ref[...] = jnp.zeros_like(ref)

    ref[...] += val


def _xspec():
    return pl.BlockSpec((ROWB, D_MODEL), lambda i: (jnp.maximum(i - 1, 0), 0))


def _h_block(head_ref, x_ref):
    return jnp.where(pl.program_id(0) == 0, head_ref[...], x_ref[...])


def _norm_in_proj(head, x, g, w_all):
    lp = head.shape[0] + x.shape[0]

    def body(head_ref, x_ref, g_ref, w_ref, hn_ref, cq_ref, ckv_ref, kr_ref, z_ref, xbc_ref, dt_ref):
        hn = _rms_fwd(_h_block(head_ref, x_ref), g_ref[...]).astype(BF16)
        hn_ref[...] = hn
        p = _dot(hn, w_ref[...])
        cq_ref[...] = p[:, PC_Q:PC_KV]
        ckv_ref[...] = p[:, PC_KV:PC_KR]
        kr_ref[...] = p[:, PC_KR:PC_Z]
        z_ref[...] = p[:, PC_Z:PC_XBC]
        xbc_ref[...] = p[:, PC_XBC:PC_DT]
        dt_ref[...] = p[:, PC_DT:PROJ_W]

    widths = (Q_LORA, KV_LORA, HEADW, SSM_WIDTH, CONV_DIM, LANE)
    return pl.pallas_call(
        body, name="norm_in_proj", grid=(lp // ROWB,),
        in_specs=[_fullspec((ROWB, D_MODEL)), _xspec(), _fullspec((1, D_MODEL)), _fullspec((D_MODEL, PROJ_W))],
        out_specs=[_rowspec(ROWB, D_MODEL)] + [_rowspec(ROWB, w) for w in widths],
        out_shape=[_sds((lp, D_MODEL), BF16)] + [_sds((lp, w), F32) for w in widths],
        compiler_params=_cp(("arbitrary",)),
    )(head, x, g, w_all)


def _rope(x, cos, sa, sb):
    w = x.shape[1]
    return x * cos + pltpu.roll(x, w - 32, 1) * sa + pltpu.roll(x, 32, 1) * sb


def _rope_t(dy, cos, sa, sb):
    w = dy.shape[1]
    return dy * cos + pltpu.roll(dy * sa, 32, 1) + pltpu.roll(dy * sb, w - 32, 1)


def _tile8(t):
    return jnp.concatenate([t] * ATT_HEADS, axis=1)


def _qkv(cq, ckv, kr, cos, sa, sb, gq, gkv, wq, wk, wv):
    lp = cq.shape[0]
    qw = ATT_HEADS * HEADW

    def body(cq_ref, ckv_ref, kr_ref, cos_ref, sa_ref, sb_ref, gq_ref, gkv_ref, wq_ref, wk_ref, wv_ref,
             q_ref, k_ref, v_ref, cqn_ref, ckvn_ref):
        cos_, sa_, sb_ = cos_ref[...], sa_ref[...], sb_ref[...]
        cqn = _rms_fwd(cq_ref[...], gq_ref[...]).astype(BF16)
        ckvn = _rms_fwd(ckv_ref[...], gkv_ref[...]).astype(BF16)
        cqn_ref[...] = cqn
        ckvn_ref[...] = ckvn
        q = _dot(cqn, wq_ref[...])
        q_ref[...] = (_rope(q, _tile8(cos_), _tile8(sa_), _tile8(sb_)) * Q_PRESCALE).astype(BF16)
        k = _dot(ckvn, wk_ref[...]) + _tile8(_rope(kr_ref[...], cos_, sa_, sb_))
        k_ref[...] = k.astype(BF16)
        lanes = lax.broadcasted_iota(jnp.int32, (1, qw), 1)
        ones = ((lanes % HEADW) >= V_HEAD).astype(F32)
        v_ref[...] = (_dot(ckvn, wv_ref[...]) + ones).astype(BF16)

    return pl.pallas_call(
        body, name="qkv", grid=(lp // ROWB,),
        in_specs=[_rowspec(ROWB, Q_LORA), _rowspec(ROWB, KV_LORA), _rowspec(ROWB, HEADW)]
        + [_rowspec(ROWB, HEADW)] * 3
        + [_fullspec((1, Q_LORA)), _fullspec((1, KV_LORA)), _fullspec((Q_LORA, qw)), _fullspec((KV_LORA, qw)),
           _fullspec((KV_LORA, qw))],
        out_specs=[_rowspec(ROWB, qw), _rowspec(ROWB, qw), _rowspec(ROWB, qw),
                   _rowspec(ROWB, Q_LORA), _rowspec(ROWB, KV_LORA)],
        out_shape=[_sds((lp, qw), BF16), _sds((lp, qw), BF16), _sds((lp, qw), BF16),
                   _sds((lp, Q_LORA), BF16), _sds((lp, KV_LORA), BF16)],
        compiler_params=_cp(("arbitrary",)),
    )(cq, ckv, kr, cos, sa, sb, gq, gkv, wq, wk, wv)


ATT_SCALE = (QK_NOPE + QK_ROPE) ** -0.5
LOG2E = 1.4426950408889634
LN2 = 0.6931471805599453
Q_PRESCALE = ATT_SCALE * LOG2E
KVB = 512


def _att_ok(qrow, krow):
    return (krow <= qrow) & ((krow >= PADF) | (qrow < PADF))


def _lanes(x, n):
    return x if n == 1 else jnp.concatenate([x] * n, axis=1)


def _pair_loop(lo, hi, tile, unrolls=(2,)):
    for u in tuple(unrolls) + (1,):
        n = jnp.maximum(hi - lo, 0)
        trips = n // u

        def many(t, c, u=u, lo=lo):
            for d in range(u):
                tile(lo + u * t + d)
            return c

        lax.fori_loop(0, trips, many, 0)
        lo = lo + trips * u


def _flash_fwd(q, k, v):
    lp = q.shape[0]
    nq = lp // ROWB
    rep = KVB // LANE

    def body(q_ref, k_ref, v_ref, o_ref, lse_ref, acc, m_s):
        i = pl.program_id(1)
        qb = q_ref[...]
        m_s[...] = jnp.full_like(m_s, NEG)
        acc[...] = jnp.zeros_like(acc)

        def tile(j, masked):
            off = pl.multiple_of(j * KVB, KVB)
            kb = k_ref[pl.ds(off, KVB), :]
            vb = v_ref[pl.ds(off, KVB), :]
            s = _dot(qb, kb, NT)
            if masked:
                qrow = i * ROWB + lax.broadcasted_iota(jnp.int32, s.shape, 0)
                krow = j * KVB + lax.broadcasted_iota(jnp.int32, s.shape, 1)
                s = jnp.where(_att_ok(qrow, krow), s, NEG)
            m_prev = m_s[...]
            m_new = jnp.maximum(m_prev, jnp.max(s, axis=1, keepdims=True))
            alpha = jnp.exp2(m_prev - m_new)
            p = jnp.exp2(s - _lanes(m_new, rep))
            acc[...] = _lanes(alpha, 2) * acc[...] + _dot(p.astype(BF16), vb)
            m_s[...] = m_new

        tile(0, True)
        _pair_loop(1, i, lambda j: tile(j, False), (16, 8, 4, 2))

        @pl.when(i > 0)
        def _():
            tile(i, True)

        l = acc[:, V_HEAD:]
        o_ref[...] = (acc[:, :V_HEAD] / l).astype(BF16)
        lse_ref[0] = (m_s[...] + jnp.log2(l)).T[0:1, :]

    return pl.pallas_call(
        body, name="flash_fwd", grid=(ATT_HEADS, nq),
        in_specs=[pl.BlockSpec((ROWB, HEADW), lambda h, i: (i, h)),
                  pl.BlockSpec((lp, HEADW), lambda h, i: (0, h)),
                  pl.BlockSpec((lp, HEADW), lambda h, i: (0, h))],
        out_specs=[pl.BlockSpec((ROWB, V_HEAD), lambda h, i: (i, h)),
                   pl.BlockSpec((1, 1, ROWB), lambda h, i: (h, 0, i))],
        out_shape=[_sds((lp, ATT_HEADS * V_HEAD), BF16), _sds((ATT_HEADS, 1, lp), F32)],
        scratch_shapes=[pltpu.VMEM((ROWB, HEADW), F32), pltpu.VMEM((ROWB, LANE), F32)],
        compiler_params=_cp(("arbitrary", "arbitrary")),
    )(q, k, v)


def _silu(x):
    return x * _sigmoid(x)


CONV_ROWS = 64


def _conv_fwd(xbc, cw, cb):
    lp, c = xbc.shape
    t8 = ROWB // 8

    def body(x_ref, prev_ref, w_ref, b_ref, o_ref, buf):
        i = pl.program_id(0)
        buf[pl.ds(0, 8), :] = jnp.where(i > 0, prev_ref[...], 0.0)
        buf[pl.ds(8, ROWB), :] = x_ref[...]

        def strip(s, carry):
            cs = pl.ds(pl.multiple_of(s * LANE, LANE), LANE)
            w, b = w_ref[:, cs], b_ref[:, cs]
            for r0 in range(0, ROWB, CONV_ROWS):
                pre = b + sum(w[kk:kk + 1, :] * buf[pl.ds(8 - (CONV_K - 1) + kk + r0, CONV_ROWS), cs]
                              for kk in range(CONV_K))
                o_ref[pl.ds(r0, CONV_ROWS), cs] = _silu(pre)
            return carry

        lax.fori_loop(0, c // LANE, strip, 0)

    return pl.pallas_call(
        body, name="conv_fwd", grid=(lp // ROWB,),
        in_specs=[_rowspec(ROWB, c), pl.BlockSpec((8, c), lambda i: (jnp.maximum(i * t8 - 1, 0), 0)),
                  _fullspec((8, c)), _fullspec((1, c))],
        out_specs=_rowspec(ROWB, c), out_shape=_sds((lp, c), F32),
        scratch_shapes=[pltpu.VMEM((ROWB + 8, c), F32)],
        compiler_params=_cp(("arbitrary",)),
    )(xbc, xbc, cw, cb)


def _expand_mat():
    r = np.arange(LANE)[:, None]
    c = np.arange(SSM_WIDTH)[None, :]
    return jnp.asarray((c // SSM_HEAD_DIM == r).astype(np.float32))


def _tri_mat():
    i = np.arange(CHUNK)
    return jnp.asarray((i[:, None] >= i[None, :]).astype(np.float32))


def _x3(m, axis):
    return jnp.concatenate([m.astype(BF16)] * 3, axis=axis)


def _split3(x):
    hi = x.astype(BF16)
    r = x - hi.astype(F32)
    mid = r.astype(BF16)
    return hi, mid, (r - mid.astype(F32)).astype(BF16)


def _dot01_r(x, m3):
    return _dot(jnp.concatenate(_split3(x), axis=1), m3)


def _dot01_l(m3, x):
    return _dot(m3, jnp.concatenate(_split3(x), axis=0))


def _ssd_prep(dtr_ref, bias_ref, alog_ref, tri3, c, seq_rows):
    rows = c * CHUNK + lax.broadcasted_iota(jnp.int32, (CHUNK, LANE), 0)
    lanes = lax.broadcasted_iota(jnp.int32, (CHUNK, LANE), 1)
    valid = (rows >= PADF) & (rows < PADF + seq_rows) & (lanes < SSM_HEADS)
    dtr = dtr_ref[...] + bias_ref[...]
    sp = jnp.maximum(dtr, 0.0) + jnp.log(1.0 + jnp.exp(-jnp.abs(dtr)))
    dt = jnp.where(valid, sp, 0.0)
    a = -jnp.exp(alog_ref[...])
    acol = _dot01_l(tri3, dt * a)
    return dt, a, acol, valid, dtr


def _row16(v):
    return jnp.broadcast_to(v, (16, v.shape[1]))


def _ssd_fwd(xbc_act, dtr, dt_bias, a_log, seq_rows, gather=()):
    lp = xbc_act.shape[0]
    nc = lp // CHUNK
    gw = SSM_WIDTH // SSM_GROUPS
    hpg = SSM_HEADS // SSM_GROUPS

    na = len(gather)

    def body(x_ref, b_ref, c_ref, dtr_ref, bias_ref, alog_ref, tri_ref, tri3_ref, ex3_ref, *rest):
        gin, (y_ref, hp_ref), gout, h_s, sems = rest[:na], rest[na:na + 2], rest[na + 2:2 * na + 2], rest[2 * na + 2], rest[2 * na + 3:]
        c = pl.program_id(0)

        @pl.when(c == 0)
        def _():
            h_s[...] = jnp.zeros_like(h_s)

        if na:
            g_start, g_forward, g_finish = _gather_ops(gin, gout, *sems)
            pl.when(c == 0)(g_start)
            pl.when(c == nc // 2)(g_forward)

        ex3 = ex3_ref[...]
        dt, a, acol, _, _ = _ssd_prep(dtr_ref, bias_ref, alog_ref, tri3_ref[...], c, seq_rows)
        arow = acol.T
        dtrow = dt.T
        alast = acol[CHUNK - 1:CHUNK, :]
        e_all = _dot01_r(jnp.exp(acol), ex3)
        wx_all = _dot01_r(jnp.exp(alast - acol) * dt, ex3)
        dec_all = _dot01_r(_row16(jnp.exp(alast)), ex3)[0:1, :]
        causal = tri_ref[...] > 0.5
        hp_ref[0] = h_s[...]
        for g in range(SSM_GROUPS):
            gs = slice(g * gw, (g + 1) * gw)
            bg = b_ref[:, g * SSM_STATE:(g + 1) * SSM_STATE]
            cg = c_ref[:, g * SSM_STATE:(g + 1) * SSM_STATE].astype(BF16)
            xg = x_ref[:, gs]
            hg = h_s[:, gs]
            gm = _bdot(cg, bg, NT)
            y_off = _bdot(cg, hg) * e_all[:, gs]
            for r in range(hpg):
                hd = g * hpg + r
                seg = acol[:, hd:hd + 1] - arow[hd:hd + 1, :]
                lm = jnp.where(causal, jnp.exp(jnp.where(causal, seg, 0.0)), 0.0)
                w = gm * lm * dtrow[hd:hd + 1, :]
                cs = slice(r * SSM_HEAD_DIM, (r + 1) * SSM_HEAD_DIM)
                y_ref[:, pl.ds(hd * SSM_HEAD_DIM, SSM_HEAD_DIM)] = _bdot(w, xg[:, cs]) + y_off[:, cs]
            st = _bdot(bg.T, xg * wx_all[:, gs])
            h_s[:, gs] = hg * dec_all[:, gs] + st

        if na:
            pl.when(c == nc - 1)(g_finish)

    xs_spec = pl.BlockSpec((CHUNK, SSM_WIDTH), lambda c: (c, 0))
    b_spec = pl.BlockSpec((CHUNK, 2 * SSM_STATE), lambda c: (c, SSM_WIDTH // (2 * SSM_STATE)))
    c_spec = pl.BlockSpec((CHUNK, 2 * SSM_STATE), lambda c: (c, SSM_WIDTH // (2 * SSM_STATE) + 1))
    y, hprev, *gathered = pl.pallas_call(
        body, name="ssd_fwd", grid=(nc,),
        in_specs=[xs_spec, b_spec, c_spec, pl.BlockSpec((CHUNK, LANE), lambda c: (c, 0)),
                  _fullspec((1, LANE)), _fullspec((1, LANE)), _fullspec((CHUNK, CHUNK)), _fullspec((CHUNK, 3 * CHUNK)),
                  _fullspec((3 * LANE, SSM_WIDTH))] + [ANY] * na,
        out_specs=[xs_spec, pl.BlockSpec((1, SSM_STATE, SSM_WIDTH), lambda c: (c, 0, 0))] + [ANY] * na,
        out_shape=[_sds((lp, SSM_WIDTH), F32), _sds((nc, SSM_STATE, SSM_WIDTH), F32)]
        + [_sds((N_DEV,) + s.shape, s.dtype) for s in gather],
        scratch_shapes=[pltpu.VMEM((SSM_STATE, SSM_WIDTH), F32)] + (_gather_scratch(na) if na else []),
        compiler_params=_cp(("arbitrary",)),
    )(xbc_act, xbc_act, xbc_act, dtr, dt_bias, a_log, _tri_mat(), _x3(_tri_mat(), 1), _x3(_expand_mat(), 0), *gather)
    return y, hprev, gathered


def _group_mean(x):
    gw = SSM_WIDTH // SSM_GROUPS
    parts = [jnp.broadcast_to(jnp.mean(x[:, g * gw:(g + 1) * gw], axis=-1, keepdims=True), (x.shape[0], gw))
             for g in range(SSM_GROUPS)]
    return jnp.concatenate(parts, axis=1)


def _ssd_post(y, xbc_act, z, dskip, gnorm):
    lp = y.shape[0]

    def body(y_ref, x_ref, z_ref, d_ref, g_ref, o_ref):
        z_ = z_ref[...]
        gt = (y_ref[...] + d_ref[...] * x_ref[...]) * _silu(z_)
        r = lax.rsqrt(_group_mean(gt * gt) + EPS)
        o_ref[...] = ((gt * r) * g_ref[...]).astype(BF16)

    return pl.pallas_call(
        body, name="ssd_post", grid=(lp // ROWB,),
        in_specs=[_rowspec(ROWB, SSM_WIDTH)] * 3 + [_fullspec((1, SSM_WIDTH))] * 2,
        out_specs=_rowspec(ROWB, SSM_WIDTH), out_shape=_sds((lp, SSM_WIDTH), BF16),
        compiler_params=_cp(("arbitrary",)),
    )(y, xbc_act, z, dskip, gnorm)


def _out_proj(att, ssm, head, x, w_out, g_post):
    lp = att.shape[0]

    def body(a_ref, s_ref, head_ref, x_ref, w_ref, g_ref, mix_ref, h1_ref):
        mix = _dot(a_ref[...], w_ref[pl.ds(0, 1024), :]) + _dot(s_ref[...], w_ref[pl.ds(1024, 1024), :])
        mix_ref[...] = mix
        h1_ref[...] = _h_block(head_ref, x_ref) + _rms_fwd(mix, g_ref[...])

    return pl.pallas_call(
        body, name="out_proj", grid=(lp // ROWB,),
        in_specs=[_rowspec(ROWB, 1024)] * 2 + [_fullspec((ROWB, D_MODEL)), _xspec(), _fullspec((2048, D_MODEL)),
                                               _fullspec((1, D_MODEL))],
        out_specs=[_rowspec(ROWB, D_MODEL)] * 2, out_shape=[_sds((lp, D_MODEL), F32)] * 2,
        compiler_params=_cp(("arbitrary",)),
    )(att, ssm, head, x, w_out, g_post)


def _resident(w_hbm, w_vmem, sem):
    @pl.when(pl.program_id(0) == 0)
    def _():
        cp = pltpu.make_async_copy(w_hbm, w_vmem, sem)
        cp.start()
        cp.wait()


ANY = pl.BlockSpec(memory_space=pl.ANY)


def _mlp_fwd(h1, tgt, w_up, w_down, g_pre, g_post, seq_rows):
    lp = h1.shape[0]

    def body(h1_ref, t_ref, wu_hbm, wd_hbm, gpre_ref, gpost_ref, hn2_ref, f_ref, dh2_ref, loss_ref, wu, wd, sems):
        _resident(wu_hbm, wu, sems.at[0])
        _resident(wd_hbm, wd, sems.at[1])
        i = pl.program_id(0)
        h1_ = h1_ref[...]
        hn2 = _rms_fwd(h1_, gpre_ref[...]).astype(BF16)
        hn2_ref[...] = hn2
        u = jnp.maximum(_dot(hn2, wu[...]), 0.0)
        f = _dot((u * u).astype(BF16), wd[...])
        f_ref[...] = f
        h2 = h1_ + _rms_fwd(f, gpost_ref[...])
        rows = i * ROWB + lax.broadcasted_iota(jnp.int32, (ROWB, 1), 0)
        real = (rows >= PADF + N_META) & (rows < PADF + seq_rows)
        err = jnp.where(real, h2 - t_ref[...], 0.0)
        dh2_ref[...] = err * (1.0 / D_MODEL)
        _acc(loss_ref, _colsum8(err * err))

    return pl.pallas_call(
        body, name="mlp_fwd", grid=(lp // ROWB,),
        in_specs=[_rowspec(ROWB, D_MODEL), _xspec()] + [ANY, ANY] + [_fullspec((1, D_MODEL))] * 2,
        out_specs=[_rowspec(ROWB, D_MODEL)] * 3 + [_fullspec((8, D_MODEL))],
        out_shape=[_sds((lp, D_MODEL), BF16), _sds((lp, D_MODEL), F32), _sds((lp, D_MODEL), F32), _sds((8, D_MODEL), F32)],
        scratch_shapes=[pltpu.VMEM((D_MODEL, D_FF), BF16), pltpu.VMEM((D_FF, D_MODEL), BF16), pltpu.SemaphoreType.DMA((2,))],
        compiler_params=_cp(("arbitrary",)),
    )(h1, tgt, w_up, w_down, g_pre, g_post)


def _pad_cols(w, width):
    return jnp.pad(w, ((0, 0), (0, width - w.shape[1])))


def _layout_weights(w_in, w_q_up, w_kv_up):
    o = np.cumsum((0,) + IN_SPLITS)
    pieces = [w_in[:, o[k]:o[k + 1]] for k in range(6)]
    kr = jnp.pad(pieces[2], ((0, 0), (QK_NOPE, HEADW - QK_NOPE - QK_ROPE)))
    w_all = jnp.concatenate([pieces[0], pieces[1], kr, pieces[3], pieces[4], _pad_cols(pieces[5], LANE)], axis=1)
    wq = jnp.pad(w_q_up.reshape(Q_LORA, ATT_HEADS, QK_NOPE + QK_ROPE), ((0, 0), (0, 0), (0, HEADW - QK_NOPE - QK_ROPE)))
    wkv = w_kv_up.reshape(KV_LORA, ATT_HEADS, QK_NOPE + V_HEAD)
    wk = jnp.pad(wkv[:, :, :QK_NOPE], ((0, 0), (0, 0), (0, HEADW - QK_NOPE)))
    wv = wkv[:, :, QK_NOPE:]
    wvp = jnp.pad(wv, ((0, 0), (0, 0), (0, HEADW - V_HEAD)))
    return (w_all, wq.reshape(Q_LORA, -1), wk.reshape(KV_LORA, -1), wv.reshape(KV_LORA, -1),
            wvp.reshape(KV_LORA, -1))


def _rope_tables(lp):
    pos = jnp.maximum(jnp.arange(lp, dtype=jnp.int32) - PADF, 0).astype(F32)
    inv_freq = ROPE_THETA ** (-jnp.arange(0, QK_ROPE, 2, dtype=F32) / QK_ROPE)
    ang = pos[:, None] * inv_freq[None, :]
    cos, sin = jnp.cos(ang), jnp.sin(ang)
    one, zero = jnp.ones((lp, QK_NOPE), F32), jnp.zeros((lp, QK_NOPE), F32)
    z32, z64 = jnp.zeros((lp, 32), F32), jnp.zeros((lp, 64), F32)
    cos_t = jnp.concatenate([one, cos, cos, jnp.ones((lp, 64), F32)], axis=1)
    sa = jnp.concatenate([zero, -sin, z32, z64], axis=1)
    sb = jnp.concatenate([zero, z32, sin, z64], axis=1)
    return cos_t, sa, sb


def _row1(v, width=None):
    v = v.reshape(1, -1).astype(F32)
    return v if width is None else _pad_cols(v, width)


LATE = ("w_out", "w_mlp_up", "w_mlp_down")


def _local_forward(head, x, tgt, p, late=None):
    assert head.shape[0] == ROWB and x.shape[0] % ROWB == 0
    lp = ROWB + x.shape[0]
    seq_rows = N_META + x.shape[0]
    f = {"seq_rows": seq_rows}
    w_all, wq, wk, wv, wvp = _layout_weights(p["w_in"], p["w_q_up"], p["w_kv_up"])
    f.update(w_all=w_all, wq=wq, wk=wk, wv=wv)
    f["hn"], cq, ckv, kr, f["z"], f["xbc"], f["dtr"] = _norm_in_proj(head, x, _row1(p["norm_mix_pre"]), w_all)
    f.update(cq=cq, ckv=ckv)
    f["rope"] = _rope_tables(lp)
    f["q"], f["k"], f["v"], f["cqn"], f["ckvn"] = _qkv(cq, ckv, kr, *f["rope"], _row1(p["q_a_norm"]),
                                                   _row1(p["kv_a_norm"]), wq, wk, wvp)
    f["att"], f["lse"] = _flash_fwd(f["q"], f["k"], f["v"])
    f["cw"] = jnp.pad(p["conv_w"].astype(F32), ((0, 8 - CONV_K), (0, 0)))
    f["xact"] = _conv_fwd(f["xbc"], f["cw"], _row1(p["conv_b"]))
    f["dt_bias"], f["a_log"] = _row1(p["dt_bias"], LANE), _row1(p["a_log"], LANE)
    f["y"], f["hprev"], gathered = _ssd_fwd(f["xact"], f["dtr"], f["dt_bias"], f["a_log"], seq_rows,
                                            gather=[late[n] for n in LATE] if late else ())
    p = {**p, **{n: _from_shards(n, s) for n, s in zip(LATE, gathered)}}
    f["p"] = p
    f["dskip"] = jnp.repeat(p["d_skip"].reshape(-1).astype(F32), SSM_HEAD_DIM).reshape(1, SSM_WIDTH)
    f["ssm"] = _ssd_post(f["y"], f["xact"], f["z"], f["dskip"], _row1(p["ssm_norm"]))
    f["mix"], f["h1"] = _out_proj(f["att"], f["ssm"], head, x, p["w_out"], _row1(p["norm_mix_post"]))
    f["hn2"], f["f"], f["dh2"], loss8 = _mlp_fwd(f["h1"], tgt, p["w_mlp_up"], p["w_mlp_down"],
                                                 _row1(p["norm_mlp_pre"]), _row1(p["norm_mlp_post"]), seq_rows)
    f["loss"] = 0.5 * jnp.sum(loss8) / D_MODEL
    return f


MLPB = 256


def _mlp_bwd(dh2, f, h1, hn2, w_up, w_down, g_pre, g_post):
    lp = h1.shape[0]

    def body(dh2_ref, f_ref, h1_ref, hn2_ref, wu_hbm, wd_hbm, gpre_ref, gpost_ref,
             dh1_ref, du_ref, a_ref, df_ref, dgpre_ref, dgpost_ref, wu, wd, sems):
        _resident(wu_hbm, wu, sems.at[0])
        _resident(wd_hbm, wd, sems.at[1])
        dh2_ = dh2_ref[...]
        df, dgp = _rms_bwd(f_ref[...], gpost_ref[...], dh2_)
        dfb = df.astype(BF16)
        df_ref[...] = dfb
        da = _dot(dfb, wd[...], NT)
        u = jnp.maximum(_dot(hn2_ref[...], wu[...]), 0.0)
        a_ref[...] = (u * u).astype(BF16)
        du = (da * (2.0 * u)).astype(BF16)
        du_ref[...] = du
        dhn2 = _dot(du, wu[...], NT)
        dx, dgq = _rms_bwd(h1_ref[...], gpre_ref[...], dhn2)
        dh1_ref[...] = dh2_ + dx
        _acc(dgpre_ref, _colsum8(dgq))
        _acc(dgpost_ref, _colsum8(dgp))

    return pl.pallas_call(
        body, name="mlp_bwd", grid=(lp // MLPB,),
        in_specs=[_rowspec(MLPB, D_MODEL)] * 4 + [ANY, ANY] + [_fullspec((1, D_MODEL))] * 2,
        out_specs=[_rowspec(MLPB, D_MODEL), _rowspec(MLPB, D_FF), _rowspec(MLPB, D_FF), _rowspec(MLPB, D_MODEL),
                   _fullspec((8, D_MODEL)), _fullspec((8, D_MODEL))],
        out_shape=[_sds((lp, D_MODEL), F32), _sds((lp, D_FF), BF16), _sds((lp, D_FF), BF16), _sds((lp, D_MODEL), BF16),
                   _sds((8, D_MODEL), F32), _sds((8, D_MODEL), F32)],
        scratch_shapes=[pltpu.VMEM((D_MODEL, D_FF), BF16), pltpu.VMEM((D_FF, D_MODEL), BF16), pltpu.SemaphoreType.DMA((2,))],
        compiler_params=_cp(("arbitrary",)),
    )(dh2, f, h1, hn2, w_up, w_down, g_pre, g_post)


def _out_bwd(dh1, mix, att, w_out, g_post):
    lp = dh1.shape[0]

    def body(dh1_ref, mix_ref, att_ref, w_ref, g_ref, dmix_ref, datt_ref, dssm_ref, dg_ref, dl_ref):
        dmix, dg = _rms_bwd(mix_ref[...], g_ref[...], dh1_ref[...])
        dmb = dmix.astype(BF16)
        dmix_ref[...] = dmb
        datt = _dot(dmb, w_ref[pl.ds(0, 1024), :], NT).astype(BF16)
        datt_ref[...] = datt
        dssm_ref[...] = _dot(dmb, w_ref[pl.ds(1024, 1024), :], NT)
        _acc(dg_ref, _colsum8(dg))
        prod = datt.astype(F32) * att_ref[...].astype(F32)
        for hh in range(ATT_HEADS):
            d = jnp.sum(prod[:, hh * V_HEAD:(hh + 1) * V_HEAD], axis=1, keepdims=True)
            dl_ref[hh] = jnp.broadcast_to(d, (ROWB, LANE)).T[0:1, :]

    return pl.pallas_call(
        body, name="out_bwd", grid=(lp // ROWB,),
        in_specs=[_rowspec(ROWB, D_MODEL)] * 3 + [_fullspec((2048, D_MODEL)), _fullspec((1, D_MODEL))],
        out_specs=[_rowspec(ROWB, D_MODEL)] * 3 + [_fullspec((8, D_MODEL)),
                                                   pl.BlockSpec((ATT_HEADS, 1, ROWB), lambda i: (0, 0, i))],
        out_shape=[_sds((lp, D_MODEL), BF16), _sds((lp, 1024), BF16), _sds((lp, 1024), F32), _sds((8, D_MODEL), F32),
                   _sds((ATT_HEADS, 1, lp), F32)],
        compiler_params=_cp(("arbitrary",)),
    )(dh1, mix, att, w_out, g_post)


def _ssd_post_bwd(dssm, y, xact, z, dskip, gnorm):
    lp = y.shape[0]

    def body(do_ref, y_ref, x_ref, z_ref, d_ref, g_ref, dy_ref, dz_ref, dg_ref, dd_ref):
        z_, x_ = z_ref[...], x_ref[...]
        sg = _sigmoid(z_)
        sz = z_ * sg
        y2 = y_ref[...] + d_ref[...] * x_
        gt = y2 * sz
        r = lax.rsqrt(_group_mean(gt * gt) + EPS)
        gh = gt * r
        do = do_ref[...]
        dgh = do * g_ref[...]
        dgt = r * (dgh - gh * _group_mean(dgh * gh))
        dy2 = dgt * sz
        dy_ref[...] = dy2
        dz_ref[...] = (dgt * y2 * (sg * (1.0 + z_ * (1.0 - sg)))).astype(BF16)
        _acc(dg_ref, _colsum8(do * gh))
        _acc(dd_ref, _colsum8(dy2 * x_))

    return pl.pallas_call(
        body, name="ssd_post_bwd", grid=(lp // ROWB,),
        in_specs=[_rowspec(ROWB, SSM_WIDTH)] * 4 + [_fullspec((1, SSM_WIDTH))] * 2,
        out_specs=[_rowspec(ROWB, SSM_WIDTH)] * 2 + [_fullspec((8, SSM_WIDTH))] * 2,
        out_shape=[_sds((lp, SSM_WIDTH), F32), _sds((lp, SSM_WIDTH), BF16), _sds((8, SSM_WIDTH), F32), _sds((8, SSM_WIDTH), F32)],
        compiler_params=_cp(("arbitrary",)),
    )(dssm, y, xact, z, dskip, gnorm)


def _ssd_bwd(dy, xact, dtr, hprev, dt_bias, a_log, dskip, seq_rows, exchange=()):
    lp = xact.shape[0]
    nc = lp // CHUNK
    gw = SSM_WIDTH // SSM_GROUPS
    hpg = SSM_HEADS // SSM_GROUPS
    nb = SSM_WIDTH // (2 * SSM_STATE)
    na = len(exchange)

    def body(dy_ref, x_ref, b_ref, c_ref, dtr_ref, hp_ref, bias_ref, alog_ref, dsk_ref, tri_ref, tri3_ref, trit3_ref,
             ex3_ref, ext3_ref, *rest):
        xin, (dact_ref, ddtr_ref, da_ref, dbias_ref), xout = rest[:na], rest[na:na + 4], rest[na + 4:2 * na + 4]
        dh_s, sems = rest[2 * na + 4], rest[2 * na + 5:]
        step = pl.program_id(0)
        c = nc - 1 - step

        @pl.when(step == 0)
        def _():
            dh_s[...] = jnp.zeros_like(dh_s)

        if na:
            x_start, x_finish = _exchange_ops(xin, xout, *sems)
            pl.when(step == 0)(x_start)

        tri = tri_ref[...]
        ex3 = ex3_ref[...]
        dt, a, acol, valid, dtr_ = _ssd_prep(dtr_ref, bias_ref, alog_ref, tri3_ref[...], c, seq_rows)
        arow = acol.T
        dtrow = dt.T
        alast = acol[CHUNK - 1:CHUNK, :]
        e_all = _dot01_r(jnp.exp(acol), ex3)
        wgt0 = jnp.exp(alast - acol)
        wgt = wgt0 * dt
        wx_all = _dot01_r(wgt, ex3)
        elast = jnp.exp(alast)
        dec_all = _dot01_r(_row16(elast), ex3)[0:1, :]
        causal = tri > 0.5
        upper = tri.T > 0.5
        lane_id = lax.broadcasted_iota(jnp.int32, (1, LANE), 1)
        sub_id = lax.broadcasted_iota(jnp.int32, (CHUNK, 1), 0)
        dacol = jnp.zeros((CHUNK, LANE), F32)
        darowf = jnp.zeros((CHUNK, LANE), F32)
        ddtrowf = jnp.zeros((CHUNK, LANE), F32)
        dwgt = jnp.zeros((CHUNK, LANE), F32)
        delast = jnp.zeros((1, LANE), F32)
        for g in range(SSM_GROUPS):
            gs = slice(g * gw, (g + 1) * gw)
            ext3_g = ext3_ref[g]
            bg = b_ref[:, g * SSM_STATE:(g + 1) * SSM_STATE]
            cg = c_ref[:, g * SSM_STATE:(g + 1) * SSM_STATE]
            bgb, cgb = bg.astype(BF16), cg.astype(BF16)
            xg = x_ref[:, gs]
            dyg = dy_ref[:, gs]
            hg = hp_ref[0, :, gs]
            dhg = dh_s[:, gs]
            hgb, dhgb = hg.astype(BF16), dhg.astype(BF16)
            gm = _dot(cgb, bgb, NT)
            gmt = _dot(bgb, cgb, NT)
            y_off = _dot(cgb, hgb) * e_all[:, gs]
            dy0 = (dyg * e_all[:, gs]).astype(BF16)
            dcg = _dot(dy0, hgb, NT)
            dh_in = _dot(cg.T.astype(BF16), dy0) + dhg * dec_all[:, gs]
            dacol = dacol + _dot01_r(dyg * y_off, ext3_g)
            xw = xg * wx_all[:, gs]
            dxw = _dot(bgb, dhgb)
            dx_state = dxw * wx_all[:, gs]
            dwgt = dwgt + _dot01_r(dxw * xg, ext3_g)
            dbt = _dot(dhgb, xw.astype(BF16), NT)
            hh = _colsum8(dhg * hg)
            hh16 = jnp.concatenate([hh, jnp.zeros_like(hh)], axis=0)
            delast = delast + jnp.sum(_dot01_r(hh16, ext3_g), axis=0, keepdims=True)
            dgm = jnp.zeros((CHUNK, CHUNK), F32)
            for r in range(hpg):
                hd = g * hpg + r
                cs = slice(r * SSM_HEAD_DIM, (r + 1) * SSM_HEAD_DIM)
                acol_r, arow_r = acol[:, hd:hd + 1], arow[hd:hd + 1, :]
                dtrow_r, dtcol_r = dtrow[hd:hd + 1, :], dt[:, hd:hd + 1]
                lm = jnp.where(causal, jnp.exp(jnp.where(causal, acol_r - arow_r, 0.0)), 0.0)
                lmt = jnp.where(upper, jnp.exp(jnp.where(upper, arow_r - acol_r, 0.0)), 0.0)
                wt = gmt * lmt * dtcol_r
                dy_r = dyg[:, cs].astype(BF16)
                dx_r = _dot(wt.astype(BF16), dy_r)
                dw = _dot(dy_r, xg[:, cs].astype(BF16), NT)
                t1 = dw * lm
                dgm = dgm + t1 * dtrow_r
                q1 = t1 * gm
                m = q1 * dtrow_r
                dacol = dacol + jnp.sum(m, axis=1, keepdims=True) * (lane_id == hd).astype(F32)
                darowf = darowf - (sub_id == hd).astype(F32) * jnp.sum(m, axis=0, keepdims=True)
                ddtrowf = ddtrowf + (sub_id == hd).astype(F32) * jnp.sum(q1, axis=0, keepdims=True)
                dact_ref[:, pl.ds(hd * SSM_HEAD_DIM, SSM_HEAD_DIM)] = (
                    dx_r + dx_state[:, cs] + dyg[:, cs] * dsk_ref[:, pl.ds(hd * SSM_HEAD_DIM, SSM_HEAD_DIM)])
            dgmb = dgm.astype(BF16)
            dact_ref[:, pl.ds(SSM_WIDTH + g * SSM_STATE, SSM_STATE)] = dbt.T + _dot(dgm.T.astype(BF16), cgb)
            dact_ref[:, pl.ds(SSM_WIDTH + 2 * SSM_STATE + g * SSM_STATE, SSM_STATE)] = dcg + _dot(dgmb, bgb)
            dh_s[:, gs] = dh_in
        t = dwgt * wgt
        dalast = jnp.sum(t, axis=0, keepdims=True) + delast * elast
        dacol_tot = dacol - t + darowf.T + (sub_id == CHUNK - 1).astype(F32) * dalast
        dda = _dot01_l(trit3_ref[...], dacol_tot)
        ddt = dwgt * wgt0 + ddtrowf.T + dda * a
        ddtr = jnp.where(valid, ddt * _sigmoid(dtr_), 0.0)
        ddtr_ref[...] = ddtr
        _acc(da_ref, _colsum8(dda * dt) * a)
        _acc(dbias_ref, _colsum8(ddtr))
        if na:
            pl.when(step == nc - 1)(x_finish)

    rev = lambda c: nc - 1 - c
    xs_spec = pl.BlockSpec((CHUNK, SSM_WIDTH), lambda c: (rev(c), 0))
    dact, ddtr, da8, dbias8, *received = pl.pallas_call(
        body, name="ssd_bwd", grid=(nc,),
        in_specs=[xs_spec, xs_spec,
                  pl.BlockSpec((CHUNK, 2 * SSM_STATE), lambda c: (rev(c), nb)),
                  pl.BlockSpec((CHUNK, 2 * SSM_STATE), lambda c: (rev(c), nb + 1)),
                  pl.BlockSpec((CHUNK, LANE), lambda c: (rev(c), 0)),
                  pl.BlockSpec((1, SSM_STATE, SSM_WIDTH), lambda c: (rev(c), 0, 0)),
                  _fullspec((1, LANE)), _fullspec((1, LANE)), _fullspec((1, SSM_WIDTH)),
                  _fullspec((CHUNK, CHUNK)), _fullspec((CHUNK, 3 * CHUNK)), _fullspec((CHUNK, 3 * CHUNK)),
                  _fullspec((3 * LANE, SSM_WIDTH)), _fullspec((SSM_GROUPS, 3 * gw, LANE))] + [ANY] * na,
        out_specs=[pl.BlockSpec((CHUNK, CONV_DIM), lambda c: (rev(c), 0)), pl.BlockSpec((CHUNK, LANE), lambda c: (rev(c), 0)),
                   _fullspec((8, LANE)), _fullspec((8, LANE))] + [ANY] * na,
        out_shape=[_sds((lp, CONV_DIM), F32), _sds((lp, LANE), F32), _sds((8, LANE), F32), _sds((8, LANE), F32)]
        + [_sds(e.shape, e.dtype) for e in exchange],
        scratch_shapes=[pltpu.VMEM((SSM_STATE, SSM_WIDTH), F32)] + (_gather_scratch(na) if na else []),
        compiler_params=_cp(("arbitrary",)),
    )(dy, xact, xact, xact, dtr, hprev, dt_bias, a_log, dskip, _tri_mat(), _x3(_tri_mat(), 1), _x3(_tri_mat().T, 1),
      _x3(_expand_mat(), 0), jnp.stack([_x3(_expand_mat().T[g * gw:(g + 1) * gw], 0) for g in range(SSM_GROUPS)]),
      *exchange)
    return dact, ddtr, da8, dbias8, received


def _conv_bwd(dact, xbc, cw, cb):
    lp, c = xbc.shape
    t8 = ROWB // 8
    nb = lp // ROWB

    def body(d_ref, dnext_ref, x_ref, prev_ref, next_ref, w_ref, b_ref, dx_ref, dw_ref, db_ref, xb, dp):
        i = pl.program_id(0)
        last = i == nb - 1
        xb[pl.ds(0, 8), :] = jnp.where(i > 0, prev_ref[...], 0.0)
        xb[pl.ds(8, ROWB), :] = x_ref[...]
        xb[pl.ds(8 + ROWB, 8), :] = jnp.where(last, 0.0, next_ref[...])

        @pl.when(i == 0)
        def _():
            dw_ref[...] = jnp.zeros_like(dw_ref)
            db_ref[...] = jnp.zeros_like(db_ref)

        sub = lax.broadcasted_iota(jnp.int32, (8, 1), 0)
        x0 = 8 - (CONV_K - 1)

        def strip(s, carry):
            cs = pl.ds(pl.multiple_of(s * LANE, LANE), LANE)
            w, b = w_ref[:, cs], b_ref[:, cs]

            def dpre_rows(r0, n, d):
                xs = [xb[pl.ds(x0 + kk + r0, n), cs] for kk in range(CONV_K)]
                pre = b + sum(w[kk:kk + 1, :] * xs[kk] for kk in range(CONV_K))
                sg = _sigmoid(pre)
                return d * (sg * (1.0 + pre * (1.0 - sg))), xs

            dws = [jnp.zeros((8, LANE), F32) for _ in range(CONV_K)]
            dbs = jnp.zeros((8, LANE), F32)
            for r0 in range(0, ROWB, CONV_ROWS):
                dpre, xs = dpre_rows(r0, CONV_ROWS, d_ref[pl.ds(r0, CONV_ROWS), cs])
                dp[pl.ds(r0, CONV_ROWS), cs] = dpre
                dbs = dbs + _colsum8(dpre)
                for kk in range(CONV_K):
                    dws[kk] = dws[kk] + _colsum8(dpre * xs[kk])
            dp[pl.ds(ROWB, 8), cs] = dpre_rows(ROWB, 8, jnp.where(last, 0.0, dnext_ref[:, cs]))[0]
            dwv = sum(jnp.where(sub == kk, jnp.sum(dws[kk], axis=0, keepdims=True), 0.0) for kk in range(CONV_K))
            dw_ref[:, cs] += dwv
            db_ref[:, cs] += dbs
            for r0 in range(0, ROWB, CONV_ROWS):
                dx = sum(w[kk:kk + 1, :] * dp[pl.ds(CONV_K - 1 - kk + r0, CONV_ROWS), cs] for kk in range(CONV_K))
                dx_ref[pl.ds(r0, CONV_ROWS), cs] = dx.astype(BF16)
            return carry

        lax.fori_loop(0, c // LANE, strip, 0)

    nxt = lambda i: (jnp.minimum((i + 1) * t8, lp // 8 - 1), 0)
    prv = lambda i: (jnp.maximum(i * t8 - 1, 0), 0)
    return pl.pallas_call(
        body, name="conv_bwd", grid=(nb,),
        in_specs=[_rowspec(ROWB, c), pl.BlockSpec((8, c), nxt), _rowspec(ROWB, c), pl.BlockSpec((8, c), prv),
                  pl.BlockSpec((8, c), nxt), _fullspec((8, c)), _fullspec((1, c))],
        out_specs=[_rowspec(ROWB, c), _fullspec((8, c)), _fullspec((8, c))],
        out_shape=[_sds((lp, c), BF16), _sds((8, c), F32), _sds((8, c), F32)],
        scratch_shapes=[pltpu.VMEM((ROWB + 16, c), F32), pltpu.VMEM((ROWB + 8, c), F32)],
        compiler_params=_cp(("arbitrary",)),
    )(dact, dact, xbc, xbc, xbc, cw, cb)


def _flash_bwd(q, k, v, datt, lse_row, delta_row, cos, sa, sb):
    lp = q.shape[0]
    nk = lp // ROWB

    def body(k_ref, v_ref, q_ref, do_ref, lse_ref, dl_ref, cos_ref, sa_ref, sb_ref, dq_ref, dk_ref, dv_ref,
             dq_acc, dk_acc, dv_acc):
        j = pl.program_id(1)

        @pl.when(j == 0)
        def _():
            dq_acc[...] = jnp.zeros_like(dq_acc)

        kb, vb = k_ref[...], v_ref[...]
        dk_acc[...] = jnp.zeros_like(dk_acc)
        dv_acc[...] = jnp.zeros_like(dv_acc)

        def tile(i, masked):
            off = pl.multiple_of(i * ROWB, ROWB)
            qb = q_ref[pl.ds(off, ROWB), :]
            dob = do_ref[pl.ds(off, ROWB), :]
            lse_r = lse_ref[0, :, pl.ds(off, ROWB)]
            dl_r = dl_ref[0, :, pl.ds(off, ROWB)]
            st = _dot(kb, qb, NT)
            if masked:
                krow = j * ROWB + lax.broadcasted_iota(jnp.int32, st.shape, 0)
                qrow = i * ROWB + lax.broadcasted_iota(jnp.int32, st.shape, 1)
                st = jnp.where(_att_ok(qrow, krow), st, NEG)
            pt = jnp.exp2(st - lse_r)
            dv_acc[...] += _dot(pt.astype(BF16), dob)
            dpt = _dot(vb, dob, NT)
            dst = (pt * (dpt - dl_r)).astype(BF16)
            dk_acc[...] += _dot(dst, qb)
            dq_acc[pl.ds(off, ROWB), :] += _dot(dst, kb, ((0,), (0,)))

        tile(j, True)

        @pl.when(j == 0)
        def _():
            _pair_loop(1, nk, lambda i: tile(i, True), (2,))

        @pl.when(j > 0)
        def _():
            _pair_loop(j + 1, nk, lambda i: tile(i, False), (4, 2))

        dk_ref[...] = (dk_acc[...] * LN2).astype(BF16)
        dv_ref[...] = dv_acc[...].astype(BF16)
        dq = dq_acc[pl.ds(pl.multiple_of(j * ROWB, ROWB), ROWB), :] * ATT_SCALE
        dq_ref[...] = _rope_t(dq, cos_ref[...], sa_ref[...], sb_ref[...]).astype(BF16)

    stat = pl.BlockSpec((1, 1, lp), lambda h, j: (h, 0, 0))
    blk = pl.BlockSpec((ROWB, HEADW), lambda h, j: (j, h))
    tab = pl.BlockSpec((ROWB, HEADW), lambda h, j: (j, 0))
    return pl.pallas_call(
        body, name="flash_bwd", grid=(ATT_HEADS, nk),
        in_specs=[blk, pl.BlockSpec((ROWB, V_HEAD), lambda h, j: (j, 2 * h)),
                  pl.BlockSpec((lp, HEADW), lambda h, j: (0, h)), pl.BlockSpec((lp, V_HEAD), lambda h, j: (0, h)),
                  stat, stat, tab, tab, tab],
        out_specs=[blk, blk, pl.BlockSpec((ROWB, V_HEAD), lambda h, j: (j, h))],
        out_shape=[_sds((lp, ATT_HEADS * HEADW), BF16), _sds((lp, ATT_HEADS * HEADW), BF16),
                   _sds((lp, ATT_HEADS * V_HEAD), BF16)],
        scratch_shapes=[pltpu.VMEM((lp, HEADW), F32), pltpu.VMEM((ROWB, HEADW), F32), pltpu.VMEM((ROWB, V_HEAD), F32)],
        compiler_params=_cp(("arbitrary", "arbitrary")),
    )(k, v, q, datt, lse_row, delta_row, cos, sa, sb)


def _qkv_bwd(dqp, dk, dv, cq, ckv, cos, sa, sb, gq, gkv, wq, wk, wv):
    lp = cq.shape[0]
    qw = ATT_HEADS * HEADW

    def body(dqp_ref, dk_ref, dv_ref, cq_ref, ckv_ref, cos_ref, sa_ref, sb_ref, gq_ref, gkv_ref, wq_ref, wk_ref, wv_ref,
             dcq_ref, dckv_ref, dkr_ref, dgq_ref, dgkv_ref):
        dcq, dgq = _rms_bwd(cq_ref[...], gq_ref[...], _dot(dqp_ref[...], wq_ref[...], NT))
        dcq_ref[...] = dcq.astype(BF16)
        dkb = dk_ref[...]
        dksum = sum(dkb[:, hh * HEADW:(hh + 1) * HEADW].astype(F32) for hh in range(ATT_HEADS))
        dkr_ref[...] = _rope_t(dksum, cos_ref[...], sa_ref[...], sb_ref[...]).astype(BF16)
        dckvn = _dot(dkb, wk_ref[...], NT) + _dot(dv_ref[...], wv_ref[...], NT)
        dckv, dgkv = _rms_bwd(ckv_ref[...], gkv_ref[...], dckvn)
        dckv_ref[...] = dckv.astype(BF16)
        _acc(dgq_ref, _colsum8(dgq))
        _acc(dgkv_ref, _colsum8(dgkv))

    return pl.pallas_call(
        body, name="qkv_bwd", grid=(lp // ROWB,),
        in_specs=[_rowspec(ROWB, qw), _rowspec(ROWB, qw), _rowspec(ROWB, ATT_HEADS * V_HEAD),
                  _rowspec(ROWB, Q_LORA), _rowspec(ROWB, KV_LORA)] + [_rowspec(ROWB, HEADW)] * 3
        + [_fullspec((1, Q_LORA)), _fullspec((1, KV_LORA)), _fullspec((Q_LORA, qw)), _fullspec((KV_LORA, qw)),
           _fullspec((KV_LORA, ATT_HEADS * V_HEAD))],
        out_specs=[_rowspec(ROWB, Q_LORA), _rowspec(ROWB, KV_LORA), _rowspec(ROWB, HEADW),
                   _fullspec((8, Q_LORA)), _fullspec((8, KV_LORA))],
        out_shape=[_sds((lp, Q_LORA), BF16), _sds((lp, KV_LORA), BF16), _sds((lp, HEADW), BF16),
                   _sds((8, Q_LORA), F32), _sds((8, KV_LORA), F32)],
        compiler_params=_cp(("arbitrary",)),
    )(dqp, dk, dv, cq, ckv, cos, sa, sb, gq, gkv, wq, wk, wv)


def _in_bwd(pieces, head, x, dh1, g, w_all):
    lp = dh1.shape[0]
    npc = len(pieces)
    assert sum(pc.shape[1] for pc in pieces) == PROJ_W

    def body(*refs):
        head_ref, x_ref, dh1_ref, g_ref, w_ref, dx_ref, dhead_ref, dg_ref, dp_ref = refs[npc:]
        dp = jnp.concatenate([r[...].astype(BF16) for r in refs[:npc]], axis=1)
        dp_ref[...] = dp
        dx, dg = _rms_bwd(_h_block(head_ref, x_ref), g_ref[...], _dot(dp, w_ref[...], NT))
        dh = dh1_ref[...] + dx

        @pl.when(pl.program_id(0) == 0)
        def _():
            dhead_ref[...] = dh

        @pl.when(pl.program_id(0) > 0)
        def _():
            dx_ref[...] = dh

        _acc(dg_ref, _colsum8(dg))

    return pl.pallas_call(
        body, name="in_bwd", grid=(lp // ROWB,),
        in_specs=[_rowspec(ROWB, pc.shape[1]) for pc in pieces]
        + [_fullspec((ROWB, D_MODEL)), _xspec(), _rowspec(ROWB, D_MODEL), _fullspec((1, D_MODEL)),
           _fullspec((D_MODEL, PROJ_W))],
        out_specs=[_xspec(), _fullspec((ROWB, D_MODEL)), _fullspec((8, D_MODEL)), _rowspec(ROWB, PROJ_W)],
        out_shape=[_sds(x.shape, F32), _sds((ROWB, D_MODEL), F32), _sds((8, D_MODEL), F32), _sds((lp, PROJ_W), BF16)],
        compiler_params=_cp(("arbitrary",)),
    )(*pieces, head, x, dh1, g, w_all)


def _tile_of(n, cap=1024):
    return max(t for t in range(LANE, min(n, cap) + 1, LANE) if n % t == 0)


def _matmul_tn(name, a, b):
    rows, kd = a.shape
    nd = b.shape[1]
    tk, tn = _tile_of(kd), _tile_of(nd)
    rb = 3 * ROWB if rows % (3 * ROWB) == 0 else ROWB

    def body(a_ref, b_ref, o_ref):
        @pl.when(pl.program_id(2) == 0)
        def _():
            o_ref[...] = jnp.zeros_like(o_ref)

        o_ref[...] += _dot(a_ref[...], b_ref[...], ((0,), (0,)))

    return pl.pallas_call(
        body, name=name, grid=(kd // tk, nd // tn, rows // rb),
        in_specs=[pl.BlockSpec((rb, tk), lambda i, j, r: (r, i)), pl.BlockSpec((rb, tn), lambda i, j, r: (r, j))],
        out_specs=pl.BlockSpec((tk, tn), lambda i, j, r: (i, j)), out_shape=_sds((kd, nd), F32),
        compiler_params=_cp(("arbitrary", "arbitrary", "arbitrary")),
    )(a, b)


def _local_backward(head, x, f, exchange_late=False):
    p = f["p"]
    g = {}
    row = lambda v: _row1(v)
    s8 = lambda v: jnp.sum(v, axis=0)
    dh1, du, a_, df, dgpre, dgpost = _mlp_bwd(f["dh2"], f["f"], f["h1"], f["hn2"], p["w_mlp_up"], p["w_mlp_down"],
                                              row(p["norm_mlp_pre"]), row(p["norm_mlp_post"]))
    g["norm_mlp_pre"], g["norm_mlp_post"] = s8(dgpre), s8(dgpost)
    g["w_mlp_up"] = _matmul_tn("dw_mlp_up", f["hn2"], du)
    g["w_mlp_down"] = _matmul_tn("dw_mlp_down", a_, df)
    dmix, datt, dssm, dgmp, delta = _out_bwd(dh1, f["mix"], f["att"], p["w_out"], row(p["norm_mix_post"]))
    g["norm_mix_post"] = s8(dgmp)
    g["w_out"] = jnp.concatenate([_matmul_tn("dw_out_att", f["att"], dmix), _matmul_tn("dw_out_ssm", f["ssm"], dmix)], axis=0)
    dy, dz, dgn, dd = _ssd_post_bwd(dssm, f["y"], f["xact"], f["z"], f["dskip"], row(p["ssm_norm"]))
    g["ssm_norm"] = s8(dgn)
    g["d_skip"] = s8(dd).reshape(SSM_HEADS, SSM_HEAD_DIM).sum(axis=1)
    dact, ddtr, da8, dbias8, received = _ssd_bwd(
        dy, f["xact"], f["dtr"], f["hprev"], f["dt_bias"], f["a_log"], f["dskip"], f["seq_rows"],
        exchange=[_to_chunks(n, g[n]).astype(BF16) for n in LATE] if exchange_late else ())
    g["a_log"], g["dt_bias"] = s8(da8)[:SSM_HEADS], s8(dbias8)[:SSM_HEADS]
    dxbc, dcw8, dcb8 = _conv_bwd(dact, f["xbc"], f["cw"], row(p["conv_b"]))
    g["conv_w"], g["conv_b"] = dcw8[:CONV_K], s8(dcb8)
    dqp, dkb, dv = _flash_bwd(f["q"], f["k"], f["v"], datt, f["lse"], delta, *f["rope"])
    dcq, dckv, dkr, dgq, dgkv = _qkv_bwd(dqp, dkb, dv, f["cq"], f["ckv"], *f["rope"], row(p["q_a_norm"]),
                                         row(p["kv_a_norm"]), f["wq"], f["wk"], f["wv"])
    g["q_a_norm"], g["kv_a_norm"] = s8(dgq), s8(dgkv)
    dwq = _matmul_tn("dw_q_up", f["cqn"], dqp).reshape(Q_LORA, ATT_HEADS, HEADW)
    g["w_q_up"] = dwq[:, :, :QK_NOPE + QK_ROPE].reshape(Q_LORA, -1)
    dwk = _matmul_tn("dw_k_up", f["ckvn"], dkb).reshape(KV_LORA, ATT_HEADS, HEADW)[:, :, :QK_NOPE]
    dwv = _matmul_tn("dw_v_up", f["ckvn"], dv).reshape(KV_LORA, ATT_HEADS, V_HEAD)
    g["w_kv_up"] = jnp.concatenate([dwk, dwv], axis=2).reshape(KV_LORA, -1)
    dx, dhead, dgin, dproj = _in_bwd([dcq, dckv, dkr, dz, dxbc, ddtr], head, x, dh1, row(p["norm_mix_pre"]), f["w_all"])
    g["norm_mix_pre"] = s8(dgin)
    g["meta_tokens"] = dhead[PADF:]
    dwa = _matmul_tn("dw_in", f["hn"], dproj)
    g["w_in"] = jnp.concatenate([dwa[:, PC_Q:PC_KR], dwa[:, PC_KR + QK_NOPE:PC_KR + QK_NOPE + QK_ROPE],
                                 dwa[:, PC_Z:PC_DT + SSM_HEADS]], axis=1)
    return dx, g, received


BIG = {"w_in": ((D_MODEL, IN_WIDTH), 1), "w_q_up": ((Q_LORA, ATT_HEADS * (QK_NOPE + QK_ROPE)), 1),
       "w_kv_up": ((KV_LORA, ATT_HEADS * (QK_NOPE + V_HEAD)), 1), "w_out": ((2 * D_MODEL, D_MODEL), 0),
       "w_mlp_up": ((D_MODEL, D_FF), 1), "w_mlp_down": ((D_FF, D_MODEL), 0), "conv_w": ((CONV_K, CONV_DIM), 1),
       "meta_tokens": ((N_META, D_MODEL), 1)}
SMALL = {"norm_mix_pre": D_MODEL, "q_a_norm": Q_LORA, "kv_a_norm": KV_LORA, "conv_b": CONV_DIM, "dt_bias": SSM_HEADS,
         "a_log": SSM_HEADS, "d_skip": SSM_HEADS, "ssm_norm": SSM_WIDTH, "norm_mix_post": D_MODEL,
         "norm_mlp_pre": D_MODEL, "norm_mlp_post": D_MODEL}
WEIGHT_ORDER = ("meta_tokens", "norm_mix_pre", "w_in", "q_a_norm", "w_q_up", "kv_a_norm", "w_kv_up", "conv_w", "conv_b",
                "dt_bias", "a_log", "d_skip", "ssm_norm", "w_out", "norm_mix_post", "norm_mlp_pre", "w_mlp_up",
                "w_mlp_down", "norm_mlp_post")
ADAM_ROWS = 256


def _shard_shape(name):
    shape, ax = BIG[name]
    return tuple(d // N_DEV if a == ax else d for a, d in enumerate(shape))


SMALL_ROWS = -(-sum(SMALL.values()) // (LANE * 8)) * 8


def _pack(flats, rows):
    v = jnp.concatenate([f.reshape(-1) for f in flats])
    return jnp.pad(v, (0, rows * LANE - v.shape[0])).reshape(rows, LANE)


def _unpack(packed, shapes):
    v = packed.reshape(-1)
    out, o = [], 0
    for s in shapes:
        n = math.prod(s)
        out.append(v[o:o + n].reshape(s))
        o += n
    return out


def _to_chunks(name, full):
    shape, ax = BIG[name]
    if ax == 0:
        return full.reshape((N_DEV,) + _shard_shape(name))
    k, n = shape
    return full.reshape(k, N_DEV, n // N_DEV).transpose(1, 0, 2)


def _from_shards(name, shards):
    shape, ax = BIG[name]
    if ax == 0:
        return shards.reshape(shape)
    return shards.transpose(1, 0, 2).reshape(shape)


def _peer(k):
    x, y, c = lax.axis_index("x"), lax.axis_index("y"), lax.axis_index("c")
    px = 1 - x if k & 4 else x
    py = 1 - y if k & 2 else y
    pc = 1 - c if k & 1 else c
    return (px, py, pc), 4 * px + 2 * py + pc


def _gather_ops(x_refs, out_refs, send_sems, recv_sems, local_sems):
    na = len(x_refs)
    chips = (4, 2, 6)

    def copy(a, n, block, to, src=None):
        return pltpu.make_async_remote_copy(
            src_ref=out_refs[a].at[block] if src is None else src, dst_ref=out_refs[a].at[block],
            send_sem=send_sems.at[7 * a + n], recv_sem=recv_sems.at[7 * a + n], device_id=to, device_id_type=MESH)

    def mine():
        me = _peer(0)[1]
        return [pltpu.make_async_copy(x_refs[a], out_refs[a].at[me], local_sems.at[a]) for a in range(na)]

    def first():
        me, sibling = _peer(0)[1], _peer(1)[0]
        out = [copy(a, 0, me, sibling, src=x_refs[a]) for a in range(na)]
        return out + [copy(a, 1 + n, me, _peer(k)[0], src=x_refs[a]) for n, k in enumerate(chips) for a in range(na)]

    def passed():
        sibling = _peer(1)[0]
        return [copy(a, 4 + n, _peer(k)[1], sibling) for n, k in enumerate(chips) for a in range(na)]

    def start():
        for cp in mine() + first():
            cp.start()

    def forward():
        sibling = _peer(1)[0]
        fwd = passed()
        for n, k in enumerate(chips):
            for a in range(na):
                copy(a, 1 + n, _peer(k)[1], sibling).wait_recv()
                fwd[n * na + a].start()

    def finish():
        sibling = _peer(1)[0]
        for a in range(na):
            copy(a, 0, _peer(1)[1], sibling).wait_recv()
        for n, k in enumerate(chips):
            for a in range(na):
                copy(a, 4 + n, _peer(k | 1)[1], sibling).wait_recv()
        for cp in first() + passed():
            cp.wait_send()
        for cp in mine():
            cp.wait()

    return start, forward, finish


def _gather_scratch(na):
    return [pltpu.SemaphoreType.DMA((7 * na,)), pltpu.SemaphoreType.DMA((7 * na,)), pltpu.SemaphoreType.DMA((na,))]


def _all_gather(shards):
    na = len(shards)

    def body(*refs):
        for step in _gather_ops(refs[:na], refs[na:2 * na], *refs[2 * na:]):
            step()

    return pl.pallas_call(
        body, name="all_gather_weights", out_shape=[_sds((N_DEV,) + s.shape, s.dtype) for s in shards],
        in_specs=[ANY] * na, out_specs=[ANY] * na, scratch_shapes=_gather_scratch(na),
    )(*shards)


def _exchange(chunks, small):
    na = len(chunks) + 1

    def body(*refs):
        for step in _exchange_ops(refs[:na], refs[na:2 * na], *refs[2 * na:], whole=(na - 1,)):
            step()

    arrays = list(chunks) + [small]
    return pl.pallas_call(
        body, name="exchange_grads",
        out_shape=[_sds(c.shape, c.dtype) for c in chunks] + [_sds((N_DEV,) + small.shape, small.dtype)],
        in_specs=[ANY] * na, out_specs=[ANY] * na, scratch_shapes=_gather_scratch(na),
    )(*arrays)


def _exchange_ops(in_refs, out_refs, send_sems, recv_sems, local_sems, whole=()):
    na = len(in_refs)

    def src(a, idx):
        return in_refs[a] if a in whole else in_refs[a].at[idx]

    def own():
        me = _peer(0)[1]
        return [pltpu.make_async_copy(src(a, me), out_refs[a].at[me], local_sems.at[a]) for a in range(na)]

    def copy(a, k, sending):
        me = _peer(0)[1]
        to, idx = _peer(k)
        return pltpu.make_async_remote_copy(
            src_ref=src(a, idx if sending else me), dst_ref=out_refs[a].at[me if sending else idx],
            send_sem=send_sems.at[7 * a + k - 1], recv_sem=recv_sems.at[7 * a + k - 1],
            device_id=to, device_id_type=MESH)

    def sent():
        return [copy(a, k, True) for k in range(1, N_DEV) for a in range(na)]

    def start():
        for cp in own() + sent():
            cp.start()

    def finish():
        for k in range(1, N_DEV):
            for a in range(na):
                copy(a, k, False).wait_recv()
        for cp in sent():
            cp.wait_send()
        for cp in own():
            cp.wait()

    return start, finish


def _reduce_adamw(name, recv, w, m, v):
    rows, cols = w.shape
    blk = ADAM_ROWS if rows % ADAM_ROWS == 0 else rows
    c1 = 1.0 - ADAM_B1 ** ADAM_STEP
    c2 = 1.0 - ADAM_B2 ** ADAM_STEP

    def body(r_ref, w_ref, m_ref, v_ref, g_ref, d_ref, nm_ref, nv_ref):
        g = r_ref[0].astype(F32)
        for s in range(1, N_DEV):
            g = g + r_ref[s].astype(F32)
        g_ref[...] = g
        m_ = ADAM_B1 * m_ref[...] + (1.0 - ADAM_B1) * g
        v_ = ADAM_B2 * v_ref[...] + (1.0 - ADAM_B2) * (g * g)
        nm_ref[...] = m_
        nv_ref[...] = v_
        d_ref[...] = -ADAM_LR * ((m_ / c1) / (jnp.sqrt(v_ / c2) + ADAM_EPS) + ADAM_WD * w_ref[...])

    spec = _rowspec(blk, cols)
    return pl.pallas_call(
        body, name="reduce_adamw_" + name, grid=(rows // blk,),
        in_specs=[pl.BlockSpec((N_DEV, blk, cols), lambda i: (0, i, 0)), spec, spec, spec],
        out_specs=[spec] * 4, out_shape=[_sds((rows, cols), F32)] * 4,
        compiler_params=_cp(("arbitrary",)),
    )(recv, w, m, v)


def kernel(x, meta_tokens, norm_mix_pre, w_in, q_a_norm, w_q_up, kv_a_norm, w_kv_up, conv_w, conv_b, dt_bias, a_log, d_skip, ssm_norm, w_out, norm_mix_post, norm_mlp_pre, w_mlp_up, w_mlp_down, norm_mlp_post, loss_target, m_meta_tokens, m_norm_mix_pre, m_w_in, m_q_a_norm, m_w_q_up, m_kv_a_norm, m_w_kv_up, m_conv_w, m_conv_b, m_dt_bias, m_a_log, m_d_skip, m_ssm_norm, m_w_out, m_norm_mix_post, m_norm_mlp_pre, m_w_mlp_up, m_w_mlp_down, m_norm_mlp_post, v_meta_tokens, v_norm_mix_pre, v_w_in, v_q_a_norm, v_w_q_up, v_kv_a_norm, v_w_kv_up, v_conv_w, v_conv_b, v_dt_bias, v_a_log, v_d_skip, v_ssm_norm, v_w_out, v_norm_mix_post, v_norm_mlp_pre, v_w_mlp_up, v_w_mlp_down, v_norm_mlp_post):
    w = dict(meta_tokens=meta_tokens, norm_mix_pre=norm_mix_pre, w_in=w_in, q_a_norm=q_a_norm, w_q_up=w_q_up,
             kv_a_norm=kv_a_norm, w_kv_up=w_kv_up, conv_w=conv_w, conv_b=conv_b, dt_bias=dt_bias, a_log=a_log,
             d_skip=d_skip, ssm_norm=ssm_norm, w_out=w_out, norm_mix_post=norm_mix_post, norm_mlp_pre=norm_mlp_pre,
             w_mlp_up=w_mlp_up, w_mlp_down=w_mlp_down, norm_mlp_post=norm_mlp_post)
    m = dict(meta_tokens=m_meta_tokens, norm_mix_pre=m_norm_mix_pre, w_in=m_w_in, q_a_norm=m_q_a_norm, w_q_up=m_w_q_up,
             kv_a_norm=m_kv_a_norm, w_kv_up=m_w_kv_up, conv_w=m_conv_w, conv_b=m_conv_b, dt_bias=m_dt_bias,
             a_log=m_a_log, d_skip=m_d_skip, ssm_norm=m_ssm_norm, w_out=m_w_out, norm_mix_post=m_norm_mix_post,
             norm_mlp_pre=m_norm_mlp_pre, w_mlp_up=m_w_mlp_up, w_mlp_down=m_w_mlp_down, norm_mlp_post=m_norm_mlp_post)
    v = dict(meta_tokens=v_meta_tokens, norm_mix_pre=v_norm_mix_pre, w_in=v_w_in, q_a_norm=v_q_a_norm, w_q_up=v_w_q_up,
             kv_a_norm=v_kv_a_norm, w_kv_up=v_w_kv_up, conv_w=v_conv_w, conv_b=v_conv_b, dt_bias=v_dt_bias,
             a_log=v_a_log, d_skip=v_d_skip, ssm_norm=v_ssm_norm, w_out=v_w_out, norm_mix_post=v_norm_mix_post,
             norm_mlp_pre=v_norm_mlp_pre, w_mlp_up=v_w_mlp_up, w_mlp_down=v_w_mlp_down, norm_mlp_post=v_norm_mlp_post)
    big_names = [n for n in WEIGHT_ORDER if n in BIG]
    small_names = [n for n in WEIGHT_ORDER if n in SMALL]
    shard = lambda d, n: d[n].reshape(_shard_shape(n))

    f32_names = ("conv_w", "meta_tokens")
    early = [n for n in big_names if n not in LATE]
    gathered = _all_gather([shard(w, n).astype(F32 if n in f32_names else BF16) for n in early])
    p = {n: w[n].reshape(-1) for n in small_names}
    p.update({n: _from_shards(n, s) for n, s in zip(early, gathered)})
    head = jnp.concatenate([jnp.zeros((PADF, D_MODEL), F32), p["meta_tokens"]], axis=0)
    f = _local_forward(head, x[0], loss_target[0], p, late={n: shard(w, n).astype(BF16) for n in LATE})
    dx, g, recv_late = _local_backward(head, x[0], f, exchange_late=True)
    grad_x = dx[None]
    loss = lax.psum(f["loss"], ("x", "y", "c"))

    small = _pack([g[n] for n in small_names], SMALL_ROWS)
    *recv_early, recv_small = _exchange([_to_chunks(n, g[n]).astype(BF16) for n in early], small)
    recv_of = {**dict(zip(early, recv_early)), **dict(zip(LATE, recv_late))}

    outs = {}
    kinds = ("grad", "delta", "new_m", "new_v")
    for n, recv in ((n, recv_of[n]) for n in big_names):
        for kind, arr in zip(kinds, _reduce_adamw(n, recv, shard(w, n), shard(m, n), shard(v, n))):
            outs[kind, n] = arr.reshape(w[n].shape)
    packed = [_pack([d[n] for n in small_names], SMALL_ROWS) for d in (w, m, v)]
    for kind, arr in zip(kinds, _reduce_adamw("small", recv_small, *packed)):
        for n, val in zip(small_names, _unpack(arr, [(SMALL[n],) for n in small_names])):
            outs[kind, n] = val.reshape(w[n].shape)
    return (loss, grad_x) + tuple(outs[kind, n] for kind in ("grad", "delta", "new_m", "new_v") for n in WEIGHT_ORDER)
```

```python
import math

import jax
import jax.numpy as jnp
import numpy as np
from jax import lax
from jax.experimental import pallas as pl
from jax.experimental.pallas import tpu as pltpu

F32 = jnp.float32
BF16 = jnp.bfloat16

D_MODEL = 1024
N_META = 16
EPS = 1e-6
ATT_HEADS = 8
Q_LORA = 384
KV_LORA = 256
QK_NOPE = 128
QK_ROPE = 64
V_HEAD = 128
ROPE_THETA = 10000.0
SSM_HEADS = 16
SSM_HEAD_DIM = 64
SSM_WIDTH = 1024
SSM_GROUPS = 2
SSM_STATE = 128
CONV_K = 4
CHUNK = 128
CONV_DIM = 1536
D_FF = 4096
IN_SPLITS = (Q_LORA, KV_LORA, QK_ROPE, SSM_WIDTH, CONV_DIM, SSM_HEADS)
IN_WIDTH = sum(IN_SPLITS)
ADAM_LR, ADAM_B1, ADAM_B2, ADAM_EPS, ADAM_WD, ADAM_STEP = 0.001, 0.9, 0.999, 1e-08, 0.01, 10

LANE = 128
ROWB = 512
PADF = ROWB - N_META
HEADW = 256
PC_Q, PC_KV, PC_KR, PC_Z, PC_XBC, PC_DT, PROJ_W = 0, 384, 640, 896, 1920, 3456, 3584
NEG = -1e30
N_DEV = 8
VMEM_LIMIT = 56 * 1024 * 1024
MESH = pl.DeviceIdType.MESH


def _cp(sem, vmem=VMEM_LIMIT, **kw):
    return pltpu.CompilerParams(dimension_semantics=sem, vmem_limit_bytes=vmem, **kw)


def _dot(a, b, dims=((1,), (0,))):
    return lax.dot_general(a, b, (dims, ((), ())), preferred_element_type=F32)


def _bdot(a, b, dims=((1,), (0,))):
    return _dot(a.astype(BF16), b.astype(BF16), dims)


NT = ((1,), (1,))


def _rms_fwd(x, w):
    r = lax.rsqrt(jnp.mean(x * x, axis=-1, keepdims=True) + EPS)
    return (x * r) * w


def _rms_bwd(x, w, dy):
    r = lax.rsqrt(jnp.mean(x * x, axis=-1, keepdims=True) + EPS)
    xh = x * r
    g = dy * w
    dx = r * (g - xh * jnp.mean(g * xh, axis=-1, keepdims=True))
    return dx, dy * xh


def _sigmoid(x):
    return 0.5 * jnp.tanh(0.5 * x) + 0.5


def _colsum8(x):
    t, c = x.shape
    return jnp.sum(x.reshape(t // 8, 8, c), axis=0)


def _rowspec(t, c, cb=0):
    return pl.BlockSpec((t, c), lambda i: (i, cb))


def _fullspec(shape):
    n = len(shape)
    return pl.BlockSpec(shape, lambda i: (0,) * n)


def _sds(shape, dt):
    return jax.ShapeDtypeStruct(shape, dt)


def _acc(ref, val):
    @pl.when(pl.program_id(0) == 0)
    def _():
        ref[...] = jnp.zeros_like(ref)

    ref[...] += val


def _xspec():
    return pl.BlockSpec((ROWB, D_MODEL), lambda i: (jnp.maximum(i - 1, 0), 0))


def _h_block(head_ref, x_ref):
    return jnp.where(pl.program_id(0) == 0, head_ref[...], x_ref[...])


def _norm_in_proj(head, x, g, w_all):
    lp = head.shape[0] + x.shape[0]

    def body(head_ref, x_ref, g_ref, w_ref, hn_ref, cq_ref, ckv_ref, kr_ref, z_ref, xbc_ref, dt_ref):
        hn = _rms_fwd(_h_block(head_ref, x_ref), g_ref[...]).astype(BF16)
        hn_ref[...] = hn
        p = _dot(hn, w_ref[...])
        cq_ref[...] = p[:, PC_Q:PC_KV]
        ckv_ref[...] = p[:, PC_KV:PC_KR]
        kr_ref[...] = p[:, PC_KR:PC_Z]
        z_ref[...] = p[:, PC_Z:PC_XBC]
        xbc_ref[...] = p[:, PC_XBC:PC_DT]
        dt_ref[...] = p[:, PC_DT:PROJ_W]

    widths = (Q_LORA, KV_LORA, HEADW, SSM_WIDTH, CONV_DIM, LANE)
    return pl.pallas_call(
        body, name="norm_in_proj", grid=(lp // ROWB,),
        in_specs=[_fullspec((ROWB, D_MODEL)), _xspec(), _fullspec((1, D_MODEL)), _fullspec((D_MODEL, PROJ_W))],
        out_specs=[_rowspec(ROWB, D_MODEL)] + [_rowspec(ROWB, w) for w in widths],
        out_shape=[_sds((lp, D_MODEL), BF16)] + [_sds((lp, w), F32) for w in widths],
        compiler_params=_cp(("arbitrary",)),
    )(head, x, g, w_all)


def _rope(x, cos, sa, sb):
    w = x.shape[1]
    return x * cos + pltpu.roll(x, w - 32, 1) * sa + pltpu.roll(x, 32, 1) * sb


def _rope_t(dy, cos, sa, sb):
    w = dy.shape[1]
    return dy * cos + pltpu.roll(dy * sa, 32, 1) + pltpu.roll(dy * sb, w - 32, 1)


def _tile8(t):
    return jnp.concatenate([t] * ATT_HEADS, axis=1)


def _qkv(cq, ckv, kr, cos, sa, sb, gq, gkv, wq, wk, wv):
    lp = cq.shape[0]
    qw = ATT_HEADS * HEADW

    def body(cq_ref, ckv_ref, kr_ref, cos_ref, sa_ref, sb_ref, gq_ref, gkv_ref, wq_ref, wk_ref, wv_ref,
             q_ref, k_ref, v_ref, cqn_ref, ckvn_ref):
        cos_, sa_, sb_ = cos_ref[...], sa_ref[...], sb_ref[...]
        cqn = _rms_fwd(cq_ref[...], gq_ref[...]).astype(BF16)
        ckvn = _rms_fwd(ckv_ref[...], gkv_ref[...]).astype(BF16)
        cqn_ref[...] = cqn
        ckvn_ref[...] = ckvn
        q = _dot(cqn, wq_ref[...])
        q_ref[...] = (_rope(q, _tile8(cos_), _tile8(sa_), _tile8(sb_)) * Q_PRESCALE).astype(BF16)
        k = _dot(ckvn, wk_ref[...]) + _tile8(_rope(kr_ref[...], cos_, sa_, sb_))
        k_ref[...] = k.astype(BF16)
        lanes = lax.broadcasted_iota(jnp.int32, (1, qw), 1)
        ones = ((lanes % HEADW) >= V_HEAD).astype(F32)
        v_ref[...] = (_dot(ckvn, wv_ref[...]) + ones).astype(BF16)

    return pl.pallas_call(
        body, name="qkv", grid=(lp // ROWB,),
        in_specs=[_rowspec(ROWB, Q_LORA), _rowspec(ROWB, KV_LORA), _rowspec(ROWB, HEADW)]
        + [_rowspec(ROWB, HEADW)] * 3
        + [_fullspec((1, Q_LORA)), _fullspec((1, KV_LORA)), _fullspec((Q_LORA, qw)), _fullspec((KV_LORA, qw)),
           _fullspec((KV_LORA, qw))],
        out_specs=[_rowspec(ROWB, qw), _rowspec(ROWB, qw), _rowspec(ROWB, qw),
                   _rowspec(ROWB, Q_LORA), _rowspec(ROWB, KV_LORA)],
        out_shape=[_sds((lp, qw), BF16), _sds((lp, qw), BF16), _sds((lp, qw), BF16),
                   _sds((lp, Q_LORA), BF16), _sds((lp, KV_LORA), BF16)],
        compiler_params=_cp(("arbitrary",)),
    )(cq, ckv, kr, cos, sa, sb, gq, gkv, wq, wk, wv)


ATT_SCALE = (QK_NOPE + QK_ROPE) ** -0.5
LOG2E = 1.4426950408889634
LN2 = 0.6931471805599453
Q_PRESCALE = ATT_SCALE * LOG2E
KVB = 512
META_KEYS = LANE
assert N_META <= META_KEYS


def _att_ok(qrow, krow):
    return (krow <= qrow) & ((krow >= PADF) | (qrow < PADF))


def _lanes(x, n):
    return x if n == 1 else jnp.concatenate([x] * n, axis=1)


def _pair_loop(lo, hi, tile, unrolls=(2,)):
    for u in tuple(unrolls) + (1,):
        n = jnp.maximum(hi - lo, 0)
        trips = n // u

        def many(t, c, u=u, lo=lo):
            for d in range(u):
                tile(lo + u * t + d)
            return c

        lax.fori_loop(0, trips, many, 0)
        lo = lo + trips * u


def _flash_fwd(q, k, v):
    lp = q.shape[0]
    nq = lp // ROWB

    def body(q_ref, k_ref, v_ref, o_ref, lse_ref, acc, m_s):
        i = pl.program_id(1)
        qb = q_ref[...]
        m_s[...] = jnp.full_like(m_s, NEG)
        acc[...] = jnp.zeros_like(acc)

        def tile(j, masked, off=None, nkeys=KVB):
            off = pl.multiple_of(j * KVB, KVB) if off is None else off
            kb = k_ref[pl.ds(off, nkeys), :]
            vb = v_ref[pl.ds(off, nkeys), :]
            s = _dot(qb, kb, NT)
            if masked:
                qrow = i * ROWB + lax.broadcasted_iota(jnp.int32, s.shape, 0)
                krow = off + lax.broadcasted_iota(jnp.int32, s.shape, 1)
                s = jnp.where(_att_ok(qrow, krow), s, NEG)
            m_prev = m_s[...]
            m_new = jnp.maximum(m_prev, jnp.max(s, axis=1, keepdims=True))
            alpha = jnp.exp2(m_prev - m_new)
            p = jnp.exp2(s - _lanes(m_new, nkeys // LANE))
            acc[...] = _lanes(alpha, 2) * acc[...] + _dot(p.astype(BF16), vb)
            m_s[...] = m_new

        def first_tile():
            tile(0, True, off=ROWB - META_KEYS, nkeys=META_KEYS)

        @pl.when(i == 0)
        def _():
            first_tile()

        @pl.when(i > 0)
        def _():
            first_tile()
            tile(i, True)

        _pair_loop(1, i, lambda j: tile(j, False), (16, 8, 4, 2))
        l = acc[:, V_HEAD:]
        o_ref[...] = (acc[:, :V_HEAD] / l).astype(BF16)
        lse_ref[0] = (m_s[...] + jnp.log2(l)).T[0:1, :]

    return pl.pallas_call(
        body, name="flash_fwd", grid=(ATT_HEADS, nq),
        in_specs=[pl.BlockSpec((ROWB, HEADW), lambda h, i: (i, h)),
                  pl.BlockSpec((lp, HEADW), lambda h, i: (0, h)),
                  pl.BlockSpec((lp, HEADW), lambda h, i: (0, h))],
        out_specs=[pl.BlockSpec((ROWB, V_HEAD), lambda h, i: (i, h)),
                   pl.BlockSpec((1, 1, ROWB), lambda h, i: (h, 0, i))],
        out_shape=[_sds((lp, ATT_HEADS * V_HEAD), BF16), _sds((ATT_HEADS, 1, lp), F32)],
        scratch_shapes=[pltpu.VMEM((ROWB, HEADW), F32), pltpu.VMEM((ROWB, LANE), F32)],
        compiler_params=_cp(("arbitrary", "arbitrary")),
    )(q, k, v)


def _silu(x):
    return x * _sigmoid(x)


CONV_ROWS = 64


def _conv_fwd(xbc, cw, cb):
    lp, c = xbc.shape
    t8 = ROWB // 8

    def body(x_ref, prev_ref, w_ref, b_ref, o_ref, buf):
        i = pl.program_id(0)
        buf[pl.ds(0, 8), :] = jnp.where(i > 0, prev_ref[...], 0.0)
        buf[pl.ds(8, ROWB), :] = x_ref[...]

        def strip(s, carry):
            cs = pl.ds(pl.multiple_of(s * LANE, LANE), LANE)
            w, b = w_ref[:, cs], b_ref[:, cs]
            for r0 in range(0, ROWB, CONV_ROWS):
                pre = b + sum(w[kk:kk + 1, :] * buf[pl.ds(8 - (CONV_K - 1) + kk + r0, CONV_ROWS), cs]
                              for kk in range(CONV_K))
                o_ref[pl.ds(r0, CONV_ROWS), cs] = _silu(pre)
            return carry

        lax.fori_loop(0, c // LANE, strip, 0)

    return pl.pallas_call(
        body, name="conv_fwd", grid=(lp // ROWB,),
        in_specs=[_rowspec(ROWB, c), pl.BlockSpec((8, c), lambda i: (jnp.maximum(i * t8 - 1, 0), 0)),
                  _fullspec((8, c)), _fullspec((1, c))],
        out_specs=_rowspec(ROWB, c), out_shape=_sds((lp, c), F32),
        scratch_shapes=[pltpu.VMEM((ROWB + 8, c), F32)],
        compiler_params=_cp(("arbitrary",)),
    )(xbc, xbc, cw, cb)


def _expand_mat():
    r = np.arange(LANE)[:, None]
    c = np.arange(SSM_WIDTH)[None, :]
    return jnp.asarray((c // SSM_HEAD_DIM == r).astype(np.float32))


def _tri_mat():
    i = np.arange(CHUNK)
    return jnp.asarray((i[:, None] >= i[None, :]).astype(np.float32))


def _x3(m, axis):
    return jnp.concatenate([m.astype(BF16)] * 3, axis=axis)


def _split3(x):
    hi = x.astype(BF16)
    r = x - hi.astype(F32)
    mid = r.astype(BF16)
    return hi, mid, (r - mid.astype(F32)).astype(BF16)


def _dot01_r(x, m3):
    return _dot(jnp.concatenate(_split3(x), axis=1), m3)


def _dot01_l(m3, x):
    return _dot(m3, jnp.concatenate(_split3(x), axis=0))


def _ssd_prep(dtr_ref, bias_ref, alog_ref, tri3, c, seq_rows):
    rows = c * CHUNK + lax.broadcasted_iota(jnp.int32, (CHUNK, LANE), 0)
    lanes = lax.broadcasted_iota(jnp.int32, (CHUNK, LANE), 1)
    valid = (rows >= PADF) & (rows < PADF + seq_rows) & (lanes < SSM_HEADS)
    dtr = dtr_ref[...] + bias_ref[...]
    sp = jnp.maximum(dtr, 0.0) + jnp.log(1.0 + jnp.exp(-jnp.abs(dtr)))
    dt = jnp.where(valid, sp, 0.0)
    a = -jnp.exp(alog_ref[...])
    acol = _dot01_l(tri3, dt * a)
    return dt, a, acol, valid, dtr


def _row16(v):
    return jnp.broadcast_to(v, (16, v.shape[1]))


def _ssd_fwd(xbc_act, dtr, dt_bias, a_log, seq_rows, gather=()):
    lp = xbc_act.shape[0]
    nc = lp // CHUNK
    gw = SSM_WIDTH // SSM_GROUPS
    hpg = SSM_HEADS // SSM_GROUPS

    na = len(gather)

    def body(x_ref, b_ref, c_ref, dtr_ref, bias_ref, alog_ref, tri_ref, tri3_ref, ex3_ref, *rest):
        gin, (y_ref, hp_ref), gout, h_s, sems = rest[:na], rest[na:na + 2], rest[na + 2:2 * na + 2], rest[2 * na + 2], rest[2 * na + 3:]
        c = pl.program_id(0)

        @pl.when(c == 0)
        def _():
            h_s[...] = jnp.zeros_like(h_s)

        if na:
            g_start, g_forward, g_finish = _gather_ops(gin, gout, *sems)
            pl.when(c == 0)(g_start)
            pl.when(c == nc // 2)(g_forward)

        ex3 = ex3_ref[...]
        dt, a, acol, _, _ = _ssd_prep(dtr_ref, bias_ref, alog_ref, tri3_ref[...], c, seq_rows)
        arow = acol.T
        dtrow = dt.T
        alast = acol[CHUNK - 1:CHUNK, :]
        e_all = _dot01_r(jnp.exp(acol), ex3)
        wx_all = _dot01_r(jnp.exp(alast - acol) * dt, ex3)
        dec_all = _dot01_r(_row16(jnp.exp(alast)), ex3)[0:1, :]
        causal = tri_ref[...] > 0.5
        hp_ref[0] = h_s[...]
        for g in range(SSM_GROUPS):
            gs = slice(g * gw, (g + 1) * gw)
            bg = b_ref[:, g * SSM_STATE:(g + 1) * SSM_STATE]
            cg = c_ref[:, g * SSM_STATE:(g + 1) * SSM_STATE].astype(BF16)
            xg = x_ref[:, gs]
            hg = h_s[:, gs]
            gm = _bdot(cg, bg, NT)
            y_off = _bdot(cg, hg) * e_all[:, gs]
            for r in range(hpg):
                hd = g * hpg + r
                seg = acol[:, hd:hd + 1] - arow[hd:hd + 1, :]
                lm = jnp.where(causal, jnp.exp(jnp.where(causal, seg, 0.0)), 0.0)
                w = gm * lm * dtrow[hd:hd + 1, :]
                cs = slice(r * SSM_HEAD_DIM, (r + 1) * SSM_HEAD_DIM)
                y_ref[:, pl.ds(hd * SSM_HEAD_DIM, SSM_HEAD_DIM)] = _bdot(w, xg[:, cs]) + y_off[:, cs]
            st = _bdot(bg.T, xg * wx_all[:, gs])
            h_s[:, gs] = hg * dec_all[:, gs] + st

        if na:
            pl.when(c == nc - 1)(g_finish)

    xs_spec = pl.BlockSpec((CHUNK, SSM_WIDTH), lambda c: (c, 0))
    b_spec = pl.BlockSpec((CHUNK, 2 * SSM_STATE), lambda c: (c, SSM_WIDTH // (2 * SSM_STATE)))
    c_spec = pl.BlockSpec((CHUNK, 2 * SSM_STATE), lambda c: (c, SSM_WIDTH // (2 * SSM_STATE) + 1))
    y, hprev, *gathered = pl.pallas_call(
        body, name="ssd_fwd", grid=(nc,),
        in_specs=[xs_spec, b_spec, c_spec, pl.BlockSpec((CHUNK, LANE), lambda c: (c, 0)),
                  _fullspec((1, LANE)), _fullspec((1, LANE)), _fullspec((CHUNK, CHUNK)), _fullspec((CHUNK, 3 * CHUNK)),
                  _fullspec((3 * LANE, SSM_WIDTH))] + [ANY] * na,
        out_specs=[xs_spec, pl.BlockSpec((1, SSM_STATE, SSM_WIDTH), lambda c: (c, 0, 0))] + [ANY] * na,
        out_shape=[_sds((lp, SSM_WIDTH), F32), _sds((nc, SSM_STATE, SSM_WIDTH), F32)]
        + [_sds((N_DEV,) + s.shape, s.dtype) for s in gather],
        scratch_shapes=[pltpu.VMEM((SSM_STATE, SSM_WIDTH), F32)] + (_gather_scratch(na) if na else []),
        compiler_params=_cp(("arbitrary",)),
    )(xbc_act, xbc_act, xbc_act, dtr, dt_bias, a_log, _tri_mat(), _x3(_tri_mat(), 1), _x3(_expand_mat(), 0), *gather)
    return y, hprev, gathered


def _group_mean(x):
    gw = SSM_WIDTH // SSM_GROUPS
    parts = [jnp.broadcast_to(jnp.mean(x[:, g * gw:(g + 1) * gw], axis=-1, keepdims=True), (x.shape[0], gw))
             for g in range(SSM_GROUPS)]
    return jnp.concatenate(parts, axis=1)


def _ssd_post(y, xbc_act, z, dskip, gnorm):
    lp = y.shape[0]

    def body(y_ref, x_ref, z_ref, d_ref, g_ref, o_ref):
        z_ = z_ref[...]
        gt = (y_ref[...] + d_ref[...] * x_ref[...]) * _silu(z_)
        r = lax.rsqrt(_group_mean(gt * gt) + EPS)
        o_ref[...] = ((gt * r) * g_ref[...]).astype(BF16)

    return pl.pallas_call(
        body, name="ssd_post", grid=(lp // ROWB,),
        in_specs=[_rowspec(ROWB, SSM_WIDTH)] * 3 + [_fullspec((1, SSM_WIDTH))] * 2,
        out_specs=_rowspec(ROWB, SSM_WIDTH), out_shape=_sds((lp, SSM_WIDTH), BF16),
        compiler_params=_cp(("arbitrary",)),
    )(y, xbc_act, z, dskip, gnorm)


def _out_proj(att, ssm, head, x, w_out, g_post):
    lp = att.shape[0]

    def body(a_ref, s_ref, head_ref, x_ref, w_ref, g_ref, mix_ref, h1_ref):
        mix = _dot(a_ref[...], w_ref[pl.ds(0, 1024), :]) + _dot(s_ref[...], w_ref[pl.ds(1024, 1024), :])
        mix_ref[...] = mix
        h1_ref[...] = _h_block(head_ref, x_ref) + _rms_fwd(mix, g_ref[...])

    return pl.pallas_call(
        body, name="out_proj", grid=(lp // ROWB,),
        in_specs=[_rowspec(ROWB, 1024)] * 2 + [_fullspec((ROWB, D_MODEL)), _xspec(), _fullspec((2048, D_MODEL)),
                                               _fullspec((1, D_MODEL))],
        out_specs=[_rowspec(ROWB, D_MODEL)] * 2, out_shape=[_sds((lp, D_MODEL), F32)] * 2,
        compiler_params=_cp(("arbitrary",)),
    )(att, ssm, head, x, w_out, g_post)


def _resident(w_hbm, w_vmem, sem):
    @pl.when(pl.program_id(0) == 0)
    def _():
        cp = pltpu.make_async_copy(w_hbm, w_vmem, sem)
        cp.start()
        cp.wait()


ANY = pl.BlockSpec(memory_space=pl.ANY)


def _mlp_fwd(h1, tgt, w_up, w_down, g_pre, g_post, seq_rows):
    lp = h1.shape[0]

    def body(h1_ref, t_ref, wu_hbm, wd_hbm, gpre_ref, gpost_ref, hn2_ref, f_ref, dh2_ref, loss_ref, wu, wd, sems):
        _resident(wu_hbm, wu, sems.at[0])
        _resident(wd_hbm, wd, sems.at[1])
        i = pl.program_id(0)
        h1_ = h1_ref[...]
        hn2 = _rms_fwd(h1_, gpre_ref[...]).astype(BF16)
        hn2_ref[...] = hn2
        u = jnp.maximum(_dot(hn2, wu[...]), 0.0)
        f = _dot((u * u).astype(BF16), wd[...])
        f_ref[...] = f
        h2 = h1_ + _rms_fwd(f, gpost_ref[...])
        rows = i * ROWB + lax.broadcasted_iota(jnp.int32, (ROWB, 1), 0)
        real = (rows >= PADF + N_META) & (rows < PADF + seq_rows)
        err = jnp.where(real, h2 - t_ref[...], 0.0)
        dh2_ref[...] = err * (1.0 / D_MODEL)
        _acc(loss_ref, _colsum8(err * err))

    return pl.pallas_call(
        body, name="mlp_fwd", grid=(lp // ROWB,),
        in_specs=[_rowspec(ROWB, D_MODEL), _xspec()] + [ANY, ANY] + [_fullspec((1, D_MODEL))] * 2,
        out_specs=[_rowspec(ROWB, D_MODEL)] * 3 + [_fullspec((8, D_MODEL))],
        out_shape=[_sds((lp, D_MODEL), BF16), _sds((lp, D_MODEL), F32), _sds((lp, D_MODEL), F32), _sds((8, D_MODEL), F32)],
        scratch_shapes=[pltpu.VMEM((D_MODEL, D_FF), BF16), pltpu.VMEM((D_FF, D_MODEL), BF16), pltpu.SemaphoreType.DMA((2,))],
        compiler_params=_cp(("arbitrary",)),
    )(h1, tgt, w_up, w_down, g_pre, g_post)


def _pad_cols(w, width):
    return jnp.pad(w, ((0, 0), (0, width - w.shape[1])))


def _layout_weights(w_in, w_q_up, w_kv_up):
    o = np.cumsum((0,) + IN_SPLITS)
    pieces = [w_in[:, o[k]:o[k + 1]] for k in range(6)]
    kr = jnp.pad(pieces[2], ((0, 0), (QK_NOPE, HEADW - QK_NOPE - QK_ROPE)))
    w_all = jnp.concatenate([pieces[0], pieces[1], kr, pieces[3], pieces[4], _pad_cols(pieces[5], LANE)], axis=1)
    wq = jnp.pad(w_q_up.reshape(Q_LORA, ATT_HEADS, QK_NOPE + QK_ROPE), ((0, 0), (0, 0), (0, HEADW - QK_NOPE - QK_ROPE)))
    wkv = w_kv_up.reshape(KV_LORA, ATT_HEADS, QK_NOPE + V_HEAD)
    wk = jnp.pad(wkv[:, :, :QK_NOPE], ((0, 0), (0, 0), (0, HEADW - QK_NOPE)))
    wv = wkv[:, :, QK_NOPE:]
    wvp = jnp.pad(wv, ((0, 0), (0, 0), (0, HEADW - V_HEAD)))
    return (w_all, wq.reshape(Q_LORA, -1), wk.reshape(KV_LORA, -1), wv.reshape(KV_LORA, -1),
            wvp.reshape(KV_LORA, -1))


def _rope_tables(lp):
    pos = jnp.maximum(jnp.arange(lp, dtype=jnp.int32) - PADF, 0).astype(F32)
    inv_freq = ROPE_THETA ** (-jnp.arange(0, QK_ROPE, 2, dtype=F32) / QK_ROPE)
    ang = pos[:, None] * inv_freq[None, :]
    cos, sin = jnp.cos(ang), jnp.sin(ang)
    one, zero = jnp.ones((lp, QK_NOPE), F32), jnp.zeros((lp, QK_NOPE), F32)
    z32, z64 = jnp.zeros((lp, 32), F32), jnp.zeros((lp, 64), F32)
    cos_t = jnp.concatenate([one, cos, cos, jnp.ones((lp, 64), F32)], axis=1)
    sa = jnp.concatenate([zero, -sin, z32, z64], axis=1)
    sb = jnp.concatenate([zero, z32, sin, z64], axis=1)
    return cos_t, sa, sb


def _row1(v, width=None):
    v = v.reshape(1, -1).astype(F32)
    return v if width is None else _pad_cols(v, width)


LATE = ("w_out", "w_mlp_up", "w_mlp_down")


def _local_forward(head, x, tgt, p, late=None):
    assert head.shape[0] == ROWB and x.shape[0] % ROWB == 0
    lp = ROWB + x.shape[0]
    seq_rows = N_META + x.shape[0]
    f = {"seq_rows": seq_rows}
    w_all, wq, wk, wv, wvp = _layout_weights(p["w_in"], p["w_q_up"], p["w_kv_up"])
    f.update(w_all=w_all, wq=wq, wk=wk, wv=wv)
    f["hn"], cq, ckv, kr, f["z"], f["xbc"], f["dtr"] = _norm_in_proj(head, x, _row1(p["norm_mix_pre"]), w_all)
    f.update(cq=cq, ckv=ckv)
    f["rope"] = _rope_tables(lp)
    f["q"], f["k"], f["v"], f["cqn"], f["ckvn"] = _qkv(cq, ckv, kr, *f["rope"], _row1(p["q_a_norm"]),
                                                   _row1(p["kv_a_norm"]), wq, wk, wvp)
    f["att"], f["lse"] = _flash_fwd(f["q"], f["k"], f["v"])
    f["cw"] = jnp.pad(p["conv_w"].astype(F32), ((0, 8 - CONV_K), (0, 0)))
    f["xact"] = _conv_fwd(f["xbc"], f["cw"], _row1(p["conv_b"]))
    f["dt_bias"], f["a_log"] = _row1(p["dt_bias"], LANE), _row1(p["a_log"], LANE)
    f["y"], f["hprev"], gathered = _ssd_fwd(f["xact"], f["dtr"], f["dt_bias"], f["a_log"], seq_rows,
                                            gather=[late[n] for n in LATE] if late else ())
    p = {**p, **{n: _from_shards(n, s) for n, s in zip(LATE, gathered)}}
    f["p"] = p
    f["dskip"] = jnp.repeat(p["d_skip"].reshape(-1).astype(F32), SSM_HEAD_DIM).reshape(1, SSM_WIDTH)
    f["ssm"] = _ssd_post(f["y"], f["xact"], f["z"], f["dskip"], _row1(p["ssm_norm"]))
    f["mix"], f["h1"] = _out_proj(f["att"], f["ssm"], head, x, p["w_out"], _row1(p["norm_mix_post"]))
    f["hn2"], f["f"], f["dh2"], loss8 = _mlp_fwd(f["h1"], tgt, p["w_mlp_up"], p["w_mlp_down"],
                                                 _row1(p["norm_mlp_pre"]), _row1(p["norm_mlp_post"]), seq_rows)
    f["loss"] = 0.5 * jnp.sum(loss8) / D_MODEL
    return f


MLPB = 256


def _mlp_bwd(dh2, f, h1, hn2, w_up, w_down, g_pre, g_post):
    lp = h1.shape[0]

    def body(dh2_ref, f_ref, h1_ref, hn2_ref, wu_hbm, wd_hbm, gpre_ref, gpost_ref,
             dh1_ref, du_ref, a_ref, df_ref, dgpre_ref, dgpost_ref, wu, wd, sems):
        _resident(wu_hbm, wu, sems.at[0])
        _resident(wd_hbm, wd, sems.at[1])
        dh2_ = dh2_ref[...]
        df, dgp = _rms_bwd(f_ref[...], gpost_ref[...], dh2_)
        dfb = df.astype(BF16)
        df_ref[...] = dfb
        da = _dot(dfb, wd[...], NT)
        u = jnp.maximum(_dot(hn2_ref[...], wu[...]), 0.0)
        a_ref[...] = (u * u).astype(BF16)
        du = (da * (2.0 * u)).astype(BF16)
        du_ref[...] = du
        dhn2 = _dot(du, wu[...], NT)
        dx, dgq = _rms_bwd(h1_ref[...], gpre_ref[...], dhn2)
        dh1_ref[...] = dh2_ + dx
        _acc(dgpre_ref, _colsum8(dgq))
        _acc(dgpost_ref, _colsum8(dgp))

    return pl.pallas_call(
        body, name="mlp_bwd", grid=(lp // MLPB,),
        in_specs=[_rowspec(MLPB, D_MODEL)] * 4 + [ANY, ANY] + [_fullspec((1, D_MODEL))] * 2,
        out_specs=[_rowspec(MLPB, D_MODEL), _rowspec(MLPB, D_FF), _rowspec(MLPB, D_FF), _rowspec(MLPB, D_MODEL),
                   _fullspec((8, D_MODEL)), _fullspec((8, D_MODEL))],
        out_shape=[_sds((lp, D_MODEL), F32), _sds((lp, D_FF), BF16), _sds((lp, D_FF), BF16), _sds((lp, D_MODEL), BF16),
                   _sds((8, D_MODEL), F32), _sds((8, D_MODEL), F32)],
        scratch_shapes=[pltpu.VMEM((D_MODEL, D_FF), BF16), pltpu.VMEM((D_FF, D_MODEL), BF16), pltpu.SemaphoreType.DMA((2,))],
        compiler_params=_cp(("arbitrary",)),
    )(dh2, f, h1, hn2, w_up, w_down, g_pre, g_post)


def _out_bwd(dh1, mix, att, w_out, g_post):
    lp = dh1.shape[0]

    def body(dh1_ref, mix_ref, att_ref, w_ref, g_ref, dmix_ref, datt_ref, dssm_ref, dg_ref, dl_ref):
        dmix, dg = _rms_bwd(mix_ref[...], g_ref[...], dh1_ref[...])
        dmb = dmix.astype(BF16)
        dmix_ref[...] = dmb
        datt = _dot(dmb, w_ref[pl.ds(0, 1024), :], NT).astype(BF16)
        datt_ref[...] = datt
        dssm_ref[...] = _dot(dmb, w_ref[pl.ds(1024, 1024), :], NT)
        _acc(dg_ref, _colsum8(dg))
        prod = datt.astype(F32) * att_ref[...].astype(F32)
        for hh in range(ATT_HEADS):
            d = jnp.sum(prod[:, hh * V_HEAD:(hh + 1) * V_HEAD], axis=1, keepdims=True)
            dl_ref[hh] = jnp.broadcast_to(d, (ROWB, LANE)).T[0:1, :]

    return pl.pallas_call(
        body, name="out_bwd", grid=(lp // ROWB,),
        in_specs=[_rowspec(ROWB, D_MODEL)] * 3 + [_fullspec((2048, D_MODEL)), _fullspec((1, D_MODEL))],
        out_specs=[_rowspec(ROWB, D_MODEL)] * 3 + [_fullspec((8, D_MODEL)),
                                                   pl.BlockSpec((ATT_HEADS, 1, ROWB), lambda i: (0, 0, i))],
        out_shape=[_sds((lp, D_MODEL), BF16), _sds((lp, 1024), BF16), _sds((lp, 1024), F32), _sds((8, D_MODEL), F32),
                   _sds((ATT_HEADS, 1, lp), F32)],
        compiler_params=_cp(("arbitrary",)),
    )(dh1, mix, att, w_out, g_post)


def _ssd_post_bwd(dssm, y, xact, z, dskip, gnorm):
    lp = y.shape[0]

    def body(do_ref, y_ref, x_ref, z_ref, d_ref, g_ref, dy_ref, dz_ref, dg_ref, dd_ref):
        z_, x_ = z_ref[...], x_ref[...]
        sg = _sigmoid(z_)
        sz = z_ * sg
        y2 = y_ref[...] + d_ref[...] * x_
        gt = y2 * sz
        r = lax.rsqrt(_group_mean(gt * gt) + EPS)
        gh = gt * r
        do = do_ref[...]
        dgh = do * g_ref[...]
        dgt = r * (dgh - gh * _group_mean(dgh * gh))
        dy2 = dgt * sz
        dy_ref[...] = dy2
        dz_ref[...] = (dgt * y2 * (sg * (1.0 + z_ * (1.0 - sg)))).astype(BF16)
        _acc(dg_ref, _colsum8(do * gh))
        _acc(dd_ref, _colsum8(dy2 * x_))

    return pl.pallas_call(
        body, name="ssd_post_bwd", grid=(lp // ROWB,),
        in_specs=[_rowspec(ROWB, SSM_WIDTH)] * 4 + [_fullspec((1, SSM_WIDTH))] * 2,
        out_specs=[_rowspec(ROWB, SSM_WIDTH)] * 2 + [_fullspec((8, SSM_WIDTH))] * 2,
        out_shape=[_sds((lp, SSM_WIDTH), F32), _sds((lp, SSM_WIDTH), BF16), _sds((8, SSM_WIDTH), F32), _sds((8, SSM_WIDTH), F32)],
        compiler_params=_cp(("arbitrary",)),
    )(dssm, y, xact, z, dskip, gnorm)


def _ssd_bwd(dy, xact, dtr, hprev, dt_bias, a_log, dskip, seq_rows, exchange=()):
    lp = xact.shape[0]
    nc = lp // CHUNK
    gw = SSM_WIDTH // SSM_GROUPS
    hpg = SSM_HEADS // SSM_GROUPS
    nb = SSM_WIDTH // (2 * SSM_STATE)
    na = len(exchange)

    def body(dy_ref, x_ref, b_ref, c_ref, dtr_ref, hp_ref, bias_ref, alog_ref, dsk_ref, tri_ref, tri3_ref, trit3_ref,
             ex3_ref, ext3_ref, *rest):
        xin, (dact_ref, ddtr_ref, da_ref, dbias_ref), xout = rest[:na], rest[na:na + 4], rest[na + 4:2 * na + 4]
        dh_s, sems = rest[2 * na + 4], rest[2 * na + 5:]
        step = pl.program_id(0)
        c = nc - 1 - step

        @pl.when(step == 0)
        def _():
            dh_s[...] = jnp.zeros_like(dh_s)

        if na:
            x_start, x_finish = _exchange_ops(xin, xout, *sems)
            pl.when(step == 0)(x_start)

        tri = tri_ref[...]
        ex3 = ex3_ref[...]
        dt, a, acol, valid, dtr_ = _ssd_prep(dtr_ref, bias_ref, alog_ref, tri3_ref[...], c, seq_rows)
        arow = acol.T
        dtrow = dt.T
        alast = acol[CHUNK - 1:CHUNK, :]
        e_all = _dot01_r(jnp.exp(acol), ex3)
        wgt0 = jnp.exp(alast - acol)
        wgt = wgt0 * dt
        wx_all = _dot01_r(wgt, ex3)
        elast = jnp.exp(alast)
        dec_all = _dot01_r(_row16(elast), ex3)[0:1, :]
        causal = tri > 0.5
        upper = tri.T > 0.5
        lane_id = lax.broadcasted_iota(jnp.int32, (1, LANE), 1)
        sub_id = lax.broadcasted_iota(jnp.int32, (CHUNK, 1), 0)
        dacol = jnp.zeros((CHUNK, LANE), F32)
        darowf = jnp.zeros((CHUNK, LANE), F32)
        ddtrowf = jnp.zeros((CHUNK, LANE), F32)
        dwgt = jnp.zeros((CHUNK, LANE), F32)
        delast = jnp.zeros((1, LANE), F32)
        for g in range(SSM_GROUPS):
            gs = slice(g * gw, (g + 1) * gw)
            ext3_g = ext3_ref[g]
            bg = b_ref[:, g * SSM_STATE:(g + 1) * SSM_STATE]
            cg = c_ref[:, g * SSM_STATE:(g + 1) * SSM_STATE]
            bgb, cgb = bg.astype(BF16), cg.astype(BF16)
            xg = x_ref[:, gs]
            dyg = dy_ref[:, gs]
            hg = hp_ref[0, :, gs]
            dhg = dh_s[:, gs]
            hgb, dhgb = hg.astype(BF16), dhg.astype(BF16)
            gm = _dot(cgb, bgb, NT)
            gmt = _dot(bgb, cgb, NT)
            y_off = _dot(cgb, hgb) * e_all[:, gs]
            dy0 = (dyg * e_all[:, gs]).astype(BF16)
            dcg = _dot(dy0, hgb, NT)
            dh_in = _dot(cg.T.astype(BF16), dy0) + dhg * dec_all[:, gs]
            dacol = dacol + _dot01_r(dyg * y_off, ext3_g)
            xw = xg * wx_all[:, gs]
            dxw = _dot(bgb, dhgb)
            dx_state = dxw * wx_all[:, gs]
            dwgt = dwgt + _dot01_r(dxw * xg, ext3_g)
            dbt = _dot(dhgb, xw.astype(BF16), NT)
            hh = _colsum8(dhg * hg)
            hh16 = jnp.concatenate([hh, jnp.zeros_like(hh)], axis=0)
            delast = delast + jnp.sum(_dot01_r(hh16, ext3_g), axis=0, keepdims=True)
            dgm = jnp.zeros((CHUNK, CHUNK), F32)
            for r in range(hpg):
                hd = g * hpg + r
                cs = slice(r * SSM_HEAD_DIM, (r + 1) * SSM_HEAD_DIM)
                acol_r, arow_r = acol[:, hd:hd + 1], arow[hd:hd + 1, :]
                dtrow_r, dtcol_r = dtrow[hd:hd + 1, :], dt[:, hd:hd + 1]
                lm = jnp.where(causal, jnp.exp(jnp.where(causal, acol_r - arow_r, 0.0)), 0.0)
                lmt = jnp.where(upper, jnp.exp(jnp.where(upper, arow_r - acol_r, 0.0)), 0.0)
                wt = gmt * lmt * dtcol_r
                dy_r = dyg[:, cs].astype(BF16)
                dx_r = _dot(wt.astype(BF16), dy_r)
                dw = _dot(dy_r, xg[:, cs].astype(BF16), NT)
                t1 = dw * lm
                dgm = dgm + t1 * dtrow_r
                q1 = t1 * gm
                m = q1 * dtrow_r
                dacol = dacol + jnp.sum(m, axis=1, keepdims=True) * (lane_id == hd).astype(F32)
                darowf = darowf - (sub_id == hd).astype(F32) * jnp.sum(m, axis=0, keepdims=True)
                ddtrowf = ddtrowf + (sub_id == hd).astype(F32) * jnp.sum(q1, axis=0, keepdims=True)
                dact_ref[:, pl.ds(hd * SSM_HEAD_DIM, SSM_HEAD_DIM)] = (
                    dx_r + dx_state[:, cs] + dyg[:, cs] * dsk_ref[:, pl.ds(hd * SSM_HEAD_DIM, SSM_HEAD_DIM)])
            dgmb = dgm.astype(BF16)
            dact_ref[:, pl.ds(SSM_WIDTH + g * SSM_STATE, SSM_STATE)] = dbt.T + _dot(dgm.T.astype(BF16), cgb)
            dact_ref[:, pl.ds(SSM_WIDTH + 2 * SSM_STATE + g * SSM_STATE, SSM_STATE)] = dcg + _dot(dgmb, bgb)
            dh_s[:, gs] = dh_in
        t = dwgt * wgt
        dalast = jnp.sum(t, axis=0, keepdims=True) + delast * elast
        dacol_tot = dacol - t + darowf.T + (sub_id == CHUNK - 1).astype(F32) * dalast
        dda = _dot01_l(trit3_ref[...], dacol_tot)
        ddt = dwgt * wgt0 + ddtrowf.T + dda * a
        ddtr = jnp.where(valid, ddt * _sigmoid(dtr_), 0.0)
        ddtr_ref[...] = ddtr
        _acc(da_ref, _colsum8(dda * dt) * a)
        _acc(dbias_ref, _colsum8(ddtr))
        if na:
            pl.when(step == nc - 1)(x_finish)

    rev = lambda c: nc - 1 - c
    xs_spec = pl.BlockSpec((CHUNK, SSM_WIDTH), lambda c: (rev(c), 0))
    dact, ddtr, da8, dbias8, *received = pl.pallas_call(
        body, name="ssd_bwd", grid=(nc,),
        in_specs=[xs_spec, xs_spec,
                  pl.BlockSpec((CHUNK, 2 * SSM_STATE), lambda c: (rev(c), nb)),
                  pl.BlockSpec((CHUNK, 2 * SSM_STATE), lambda c: (rev(c), nb + 1)),
                  pl.BlockSpec((CHUNK, LANE), lambda c: (rev(c), 0)),
                  pl.BlockSpec((1, SSM_STATE, SSM_WIDTH), lambda c: (rev(c), 0, 0)),
                  _fullspec((1, LANE)), _fullspec((1, LANE)), _fullspec((1, SSM_WIDTH)),
                  _fullspec((CHUNK, CHUNK)), _fullspec((CHUNK, 3 * CHUNK)), _fullspec((CHUNK, 3 * CHUNK)),
                  _fullspec((3 * LANE, SSM_WIDTH)), _fullspec((SSM_GROUPS, 3 * gw, LANE))] + [ANY] * na,
        out_specs=[pl.BlockSpec((CHUNK, CONV_DIM), lambda c: (rev(c), 0)), pl.BlockSpec((CHUNK, LANE), lambda c: (rev(c), 0)),
                   _fullspec((8, LANE)), _fullspec((8, LANE))] + [ANY] * na,
        out_shape=[_sds((lp, CONV_DIM), F32), _sds((lp, LANE), F32), _sds((8, LANE), F32), _sds((8, LANE), F32)]
        + [_sds(e.shape, e.dtype) for e in exchange],
        scratch_shapes=[pltpu.VMEM((SSM_STATE, SSM_WIDTH), F32)] + (_gather_scratch(na) if na else []),
        compiler_params=_cp(("arbitrary",)),
    )(dy, xact, xact, xact, dtr, hprev, dt_bias, a_log, dskip, _tri_mat(), _x3(_tri_mat(), 1), _x3(_tri_mat().T, 1),
      _x3(_expand_mat(), 0), jnp.stack([_x3(_expand_mat().T[g * gw:(g + 1) * gw], 0) for g in range(SSM_GROUPS)]),
      *exchange)
    return dact, ddtr, da8, dbias8, received


def _conv_bwd(dact, xbc, cw, cb):
    lp, c = xbc.shape
    t8 = ROWB // 8
    nb = lp // ROWB

    def body(d_ref, dnext_ref, x_ref, prev_ref, next_ref, w_ref, b_ref, dx_ref, dw_ref, db_ref, xb, dp):
        i = pl.program_id(0)
        last = i == nb - 1
        xb[pl.ds(0, 8), :] = jnp.where(i > 0, prev_ref[...], 0.0)
        xb[pl.ds(8, ROWB), :] = x_ref[...]
        xb[pl.ds(8 + ROWB, 8), :] = jnp.where(last, 0.0, next_ref[...])

        @pl.when(i == 0)
        def _():
            dw_ref[...] = jnp.zeros_like(dw_ref)
            db_ref[...] = jnp.zeros_like(db_ref)

        sub = lax.broadcasted_iota(jnp.int32, (8, 1), 0)
        x0 = 8 - (CONV_K - 1)

        def strip(s, carry):
            cs = pl.ds(pl.multiple_of(s * LANE, LANE), LANE)
            w, b = w_ref[:, cs], b_ref[:, cs]

            def dpre_rows(r0, n, d):
                xs = [xb[pl.ds(x0 + kk + r0, n), cs] for kk in range(CONV_K)]
                pre = b + sum(w[kk:kk + 1, :] * xs[kk] for kk in range(CONV_K))
                sg = _sigmoid(pre)
                return d * (sg * (1.0 + pre * (1.0 - sg))), xs

            dws = [jnp.zeros((8, LANE), F32) for _ in range(CONV_K)]
            dbs = jnp.zeros((8, LANE), F32)
            for r0 in range(0, ROWB, CONV_ROWS):
                dpre, xs = dpre_rows(r0, CONV_ROWS, d_ref[pl.ds(r0, CONV_ROWS), cs])
                dp[pl.ds(r0, CONV_ROWS), cs] = dpre
                dbs = dbs + _colsum8(dpre)
                for kk in range(CONV_K):
                    dws[kk] = dws[kk] + _colsum8(dpre * xs[kk])
            dp[pl.ds(ROWB, 8), cs] = dpre_rows(ROWB, 8, jnp.where(last, 0.0, dnext_ref[:, cs]))[0]
            dwv = sum(jnp.where(sub == kk, jnp.sum(dws[kk], axis=0, keepdims=True), 0.0) for kk in range(CONV_K))
            dw_ref[:, cs] += dwv
            db_ref[:, cs] += dbs
            for r0 in range(0, ROWB, CONV_ROWS):
                dx = sum(w[kk:kk + 1, :] * dp[pl.ds(CONV_K - 1 - kk + r0, CONV_ROWS), cs] for kk in range(CONV_K))
                dx_ref[pl.ds(r0, CONV_ROWS), cs] = dx.astype(BF16)
            return carry

        lax.fori_loop(0, c // LANE, strip, 0)

    nxt = lambda i: (jnp.minimum((i + 1) * t8, lp // 8 - 1), 0)
    prv = lambda i: (jnp.maximum(i * t8 - 1, 0), 0)
    return pl.pallas_call(
        body, name="conv_bwd", grid=(nb,),
        in_specs=[_rowspec(ROWB, c), pl.BlockSpec((8, c), nxt), _rowspec(ROWB, c), pl.BlockSpec((8, c), prv),
                  pl.BlockSpec((8, c), nxt), _fullspec((8, c)), _fullspec((1, c))],
        out_specs=[_rowspec(ROWB, c), _fullspec((8, c)), _fullspec((8, c))],
        out_shape=[_sds((lp, c), BF16), _sds((8, c), F32), _sds((8, c), F32)],
        scratch_shapes=[pltpu.VMEM((ROWB + 16, c), F32), pltpu.VMEM((ROWB + 8, c), F32)],
        compiler_params=_cp(("arbitrary",)),
    )(dact, dact, xbc, xbc, xbc, cw, cb)


def _flash_bwd(q, k, v, datt, lse_row, delta_row, cos, sa, sb):
    lp = q.shape[0]
    nk = lp // ROWB

    def body(k_ref, v_ref, q_ref, do_ref, lse_ref, dl_ref, cos_ref, sa_ref, sb_ref, dq_ref, dk_ref, dv_ref,
             dq_acc, dk_acc, dv_acc):
        j = pl.program_id(1)

        @pl.when(j == 0)
        def _():
            dq_acc[...] = jnp.zeros_like(dq_acc)

        dk_acc[...] = jnp.zeros_like(dk_acc)
        dv_acc[...] = jnp.zeros_like(dv_acc)

        def tile(i, masked, key0=0, nkeys=ROWB):
            keys = pl.ds(key0, nkeys)
            kb, vb = k_ref[keys, :], v_ref[keys, :]
            off = pl.multiple_of(i * ROWB, ROWB)
            qb = q_ref[pl.ds(off, ROWB), :]
            dob = do_ref[pl.ds(off, ROWB), :]
            lse_r = lse_ref[0, :, pl.ds(off, ROWB)]
            dl_r = dl_ref[0, :, pl.ds(off, ROWB)]
            st = _dot(kb, qb, NT)
            if masked:
                krow = j * ROWB + key0 + lax.broadcasted_iota(jnp.int32, st.shape, 0)
                qrow = i * ROWB + lax.broadcasted_iota(jnp.int32, st.shape, 1)
                st = jnp.where(_att_ok(qrow, krow), st, NEG)
            pt = jnp.exp2(st - lse_r)
            dv_acc[keys, :] += _dot(pt.astype(BF16), dob)
            dpt = _dot(vb, dob, NT)
            dst = (pt * (dpt - dl_r)).astype(BF16)
            dk_acc[keys, :] += _dot(dst, qb)
            dq_acc[pl.ds(off, ROWB), :] += _dot(dst, kb, ((0,), (0,)))

        @pl.when(j == 0)
        def _():
            _pair_loop(0, nk, lambda i: tile(i, True, ROWB - META_KEYS, META_KEYS), (4, 2))

        @pl.when((j > 0) & (j < nk - 1))
        def _():
            tile(j, True)
            tile(j + 1, False)
            _pair_loop(j + 2, nk, lambda i: tile(i, False), (4, 2))

        @pl.when(j == nk - 1)
        def _():
            tile(j, True)

        dk_ref[...] = (dk_acc[...] * LN2).astype(BF16)
        dv_ref[...] = dv_acc[...].astype(BF16)
        dq = dq_acc[pl.ds(pl.multiple_of(j * ROWB, ROWB), ROWB), :] * ATT_SCALE
        dq_ref[...] = _rope_t(dq, cos_ref[...], sa_ref[...], sb_ref[...]).astype(BF16)

    stat = pl.BlockSpec((1, 1, lp), lambda h, j: (h, 0, 0))
    blk = pl.BlockSpec((ROWB, HEADW), lambda h, j: (j, h))
    tab = pl.BlockSpec((ROWB, HEADW), lambda h, j: (j, 0))
    return pl.pallas_call(
        body, name="flash_bwd", grid=(ATT_HEADS, nk),
        in_specs=[blk, pl.BlockSpec((ROWB, V_HEAD), lambda h, j: (j, 2 * h)),
                  pl.BlockSpec((lp, HEADW), lambda h, j: (0, h)), pl.BlockSpec((lp, V_HEAD), lambda h, j: (0, h)),
                  stat, stat, tab, tab, tab],
        out_specs=[blk, blk, pl.BlockSpec((ROWB, V_HEAD), lambda h, j: (j, h))],
        out_shape=[_sds((lp, ATT_HEADS * HEADW), BF16), _sds((lp, ATT_HEADS * HEADW), BF16),
                   _sds((lp, ATT_HEADS * V_HEAD), BF16)],
        scratch_shapes=[pltpu.VMEM((lp, HEADW), F32), pltpu.VMEM((ROWB, HEADW), F32), pltpu.VMEM((ROWB, V_HEAD), F32)],
        compiler_params=_cp(("arbitrary", "arbitrary")),
    )(k, v, q, datt, lse_row, delta_row, cos, sa, sb)


def _qkv_bwd(dqp, dk, dv, cq, ckv, cos, sa, sb, gq, gkv, wq, wk, wv):
    lp = cq.shape[0]
    qw = ATT_HEADS * HEADW

    def body(dqp_ref, dk_ref, dv_ref, cq_ref, ckv_ref, cos_ref, sa_ref, sb_ref, gq_ref, gkv_ref, wq_ref, wk_ref, wv_ref,
             dcq_ref, dckv_ref, dkr_ref, dgq_ref, dgkv_ref):
        dcq, dgq = _rms_bwd(cq_ref[...], gq_ref[...], _dot(dqp_ref[...], wq_ref[...], NT))
        dcq_ref[...] = dcq.astype(BF16)
        dkb = dk_ref[...]
        dksum = sum(dkb[:, hh * HEADW:(hh + 1) * HEADW].astype(F32) for hh in range(ATT_HEADS))
        dkr_ref[...] = _rope_t(dksum, cos_ref[...], sa_ref[...], sb_ref[...]).astype(BF16)
        dckvn = _dot(dkb, wk_ref[...], NT) + _dot(dv_ref[...], wv_ref[...], NT)
        dckv, dgkv = _rms_bwd(ckv_ref[...], gkv_ref[...], dckvn)
        dckv_ref[...] = dckv.astype(BF16)
        _acc(dgq_ref, _colsum8(dgq))
        _acc(dgkv_ref, _colsum8(dgkv))

    return pl.pallas_call(
        body, name="qkv_bwd", grid=(lp // ROWB,),
        in_specs=[_rowspec(ROWB, qw), _rowspec(ROWB, qw), _rowspec(ROWB, ATT_HEADS * V_HEAD),
                  _rowspec(ROWB, Q_LORA), _rowspec(ROWB, KV_LORA)] + [_rowspec(ROWB, HEADW)] * 3
        + [_fullspec((1, Q_LORA)), _fullspec((1, KV_LORA)), _fullspec((Q_LORA, qw)), _fullspec((KV_LORA, qw)),
           _fullspec((KV_LORA, ATT_HEADS * V_HEAD))],
        out_specs=[_rowspec(ROWB, Q_LORA), _rowspec(ROWB, KV_LORA), _rowspec(ROWB, HEADW),
                   _fullspec((8, Q_LORA)), _fullspec((8, KV_LORA))],
        out_shape=[_sds((lp, Q_LORA), BF16), _sds((lp, KV_LORA), BF16), _sds((lp, HEADW), BF16),
                   _sds((8, Q_LORA), F32), _sds((8, KV_LORA), F32)],
        compiler_params=_cp(("arbitrary",)),
    )(dqp, dk, dv, cq, ckv, cos, sa, sb, gq, gkv, wq, wk, wv)


def _in_bwd(pieces, head, x, dh1, g, w_all):
    lp = dh1.shape[0]
    npc = len(pieces)
    assert sum(pc.shape[1] for pc in pieces) == PROJ_W

    def body(*refs):
        head_ref, x_ref, dh1_ref, g_ref, w_ref, dx_ref, dhead_ref, dg_ref, dp_ref = refs[npc:]
        dp = jnp.concatenate([r[...].astype(BF16) for r in refs[:npc]], axis=1)
        dp_ref[...] = dp
        dx, dg = _rms_bwd(_h_block(head_ref, x_ref), g_ref[...], _dot(dp, w_ref[...], NT))
        dh = dh1_ref[...] + dx

        @pl.when(pl.program_id(0) == 0)
        def _():
            dhead_ref[...] = dh

        @pl.when(pl.program_id(0) > 0)
        def _():
            dx_ref[...] = dh

        _acc(dg_ref, _colsum8(dg))

    return pl.pallas_call(
        body, name="in_bwd", grid=(lp // ROWB,),
        in_specs=[_rowspec(ROWB, pc.shape[1]) for pc in pieces]
        + [_fullspec((ROWB, D_MODEL)), _xspec(), _rowspec(ROWB, D_MODEL), _fullspec((1, D_MODEL)),
           _fullspec((D_MODEL, PROJ_W))],
        out_specs=[_xspec(), _fullspec((ROWB, D_MODEL)), _fullspec((8, D_MODEL)), _rowspec(ROWB, PROJ_W)],
        out_shape=[_sds(x.shape, F32), _sds((ROWB, D_MODEL), F32), _sds((8, D_MODEL), F32), _sds((lp, PROJ_W), BF16)],
        compiler_params=_cp(("arbitrary",)),
    )(*pieces, head, x, dh1, g, w_all)


def _tile_of(n, cap=1024):
    return max(t for t in range(LANE, min(n, cap) + 1, LANE) if n % t == 0)


def _matmul_tn(name, a, b):
    rows, kd = a.shape
    nd = b.shape[1]
    tk, tn = _tile_of(kd), _tile_of(nd)
    rb = 3 * ROWB if rows % (3 * ROWB) == 0 else ROWB

    def body(a_ref, b_ref, o_ref):
        @pl.when(pl.program_id(2) == 0)
        def _():
            o_ref[...] = jnp.zeros_like(o_ref)

        o_ref[...] += _dot(a_ref[...], b_ref[...], ((0,), (0,)))

    return pl.pallas_call(
        body, name=name, grid=(kd // tk, nd // tn, rows // rb),
        in_specs=[pl.BlockSpec((rb, tk), lambda i, j, r: (r, i)), pl.BlockSpec((rb, tn), lambda i, j, r: (r, j))],
        out_specs=pl.BlockSpec((tk, tn), lambda i, j, r: (i, j)), out_shape=_sds((kd, nd), F32),
        compiler_params=_cp(("arbitrary", "arbitrary", "arbitrary")),
    )(a, b)


def _local_backward(head, x, f, exchange_late=False):
    p = f["p"]
    g = {}
    row = lambda v: _row1(v)
    s8 = lambda v: jnp.sum(v, axis=0)
    dh1, du, a_, df, dgpre, dgpost = _mlp_bwd(f["dh2"], f["f"], f["h1"], f["hn2"], p["w_mlp_up"], p["w_mlp_down"],
                                              row(p["norm_mlp_pre"]), row(p["norm_mlp_post"]))
    g["norm_mlp_pre"], g["norm_mlp_post"] = s8(dgpre), s8(dgpost)
    g["w_mlp_up"] = _matmul_tn("dw_mlp_up", f["hn2"], du)
    g["w_mlp_down"] = _matmul_tn("dw_mlp_down", a_, df)
    dmix, datt, dssm, dgmp, delta = _out_bwd(dh1, f["mix"], f["att"], p["w_out"], row(p["norm_mix_post"]))
    g["norm_mix_post"] = s8(dgmp)
    g["w_out"] = jnp.concatenate([_matmul_tn("dw_out_att", f["att"], dmix), _matmul_tn("dw_out_ssm", f["ssm"], dmix)], axis=0)
    dy, dz, dgn, dd = _ssd_post_bwd(dssm, f["y"], f["xact"], f["z"], f["dskip"], row(p["ssm_norm"]))
    g["ssm_norm"] = s8(dgn)
    g["d_skip"] = s8(dd).reshape(SSM_HEADS, SSM_HEAD_DIM).sum(axis=1)
    dact, ddtr, da8, dbias8, received = _ssd_bwd(
        dy, f["xact"], f["dtr"], f["hprev"], f["dt_bias"], f["a_log"], f["dskip"], f["seq_rows"],
        exchange=[_to_chunks(n, g[n]).astype(BF16) for n in LATE] if exchange_late else ())
    g["a_log"], g["dt_bias"] = s8(da8)[:SSM_HEADS], s8(dbias8)[:SSM_HEADS]
    dxbc, dcw8, dcb8 = _conv_bwd(dact, f["xbc"], f["cw"], row(p["conv_b"]))
    g["conv_w"], g["conv_b"] = dcw8[:CONV_K], s8(dcb8)
    dqp, dkb, dv = _flash_bwd(f["q"], f["k"], f["v"], datt, f["lse"], delta, *f["rope"])
    dcq, dckv, dkr, dgq, dgkv = _qkv_bwd(dqp, dkb, dv, f["cq"], f["ckv"], *f["rope"], row(p["q_a_norm"]),
                                         row(p["kv_a_norm"]), f["wq"], f["wk"], f["wv"])
    g["q_a_norm"], g["kv_a_norm"] = s8(dgq), s8(dgkv)
    dwq = _matmul_tn("dw_q_up", f["cqn"], dqp).reshape(Q_LORA, ATT_HEADS, HEADW)
    g["w_q_up"] = dwq[:, :, :QK_NOPE + QK_ROPE].reshape(Q_LORA, -1)
    dwk = _matmul_tn("dw_k_up", f["ckvn"], dkb).reshape(KV_LORA, ATT_HEADS, HEADW)[:, :, :QK_NOPE]
    dwv = _matmul_tn("dw_v_up", f["ckvn"], dv).reshape(KV_LORA, ATT_HEADS, V_HEAD)
    g["w_kv_up"] = jnp.concatenate([dwk, dwv], axis=2).reshape(KV_LORA, -1)
    dx, dhead, dgin, dproj = _in_bwd([dcq, dckv, dkr, dz, dxbc, ddtr], head, x, dh1, row(p["norm_mix_pre"]), f["w_all"])
    g["norm_mix_pre"] = s8(dgin)
    g["meta_tokens"] = dhead[PADF:]
    dwa = _matmul_tn("dw_in", f["hn"], dproj)
    g["w_in"] = jnp.concatenate([dwa[:, PC_Q:PC_KR], dwa[:, PC_KR + QK_NOPE:PC_KR + QK_NOPE + QK_ROPE],
                                 dwa[:, PC_Z:PC_DT + SSM_HEADS]], axis=1)
    return dx, g, received


BIG = {"w_in": ((D_MODEL, IN_WIDTH), 1), "w_q_up": ((Q_LORA, ATT_HEADS * (QK_NOPE + QK_ROPE)), 1),
       "w_kv_up": ((KV_LORA, ATT_HEADS * (QK_NOPE + V_HEAD)), 1), "w_out": ((2 * D_MODEL, D_MODEL), 0),
       "w_mlp_up": ((D_MODEL, D_FF), 1), "w_mlp_down": ((D_FF, D_MODEL), 0), "conv_w": ((CONV_K, CONV_DIM), 1),
       "meta_tokens": ((N_META, D_MODEL), 1)}
SMALL = {"norm_mix_pre": D_MODEL, "q_a_norm": Q_LORA, "kv_a_norm": KV_LORA, "conv_b": CONV_DIM, "dt_bias": SSM_HEADS,
         "a_log": SSM_HEADS, "d_skip": SSM_HEADS, "ssm_norm": SSM_WIDTH, "norm_mix_post": D_MODEL,
         "norm_mlp_pre": D_MODEL, "norm_mlp_post": D_MODEL}
WEIGHT_ORDER = ("meta_tokens", "norm_mix_pre", "w_in", "q_a_norm", "w_q_up", "kv_a_norm", "w_kv_up", "conv_w", "conv_b",
                "dt_bias", "a_log", "d_skip", "ssm_norm", "w_out", "norm_mix_post", "norm_mlp_pre", "w_mlp_up",
                "w_mlp_down", "norm_mlp_post")
ADAM_ROWS = 256


def _shard_shape(name):
    shape, ax = BIG[name]
    return tuple(d // N_DEV if a == ax else d for a, d in enumerate(shape))


SMALL_ROWS = -(-sum(SMALL.values()) // (LANE * 8)) * 8


def _pack(flats, rows):
    v = jnp.concatenate([f.reshape(-1) for f in flats])
    return jnp.pad(v, (0, rows * LANE - v.shape[0])).reshape(rows, LANE)


def _unpack(packed, shapes):
    v = packed.reshape(-1)
    out, o = [], 0
    for s in shapes:
        n = math.prod(s)
        out.append(v[o:o + n].reshape(s))
        o += n
    return out


def _to_chunks(name, full):
    shape, ax = BIG[name]
    if ax == 0:
        return full.reshape((N_DEV,) + _shard_shape(name))
    k, n = shape
    return full.reshape(k, N_DEV, n // N_DEV).transpose(1, 0, 2)


def _from_shards(name, shards):
    shape, ax = BIG[name]
    if ax == 0:
        return shards.reshape(shape)
    return shards.transpose(1, 0, 2).reshape(shape)


def _peer(k):
    x, y, c = lax.axis_index("x"), lax.axis_index("y"), lax.axis_index("c")
    px = 1 - x if k & 4 else x
    py = 1 - y if k & 2 else y
    pc = 1 - c if k & 1 else c
    return (px, py, pc), 4 * px + 2 * py + pc


def _gather_ops(x_refs, out_refs, send_sems, recv_sems, local_sems):
    na = len(x_refs)
    chips = (4, 2, 6)

    def copy(a, n, block, to, src=None):
        return pltpu.make_async_remote_copy(
            src_ref=out_refs[a].at[block] if src is None else src, dst_ref=out_refs[a].at[block],
            send_sem=send_sems.at[7 * a + n], recv_sem=recv_sems.at[7 * a + n], device_id=to, device_id_type=MESH)

    def mine():
        me = _peer(0)[1]
        return [pltpu.make_async_copy(x_refs[a], out_refs[a].at[me], local_sems.at[a]) for a in range(na)]

    def first():
        me, sibling = _peer(0)[1], _peer(1)[0]
        out = [copy(a, 0, me, sibling, src=x_refs[a]) for a in range(na)]
        return out + [copy(a, 1 + n, me, _peer(k)[0], src=x_refs[a]) for n, k in enumerate(chips) for a in range(na)]

    def passed():
        sibling = _peer(1)[0]
        return [copy(a, 4 + n, _peer(k)[1], sibling) for n, k in enumerate(chips) for a in range(na)]

    def start():
        for cp in mine() + first():
            cp.start()

    def forward():
        sibling = _peer(1)[0]
        fwd = passed()
        for n, k in enumerate(chips):
            for a in range(na):
                copy(a, 1 + n, _peer(k)[1], sibling).wait_recv()
                fwd[n * na + a].start()

    def finish():
        sibling = _peer(1)[0]
        for a in range(na):
            copy(a, 0, _peer(1)[1], sibling).wait_recv()
        for n, k in enumerate(chips):
            for a in range(na):
                copy(a, 4 + n, _peer(k | 1)[1], sibling).wait_recv()
        for cp in first() + passed():
            cp.wait_send()
        for cp in mine():
            cp.wait()

    return start, forward, finish


def _gather_scratch(na):
    return [pltpu.SemaphoreType.DMA((7 * na,)), pltpu.SemaphoreType.DMA((7 * na,)), pltpu.SemaphoreType.DMA((na,))]


def _all_gather(shards):
    na = len(shards)

    def body(*refs):
        for step in _gather_ops(refs[:na], refs[na:2 * na], *refs[2 * na:]):
            step()

    return pl.pallas_call(
        body, name="all_gather_weights", out_shape=[_sds((N_DEV,) + s.shape, s.dtype) for s in shards],
        in_specs=[ANY] * na, out_specs=[ANY] * na, scratch_shapes=_gather_scratch(na),
    )(*shards)


def _exchange(chunks, small):
    na = len(chunks) + 1

    def body(*refs):
        for step in _exchange_ops(refs[:na], refs[na:2 * na], *refs[2 * na:], whole=(na - 1,)):
            step()

    arrays = list(chunks) + [small]
    return pl.pallas_call(
        body, name="exchange_grads",
        out_shape=[_sds(c.shape, c.dtype) for c in chunks] + [_sds((N_DEV,) + small.shape, small.dtype)],
        in_specs=[ANY] * na, out_specs=[ANY] * na, scratch_shapes=_gather_scratch(na),
    )(*arrays)


def _exchange_ops(in_refs, out_refs, send_sems, recv_sems, local_sems, whole=()):
    na = len(in_refs)

    def src(a, idx):
        return in_refs[a] if a in whole else in_refs[a].at[idx]

    def own():
        me = _peer(0)[1]
        return [pltpu.make_async_copy(src(a, me), out_refs[a].at[me], local_sems.at[a]) for a in range(na)]

    def copy(a, k, sending):
        me = _peer(0)[1]
        to, idx = _peer(k)
        return pltpu.make_async_remote_copy(
            src_ref=src(a, idx if sending else me), dst_ref=out_refs[a].at[me if sending else idx],
            send_sem=send_sems.at[7 * a + k - 1], recv_sem=recv_sems.at[7 * a + k - 1],
            device_id=to, device_id_type=MESH)

    def sent():
        return [copy(a, k, True) for k in range(1, N_DEV) for a in range(na)]

    def start():
        for cp in own() + sent():
            cp.start()

    def finish():
        for k in range(1, N_DEV):
            for a in range(na):
                copy(a, k, False).wait_recv()
        for cp in sent():
            cp.wait_send()
        for cp in own():
            cp.wait()

    return start, finish


def _reduce_adamw(name, recv, w, m, v):
    rows, cols = w.shape
    blk = ADAM_ROWS if rows % ADAM_ROWS == 0 else rows
    c1 = 1.0 - ADAM_B1 ** ADAM_STEP
    c2 = 1.0 - ADAM_B2 ** ADAM_STEP

    def body(r_ref, w_ref, m_ref, v_ref, g_ref, d_ref, nm_ref, nv_ref):
        g = r_ref[0].astype(F32)
        for s in range(1, N_DEV):
            g = g + r_ref[s].astype(F32)
        g_ref[...] = g
        m_ = ADAM_B1 * m_ref[...] + (1.0 - ADAM_B1) * g
        v_ = ADAM_B2 * v_ref[...] + (1.0 - ADAM_B2) * (g * g)
        nm_ref[...] = m_
        nv_ref[...] = v_
        d_ref[...] = -ADAM_LR * ((m_ / c1) / (jnp.sqrt(v_ / c2) + ADAM_EPS) + ADAM_WD * w_ref[...])

    spec = _rowspec(blk, cols)
    return pl.pallas_call(
        body, name="reduce_adamw_" + name, grid=(rows // blk,),
        in_specs=[pl.BlockSpec((N_DEV, blk, cols), lambda i: (0, i, 0)), spec, spec, spec],
        out_specs=[spec] * 4, out_shape=[_sds((rows, cols), F32)] * 4,
        compiler_params=_cp(("arbitrary",)),
    )(recv, w, m, v)


def kernel(x, meta_tokens, norm_mix_pre, w_in, q_a_norm, w_q_up, kv_a_norm, w_kv_up, conv_w, conv_b, dt_bias, a_log, d_skip, ssm_norm, w_out, norm_mix_post, norm_mlp_pre, w_mlp_up, w_mlp_down, norm_mlp_post, loss_target, m_meta_tokens, m_norm_mix_pre, m_w_in, m_q_a_norm, m_w_q_up, m_kv_a_norm, m_w_kv_up, m_conv_w, m_conv_b, m_dt_bias, m_a_log, m_d_skip, m_ssm_norm, m_w_out, m_norm_mix_post, m_norm_mlp_pre, m_w_mlp_up, m_w_mlp_down, m_norm_mlp_post, v_meta_tokens, v_norm_mix_pre, v_w_in, v_q_a_norm, v_w_q_up, v_kv_a_norm, v_w_kv_up, v_conv_w, v_conv_b, v_dt_bias, v_a_log, v_d_skip, v_ssm_norm, v_w_out, v_norm_mix_post, v_norm_mlp_pre, v_w_mlp_up, v_w_mlp_down, v_norm_mlp_post):
    w = dict(meta_tokens=meta_tokens, norm_mix_pre=norm_mix_pre, w_in=w_in, q_a_norm=q_a_norm, w_q_up=w_q_up,
             kv_a_norm=kv_a_norm, w_kv_up=w_kv_up, conv_w=conv_w, conv_b=conv_b, dt_bias=dt_bias, a_log=a_log,
             d_skip=d_skip, ssm_norm=ssm_norm, w_out=w_out, norm_mix_post=norm_mix_post, norm_mlp_pre=norm_mlp_pre,
             w_mlp_up=w_mlp_up, w_mlp_down=w_mlp_down, norm_mlp_post=norm_mlp_post)
    m = dict(meta_tokens=m_meta_tokens, norm_mix_pre=m_norm_mix_pre, w_in=m_w_in, q_a_norm=m_q_a_norm, w_q_up=m_w_q_up,
             kv_a_norm=m_kv_a_norm, w_kv_up=m_w_kv_up, conv_w=m_conv_w, conv_b=m_conv_b, dt_bias=m_dt_bias,
             a_log=m_a_log, d_skip=m_d_skip, ssm_norm=m_ssm_norm, w_out=m_w_out, norm_mix_post=m_norm_mix_post,
             norm_mlp_pre=m_norm_mlp_pre, w_mlp_up=m_w_mlp_up, w_mlp_down=m_w_mlp_down, norm_mlp_post=m_norm_mlp_post)
    v = dict(meta_tokens=v_meta_tokens, norm_mix_pre=v_norm_mix_pre, w_in=v_w_in, q_a_norm=v_q_a_norm, w_q_up=v_w_q_up,
             kv_a_norm=v_kv_a_norm, w_kv_up=v_w_kv_up, conv_w=v_conv_w, conv_b=v_conv_b, dt_bias=v_dt_bias,
             a_log=v_a_log, d_skip=v_d_skip, ssm_norm=v_ssm_norm, w_out=v_w_out, norm_mix_post=v_norm_mix_post,
             norm_mlp_pre=v_norm_mlp_pre, w_mlp_up=v_w_mlp_up, w_mlp_down=v_w_mlp_down, norm_mlp_post=v_norm_mlp_post)
    big_names = [n for n in WEIGHT_ORDER if n in BIG]
    small_names = [n for n in WEIGHT_ORDER if n in SMALL]
    shard = lambda d, n: d[n].reshape(_shard_shape(n))

    f32_names = ("conv_w", "meta_tokens")
    early = [n for n in big_names if n not in LATE]
    gathered = _all_gather([shard(w, n).astype(F32 if n in f32_names else BF16) for n in early])
    p = {n: w[n].reshape(-1) for n in small_names}
    p.update({n: _from_shards(n, s) for n, s in zip(early, gathered)})
    head = jnp.concatenate([jnp.zeros((PADF, D_MODEL), F32), p["meta_tokens"]], axis=0)
    f = _local_forward(head, x[0], loss_target[0], p, late={n: shard(w, n).astype(BF16) for n in LATE})
    dx, g, recv_late = _local_backward(head, x[0], f, exchange_late=True)
    grad_x = dx[None]
    loss = lax.psum(f["loss"], ("x", "y", "c"))

    small = _pack([g[n] for n in small_names], SMALL_ROWS)
    *recv_early, recv_small = _exchange([_to_chunks(n, g[n]).astype(BF16) for n in early], small)
    recv_of = {**dict(zip(early, recv_early)), **dict(zip(LATE, recv_late))}

    outs = {}
    kinds = ("grad", "delta", "new_m", "new_v")
    for n, recv in ((n, recv_of[n]) for n in big_names):
        for kind, arr in zip(kinds, _reduce_adamw(n, recv, shard(w, n), shard(m, n), shard(v, n))):
            outs[kind, n] = arr.reshape(w[n].shape)
    packed = [_pack([d[n] for n in small_names], SMALL_ROWS) for d in (w, m, v)]
    for kind, arr in zip(kinds, _reduce_adamw("small", recv_small, *packed)):
        for n, val in zip(small_names, _unpack(arr, [(SMALL[n],) for n in small_names])):
            outs[kind, n] = val.reshape(w[n].shape)
    return (loss, grad_x) + tuple(outs[kind, n] for kind in ("grad", "delta", "new_m", "new_v") for n in WEIGHT_ORDER)
```

```python
import math

import jax
import jax.numpy as jnp
import numpy as np
from jax import lax
from jax.experimental import pallas as pl
from jax.experimental.pallas import tpu as pltpu

F32 = jnp.float32
BF16 = jnp.bfloat16

D_MODEL = 1024
N_META = 16
EPS = 1e-6
ATT_HEADS = 8
Q_LORA = 384
KV_LORA = 256
QK_NOPE = 128
QK_ROPE = 64
V_HEAD = 128
ROPE_THETA = 10000.0
SSM_HEADS = 16
SSM_HEAD_DIM = 64
SSM_WIDTH = 1024
SSM_GROUPS = 2
SSM_STATE = 128
CONV_K = 4
CHUNK = 128
CONV_DIM = 1536
D_FF = 4096
IN_SPLITS = (Q_LORA, KV_LORA, QK_ROPE, SSM_WIDTH, CONV_DIM, SSM_HEADS)
IN_WIDTH = sum(IN_SPLITS)
ADAM_LR, ADAM_B1, ADAM_B2, ADAM_EPS, ADAM_WD, ADAM_STEP = 0.001, 0.9, 0.999, 1e-08, 0.01, 10

LANE = 128
ROWB = 512
PADF = ROWB - N_META
HEADW = 256
PC_Q, PC_KV, PC_KR, PC_Z, PC_XBC, PC_DT, PROJ_W = 0, 384, 640, 896, 1920, 3456, 3584
NEG = -1e30
N_DEV = 8
VMEM_LIMIT = 56 * 1024 * 1024
MESH = pl.DeviceIdType.MESH


def _cp(sem, vmem=VMEM_LIMIT, **kw):
    return pltpu.CompilerParams(dimension_semantics=sem, vmem_limit_bytes=vmem, **kw)


def _dot(a, b, dims=((1,), (0,))):
    return lax.dot_general(a, b, (dims, ((), ())), preferred_element_type=F32)


def _bdot(a, b, dims=((1,), (0,))):
    return _dot(a.astype(BF16), b.astype(BF16), dims)


NT = ((1,), (1,))


def _rms_fwd(x, w):
    r = lax.rsqrt(jnp.mean(x * x, axis=-1, keepdims=True) + EPS)
    return (x * r) * w


def _rms_bwd(x, w, dy):
    r = lax.rsqrt(jnp.mean(x * x, axis=-1, keepdims=True) + EPS)
    xh = x * r
    g = dy * w
    dx = r * (g - xh * jnp.mean(g * xh, axis=-1, keepdims=True))
    return dx, dy * xh


def _sigmoid(x):
    return 0.5 * jnp.tanh(0.5 * x) + 0.5


def _colsum8(x):
    t, c = x.shape
    return jnp.sum(x.reshape(t // 8, 8, c), axis=0)


def _rowspec(t, c, cb=0):
    return pl.BlockSpec((t, c), lambda i: (i, cb))


def _fullspec(shape):
    n = len(shape)
    return pl.BlockSpec(shape, lambda i: (0,) * n)


def _sds(shape, dt):
    return jax.ShapeDtypeStruct(shape, dt)


def _acc(ref, val):
    @pl.when(pl.program_id(0) == 0)
    def _():
        ref[...] = jnp.zeros_like(ref)

    ref[...] += val


def _xspec():
    return pl.BlockSpec((ROWB, D_MODEL), lambda i: (jnp.maximum(i - 1, 0), 0))


def _h_block(head_ref, x_ref):
    return jnp.where(pl.program_id(0) == 0, head_ref[...], x_ref[...])


def _norm_in_proj(head, x, g, w_all):
    lp = head.shape[0] + x.shape[0]

    def body(head_ref, x_ref, g_ref, w_ref, hn_ref, cq_ref, ckv_ref, kr_ref, z_ref, xbc_ref, dt_ref):
        hn = _rms_fwd(_h_block(head_ref, x_ref), g_ref[...]).astype(BF16)
        hn_ref[...] = hn
        p = _dot(hn, w_ref[...])
        cq_ref[...] = p[:, PC_Q:PC_KV]
        ckv_ref[...] = p[:, PC_KV:PC_KR]
        kr_ref[...] = p[:, PC_KR:PC_Z]
        z_ref[...] = p[:, PC_Z:PC_XBC]
        xbc_ref[...] = p[:, PC_XBC:PC_DT]
        dt_ref[...] = p[:, PC_DT:PROJ_W]

    widths = (Q_LORA, KV_LORA, HEADW, SSM_WIDTH, CONV_DIM, LANE)
    return pl.pallas_call(
        body, name="norm_in_proj", grid=(lp // ROWB,),
        in_specs=[_fullspec((ROWB, D_MODEL)), _xspec(), _fullspec((1, D_MODEL)), _fullspec((D_MODEL, PROJ_W))],
        out_specs=[_rowspec(ROWB, D_MODEL)] + [_rowspec(ROWB, w) for w in widths],
        out_shape=[_sds((lp, D_MODEL), BF16)] + [_sds((lp, w), F32) for w in widths],
        compiler_params=_cp(("arbitrary",)),
    )(head, x, g, w_all)


def _rope(x, cos, sa, sb):
    w = x.shape[1]
    return x * cos + pltpu.roll(x, w - 32, 1) * sa + pltpu.roll(x, 32, 1) * sb


def _rope_t(dy, cos, sa, sb):
    w = dy.shape[1]
    return dy * cos + pltpu.roll(dy * sa, 32, 1) + pltpu.roll(dy * sb, w - 32, 1)


def _tile8(t):
    return jnp.concatenate([t] * ATT_HEADS, axis=1)


def _qkv(cq, ckv, kr, cos, sa, sb, gq, gkv, wq, wk, wv):
    lp = cq.shape[0]
    qw = ATT_HEADS * HEADW

    def body(cq_ref, ckv_ref, kr_ref, cos_ref, sa_ref, sb_ref, gq_ref, gkv_ref, wq_ref, wk_ref, wv_ref,
             q_ref, k_ref, v_ref, cqn_ref, ckvn_ref):
        cos_, sa_, sb_ = cos_ref[...], sa_ref[...], sb_ref[...]
        cqn = _rms_fwd(cq_ref[...], gq_ref[...]).astype(BF16)
        ckvn = _rms_fwd(ckv_ref[...], gkv_ref[...]).astype(BF16)
        cqn_ref[...] = cqn
        ckvn_ref[...] = ckvn
        q = _dot(cqn, wq_ref[...])
        q_ref[...] = (_rope(q, _tile8(cos_), _tile8(sa_), _tile8(sb_)) * Q_PRESCALE).astype(BF16)
        k = _dot(ckvn, wk_ref[...]) + _tile8(_rope(kr_ref[...], cos_, sa_, sb_))
        k_ref[...] = k.astype(BF16)
        lanes = lax.broadcasted_iota(jnp.int32, (1, qw), 1)
        ones = ((lanes % HEADW) >= V_HEAD).astype(F32)
        v_ref[...] = (_dot(ckvn, wv_ref[...]) + ones).astype(BF16)

    return pl.pallas_call(
        body, name="qkv", grid=(lp // ROWB,),
        in_specs=[_rowspec(ROWB, Q_LORA), _rowspec(ROWB, KV_LORA), _rowspec(ROWB, HEADW)]
        + [_rowspec(ROWB, HEADW)] * 3
        + [_fullspec((1, Q_LORA)), _fullspec((1, KV_LORA)), _fullspec((Q_LORA, qw)), _fullspec((KV_LORA, qw)),
           _fullspec((KV_LORA, qw))],
        out_specs=[_rowspec(ROWB, qw), _rowspec(ROWB, qw), _rowspec(ROWB, qw),
                   _rowspec(ROWB, Q_LORA), _rowspec(ROWB, KV_LORA)],
        out_shape=[_sds((lp, qw), BF16), _sds((lp, qw), BF16), _sds((lp, qw), BF16),
                   _sds((lp, Q_LORA), BF16), _sds((lp, KV_LORA), BF16)],
        compiler_params=_cp(("arbitrary",)),
    )(cq, ckv, kr, cos, sa, sb, gq, gkv, wq, wk, wv)


ATT_SCALE = (QK_NOPE + QK_ROPE) ** -0.5
LOG2E = 1.4426950408889634
LN2 = 0.6931471805599453
Q_PRESCALE = ATT_SCALE * LOG2E
KVB = 512
META_KEYS = LANE
assert N_META <= META_KEYS


def _att_ok(qrow, krow):
    return (krow <= qrow) & ((krow >= PADF) | (qrow < PADF))


def _lanes(x, n):
    return x if n == 1 else jnp.concatenate([x] * n, axis=1)


def _pair_loop(lo, hi, tile, unrolls=(2,)):
    for u in tuple(unrolls) + (1,):
        n = jnp.maximum(hi - lo, 0)
        trips = n // u

        def many(t, c, u=u, lo=lo):
            for d in range(u):
                tile(lo + u * t + d)
            return c

        lax.fori_loop(0, trips, many, 0)
        lo = lo + trips * u


def _flash_fwd(q, k, v):
    lp = q.shape[0]
    nq = lp // ROWB

    def body(q_ref, k_ref, v_ref, o_ref, lse_ref, acc, m_s):
        i = pl.program_id(1)
        qb = q_ref[...]
        m_s[...] = jnp.full_like(m_s, NEG)
        acc[...] = jnp.zeros_like(acc)

        def tile(j, masked, off=None, nkeys=KVB):
            off = pl.multiple_of(j * KVB, KVB) if off is None else off
            kb = k_ref[pl.ds(off, nkeys), :]
            vb = v_ref[pl.ds(off, nkeys), :]
            s = _dot(qb, kb, NT)
            if masked:
                qrow = i * ROWB + lax.broadcasted_iota(jnp.int32, s.shape, 0)
                krow = off + lax.broadcasted_iota(jnp.int32, s.shape, 1)
                s = jnp.where(_att_ok(qrow, krow), s, NEG)
            m_prev = m_s[...]
            m_new = jnp.maximum(m_prev, jnp.max(s, axis=1, keepdims=True))
            alpha = jnp.exp2(m_prev - m_new)
            p = jnp.exp2(s - _lanes(m_new, nkeys // LANE))
            acc[...] = _lanes(alpha, 2) * acc[...] + _dot(p.astype(BF16), vb)
            m_s[...] = m_new

        def first_tile():
            tile(0, True, off=ROWB - META_KEYS, nkeys=META_KEYS)

        @pl.when(i == 0)
        def _():
            first_tile()

        @pl.when(i > 0)
        def _():
            first_tile()
            tile(i, True)

        _pair_loop(1, i, lambda j: tile(j, False), (16, 8, 4, 2))
        l = acc[:, V_HEAD:]
        o_ref[...] = (acc[:, :V_HEAD] / l).astype(BF16)
        lse_ref[0] = (m_s[...] + jnp.log2(l)).T[0:1, :]

    return pl.pallas_call(
        body, name="flash_fwd", grid=(ATT_HEADS, nq),
        in_specs=[pl.BlockSpec((ROWB, HEADW), lambda h, i: (i, h)),
                  pl.BlockSpec((lp, HEADW), lambda h, i: (0, h)),
                  pl.BlockSpec((lp, HEADW), lambda h, i: (0, h))],
        out_specs=[pl.BlockSpec((ROWB, V_HEAD), lambda h, i: (i, h)),
                   pl.BlockSpec((1, 1, ROWB), lambda h, i: (h, 0, i))],
        out_shape=[_sds((lp, ATT_HEADS * V_HEAD), BF16), _sds((ATT_HEADS, 1, lp), F32)],
        scratch_shapes=[pltpu.VMEM((ROWB, HEADW), F32), pltpu.VMEM((ROWB, LANE), F32)],
        compiler_params=_cp(("arbitrary", "arbitrary")),
    )(q, k, v)


def _silu(x):
    return x * _sigmoid(x)


CONV_ROWS = 64


def _conv_fwd(xbc, cw, cb):
    lp, c = xbc.shape
    t8 = ROWB // 8

    def body(x_ref, prev_ref, w_ref, b_ref, o_ref, buf):
        i = pl.program_id(0)
        buf[pl.ds(0, 8), :] = jnp.where(i > 0, prev_ref[...], 0.0)
        buf[pl.ds(8, ROWB), :] = x_ref[...]

        def strip(s, carry):
            cs = pl.ds(pl.multiple_of(s * LANE, LANE), LANE)
            w, b = w_ref[:, cs], b_ref[:, cs]
            for r0 in range(0, ROWB, CONV_ROWS):
                pre = b + sum(w[kk:kk + 1, :] * buf[pl.ds(8 - (CONV_K - 1) + kk + r0, CONV_ROWS), cs]
                              for kk in range(CONV_K))
                o_ref[pl.ds(r0, CONV_ROWS), cs] = _silu(pre)
            return carry

        lax.fori_loop(0, c // LANE, strip, 0)

    return pl.pallas_call(
        body, name="conv_fwd", grid=(lp // ROWB,),
        in_specs=[_rowspec(ROWB, c), pl.BlockSpec((8, c), lambda i: (jnp.maximum(i * t8 - 1, 0), 0)),
                  _fullspec((8, c)), _fullspec((1, c))],
        out_specs=_rowspec(ROWB, c), out_shape=_sds((lp, c), F32),
        scratch_shapes=[pltpu.VMEM((ROWB + 8, c), F32)],
        compiler_params=_cp(("arbitrary",)),
    )(xbc, xbc, cw, cb)


def _expand_mat():
    r = np.arange(LANE)[:, None]
    c = np.arange(SSM_WIDTH)[None, :]
    return jnp.asarray((c // SSM_HEAD_DIM == r).astype(np.float32))


def _tri_mat():
    i = np.arange(CHUNK)
    return jnp.asarray((i[:, None] >= i[None, :]).astype(np.float32))


def _x3(m, axis):
    return jnp.concatenate([m.astype(BF16)] * 3, axis=axis)


def _split3(x):
    hi = x.astype(BF16)
    r = x - hi.astype(F32)
    mid = r.astype(BF16)
    return hi, mid, (r - mid.astype(F32)).astype(BF16)


def _dot01_r(x, m3):
    return _dot(jnp.concatenate(_split3(x), axis=1), m3)


def _dot01_l(m3, x):
    return _dot(m3, jnp.concatenate(_split3(x), axis=0))


def _ssd_prep(dtr_ref, bias_ref, alog_ref, tri3, c, seq_rows):
    rows = c * CHUNK + lax.broadcasted_iota(jnp.int32, (CHUNK, LANE), 0)
    lanes = lax.broadcasted_iota(jnp.int32, (CHUNK, LANE), 1)
    valid = (rows >= PADF) & (rows < PADF + seq_rows) & (lanes < SSM_HEADS)
    dtr = dtr_ref[...] + bias_ref[...]
    sp = jnp.maximum(dtr, 0.0) + jnp.log(1.0 + jnp.exp(-jnp.abs(dtr)))
    dt = jnp.where(valid, sp, 0.0)
    a = -jnp.exp(alog_ref[...])
    acol = _dot01_l(tri3, dt * a)
    return dt, a, acol, valid, dtr


def _row16(v):
    return jnp.broadcast_to(v, (16, v.shape[1]))


def _ssd_fwd(xbc_act, dtr, dt_bias, a_log, seq_rows, gather=()):
    lp = xbc_act.shape[0]
    nc = lp // CHUNK
    gw = SSM_WIDTH // SSM_GROUPS
    hpg = SSM_HEADS // SSM_GROUPS

    na = len(gather)

    def body(x_ref, b_ref, c_ref, dtr_ref, bias_ref, alog_ref, tri_ref, tri3_ref, ex3_ref, *rest):
        gin, (y_ref, hp_ref), gout, h_s, sems = rest[:na], rest[na:na + 2], rest[na + 2:2 * na + 2], rest[2 * na + 2], rest[2 * na + 3:]
        c = pl.program_id(0)

        @pl.when(c == 0)
        def _():
            h_s[...] = jnp.zeros_like(h_s)

        if na:
            g_start, g_forward, g_finish = _gather_ops(gin, gout, *sems)
            pl.when(c == 0)(g_start)
            pl.when(c == nc // 2)(g_forward)

        ex3 = ex3_ref[...]
        dt, a, acol, _, _ = _ssd_prep(dtr_ref, bias_ref, alog_ref, tri3_ref[...], c, seq_rows)
        arow = acol.T
        dtrow = dt.T
        alast = acol[CHUNK - 1:CHUNK, :]
        e_all = _dot01_r(jnp.exp(acol), ex3)
        wx_all = _dot01_r(jnp.exp(alast - acol) * dt, ex3)
        dec_all = _dot01_r(_row16(jnp.exp(alast)), ex3)[0:1, :]
        causal = tri_ref[...] > 0.5
        hp_ref[0] = h_s[...]
        for g in range(SSM_GROUPS):
            gs = slice(g * gw, (g + 1) * gw)
            bg = b_ref[:, g * SSM_STATE:(g + 1) * SSM_STATE]
            cg = c_ref[:, g * SSM_STATE:(g + 1) * SSM_STATE].astype(BF16)
            xg = x_ref[:, gs]
            hg = h_s[:, gs]
            gm = _bdot(cg, bg, NT)
            y_off = _bdot(cg, hg) * e_all[:, gs]
            for r in range(hpg):
                hd = g * hpg + r
                seg = acol[:, hd:hd + 1] - arow[hd:hd + 1, :]
                lm = jnp.where(causal, jnp.exp(jnp.where(causal, seg, 0.0)), 0.0)
                w = gm * lm * dtrow[hd:hd + 1, :]
                cs = slice(r * SSM_HEAD_DIM, (r + 1) * SSM_HEAD_DIM)
                y_ref[:, pl.ds(hd * SSM_HEAD_DIM, SSM_HEAD_DIM)] = _bdot(w, xg[:, cs]) + y_off[:, cs]
            st = _bdot(bg.T, xg * wx_all[:, gs])
            h_s[:, gs] = hg * dec_all[:, gs] + st

        if na:
            pl.when(c == nc - 1)(g_finish)

    xs_spec = pl.BlockSpec((CHUNK, SSM_WIDTH), lambda c: (c, 0))
    b_spec = pl.BlockSpec((CHUNK, 2 * SSM_STATE), lambda c: (c, SSM_WIDTH // (2 * SSM_STATE)))
    c_spec = pl.BlockSpec((CHUNK, 2 * SSM_STATE), lambda c: (c, SSM_WIDTH // (2 * SSM_STATE) + 1))
    y, hprev, *gathered = pl.pallas_call(
        body, name="ssd_fwd", grid=(nc,),
        in_specs=[xs_spec, b_spec, c_spec, pl.BlockSpec((CHUNK, LANE), lambda c: (c, 0)),
                  _fullspec((1, LANE)), _fullspec((1, LANE)), _fullspec((CHUNK, CHUNK)), _fullspec((CHUNK, 3 * CHUNK)),
                  _fullspec((3 * LANE, SSM_WIDTH))] + [ANY] * na,
        out_specs=[xs_spec, pl.BlockSpec((1, SSM_STATE, SSM_WIDTH), lambda c: (c, 0, 0))] + [ANY] * na,
        out_shape=[_sds((lp, SSM_WIDTH), F32), _sds((nc, SSM_STATE, SSM_WIDTH), F32)]
        + [_sds((N_DEV,) + s.shape, s.dtype) for s in gather],
        scratch_shapes=[pltpu.VMEM((SSM_STATE, SSM_WIDTH), F32)] + (_gather_scratch(na) if na else []),
        compiler_params=_cp(("arbitrary",)),
    )(xbc_act, xbc_act, xbc_act, dtr, dt_bias, a_log, _tri_mat(), _x3(_tri_mat(), 1), _x3(_expand_mat(), 0), *gather)
    return y, hprev, gathered


def _group_mean(x):
    gw = SSM_WIDTH // SSM_GROUPS
    parts = [jnp.broadcast_to(jnp.mean(x[:, g * gw:(g + 1) * gw], axis=-1, keepdims=True), (x.shape[0], gw))
             for g in range(SSM_GROUPS)]
    return jnp.concatenate(parts, axis=1)


def _ssd_post(y, xbc_act, z, dskip, gnorm):
    lp = y.shape[0]

    def body(y_ref, x_ref, z_ref, d_ref, g_ref, o_ref):
        z_ = z_ref[...]
        gt = (y_ref[...] + d_ref[...] * x_ref[...]) * _silu(z_)
        r = lax.rsqrt(_group_mean(gt * gt) + EPS)
        o_ref[...] = ((gt * r) * g_ref[...]).astype(BF16)

    return pl.pallas_call(
        body, name="ssd_post", grid=(lp // ROWB,),
        in_specs=[_rowspec(ROWB, SSM_WIDTH)] * 3 + [_fullspec((1, SSM_WIDTH))] * 2,
        out_specs=_rowspec(ROWB, SSM_WIDTH), out_shape=_sds((lp, SSM_WIDTH), BF16),
        compiler_params=_cp(("arbitrary",)),
    )(y, xbc_act, z, dskip, gnorm)


def _out_proj(att, ssm, head, x, w_out, g_post):
    lp = att.shape[0]

    def body(a_ref, s_ref, head_ref, x_ref, w_ref, g_ref, mix_ref, h1_ref):
        mix = _dot(a_ref[...], w_ref[pl.ds(0, 1024), :]) + _dot(s_ref[...], w_ref[pl.ds(1024, 1024), :])
        mix_ref[...] = mix
        h1_ref[...] = _h_block(head_ref, x_ref) + _rms_fwd(mix, g_ref[...])

    return pl.pallas_call(
        body, name="out_proj", grid=(lp // ROWB,),
        in_specs=[_rowspec(ROWB, 1024)] * 2 + [_fullspec((ROWB, D_MODEL)), _xspec(), _fullspec((2048, D_MODEL)),
                                               _fullspec((1, D_MODEL))],
        out_specs=[_rowspec(ROWB, D_MODEL)] * 2, out_shape=[_sds((lp, D_MODEL), F32)] * 2,
        compiler_params=_cp(("arbitrary",)),
    )(att, ssm, head, x, w_out, g_post)


def _resident(w_hbm, w_vmem, sem):
    @pl.when(pl.program_id(0) == 0)
    def _():
        cp = pltpu.make_async_copy(w_hbm, w_vmem, sem)
        cp.start()
        cp.wait()


ANY = pl.BlockSpec(memory_space=pl.ANY)


def _mlp_fwd(h1, tgt, w_up, w_down, g_pre, g_post, seq_rows):
    lp = h1.shape[0]

    def body(h1_ref, t_ref, wu_hbm, wd_hbm, gpre_ref, gpost_ref, hn2_ref, u_ref, a_ref, f_ref, dh2_ref, loss_ref,
             wu, wd, sems):
        _resident(wu_hbm, wu, sems.at[0])
        _resident(wd_hbm, wd, sems.at[1])
        i = pl.program_id(0)
        h1_ = h1_ref[...]
        hn2 = _rms_fwd(h1_, gpre_ref[...]).astype(BF16)
        hn2_ref[...] = hn2
        u = jnp.maximum(_dot(hn2, wu[...]), 0.0)
        u_ref[...] = u.astype(BF16)
        a = (u * u).astype(BF16)
        a_ref[...] = a
        f = _dot(a, wd[...])
        f_ref[...] = f
        h2 = h1_ + _rms_fwd(f, gpost_ref[...])
        rows = i * ROWB + lax.broadcasted_iota(jnp.int32, (ROWB, 1), 0)
        real = (rows >= PADF + N_META) & (rows < PADF + seq_rows)
        err = jnp.where(real, h2 - t_ref[...], 0.0)
        dh2_ref[...] = err * (1.0 / D_MODEL)
        _acc(loss_ref, _colsum8(err * err))

    return pl.pallas_call(
        body, name="mlp_fwd", grid=(lp // ROWB,),
        in_specs=[_rowspec(ROWB, D_MODEL), _xspec()] + [ANY, ANY] + [_fullspec((1, D_MODEL))] * 2,
        out_specs=[_rowspec(ROWB, D_MODEL), _rowspec(ROWB, D_FF), _rowspec(ROWB, D_FF)] + [_rowspec(ROWB, D_MODEL)] * 2
        + [_fullspec((8, D_MODEL))],
        out_shape=[_sds((lp, D_MODEL), BF16), _sds((lp, D_FF), BF16), _sds((lp, D_FF), BF16), _sds((lp, D_MODEL), F32),
                   _sds((lp, D_MODEL), F32), _sds((8, D_MODEL), F32)],
        scratch_shapes=[pltpu.VMEM((D_MODEL, D_FF), BF16), pltpu.VMEM((D_FF, D_MODEL), BF16), pltpu.SemaphoreType.DMA((2,))],
        compiler_params=_cp(("arbitrary",)),
    )(h1, tgt, w_up, w_down, g_pre, g_post)


def _pad_cols(w, width):
    return jnp.pad(w, ((0, 0), (0, width - w.shape[1])))


def _layout_weights(w_in, w_q_up, w_kv_up):
    o = np.cumsum((0,) + IN_SPLITS)
    pieces = [w_in[:, o[k]:o[k + 1]] for k in range(6)]
    kr = jnp.pad(pieces[2], ((0, 0), (QK_NOPE, HEADW - QK_NOPE - QK_ROPE)))
    w_all = jnp.concatenate([pieces[0], pieces[1], kr, pieces[3], pieces[4], _pad_cols(pieces[5], LANE)], axis=1)
    wq = jnp.pad(w_q_up.reshape(Q_LORA, ATT_HEADS, QK_NOPE + QK_ROPE), ((0, 0), (0, 0), (0, HEADW - QK_NOPE - QK_ROPE)))
    wkv = w_kv_up.reshape(KV_LORA, ATT_HEADS, QK_NOPE + V_HEAD)
    wk = jnp.pad(wkv[:, :, :QK_NOPE], ((0, 0), (0, 0), (0, HEADW - QK_NOPE)))
    wv = wkv[:, :, QK_NOPE:]
    wvp = jnp.pad(wv, ((0, 0), (0, 0), (0, HEADW - V_HEAD)))
    return (w_all, wq.reshape(Q_LORA, -1), wk.reshape(KV_LORA, -1), wv.reshape(KV_LORA, -1),
            wvp.reshape(KV_LORA, -1))


def _rope_tables(lp):
    pos = jnp.maximum(jnp.arange(lp, dtype=jnp.int32) - PADF, 0).astype(F32)
    inv_freq = ROPE_THETA ** (-jnp.arange(0, QK_ROPE, 2, dtype=F32) / QK_ROPE)
    ang = pos[:, None] * inv_freq[None, :]
    cos, sin = jnp.cos(ang), jnp.sin(ang)
    one, zero = jnp.ones((lp, QK_NOPE), F32), jnp.zeros((lp, QK_NOPE), F32)
    z32, z64 = jnp.zeros((lp, 32), F32), jnp.zeros((lp, 64), F32)
    cos_t = jnp.concatenate([one, cos, cos, jnp.ones((lp, 64), F32)], axis=1)
    sa = jnp.concatenate([zero, -sin, z32, z64], axis=1)
    sb = jnp.concatenate([zero, z32, sin, z64], axis=1)
    return cos_t, sa, sb


def _row1(v, width=None):
    v = v.reshape(1, -1).astype(F32)
    return v if width is None else _pad_cols(v, width)


LATE = ("w_out", "w_mlp_up", "w_mlp_down")


def _local_forward(head, x, tgt, p, late=None):
    assert head.shape[0] == ROWB and x.shape[0] % ROWB == 0
    lp = ROWB + x.shape[0]
    seq_rows = N_META + x.shape[0]
    f = {"seq_rows": seq_rows}
    w_all, wq, wk, wv, wvp = _layout_weights(p["w_in"], p["w_q_up"], p["w_kv_up"])
    f.update(w_all=w_all, wq=wq, wk=wk, wv=wv)
    f["hn"], cq, ckv, kr, f["z"], f["xbc"], f["dtr"] = _norm_in_proj(head, x, _row1(p["norm_mix_pre"]), w_all)
    f.update(cq=cq, ckv=ckv)
    f["rope"] = _rope_tables(lp)
    f["q"], f["k"], f["v"], f["cqn"], f["ckvn"] = _qkv(cq, ckv, kr, *f["rope"], _row1(p["q_a_norm"]),
                                                   _row1(p["kv_a_norm"]), wq, wk, wvp)
    f["att"], f["lse"] = _flash_fwd(f["q"], f["k"], f["v"])
    f["cw"] = jnp.pad(p["conv_w"].astype(F32), ((0, 8 - CONV_K), (0, 0)))
    f["xact"] = _conv_fwd(f["xbc"], f["cw"], _row1(p["conv_b"]))
    f["dt_bias"], f["a_log"] = _row1(p["dt_bias"], LANE), _row1(p["a_log"], LANE)
    f["y"], f["hprev"], gathered = _ssd_fwd(f["xact"], f["dtr"], f["dt_bias"], f["a_log"], seq_rows,
                                            gather=[late[n] for n in LATE] if late else ())
    p = {**p, **{n: _from_shards(n, s) for n, s in zip(LATE, gathered)}}
    f["p"] = p
    f["dskip"] = jnp.repeat(p["d_skip"].reshape(-1).astype(F32), SSM_HEAD_DIM).reshape(1, SSM_WIDTH)
    f["ssm"] = _ssd_post(f["y"], f["xact"], f["z"], f["dskip"], _row1(p["ssm_norm"]))
    f["mix"], f["h1"] = _out_proj(f["att"], f["ssm"], head, x, p["w_out"], _row1(p["norm_mix_post"]))
    f["hn2"], f["u"], f["a"], f["f"], f["dh2"], loss8 = _mlp_fwd(
        f["h1"], tgt, p["w_mlp_up"], p["w_mlp_down"], _row1(p["norm_mlp_pre"]), _row1(p["norm_mlp_post"]), seq_rows)
    f["loss"] = 0.5 * jnp.sum(loss8) / D_MODEL
    return f


MLPB = 256


def _mlp_bwd(dh2, f, h1, u, w_up, w_down, g_pre, g_post):
    lp = h1.shape[0]

    def body(dh2_ref, f_ref, h1_ref, u_ref, wu_hbm, wd_hbm, gpre_ref, gpost_ref,
             dh1_ref, du_ref, df_ref, dgpre_ref, dgpost_ref, wu, wd, sems):
        _resident(wu_hbm, wu, sems.at[0])
        _resident(wd_hbm, wd, sems.at[1])
        dh2_ = dh2_ref[...]
        df, dgp = _rms_bwd(f_ref[...], gpost_ref[...], dh2_)
        dfb = df.astype(BF16)
        df_ref[...] = dfb
        da = _dot(dfb, wd[...], NT)
        du = (da * (2.0 * u_ref[...].astype(F32))).astype(BF16)
        du_ref[...] = du
        dhn2 = _dot(du, wu[...], NT)
        dx, dgq = _rms_bwd(h1_ref[...], gpre_ref[...], dhn2)
        dh1_ref[...] = dh2_ + dx
        _acc(dgpre_ref, _colsum8(dgq))
        _acc(dgpost_ref, _colsum8(dgp))

    return pl.pallas_call(
        body, name="mlp_bwd", grid=(lp // MLPB,),
        in_specs=[_rowspec(MLPB, D_MODEL)] * 3 + [_rowspec(MLPB, D_FF), ANY, ANY] + [_fullspec((1, D_MODEL))] * 2,
        out_specs=[_rowspec(MLPB, D_MODEL), _rowspec(MLPB, D_FF), _rowspec(MLPB, D_MODEL),
                   _fullspec((8, D_MODEL)), _fullspec((8, D_MODEL))],
        out_shape=[_sds((lp, D_MODEL), F32), _sds((lp, D_FF), BF16), _sds((lp, D_MODEL), BF16),
                   _sds((8, D_MODEL), F32), _sds((8, D_MODEL), F32)],
        scratch_shapes=[pltpu.VMEM((D_MODEL, D_FF), BF16), pltpu.VMEM((D_FF, D_MODEL), BF16), pltpu.SemaphoreType.DMA((2,))],
        compiler_params=_cp(("arbitrary",)),
    )(dh2, f, h1, u, w_up, w_down, g_pre, g_post)


def _out_bwd(dh1, mix, att, w_out, g_post):
    lp = dh1.shape[0]

    def body(dh1_ref, mix_ref, att_ref, w_ref, g_ref, dmix_ref, datt_ref, dssm_ref, dg_ref, dl_ref):
        dmix, dg = _rms_bwd(mix_ref[...], g_ref[...], dh1_ref[...])
        dmb = dmix.astype(BF16)
        dmix_ref[...] = dmb
        datt = _dot(dmb, w_ref[pl.ds(0, 1024), :], NT).astype(BF16)
        datt_ref[...] = datt
        dssm_ref[...] = _dot(dmb, w_ref[pl.ds(1024, 1024), :], NT)
        _acc(dg_ref, _colsum8(dg))
        prod = datt.astype(F32) * att_ref[...].astype(F32)
        for hh in range(ATT_HEADS):
            d = jnp.sum(prod[:, hh * V_HEAD:(hh + 1) * V_HEAD], axis=1, keepdims=True)
            dl_ref[hh] = jnp.broadcast_to(d, (ROWB, LANE)).T[0:1, :]

    return pl.pallas_call(
        body, name="out_bwd", grid=(lp // ROWB,),
        in_specs=[_rowspec(ROWB, D_MODEL)] * 3 + [_fullspec((2048, D_MODEL)), _fullspec((1, D_MODEL))],
        out_specs=[_rowspec(ROWB, D_MODEL)] * 3 + [_fullspec((8, D_MODEL)),
                                                   pl.BlockSpec((ATT_HEADS, 1, ROWB), lambda i: (0, 0, i))],
        out_shape=[_sds((lp, D_MODEL), BF16), _sds((lp, 1024), BF16), _sds((lp, 1024), F32), _sds((8, D_MODEL), F32),
                   _sds((ATT_HEADS, 1, lp), F32)],
        compiler_params=_cp(("arbitrary",)),
    )(dh1, mix, att, w_out, g_post)


def _ssd_post_bwd(dssm, y, xact, z, dskip, gnorm):
    lp = y.shape[0]

    def body(do_ref, y_ref, x_ref, z_ref, d_ref, g_ref, dy_ref, dz_ref, dg_ref, dd_ref):
        z_, x_ = z_ref[...], x_ref[...]
        sg = _sigmoid(z_)
        sz = z_ * sg
        y2 = y_ref[...] + d_ref[...] * x_
        gt = y2 * sz
        r = lax.rsqrt(_group_mean(gt * gt) + EPS)
        gh = gt * r
        do = do_ref[...]
        dgh = do * g_ref[...]
        dgt = r * (dgh - gh * _group_mean(dgh * gh))
        dy2 = dgt * sz
        dy_ref[...] = dy2
        dz_ref[...] = (dgt * y2 * (sg * (1.0 + z_ * (1.0 - sg)))).astype(BF16)
        _acc(dg_ref, _colsum8(do * gh))
        _acc(dd_ref, _colsum8(dy2 * x_))

    return pl.pallas_call(
        body, name="ssd_post_bwd", grid=(lp // ROWB,),
        in_specs=[_rowspec(ROWB, SSM_WIDTH)] * 4 + [_fullspec((1, SSM_WIDTH))] * 2,
        out_specs=[_rowspec(ROWB, SSM_WIDTH)] * 2 + [_fullspec((8, SSM_WIDTH))] * 2,
        out_shape=[_sds((lp, SSM_WIDTH), F32), _sds((lp, SSM_WIDTH), BF16), _sds((8, SSM_WIDTH), F32), _sds((8, SSM_WIDTH), F32)],
        compiler_params=_cp(("arbitrary",)),
    )(dssm, y, xact, z, dskip, gnorm)


def _ssd_bwd(dy, xact, dtr, hprev, dt_bias, a_log, dskip, seq_rows, exchange=()):
    lp = xact.shape[0]
    nc = lp // CHUNK
    gw = SSM_WIDTH // SSM_GROUPS
    hpg = SSM_HEADS // SSM_GROUPS
    nb = SSM_WIDTH // (2 * SSM_STATE)
    na = len(exchange)

    def body(dy_ref, x_ref, b_ref, c_ref, dtr_ref, hp_ref, bias_ref, alog_ref, dsk_ref, tri_ref, tri3_ref, trit3_ref,
             ex3_ref, ext3_ref, *rest):
        xin, (dact_ref, ddtr_ref, da_ref, dbias_ref), xout = rest[:na], rest[na:na + 4], rest[na + 4:2 * na + 4]
        dh_s, sems = rest[2 * na + 4], rest[2 * na + 5:]
        step = pl.program_id(0)
        c = nc - 1 - step

        @pl.when(step == 0)
        def _():
            dh_s[...] = jnp.zeros_like(dh_s)

        if na:
            x_start, x_finish = _exchange_ops(xin, xout, *sems)
            pl.when(step == 0)(x_start)

        tri = tri_ref[...]
        ex3 = ex3_ref[...]
        dt, a, acol, valid, dtr_ = _ssd_prep(dtr_ref, bias_ref, alog_ref, tri3_ref[...], c, seq_rows)
        arow = acol.T
        dtrow = dt.T
        alast = acol[CHUNK - 1:CHUNK, :]
        e_all = _dot01_r(jnp.exp(acol), ex3)
        wgt0 = jnp.exp(alast - acol)
        wgt = wgt0 * dt
        wx_all = _dot01_r(wgt, ex3)
        elast = jnp.exp(alast)
        dec_all = _dot01_r(_row16(elast), ex3)[0:1, :]
        causal = tri > 0.5
        upper = tri.T > 0.5
        lane_id = lax.broadcasted_iota(jnp.int32, (1, LANE), 1)
        sub_id = lax.broadcasted_iota(jnp.int32, (CHUNK, 1), 0)
        dacol = jnp.zeros((CHUNK, LANE), F32)
        darowf = jnp.zeros((CHUNK, LANE), F32)
        ddtrowf = jnp.zeros((CHUNK, LANE), F32)
        dwgt = jnp.zeros((CHUNK, LANE), F32)
        delast = jnp.zeros((1, LANE), F32)
        for g in range(SSM_GROUPS):
            gs = slice(g * gw, (g + 1) * gw)
            ext3_g = ext3_ref[g]
            bg = b_ref[:, g * SSM_STATE:(g + 1) * SSM_STATE]
            cg = c_ref[:, g * SSM_STATE:(g + 1) * SSM_STATE]
            bgb, cgb = bg.astype(BF16), cg.astype(BF16)
            xg = x_ref[:, gs]
            dyg = dy_ref[:, gs]
            hg = hp_ref[0, :, gs]
            dhg = dh_s[:, gs]
            hgb, dhgb = hg.astype(BF16), dhg.astype(BF16)
            gm = _dot(cgb, bgb, NT)
            gmt = _dot(bgb, cgb, NT)
            y_off = _dot(cgb, hgb) * e_all[:, gs]
            dy0 = (dyg * e_all[:, gs]).astype(BF16)
            dcg = _dot(dy0, hgb, NT)
            dh_in = _dot(cg.T.astype(BF16), dy0) + dhg * dec_all[:, gs]
            dacol = dacol + _dot01_r(dyg * y_off, ext3_g)
            xw = xg * wx_all[:, gs]
            dxw = _dot(bgb, dhgb)
            dx_state = dxw * wx_all[:, gs]
            dwgt = dwgt + _dot01_r(dxw * xg, ext3_g)
            dbt = _dot(dhgb, xw.astype(BF16), NT)
            hh = _colsum8(dhg * hg)
            hh16 = jnp.concatenate([hh, jnp.zeros_like(hh)], axis=0)
            delast = delast + jnp.sum(_dot01_r(hh16, ext3_g), axis=0, keepdims=True)
            dgm = jnp.zeros((CHUNK, CHUNK), F32)
            for r in range(hpg):
                hd = g * hpg + r
                cs = slice(r * SSM_HEAD_DIM, (r + 1) * SSM_HEAD_DIM)
                acol_r, arow_r = acol[:, hd:hd + 1], arow[hd:hd + 1, :]
                dtrow_r, dtcol_r = dtrow[hd:hd + 1, :], dt[:, hd:hd + 1]
                lm = jnp.where(causal, jnp.exp(jnp.where(causal, acol_r - arow_r, 0.0)), 0.0)
                lmt = jnp.where(upper, jnp.exp(jnp.where(upper, arow_r - acol_r, 0.0)), 0.0)
                wt = gmt * lmt * dtcol_r
                dy_r = dyg[:, cs].astype(BF16)
                dx_r = _dot(wt.astype(BF16), dy_r)
                dw = _dot(dy_r, xg[:, cs].astype(BF16), NT)
                t1 = dw * lm
                dgm = dgm + t1 * dtrow_r
                q1 = t1 * gm
                m = q1 * dtrow_r
                dacol = dacol + jnp.sum(m, axis=1, keepdims=True) * (lane_id == hd).astype(F32)
                darowf = darowf - (sub_id == hd).astype(F32) * jnp.sum(m, axis=0, keepdims=True)
                ddtrowf = ddtrowf + (sub_id == hd).astype(F32) * jnp.sum(q1, axis=0, keepdims=True)
                dact_ref[:, pl.ds(hd * SSM_HEAD_DIM, SSM_HEAD_DIM)] = (
                    dx_r + dx_state[:, cs] + dyg[:, cs] * dsk_ref[:, pl.ds(hd * SSM_HEAD_DIM, SSM_HEAD_DIM)])
            dgmb = dgm.astype(BF16)
            dact_ref[:, pl.ds(SSM_WIDTH + g * SSM_STATE, SSM_STATE)] = dbt.T + _dot(dgm.T.astype(BF16), cgb)
            dact_ref[:, pl.ds(SSM_WIDTH + 2 * SSM_STATE + g * SSM_STATE, SSM_STATE)] = dcg + _dot(dgmb, bgb)
            dh_s[:, gs] = dh_in
        t = dwgt * wgt
        dalast = jnp.sum(t, axis=0, keepdims=True) + delast * elast
        dacol_tot = dacol - t + darowf.T + (sub_id == CHUNK - 1).astype(F32) * dalast
        dda = _dot01_l(trit3_ref[...], dacol_tot)
        ddt = dwgt * wgt0 + ddtrowf.T + dda * a
        ddtr = jnp.where(valid, ddt * _sigmoid(dtr_), 0.0)
        ddtr_ref[...] = ddtr
        _acc(da_ref, _colsum8(dda * dt) * a)
        _acc(dbias_ref, _colsum8(ddtr))
        if na:
            pl.when(step == nc - 1)(x_finish)

    rev = lambda c: nc - 1 - c
    xs_spec = pl.BlockSpec((CHUNK, SSM_WIDTH), lambda c: (rev(c), 0))
    dact, ddtr, da8, dbias8, *received = pl.pallas_call(
        body, name="ssd_bwd", grid=(nc,),
        in_specs=[xs_spec, xs_spec,
                  pl.BlockSpec((CHUNK, 2 * SSM_STATE), lambda c: (rev(c), nb)),
                  pl.BlockSpec((CHUNK, 2 * SSM_STATE), lambda c: (rev(c), nb + 1)),
                  pl.BlockSpec((CHUNK, LANE), lambda c: (rev(c), 0)),
                  pl.BlockSpec((1, SSM_STATE, SSM_WIDTH), lambda c: (rev(c), 0, 0)),
                  _fullspec((1, LANE)), _fullspec((1, LANE)), _fullspec((1, SSM_WIDTH)),
                  _fullspec((CHUNK, CHUNK)), _fullspec((CHUNK, 3 * CHUNK)), _fullspec((CHUNK, 3 * CHUNK)),
                  _fullspec((3 * LANE, SSM_WIDTH)), _fullspec((SSM_GROUPS, 3 * gw, LANE))] + [ANY] * na,
        out_specs=[pl.BlockSpec((CHUNK, CONV_DIM), lambda c: (rev(c), 0)), pl.BlockSpec((CHUNK, LANE), lambda c: (rev(c), 0)),
                   _fullspec((8, LANE)), _fullspec((8, LANE))] + [ANY] * na,
        out_shape=[_sds((lp, CONV_DIM), F32), _sds((lp, LANE), F32), _sds((8, LANE), F32), _sds((8, LANE), F32)]
        + [_sds(e.shape, e.dtype) for e in exchange],
        scratch_shapes=[pltpu.VMEM((SSM_STATE, SSM_WIDTH), F32)] + (_gather_scratch(na) if na else []),
        compiler_params=_cp(("arbitrary",)),
    )(dy, xact, xact, xact, dtr, hprev, dt_bias, a_log, dskip, _tri_mat(), _x3(_tri_mat(), 1), _x3(_tri_mat().T, 1),
      _x3(_expand_mat(), 0), jnp.stack([_x3(_expand_mat().T[g * gw:(g + 1) * gw], 0) for g in range(SSM_GROUPS)]),
      *exchange)
    return dact, ddtr, da8, dbias8, received


def _conv_bwd(dact, xbc, cw, cb):
    lp, c = xbc.shape
    t8 = ROWB // 8
    nb = lp // ROWB

    def body(d_ref, dnext_ref, x_ref, prev_ref, next_ref, w_ref, b_ref, dx_ref, dw_ref, db_ref, xb, dp):
        i = pl.program_id(0)
        last = i == nb - 1
        xb[pl.ds(0, 8), :] = jnp.where(i > 0, prev_ref[...], 0.0)
        xb[pl.ds(8, ROWB), :] = x_ref[...]
        xb[pl.ds(8 + ROWB, 8), :] = jnp.where(last, 0.0, next_ref[...])

        @pl.when(i == 0)
        def _():
            dw_ref[...] = jnp.zeros_like(dw_ref)
            db_ref[...] = jnp.zeros_like(db_ref)

        sub = lax.broadcasted_iota(jnp.int32, (8, 1), 0)
        x0 = 8 - (CONV_K - 1)

        def strip(s, carry):
            cs = pl.ds(pl.multiple_of(s * LANE, LANE), LANE)
            w, b = w_ref[:, cs], b_ref[:, cs]

            def dpre_rows(r0, n, d):
                xs = [xb[pl.ds(x0 + kk + r0, n), cs] for kk in range(CONV_K)]
                pre = b + sum(w[kk:kk + 1, :] * xs[kk] for kk in range(CONV_K))
                sg = _sigmoid(pre)
                return d * (sg * (1.0 + pre * (1.0 - sg))), xs

            dws = [jnp.zeros((8, LANE), F32) for _ in range(CONV_K)]
            dbs = jnp.zeros((8, LANE), F32)
            for r0 in range(0, ROWB, CONV_ROWS):
                dpre, xs = dpre_rows(r0, CONV_ROWS, d_ref[pl.ds(r0, CONV_ROWS), cs])
                dp[pl.ds(r0, CONV_ROWS), cs] = dpre
                dbs = dbs + _colsum8(dpre)
                for kk in range(CONV_K):
                    dws[kk] = dws[kk] + _colsum8(dpre * xs[kk])
            dp[pl.ds(ROWB, 8), cs] = dpre_rows(ROWB, 8, jnp.where(last, 0.0, dnext_ref[:, cs]))[0]
            dwv = sum(jnp.where(sub == kk, jnp.sum(dws[kk], axis=0, keepdims=True), 0.0) for kk in range(CONV_K))
            dw_ref[:, cs] += dwv
            db_ref[:, cs] += dbs
            for r0 in range(0, ROWB, CONV_ROWS):
                dx = sum(w[kk:kk + 1, :] * dp[pl.ds(CONV_K - 1 - kk + r0, CONV_ROWS), cs] for kk in range(CONV_K))
                dx_ref[pl.ds(r0, CONV_ROWS), cs] = dx.astype(BF16)
            return carry

        lax.fori_loop(0, c // LANE, strip, 0)

    nxt = lambda i: (jnp.minimum((i + 1) * t8, lp // 8 - 1), 0)
    prv = lambda i: (jnp.maximum(i * t8 - 1, 0), 0)
    return pl.pallas_call(
        body, name="conv_bwd", grid=(nb,),
        in_specs=[_rowspec(ROWB, c), pl.BlockSpec((8, c), nxt), _rowspec(ROWB, c), pl.BlockSpec((8, c), prv),
                  pl.BlockSpec((8, c), nxt), _fullspec((8, c)), _fullspec((1, c))],
        out_specs=[_rowspec(ROWB, c), _fullspec((8, c)), _fullspec((8, c))],
        out_shape=[_sds((lp, c), BF16), _sds((8, c), F32), _sds((8, c), F32)],
        scratch_shapes=[pltpu.VMEM((ROWB + 16, c), F32), pltpu.VMEM((ROWB + 8, c), F32)],
        compiler_params=_cp(("arbitrary",)),
    )(dact, dact, xbc, xbc, xbc, cw, cb)


def _flash_bwd(q, k, v, datt, lse_row, delta_row, cos, sa, sb):
    lp = q.shape[0]
    nk = lp // ROWB

    def body(k_ref, v_ref, q_ref, do_ref, lse_ref, dl_ref, cos_ref, sa_ref, sb_ref, dq_ref, dk_ref, dv_ref,
             dq_acc, dk_acc, dv_acc):
        j = pl.program_id(1)

        @pl.when(j == 0)
        def _():
            dq_acc[...] = jnp.zeros_like(dq_acc)

        dk_acc[...] = jnp.zeros_like(dk_acc)
        dv_acc[...] = jnp.zeros_like(dv_acc)

        def tile(i, masked, key0=0, nkeys=ROWB):
            keys = pl.ds(key0, nkeys)
            kb, vb = k_ref[keys, :], v_ref[keys, :]
            off = pl.multiple_of(i * ROWB, ROWB)
            qb = q_ref[pl.ds(off, ROWB), :]
            dob = do_ref[pl.ds(off, ROWB), :]
            lse_r = lse_ref[0, :, pl.ds(off, ROWB)]
            dl_r = dl_ref[0, :, pl.ds(off, ROWB)]
            st = _dot(kb, qb, NT)
            if masked:
                krow = j * ROWB + key0 + lax.broadcasted_iota(jnp.int32, st.shape, 0)
                qrow = i * ROWB + lax.broadcasted_iota(jnp.int32, st.shape, 1)
                st = jnp.where(_att_ok(qrow, krow), st, NEG)
            pt = jnp.exp2(st - lse_r)
            dv_acc[keys, :] += _dot(pt.astype(BF16), dob)
            dpt = _dot(vb, dob, NT)
            dst = (pt * (dpt - dl_r)).astype(BF16)
            dk_acc[keys, :] += _dot(dst, qb)
            dq_acc[pl.ds(off, ROWB), :] += _dot(dst, kb, ((0,), (0,)))

        @pl.when(j == 0)
        def _():
            _pair_loop(0, nk, lambda i: tile(i, True, ROWB - META_KEYS, META_KEYS), (4, 2))

        @pl.when((j > 0) & (j < nk - 1))
        def _():
            tile(j, True)
            tile(j + 1, False)
            _pair_loop(j + 2, nk, lambda i: tile(i, False), (4, 2))

        @pl.when(j == nk - 1)
        def _():
            tile(j, True)

        dk_ref[...] = (dk_acc[...] * LN2).astype(BF16)
        dv_ref[...] = dv_acc[...].astype(BF16)
        dq = dq_acc[pl.ds(pl.multiple_of(j * ROWB, ROWB), ROWB), :] * ATT_SCALE
        dq_ref[...] = _rope_t(dq, cos_ref[...], sa_ref[...], sb_ref[...]).astype(BF16)

    stat = pl.BlockSpec((1, 1, lp), lambda h, j: (h, 0, 0))
    blk = pl.BlockSpec((ROWB, HEADW), lambda h, j: (j, h))
    tab = pl.BlockSpec((ROWB, HEADW), lambda h, j: (j, 0))
    return pl.pallas_call(
        body, name="flash_bwd", grid=(ATT_HEADS, nk),
        in_specs=[blk, pl.BlockSpec((ROWB, V_HEAD), lambda h, j: (j, 2 * h)),
                  pl.BlockSpec((lp, HEADW), lambda h, j: (0, h)), pl.BlockSpec((lp, V_HEAD), lambda h, j: (0, h)),
                  stat, stat, tab, tab, tab],
        out_specs=[blk, blk, pl.BlockSpec((ROWB, V_HEAD), lambda h, j: (j, h))],
        out_shape=[_sds((lp, ATT_HEADS * HEADW), BF16), _sds((lp, ATT_HEADS * HEADW), BF16),
                   _sds((lp, ATT_HEADS * V_HEAD), BF16)],
        scratch_shapes=[pltpu.VMEM((lp, HEADW), F32), pltpu.VMEM((ROWB, HEADW), F32), pltpu.VMEM((ROWB, V_HEAD), F32)],
        compiler_params=_cp(("arbitrary", "arbitrary")),
    )(k, v, q, datt, lse_row, delta_row, cos, sa, sb)


def _qkv_bwd(dqp, dk, dv, cq, ckv, cos, sa, sb, gq, gkv, wq, wk, wv):
    lp = cq.shape[0]
    qw = ATT_HEADS * HEADW

    def body(dqp_ref, dk_ref, dv_ref, cq_ref, ckv_ref, cos_ref, sa_ref, sb_ref, gq_ref, gkv_ref, wq_ref, wk_ref, wv_ref,
             dcq_ref, dckv_ref, dkr_ref, dgq_ref, dgkv_ref):
        dcq, dgq = _rms_bwd(cq_ref[...], gq_ref[...], _dot(dqp_ref[...], wq_ref[...], NT))
        dcq_ref[...] = dcq.astype(BF16)
        dkb = dk_ref[...]
        dksum = sum(dkb[:, hh * HEADW:(hh + 1) * HEADW].astype(F32) for hh in range(ATT_HEADS))
        dkr_ref[...] = _rope_t(dksum, cos_ref[...], sa_ref[...], sb_ref[...]).astype(BF16)
        dckvn = _dot(dkb, wk_ref[...], NT) + _dot(dv_ref[...], wv_ref[...], NT)
        dckv, dgkv = _rms_bwd(ckv_ref[...], gkv_ref[...], dckvn)
        dckv_ref[...] = dckv.astype(BF16)
        _acc(dgq_ref, _colsum8(dgq))
        _acc(dgkv_ref, _colsum8(dgkv))

    return pl.pallas_call(
        body, name="qkv_bwd", grid=(lp // ROWB,),
        in_specs=[_rowspec(ROWB, qw), _rowspec(ROWB, qw), _rowspec(ROWB, ATT_HEADS * V_HEAD),
                  _rowspec(ROWB, Q_LORA), _rowspec(ROWB, KV_LORA)] + [_rowspec(ROWB, HEADW)] * 3
        + [_fullspec((1, Q_LORA)), _fullspec((1, KV_LORA)), _fullspec((Q_LORA, qw)), _fullspec((KV_LORA, qw)),
           _fullspec((KV_LORA, ATT_HEADS * V_HEAD))],
        out_specs=[_rowspec(ROWB, Q_LORA), _rowspec(ROWB, KV_LORA), _rowspec(ROWB, HEADW),
                   _fullspec((8, Q_LORA)), _fullspec((8, KV_LORA))],
        out_shape=[_sds((lp, Q_LORA), BF16), _sds((lp, KV_LORA), BF16), _sds((lp, HEADW), BF16),
                   _sds((8, Q_LORA), F32), _sds((8, KV_LORA), F32)],
        compiler_params=_cp(("arbitrary",)),
    )(dqp, dk, dv, cq, ckv, cos, sa, sb, gq, gkv, wq, wk, wv)


def _in_bwd(pieces, head, x, dh1, g, w_all):
    lp = dh1.shape[0]
    npc = len(pieces)
    assert sum(pc.shape[1] for pc in pieces) == PROJ_W

    def body(*refs):
        head_ref, x_ref, dh1_ref, g_ref, w_ref, dx_ref, dhead_ref, dg_ref, dp_ref = refs[npc:]
        dp = jnp.concatenate([r[...].astype(BF16) for r in refs[:npc]], axis=1)
        dp_ref[...] = dp
        dx, dg = _rms_bwd(_h_block(head_ref, x_ref), g_ref[...], _dot(dp, w_ref[...], NT))
        dh = dh1_ref[...] + dx

        @pl.when(pl.program_id(0) == 0)
        def _():
            dhead_ref[...] = dh

        @pl.when(pl.program_id(0) > 0)
        def _():
            dx_ref[...] = dh

        _acc(dg_ref, _colsum8(dg))

    return pl.pallas_call(
        body, name="in_bwd", grid=(lp // ROWB,),
        in_specs=[_rowspec(ROWB, pc.shape[1]) for pc in pieces]
        + [_fullspec((ROWB, D_MODEL)), _xspec(), _rowspec(ROWB, D_MODEL), _fullspec((1, D_MODEL)),
           _fullspec((D_MODEL, PROJ_W))],
        out_specs=[_xspec(), _fullspec((ROWB, D_MODEL)), _fullspec((8, D_MODEL)), _rowspec(ROWB, PROJ_W)],
        out_shape=[_sds(x.shape, F32), _sds((ROWB, D_MODEL), F32), _sds((8, D_MODEL), F32), _sds((lp, PROJ_W), BF16)],
        compiler_params=_cp(("arbitrary",)),
    )(*pieces, head, x, dh1, g, w_all)


def _tile_of(n, cap=1024):
    return max(t for t in range(LANE, min(n, cap) + 1, LANE) if n % t == 0)


def _matmul_tn(name, a, b):
    rows, kd = a.shape
    nd = b.shape[1]
    tk, tn = _tile_of(kd), _tile_of(nd)
    rb = 3 * ROWB if rows % (3 * ROWB) == 0 else ROWB

    def body(a_ref, b_ref, o_ref):
        @pl.when(pl.program_id(2) == 0)
        def _():
            o_ref[...] = jnp.zeros_like(o_ref)

        o_ref[...] += _dot(a_ref[...], b_ref[...], ((0,), (0,)))

    return pl.pallas_call(
        body, name=name, grid=(kd // tk, nd // tn, rows // rb),
        in_specs=[pl.BlockSpec((rb, tk), lambda i, j, r: (r, i)), pl.BlockSpec((rb, tn), lambda i, j, r: (r, j))],
        out_specs=pl.BlockSpec((tk, tn), lambda i, j, r: (i, j)), out_shape=_sds((kd, nd), F32),
        compiler_params=_cp(("arbitrary", "arbitrary", "arbitrary")),
    )(a, b)


def _local_backward(head, x, f, exchange_late=False):
    p = f["p"]
    g = {}
    row = lambda v: _row1(v)
    s8 = lambda v: jnp.sum(v, axis=0)
    dh1, du, df, dgpre, dgpost = _mlp_bwd(f["dh2"], f["f"], f["h1"], f["u"], p["w_mlp_up"], p["w_mlp_down"],
                                          row(p["norm_mlp_pre"]), row(p["norm_mlp_post"]))
    g["norm_mlp_pre"], g["norm_mlp_post"] = s8(dgpre), s8(dgpost)
    g["w_mlp_up"] = _matmul_tn("dw_mlp_up", f["hn2"], du)
    g["w_mlp_down"] = _matmul_tn("dw_mlp_down", f["a"], df)
    dmix, datt, dssm, dgmp, delta = _out_bwd(dh1, f["mix"], f["att"], p["w_out"], row(p["norm_mix_post"]))
    g["norm_mix_post"] = s8(dgmp)
    g["w_out"] = jnp.concatenate([_matmul_tn("dw_out_att", f["att"], dmix), _matmul_tn("dw_out_ssm", f["ssm"], dmix)], axis=0)
    dy, dz, dgn, dd = _ssd_post_bwd(dssm, f["y"], f["xact"], f["z"], f["dskip"], row(p["ssm_norm"]))
    g["ssm_norm"] = s8(dgn)
    g["d_skip"] = s8(dd).reshape(SSM_HEADS, SSM_HEAD_DIM).sum(axis=1)
    dact, ddtr, da8, dbias8, received = _ssd_bwd(
        dy, f["xact"], f["dtr"], f["hprev"], f["dt_bias"], f["a_log"], f["dskip"], f["seq_rows"],
        exchange=[_to_chunks(n, g[n]).astype(BF16) for n in LATE] if exchange_late else ())
    g["a_log"], g["dt_bias"] = s8(da8)[:SSM_HEADS], s8(dbias8)[:SSM_HEADS]
    dxbc, dcw8, dcb8 = _conv_bwd(dact, f["xbc"], f["cw"], row(p["conv_b"]))
    g["conv_w"], g["conv_b"] = dcw8[:CONV_K], s8(dcb8)
    dqp, dkb, dv = _flash_bwd(f["q"], f["k"], f["v"], datt, f["lse"], delta, *f["rope"])
    dcq, dckv, dkr, dgq, dgkv = _qkv_bwd(dqp, dkb, dv, f["cq"], f["ckv"], *f["rope"], row(p["q_a_norm"]),
                                         row(p["kv_a_norm"]), f["wq"], f["wk"], f["wv"])
    g["q_a_norm"], g["kv_a_norm"] = s8(dgq), s8(dgkv)
    dwq = _matmul_tn("dw_q_up", f["cqn"], dqp).reshape(Q_LORA, ATT_HEADS, HEADW)
    g["w_q_up"] = dwq[:, :, :QK_NOPE + QK_ROPE].reshape(Q_LORA, -1)
    dwk = _matmul_tn("dw_k_up", f["ckvn"], dkb).reshape(KV_LORA, ATT_HEADS, HEADW)[:, :, :QK_NOPE]
    dwv = _matmul_tn("dw_v_up", f["ckvn"], dv).reshape(KV_LORA, ATT_HEADS, V_HEAD)
    g["w_kv_up"] = jnp.concatenate([dwk, dwv], axis=2).reshape(KV_LORA, -1)
    dx, dhead, dgin, dproj = _in_bwd([dcq, dckv, dkr, dz, dxbc, ddtr], head, x, dh1, row(p["norm_mix_pre"]), f["w_all"])
    g["norm_mix_pre"] = s8(dgin)
    g["meta_tokens"] = dhead[PADF:]
    dwa = _matmul_tn("dw_in", f["hn"], dproj)
    g["w_in"] = jnp.concatenate([dwa[:, PC_Q:PC_KR], dwa[:, PC_KR + QK_NOPE:PC_KR + QK_NOPE + QK_ROPE],
                                 dwa[:, PC_Z:PC_DT + SSM_HEADS]], axis=1)
    return dx, g, received


BIG = {"w_in": ((D_MODEL, IN_WIDTH), 1), "w_q_up": ((Q_LORA, ATT_HEADS * (QK_NOPE + QK_ROPE)), 1),
       "w_kv_up": ((KV_LORA, ATT_HEADS * (QK_NOPE + V_HEAD)), 1), "w_out": ((2 * D_MODEL, D_MODEL), 0),
       "w_mlp_up": ((D_MODEL, D_FF), 1), "w_mlp_down": ((D_FF, D_MODEL), 0), "conv_w": ((CONV_K, CONV_DIM), 1),
       "meta_tokens": ((N_META, D_MODEL), 1)}
SMALL = {"norm_mix_pre": D_MODEL, "q_a_norm": Q_LORA, "kv_a_norm": KV_LORA, "conv_b": CONV_DIM, "dt_bias": SSM_HEADS,
         "a_log": SSM_HEADS, "d_skip": SSM_HEADS, "ssm_norm": SSM_WIDTH, "norm_mix_post": D_MODEL,
         "norm_mlp_pre": D_MODEL, "norm_mlp_post": D_MODEL}
WEIGHT_ORDER = ("meta_tokens", "norm_mix_pre", "w_in", "q_a_norm", "w_q_up", "kv_a_norm", "w_kv_up", "conv_w", "conv_b",
                "dt_bias", "a_log", "d_skip", "ssm_norm", "w_out", "norm_mix_post", "norm_mlp_pre", "w_mlp_up",
                "w_mlp_down", "norm_mlp_post")
ADAM_ROWS = 256


def _shard_shape(name):
    shape, ax = BIG[name]
    return tuple(d // N_DEV if a == ax else d for a, d in enumerate(shape))


SMALL_ROWS = -(-sum(SMALL.values()) // (LANE * 8)) * 8


def _pack(flats, rows):
    v = jnp.concatenate([f.reshape(-1) for f in flats])
    return jnp.pad(v, (0, rows * LANE - v.shape[0])).reshape(rows, LANE)


def _unpack(packed, shapes):
    v = packed.reshape(-1)
    out, o = [], 0
    for s in shapes:
        n = math.prod(s)
        out.append(v[o:o + n].reshape(s))
        o += n
    return out


def _to_chunks(name, full):
    shape, ax = BIG[name]
    if ax == 0:
        return full.reshape((N_DEV,) + _shard_shape(name))
    k, n = shape
    return full.reshape(k, N_DEV, n // N_DEV).transpose(1, 0, 2)


def _from_shards(name, shards):
    shape, ax = BIG[name]
    if ax == 0:
        return shards.reshape(shape)
    return shards.transpose(1, 0, 2).reshape(shape)


def _peer(k):
    x, y, c = lax.axis_index("x"), lax.axis_index("y"), lax.axis_index("c")
    px = 1 - x if k & 4 else x
    py = 1 - y if k & 2 else y
    pc = 1 - c if k & 1 else c
    return (px, py, pc), 4 * px + 2 * py + pc


def _gather_ops(x_refs, out_refs, send_sems, recv_sems, local_sems):
    na = len(x_refs)
    chips = (4, 2, 6)

    def copy(a, n, block, to, src=None):
        return pltpu.make_async_remote_copy(
            src_ref=out_refs[a].at[block] if src is None else src, dst_ref=out_refs[a].at[block],
            send_sem=send_sems.at[7 * a + n], recv_sem=recv_sems.at[7 * a + n], device_id=to, device_id_type=MESH)

    def mine():
        me = _peer(0)[1]
        return [pltpu.make_async_copy(x_refs[a], out_refs[a].at[me], local_sems.at[a]) for a in range(na)]

    def first():
        me, sibling = _peer(0)[1], _peer(1)[0]
        out = [copy(a, 0, me, sibling, src=x_refs[a]) for a in range(na)]
        return out + [copy(a, 1 + n, me, _peer(k)[0], src=x_refs[a]) for n, k in enumerate(chips) for a in range(na)]

    def passed():
        sibling = _peer(1)[0]
        return [copy(a, 4 + n, _peer(k)[1], sibling) for n, k in enumerate(chips) for a in range(na)]

    def start():
        for cp in mine() + first():
            cp.start()

    def forward():
        sibling = _peer(1)[0]
        fwd = passed()
        for n, k in enumerate(chips):
            for a in range(na):
                copy(a, 1 + n, _peer(k)[1], sibling).wait_recv()
                fwd[n * na + a].start()

    def finish():
        sibling = _peer(1)[0]
        for a in range(na):
            copy(a, 0, _peer(1)[1], sibling).wait_recv()
        for n, k in enumerate(chips):
            for a in range(na):
                copy(a, 4 + n, _peer(k | 1)[1], sibling).wait_recv()
        for cp in first() + passed():
            cp.wait_send()
        for cp in mine():
            cp.wait()

    return start, forward, finish


def _gather_scratch(na):
    return [pltpu.SemaphoreType.DMA((7 * na,)), pltpu.SemaphoreType.DMA((7 * na,)), pltpu.SemaphoreType.DMA((na,))]


def _all_gather(shards):
    na = len(shards)

    def body(*refs):
        for step in _gather_ops(refs[:na], refs[na:2 * na], *refs[2 * na:]):
            step()

    return pl.pallas_call(
        body, name="all_gather_weights", out_shape=[_sds((N_DEV,) + s.shape, s.dtype) for s in shards],
        in_specs=[ANY] * na, out_specs=[ANY] * na, scratch_shapes=_gather_scratch(na),
    )(*shards)


def _exchange(chunks, small):
    na = len(chunks) + 1

    def body(*refs):
        for step in _exchange_ops(refs[:na], refs[na:2 * na], *refs[2 * na:], whole=(na - 1,)):
            step()

    arrays = list(chunks) + [small]
    return pl.pallas_call(
        body, name="exchange_grads",
        out_shape=[_sds(c.shape, c.dtype) for c in chunks] + [_sds((N_DEV,) + small.shape, small.dtype)],
        in_specs=[ANY] * na, out_specs=[ANY] * na, scratch_shapes=_gather_scratch(na),
    )(*arrays)


def _exchange_ops(in_refs, out_refs, send_sems, recv_sems, local_sems, whole=()):
    na = len(in_refs)

    def src(a, idx):
        return in_refs[a] if a in whole else in_refs[a].at[idx]

    def own():
        me = _peer(0)[1]
        return [pltpu.make_async_copy(src(a, me), out_refs[a].at[me], local_sems.at[a]) for a in range(na)]

    def copy(a, k, sending):
        me = _peer(0)[1]
        to, idx = _peer(k)
        return pltpu.make_async_remote_copy(
            src_ref=src(a, idx if sending else me), dst_ref=out_refs[a].at[me if sending else idx],
            send_sem=send_sems.at[7 * a + k - 1], recv_sem=recv_sems.at[7 * a + k - 1],
            device_id=to, device_id_type=MESH)

    def sent():
        return [copy(a, k, True) for k in range(1, N_DEV) for a in range(na)]

    def start():
        for cp in own() + sent():
            cp.start()

    def finish():
        for k in range(1, N_DEV):
            for a in range(na):
                copy(a, k, False).wait_recv()
        for cp in sent():
            cp.wait_send()
        for cp in own():
            cp.wait()

    return start, finish


def _reduce_adamw(name, recv, w, m, v):
    rows, cols = w.shape
    blk = ADAM_ROWS if rows % ADAM_ROWS == 0 else rows
    c1 = 1.0 - ADAM_B1 ** ADAM_STEP
    c2 = 1.0 - ADAM_B2 ** ADAM_STEP

    def body(r_ref, w_ref, m_ref, v_ref, g_ref, d_ref, nm_ref, nv_ref):
        g = r_ref[0].astype(F32)
        for s in range(1, N_DEV):
            g = g + r_ref[s].astype(F32)
        g_ref[...] = g
        m_ = ADAM_B1 * m_ref[...] + (1.0 - ADAM_B1) * g
        v_ = ADAM_B2 * v_ref[...] + (1.0 - ADAM_B2) * (g * g)
        nm_ref[...] = m_
        nv_ref[...] = v_
        d_ref[...] = -ADAM_LR * ((m_ / c1) / (jnp.sqrt(v_ / c2) + ADAM_EPS) + ADAM_WD * w_ref[...])

    spec = _rowspec(blk, cols)
    return pl.pallas_call(
        body, name="reduce_adamw_" + name, grid=(rows // blk,),
        in_specs=[pl.BlockSpec((N_DEV, blk, cols), lambda i: (0, i, 0)), spec, spec, spec],
        out_specs=[spec] * 4, out_shape=[_sds((rows, cols), F32)] * 4,
        compiler_params=_cp(("arbitrary",)),
    )(recv, w, m, v)


def kernel(x, meta_tokens, norm_mix_pre, w_in, q_a_norm, w_q_up, kv_a_norm, w_kv_up, conv_w, conv_b, dt_bias, a_log, d_skip, ssm_norm, w_out, norm_mix_post, norm_mlp_pre, w_mlp_up, w_mlp_down, norm_mlp_post, loss_target, m_meta_tokens, m_norm_mix_pre, m_w_in, m_q_a_norm, m_w_q_up, m_kv_a_norm, m_w_kv_up, m_conv_w, m_conv_b, m_dt_bias, m_a_log, m_d_skip, m_ssm_norm, m_w_out, m_norm_mix_post, m_norm_mlp_pre, m_w_mlp_up, m_w_mlp_down, m_norm_mlp_post, v_meta_tokens, v_norm_mix_pre, v_w_in, v_q_a_norm, v_w_q_up, v_kv_a_norm, v_w_kv_up, v_conv_w, v_conv_b, v_dt_bias, v_a_log, v_d_skip, v_ssm_norm, v_w_out, v_norm_mix_post, v_norm_mlp_pre, v_w_mlp_up, v_w_mlp_down, v_norm_mlp_post):
    w = dict(meta_tokens=meta_tokens, norm_mix_pre=norm_mix_pre, w_in=w_in, q_a_norm=q_a_norm, w_q_up=w_q_up,
             kv_a_norm=kv_a_norm, w_kv_up=w_kv_up, conv_w=conv_w, conv_b=conv_b, dt_bias=dt_bias, a_log=a_log,
             d_skip=d_skip, ssm_norm=ssm_norm, w_out=w_out, norm_mix_post=norm_mix_post, norm_mlp_pre=norm_mlp_pre,
             w_mlp_up=w_mlp_up, w_mlp_down=w_mlp_down, norm_mlp_post=norm_mlp_post)
    m = dict(meta_tokens=m_meta_tokens, norm_mix_pre=m_norm_mix_pre, w_in=m_w_in, q_a_norm=m_q_a_norm, w_q_up=m_w_q_up,
             kv_a_norm=m_kv_a_norm, w_kv_up=m_w_kv_up, conv_w=m_conv_w, conv_b=m_conv_b, dt_bias=m_dt_bias,
             a_log=m_a_log, d_skip=m_d_skip, ssm_norm=m_ssm_norm, w_out=m_w_out, norm_mix_post=m_norm_mix_post,
             norm_mlp_pre=m_norm_mlp_pre, w_mlp_up=m_w_mlp_up, w_mlp_down=m_w_mlp_down, norm_mlp_post=m_norm_mlp_post)
    v = dict(meta_tokens=v_meta_tokens, norm_mix_pre=v_norm_mix_pre, w_in=v_w_in, q_a_norm=v_q_a_norm, w_q_up=v_w_q_up,
             kv_a_norm=v_kv_a_norm, w_kv_up=v_w_kv_up, conv_w=v_conv_w, conv_b=v_conv_b, dt_bias=v_dt_bias,
             a_log=v_a_log, d_skip=v_d_skip, ssm_norm=v_ssm_norm, w_out=v_w_out, norm_mix_post=v_norm_mix_post,
             norm_mlp_pre=v_norm_mlp_pre, w_mlp_up=v_w_mlp_up, w_mlp_down=v_w_mlp_down, norm_mlp_post=v_norm_mlp_post)
    big_names = [n for n in WEIGHT_ORDER if n in BIG]
    small_names = [n for n in WEIGHT_ORDER if n in SMALL]
    shard = lambda d, n: d[n].reshape(_shard_shape(n))

    f32_names = ("conv_w", "meta_tokens")
    early = [n for n in big_names if n not in LATE]
    gathered = _all_gather([shard(w, n).astype(F32 if n in f32_names else BF16) for n in early])
    p = {n: w[n].reshape(-1) for n in small_names}
    p.update({n: _from_shards(n, s) for n, s in zip(early, gathered)})
    head = jnp.concatenate([jnp.zeros((PADF, D_MODEL), F32), p["meta_tokens"]], axis=0)
    f = _local_forward(head, x[0], loss_target[0], p, late={n: shard(w, n).astype(BF16) for n in LATE})
    dx, g, recv_late = _local_backward(head, x[0], f, exchange_late=True)
    grad_x = dx[None]
    loss = lax.psum(f["loss"], ("x", "y", "c"))

    small = _pack([g[n] for n in small_names], SMALL_ROWS)
    *recv_early, recv_small = _exchange([_to_chunks(n, g[n]).astype(BF16) for n in early], small)
    recv_of = {**dict(zip(early, recv_early)), **dict(zip(LATE, recv_late))}

    outs = {}
    kinds = ("grad", "delta", "new_m", "new_v")
    for n, recv in ((n, recv_of[n]) for n in big_names):
        for kind, arr in zip(kinds, _reduce_adamw(n, recv, shard(w, n), shard(m, n), shard(v, n))):
            outs[kind, n] = arr.reshape(w[n].shape)
    packed = [_pack([d[n] for n in small_names], SMALL_ROWS) for d in (w, m, v)]
    for kind, arr in zip(kinds, _reduce_adamw("small", recv_small, *packed)):
        for n, val in zip(small_names, _unpack(arr, [(SMALL[n],) for n in small_names])):
            outs[kind, n] = val.reshape(w[n].shape)
    return (loss, grad_x) + tuple(outs[kind, n] for kind in ("grad", "delta", "new_m", "new_v") for n in WEIGHT_ORDER)
```

```python
import math

import jax
import jax.numpy as jnp
import numpy as np
from jax import lax
from jax.experimental import pallas as pl
from jax.experimental.pallas import tpu as pltpu

F32 = jnp.float32
BF16 = jnp.bfloat16

D_MODEL = 1024
N_META = 16
EPS = 1e-6
ATT_HEADS = 8
Q_LORA = 384
KV_LORA = 256
QK_NOPE = 128
QK_ROPE = 64
V_HEAD = 128
ROPE_THETA = 10000.0
SSM_HEADS = 16
SSM_HEAD_DIM = 64
SSM_WIDTH = 1024
SSM_GROUPS = 2
SSM_STATE = 128
CONV_K = 4
CHUNK = 128
CONV_DIM = 1536
D_FF = 4096
IN_SPLITS = (Q_LORA, KV_LORA, QK_ROPE, SSM_WIDTH, CONV_DIM, SSM_HEADS)
IN_WIDTH = sum(IN_SPLITS)
ADAM_LR, ADAM_B1, ADAM_B2, ADAM_EPS, ADAM_WD, ADAM_STEP = 0.001, 0.9, 0.999, 1e-08, 0.01, 10

LANE = 128
ROWB = 512
PADF = ROWB - N_META
HEADW = 256
PC_Q, PC_KV, PC_KR, PC_Z, PC_XBC, PC_DT, PROJ_W = 0, 384, 640, 896, 1920, 3456, 3584
NEG = -1e30
N_DEV = 8
VMEM_LIMIT = 56 * 1024 * 1024
MESH = pl.DeviceIdType.MESH


def _cp(sem, vmem=VMEM_LIMIT, **kw):
    return pltpu.CompilerParams(dimension_semantics=sem, vmem_limit_bytes=vmem, **kw)


def _dot(a, b, dims=((1,), (0,))):
    return lax.dot_general(a, b, (dims, ((), ())), preferred_element_type=F32)


def _bdot(a, b, dims=((1,), (0,))):
    return _dot(a.astype(BF16), b.astype(BF16), dims)


NT = ((1,), (1,))


def _rms_fwd(x, w):
    r = lax.rsqrt(jnp.mean(x * x, axis=-1, keepdims=True) + EPS)
    return (x * r) * w


def _rms_bwd(x, w, dy):
    r = lax.rsqrt(jnp.mean(x * x, axis=-1, keepdims=True) + EPS)
    xh = x * r
    g = dy * w
    dx = r * (g - xh * jnp.mean(g * xh, axis=-1, keepdims=True))
    return dx, dy * xh


def _sigmoid(x):
    return 0.5 * jnp.tanh(0.5 * x) + 0.5


def _colsum8(x):
    t, c = x.shape
    return jnp.sum(x.reshape(t // 8, 8, c), axis=0)


def _rowspec(t, c, cb=0):
    return pl.BlockSpec((t, c), lambda i: (i, cb))


def _fullspec(shape):
    n = len(shape)
    return pl.BlockSpec(shape, lambda i: (0,) * n)


def _sds(shape, dt):
    return jax.ShapeDtypeStruct(shape, dt)


def _acc(ref, val):
    @pl.when(pl.program_id(0) == 0)
    def _():
        ref[...] = jnp.zeros_like(ref)

    ref[...] += val


def _xspec():
    return pl.BlockSpec((ROWB, D_MODEL), lambda i: (jnp.maximum(i - 1, 0), 0))


def _h_block(head_ref, x_ref):
    return jnp.where(pl.program_id(0) == 0, head_ref[...], x_ref[...])


def _norm_in_proj(head, x, g, w_all):
    lp = head.shape[0] + x.shape[0]

    def body(head_ref, x_ref, g_ref, w_ref, hn_ref, cq_ref, ckv_ref, kr_ref, z_ref, xbc_ref, dt_ref):
        hn = _rms_fwd(_h_block(head_ref, x_ref), g_ref[...]).astype(BF16)
        hn_ref[...] = hn
        p = _dot(hn, w_ref[...])
        cq_ref[...] = p[:, PC_Q:PC_KV]
        ckv_ref[...] = p[:, PC_KV:PC_KR]
        kr_ref[...] = p[:, PC_KR:PC_Z]
        z_ref[...] = p[:, PC_Z:PC_XBC]
        xbc_ref[...] = p[:, PC_XBC:PC_DT]
        dt_ref[...] = p[:, PC_DT:PROJ_W]

    widths = (Q_LORA, KV_LORA, HEADW, SSM_WIDTH, CONV_DIM, LANE)
    return pl.pallas_call(
        body, name="norm_in_proj", grid=(lp // ROWB,),
        in_specs=[_fullspec((ROWB, D_MODEL)), _xspec(), _fullspec((1, D_MODEL)), _fullspec((D_MODEL, PROJ_W))],
        out_specs=[_rowspec(ROWB, D_MODEL)] + [_rowspec(ROWB, w) for w in widths],
        out_shape=[_sds((lp, D_MODEL), BF16)] + [_sds((lp, w), F32) for w in widths],
        compiler_params=_cp(("arbitrary",)),
    )(head, x, g, w_all)


def _rope(x, cos, sa, sb):
    w = x.shape[1]
    return x * cos + pltpu.roll(x, w - 32, 1) * sa + pltpu.roll(x, 32, 1) * sb


def _rope_t(dy, cos, sa, sb):
    w = dy.shape[1]
    return dy * cos + pltpu.roll(dy * sa, 32, 1) + pltpu.roll(dy * sb, w - 32, 1)


def _qkv(cq, ckv, kr, cos, sa, sb, gq, gkv, wq, wk, wv):
    lp = cq.shape[0]
    qw = ATT_HEADS * HEADW
    half = HEADW // 2
    assert half == QK_NOPE == V_HEAD == LANE

    def body(cq_ref, ckv_ref, kr_ref, cos_ref, sa_ref, sb_ref, gq_ref, gkv_ref, wq_ref, wk_ref, wv_ref,
             q_ref, k_ref, v_ref, cqn_ref, ckvn_ref):
        tabs = [cos_ref[...], sa_ref[...], sb_ref[...]]
        cqn = _rms_fwd(cq_ref[...], gq_ref[...]).astype(BF16)
        ckvn = _rms_fwd(ckv_ref[...], gkv_ref[...]).astype(BF16)
        cqn_ref[...] = cqn
        ckvn_ref[...] = ckvn
        q = _dot(cqn, wq_ref[...])
        kn = _dot(ckvn, wk_ref[...])
        vv = _dot(ckvn, wv_ref[...])
        krope = _rope(kr_ref[:, pl.ds(half, half)], *tabs).astype(BF16)
        ones = jnp.ones((ROWB, half), BF16)
        for hh in range(ATT_HEADS):
            lo, hi, src = pl.ds(hh * HEADW, half), pl.ds(hh * HEADW + half, half), slice(hh * half, (hh + 1) * half)
            q_ref[:, lo] = (q[:, hh * HEADW:hh * HEADW + half] * Q_PRESCALE).astype(BF16)
            q_ref[:, hi] = (_rope(q[:, hh * HEADW + half:(hh + 1) * HEADW], *tabs) * Q_PRESCALE).astype(BF16)
            k_ref[:, lo] = kn[:, src].astype(BF16)
            k_ref[:, hi] = krope
            v_ref[:, lo] = vv[:, src].astype(BF16)
            v_ref[:, hi] = ones

    return pl.pallas_call(
        body, name="qkv", grid=(lp // ROWB,),
        in_specs=[_rowspec(ROWB, Q_LORA), _rowspec(ROWB, KV_LORA), _rowspec(ROWB, HEADW)]
        + [_rowspec(ROWB, half)] * 3
        + [_fullspec((1, Q_LORA)), _fullspec((1, KV_LORA)), _fullspec((Q_LORA, qw)),
           _fullspec((KV_LORA, ATT_HEADS * QK_NOPE)), _fullspec((KV_LORA, ATT_HEADS * V_HEAD))],
        out_specs=[_rowspec(ROWB, qw), _rowspec(ROWB, qw), _rowspec(ROWB, qw),
                   _rowspec(ROWB, Q_LORA), _rowspec(ROWB, KV_LORA)],
        out_shape=[_sds((lp, qw), BF16), _sds((lp, qw), BF16), _sds((lp, qw), BF16),
                   _sds((lp, Q_LORA), BF16), _sds((lp, KV_LORA), BF16)],
        compiler_params=_cp(("arbitrary",)),
    )(cq, ckv, kr, cos, sa, sb, gq, gkv, wq, wk, wv)


ATT_SCALE = (QK_NOPE + QK_ROPE) ** -0.5
LOG2E = 1.4426950408889634
LN2 = 0.6931471805599453
Q_PRESCALE = ATT_SCALE * LOG2E
KVB = 512
META_KEYS = LANE
assert N_META <= META_KEYS


def _att_ok(qrow, krow):
    return (krow <= qrow) & ((krow >= PADF) | (qrow < PADF))


def _lanes(x, n):
    return x if n == 1 else jnp.concatenate([x] * n, axis=1)


def _pair_loop(lo, hi, tile, unrolls=(2,)):
    for u in tuple(unrolls) + (1,):
        n = jnp.maximum(hi - lo, 0)
        trips = n // u

        def many(t, c, u=u, lo=lo):
            for d in range(u):
                tile(lo + u * t + d)
            return c

        lax.fori_loop(0, trips, many, 0)
        lo = lo + trips * u


def _flash_fwd(q, k, v):
    lp = q.shape[0]
    nq = lp // ROWB

    def body(q_ref, k_ref, v_ref, o_ref, lse_ref, acc, m_s):
        i = pl.program_id(1)
        qb = q_ref[...]
        m_s[...] = jnp.full_like(m_s, NEG)
        acc[...] = jnp.zeros_like(acc)

        def tile(j, masked, off=None, nkeys=KVB):
            off = pl.multiple_of(j * KVB, KVB) if off is None else off
            kb = k_ref[pl.ds(off, nkeys), :]
            vb = v_ref[pl.ds(off, nkeys), :]
            s = _dot(qb, kb, NT)
            if masked:
                qrow = i * ROWB + lax.broadcasted_iota(jnp.int32, s.shape, 0)
                krow = off + lax.broadcasted_iota(jnp.int32, s.shape, 1)
                s = jnp.where(_att_ok(qrow, krow), s, NEG)
            m_prev = m_s[...]
            m_new = jnp.maximum(m_prev, jnp.max(s, axis=1, keepdims=True))
            alpha = jnp.exp2(m_prev - m_new)
            p = jnp.exp2(s - _lanes(m_new, nkeys // LANE))
            acc[...] = _lanes(alpha, 2) * acc[...] + _dot(p.astype(BF16), vb)
            m_s[...] = m_new

        def first_tile():
            tile(0, True, off=ROWB - META_KEYS, nkeys=META_KEYS)

        @pl.when(i == 0)
        def _():
            first_tile()

        @pl.when(i > 0)
        def _():
            first_tile()
            tile(i, True)

        _pair_loop(1, i, lambda j: tile(j, False), (16, 8, 4, 2))
        l = acc[:, V_HEAD:]
        o_ref[...] = (acc[:, :V_HEAD] / l).astype(BF16)
        lse_ref[0] = (m_s[...] + jnp.log2(l)).T[0:1, :]

    return pl.pallas_call(
        body, name="flash_fwd", grid=(ATT_HEADS, nq),
        in_specs=[pl.BlockSpec((ROWB, HEADW), lambda h, i: (i, h)),
                  pl.BlockSpec((lp, HEADW), lambda h, i: (0, h)),
                  pl.BlockSpec((lp, HEADW), lambda h, i: (0, h))],
        out_specs=[pl.BlockSpec((ROWB, V_HEAD), lambda h, i: (i, h)),
                   pl.BlockSpec((1, 1, ROWB), lambda h, i: (h, 0, i))],
        out_shape=[_sds((lp, ATT_HEADS * V_HEAD), BF16), _sds((ATT_HEADS, 1, lp), F32)],
        scratch_shapes=[pltpu.VMEM((ROWB, HEADW), F32), pltpu.VMEM((ROWB, LANE), F32)],
        compiler_params=_cp(("arbitrary", "arbitrary")),
    )(q, k, v)


def _silu(x):
    return x * _sigmoid(x)


CONV_ROWS = 64


def _conv_fwd(xbc, cw, cb):
    lp, c = xbc.shape
    t8 = ROWB // 8

    def body(x_ref, prev_ref, w_ref, b_ref, o_ref, buf):
        i = pl.program_id(0)
        buf[pl.ds(0, 8), :] = jnp.where(i > 0, prev_ref[...], 0.0)
        buf[pl.ds(8, ROWB), :] = x_ref[...]

        def strip(s, carry):
            cs = pl.ds(pl.multiple_of(s * LANE, LANE), LANE)
            w, b = w_ref[:, cs], b_ref[:, cs]
            for r0 in range(0, ROWB, CONV_ROWS):
                pre = b + sum(w[kk:kk + 1, :] * buf[pl.ds(8 - (CONV_K - 1) + kk + r0, CONV_ROWS), cs]
                              for kk in range(CONV_K))
                o_ref[pl.ds(r0, CONV_ROWS), cs] = _silu(pre)
            return carry

        lax.fori_loop(0, c // LANE, strip, 0)

    return pl.pallas_call(
        body, name="conv_fwd", grid=(lp // ROWB,),
        in_specs=[_rowspec(ROWB, c), pl.BlockSpec((8, c), lambda i: (jnp.maximum(i * t8 - 1, 0), 0)),
                  _fullspec((8, c)), _fullspec((1, c))],
        out_specs=_rowspec(ROWB, c), out_shape=_sds((lp, c), F32),
        scratch_shapes=[pltpu.VMEM((ROWB + 8, c), F32)],
        compiler_params=_cp(("arbitrary",)),
    )(xbc, xbc, cw, cb)


def _expand_mat():
    r = np.arange(LANE)[:, None]
    c = np.arange(SSM_WIDTH)[None, :]
    return jnp.asarray((c // SSM_HEAD_DIM == r).astype(np.float32))


def _tri_mat():
    i = np.arange(CHUNK)
    return jnp.asarray((i[:, None] >= i[None, :]).astype(np.float32))


def _x3(m, axis):
    return jnp.concatenate([m.astype(BF16)] * 3, axis=axis)


def _split3(x):
    hi = x.astype(BF16)
    r = x - hi.astype(F32)
    mid = r.astype(BF16)
    return hi, mid, (r - mid.astype(F32)).astype(BF16)


def _dot01_r(x, m3):
    return _dot(jnp.concatenate(_split3(x), axis=1), m3)


def _dot01_l(m3, x):
    return _dot(m3, jnp.concatenate(_split3(x), axis=0))


def _ssd_prep(dtr_ref, bias_ref, alog_ref, tri3, c, seq_rows):
    rows = c * CHUNK + lax.broadcasted_iota(jnp.int32, (CHUNK, LANE), 0)
    lanes = lax.broadcasted_iota(jnp.int32, (CHUNK, LANE), 1)
    valid = (rows >= PADF) & (rows < PADF + seq_rows) & (lanes < SSM_HEADS)
    dtr = dtr_ref[...] + bias_ref[...]
    sp = jnp.maximum(dtr, 0.0) + jnp.log(1.0 + jnp.exp(-jnp.abs(dtr)))
    dt = jnp.where(valid, sp, 0.0)
    a = -jnp.exp(alog_ref[...])
    acol = _dot01_l(tri3, dt * a)
    return dt, a, acol, valid, dtr


def _row16(v):
    return jnp.broadcast_to(v, (16, v.shape[1]))


def _ssd_fwd(xbc_act, dtr, dt_bias, a_log, seq_rows, gather=()):
    lp = xbc_act.shape[0]
    nc = lp // CHUNK
    gw = SSM_WIDTH // SSM_GROUPS
    hpg = SSM_HEADS // SSM_GROUPS

    na = len(gather)

    def body(x_ref, b_ref, c_ref, dtr_ref, bias_ref, alog_ref, tri_ref, tri3_ref, ex3_ref, *rest):
        gin, (y_ref, hp_ref), gout, h_s, sems = rest[:na], rest[na:na + 2], rest[na + 2:2 * na + 2], rest[2 * na + 2], rest[2 * na + 3:]
        c = pl.program_id(0)

        @pl.when(c == 0)
        def _():
            h_s[...] = jnp.zeros_like(h_s)

        if na:
            g_start, g_forward, g_finish = _gather_ops(gin, gout, *sems)
            pl.when(c == 0)(g_start)
            pl.when(c == nc // 2)(g_forward)

        ex3 = ex3_ref[...]
        dt, a, acol, _, _ = _ssd_prep(dtr_ref, bias_ref, alog_ref, tri3_ref[...], c, seq_rows)
        arow = acol.T
        dtrow = dt.T
        alast = acol[CHUNK - 1:CHUNK, :]
        e_all = _dot01_r(jnp.exp(acol), ex3)
        wx_all = _dot01_r(jnp.exp(alast - acol) * dt, ex3)
        dec_all = _dot01_r(_row16(jnp.exp(alast)), ex3)[0:1, :]
        causal = tri_ref[...] > 0.5
        hp_ref[0] = h_s[...]
        for g in range(SSM_GROUPS):
            gs = slice(g * gw, (g + 1) * gw)
            bg = b_ref[:, g * SSM_STATE:(g + 1) * SSM_STATE]
            cg = c_ref[:, g * SSM_STATE:(g + 1) * SSM_STATE].astype(BF16)
            xg = x_ref[:, gs]
            hg = h_s[:, gs]
            gm = _bdot(cg, bg, NT)
            y_off = _bdot(cg, hg) * e_all[:, gs]
            for r in range(hpg):
                hd = g * hpg + r
                seg = acol[:, hd:hd + 1] - arow[hd:hd + 1, :]
                lm = jnp.where(causal, jnp.exp(jnp.where(causal, seg, 0.0)), 0.0)
                w = gm * lm * dtrow[hd:hd + 1, :]
                cs = slice(r * SSM_HEAD_DIM, (r + 1) * SSM_HEAD_DIM)
                y_ref[:, pl.ds(hd * SSM_HEAD_DIM, SSM_HEAD_DIM)] = _bdot(w, xg[:, cs]) + y_off[:, cs]
            st = _bdot(bg.T, xg * wx_all[:, gs])
            h_s[:, gs] = hg * dec_all[:, gs] + st

        if na:
            pl.when(c == nc - 1)(g_finish)

    xs_spec = pl.BlockSpec((CHUNK, SSM_WIDTH), lambda c: (c, 0))
    b_spec = pl.BlockSpec((CHUNK, 2 * SSM_STATE), lambda c: (c, SSM_WIDTH // (2 * SSM_STATE)))
    c_spec = pl.BlockSpec((CHUNK, 2 * SSM_STATE), lambda c: (c, SSM_WIDTH // (2 * SSM_STATE) + 1))
    y, hprev, *gathered = pl.pallas_call(
        body, name="ssd_fwd", grid=(nc,),
        in_specs=[xs_spec, b_spec, c_spec, pl.BlockSpec((CHUNK, LANE), lambda c: (c, 0)),
                  _fullspec((1, LANE)), _fullspec((1, LANE)), _fullspec((CHUNK, CHUNK)), _fullspec((CHUNK, 3 * CHUNK)),
                  _fullspec((3 * LANE, SSM_WIDTH))] + [ANY] * na,
        out_specs=[xs_spec, pl.BlockSpec((1, SSM_STATE, SSM_WIDTH), lambda c: (c, 0, 0))] + [ANY] * na,
        out_shape=[_sds((lp, SSM_WIDTH), F32), _sds((nc, SSM_STATE, SSM_WIDTH), F32)]
        + [_sds((N_DEV,) + s.shape, s.dtype) for s in gather],
        scratch_shapes=[pltpu.VMEM((SSM_STATE, SSM_WIDTH), F32)] + (_gather_scratch(na) if na else []),
        compiler_params=_cp(("arbitrary",)),
    )(xbc_act, xbc_act, xbc_act, dtr, dt_bias, a_log, _tri_mat(), _x3(_tri_mat(), 1), _x3(_expand_mat(), 0), *gather)
    return y, hprev, gathered


def _group_mean(x):
    gw = SSM_WIDTH // SSM_GROUPS
    parts = [jnp.broadcast_to(jnp.mean(x[:, g * gw:(g + 1) * gw], axis=-1, keepdims=True), (x.shape[0], gw))
             for g in range(SSM_GROUPS)]
    return jnp.concatenate(parts, axis=1)


def _ssd_post(y, xbc_act, z, dskip, gnorm):
    lp = y.shape[0]

    def body(y_ref, x_ref, z_ref, d_ref, g_ref, o_ref):
        z_ = z_ref[...]
        gt = (y_ref[...] + d_ref[...] * x_ref[...]) * _silu(z_)
        r = lax.rsqrt(_group_mean(gt * gt) + EPS)
        o_ref[...] = ((gt * r) * g_ref[...]).astype(BF16)

    return pl.pallas_call(
        body, name="ssd_post", grid=(lp // ROWB,),
        in_specs=[_rowspec(ROWB, SSM_WIDTH)] * 3 + [_fullspec((1, SSM_WIDTH))] * 2,
        out_specs=_rowspec(ROWB, SSM_WIDTH), out_shape=_sds((lp, SSM_WIDTH), BF16),
        compiler_params=_cp(("arbitrary",)),
    )(y, xbc_act, z, dskip, gnorm)


def _out_proj(att, ssm, head, x, w_out, g_post):
    lp = att.shape[0]

    def body(a_ref, s_ref, head_ref, x_ref, w_ref, g_ref, mix_ref, h1_ref):
        mix = _dot(a_ref[...], w_ref[pl.ds(0, 1024), :]) + _dot(s_ref[...], w_ref[pl.ds(1024, 1024), :])
        mix_ref[...] = mix
        h1_ref[...] = _h_block(head_ref, x_ref) + _rms_fwd(mix, g_ref[...])

    return pl.pallas_call(
        body, name="out_proj", grid=(lp // ROWB,),
        in_specs=[_rowspec(ROWB, 1024)] * 2 + [_fullspec((ROWB, D_MODEL)), _xspec(), _fullspec((2048, D_MODEL)),
                                               _fullspec((1, D_MODEL))],
        out_specs=[_rowspec(ROWB, D_MODEL)] * 2, out_shape=[_sds((lp, D_MODEL), F32)] * 2,
        compiler_params=_cp(("arbitrary",)),
    )(att, ssm, head, x, w_out, g_post)


def _resident(w_hbm, w_vmem, sem):
    @pl.when(pl.program_id(0) == 0)
    def _():
        cp = pltpu.make_async_copy(w_hbm, w_vmem, sem)
        cp.start()
        cp.wait()


ANY = pl.BlockSpec(memory_space=pl.ANY)


def _mlp_fwd(h1, tgt, w_up, w_down, g_pre, g_post, seq_rows):
    lp = h1.shape[0]

    def body(h1_ref, t_ref, wu_hbm, wd_hbm, gpre_ref, gpost_ref, hn2_ref, u_ref, a_ref, f_ref, dh2_ref, loss_ref,
             wu, wd, sems):
        _resident(wu_hbm, wu, sems.at[0])
        _resident(wd_hbm, wd, sems.at[1])
        i = pl.program_id(0)
        h1_ = h1_ref[...]
        hn2 = _rms_fwd(h1_, gpre_ref[...]).astype(BF16)
        hn2_ref[...] = hn2
        u = jnp.maximum(_dot(hn2, wu[...]), 0.0)
        u_ref[...] = u.astype(BF16)
        a = (u * u).astype(BF16)
        a_ref[...] = a
        f = _dot(a, wd[...])
        f_ref[...] = f
        h2 = h1_ + _rms_fwd(f, gpost_ref[...])
        rows = i * ROWB + lax.broadcasted_iota(jnp.int32, (ROWB, 1), 0)
        real = (rows >= PADF + N_META) & (rows < PADF + seq_rows)
        err = jnp.where(real, h2 - t_ref[...], 0.0)
        dh2_ref[...] = err * (1.0 / D_MODEL)
        _acc(loss_ref, _colsum8(err * err))

    return pl.pallas_call(
        body, name="mlp_fwd", grid=(lp // ROWB,),
        in_specs=[_rowspec(ROWB, D_MODEL), _xspec()] + [ANY, ANY] + [_fullspec((1, D_MODEL))] * 2,
        out_specs=[_rowspec(ROWB, D_MODEL), _rowspec(ROWB, D_FF), _rowspec(ROWB, D_FF)] + [_rowspec(ROWB, D_MODEL)] * 2
        + [_fullspec((8, D_MODEL))],
        out_shape=[_sds((lp, D_MODEL), BF16), _sds((lp, D_FF), BF16), _sds((lp, D_FF), BF16), _sds((lp, D_MODEL), F32),
                   _sds((lp, D_MODEL), F32), _sds((8, D_MODEL), F32)],
        scratch_shapes=[pltpu.VMEM((D_MODEL, D_FF), BF16), pltpu.VMEM((D_FF, D_MODEL), BF16), pltpu.SemaphoreType.DMA((2,))],
        compiler_params=_cp(("arbitrary",)),
    )(h1, tgt, w_up, w_down, g_pre, g_post)


def _pad_cols(w, width):
    return jnp.pad(w, ((0, 0), (0, width - w.shape[1])))


def _layout_weights(w_in, w_q_up, w_kv_up):
    o = np.cumsum((0,) + IN_SPLITS)
    pieces = [w_in[:, o[k]:o[k + 1]] for k in range(6)]
    kr = jnp.pad(pieces[2], ((0, 0), (QK_NOPE, HEADW - QK_NOPE - QK_ROPE)))
    w_all = jnp.concatenate([pieces[0], pieces[1], kr, pieces[3], pieces[4], _pad_cols(pieces[5], LANE)], axis=1)
    wq = jnp.pad(w_q_up.reshape(Q_LORA, ATT_HEADS, QK_NOPE + QK_ROPE), ((0, 0), (0, 0), (0, HEADW - QK_NOPE - QK_ROPE)))
    wkv = w_kv_up.reshape(KV_LORA, ATT_HEADS, QK_NOPE + V_HEAD)
    wk = jnp.pad(wkv[:, :, :QK_NOPE], ((0, 0), (0, 0), (0, HEADW - QK_NOPE)))
    wv = wkv[:, :, QK_NOPE:]
    return (w_all, wq.reshape(Q_LORA, -1), wk.reshape(KV_LORA, -1), wv.reshape(KV_LORA, -1),
            wkv[:, :, :QK_NOPE].reshape(KV_LORA, -1))


def _rope_tables(lp):
    pos = jnp.maximum(jnp.arange(lp, dtype=jnp.int32) - PADF, 0).astype(F32)
    inv_freq = ROPE_THETA ** (-jnp.arange(0, QK_ROPE, 2, dtype=F32) / QK_ROPE)
    ang = pos[:, None] * inv_freq[None, :]
    cos, sin = jnp.cos(ang), jnp.sin(ang)
    z32, z64 = jnp.zeros((lp, 32), F32), jnp.zeros((lp, 64), F32)
    cos_t = jnp.concatenate([cos, cos, jnp.ones((lp, 64), F32)], axis=1)
    sa = jnp.concatenate([-sin, z32, z64], axis=1)
    sb = jnp.concatenate([z32, sin, z64], axis=1)
    return cos_t, sa, sb


def _row1(v, width=None):
    v = v.reshape(1, -1).astype(F32)
    return v if width is None else _pad_cols(v, width)


LATE = ("w_out", "w_mlp_up", "w_mlp_down")


def _local_forward(head, x, tgt, p, late=None):
    assert head.shape[0] == ROWB and x.shape[0] % ROWB == 0
    lp = ROWB + x.shape[0]
    seq_rows = N_META + x.shape[0]
    f = {"seq_rows": seq_rows}
    w_all, wq, wk, wv, wkn = _layout_weights(p["w_in"], p["w_q_up"], p["w_kv_up"])
    f.update(w_all=w_all, wq=wq, wk=wk, wv=wv)
    f["hn"], cq, ckv, kr, f["z"], f["xbc"], f["dtr"] = _norm_in_proj(head, x, _row1(p["norm_mix_pre"]), w_all)
    f.update(cq=cq, ckv=ckv)
    f["rope"] = _rope_tables(lp)
    f["q"], f["k"], f["v"], f["cqn"], f["ckvn"] = _qkv(cq, ckv, kr, *f["rope"], _row1(p["q_a_norm"]),
                                                   _row1(p["kv_a_norm"]), wq, wkn, wv)
    f["att"], f["lse"] = _flash_fwd(f["q"], f["k"], f["v"])
    f["cw"] = jnp.pad(p["conv_w"].astype(F32), ((0, 8 - CONV_K), (0, 0)))
    f["xact"] = _conv_fwd(f["xbc"], f["cw"], _row1(p["conv_b"]))
    f["dt_bias"], f["a_log"] = _row1(p["dt_bias"], LANE), _row1(p["a_log"], LANE)
    f["y"], f["hprev"], gathered = _ssd_fwd(f["xact"], f["dtr"], f["dt_bias"], f["a_log"], seq_rows,
                                            gather=[late[n] for n in LATE] if late else ())
    p = {**p, **{n: _from_shards(n, s) for n, s in zip(LATE, gathered)}}
    f["p"] = p
    f["dskip"] = jnp.repeat(p["d_skip"].reshape(-1).astype(F32), SSM_HEAD_DIM).reshape(1, SSM_WIDTH)
    f["ssm"] = _ssd_post(f["y"], f["xact"], f["z"], f["dskip"], _row1(p["ssm_norm"]))
    f["mix"], f["h1"] = _out_proj(f["att"], f["ssm"], head, x, p["w_out"], _row1(p["norm_mix_post"]))
    f["hn2"], f["u"], f["a"], f["f"], f["dh2"], loss8 = _mlp_fwd(
        f["h1"], tgt, p["w_mlp_up"], p["w_mlp_down"], _row1(p["norm_mlp_pre"]), _row1(p["norm_mlp_post"]), seq_rows)
    f["loss"] = 0.5 * jnp.sum(loss8) / D_MODEL
    return f


MLPB = 256


def _mlp_bwd(dh2, f, h1, u, w_up, w_down, g_pre, g_post):
    lp = h1.shape[0]

    def body(dh2_ref, f_ref, h1_ref, u_ref, wu_hbm, wd_hbm, gpre_ref, gpost_ref,
             dh1_ref, du_ref, df_ref, dgpre_ref, dgpost_ref, wu, wd, sems):
        _resident(wu_hbm, wu, sems.at[0])
        _resident(wd_hbm, wd, sems.at[1])
        dh2_ = dh2_ref[...]
        df, dgp = _rms_bwd(f_ref[...], gpost_ref[...], dh2_)
        dfb = df.astype(BF16)
        df_ref[...] = dfb
        da = _dot(dfb, wd[...], NT)
        du = (da * (2.0 * u_ref[...].astype(F32))).astype(BF16)
        du_ref[...] = du
        dhn2 = _dot(du, wu[...], NT)
        dx, dgq = _rms_bwd(h1_ref[...], gpre_ref[...], dhn2)
        dh1_ref[...] = dh2_ + dx
        _acc(dgpre_ref, _colsum8(dgq))
        _acc(dgpost_ref, _colsum8(dgp))

    return pl.pallas_call(
        body, name="mlp_bwd", grid=(lp // MLPB,),
        in_specs=[_rowspec(MLPB, D_MODEL)] * 3 + [_rowspec(MLPB, D_FF), ANY, ANY] + [_fullspec((1, D_MODEL))] * 2,
        out_specs=[_rowspec(MLPB, D_MODEL), _rowspec(MLPB, D_FF), _rowspec(MLPB, D_MODEL),
                   _fullspec((8, D_MODEL)), _fullspec((8, D_MODEL))],
        out_shape=[_sds((lp, D_MODEL), F32), _sds((lp, D_FF), BF16), _sds((lp, D_MODEL), BF16),
                   _sds((8, D_MODEL), F32), _sds((8, D_MODEL), F32)],
        scratch_shapes=[pltpu.VMEM((D_MODEL, D_FF), BF16), pltpu.VMEM((D_FF, D_MODEL), BF16), pltpu.SemaphoreType.DMA((2,))],
        compiler_params=_cp(("arbitrary",)),
    )(dh2, f, h1, u, w_up, w_down, g_pre, g_post)


def _out_bwd(dh1, mix, att, w_out, g_post):
    lp = dh1.shape[0]

    def body(dh1_ref, mix_ref, att_ref, w_ref, g_ref, dmix_ref, datt_ref, dssm_ref, dg_ref, dl_ref):
        dmix, dg = _rms_bwd(mix_ref[...], g_ref[...], dh1_ref[...])
        dmb = dmix.astype(BF16)
        dmix_ref[...] = dmb
        datt = _dot(dmb, w_ref[pl.ds(0, 1024), :], NT).astype(BF16)
        datt_ref[...] = datt
        dssm_ref[...] = _dot(dmb, w_ref[pl.ds(1024, 1024), :], NT)
        _acc(dg_ref, _colsum8(dg))
        prod = datt.astype(F32) * att_ref[...].astype(F32)
        for hh in range(ATT_HEADS):
            d = jnp.sum(prod[:, hh * V_HEAD:(hh + 1) * V_HEAD], axis=1, keepdims=True)
            dl_ref[hh] = jnp.broadcast_to(d, (ROWB, LANE)).T[0:1, :]

    return pl.pallas_call(
        body, name="out_bwd", grid=(lp // ROWB,),
        in_specs=[_rowspec(ROWB, D_MODEL)] * 3 + [_fullspec((2048, D_MODEL)), _fullspec((1, D_MODEL))],
        out_specs=[_rowspec(ROWB, D_MODEL)] * 3 + [_fullspec((8, D_MODEL)),
                                                   pl.BlockSpec((ATT_HEADS, 1, ROWB), lambda i: (0, 0, i))],
        out_shape=[_sds((lp, D_MODEL), BF16), _sds((lp, 1024), BF16), _sds((lp, 1024), F32), _sds((8, D_MODEL), F32),
                   _sds((ATT_HEADS, 1, lp), F32)],
        compiler_params=_cp(("arbitrary",)),
    )(dh1, mix, att, w_out, g_post)


def _ssd_post_bwd(dssm, y, xact, z, dskip, gnorm):
    lp = y.shape[0]

    def body(do_ref, y_ref, x_ref, z_ref, d_ref, g_ref, dy_ref, dz_ref, dg_ref, dd_ref):
        z_, x_ = z_ref[...], x_ref[...]
        sg = _sigmoid(z_)
        sz = z_ * sg
        y2 = y_ref[...] + d_ref[...] * x_
        gt = y2 * sz
        r = lax.rsqrt(_group_mean(gt * gt) + EPS)
        gh = gt * r
        do = do_ref[...]
        dgh = do * g_ref[...]
        dgt = r * (dgh - gh * _group_mean(dgh * gh))
        dy2 = dgt * sz
        dy_ref[...] = dy2
        dz_ref[...] = (dgt * y2 * (sg * (1.0 + z_ * (1.0 - sg)))).astype(BF16)
        _acc(dg_ref, _colsum8(do * gh))
        _acc(dd_ref, _colsum8(dy2 * x_))

    return pl.pallas_call(
        body, name="ssd_post_bwd", grid=(lp // ROWB,),
        in_specs=[_rowspec(ROWB, SSM_WIDTH)] * 4 + [_fullspec((1, SSM_WIDTH))] * 2,
        out_specs=[_rowspec(ROWB, SSM_WIDTH)] * 2 + [_fullspec((8, SSM_WIDTH))] * 2,
        out_shape=[_sds((lp, SSM_WIDTH), F32), _sds((lp, SSM_WIDTH), BF16), _sds((8, SSM_WIDTH), F32), _sds((8, SSM_WIDTH), F32)],
        compiler_params=_cp(("arbitrary",)),
    )(dssm, y, xact, z, dskip, gnorm)


def _ssd_bwd(dy, xact, dtr, hprev, dt_bias, a_log, dskip, seq_rows, exchange=()):
    lp = xact.shape[0]
    nc = lp // CHUNK
    gw = SSM_WIDTH // SSM_GROUPS
    hpg = SSM_HEADS // SSM_GROUPS
    nb = SSM_WIDTH // (2 * SSM_STATE)
    na = len(exchange)

    def body(dy_ref, x_ref, b_ref, c_ref, dtr_ref, hp_ref, bias_ref, alog_ref, dsk_ref, tri_ref, tri3_ref, trit3_ref,
             ex3_ref, ext3_ref, *rest):
        xin, (dact_ref, ddtr_ref, da_ref, dbias_ref), xout = rest[:na], rest[na:na + 4], rest[na + 4:2 * na + 4]
        dh_s, sems = rest[2 * na + 4], rest[2 * na + 5:]
        step = pl.program_id(0)
        c = nc - 1 - step

        @pl.when(step == 0)
        def _():
            dh_s[...] = jnp.zeros_like(dh_s)

        if na:
            x_start, x_finish = _exchange_ops(xin, xout, *sems)
            pl.when(step == 0)(x_start)

        tri = tri_ref[...]
        ex3 = ex3_ref[...]
        dt, a, acol, valid, dtr_ = _ssd_prep(dtr_ref, bias_ref, alog_ref, tri3_ref[...], c, seq_rows)
        arow = acol.T
        dtrow = dt.T
        alast = acol[CHUNK - 1:CHUNK, :]
        e_all = _dot01_r(jnp.exp(acol), ex3)
        wgt0 = jnp.exp(alast - acol)
        wgt = wgt0 * dt
        wx_all = _dot01_r(wgt, ex3)
        elast = jnp.exp(alast)
        dec_all = _dot01_r(_row16(elast), ex3)[0:1, :]
        causal = tri > 0.5
        upper = tri.T > 0.5
        lane_id = lax.broadcasted_iota(jnp.int32, (1, LANE), 1)
        sub_id = lax.broadcasted_iota(jnp.int32, (CHUNK, 1), 0)
        dacol = jnp.zeros((CHUNK, LANE), F32)
        darowf = jnp.zeros((CHUNK, LANE), F32)
        ddtrowf = jnp.zeros((CHUNK, LANE), F32)
        dwgt = jnp.zeros((CHUNK, LANE), F32)
        delast = jnp.zeros((1, LANE), F32)
        for g in range(SSM_GROUPS):
            gs = slice(g * gw, (g + 1) * gw)
            ext3_g = ext3_ref[g]
            bg = b_ref[:, g * SSM_STATE:(g + 1) * SSM_STATE]
            cg = c_ref[:, g * SSM_STATE:(g + 1) * SSM_STATE]
            bgb, cgb = bg.astype(BF16), cg.astype(BF16)
            xg = x_ref[:, gs]
            dyg = dy_ref[:, gs]
            hg = hp_ref[0, :, gs]
            dhg = dh_s[:, gs]
            hgb, dhgb = hg.astype(BF16), dhg.astype(BF16)
            gm = _dot(cgb, bgb, NT)
            gmt = _dot(bgb, cgb, NT)
            y_off = _dot(cgb, hgb) * e_all[:, gs]
            dy0 = (dyg * e_all[:, gs]).astype(BF16)
            dcg = _dot(dy0, hgb, NT)
            dh_in = _dot(cg.T.astype(BF16), dy0) + dhg * dec_all[:, gs]
            dacol = dacol + _dot01_r(dyg * y_off, ext3_g)
            xw = xg * wx_all[:, gs]
            dxw = _dot(bgb, dhgb)
            dx_state = dxw * wx_all[:, gs]
            dwgt = dwgt + _dot01_r(dxw * xg, ext3_g)
            dbt = _dot(dhgb, xw.astype(BF16), NT)
            hh = _colsum8(dhg * hg)
            hh16 = jnp.concatenate([hh, jnp.zeros_like(hh)], axis=0)
            delast = delast + jnp.sum(_dot01_r(hh16, ext3_g), axis=0, keepdims=True)
            dgm = jnp.zeros((CHUNK, CHUNK), F32)
            for r in range(hpg):
                hd = g * hpg + r
                cs = slice(r * SSM_HEAD_DIM, (r + 1) * SSM_HEAD_DIM)
                acol_r, arow_r = acol[:, hd:hd + 1], arow[hd:hd + 1, :]
                dtrow_r, dtcol_r = dtrow[hd:hd + 1, :], dt[:, hd:hd + 1]
                lm = jnp.where(causal, jnp.exp(jnp.where(causal, acol_r - arow_r, 0.0)), 0.0)
                lmt = jnp.where(upper, jnp.exp(jnp.where(upper, arow_r - acol_r, 0.0)), 0.0)
                wt = gmt * lmt * dtcol_r
                dy_r = dyg[:, cs].astype(BF16)
                dx_r = _dot(wt.astype(BF16), dy_r)
                dw = _dot(dy_r, xg[:, cs].astype(BF16), NT)
                t1 = dw * lm
                dgm = dgm + t1 * dtrow_r
                q1 = t1 * gm
                m = q1 * dtrow_r
                dacol = dacol + jnp.sum(m, axis=1, keepdims=True) * (lane_id == hd).astype(F32)
                darowf = darowf - (sub_id == hd).astype(F32) * jnp.sum(m, axis=0, keepdims=True)
                ddtrowf = ddtrowf + (sub_id == hd).astype(F32) * jnp.sum(q1, axis=0, keepdims=True)
                dact_ref[:, pl.ds(hd * SSM_HEAD_DIM, SSM_HEAD_DIM)] = (
                    dx_r + dx_state[:, cs] + dyg[:, cs] * dsk_ref[:, pl.ds(hd * SSM_HEAD_DIM, SSM_HEAD_DIM)])
            dgmb = dgm.astype(BF16)
            dact_ref[:, pl.ds(SSM_WIDTH + g * SSM_STATE, SSM_STATE)] = dbt.T + _dot(dgm.T.astype(BF16), cgb)
            dact_ref[:, pl.ds(SSM_WIDTH + 2 * SSM_STATE + g * SSM_STATE, SSM_STATE)] = dcg + _dot(dgmb, bgb)
            dh_s[:, gs] = dh_in
        t = dwgt * wgt
        dalast = jnp.sum(t, axis=0, keepdims=True) + delast * elast
        dacol_tot = dacol - t + darowf.T + (sub_id == CHUNK - 1).astype(F32) * dalast
        dda = _dot01_l(trit3_ref[...], dacol_tot)
        ddt = dwgt * wgt0 + ddtrowf.T + dda * a
        ddtr = jnp.where(valid, ddt * _sigmoid(dtr_), 0.0)
        ddtr_ref[...] = ddtr
        _acc(da_ref, _colsum8(dda * dt) * a)
        _acc(dbias_ref, _colsum8(ddtr))
        if na:
            pl.when(step == nc - 1)(x_finish)

    rev = lambda c: nc - 1 - c
    xs_spec = pl.BlockSpec((CHUNK, SSM_WIDTH), lambda c: (rev(c), 0))
    dact, ddtr, da8, dbias8, *received = pl.pallas_call(
        body, name="ssd_bwd", grid=(nc,),
        in_specs=[xs_spec, xs_spec,
                  pl.BlockSpec((CHUNK, 2 * SSM_STATE), lambda c: (rev(c), nb)),
                  pl.BlockSpec((CHUNK, 2 * SSM_STATE), lambda c: (rev(c), nb + 1)),
                  pl.BlockSpec((CHUNK, LANE), lambda c: (rev(c), 0)),
                  pl.BlockSpec((1, SSM_STATE, SSM_WIDTH), lambda c: (rev(c), 0, 0)),
                  _fullspec((1, LANE)), _fullspec((1, LANE)), _fullspec((1, SSM_WIDTH)),
                  _fullspec((CHUNK, CHUNK)), _fullspec((CHUNK, 3 * CHUNK)), _fullspec((CHUNK, 3 * CHUNK)),
                  _fullspec((3 * LANE, SSM_WIDTH)), _fullspec((SSM_GROUPS, 3 * gw, LANE))] + [ANY] * na,
        out_specs=[pl.BlockSpec((CHUNK, CONV_DIM), lambda c: (rev(c), 0)), pl.BlockSpec((CHUNK, LANE), lambda c: (rev(c), 0)),
                   _fullspec((8, LANE)), _fullspec((8, LANE))] + [ANY] * na,
        out_shape=[_sds((lp, CONV_DIM), F32), _sds((lp, LANE), F32), _sds((8, LANE), F32), _sds((8, LANE), F32)]
        + [_sds(e.shape, e.dtype) for e in exchange],
        scratch_shapes=[pltpu.VMEM((SSM_STATE, SSM_WIDTH), F32)] + (_gather_scratch(na) if na else []),
        compiler_params=_cp(("arbitrary",)),
    )(dy, xact, xact, xact, dtr, hprev, dt_bias, a_log, dskip, _tri_mat(), _x3(_tri_mat(), 1), _x3(_tri_mat().T, 1),
      _x3(_expand_mat(), 0), jnp.stack([_x3(_expand_mat().T[g * gw:(g + 1) * gw], 0) for g in range(SSM_GROUPS)]),
      *exchange)
    return dact, ddtr, da8, dbias8, received


def _conv_bwd(dact, xbc, cw, cb):
    lp, c = xbc.shape
    t8 = ROWB // 8
    nb = lp // ROWB

    def body(d_ref, dnext_ref, x_ref, prev_ref, next_ref, w_ref, b_ref, dx_ref, dw_ref, db_ref, xb, dp):
        i = pl.program_id(0)
        last = i == nb - 1
        xb[pl.ds(0, 8), :] = jnp.where(i > 0, prev_ref[...], 0.0)
        xb[pl.ds(8, ROWB), :] = x_ref[...]
        xb[pl.ds(8 + ROWB, 8), :] = jnp.where(last, 0.0, next_ref[...])

        @pl.when(i == 0)
        def _():
            dw_ref[...] = jnp.zeros_like(dw_ref)
            db_ref[...] = jnp.zeros_like(db_ref)

        sub = lax.broadcasted_iota(jnp.int32, (8, 1), 0)
        x0 = 8 - (CONV_K - 1)

        def strip(s, carry):
            cs = pl.ds(pl.multiple_of(s * LANE, LANE), LANE)
            w, b = w_ref[:, cs], b_ref[:, cs]

            def dpre_rows(r0, n, d):
                xs = [xb[pl.ds(x0 + kk + r0, n), cs] for kk in range(CONV_K)]
                pre = b + sum(w[kk:kk + 1, :] * xs[kk] for kk in range(CONV_K))
                sg = _sigmoid(pre)
                return d * (sg * (1.0 + pre * (1.0 - sg))), xs

            dws = [jnp.zeros((8, LANE), F32) for _ in range(CONV_K)]
            dbs = jnp.zeros((8, LANE), F32)
            for r0 in range(0, ROWB, CONV_ROWS):
                dpre, xs = dpre_rows(r0, CONV_ROWS, d_ref[pl.ds(r0, CONV_ROWS), cs])
                dp[pl.ds(r0, CONV_ROWS), cs] = dpre
                dbs = dbs + _colsum8(dpre)
                for kk in range(CONV_K):
                    dws[kk] = dws[kk] + _colsum8(dpre * xs[kk])
            dp[pl.ds(ROWB, 8), cs] = dpre_rows(ROWB, 8, jnp.where(last, 0.0, dnext_ref[:, cs]))[0]
            dwv = sum(jnp.where(sub == kk, jnp.sum(dws[kk], axis=0, keepdims=True), 0.0) for kk in range(CONV_K))
            dw_ref[:, cs] += dwv
            db_ref[:, cs] += dbs
            for r0 in range(0, ROWB, CONV_ROWS):
                dx = sum(w[kk:kk + 1, :] * dp[pl.ds(CONV_K - 1 - kk + r0, CONV_ROWS), cs] for kk in range(CONV_K))
                dx_ref[pl.ds(r0, CONV_ROWS), cs] = dx.astype(BF16)
            return carry

        lax.fori_loop(0, c // LANE, strip, 0)

    nxt = lambda i: (jnp.minimum((i + 1) * t8, lp // 8 - 1), 0)
    prv = lambda i: (jnp.maximum(i * t8 - 1, 0), 0)
    return pl.pallas_call(
        body, name="conv_bwd", grid=(nb,),
        in_specs=[_rowspec(ROWB, c), pl.BlockSpec((8, c), nxt), _rowspec(ROWB, c), pl.BlockSpec((8, c), prv),
                  pl.BlockSpec((8, c), nxt), _fullspec((8, c)), _fullspec((1, c))],
        out_specs=[_rowspec(ROWB, c), _fullspec((8, c)), _fullspec((8, c))],
        out_shape=[_sds((lp, c), BF16), _sds((8, c), F32), _sds((8, c), F32)],
        scratch_shapes=[pltpu.VMEM((ROWB + 16, c), F32), pltpu.VMEM((ROWB + 8, c), F32)],
        compiler_params=_cp(("arbitrary",)),
    )(dact, dact, xbc, xbc, xbc, cw, cb)


def _flash_bwd(q, k, v, datt, lse_row, delta_row, cos, sa, sb):
    lp = q.shape[0]
    nk = lp // ROWB

    def body(k_ref, v_ref, q_ref, do_ref, lse_ref, dl_ref, cos_ref, sa_ref, sb_ref, dq_ref, dk_ref, dv_ref,
             dq_acc, dk_acc, dv_acc):
        j = pl.program_id(1)

        @pl.when(j == 0)
        def _():
            dq_acc[...] = jnp.zeros_like(dq_acc)

        dk_acc[...] = jnp.zeros_like(dk_acc)
        dv_acc[...] = jnp.zeros_like(dv_acc)

        def tile(i, masked, key0=0, nkeys=ROWB):
            keys = pl.ds(key0, nkeys)
            kb, vb = k_ref[keys, :], v_ref[keys, :]
            off = pl.multiple_of(i * ROWB, ROWB)
            qb = q_ref[pl.ds(off, ROWB), :]
            dob = do_ref[pl.ds(off, ROWB), :]
            lse_r = lse_ref[0, :, pl.ds(off, ROWB)]
            dl_r = dl_ref[0, :, pl.ds(off, ROWB)]
            st = _dot(kb, qb, NT)
            if masked:
                krow = j * ROWB + key0 + lax.broadcasted_iota(jnp.int32, st.shape, 0)
                qrow = i * ROWB + lax.broadcasted_iota(jnp.int32, st.shape, 1)
                st = jnp.where(_att_ok(qrow, krow), st, NEG)
            pt = jnp.exp2(st - lse_r)
            dv_acc[keys, :] += _dot(pt.astype(BF16), dob)
            dpt = _dot(vb, dob, NT)
            dst = (pt * (dpt - dl_r)).astype(BF16)
            dk_acc[keys, :] += _dot(dst, qb)
            dq_acc[pl.ds(off, ROWB), :] += _dot(dst, kb, ((0,), (0,)))

        @pl.when(j == 0)
        def _():
            _pair_loop(0, nk, lambda i: tile(i, True, ROWB - META_KEYS, META_KEYS), (4, 2))

        @pl.when((j > 0) & (j < nk - 1))
        def _():
            tile(j, True)
            tile(j + 1, False)
            _pair_loop(j + 2, nk, lambda i: tile(i, False), (4, 2))

        @pl.when(j == nk - 1)
        def _():
            tile(j, True)

        dk_ref[...] = (dk_acc[...] * LN2).astype(BF16)
        dv_ref[...] = dv_acc[...].astype(BF16)
        dq = dq_acc[pl.ds(pl.multiple_of(j * ROWB, ROWB), ROWB), :] * ATT_SCALE
        half = HEADW // 2
        dq_ref[:, pl.ds(0, half)] = dq[:, :half].astype(BF16)
        dq_ref[:, pl.ds(half, half)] = _rope_t(dq[:, half:], cos_ref[...], sa_ref[...], sb_ref[...]).astype(BF16)

    stat = pl.BlockSpec((1, 1, lp), lambda h, j: (h, 0, 0))
    blk = pl.BlockSpec((ROWB, HEADW), lambda h, j: (j, h))
    tab = pl.BlockSpec((ROWB, HEADW // 2), lambda h, j: (j, 0))
    return pl.pallas_call(
        body, name="flash_bwd", grid=(ATT_HEADS, nk),
        in_specs=[blk, pl.BlockSpec((ROWB, V_HEAD), lambda h, j: (j, 2 * h)),
                  pl.BlockSpec((lp, HEADW), lambda h, j: (0, h)), pl.BlockSpec((lp, V_HEAD), lambda h, j: (0, h)),
                  stat, stat, tab, tab, tab],
        out_specs=[blk, blk, pl.BlockSpec((ROWB, V_HEAD), lambda h, j: (j, h))],
        out_shape=[_sds((lp, ATT_HEADS * HEADW), BF16), _sds((lp, ATT_HEADS * HEADW), BF16),
                   _sds((lp, ATT_HEADS * V_HEAD), BF16)],
        scratch_shapes=[pltpu.VMEM((lp, HEADW), F32), pltpu.VMEM((ROWB, HEADW), F32), pltpu.VMEM((ROWB, V_HEAD), F32)],
        compiler_params=_cp(("arbitrary", "arbitrary")),
    )(k, v, q, datt, lse_row, delta_row, cos, sa, sb)


def _qkv_bwd(dqp, dk, dv, cq, ckv, cos, sa, sb, gq, gkv, wq, wk, wv):
    lp = cq.shape[0]
    qw = ATT_HEADS * HEADW

    def body(dqp_ref, dk_ref, dv_ref, cq_ref, ckv_ref, cos_ref, sa_ref, sb_ref, gq_ref, gkv_ref, wq_ref, wk_ref, wv_ref,
             dcq_ref, dckv_ref, dkr_ref, dgq_ref, dgkv_ref):
        dcq, dgq = _rms_bwd(cq_ref[...], gq_ref[...], _dot(dqp_ref[...], wq_ref[...], NT))
        dcq_ref[...] = dcq.astype(BF16)
        dkb = dk_ref[...]
        half = HEADW // 2
        dksum = sum(dkb[:, hh * HEADW + half:(hh + 1) * HEADW].astype(F32) for hh in range(ATT_HEADS))
        dkr_ref[:, pl.ds(0, half)] = jnp.zeros((ROWB, half), BF16)
        dkr_ref[:, pl.ds(half, half)] = _rope_t(dksum, cos_ref[...], sa_ref[...], sb_ref[...]).astype(BF16)
        dckvn = _dot(dkb, wk_ref[...], NT) + _dot(dv_ref[...], wv_ref[...], NT)
        dckv, dgkv = _rms_bwd(ckv_ref[...], gkv_ref[...], dckvn)
        dckv_ref[...] = dckv.astype(BF16)
        _acc(dgq_ref, _colsum8(dgq))
        _acc(dgkv_ref, _colsum8(dgkv))

    return pl.pallas_call(
        body, name="qkv_bwd", grid=(lp // ROWB,),
        in_specs=[_rowspec(ROWB, qw), _rowspec(ROWB, qw), _rowspec(ROWB, ATT_HEADS * V_HEAD),
                  _rowspec(ROWB, Q_LORA), _rowspec(ROWB, KV_LORA)] + [_rowspec(ROWB, HEADW // 2)] * 3
        + [_fullspec((1, Q_LORA)), _fullspec((1, KV_LORA)), _fullspec((Q_LORA, qw)), _fullspec((KV_LORA, qw)),
           _fullspec((KV_LORA, ATT_HEADS * V_HEAD))],
        out_specs=[_rowspec(ROWB, Q_LORA), _rowspec(ROWB, KV_LORA), _rowspec(ROWB, HEADW),
                   _fullspec((8, Q_LORA)), _fullspec((8, KV_LORA))],
        out_shape=[_sds((lp, Q_LORA), BF16), _sds((lp, KV_LORA), BF16), _sds((lp, HEADW), BF16),
                   _sds((8, Q_LORA), F32), _sds((8, KV_LORA), F32)],
        compiler_params=_cp(("arbitrary",)),
    )(dqp, dk, dv, cq, ckv, cos, sa, sb, gq, gkv, wq, wk, wv)


def _in_bwd(pieces, head, x, dh1, g, w_all):
    lp = dh1.shape[0]
    npc = len(pieces)
    assert sum(pc.shape[1] for pc in pieces) == PROJ_W

    def body(*refs):
        head_ref, x_ref, dh1_ref, g_ref, w_ref, dx_ref, dhead_ref, dg_ref, dp_ref = refs[npc:]
        dp = jnp.concatenate([r[...].astype(BF16) for r in refs[:npc]], axis=1)
        dp_ref[...] = dp
        dx, dg = _rms_bwd(_h_block(head_ref, x_ref), g_ref[...], _dot(dp, w_ref[...], NT))
        dh = dh1_ref[...] + dx

        @pl.when(pl.program_id(0) == 0)
        def _():
            dhead_ref[...] = dh

        @pl.when(pl.program_id(0) > 0)
        def _():
            dx_ref[...] = dh

        _acc(dg_ref, _colsum8(dg))

    return pl.pallas_call(
        body, name="in_bwd", grid=(lp // ROWB,),
        in_specs=[_rowspec(ROWB, pc.shape[1]) for pc in pieces]
        + [_fullspec((ROWB, D_MODEL)), _xspec(), _rowspec(ROWB, D_MODEL), _fullspec((1, D_MODEL)),
           _fullspec((D_MODEL, PROJ_W))],
        out_specs=[_xspec(), _fullspec((ROWB, D_MODEL)), _fullspec((8, D_MODEL)), _rowspec(ROWB, PROJ_W)],
        out_shape=[_sds(x.shape, F32), _sds((ROWB, D_MODEL), F32), _sds((8, D_MODEL), F32), _sds((lp, PROJ_W), BF16)],
        compiler_params=_cp(("arbitrary",)),
    )(*pieces, head, x, dh1, g, w_all)


def _tile_of(n, cap=1024):
    return max(t for t in range(LANE, min(n, cap) + 1, LANE) if n % t == 0)


def _matmul_tn(name, a, b):
    rows, kd = a.shape
    nd = b.shape[1]
    tk, tn = _tile_of(kd), _tile_of(nd)
    rb = 3 * ROWB if rows % (3 * ROWB) == 0 else ROWB

    def body(a_ref, b_ref, o_ref):
        @pl.when(pl.program_id(2) == 0)
        def _():
            o_ref[...] = jnp.zeros_like(o_ref)

        o_ref[...] += _dot(a_ref[...], b_ref[...], ((0,), (0,)))

    return pl.pallas_call(
        body, name=name, grid=(kd // tk, nd // tn, rows // rb),
        in_specs=[pl.BlockSpec((rb, tk), lambda i, j, r: (r, i)), pl.BlockSpec((rb, tn), lambda i, j, r: (r, j))],
        out_specs=pl.BlockSpec((tk, tn), lambda i, j, r: (i, j)), out_shape=_sds((kd, nd), F32),
        compiler_params=_cp(("arbitrary", "arbitrary", "arbitrary")),
    )(a, b)


def _local_backward(head, x, f, exchange_late=False):
    p = f["p"]
    g = {}
    row = lambda v: _row1(v)
    s8 = lambda v: jnp.sum(v, axis=0)
    dh1, du, df, dgpre, dgpost = _mlp_bwd(f["dh2"], f["f"], f["h1"], f["u"], p["w_mlp_up"], p["w_mlp_down"],
                                          row(p["norm_mlp_pre"]), row(p["norm_mlp_post"]))
    g["norm_mlp_pre"], g["norm_mlp_post"] = s8(dgpre), s8(dgpost)
    g["w_mlp_up"] = _matmul_tn("dw_mlp_up", f["hn2"], du)
    g["w_mlp_down"] = _matmul_tn("dw_mlp_down", f["a"], df)
    dmix, datt, dssm, dgmp, delta = _out_bwd(dh1, f["mix"], f["att"], p["w_out"], row(p["norm_mix_post"]))
    g["norm_mix_post"] = s8(dgmp)
    g["w_out"] = jnp.concatenate([_matmul_tn("dw_out_att", f["att"], dmix), _matmul_tn("dw_out_ssm", f["ssm"], dmix)], axis=0)
    dy, dz, dgn, dd = _ssd_post_bwd(dssm, f["y"], f["xact"], f["z"], f["dskip"], row(p["ssm_norm"]))
    g["ssm_norm"] = s8(dgn)
    g["d_skip"] = s8(dd).reshape(SSM_HEADS, SSM_HEAD_DIM).sum(axis=1)
    dact, ddtr, da8, dbias8, received = _ssd_bwd(
        dy, f["xact"], f["dtr"], f["hprev"], f["dt_bias"], f["a_log"], f["dskip"], f["seq_rows"],
        exchange=[_to_chunks(n, g[n]).astype(BF16) for n in LATE] if exchange_late else ())
    g["a_log"], g["dt_bias"] = s8(da8)[:SSM_HEADS], s8(dbias8)[:SSM_HEADS]
    dxbc, dcw8, dcb8 = _conv_bwd(dact, f["xbc"], f["cw"], row(p["conv_b"]))
    g["conv_w"], g["conv_b"] = dcw8[:CONV_K], s8(dcb8)
    dqp, dkb, dv = _flash_bwd(f["q"], f["k"], f["v"], datt, f["lse"], delta, *f["rope"])
    dcq, dckv, dkr, dgq, dgkv = _qkv_bwd(dqp, dkb, dv, f["cq"], f["ckv"], *f["rope"], row(p["q_a_norm"]),
                                         row(p["kv_a_norm"]), f["wq"], f["wk"], f["wv"])
    g["q_a_norm"], g["kv_a_norm"] = s8(dgq), s8(dgkv)
    dwq = _matmul_tn("dw_q_up", f["cqn"], dqp).reshape(Q_LORA, ATT_HEADS, HEADW)
    g["w_q_up"] = dwq[:, :, :QK_NOPE + QK_ROPE].reshape(Q_LORA, -1)
    dwk = _matmul_tn("dw_k_up", f["ckvn"], dkb).reshape(KV_LORA, ATT_HEADS, HEADW)[:, :, :QK_NOPE]
    dwv = _matmul_tn("dw_v_up", f["ckvn"], dv).reshape(KV_LORA, ATT_HEADS, V_HEAD)
    g["w_kv_up"] = jnp.concatenate([dwk, dwv], axis=2).reshape(KV_LORA, -1)
    dx, dhead, dgin, dproj = _in_bwd([dcq, dckv, dkr, dz, dxbc, ddtr], head, x, dh1, row(p["norm_mix_pre"]), f["w_all"])
    g["norm_mix_pre"] = s8(dgin)
    g["meta_tokens"] = dhead[PADF:]
    dwa = _matmul_tn("dw_in", f["hn"], dproj)
    g["w_in"] = jnp.concatenate([dwa[:, PC_Q:PC_KR], dwa[:, PC_KR + QK_NOPE:PC_KR + QK_NOPE + QK_ROPE],
                                 dwa[:, PC_Z:PC_DT + SSM_HEADS]], axis=1)
    return dx, g, received


BIG = {"w_in": ((D_MODEL, IN_WIDTH), 1), "w_q_up": ((Q_LORA, ATT_HEADS * (QK_NOPE + QK_ROPE)), 1),
       "w_kv_up": ((KV_LORA, ATT_HEADS * (QK_NOPE + V_HEAD)), 1), "w_out": ((2 * D_MODEL, D_MODEL), 0),
       "w_mlp_up": ((D_MODEL, D_FF), 1), "w_mlp_down": ((D_FF, D_MODEL), 0), "conv_w": ((CONV_K, CONV_DIM), 1),
       "meta_tokens": ((N_META, D_MODEL), 1)}
SMALL = {"norm_mix_pre": D_MODEL, "q_a_norm": Q_LORA, "kv_a_norm": KV_LORA, "conv_b": CONV_DIM, "dt_bias": SSM_HEADS,
         "a_log": SSM_HEADS, "d_skip": SSM_HEADS, "ssm_norm": SSM_WIDTH, "norm_mix_post": D_MODEL,
         "norm_mlp_pre": D_MODEL, "norm_mlp_post": D_MODEL}
WEIGHT_ORDER = ("meta_tokens", "norm_mix_pre", "w_in", "q_a_norm", "w_q_up", "kv_a_norm", "w_kv_up", "conv_w", "conv_b",
                "dt_bias", "a_log", "d_skip", "ssm_norm", "w_out", "norm_mix_post", "norm_mlp_pre", "w_mlp_up",
                "w_mlp_down", "norm_mlp_post")
ADAM_ROWS = 256


def _shard_shape(name):
    shape, ax = BIG[name]
    return tuple(d // N_DEV if a == ax else d for a, d in enumerate(shape))


SMALL_ROWS = -(-sum(SMALL.values()) // (LANE * 8)) * 8


def _pack(flats, rows):
    v = jnp.concatenate([f.reshape(-1) for f in flats])
    return jnp.pad(v, (0, rows * LANE - v.shape[0])).reshape(rows, LANE)


def _unpack(packed, shapes):
    v = packed.reshape(-1)
    out, o = [], 0
    for s in shapes:
        n = math.prod(s)
        out.append(v[o:o + n].reshape(s))
        o += n
    return out


def _to_chunks(name, full):
    shape, ax = BIG[name]
    if ax == 0:
        return full.reshape((N_DEV,) + _shard_shape(name))
    k, n = shape
    return full.reshape(k, N_DEV, n // N_DEV).transpose(1, 0, 2)


def _from_shards(name, shards):
    shape, ax = BIG[name]
    if ax == 0:
        return shards.reshape(shape)
    return shards.transpose(1, 0, 2).reshape(shape)


def _peer(k):
    x, y, c = lax.axis_index("x"), lax.axis_index("y"), lax.axis_index("c")
    px = 1 - x if k & 4 else x
    py = 1 - y if k & 2 else y
    pc = 1 - c if k & 1 else c
    return (px, py, pc), 4 * px + 2 * py + pc


def _gather_ops(x_refs, out_refs, send_sems, recv_sems, local_sems):
    na = len(x_refs)
    chips = (4, 2, 6)

    def copy(a, n, block, to, src=None):
        return pltpu.make_async_remote_copy(
            src_ref=out_refs[a].at[block] if src is None else src, dst_ref=out_refs[a].at[block],
            send_sem=send_sems.at[7 * a + n], recv_sem=recv_sems.at[7 * a + n], device_id=to, device_id_type=MESH)

    def mine():
        me = _peer(0)[1]
        return [pltpu.make_async_copy(x_refs[a], out_refs[a].at[me], local_sems.at[a]) for a in range(na)]

    def first():
        me, sibling = _peer(0)[1], _peer(1)[0]
        out = [copy(a, 0, me, sibling, src=x_refs[a]) for a in range(na)]
        return out + [copy(a, 1 + n, me, _peer(k)[0], src=x_refs[a]) for n, k in enumerate(chips) for a in range(na)]

    def passed():
        sibling = _peer(1)[0]
        return [copy(a, 4 + n, _peer(k)[1], sibling) for n, k in enumerate(chips) for a in range(na)]

    def start():
        for cp in mine() + first():
            cp.start()

    def forward():
        sibling = _peer(1)[0]
        fwd = passed()
        for n, k in enumerate(chips):
            for a in range(na):
                copy(a, 1 + n, _peer(k)[1], sibling).wait_recv()
                fwd[n * na + a].start()

    def finish():
        sibling = _peer(1)[0]
        for a in range(na):
            copy(a, 0, _peer(1)[1], sibling).wait_recv()
        for n, k in enumerate(chips):
            for a in range(na):
                copy(a, 4 + n, _peer(k | 1)[1], sibling).wait_recv()
        for cp in first() + passed():
            cp.wait_send()
        for cp in mine():
            cp.wait()

    return start, forward, finish


def _gather_scratch(na):
    return [pltpu.SemaphoreType.DMA((7 * na,)), pltpu.SemaphoreType.DMA((7 * na,)), pltpu.SemaphoreType.DMA((na,))]


def _all_gather(shards):
    na = len(shards)

    def body(*refs):
        for step in _gather_ops(refs[:na], refs[na:2 * na], *refs[2 * na:]):
            step()

    return pl.pallas_call(
        body, name="all_gather_weights", out_shape=[_sds((N_DEV,) + s.shape, s.dtype) for s in shards],
        in_specs=[ANY] * na, out_specs=[ANY] * na, scratch_shapes=_gather_scratch(na),
    )(*shards)


def _exchange(chunks, small):
    na = len(chunks) + 1

    def body(*refs):
        for step in _exchange_ops(refs[:na], refs[na:2 * na], *refs[2 * na:], whole=(na - 1,)):
            step()

    arrays = list(chunks) + [small]
    return pl.pallas_call(
        body, name="exchange_grads",
        out_shape=[_sds(c.shape, c.dtype) for c in chunks] + [_sds((N_DEV,) + small.shape, small.dtype)],
        in_specs=[ANY] * na, out_specs=[ANY] * na, scratch_shapes=_gather_scratch(na),
    )(*arrays)


def _exchange_ops(in_refs, out_refs, send_sems, recv_sems, local_sems, whole=()):
    na = len(in_refs)

    def src(a, idx):
        return in_refs[a] if a in whole else in_refs[a].at[idx]

    def own():
        me = _peer(0)[1]
        return [pltpu.make_async_copy(src(a, me), out_refs[a].at[me], local_sems.at[a]) for a in range(na)]

    def copy(a, k, sending):
        me = _peer(0)[1]
        to, idx = _peer(k)
        return pltpu.make_async_remote_copy(
            src_ref=src(a, idx if sending else me), dst_ref=out_refs[a].at[me if sending else idx],
            send_sem=send_sems.at[7 * a + k - 1], recv_sem=recv_sems.at[7 * a + k - 1],
            device_id=to, device_id_type=MESH)

    def sent():
        return [copy(a, k, True) for k in range(1, N_DEV) for a in range(na)]

    def start():
        for cp in own() + sent():
            cp.start()

    def finish():
        for k in range(1, N_DEV):
            for a in range(na):
                copy(a, k, False).wait_recv()
        for cp in sent():
            cp.wait_send()
        for cp in own():
            cp.wait()

    return start, finish


def _reduce_adamw(name, recv, w, m, v):
    rows, cols = w.shape
    blk = ADAM_ROWS if rows % ADAM_ROWS == 0 else rows
    c1 = 1.0 - ADAM_B1 ** ADAM_STEP
    c2 = 1.0 - ADAM_B2 ** ADAM_STEP

    def body(r_ref, w_ref, m_ref, v_ref, g_ref, d_ref, nm_ref, nv_ref):
        g = r_ref[0].astype(F32)
        for s in range(1, N_DEV):
            g = g + r_ref[s].astype(F32)
        g_ref[...] = g
        m_ = ADAM_B1 * m_ref[...] + (1.0 - ADAM_B1) * g
        v_ = ADAM_B2 * v_ref[...] + (1.0 - ADAM_B2) * (g * g)
        nm_ref[...] = m_
        nv_ref[...] = v_
        d_ref[...] = -ADAM_LR * ((m_ / c1) / (jnp.sqrt(v_ / c2) + ADAM_EPS) + ADAM_WD * w_ref[...])

    spec = _rowspec(blk, cols)
    return pl.pallas_call(
        body, name="reduce_adamw_" + name, grid=(rows // blk,),
        in_specs=[pl.BlockSpec((N_DEV, blk, cols), lambda i: (0, i, 0)), spec, spec, spec],
        out_specs=[spec] * 4, out_shape=[_sds((rows, cols), F32)] * 4,
        compiler_params=_cp(("arbitrary",)),
    )(recv, w, m, v)


def kernel(x, meta_tokens, norm_mix_pre, w_in, q_a_norm, w_q_up, kv_a_norm, w_kv_up, conv_w, conv_b, dt_bias, a_log, d_skip, ssm_norm, w_out, norm_mix_post, norm_mlp_pre, w_mlp_up, w_mlp_down, norm_mlp_post, loss_target, m_meta_tokens, m_norm_mix_pre, m_w_in, m_q_a_norm, m_w_q_up, m_kv_a_norm, m_w_kv_up, m_conv_w, m_conv_b, m_dt_bias, m_a_log, m_d_skip, m_ssm_norm, m_w_out, m_norm_mix_post, m_norm_mlp_pre, m_w_mlp_up, m_w_mlp_down, m_norm_mlp_post, v_meta_tokens, v_norm_mix_pre, v_w_in, v_q_a_norm, v_w_q_up, v_kv_a_norm, v_w_kv_up, v_conv_w, v_conv_b, v_dt_bias, v_a_log, v_d_skip, v_ssm_norm, v_w_out, v_norm_mix_post, v_norm_mlp_pre, v_w_mlp_up, v_w_mlp_down, v_norm_mlp_post):
    w = dict(meta_tokens=meta_tokens, norm_mix_pre=norm_mix_pre, w_in=w_in, q_a_norm=q_a_norm, w_q_up=w_q_up,
             kv_a_norm=kv_a_norm, w_kv_up=w_kv_up, conv_w=conv_w, conv_b=conv_b, dt_bias=dt_bias, a_log=a_log,
             d_skip=d_skip, ssm_norm=ssm_norm, w_out=w_out, norm_mix_post=norm_mix_post, norm_mlp_pre=norm_mlp_pre,
             w_mlp_up=w_mlp_up, w_mlp_down=w_mlp_down, norm_mlp_post=norm_mlp_post)
    m = dict(meta_tokens=m_meta_tokens, norm_mix_pre=m_norm_mix_pre, w_in=m_w_in, q_a_norm=m_q_a_norm, w_q_up=m_w_q_up,
             kv_a_norm=m_kv_a_norm, w_kv_up=m_w_kv_up, conv_w=m_conv_w, conv_b=m_conv_b, dt_bias=m_dt_bias,
             a_log=m_a_log, d_skip=m_d_skip, ssm_norm=m_ssm_norm, w_out=m_w_out, norm_mix_post=m_norm_mix_post,
             norm_mlp_pre=m_norm_mlp_pre, w_mlp_up=m_w_mlp_up, w_mlp_down=m_w_mlp_down, norm_mlp_post=m_norm_mlp_post)
    v = dict(meta_tokens=v_meta_tokens, norm_mix_pre=v_norm_mix_pre, w_in=v_w_in, q_a_norm=v_q_a_norm, w_q_up=v_w_q_up,
             kv_a_norm=v_kv_a_norm, w_kv_up=v_w_kv_up, conv_w=v_conv_w, conv_b=v_conv_b, dt_bias=v_dt_bias,
             a_log=v_a_log, d_skip=v_d_skip, ssm_norm=v_ssm_norm, w_out=v_w_out, norm_mix_post=v_norm_mix_post,
             norm_mlp_pre=v_norm_mlp_pre, w_mlp_up=v_w_mlp_up, w_mlp_down=v_w_mlp_down, norm_mlp_post=v_norm_mlp_post)
    big_names = [n for n in WEIGHT_ORDER if n in BIG]
    small_names = [n for n in WEIGHT_ORDER if n in SMALL]
    shard = lambda d, n: d[n].reshape(_shard_shape(n))

    f32_names = ("conv_w", "meta_tokens")
    early = [n for n in big_names if n not in LATE]
    gathered = _all_gather([shard(w, n).astype(F32 if n in f32_names else BF16) for n in early])
    p = {n: w[n].reshape(-1) for n in small_names}
    p.update({n: _from_shards(n, s) for n, s in zip(early, gathered)})
    head = jnp.concatenate([jnp.zeros((PADF, D_MODEL), F32), p["meta_tokens"]], axis=0)
    f = _local_forward(head, x[0], loss_target[0], p, late={n: shard(w, n).astype(BF16) for n in LATE})
    dx, g, recv_late = _local_backward(head, x[0], f, exchange_late=True)
    grad_x = dx[None]
    loss = lax.psum(f["loss"], ("x", "y", "c"))

    small = _pack([g[n] for n in small_names], SMALL_ROWS)
    *recv_early, recv_small = _exchange([_to_chunks(n, g[n]).astype(BF16) for n in early], small)
    recv_of = {**dict(zip(early, recv_early)), **dict(zip(LATE, recv_late))}

    outs = {}
    kinds = ("grad", "delta", "new_m", "new_v")
    for n, recv in ((n, recv_of[n]) for n in big_names):
        for kind, arr in zip(kinds, _reduce_adamw(n, recv, shard(w, n), shard(m, n), shard(v, n))):
            outs[kind, n] = arr.reshape(w[n].shape)
    packed = [_pack([d[n] for n in small_names], SMALL_ROWS) for d in (w, m, v)]
    for kind, arr in zip(kinds, _reduce_adamw("small", recv_small, *packed)):
        for n, val in zip(small_names, _unpack(arr, [(SMALL[n],) for n in small_names])):
            outs[kind, n] = val.reshape(w[n].shape)
    return (loss, grad_x) + tuple(outs[kind, n] for kind in ("grad", "delta", "new_m", "new_v") for n in WEIGHT_ORDER)
```

```python
import math

import jax
import jax.numpy as jnp
import numpy as np
from jax import lax
from jax.experimental import pallas as pl
from jax.experimental.pallas import tpu as pltpu

F32 = jnp.float32
BF16 = jnp.bfloat16

D_MODEL = 1024
N_META = 16
EPS = 1e-6
ATT_HEADS = 8
Q_LORA = 384
KV_LORA = 256
QK_NOPE = 128
QK_ROPE = 64
V_HEAD = 128
ROPE_THETA = 10000.0
SSM_HEADS = 16
SSM_HEAD_DIM = 64
SSM_WIDTH = 1024
SSM_GROUPS = 2
SSM_STATE = 128
CONV_K = 4
CHUNK = 128
CONV_DIM = 1536
D_FF = 4096
IN_SPLITS = (Q_LORA, KV_LORA, QK_ROPE, SSM_WIDTH, CONV_DIM, SSM_HEADS)
IN_WIDTH = sum(IN_SPLITS)
ADAM_LR, ADAM_B1, ADAM_B2, ADAM_EPS, ADAM_WD, ADAM_STEP = 0.001, 0.9, 0.999, 1e-08, 0.01, 10

LANE = 128
ROWB = 512
PADF = ROWB - N_META
HEADW = 256
PC_Q, PC_KV, PC_KR, PC_Z, PC_XBC, PC_DT, PROJ_W = 0, 384, 640, 896, 1920, 3456, 3584
NEG = -1e30
N_DEV = 8
VMEM_LIMIT = 56 * 1024 * 1024
MESH = pl.DeviceIdType.MESH


def _cp(sem, vmem=VMEM_LIMIT, **kw):
    return pltpu.CompilerParams(dimension_semantics=sem, vmem_limit_bytes=vmem, **kw)


def _dot(a, b, dims=((1,), (0,))):
    return lax.dot_general(a, b, (dims, ((), ())), preferred_element_type=F32)


def _bdot(a, b, dims=((1,), (0,))):
    return _dot(a.astype(BF16), b.astype(BF16), dims)


NT = ((1,), (1,))


def _rms_fwd(x, w):
    r = lax.rsqrt(jnp.mean(x * x, axis=-1, keepdims=True) + EPS)
    return (x * r) * w


def _rms_bwd(x, w, dy):
    r = lax.rsqrt(jnp.mean(x * x, axis=-1, keepdims=True) + EPS)
    xh = x * r
    g = dy * w
    dx = r * (g - xh * jnp.mean(g * xh, axis=-1, keepdims=True))
    return dx, dy * xh


def _sigmoid(x):
    return 0.5 * jnp.tanh(0.5 * x) + 0.5


def _colsum8(x):
    t, c = x.shape
    return jnp.sum(x.reshape(t // 8, 8, c), axis=0)


def _rowspec(t, c, cb=0):
    return pl.BlockSpec((t, c), lambda i: (i, cb))


def _fullspec(shape):
    n = len(shape)
    return pl.BlockSpec(shape, lambda i: (0,) * n)


def _sds(shape, dt):
    return jax.ShapeDtypeStruct(shape, dt)


def _acc(ref, val):
    @pl.when(pl.program_id(0) == 0)
    def _():
        ref[...] = jnp.zeros_like(ref)

    ref[...] += val


def _xspec():
    return pl.BlockSpec((ROWB, D_MODEL), lambda i: (jnp.maximum(i - 1, 0), 0))


def _h_block(head_ref, x_ref):
    return jnp.where(pl.program_id(0) == 0, head_ref[...], x_ref[...])


def _norm_in_proj(head, x, g, w_all):
    lp = head.shape[0] + x.shape[0]

    def body(head_ref, x_ref, g_ref, w_ref, hn_ref, cq_ref, ckv_ref, kr_ref, z_ref, xbc_ref, dt_ref):
        hn = _rms_fwd(_h_block(head_ref, x_ref), g_ref[...]).astype(BF16)
        hn_ref[...] = hn
        p = _dot(hn, w_ref[...])
        cq_ref[...] = p[:, PC_Q:PC_KV]
        ckv_ref[...] = p[:, PC_KV:PC_KR]
        kr_ref[...] = p[:, PC_KR:PC_Z]
        z_ref[...] = p[:, PC_Z:PC_XBC]
        xbc_ref[...] = p[:, PC_XBC:PC_DT]
        dt_ref[...] = p[:, PC_DT:PROJ_W]

    widths = (Q_LORA, KV_LORA, HEADW, SSM_WIDTH, CONV_DIM, LANE)
    return pl.pallas_call(
        body, name="norm_in_proj", grid=(lp // ROWB,),
        in_specs=[_fullspec((ROWB, D_MODEL)), _xspec(), _fullspec((1, D_MODEL)), _fullspec((D_MODEL, PROJ_W))],
        out_specs=[_rowspec(ROWB, D_MODEL)] + [_rowspec(ROWB, w) for w in widths],
        out_shape=[_sds((lp, D_MODEL), BF16)] + [_sds((lp, w), F32) for w in widths],
        compiler_params=_cp(("arbitrary",)),
    )(head, x, g, w_all)


def _rope(x, cos, sa, sb):
    w = x.shape[1]
    return x * cos + pltpu.roll(x, w - 32, 1) * sa + pltpu.roll(x, 32, 1) * sb


def _rope_t(dy, cos, sa, sb):
    w = dy.shape[1]
    return dy * cos + pltpu.roll(dy * sa, 32, 1) + pltpu.roll(dy * sb, w - 32, 1)


def _qkv(cq, ckv, kr, cos, sa, sb, gq, gkv, wq, wk, wv):
    lp = cq.shape[0]
    qw = ATT_HEADS * HEADW
    half = HEADW // 2
    assert half == QK_NOPE == V_HEAD == LANE

    def body(cq_ref, ckv_ref, kr_ref, cos_ref, sa_ref, sb_ref, gq_ref, gkv_ref, wq_ref, wk_ref, wv_ref,
             q_ref, k_ref, v_ref, cqn_ref, ckvn_ref):
        tabs = [cos_ref[...], sa_ref[...], sb_ref[...]]
        cqn = _rms_fwd(cq_ref[...], gq_ref[...]).astype(BF16)
        ckvn = _rms_fwd(ckv_ref[...], gkv_ref[...]).astype(BF16)
        cqn_ref[...] = cqn
        ckvn_ref[...] = ckvn
        q = _dot(cqn, wq_ref[...])
        kn = _dot(ckvn, wk_ref[...])
        vv = _dot(ckvn, wv_ref[...])
        krope = _rope(kr_ref[:, pl.ds(half, half)], *tabs).astype(BF16)
        ones = jnp.ones((ROWB, half), BF16)
        for hh in range(ATT_HEADS):
            lo, hi, src = pl.ds(hh * HEADW, half), pl.ds(hh * HEADW + half, half), slice(hh * half, (hh + 1) * half)
            q_ref[:, lo] = (q[:, hh * HEADW:hh * HEADW + half] * Q_PRESCALE).astype(BF16)
            q_ref[:, hi] = (_rope(q[:, hh * HEADW + half:(hh + 1) * HEADW], *tabs) * Q_PRESCALE).astype(BF16)
            k_ref[:, lo] = kn[:, src].astype(BF16)
            k_ref[:, hi] = krope
            v_ref[:, lo] = vv[:, src].astype(BF16)
            v_ref[:, hi] = ones

    return pl.pallas_call(
        body, name="qkv", grid=(lp // ROWB,),
        in_specs=[_rowspec(ROWB, Q_LORA), _rowspec(ROWB, KV_LORA), _rowspec(ROWB, HEADW)]
        + [_rowspec(ROWB, half)] * 3
        + [_fullspec((1, Q_LORA)), _fullspec((1, KV_LORA)), _fullspec((Q_LORA, qw)),
           _fullspec((KV_LORA, ATT_HEADS * QK_NOPE)), _fullspec((KV_LORA, ATT_HEADS * V_HEAD))],
        out_specs=[_rowspec(ROWB, qw), _rowspec(ROWB, qw), _rowspec(ROWB, qw),
                   _rowspec(ROWB, Q_LORA), _rowspec(ROWB, KV_LORA)],
        out_shape=[_sds((lp, qw), BF16), _sds((lp, qw), BF16), _sds((lp, qw), BF16),
                   _sds((lp, Q_LORA), BF16), _sds((lp, KV_LORA), BF16)],
        compiler_params=_cp(("arbitrary",)),
    )(cq, ckv, kr, cos, sa, sb, gq, gkv, wq, wk, wv)


ATT_SCALE = (QK_NOPE + QK_ROPE) ** -0.5
LOG2E = 1.4426950408889634
LN2 = 0.6931471805599453
Q_PRESCALE = ATT_SCALE * LOG2E
KVB = 512
META_KEYS = LANE
assert N_META <= META_KEYS


def _att_ok(qrow, krow):
    return (krow <= qrow) & ((krow >= PADF) | (qrow < PADF))


def _lanes(x, n):
    return x if n == 1 else jnp.concatenate([x] * n, axis=1)


def _pair_loop(lo, hi, tile, unrolls=(2,)):
    for u in tuple(unrolls) + (1,):
        n = jnp.maximum(hi - lo, 0)
        trips = n // u

        def many(t, c, u=u, lo=lo):
            for d in range(u):
                tile(lo + u * t + d)
            return c

        lax.fori_loop(0, trips, many, 0)
        lo = lo + trips * u


def _flash_fwd(q, k, v):
    lp = q.shape[0]
    nq = lp // ROWB

    def body(q_ref, k_ref, v_ref, o_ref, lse_ref, acc, m_s):
        i = pl.program_id(1)
        qb = q_ref[...]
        m_s[...] = jnp.full_like(m_s, NEG)
        acc[...] = jnp.zeros_like(acc)

        def tile(j, masked, off=None, nkeys=KVB):
            off = pl.multiple_of(j * KVB, KVB) if off is None else off
            kb = k_ref[pl.ds(off, nkeys), :]
            vb = v_ref[pl.ds(off, nkeys), :]
            s = _dot(qb, kb, NT)
            if masked:
                qrow = i * ROWB + lax.broadcasted_iota(jnp.int32, s.shape, 0)
                krow = off + lax.broadcasted_iota(jnp.int32, s.shape, 1)
                s = jnp.where(_att_ok(qrow, krow), s, NEG)
            m_prev = m_s[...]
            m_new = jnp.maximum(m_prev, jnp.max(s, axis=1, keepdims=True))
            alpha = jnp.exp2(m_prev - m_new)
            p = jnp.exp2(s - _lanes(m_new, nkeys // LANE))
            acc[...] = _lanes(alpha, 2) * acc[...] + _dot(p.astype(BF16), vb)
            m_s[...] = m_new

        def first_tile():
            tile(0, True, off=ROWB - META_KEYS, nkeys=META_KEYS)

        @pl.when(i == 0)
        def _():
            first_tile()

        @pl.when(i > 0)
        def _():
            first_tile()
            tile(i, True)

        _pair_loop(1, i, lambda j: tile(j, False), (16, 8, 4, 2))
        l = acc[:, V_HEAD:]
        o_ref[...] = (acc[:, :V_HEAD] / l).astype(BF16)
        lse_ref[0] = (m_s[...] + jnp.log2(l)).T[0:1, :]

    return pl.pallas_call(
        body, name="flash_fwd", grid=(ATT_HEADS, nq),
        in_specs=[pl.BlockSpec((ROWB, HEADW), lambda h, i: (i, h)),
                  pl.BlockSpec((lp, HEADW), lambda h, i: (0, h)),
                  pl.BlockSpec((lp, HEADW), lambda h, i: (0, h))],
        out_specs=[pl.BlockSpec((ROWB, V_HEAD), lambda h, i: (i, h)),
                   pl.BlockSpec((1, 1, ROWB), lambda h, i: (h, 0, i))],
        out_shape=[_sds((lp, ATT_HEADS * V_HEAD), BF16), _sds((ATT_HEADS, 1, lp), F32)],
        scratch_shapes=[pltpu.VMEM((ROWB, HEADW), F32), pltpu.VMEM((ROWB, LANE), F32)],
        compiler_params=_cp(("arbitrary", "arbitrary")),
    )(q, k, v)


def _silu(x):
    return x * _sigmoid(x)


CONV_ROWS = 64


def _conv_fwd(xbc, cw, cb):
    lp, c = xbc.shape
    t8 = ROWB // 8

    def body(x_ref, prev_ref, w_ref, b_ref, o_ref, buf):
        i = pl.program_id(0)
        buf[pl.ds(0, 8), :] = jnp.where(i > 0, prev_ref[...], 0.0)
        buf[pl.ds(8, ROWB), :] = x_ref[...]

        def strip(s, carry):
            cs = pl.ds(pl.multiple_of(s * LANE, LANE), LANE)
            w, b = w_ref[:, cs], b_ref[:, cs]
            for r0 in range(0, ROWB, CONV_ROWS):
                pre = b + sum(w[kk:kk + 1, :] * buf[pl.ds(8 - (CONV_K - 1) + kk + r0, CONV_ROWS), cs]
                              for kk in range(CONV_K))
                o_ref[pl.ds(r0, CONV_ROWS), cs] = _silu(pre)
            return carry

        lax.fori_loop(0, c // LANE, strip, 0)

    return pl.pallas_call(
        body, name="conv_fwd", grid=(lp // ROWB,),
        in_specs=[_rowspec(ROWB, c), pl.BlockSpec((8, c), lambda i: (jnp.maximum(i * t8 - 1, 0), 0)),
                  _fullspec((8, c)), _fullspec((1, c))],
        out_specs=_rowspec(ROWB, c), out_shape=_sds((lp, c), F32),
        scratch_shapes=[pltpu.VMEM((ROWB + 8, c), F32)],
        compiler_params=_cp(("arbitrary",)),
    )(xbc, xbc, cw, cb)


def _expand_mat():
    r = np.arange(LANE)[:, None]
    c = np.arange(SSM_WIDTH)[None, :]
    return jnp.asarray((c // SSM_HEAD_DIM == r).astype(np.float32))


def _tri_mat():
    i = np.arange(CHUNK)
    return jnp.asarray((i[:, None] >= i[None, :]).astype(np.float32))


def _x3(m, axis):
    return jnp.concatenate([m.astype(BF16)] * 3, axis=axis)


def _split3(x):
    hi = x.astype(BF16)
    r = x - hi.astype(F32)
    mid = r.astype(BF16)
    return hi, mid, (r - mid.astype(F32)).astype(BF16)


def _dot01_r(x, m3):
    return _dot(jnp.concatenate(_split3(x), axis=1), m3)


def _dot01_l(m3, x):
    return _dot(m3, jnp.concatenate(_split3(x), axis=0))


def _ssd_prep(dt_raw, bias_ref, alog_ref, tri3, c, seq_rows):
    rows = c * CHUNK + lax.broadcasted_iota(jnp.int32, (CHUNK, LANE), 0)
    lanes = lax.broadcasted_iota(jnp.int32, (CHUNK, LANE), 1)
    valid = (rows >= PADF) & (rows < PADF + seq_rows) & (lanes < SSM_HEADS)
    dtr = dt_raw + bias_ref[...]
    sp = jnp.maximum(dtr, 0.0) + jnp.log(1.0 + jnp.exp(-jnp.abs(dtr)))
    dt = jnp.where(valid, sp, 0.0)
    a = -jnp.exp(alog_ref[...])
    acol = _dot01_l(tri3, dt * a)
    return dt, a, acol, valid, dtr


def _row16(v):
    return jnp.broadcast_to(v, (16, v.shape[1]))


def _ssd_fwd(xbc_act, dtr, dt_bias, a_log, seq_rows, gather=()):
    lp = xbc_act.shape[0]
    nc = lp // CHUNK
    cps = ROWB // CHUNK
    nsteps = nc // cps
    gw = SSM_WIDTH // SSM_GROUPS
    hpg = SSM_HEADS // SSM_GROUPS

    na = len(gather)

    def body(x_ref, b_ref, c_ref, dtr_ref, bias_ref, alog_ref, tri_ref, tri3_ref, ex3_ref, *rest):
        gin, (y_ref, hp_ref), gout, h_s, sems = rest[:na], rest[na:na + 2], rest[na + 2:2 * na + 2], rest[2 * na + 2], rest[2 * na + 3:]
        step = pl.program_id(0)

        @pl.when(step == 0)
        def _():
            h_s[...] = jnp.zeros_like(h_s)

        if na:
            g_start, g_forward, g_finish = _gather_ops(gin, gout, *sems)
            pl.when(step == 0)(g_start)
            pl.when(step == nsteps // 2)(g_forward)

        ex3 = ex3_ref[...]
        causal = tri_ref[...] > 0.5
        for cc in range(cps):
            rows = pl.ds(cc * CHUNK, CHUNK)
            dt, a, acol, _, _ = _ssd_prep(dtr_ref[rows, :], bias_ref, alog_ref, tri3_ref[...], step * cps + cc, seq_rows)
            arow = acol.T
            dtrow = dt.T
            alast = acol[CHUNK - 1:CHUNK, :]
            e_all = _dot01_r(jnp.exp(acol), ex3)
            wx_all = _dot01_r(jnp.exp(alast - acol) * dt, ex3)
            dec_all = _dot01_r(_row16(jnp.exp(alast)), ex3)[0:1, :]
            hp_ref[cc] = h_s[...]
            for g in range(SSM_GROUPS):
                gs = slice(g * gw, (g + 1) * gw)
                bg = b_ref[rows, g * SSM_STATE:(g + 1) * SSM_STATE]
                cg = c_ref[rows, g * SSM_STATE:(g + 1) * SSM_STATE].astype(BF16)
                xg = x_ref[rows, gs]
                hg = h_s[:, gs]
                gm = _bdot(cg, bg, NT)
                y_off = _bdot(cg, hg) * e_all[:, gs]
                for r in range(hpg):
                    hd = g * hpg + r
                    seg = acol[:, hd:hd + 1] - arow[hd:hd + 1, :]
                    lm = jnp.where(causal, jnp.exp(jnp.where(causal, seg, 0.0)), 0.0)
                    w = gm * lm * dtrow[hd:hd + 1, :]
                    cs = slice(r * SSM_HEAD_DIM, (r + 1) * SSM_HEAD_DIM)
                    y_ref[rows, pl.ds(hd * SSM_HEAD_DIM, SSM_HEAD_DIM)] = _bdot(w, xg[:, cs]) + y_off[:, cs]
                st = _bdot(bg.T, xg * wx_all[:, gs])
                h_s[:, gs] = hg * dec_all[:, gs] + st

        if na:
            pl.when(step == nsteps - 1)(g_finish)

    xs_spec = pl.BlockSpec((ROWB, SSM_WIDTH), lambda c: (c, 0))
    b_spec = pl.BlockSpec((ROWB, 2 * SSM_STATE), lambda c: (c, SSM_WIDTH // (2 * SSM_STATE)))
    c_spec = pl.BlockSpec((ROWB, 2 * SSM_STATE), lambda c: (c, SSM_WIDTH // (2 * SSM_STATE) + 1))
    y, hprev, *gathered = pl.pallas_call(
        body, name="ssd_fwd", grid=(nsteps,),
        in_specs=[xs_spec, b_spec, c_spec, pl.BlockSpec((ROWB, LANE), lambda c: (c, 0)),
                  _fullspec((1, LANE)), _fullspec((1, LANE)), _fullspec((CHUNK, CHUNK)), _fullspec((CHUNK, 3 * CHUNK)),
                  _fullspec((3 * LANE, SSM_WIDTH))] + [ANY] * na,
        out_specs=[xs_spec, pl.BlockSpec((cps, SSM_STATE, SSM_WIDTH), lambda c: (c, 0, 0))] + [ANY] * na,
        out_shape=[_sds((lp, SSM_WIDTH), F32), _sds((nc, SSM_STATE, SSM_WIDTH), F32)]
        + [_sds((N_DEV,) + s.shape, s.dtype) for s in gather],
        scratch_shapes=[pltpu.VMEM((SSM_STATE, SSM_WIDTH), F32)] + (_gather_scratch(na) if na else []),
        compiler_params=_cp(("arbitrary",)),
    )(xbc_act, xbc_act, xbc_act, dtr, dt_bias, a_log, _tri_mat(), _x3(_tri_mat(), 1), _x3(_expand_mat(), 0), *gather)
    return y, hprev, gathered


def _group_mean(x):
    gw = SSM_WIDTH // SSM_GROUPS
    parts = [jnp.broadcast_to(jnp.mean(x[:, g * gw:(g + 1) * gw], axis=-1, keepdims=True), (x.shape[0], gw))
             for g in range(SSM_GROUPS)]
    return jnp.concatenate(parts, axis=1)


def _ssd_post(y, xbc_act, z, dskip, gnorm):
    lp = y.shape[0]

    def body(y_ref, x_ref, z_ref, d_ref, g_ref, o_ref):
        z_ = z_ref[...]
        gt = (y_ref[...] + d_ref[...] * x_ref[...]) * _silu(z_)
        r = lax.rsqrt(_group_mean(gt * gt) + EPS)
        o_ref[...] = ((gt * r) * g_ref[...]).astype(BF16)

    return pl.pallas_call(
        body, name="ssd_post", grid=(lp // ROWB,),
        in_specs=[_rowspec(ROWB, SSM_WIDTH)] * 3 + [_fullspec((1, SSM_WIDTH))] * 2,
        out_specs=_rowspec(ROWB, SSM_WIDTH), out_shape=_sds((lp, SSM_WIDTH), BF16),
        compiler_params=_cp(("arbitrary",)),
    )(y, xbc_act, z, dskip, gnorm)


def _out_proj(att, ssm, head, x, w_out, g_post):
    lp = att.shape[0]

    def body(a_ref, s_ref, head_ref, x_ref, w_ref, g_ref, mix_ref, h1_ref):
        mix = _dot(a_ref[...], w_ref[pl.ds(0, 1024), :]) + _dot(s_ref[...], w_ref[pl.ds(1024, 1024), :])
        mix_ref[...] = mix
        h1_ref[...] = _h_block(head_ref, x_ref) + _rms_fwd(mix, g_ref[...])

    return pl.pallas_call(
        body, name="out_proj", grid=(lp // ROWB,),
        in_specs=[_rowspec(ROWB, 1024)] * 2 + [_fullspec((ROWB, D_MODEL)), _xspec(), _fullspec((2048, D_MODEL)),
                                               _fullspec((1, D_MODEL))],
        out_specs=[_rowspec(ROWB, D_MODEL)] * 2, out_shape=[_sds((lp, D_MODEL), F32)] * 2,
        compiler_params=_cp(("arbitrary",)),
    )(att, ssm, head, x, w_out, g_post)


def _resident(w_hbm, w_vmem, sem):
    @pl.when(pl.program_id(0) == 0)
    def _():
        cp = pltpu.make_async_copy(w_hbm, w_vmem, sem)
        cp.start()
        cp.wait()


ANY = pl.BlockSpec(memory_space=pl.ANY)


def _mlp_fwd(h1, tgt, w_up, w_down, g_pre, g_post, seq_rows):
    lp = h1.shape[0]

    def body(h1_ref, t_ref, wu_hbm, wd_hbm, gpre_ref, gpost_ref, hn2_ref, u_ref, a_ref, f_ref, dh2_ref, loss_ref,
             wu, wd, sems):
        _resident(wu_hbm, wu, sems.at[0])
        _resident(wd_hbm, wd, sems.at[1])
        i = pl.program_id(0)
        h1_ = h1_ref[...]
        hn2 = _rms_fwd(h1_, gpre_ref[...]).astype(BF16)
        hn2_ref[...] = hn2
        u = jnp.maximum(_dot(hn2, wu[...]), 0.0)
        u_ref[...] = u.astype(BF16)
        a = (u * u).astype(BF16)
        a_ref[...] = a
        f = _dot(a, wd[...])
        f_ref[...] = f
        h2 = h1_ + _rms_fwd(f, gpost_ref[...])
        rows = i * ROWB + lax.broadcasted_iota(jnp.int32, (ROWB, 1), 0)
        real = (rows >= PADF + N_META) & (rows < PADF + seq_rows)
        err = jnp.where(real, h2 - t_ref[...], 0.0)
        dh2_ref[...] = err * (1.0 / D_MODEL)
        _acc(loss_ref, _colsum8(err * err))

    return pl.pallas_call(
        body, name="mlp_fwd", grid=(lp // ROWB,),
        in_specs=[_rowspec(ROWB, D_MODEL), _xspec()] + [ANY, ANY] + [_fullspec((1, D_MODEL))] * 2,
        out_specs=[_rowspec(ROWB, D_MODEL), _rowspec(ROWB, D_FF), _rowspec(ROWB, D_FF)] + [_rowspec(ROWB, D_MODEL)] * 2
        + [_fullspec((8, D_MODEL))],
        out_shape=[_sds((lp, D_MODEL), BF16), _sds((lp, D_FF), BF16), _sds((lp, D_FF), BF16), _sds((lp, D_MODEL), F32),
                   _sds((lp, D_MODEL), F32), _sds((8, D_MODEL), F32)],
        scratch_shapes=[pltpu.VMEM((D_MODEL, D_FF), BF16), pltpu.VMEM((D_FF, D_MODEL), BF16), pltpu.SemaphoreType.DMA((2,))],
        compiler_params=_cp(("arbitrary",)),
    )(h1, tgt, w_up, w_down, g_pre, g_post)


def _pad_cols(w, width):
    return jnp.pad(w, ((0, 0), (0, width - w.shape[1])))


def _layout_weights(w_in, w_q_up, w_kv_up):
    o = np.cumsum((0,) + IN_SPLITS)
    pieces = [w_in[:, o[k]:o[k + 1]] for k in range(6)]
    kr = jnp.pad(pieces[2], ((0, 0), (QK_NOPE, HEADW - QK_NOPE - QK_ROPE)))
    w_all = jnp.concatenate([pieces[0], pieces[1], kr, pieces[3], pieces[4], _pad_cols(pieces[5], LANE)], axis=1)
    wq = jnp.pad(w_q_up.reshape(Q_LORA, ATT_HEADS, QK_NOPE + QK_ROPE), ((0, 0), (0, 0), (0, HEADW - QK_NOPE - QK_ROPE)))
    wkv = w_kv_up.reshape(KV_LORA, ATT_HEADS, QK_NOPE + V_HEAD)
    wk = jnp.pad(wkv[:, :, :QK_NOPE], ((0, 0), (0, 0), (0, HEADW - QK_NOPE)))
    wv = wkv[:, :, QK_NOPE:]
    return (w_all, wq.reshape(Q_LORA, -1), wk.reshape(KV_LORA, -1), wv.reshape(KV_LORA, -1),
            wkv[:, :, :QK_NOPE].reshape(KV_LORA, -1))


def _rope_tables(lp):
    pos = jnp.maximum(jnp.arange(lp, dtype=jnp.int32) - PADF, 0).astype(F32)
    inv_freq = ROPE_THETA ** (-jnp.arange(0, QK_ROPE, 2, dtype=F32) / QK_ROPE)
    ang = pos[:, None] * inv_freq[None, :]
    cos, sin = jnp.cos(ang), jnp.sin(ang)
    z32, z64 = jnp.zeros((lp, 32), F32), jnp.zeros((lp, 64), F32)
    cos_t = jnp.concatenate([cos, cos, jnp.ones((lp, 64), F32)], axis=1)
    sa = jnp.concatenate([-sin, z32, z64], axis=1)
    sb = jnp.concatenate([z32, sin, z64], axis=1)
    return cos_t, sa, sb


def _row1(v, width=None):
    v = v.reshape(1, -1).astype(F32)
    return v if width is None else _pad_cols(v, width)


LATE = ("w_out", "w_mlp_up", "w_mlp_down")


def _local_forward(head, x, tgt, p, late=None):
    assert head.shape[0] == ROWB and x.shape[0] % ROWB == 0
    lp = ROWB + x.shape[0]
    seq_rows = N_META + x.shape[0]
    f = {"seq_rows": seq_rows}
    w_all, wq, wk, wv, wkn = _layout_weights(p["w_in"], p["w_q_up"], p["w_kv_up"])
    f.update(w_all=w_all, wq=wq, wk=wk, wv=wv)
    f["hn"], cq, ckv, kr, f["z"], f["xbc"], f["dtr"] = _norm_in_proj(head, x, _row1(p["norm_mix_pre"]), w_all)
    f.update(cq=cq, ckv=ckv)
    f["rope"] = _rope_tables(lp)
    f["q"], f["k"], f["v"], f["cqn"], f["ckvn"] = _qkv(cq, ckv, kr, *f["rope"], _row1(p["q_a_norm"]),
                                                   _row1(p["kv_a_norm"]), wq, wkn, wv)
    f["att"], f["lse"] = _flash_fwd(f["q"], f["k"], f["v"])
    f["cw"] = jnp.pad(p["conv_w"].astype(F32), ((0, 8 - CONV_K), (0, 0)))
    f["xact"] = _conv_fwd(f["xbc"], f["cw"], _row1(p["conv_b"]))
    f["dt_bias"], f["a_log"] = _row1(p["dt_bias"], LANE), _row1(p["a_log"], LANE)
    f["y"], f["hprev"], gathered = _ssd_fwd(f["xact"], f["dtr"], f["dt_bias"], f["a_log"], seq_rows,
                                            gather=[late[n] for n in LATE] if late else ())
    p = {**p, **{n: _from_shards(n, s) for n, s in zip(LATE, gathered)}}
    f["p"] = p
    f["dskip"] = jnp.repeat(p["d_skip"].reshape(-1).astype(F32), SSM_HEAD_DIM).reshape(1, SSM_WIDTH)
    f["ssm"] = _ssd_post(f["y"], f["xact"], f["z"], f["dskip"], _row1(p["ssm_norm"]))
    f["mix"], f["h1"] = _out_proj(f["att"], f["ssm"], head, x, p["w_out"], _row1(p["norm_mix_post"]))
    f["hn2"], f["u"], f["a"], f["f"], f["dh2"], loss8 = _mlp_fwd(
        f["h1"], tgt, p["w_mlp_up"], p["w_mlp_down"], _row1(p["norm_mlp_pre"]), _row1(p["norm_mlp_post"]), seq_rows)
    f["loss"] = 0.5 * jnp.sum(loss8) / D_MODEL
    return f


MLPB = 256


def _mlp_bwd(dh2, f, h1, u, w_up, w_down, g_pre, g_post):
    lp = h1.shape[0]

    def body(dh2_ref, f_ref, h1_ref, u_ref, wu_hbm, wd_hbm, gpre_ref, gpost_ref,
             dh1_ref, du_ref, df_ref, dgpre_ref, dgpost_ref, wu, wd, sems):
        _resident(wu_hbm, wu, sems.at[0])
        _resident(wd_hbm, wd, sems.at[1])
        dh2_ = dh2_ref[...]
        df, dgp = _rms_bwd(f_ref[...], gpost_ref[...], dh2_)
        dfb = df.astype(BF16)
        df_ref[...] = dfb
        da = _dot(dfb, wd[...], NT)
        du = (da * (2.0 * u_ref[...].astype(F32))).astype(BF16)
        du_ref[...] = du
        dhn2 = _dot(du, wu[...], NT)
        dx, dgq = _rms_bwd(h1_ref[...], gpre_ref[...], dhn2)
        dh1_ref[...] = dh2_ + dx
        _acc(dgpre_ref, _colsum8(dgq))
        _acc(dgpost_ref, _colsum8(dgp))

    return pl.pallas_call(
        body, name="mlp_bwd", grid=(lp // MLPB,),
        in_specs=[_rowspec(MLPB, D_MODEL)] * 3 + [_rowspec(MLPB, D_FF), ANY, ANY] + [_fullspec((1, D_MODEL))] * 2,
        out_specs=[_rowspec(MLPB, D_MODEL), _rowspec(MLPB, D_FF), _rowspec(MLPB, D_MODEL),
                   _fullspec((8, D_MODEL)), _fullspec((8, D_MODEL))],
        out_shape=[_sds((lp, D_MODEL), F32), _sds((lp, D_FF), BF16), _sds((lp, D_MODEL), BF16),
                   _sds((8, D_MODEL), F32), _sds((8, D_MODEL), F32)],
        scratch_shapes=[pltpu.VMEM((D_MODEL, D_FF), BF16), pltpu.VMEM((D_FF, D_MODEL), BF16), pltpu.SemaphoreType.DMA((2,))],
        compiler_params=_cp(("arbitrary",)),
    )(dh2, f, h1, u, w_up, w_down, g_pre, g_post)


def _out_bwd(dh1, mix, att, w_out, g_post):
    lp = dh1.shape[0]

    def body(dh1_ref, mix_ref, att_ref, w_ref, g_ref, dmix_ref, datt_ref, dssm_ref, dg_ref, dl_ref):
        dmix, dg = _rms_bwd(mix_ref[...], g_ref[...], dh1_ref[...])
        dmb = dmix.astype(BF16)
        dmix_ref[...] = dmb
        datt = _dot(dmb, w_ref[pl.ds(0, 1024), :], NT).astype(BF16)
        datt_ref[...] = datt
        dssm_ref[...] = _dot(dmb, w_ref[pl.ds(1024, 1024), :], NT)
        _acc(dg_ref, _colsum8(dg))
        prod = datt.astype(F32) * att_ref[...].astype(F32)
        for hh in range(ATT_HEADS):
            d = jnp.sum(prod[:, hh * V_HEAD:(hh + 1) * V_HEAD], axis=1, keepdims=True)
            dl_ref[hh] = jnp.broadcast_to(d, (ROWB, LANE)).T[0:1, :]

    return pl.pallas_call(
        body, name="out_bwd", grid=(lp // ROWB,),
        in_specs=[_rowspec(ROWB, D_MODEL)] * 3 + [_fullspec((2048, D_MODEL)), _fullspec((1, D_MODEL))],
        out_specs=[_rowspec(ROWB, D_MODEL)] * 3 + [_fullspec((8, D_MODEL)),
                                                   pl.BlockSpec((ATT_HEADS, 1, ROWB), lambda i: (0, 0, i))],
        out_shape=[_sds((lp, D_MODEL), BF16), _sds((lp, 1024), BF16), _sds((lp, 1024), F32), _sds((8, D_MODEL), F32),
                   _sds((ATT_HEADS, 1, lp), F32)],
        compiler_params=_cp(("arbitrary",)),
    )(dh1, mix, att, w_out, g_post)


def _ssd_post_bwd(dssm, y, xact, z, dskip, gnorm):
    lp = y.shape[0]

    def body(do_ref, y_ref, x_ref, z_ref, d_ref, g_ref, dy_ref, dz_ref, dg_ref, dd_ref):
        z_, x_ = z_ref[...], x_ref[...]
        sg = _sigmoid(z_)
        sz = z_ * sg
        y2 = y_ref[...] + d_ref[...] * x_
        gt = y2 * sz
        r = lax.rsqrt(_group_mean(gt * gt) + EPS)
        gh = gt * r
        do = do_ref[...]
        dgh = do * g_ref[...]
        dgt = r * (dgh - gh * _group_mean(dgh * gh))
        dy2 = dgt * sz
        dy_ref[...] = dy2
        dz_ref[...] = (dgt * y2 * (sg * (1.0 + z_ * (1.0 - sg)))).astype(BF16)
        _acc(dg_ref, _colsum8(do * gh))
        _acc(dd_ref, _colsum8(dy2 * x_))

    return pl.pallas_call(
        body, name="ssd_post_bwd", grid=(lp // ROWB,),
        in_specs=[_rowspec(ROWB, SSM_WIDTH)] * 4 + [_fullspec((1, SSM_WIDTH))] * 2,
        out_specs=[_rowspec(ROWB, SSM_WIDTH)] * 2 + [_fullspec((8, SSM_WIDTH))] * 2,
        out_shape=[_sds((lp, SSM_WIDTH), F32), _sds((lp, SSM_WIDTH), BF16), _sds((8, SSM_WIDTH), F32), _sds((8, SSM_WIDTH), F32)],
        compiler_params=_cp(("arbitrary",)),
    )(dssm, y, xact, z, dskip, gnorm)


def _ssd_bwd(dy, xact, dtr, hprev, dt_bias, a_log, dskip, seq_rows, exchange=()):
    lp = xact.shape[0]
    nc = lp // CHUNK
    gw = SSM_WIDTH // SSM_GROUPS
    hpg = SSM_HEADS // SSM_GROUPS
    nb = SSM_WIDTH // (2 * SSM_STATE)
    na = len(exchange)
    cps = ROWB // CHUNK
    nsteps = nc // cps

    def body(dy_ref, x_ref, b_ref, c_ref, dtr_ref, hp_ref, bias_ref, alog_ref, dsk_ref, tri_ref, tri3_ref, trit3_ref,
             ex3_ref, ext3_ref, *rest):
        xin, (dact_ref, ddtr_ref, da_ref, dbias_ref), xout = rest[:na], rest[na:na + 4], rest[na + 4:2 * na + 4]
        dh_s, sems = rest[2 * na + 4], rest[2 * na + 5:]
        step = pl.program_id(0)

        @pl.when(step == 0)
        def _():
            dh_s[...] = jnp.zeros_like(dh_s)
            da_ref[...] = jnp.zeros_like(da_ref)
            dbias_ref[...] = jnp.zeros_like(dbias_ref)

        if na:
            x_start, x_finish = _exchange_ops(xin, xout, *sems)
            pl.when(step == 0)(x_start)

        for lc in reversed(range(cps)):
            rows = pl.ds(lc * CHUNK, CHUNK)
            chunk((nsteps - 1 - step) * cps + lc, dy_ref.at[rows], x_ref.at[rows], b_ref.at[rows], c_ref.at[rows],
                  dtr_ref.at[rows], hp_ref.at[pl.ds(lc, 1)], bias_ref, alog_ref, dsk_ref, tri_ref, tri3_ref, trit3_ref,
                  ex3_ref, ext3_ref, dact_ref.at[rows], ddtr_ref.at[rows], da_ref, dbias_ref, dh_s)

        if na:
            pl.when(step == nsteps - 1)(x_finish)

    def chunk(c, dy_ref, x_ref, b_ref, c_ref, dtr_ref, hp_ref, bias_ref, alog_ref, dsk_ref, tri_ref, tri3_ref, trit3_ref,
              ex3_ref, ext3_ref, dact_ref, ddtr_ref, da_ref, dbias_ref, dh_s):
        tri = tri_ref[...]
        ex3 = ex3_ref[...]
        dt, a, acol, valid, dtr_ = _ssd_prep(dtr_ref[...], bias_ref, alog_ref, tri3_ref[...], c, seq_rows)
        arow = acol.T
        dtrow = dt.T
        alast = acol[CHUNK - 1:CHUNK, :]
        e_all = _dot01_r(jnp.exp(acol), ex3)
        wgt0 = jnp.exp(alast - acol)
        wgt = wgt0 * dt
        wx_all = _dot01_r(wgt, ex3)
        elast = jnp.exp(alast)
        dec_all = _dot01_r(_row16(elast), ex3)[0:1, :]
        causal = tri > 0.5
        upper = tri.T > 0.5
        lane_id = lax.broadcasted_iota(jnp.int32, (1, LANE), 1)
        sub_id = lax.broadcasted_iota(jnp.int32, (CHUNK, 1), 0)
        dacol = jnp.zeros((CHUNK, LANE), F32)
        darowf = jnp.zeros((CHUNK, LANE), F32)
        ddtrowf = jnp.zeros((CHUNK, LANE), F32)
        dwgt = jnp.zeros((CHUNK, LANE), F32)
        delast = jnp.zeros((1, LANE), F32)
        for g in range(SSM_GROUPS):
            gs = slice(g * gw, (g + 1) * gw)
            ext3_g = ext3_ref[g]
            bg = b_ref[:, g * SSM_STATE:(g + 1) * SSM_STATE]
            cg = c_ref[:, g * SSM_STATE:(g + 1) * SSM_STATE]
            bgb, cgb = bg.astype(BF16), cg.astype(BF16)
            xg = x_ref[:, gs]
            dyg = dy_ref[:, gs]
            hg = hp_ref[0, :, gs]
            dhg = dh_s[:, gs]
            hgb, dhgb = hg.astype(BF16), dhg.astype(BF16)
            gm = _dot(cgb, bgb, NT)
            gmt = _dot(bgb, cgb, NT)
            y_off = _dot(cgb, hgb) * e_all[:, gs]
            dy0 = (dyg * e_all[:, gs]).astype(BF16)
            dcg = _dot(dy0, hgb, NT)
            dh_in = _dot(cg.T.astype(BF16), dy0) + dhg * dec_all[:, gs]
            dacol = dacol + _dot01_r(dyg * y_off, ext3_g)
            xw = xg * wx_all[:, gs]
            dxw = _dot(bgb, dhgb)
            dx_state = dxw * wx_all[:, gs]
            dwgt = dwgt + _dot01_r(dxw * xg, ext3_g)
            dbt = _dot(dhgb, xw.astype(BF16), NT)
            hh = _colsum8(dhg * hg)
            hh16 = jnp.concatenate([hh, jnp.zeros_like(hh)], axis=0)
            delast = delast + jnp.sum(_dot01_r(hh16, ext3_g), axis=0, keepdims=True)
            dgm = jnp.zeros((CHUNK, CHUNK), F32)
            for r in range(hpg):
                hd = g * hpg + r
                cs = slice(r * SSM_HEAD_DIM, (r + 1) * SSM_HEAD_DIM)
                acol_r, arow_r = acol[:, hd:hd + 1], arow[hd:hd + 1, :]
                dtrow_r, dtcol_r = dtrow[hd:hd + 1, :], dt[:, hd:hd + 1]
                lm = jnp.where(causal, jnp.exp(jnp.where(causal, acol_r - arow_r, 0.0)), 0.0)
                lmt = jnp.where(upper, jnp.exp(jnp.where(upper, arow_r - acol_r, 0.0)), 0.0)
                wt = gmt * lmt * dtcol_r
                dy_r = dyg[:, cs].astype(BF16)
                dx_r = _dot(wt.astype(BF16), dy_r)
                dw = _dot(dy_r, xg[:, cs].astype(BF16), NT)
                t1 = dw * lm
                dgm = dgm + t1 * dtrow_r
                q1 = t1 * gm
                m = q1 * dtrow_r
                dacol = dacol + jnp.sum(m, axis=1, keepdims=True) * (lane_id == hd).astype(F32)
                darowf = darowf - (sub_id == hd).astype(F32) * jnp.sum(m, axis=0, keepdims=True)
                ddtrowf = ddtrowf + (sub_id == hd).astype(F32) * jnp.sum(q1, axis=0, keepdims=True)
                dact_ref[:, pl.ds(hd * SSM_HEAD_DIM, SSM_HEAD_DIM)] = (
                    dx_r + dx_state[:, cs] + dyg[:, cs] * dsk_ref[:, pl.ds(hd * SSM_HEAD_DIM, SSM_HEAD_DIM)])
            dgmb = dgm.astype(BF16)
            dact_ref[:, pl.ds(SSM_WIDTH + g * SSM_STATE, SSM_STATE)] = dbt.T + _dot(dgm.T.astype(BF16), cgb)
            dact_ref[:, pl.ds(SSM_WIDTH + 2 * SSM_STATE + g * SSM_STATE, SSM_STATE)] = dcg + _dot(dgmb, bgb)
            dh_s[:, gs] = dh_in
        t = dwgt * wgt
        dalast = jnp.sum(t, axis=0, keepdims=True) + delast * elast
        dacol_tot = dacol - t + darowf.T + (sub_id == CHUNK - 1).astype(F32) * dalast
        dda = _dot01_l(trit3_ref[...], dacol_tot)
        ddt = dwgt * wgt0 + ddtrowf.T + dda * a
        ddtr = jnp.where(valid, ddt * _sigmoid(dtr_), 0.0)
        ddtr_ref[...] = ddtr
        da_ref[...] += _colsum8(dda * dt) * a
        dbias_ref[...] += _colsum8(ddtr)

    rev = lambda c: nsteps - 1 - c
    rb = cps * CHUNK
    xs_spec = pl.BlockSpec((rb, SSM_WIDTH), lambda c: (rev(c), 0))
    dact, ddtr, da8, dbias8, *received = pl.pallas_call(
        body, name="ssd_bwd", grid=(nsteps,),
        in_specs=[xs_spec, xs_spec,
                  pl.BlockSpec((rb, 2 * SSM_STATE), lambda c: (rev(c), nb)),
                  pl.BlockSpec((rb, 2 * SSM_STATE), lambda c: (rev(c), nb + 1)),
                  pl.BlockSpec((rb, LANE), lambda c: (rev(c), 0)),
                  pl.BlockSpec((cps, SSM_STATE, SSM_WIDTH), lambda c: (rev(c), 0, 0)),
                  _fullspec((1, LANE)), _fullspec((1, LANE)), _fullspec((1, SSM_WIDTH)),
                  _fullspec((CHUNK, CHUNK)), _fullspec((CHUNK, 3 * CHUNK)), _fullspec((CHUNK, 3 * CHUNK)),
                  _fullspec((3 * LANE, SSM_WIDTH)), _fullspec((SSM_GROUPS, 3 * gw, LANE))] + [ANY] * na,
        out_specs=[pl.BlockSpec((rb, CONV_DIM), lambda c: (rev(c), 0)), pl.BlockSpec((rb, LANE), lambda c: (rev(c), 0)),
                   _fullspec((8, LANE)), _fullspec((8, LANE))] + [ANY] * na,
        out_shape=[_sds((lp, CONV_DIM), F32), _sds((lp, LANE), F32), _sds((8, LANE), F32), _sds((8, LANE), F32)]
        + [_sds(e.shape, e.dtype) for e in exchange],
        scratch_shapes=[pltpu.VMEM((SSM_STATE, SSM_WIDTH), F32)] + (_gather_scratch(na) if na else []),
        compiler_params=_cp(("arbitrary",)),
    )(dy, xact, xact, xact, dtr, hprev, dt_bias, a_log, dskip, _tri_mat(), _x3(_tri_mat(), 1), _x3(_tri_mat().T, 1),
      _x3(_expand_mat(), 0), jnp.stack([_x3(_expand_mat().T[g * gw:(g + 1) * gw], 0) for g in range(SSM_GROUPS)]),
      *exchange)
    return dact, ddtr, da8, dbias8, received


def _conv_bwd(dact, xbc, cw, cb):
    lp, c = xbc.shape
    t8 = ROWB // 8
    nb = lp // ROWB

    def body(d_ref, dnext_ref, x_ref, prev_ref, next_ref, w_ref, b_ref, dx_ref, dw_ref, db_ref, xb, dp):
        i = pl.program_id(0)
        last = i == nb - 1
        xb[pl.ds(0, 8), :] = jnp.where(i > 0, prev_ref[...], 0.0)
        xb[pl.ds(8, ROWB), :] = x_ref[...]
        xb[pl.ds(8 + ROWB, 8), :] = jnp.where(last, 0.0, next_ref[...])

        @pl.when(i == 0)
        def _():
            dw_ref[...] = jnp.zeros_like(dw_ref)
            db_ref[...] = jnp.zeros_like(db_ref)

        sub = lax.broadcasted_iota(jnp.int32, (8, 1), 0)
        x0 = 8 - (CONV_K - 1)

        def strip(s, carry):
            cs = pl.ds(pl.multiple_of(s * LANE, LANE), LANE)
            w, b = w_ref[:, cs], b_ref[:, cs]

            def dpre_rows(r0, n, d):
                xs = [xb[pl.ds(x0 + kk + r0, n), cs] for kk in range(CONV_K)]
                pre = b + sum(w[kk:kk + 1, :] * xs[kk] for kk in range(CONV_K))
                sg = _sigmoid(pre)
                return d * (sg * (1.0 + pre * (1.0 - sg))), xs

            dws = [jnp.zeros((8, LANE), F32) for _ in range(CONV_K)]
            dbs = jnp.zeros((8, LANE), F32)
            for r0 in range(0, ROWB, CONV_ROWS):
                dpre, xs = dpre_rows(r0, CONV_ROWS, d_ref[pl.ds(r0, CONV_ROWS), cs])
                dp[pl.ds(r0, CONV_ROWS), cs] = dpre
                dbs = dbs + _colsum8(dpre)
                for kk in range(CONV_K):
                    dws[kk] = dws[kk] + _colsum8(dpre * xs[kk])
            dp[pl.ds(ROWB, 8), cs] = dpre_rows(ROWB, 8, jnp.where(last, 0.0, dnext_ref[:, cs]))[0]
            dwv = sum(jnp.where(sub == kk, jnp.sum(dws[kk], axis=0, keepdims=True), 0.0) for kk in range(CONV_K))
            dw_ref[:, cs] += dwv
            db_ref[:, cs] += dbs
            for r0 in range(0, ROWB, CONV_ROWS):
                dx = sum(w[kk:kk + 1, :] * dp[pl.ds(CONV_K - 1 - kk + r0, CONV_ROWS), cs] for kk in range(CONV_K))
                dx_ref[pl.ds(r0, CONV_ROWS), cs] = dx.astype(BF16)
            return carry

        lax.fori_loop(0, c // LANE, strip, 0)

    nxt = lambda i: (jnp.minimum((i + 1) * t8, lp // 8 - 1), 0)
    prv = lambda i: (jnp.maximum(i * t8 - 1, 0), 0)
    return pl.pallas_call(
        body, name="conv_bwd", grid=(nb,),
        in_specs=[_rowspec(ROWB, c), pl.BlockSpec((8, c), nxt), _rowspec(ROWB, c), pl.BlockSpec((8, c), prv),
                  pl.BlockSpec((8, c), nxt), _fullspec((8, c)), _fullspec((1, c))],
        out_specs=[_rowspec(ROWB, c), _fullspec((8, c)), _fullspec((8, c))],
        out_shape=[_sds((lp, c), BF16), _sds((8, c), F32), _sds((8, c), F32)],
        scratch_shapes=[pltpu.VMEM((ROWB + 16, c), F32), pltpu.VMEM((ROWB + 8, c), F32)],
        compiler_params=_cp(("arbitrary",)),
    )(dact, dact, xbc, xbc, xbc, cw, cb)


def _flash_bwd(q, k, v, datt, lse_row, delta_row, cos, sa, sb):
    lp = q.shape[0]
    nk = lp // ROWB

    def body(k_ref, v_ref, q_ref, do_ref, lse_ref, dl_ref, cos_ref, sa_ref, sb_ref, dq_ref, dk_ref, dv_ref,
             dq_acc, dk_acc, dv_acc):
        j = pl.program_id(1)

        @pl.when(j == 0)
        def _():
            dq_acc[...] = jnp.zeros_like(dq_acc)

        dk_acc[...] = jnp.zeros_like(dk_acc)
        dv_acc[...] = jnp.zeros_like(dv_acc)

        def tile(i, masked, key0=0, nkeys=ROWB):
            keys = pl.ds(key0, nkeys)
            kb, vb = k_ref[keys, :], v_ref[keys, :]
            off = pl.multiple_of(i * ROWB, ROWB)
            qb = q_ref[pl.ds(off, ROWB), :]
            dob = do_ref[pl.ds(off, ROWB), :]
            lse_r = lse_ref[0, :, pl.ds(off, ROWB)]
            dl_r = dl_ref[0, :, pl.ds(off, ROWB)]
            st = _dot(kb, qb, NT)
            if masked:
                krow = j * ROWB + key0 + lax.broadcasted_iota(jnp.int32, st.shape, 0)
                qrow = i * ROWB + lax.broadcasted_iota(jnp.int32, st.shape, 1)
                st = jnp.where(_att_ok(qrow, krow), st, NEG)
            pt = jnp.exp2(st - lse_r)
            dv_acc[keys, :] += _dot(pt.astype(BF16), dob)
            dpt = _dot(vb, dob, NT)
            dst = (pt * (dpt - dl_r)).astype(BF16)
            dk_acc[keys, :] += _dot(dst, qb)
            dq_acc[pl.ds(off, ROWB), :] += _dot(dst, kb, ((0,), (0,)))

        @pl.when(j == 0)
        def _():
            _pair_loop(0, nk, lambda i: tile(i, True, ROWB - META_KEYS, META_KEYS), (4, 2))

        @pl.when((j > 0) & (j < nk - 1))
        def _():
            tile(j, True)
            tile(j + 1, False)
            _pair_loop(j + 2, nk, lambda i: tile(i, False), (4, 2))

        @pl.when(j == nk - 1)
        def _():
            tile(j, True)

        dk_ref[...] = (dk_acc[...] * LN2).astype(BF16)
        dv_ref[...] = dv_acc[...].astype(BF16)
        dq = dq_acc[pl.ds(pl.multiple_of(j * ROWB, ROWB), ROWB), :] * ATT_SCALE
        half = HEADW // 2
        dq_ref[:, pl.ds(0, half)] = dq[:, :half].astype(BF16)
        dq_ref[:, pl.ds(half, half)] = _rope_t(dq[:, half:], cos_ref[...], sa_ref[...], sb_ref[...]).astype(BF16)

    stat = pl.BlockSpec((1, 1, lp), lambda h, j: (h, 0, 0))
    blk = pl.BlockSpec((ROWB, HEADW), lambda h, j: (j, h))
    tab = pl.BlockSpec((ROWB, HEADW // 2), lambda h, j: (j, 0))
    return pl.pallas_call(
        body, name="flash_bwd", grid=(ATT_HEADS, nk),
        in_specs=[blk, pl.BlockSpec((ROWB, V_HEAD), lambda h, j: (j, 2 * h)),
                  pl.BlockSpec((lp, HEADW), lambda h, j: (0, h)), pl.BlockSpec((lp, V_HEAD), lambda h, j: (0, h)),
                  stat, stat, tab, tab, tab],
        out_specs=[blk, blk, pl.BlockSpec((ROWB, V_HEAD), lambda h, j: (j, h))],
        out_shape=[_sds((lp, ATT_HEADS * HEADW), BF16), _sds((lp, ATT_HEADS * HEADW), BF16),
                   _sds((lp, ATT_HEADS * V_HEAD), BF16)],
        scratch_shapes=[pltpu.VMEM((lp, HEADW), F32), pltpu.VMEM((ROWB, HEADW), F32), pltpu.VMEM((ROWB, V_HEAD), F32)],
        compiler_params=_cp(("arbitrary", "arbitrary")),
    )(k, v, q, datt, lse_row, delta_row, cos, sa, sb)


def _qkv_bwd(dqp, dk, dv, cq, ckv, cos, sa, sb, gq, gkv, wq, wk, wv):
    lp = cq.shape[0]
    qw = ATT_HEADS * HEADW

    def body(dqp_ref, dk_ref, dv_ref, cq_ref, ckv_ref, cos_ref, sa_ref, sb_ref, gq_ref, gkv_ref, wq_ref, wk_ref, wv_ref,
             dcq_ref, dckv_ref, dkr_ref, dgq_ref, dgkv_ref):
        dcq, dgq = _rms_bwd(cq_ref[...], gq_ref[...], _dot(dqp_ref[...], wq_ref[...], NT))
        dcq_ref[...] = dcq.astype(BF16)
        dkb = dk_ref[...]
        half = HEADW // 2
        dksum = sum(dkb[:, hh * HEADW + half:(hh + 1) * HEADW].astype(F32) for hh in range(ATT_HEADS))
        dkr_ref[:, pl.ds(0, half)] = jnp.zeros((ROWB, half), BF16)
        dkr_ref[:, pl.ds(half, half)] = _rope_t(dksum, cos_ref[...], sa_ref[...], sb_ref[...]).astype(BF16)
        dckvn = _dot(dkb, wk_ref[...], NT) + _dot(dv_ref[...], wv_ref[...], NT)
        dckv, dgkv = _rms_bwd(ckv_ref[...], gkv_ref[...], dckvn)
        dckv_ref[...] = dckv.astype(BF16)
        _acc(dgq_ref, _colsum8(dgq))
        _acc(dgkv_ref, _colsum8(dgkv))

    return pl.pallas_call(
        body, name="qkv_bwd", grid=(lp // ROWB,),
        in_specs=[_rowspec(ROWB, qw), _rowspec(ROWB, qw), _rowspec(ROWB, ATT_HEADS * V_HEAD),
                  _rowspec(ROWB, Q_LORA), _rowspec(ROWB, KV_LORA)] + [_rowspec(ROWB, HEADW // 2)] * 3
        + [_fullspec((1, Q_LORA)), _fullspec((1, KV_LORA)), _fullspec((Q_LORA, qw)), _fullspec((KV_LORA, qw)),
           _fullspec((KV_LORA, ATT_HEADS * V_HEAD))],
        out_specs=[_rowspec(ROWB, Q_LORA), _rowspec(ROWB, KV_LORA), _rowspec(ROWB, HEADW),
                   _fullspec((8, Q_LORA)), _fullspec((8, KV_LORA))],
        out_shape=[_sds((lp, Q_LORA), BF16), _sds((lp, KV_LORA), BF16), _sds((lp, HEADW), BF16),
                   _sds((8, Q_LORA), F32), _sds((8, KV_LORA), F32)],
        compiler_params=_cp(("arbitrary",)),
    )(dqp, dk, dv, cq, ckv, cos, sa, sb, gq, gkv, wq, wk, wv)


def _in_bwd(pieces, head, x, dh1, g, w_all):
    lp = dh1.shape[0]
    npc = len(pieces)
    assert sum(pc.shape[1] for pc in pieces) == PROJ_W

    def body(*refs):
        head_ref, x_ref, dh1_ref, g_ref, w_ref, dx_ref, dhead_ref, dg_ref, dp_ref = refs[npc:]
        dp = jnp.concatenate([r[...].astype(BF16) for r in refs[:npc]], axis=1)
        dp_ref[...] = dp
        dx, dg = _rms_bwd(_h_block(head_ref, x_ref), g_ref[...], _dot(dp, w_ref[...], NT))
        dh = dh1_ref[...] + dx

        @pl.when(pl.program_id(0) == 0)
        def _():
            dhead_ref[...] = dh

        @pl.when(pl.program_id(0) > 0)
        def _():
            dx_ref[...] = dh

        _acc(dg_ref, _colsum8(dg))

    return pl.pallas_call(
        body, name="in_bwd", grid=(lp // ROWB,),
        in_specs=[_rowspec(ROWB, pc.shape[1]) for pc in pieces]
        + [_fullspec((ROWB, D_MODEL)), _xspec(), _rowspec(ROWB, D_MODEL), _fullspec((1, D_MODEL)),
           _fullspec((D_MODEL, PROJ_W))],
        out_specs=[_xspec(), _fullspec((ROWB, D_MODEL)), _fullspec((8, D_MODEL)), _rowspec(ROWB, PROJ_W)],
        out_shape=[_sds(x.shape, F32), _sds((ROWB, D_MODEL), F32), _sds((8, D_MODEL), F32), _sds((lp, PROJ_W), BF16)],
        compiler_params=_cp(("arbitrary",)),
    )(*pieces, head, x, dh1, g, w_all)


def _tile_of(n, cap=1024):
    return max(t for t in range(LANE, min(n, cap) + 1, LANE) if n % t == 0)


def _matmul_tn(name, a, b):
    rows, kd = a.shape
    nd = b.shape[1]
    tk, tn = _tile_of(kd), _tile_of(nd)
    rb = 3 * ROWB if rows % (3 * ROWB) == 0 else ROWB

    def body(a_ref, b_ref, o_ref):
        @pl.when(pl.program_id(2) == 0)
        def _():
            o_ref[...] = jnp.zeros_like(o_ref)

        o_ref[...] += _dot(a_ref[...], b_ref[...], ((0,), (0,)))

    return pl.pallas_call(
        body, name=name, grid=(kd // tk, nd // tn, rows // rb),
        in_specs=[pl.BlockSpec((rb, tk), lambda i, j, r: (r, i)), pl.BlockSpec((rb, tn), lambda i, j, r: (r, j))],
        out_specs=pl.BlockSpec((tk, tn), lambda i, j, r: (i, j)), out_shape=_sds((kd, nd), F32),
        compiler_params=_cp(("arbitrary", "arbitrary", "arbitrary")),
    )(a, b)


def _local_backward(head, x, f, exchange_late=False):
    p = f["p"]
    g = {}
    row = lambda v: _row1(v)
    s8 = lambda v: jnp.sum(v, axis=0)
    dh1, du, df, dgpre, dgpost = _mlp_bwd(f["dh2"], f["f"], f["h1"], f["u"], p["w_mlp_up"], p["w_mlp_down"],
                                          row(p["norm_mlp_pre"]), row(p["norm_mlp_post"]))
    g["norm_mlp_pre"], g["norm_mlp_post"] = s8(dgpre), s8(dgpost)
    g["w_mlp_up"] = _matmul_tn("dw_mlp_up", f["hn2"], du)
    g["w_mlp_down"] = _matmul_tn("dw_mlp_down", f["a"], df)
    dmix, datt, dssm, dgmp, delta = _out_bwd(dh1, f["mix"], f["att"], p["w_out"], row(p["norm_mix_post"]))
    g["norm_mix_post"] = s8(dgmp)
    g["w_out"] = jnp.concatenate([_matmul_tn("dw_out_att", f["att"], dmix), _matmul_tn("dw_out_ssm", f["ssm"], dmix)], axis=0)
    dy, dz, dgn, dd = _ssd_post_bwd(dssm, f["y"], f["xact"], f["z"], f["dskip"], row(p["ssm_norm"]))
    g["ssm_norm"] = s8(dgn)
    g["d_skip"] = s8(dd).reshape(SSM_HEADS, SSM_HEAD_DIM).sum(axis=1)
    dact, ddtr, da8, dbias8, received = _ssd_bwd(
        dy, f["xact"], f["dtr"], f["hprev"], f["dt_bias"], f["a_log"], f["dskip"], f["seq_rows"],
        exchange=[_to_chunks(n, g[n]).astype(BF16) for n in LATE] if exchange_late else ())
    g["a_log"], g["dt_bias"] = s8(da8)[:SSM_HEADS], s8(dbias8)[:SSM_HEADS]
    dxbc, dcw8, dcb8 = _conv_bwd(dact, f["xbc"], f["cw"], row(p["conv_b"]))
    g["conv_w"], g["conv_b"] = dcw8[:CONV_K], s8(dcb8)
    dqp, dkb, dv = _flash_bwd(f["q"], f["k"], f["v"], datt, f["lse"], delta, *f["rope"])
    dcq, dckv, dkr, dgq, dgkv = _qkv_bwd(dqp, dkb, dv, f["cq"], f["ckv"], *f["rope"], row(p["q_a_norm"]),
                                         row(p["kv_a_norm"]), f["wq"], f["wk"], f["wv"])
    g["q_a_norm"], g["kv_a_norm"] = s8(dgq), s8(dgkv)
    dwq = _matmul_tn("dw_q_up", f["cqn"], dqp).reshape(Q_LORA, ATT_HEADS, HEADW)
    g["w_q_up"] = dwq[:, :, :QK_NOPE + QK_ROPE].reshape(Q_LORA, -1)
    dwk = _matmul_tn("dw_k_up", f["ckvn"], dkb).reshape(KV_LORA, ATT_HEADS, HEADW)[:, :, :QK_NOPE]
    dwv = _matmul_tn("dw_v_up", f["ckvn"], dv).reshape(KV_LORA, ATT_HEADS, V_HEAD)
    g["w_kv_up"] = jnp.concatenate([dwk, dwv], axis=2).reshape(KV_LORA, -1)
    dx, dhead, dgin, dproj = _in_bwd([dcq, dckv, dkr, dz, dxbc, ddtr], head, x, dh1, row(p["norm_mix_pre"]), f["w_all"])
    g["norm_mix_pre"] = s8(dgin)
    g["meta_tokens"] = dhead[PADF:]
    dwa = _matmul_tn("dw_in", f["hn"], dproj)
    g["w_in"] = jnp.concatenate([dwa[:, PC_Q:PC_KR], dwa[:, PC_KR + QK_NOPE:PC_KR + QK_NOPE + QK_ROPE],
                                 dwa[:, PC_Z:PC_DT + SSM_HEADS]], axis=1)
    return dx, g, received


BIG = {"w_in": ((D_MODEL, IN_WIDTH), 1), "w_q_up": ((Q_LORA, ATT_HEADS * (QK_NOPE + QK_ROPE)), 1),
       "w_kv_up": ((KV_LORA, ATT_HEADS * (QK_NOPE + V_HEAD)), 1), "w_out": ((2 * D_MODEL, D_MODEL), 0),
       "w_mlp_up": ((D_MODEL, D_FF), 1), "w_mlp_down": ((D_FF, D_MODEL), 0), "conv_w": ((CONV_K, CONV_DIM), 1),
       "meta_tokens": ((N_META, D_MODEL), 1)}
SMALL = {"norm_mix_pre": D_MODEL, "q_a_norm": Q_LORA, "kv_a_norm": KV_LORA, "conv_b": CONV_DIM, "dt_bias": SSM_HEADS,
         "a_log": SSM_HEADS, "d_skip": SSM_HEADS, "ssm_norm": SSM_WIDTH, "norm_mix_post": D_MODEL,
         "norm_mlp_pre": D_MODEL, "norm_mlp_post": D_MODEL}
WEIGHT_ORDER = ("meta_tokens", "norm_mix_pre", "w_in", "q_a_norm", "w_q_up", "kv_a_norm", "w_kv_up", "conv_w", "conv_b",
                "dt_bias", "a_log", "d_skip", "ssm_norm", "w_out", "norm_mix_post", "norm_mlp_pre", "w_mlp_up",
                "w_mlp_down", "norm_mlp_post")
ADAM_ROWS = 256


def _shard_shape(name):
    shape, ax = BIG[name]
    return tuple(d // N_DEV if a == ax else d for a, d in enumerate(shape))


SMALL_ROWS = -(-sum(SMALL.values()) // (LANE * 8)) * 8


def _pack(flats, rows):
    v = jnp.concatenate([f.reshape(-1) for f in flats])
    return jnp.pad(v, (0, rows * LANE - v.shape[0])).reshape(rows, LANE)


def _unpack(packed, shapes):
    v = packed.reshape(-1)
    out, o = [], 0
    for s in shapes:
        n = math.prod(s)
        out.append(v[o:o + n].reshape(s))
        o += n
    return out


def _to_chunks(name, full):
    shape, ax = BIG[name]
    if ax == 0:
        return full.reshape((N_DEV,) + _shard_shape(name))
    k, n = shape
    return full.reshape(k, N_DEV, n // N_DEV).transpose(1, 0, 2)


def _from_shards(name, shards):
    shape, ax = BIG[name]
    if ax == 0:
        return shards.reshape(shape)
    return shards.transpose(1, 0, 2).reshape(shape)


def _peer(k):
    x, y, c = lax.axis_index("x"), lax.axis_index("y"), lax.axis_index("c")
    px = 1 - x if k & 4 else x
    py = 1 - y if k & 2 else y
    pc = 1 - c if k & 1 else c
    return (px, py, pc), 4 * px + 2 * py + pc


def _gather_ops(x_refs, out_refs, send_sems, recv_sems, local_sems):
    na = len(x_refs)
    chips = (4, 2, 6)

    def copy(a, n, block, to, src=None):
        return pltpu.make_async_remote_copy(
            src_ref=out_refs[a].at[block] if src is None else src, dst_ref=out_refs[a].at[block],
            send_sem=send_sems.at[7 * a + n], recv_sem=recv_sems.at[7 * a + n], device_id=to, device_id_type=MESH)

    def mine():
        me = _peer(0)[1]
        return [pltpu.make_async_copy(x_refs[a], out_refs[a].at[me], local_sems.at[a]) for a in range(na)]

    def first():
        me, sibling = _peer(0)[1], _peer(1)[0]
        out = [copy(a, 0, me, sibling, src=x_refs[a]) for a in range(na)]
        return out + [copy(a, 1 + n, me, _peer(k)[0], src=x_refs[a]) for n, k in enumerate(chips) for a in range(na)]

    def passed():
        sibling = _peer(1)[0]
        return [copy(a, 4 + n, _peer(k)[1], sibling) for n, k in enumerate(chips) for a in range(na)]

    def start():
        for cp in mine() + first():
            cp.start()

    def forward():
        sibling = _peer(1)[0]
        fwd = passed()
        for n, k in enumerate(chips):
            for a in range(na):
                copy(a, 1 + n, _peer(k)[1], sibling).wait_recv()
                fwd[n * na + a].start()

    def finish():
        sibling = _peer(1)[0]
        for a in range(na):
            copy(a, 0, _peer(1)[1], sibling).wait_recv()
        for n, k in enumerate(chips):
            for a in range(na):
                copy(a, 4 + n, _peer(k | 1)[1], sibling).wait_recv()
        for cp in first() + passed():
            cp.wait_send()
        for cp in mine():
            cp.wait()

    return start, forward, finish


def _gather_scratch(na):
    return [pltpu.SemaphoreType.DMA((7 * na,)), pltpu.SemaphoreType.DMA((7 * na,)), pltpu.SemaphoreType.DMA((na,))]


def _all_gather(shards):
    na = len(shards)

    def body(*refs):
        for step in _gather_ops(refs[:na], refs[na:2 * na], *refs[2 * na:]):
            step()

    return pl.pallas_call(
        body, name="all_gather_weights", out_shape=[_sds((N_DEV,) + s.shape, s.dtype) for s in shards],
        in_specs=[ANY] * na, out_specs=[ANY] * na, scratch_shapes=_gather_scratch(na),
    )(*shards)


def _exchange(chunks, small):
    na = len(chunks) + 1

    def body(*refs):
        for step in _exchange_ops(refs[:na], refs[na:2 * na], *refs[2 * na:], whole=(na - 1,)):
            step()

    arrays = list(chunks) + [small]
    return pl.pallas_call(
        body, name="exchange_grads",
        out_shape=[_sds(c.shape, c.dtype) for c in chunks] + [_sds((N_DEV,) + small.shape, small.dtype)],
        in_specs=[ANY] * na, out_specs=[ANY] * na, scratch_shapes=_gather_scratch(na),
    )(*arrays)


def _exchange_ops(in_refs, out_refs, send_sems, recv_sems, local_sems, whole=()):
    na = len(in_refs)

    def src(a, idx):
        return in_refs[a] if a in whole else in_refs[a].at[idx]

    def own():
        me = _peer(0)[1]
        return [pltpu.make_async_copy(src(a, me), out_refs[a].at[me], local_sems.at[a]) for a in range(na)]

    def copy(a, k, sending):
        me = _peer(0)[1]
        to, idx = _peer(k)
        return pltpu.make_async_remote_copy(
            src_ref=src(a, idx if sending else me), dst_ref=out_refs[a].at[me if sending else idx],
            send_sem=send_sems.at[7 * a + k - 1], recv_sem=recv_sems.at[7 * a + k - 1],
            device_id=to, device_id_type=MESH)

    def sent():
        return [copy(a, k, True) for k in range(1, N_DEV) for a in range(na)]

    def start():
        for cp in own() + sent():
            cp.start()

    def finish():
        for k in range(1, N_DEV):
            for a in range(na):
                copy(a, k, False).wait_recv()
        for cp in sent():
            cp.wait_send()
        for cp in own():
            cp.wait()

    return start, finish


def _reduce_adamw(name, recv, w, m, v):
    rows, cols = w.shape
    blk = ADAM_ROWS if rows % ADAM_ROWS == 0 else rows
    c1 = 1.0 - ADAM_B1 ** ADAM_STEP
    c2 = 1.0 - ADAM_B2 ** ADAM_STEP

    def body(r_ref, w_ref, m_ref, v_ref, g_ref, d_ref, nm_ref, nv_ref):
        g = r_ref[0].astype(F32)
        for s in range(1, N_DEV):
            g = g + r_ref[s].astype(F32)
        g_ref[...] = g
        m_ = ADAM_B1 * m_ref[...] + (1.0 - ADAM_B1) * g
        v_ = ADAM_B2 * v_ref[...] + (1.0 - ADAM_B2) * (g * g)
        nm_ref[...] = m_
        nv_ref[...] = v_
        d_ref[...] = -ADAM_LR * ((m_ / c1) / (jnp.sqrt(v_ / c2) + ADAM_EPS) + ADAM_WD * w_ref[...])

    spec = _rowspec(blk, cols)
    return pl.pallas_call(
        body, name="reduce_adamw_" + name, grid=(rows // blk,),
        in_specs=[pl.BlockSpec((N_DEV, blk, cols), lambda i: (0, i, 0)), spec, spec, spec],
        out_specs=[spec] * 4, out_shape=[_sds((rows, cols), F32)] * 4,
        compiler_params=_cp(("arbitrary",)),
    )(recv, w, m, v)


def kernel(x, meta_tokens, norm_mix_pre, w_in, q_a_norm, w_q_up, kv_a_norm, w_kv_up, conv_w, conv_b, dt_bias, a_log, d_skip, ssm_norm, w_out, norm_mix_post, norm_mlp_pre, w_mlp_up, w_mlp_down, norm_mlp_post, loss_target, m_meta_tokens, m_norm_mix_pre, m_w_in, m_q_a_norm, m_w_q_up, m_kv_a_norm, m_w_kv_up, m_conv_w, m_conv_b, m_dt_bias, m_a_log, m_d_skip, m_ssm_norm, m_w_out, m_norm_mix_post, m_norm_mlp_pre, m_w_mlp_up, m_w_mlp_down, m_norm_mlp_post, v_meta_tokens, v_norm_mix_pre, v_w_in, v_q_a_norm, v_w_q_up, v_kv_a_norm, v_w_kv_up, v_conv_w, v_conv_b, v_dt_bias, v_a_log, v_d_skip, v_ssm_norm, v_w_out, v_norm_mix_post, v_norm_mlp_pre, v_w_mlp_up, v_w_mlp_down, v_norm_mlp_post):
    w = dict(meta_tokens=meta_tokens, norm_mix_pre=norm_mix_pre, w_in=w_in, q_a_norm=q_a_norm, w_q_up=w_q_up,
             kv_a_norm=kv_a_norm, w_kv_up=w_kv_up, conv_w=conv_w, conv_b=conv_b, dt_bias=dt_bias, a_log=a_log,
             d_skip=d_skip, ssm_norm=ssm_norm, w_out=w_out, norm_mix_post=norm_mix_post, norm_mlp_pre=norm_mlp_pre,
             w_mlp_up=w_mlp_up, w_mlp_down=w_mlp_down, norm_mlp_post=norm_mlp_post)
    m = dict(meta_tokens=m_meta_tokens, norm_mix_pre=m_norm_mix_pre, w_in=m_w_in, q_a_norm=m_q_a_norm, w_q_up=m_w_q_up,
             kv_a_norm=m_kv_a_norm, w_kv_up=m_w_kv_up, conv_w=m_conv_w, conv_b=m_conv_b, dt_bias=m_dt_bias,
             a_log=m_a_log, d_skip=m_d_skip, ssm_norm=m_ssm_norm, w_out=m_w_out, norm_mix_post=m_norm_mix_post,
             norm_mlp_pre=m_norm_mlp_pre, w_mlp_up=m_w_mlp_up, w_mlp_down=m_w_mlp_down, norm_mlp_post=m_norm_mlp_post)
    v = dict(meta_tokens=v_meta_tokens, norm_mix_pre=v_norm_mix_pre, w_in=v_w_in, q_a_norm=v_q_a_norm, w_q_up=v_w_q_up,
             kv_a_norm=v_kv_a_norm, w_kv_up=v_w_kv_up, conv_w=v_conv_w, conv_b=v_conv_b, dt_bias=v_dt_bias,
             a_log=v_a_log, d_skip=v_d_skip, ssm_norm=v_ssm_norm, w_out=v_w_out, norm_mix_post=v_norm_mix_post,
             norm_mlp_pre=v_norm_mlp_pre, w_mlp_up=v_w_mlp_up, w_mlp_down=v_w_mlp_down, norm_mlp_post=v_norm_mlp_post)
    big_names = [n for n in WEIGHT_ORDER if n in BIG]
    small_names = [n for n in WEIGHT_ORDER if n in SMALL]
    shard = lambda d, n: d[n].reshape(_shard_shape(n))

    f32_names = ("conv_w", "meta_tokens")
    early = [n for n in big_names if n not in LATE]
    gathered = _all_gather([shard(w, n).astype(F32 if n in f32_names else BF16) for n in early])
    p = {n: w[n].reshape(-1) for n in small_names}
    p.update({n: _from_shards(n, s) for n, s in zip(early, gathered)})
    head = jnp.concatenate([jnp.zeros((PADF, D_MODEL), F32), p["meta_tokens"]], axis=0)
    f = _local_forward(head, x[0], loss_target[0], p, late={n: shard(w, n).astype(BF16) for n in LATE})
    dx, g, recv_late = _local_backward(head, x[0], f, exchange_late=True)
    grad_x = dx[None]
    loss = lax.psum(f["loss"], ("x", "y", "c"))

    small = _pack([g[n] for n in small_names], SMALL_ROWS)
    *recv_early, recv_small = _exchange([_to_chunks(n, g[n]).astype(BF16) for n in early], small)
    recv_of = {**dict(zip(early, recv_early)), **dict(zip(LATE, recv_late))}

    outs = {}
    kinds = ("grad", "delta", "new_m", "new_v")
    for n, recv in ((n, recv_of[n]) for n in big_names):
        for kind, arr in zip(kinds, _reduce_adamw(n, recv, shard(w, n), shard(m, n), shard(v, n))):
            outs[kind, n] = arr.reshape(w[n].shape)
    packed = [_pack([d[n] for n in small_names], SMALL_ROWS) for d in (w, m, v)]
    for kind, arr in zip(kinds, _reduce_adamw("small", recv_small, *packed)):
        for n, val in zip(small_names, _unpack(arr, [(SMALL[n],) for n in small_names])):
            outs[kind, n] = val.reshape(w[n].shape)
    return (loss, grad_x) + tuple(outs[kind, n] for kind in ("grad", "delta", "new_m", "new_v") for n in WEIGHT_ORDER)
```

```python
import math

import jax
import jax.numpy as jnp
import numpy as np
from jax import lax
from jax.experimental import pallas as pl
from jax.experimental.pallas import tpu as pltpu

F32 = jnp.float32
BF16 = jnp.bfloat16

D_MODEL = 1024
N_META = 16
EPS = 1e-6
ATT_HEADS = 8
Q_LORA = 384
KV_LORA = 256
QK_NOPE = 128
QK_ROPE = 64
V_HEAD = 128
ROPE_THETA = 10000.0
SSM_HEADS = 16
SSM_HEAD_DIM = 64
SSM_WIDTH = 1024
SSM_GROUPS = 2
SSM_STATE = 128
CONV_K = 4
CHUNK = 128
CONV_DIM = 1536
D_FF = 4096
IN_SPLITS = (Q_LORA, KV_LORA, QK_ROPE, SSM_WIDTH, CONV_DIM, SSM_HEADS)
IN_WIDTH = sum(IN_SPLITS)
ADAM_LR, ADAM_B1, ADAM_B2, ADAM_EPS, ADAM_WD, ADAM_STEP = 0.001, 0.9, 0.999, 1e-08, 0.01, 10

LANE = 128
ROWB = 512
PADF = ROWB - N_META
HEADW = 256
PC_Q, PC_KV, PC_KR, PC_Z, PC_XBC, PC_DT, PROJ_W = 0, 384, 640, 896, 1920, 3456, 3584
NEG = -1e30
N_DEV = 8
VMEM_LIMIT = 56 * 1024 * 1024
MESH = pl.DeviceIdType.MESH


def _cp(sem, vmem=VMEM_LIMIT, **kw):
    return pltpu.CompilerParams(dimension_semantics=sem, vmem_limit_bytes=vmem, **kw)


def _dot(a, b, dims=((1,), (0,))):
    return lax.dot_general(a, b, (dims, ((), ())), preferred_element_type=F32)


def _bdot(a, b, dims=((1,), (0,))):
    return _dot(a.astype(BF16), b.astype(BF16), dims)


NT = ((1,), (1,))


def _rms_fwd(x, w):
    r = lax.rsqrt(jnp.mean(x * x, axis=-1, keepdims=True) + EPS)
    return (x * r) * w


def _rms_bwd(x, w, dy):
    r = lax.rsqrt(jnp.mean(x * x, axis=-1, keepdims=True) + EPS)
    xh = x * r
    g = dy * w
    dx = r * (g - xh * jnp.mean(g * xh, axis=-1, keepdims=True))
    return dx, dy * xh


def _sigmoid(x):
    return 0.5 * jnp.tanh(0.5 * x) + 0.5


def _colsum8(x):
    t, c = x.shape
    return jnp.sum(x.reshape(t // 8, 8, c), axis=0)


def _rowspec(t, c, cb=0):
    return pl.BlockSpec((t, c), lambda i: (i, cb))


def _fullspec(shape):
    n = len(shape)
    return pl.BlockSpec(shape, lambda i: (0,) * n)


def _sds(shape, dt):
    return jax.ShapeDtypeStruct(shape, dt)


def _acc(ref, val):
    @pl.when(pl.program_id(0) == 0)
    def _():
        ref[...] = jnp.zeros_like(ref)

    ref[...] += val


def _xspec():
    return pl.BlockSpec((ROWB, D_MODEL), lambda i: (jnp.maximum(i - 1, 0), 0))


def _h_block(head_ref, x_ref):
    return jnp.where(pl.program_id(0) == 0, head_ref[...], x_ref[...])


def _norm_in_proj(head, x, g, w_all):
    lp = head.shape[0] + x.shape[0]

    def body(head_ref, x_ref, g_ref, w_ref, hn_ref, cq_ref, ckv_ref, kr_ref, z_ref, xbc_ref, dt_ref):
        hn = _rms_fwd(_h_block(head_ref, x_ref), g_ref[...]).astype(BF16)
        hn_ref[...] = hn
        p = _dot(hn, w_ref[...])
        cq_ref[...] = p[:, PC_Q:PC_KV]
        ckv_ref[...] = p[:, PC_KV:PC_KR]
        kr_ref[...] = p[:, PC_KR:PC_Z]
        z_ref[...] = p[:, PC_Z:PC_XBC]
        xbc_ref[...] = p[:, PC_XBC:PC_DT]
        dt_ref[...] = p[:, PC_DT:PROJ_W]

    widths = (Q_LORA, KV_LORA, HEADW, SSM_WIDTH, CONV_DIM, LANE)
    return pl.pallas_call(
        body, name="norm_in_proj", grid=(lp // ROWB,),
        in_specs=[_fullspec((ROWB, D_MODEL)), _xspec(), _fullspec((1, D_MODEL)), _fullspec((D_MODEL, PROJ_W))],
        out_specs=[_rowspec(ROWB, D_MODEL)] + [_rowspec(ROWB, w) for w in widths],
        out_shape=[_sds((lp, D_MODEL), BF16)] + [_sds((lp, w), F32) for w in widths],
        compiler_params=_cp(("arbitrary",)),
    )(head, x, g, w_all)


def _rope(x, cos, sa, sb):
    w = x.shape[1]
    return x * cos + pltpu.roll(x, w - 32, 1) * sa + pltpu.roll(x, 32, 1) * sb


def _rope_t(dy, cos, sa, sb):
    w = dy.shape[1]
    return dy * cos + pltpu.roll(dy * sa, 32, 1) + pltpu.roll(dy * sb, w - 32, 1)


def _qkv(cq, ckv, kr, cos, sa, sb, gq, gkv, wq, wk, wv):
    lp = cq.shape[0]
    qw = ATT_HEADS * HEADW
    half = HEADW // 2
    assert half == QK_NOPE == V_HEAD == LANE

    def body(cq_ref, ckv_ref, kr_ref, cos_ref, sa_ref, sb_ref, gq_ref, gkv_ref, wq_ref, wk_ref, wv_ref,
             q_ref, k_ref, v_ref, cqn_ref, ckvn_ref):
        tabs = [cos_ref[...], sa_ref[...], sb_ref[...]]
        cqn = _rms_fwd(cq_ref[...], gq_ref[...]).astype(BF16)
        ckvn = _rms_fwd(ckv_ref[...], gkv_ref[...]).astype(BF16)
        cqn_ref[...] = cqn
        ckvn_ref[...] = ckvn
        q = _dot(cqn, wq_ref[...])
        kn = _dot(ckvn, wk_ref[...])
        vv = _dot(ckvn, wv_ref[...])
        krope = _rope(kr_ref[:, pl.ds(half, half)], *tabs).astype(BF16)
        ones = jnp.ones((ROWB, half), BF16)
        for hh in range(ATT_HEADS):
            lo, hi, src = pl.ds(hh * HEADW, half), pl.ds(hh * HEADW + half, half), slice(hh * half, (hh + 1) * half)
            q_ref[:, lo] = (q[:, hh * HEADW:hh * HEADW + half] * Q_PRESCALE).astype(BF16)
            q_ref[:, hi] = (_rope(q[:, hh * HEADW + half:(hh + 1) * HEADW], *tabs) * Q_PRESCALE).astype(BF16)
            k_ref[:, lo] = kn[:, src].astype(BF16)
            k_ref[:, hi] = krope
            v_ref[:, lo] = vv[:, src].astype(BF16)
            v_ref[:, hi] = ones

    return pl.pallas_call(
        body, name="qkv", grid=(lp // ROWB,),
        in_specs=[_rowspec(ROWB, Q_LORA), _rowspec(ROWB, KV_LORA), _rowspec(ROWB, HEADW)]
        + [_rowspec(ROWB, half)] * 3
        + [_fullspec((1, Q_LORA)), _fullspec((1, KV_LORA)), _fullspec((Q_LORA, qw)),
           _fullspec((KV_LORA, ATT_HEADS * QK_NOPE)), _fullspec((KV_LORA, ATT_HEADS * V_HEAD))],
        out_specs=[_rowspec(ROWB, qw), _rowspec(ROWB, qw), _rowspec(ROWB, qw),
                   _rowspec(ROWB, Q_LORA), _rowspec(ROWB, KV_LORA)],
        out_shape=[_sds((lp, qw), BF16), _sds((lp, qw), BF16), _sds((lp, qw), BF16),
                   _sds((lp, Q_LORA), BF16), _sds((lp, KV_LORA), BF16)],
        compiler_params=_cp(("arbitrary",)),
    )(cq, ckv, kr, cos, sa, sb, gq, gkv, wq, wk, wv)


ATT_SCALE = (QK_NOPE + QK_ROPE) ** -0.5
LOG2E = 1.4426950408889634
LN2 = 0.6931471805599453
Q_PRESCALE = ATT_SCALE * LOG2E
KVB = 512
META_KEYS = LANE
assert N_META <= META_KEYS


def _att_ok(qrow, krow):
    return (krow <= qrow) & ((krow >= PADF) | (qrow < PADF))


def _lanes(x, n):
    return x if n == 1 else jnp.concatenate([x] * n, axis=1)


def _pair_loop(lo, hi, tile, unrolls=(2,)):
    for u in tuple(unrolls) + (1,):
        n = jnp.maximum(hi - lo, 0)
        trips = n // u

        def many(t, c, u=u, lo=lo):
            for d in range(u):
                tile(lo + u * t + d)
            return c

        lax.fori_loop(0, trips, many, 0)
        lo = lo + trips * u


def _flash_fwd(q, k, v):
    lp = q.shape[0]
    nq = lp // ROWB

    def body(q_ref, k_ref, v_ref, o_ref, lse_ref, acc, m_s):
        i = pl.program_id(1)
        qb = q_ref[...]
        m_s[...] = jnp.full_like(m_s, NEG)
        acc[...] = jnp.zeros_like(acc)

        def tile(j, masked, off=None, nkeys=KVB):
            off = pl.multiple_of(j * KVB, KVB) if off is None else off
            kb = k_ref[pl.ds(off, nkeys), :]
            vb = v_ref[pl.ds(off, nkeys), :]
            s = _dot(qb, kb, NT)
            if masked:
                qrow = i * ROWB + lax.broadcasted_iota(jnp.int32, s.shape, 0)
                krow = off + lax.broadcasted_iota(jnp.int32, s.shape, 1)
                s = jnp.where(_att_ok(qrow, krow), s, NEG)
            m_prev = m_s[...]
            m_new = jnp.maximum(m_prev, jnp.max(s, axis=1, keepdims=True))
            alpha = jnp.exp2(m_prev - m_new)
            p = jnp.exp2(s - _lanes(m_new, nkeys // LANE))
            acc[...] = _lanes(alpha, 2) * acc[...] + _dot(p.astype(BF16), vb)
            m_s[...] = m_new

        def first_tile():
            tile(0, True, off=ROWB - META_KEYS, nkeys=META_KEYS)

        @pl.when(i == 0)
        def _():
            first_tile()

        @pl.when(i > 0)
        def _():
            first_tile()
            tile(i, True)

        _pair_loop(1, i, lambda j: tile(j, False), (16, 8, 4, 2))
        l = acc[:, V_HEAD:]
        o_ref[...] = (acc[:, :V_HEAD] / l).astype(BF16)
        lse_ref[0] = (m_s[...] + jnp.log2(l)).T[0:1, :]

    return pl.pallas_call(
        body, name="flash_fwd", grid=(ATT_HEADS, nq),
        in_specs=[pl.BlockSpec((ROWB, HEADW), lambda h, i: (i, h)),
                  pl.BlockSpec((lp, HEADW), lambda h, i: (0, h)),
                  pl.BlockSpec((lp, HEADW), lambda h, i: (0, h))],
        out_specs=[pl.BlockSpec((ROWB, V_HEAD), lambda h, i: (i, h)),
                   pl.BlockSpec((1, 1, ROWB), lambda h, i: (h, 0, i))],
        out_shape=[_sds((lp, ATT_HEADS * V_HEAD), BF16), _sds((ATT_HEADS, 1, lp), F32)],
        scratch_shapes=[pltpu.VMEM((ROWB, HEADW), F32), pltpu.VMEM((ROWB, LANE), F32)],
        compiler_params=_cp(("arbitrary", "arbitrary")),
    )(q, k, v)


def _silu(x):
    return x * _sigmoid(x)


CONV_ROWS = 64


def _shifted_rows(ref, start, n, cols, offsets):
    win = ref[pl.ds(start, n + 8), cols]
    return [win[o:o + n] if o % 8 == 0 else pltpu.roll(win, n + 8 - o, 0)[0:n] for o in offsets]


def _conv_fwd(xbc, cw, cb):
    lp, c = xbc.shape
    t8 = ROWB // 8

    def body(x_ref, prev_ref, w_ref, b_ref, o_ref, buf):
        i = pl.program_id(0)
        buf[pl.ds(0, 8), :] = jnp.where(i > 0, prev_ref[...], 0.0)
        buf[pl.ds(8, ROWB), :] = x_ref[...]

        def strip(s, carry):
            cs = pl.ds(pl.multiple_of(s * LANE, LANE), LANE)
            w, b = w_ref[:, cs], b_ref[:, cs]
            for r0 in range(0, ROWB, CONV_ROWS):
                taps = _shifted_rows(buf, r0, CONV_ROWS, cs, [8 - (CONV_K - 1) + kk for kk in range(CONV_K)])
                pre = b + sum(w[kk:kk + 1, :] * taps[kk] for kk in range(CONV_K))
                o_ref[pl.ds(r0, CONV_ROWS), cs] = _silu(pre)
            return carry

        lax.fori_loop(0, c // LANE, strip, 0)

    return pl.pallas_call(
        body, name="conv_fwd", grid=(lp // ROWB,),
        in_specs=[_rowspec(ROWB, c), pl.BlockSpec((8, c), lambda i: (jnp.maximum(i * t8 - 1, 0), 0)),
                  _fullspec((8, c)), _fullspec((1, c))],
        out_specs=_rowspec(ROWB, c), out_shape=_sds((lp, c), F32),
        scratch_shapes=[pltpu.VMEM((ROWB + 8, c), F32)],
        compiler_params=_cp(("arbitrary",)),
    )(xbc, xbc, cw, cb)


def _expand_mat():
    r = np.arange(LANE)[:, None]
    c = np.arange(SSM_WIDTH)[None, :]
    return jnp.asarray((c // SSM_HEAD_DIM == r).astype(np.float32))


def _tri_mat():
    i = np.arange(CHUNK)
    return jnp.asarray((i[:, None] >= i[None, :]).astype(np.float32))


def _x3(m, axis):
    return jnp.concatenate([m.astype(BF16)] * 3, axis=axis)


def _split3(x):
    hi = x.astype(BF16)
    r = x - hi.astype(F32)
    mid = r.astype(BF16)
    return hi, mid, (r - mid.astype(F32)).astype(BF16)


def _dot01_r(x, m3):
    return _dot(jnp.concatenate(_split3(x), axis=1), m3)


def _dot01_l(m3, x):
    return _dot(m3, jnp.concatenate(_split3(x), axis=0))


def _ssd_prep(dt_raw, bias_ref, alog_ref, tri3, c, seq_rows):
    rows = c * CHUNK + lax.broadcasted_iota(jnp.int32, (CHUNK, LANE), 0)
    lanes = lax.broadcasted_iota(jnp.int32, (CHUNK, LANE), 1)
    valid = (rows >= PADF) & (rows < PADF + seq_rows) & (lanes < SSM_HEADS)
    dtr = dt_raw + bias_ref[...]
    sp = jnp.maximum(dtr, 0.0) + jnp.log(1.0 + jnp.exp(-jnp.abs(dtr)))
    dt = jnp.where(valid, sp, 0.0)
    a = -jnp.exp(alog_ref[...])
    acol = _dot01_l(tri3, dt * a)
    return dt, a, acol, valid, dtr


def _row16(v):
    return jnp.broadcast_to(v, (16, v.shape[1]))


def _ssd_fwd(xbc_act, dtr, dt_bias, a_log, seq_rows, gather=()):
    lp = xbc_act.shape[0]
    nc = lp // CHUNK
    cps = ROWB // CHUNK
    nsteps = nc // cps
    gw = SSM_WIDTH // SSM_GROUPS
    hpg = SSM_HEADS // SSM_GROUPS

    na = len(gather)

    def body(x_ref, b_ref, c_ref, dtr_ref, bias_ref, alog_ref, tri_ref, tri3_ref, ex3_ref, *rest):
        gin, (y_ref, hp_ref), gout, h_s, sems = rest[:na], rest[na:na + 2], rest[na + 2:2 * na + 2], rest[2 * na + 2], rest[2 * na + 3:]
        step = pl.program_id(0)

        @pl.when(step == 0)
        def _():
            h_s[...] = jnp.zeros_like(h_s)

        if na:
            g_start, g_forward, g_finish = _gather_ops(gin, gout, *sems)
            pl.when(step == 0)(g_start)
            pl.when(step == nsteps // 2)(g_forward)

        ex3 = ex3_ref[...]
        causal = tri_ref[...] > 0.5
        for cc in range(cps):
            rows = pl.ds(cc * CHUNK, CHUNK)
            dt, a, acol, _, _ = _ssd_prep(dtr_ref[rows, :], bias_ref, alog_ref, tri3_ref[...], step * cps + cc, seq_rows)
            arow = acol.T
            dtrow = dt.T
            alast = acol[CHUNK - 1:CHUNK, :]
            e_all = _dot01_r(jnp.exp(acol), ex3)
            wx_all = _dot01_r(jnp.exp(alast - acol) * dt, ex3)
            dec_all = _dot01_r(_row16(jnp.exp(alast)), ex3)[0:1, :]
            hp_ref[cc] = h_s[...]
            for g in range(SSM_GROUPS):
                gs = slice(g * gw, (g + 1) * gw)
                bg = b_ref[rows, g * SSM_STATE:(g + 1) * SSM_STATE]
                cg = c_ref[rows, g * SSM_STATE:(g + 1) * SSM_STATE].astype(BF16)
                xg = x_ref[rows, gs]
                hg = h_s[:, gs]
                gm = _bdot(cg, bg, NT)
                y_off = _bdot(cg, hg) * e_all[:, gs]
                for r in range(hpg):
                    hd = g * hpg + r
                    seg = acol[:, hd:hd + 1] - arow[hd:hd + 1, :]
                    lm = jnp.where(causal, jnp.exp(jnp.where(causal, seg, 0.0)), 0.0)
                    w = gm * lm * dtrow[hd:hd + 1, :]
                    cs = slice(r * SSM_HEAD_DIM, (r + 1) * SSM_HEAD_DIM)
                    y_ref[rows, pl.ds(hd * SSM_HEAD_DIM, SSM_HEAD_DIM)] = _bdot(w, xg[:, cs]) + y_off[:, cs]
                st = _bdot(bg.T, xg * wx_all[:, gs])
                h_s[:, gs] = hg * dec_all[:, gs] + st

        if na:
            pl.when(step == nsteps - 1)(g_finish)

    xs_spec = pl.BlockSpec((ROWB, SSM_WIDTH), lambda c: (c, 0))
    b_spec = pl.BlockSpec((ROWB, 2 * SSM_STATE), lambda c: (c, SSM_WIDTH // (2 * SSM_STATE)))
    c_spec = pl.BlockSpec((ROWB, 2 * SSM_STATE), lambda c: (c, SSM_WIDTH // (2 * SSM_STATE) + 1))
    y, hprev, *gathered = pl.pallas_call(
        body, name="ssd_fwd", grid=(nsteps,),
        in_specs=[xs_spec, b_spec, c_spec, pl.BlockSpec((ROWB, LANE), lambda c: (c, 0)),
                  _fullspec((1, LANE)), _fullspec((1, LANE)), _fullspec((CHUNK, CHUNK)), _fullspec((CHUNK, 3 * CHUNK)),
                  _fullspec((3 * LANE, SSM_WIDTH))] + [ANY] * na,
        out_specs=[xs_spec, pl.BlockSpec((cps, SSM_STATE, SSM_WIDTH), lambda c: (c, 0, 0))] + [ANY] * na,
        out_shape=[_sds((lp, SSM_WIDTH), F32), _sds((nc, SSM_STATE, SSM_WIDTH), F32)]
        + [_sds((N_DEV,) + s.shape, s.dtype) for s in gather],
        scratch_shapes=[pltpu.VMEM((SSM_STATE, SSM_WIDTH), F32)] + (_gather_scratch(na) if na else []),
        compiler_params=_cp(("arbitrary",)),
    )(xbc_act, xbc_act, xbc_act, dtr, dt_bias, a_log, _tri_mat(), _x3(_tri_mat(), 1), _x3(_expand_mat(), 0), *gather)
    return y, hprev, gathered


def _group_mean(x):
    gw = SSM_WIDTH // SSM_GROUPS
    parts = [jnp.broadcast_to(jnp.mean(x[:, g * gw:(g + 1) * gw], axis=-1, keepdims=True), (x.shape[0], gw))
             for g in range(SSM_GROUPS)]
    return jnp.concatenate(parts, axis=1)


def _ssd_post(y, xbc_act, z, dskip, gnorm):
    lp = y.shape[0]

    def body(y_ref, x_ref, z_ref, d_ref, g_ref, o_ref):
        z_ = z_ref[...]
        gt = (y_ref[...] + d_ref[...] * x_ref[...]) * _silu(z_)
        r = lax.rsqrt(_group_mean(gt * gt) + EPS)
        o_ref[...] = ((gt * r) * g_ref[...]).astype(BF16)

    return pl.pallas_call(
        body, name="ssd_post", grid=(lp // ROWB,),
        in_specs=[_rowspec(ROWB, SSM_WIDTH)] * 3 + [_fullspec((1, SSM_WIDTH))] * 2,
        out_specs=_rowspec(ROWB, SSM_WIDTH), out_shape=_sds((lp, SSM_WIDTH), BF16),
        compiler_params=_cp(("arbitrary",)),
    )(y, xbc_act, z, dskip, gnorm)


def _out_proj(att, ssm, head, x, w_out, g_post):
    lp = att.shape[0]

    def body(a_ref, s_ref, head_ref, x_ref, w_ref, g_ref, mix_ref, h1_ref):
        mix = _dot(a_ref[...], w_ref[pl.ds(0, 1024), :]) + _dot(s_ref[...], w_ref[pl.ds(1024, 1024), :])
        mix_ref[...] = mix
        h1_ref[...] = _h_block(head_ref, x_ref) + _rms_fwd(mix, g_ref[...])

    return pl.pallas_call(
        body, name="out_proj", grid=(lp // ROWB,),
        in_specs=[_rowspec(ROWB, 1024)] * 2 + [_fullspec((ROWB, D_MODEL)), _xspec(), _fullspec((2048, D_MODEL)),
                                               _fullspec((1, D_MODEL))],
        out_specs=[_rowspec(ROWB, D_MODEL)] * 2, out_shape=[_sds((lp, D_MODEL), F32)] * 2,
        compiler_params=_cp(("arbitrary",)),
    )(att, ssm, head, x, w_out, g_post)


def _resident(w_hbm, w_vmem, sem):
    @pl.when(pl.program_id(0) == 0)
    def _():
        cp = pltpu.make_async_copy(w_hbm, w_vmem, sem)
        cp.start()
        cp.wait()


ANY = pl.BlockSpec(memory_space=pl.ANY)


def _mlp_fwd(h1, tgt, w_up, w_down, g_pre, g_post, seq_rows):
    lp = h1.shape[0]

    def body(h1_ref, t_ref, wu_hbm, wd_hbm, gpre_ref, gpost_ref, hn2_ref, u_ref, a_ref, f_ref, dh2_ref, loss_ref,
             wu, wd, sems):
        _resident(wu_hbm, wu, sems.at[0])
        _resident(wd_hbm, wd, sems.at[1])
        i = pl.program_id(0)
        h1_ = h1_ref[...]
        hn2 = _rms_fwd(h1_, gpre_ref[...]).astype(BF16)
        hn2_ref[...] = hn2
        u = jnp.maximum(_dot(hn2, wu[...]), 0.0)
        u_ref[...] = u.astype(BF16)
        a = (u * u).astype(BF16)
        a_ref[...] = a
        f = _dot(a, wd[...])
        f_ref[...] = f
        h2 = h1_ + _rms_fwd(f, gpost_ref[...])
        rows = i * ROWB + lax.broadcasted_iota(jnp.int32, (ROWB, 1), 0)
        real = (rows >= PADF + N_META) & (rows < PADF + seq_rows)
        err = jnp.where(real, h2 - t_ref[...], 0.0)
        dh2_ref[...] = err * (1.0 / D_MODEL)
        _acc(loss_ref, _colsum8(err * err))

    return pl.pallas_call(
        body, name="mlp_fwd", grid=(lp // ROWB,),
        in_specs=[_rowspec(ROWB, D_MODEL), _xspec()] + [ANY, ANY] + [_fullspec((1, D_MODEL))] * 2,
        out_specs=[_rowspec(ROWB, D_MODEL), _rowspec(ROWB, D_FF), _rowspec(ROWB, D_FF)] + [_rowspec(ROWB, D_MODEL)] * 2
        + [_fullspec((8, D_MODEL))],
        out_shape=[_sds((lp, D_MODEL), BF16), _sds((lp, D_FF), BF16), _sds((lp, D_FF), BF16), _sds((lp, D_MODEL), F32),
                   _sds((lp, D_MODEL), F32), _sds((8, D_MODEL), F32)],
        scratch_shapes=[pltpu.VMEM((D_MODEL, D_FF), BF16), pltpu.VMEM((D_FF, D_MODEL), BF16), pltpu.SemaphoreType.DMA((2,))],
        compiler_params=_cp(("arbitrary",)),
    )(h1, tgt, w_up, w_down, g_pre, g_post)


def _pad_cols(w, width):
    return jnp.pad(w, ((0, 0), (0, width - w.shape[1])))


def _layout_weights(w_in, w_q_up, w_kv_up):
    o = np.cumsum((0,) + IN_SPLITS)
    pieces = [w_in[:, o[k]:o[k + 1]] for k in range(6)]
    kr = jnp.pad(pieces[2], ((0, 0), (QK_NOPE, HEADW - QK_NOPE - QK_ROPE)))
    w_all = jnp.concatenate([pieces[0], pieces[1], kr, pieces[3], pieces[4], _pad_cols(pieces[5], LANE)], axis=1)
    wq = jnp.pad(w_q_up.reshape(Q_LORA, ATT_HEADS, QK_NOPE + QK_ROPE), ((0, 0), (0, 0), (0, HEADW - QK_NOPE - QK_ROPE)))
    wkv = w_kv_up.reshape(KV_LORA, ATT_HEADS, QK_NOPE + V_HEAD)
    wk = jnp.pad(wkv[:, :, :QK_NOPE], ((0, 0), (0, 0), (0, HEADW - QK_NOPE)))
    wv = wkv[:, :, QK_NOPE:]
    return (w_all, wq.reshape(Q_LORA, -1), wk.reshape(KV_LORA, -1), wv.reshape(KV_LORA, -1),
            wkv[:, :, :QK_NOPE].reshape(KV_LORA, -1))


def _rope_tables(lp):
    pos = jnp.maximum(jnp.arange(lp, dtype=jnp.int32) - PADF, 0).astype(F32)
    inv_freq = ROPE_THETA ** (-jnp.arange(0, QK_ROPE, 2, dtype=F32) / QK_ROPE)
    ang = pos[:, None] * inv_freq[None, :]
    cos, sin = jnp.cos(ang), jnp.sin(ang)
    z32, z64 = jnp.zeros((lp, 32), F32), jnp.zeros((lp, 64), F32)
    cos_t = jnp.concatenate([cos, cos, jnp.ones((lp, 64), F32)], axis=1)
    sa = jnp.concatenate([-sin, z32, z64], axis=1)
    sb = jnp.concatenate([z32, sin, z64], axis=1)
    return cos_t, sa, sb


def _row1(v, width=None):
    v = v.reshape(1, -1).astype(F32)
    return v if width is None else _pad_cols(v, width)


LATE = ("w_out", "w_mlp_up", "w_mlp_down")


def _local_forward(head, x, tgt, p, late=None):
    assert head.shape[0] == ROWB and x.shape[0] % ROWB == 0
    lp = ROWB + x.shape[0]
    seq_rows = N_META + x.shape[0]
    f = {"seq_rows": seq_rows}
    w_all, wq, wk, wv, wkn = _layout_weights(p["w_in"], p["w_q_up"], p["w_kv_up"])
    f.update(w_all=w_all, wq=wq, wk=wk, wv=wv)
    f["hn"], cq, ckv, kr, f["z"], f["xbc"], f["dtr"] = _norm_in_proj(head, x, _row1(p["norm_mix_pre"]), w_all)
    f.update(cq=cq, ckv=ckv)
    f["rope"] = _rope_tables(lp)
    f["q"], f["k"], f["v"], f["cqn"], f["ckvn"] = _qkv(cq, ckv, kr, *f["rope"], _row1(p["q_a_norm"]),
                                                   _row1(p["kv_a_norm"]), wq, wkn, wv)
    f["att"], f["lse"] = _flash_fwd(f["q"], f["k"], f["v"])
    f["cw"] = jnp.pad(p["conv_w"].astype(F32), ((0, 8 - CONV_K), (0, 0)))
    f["xact"] = _conv_fwd(f["xbc"], f["cw"], _row1(p["conv_b"]))
    f["dt_bias"], f["a_log"] = _row1(p["dt_bias"], LANE), _row1(p["a_log"], LANE)
    f["y"], f["hprev"], gathered = _ssd_fwd(f["xact"], f["dtr"], f["dt_bias"], f["a_log"], seq_rows,
                                            gather=[late[n] for n in LATE] if late else ())
    p = {**p, **{n: _from_shards(n, s) for n, s in zip(LATE, gathered)}}
    f["p"] = p
    f["dskip"] = jnp.repeat(p["d_skip"].reshape(-1).astype(F32), SSM_HEAD_DIM).reshape(1, SSM_WIDTH)
    f["ssm"] = _ssd_post(f["y"], f["xact"], f["z"], f["dskip"], _row1(p["ssm_norm"]))
    f["mix"], f["h1"] = _out_proj(f["att"], f["ssm"], head, x, p["w_out"], _row1(p["norm_mix_post"]))
    f["hn2"], f["u"], f["a"], f["f"], f["dh2"], loss8 = _mlp_fwd(
        f["h1"], tgt, p["w_mlp_up"], p["w_mlp_down"], _row1(p["norm_mlp_pre"]), _row1(p["norm_mlp_post"]), seq_rows)
    f["loss"] = 0.5 * jnp.sum(loss8) / D_MODEL
    return f


MLPB = 256


def _mlp_bwd(dh2, f, h1, u, w_up, w_down, g_pre, g_post):
    lp = h1.shape[0]

    def body(dh2_ref, f_ref, h1_ref, u_ref, wu_hbm, wd_hbm, gpre_ref, gpost_ref,
             dh1_ref, du_ref, df_ref, dgpre_ref, dgpost_ref, wu, wd, sems):
        _resident(wu_hbm, wu, sems.at[0])
        _resident(wd_hbm, wd, sems.at[1])
        dh2_ = dh2_ref[...]
        df, dgp = _rms_bwd(f_ref[...], gpost_ref[...], dh2_)
        dfb = df.astype(BF16)
        df_ref[...] = dfb
        da = _dot(dfb, wd[...], NT)
        du = (da * (2.0 * u_ref[...].astype(F32))).astype(BF16)
        du_ref[...] = du
        dhn2 = _dot(du, wu[...], NT)
        dx, dgq = _rms_bwd(h1_ref[...], gpre_ref[...], dhn2)
        dh1_ref[...] = dh2_ + dx
        _acc(dgpre_ref, _colsum8(dgq))
        _acc(dgpost_ref, _colsum8(dgp))

    return pl.pallas_call(
        body, name="mlp_bwd", grid=(lp // MLPB,),
        in_specs=[_rowspec(MLPB, D_MODEL)] * 3 + [_rowspec(MLPB, D_FF), ANY, ANY] + [_fullspec((1, D_MODEL))] * 2,
        out_specs=[_rowspec(MLPB, D_MODEL), _rowspec(MLPB, D_FF), _rowspec(MLPB, D_MODEL),
                   _fullspec((8, D_MODEL)), _fullspec((8, D_MODEL))],
        out_shape=[_sds((lp, D_MODEL), F32), _sds((lp, D_FF), BF16), _sds((lp, D_MODEL), BF16),
                   _sds((8, D_MODEL), F32), _sds((8, D_MODEL), F32)],
        scratch_shapes=[pltpu.VMEM((D_MODEL, D_FF), BF16), pltpu.VMEM((D_FF, D_MODEL), BF16), pltpu.SemaphoreType.DMA((2,))],
        compiler_params=_cp(("arbitrary",)),
    )(dh2, f, h1, u, w_up, w_down, g_pre, g_post)


def _out_bwd(dh1, mix, att, w_out, g_post):
    lp = dh1.shape[0]

    def body(dh1_ref, mix_ref, att_ref, w_ref, g_ref, dmix_ref, datt_ref, dssm_ref, dg_ref, dl_ref):
        dmix, dg = _rms_bwd(mix_ref[...], g_ref[...], dh1_ref[...])
        dmb = dmix.astype(BF16)
        dmix_ref[...] = dmb
        datt = _dot(dmb, w_ref[pl.ds(0, 1024), :], NT).astype(BF16)
        datt_ref[...] = datt
        dssm_ref[...] = _dot(dmb, w_ref[pl.ds(1024, 1024), :], NT)
        _acc(dg_ref, _colsum8(dg))
        prod = datt.astype(F32) * att_ref[...].astype(F32)
        for hh in range(ATT_HEADS):
            d = jnp.sum(prod[:, hh * V_HEAD:(hh + 1) * V_HEAD], axis=1, keepdims=True)
            dl_ref[hh] = jnp.broadcast_to(d, (ROWB, LANE)).T[0:1, :]

    return pl.pallas_call(
        body, name="out_bwd", grid=(lp // ROWB,),
        in_specs=[_rowspec(ROWB, D_MODEL)] * 3 + [_fullspec((2048, D_MODEL)), _fullspec((1, D_MODEL))],
        out_specs=[_rowspec(ROWB, D_MODEL)] * 3 + [_fullspec((8, D_MODEL)),
                                                   pl.BlockSpec((ATT_HEADS, 1, ROWB), lambda i: (0, 0, i))],
        out_shape=[_sds((lp, D_MODEL), BF16), _sds((lp, 1024), BF16), _sds((lp, 1024), F32), _sds((8, D_MODEL), F32),
                   _sds((ATT_HEADS, 1, lp), F32)],
        compiler_params=_cp(("arbitrary",)),
    )(dh1, mix, att, w_out, g_post)


def _ssd_post_bwd(dssm, y, xact, z, dskip, gnorm):
    lp = y.shape[0]

    def body(do_ref, y_ref, x_ref, z_ref, d_ref, g_ref, dy_ref, dz_ref, dg_ref, dd_ref):
        z_, x_ = z_ref[...], x_ref[...]
        sg = _sigmoid(z_)
        sz = z_ * sg
        y2 = y_ref[...] + d_ref[...] * x_
        gt = y2 * sz
        r = lax.rsqrt(_group_mean(gt * gt) + EPS)
        gh = gt * r
        do = do_ref[...]
        dgh = do * g_ref[...]
        dgt = r * (dgh - gh * _group_mean(dgh * gh))
        dy2 = dgt * sz
        dy_ref[...] = dy2
        dz_ref[...] = (dgt * y2 * (sg * (1.0 + z_ * (1.0 - sg)))).astype(BF16)
        _acc(dg_ref, _colsum8(do * gh))
        _acc(dd_ref, _colsum8(dy2 * x_))

    return pl.pallas_call(
        body, name="ssd_post_bwd", grid=(lp // ROWB,),
        in_specs=[_rowspec(ROWB, SSM_WIDTH)] * 4 + [_fullspec((1, SSM_WIDTH))] * 2,
        out_specs=[_rowspec(ROWB, SSM_WIDTH)] * 2 + [_fullspec((8, SSM_WIDTH))] * 2,
        out_shape=[_sds((lp, SSM_WIDTH), F32), _sds((lp, SSM_WIDTH), BF16), _sds((8, SSM_WIDTH), F32), _sds((8, SSM_WIDTH), F32)],
        compiler_params=_cp(("arbitrary",)),
    )(dssm, y, xact, z, dskip, gnorm)


def _ssd_bwd(dy, xact, dtr, hprev, dt_bias, a_log, dskip, seq_rows, exchange=()):
    lp = xact.shape[0]
    nc = lp // CHUNK
    gw = SSM_WIDTH // SSM_GROUPS
    hpg = SSM_HEADS // SSM_GROUPS
    nb = SSM_WIDTH // (2 * SSM_STATE)
    na = len(exchange)
    cps = ROWB // CHUNK
    nsteps = nc // cps

    def body(dy_ref, x_ref, b_ref, c_ref, dtr_ref, hp_ref, bias_ref, alog_ref, dsk_ref, tri_ref, tri3_ref, trit3_ref,
             ex3_ref, ext3_ref, *rest):
        xin, (dact_ref, ddtr_ref, da_ref, dbias_ref), xout = rest[:na], rest[na:na + 4], rest[na + 4:2 * na + 4]
        dh_s, sems = rest[2 * na + 4], rest[2 * na + 5:]
        step = pl.program_id(0)

        @pl.when(step == 0)
        def _():
            dh_s[...] = jnp.zeros_like(dh_s)
            da_ref[...] = jnp.zeros_like(da_ref)
            dbias_ref[...] = jnp.zeros_like(dbias_ref)

        if na:
            x_start, x_finish = _exchange_ops(xin, xout, *sems)
            pl.when(step == 0)(x_start)

        for lc in reversed(range(cps)):
            rows = pl.ds(lc * CHUNK, CHUNK)
            chunk((nsteps - 1 - step) * cps + lc, dy_ref.at[rows], x_ref.at[rows], b_ref.at[rows], c_ref.at[rows],
                  dtr_ref.at[rows], hp_ref.at[pl.ds(lc, 1)], bias_ref, alog_ref, dsk_ref, tri_ref, tri3_ref, trit3_ref,
                  ex3_ref, ext3_ref, dact_ref.at[rows], ddtr_ref.at[rows], da_ref, dbias_ref, dh_s)

        if na:
            pl.when(step == nsteps - 1)(x_finish)

    def chunk(c, dy_ref, x_ref, b_ref, c_ref, dtr_ref, hp_ref, bias_ref, alog_ref, dsk_ref, tri_ref, tri3_ref, trit3_ref,
              ex3_ref, ext3_ref, dact_ref, ddtr_ref, da_ref, dbias_ref, dh_s):
        tri = tri_ref[...]
        ex3 = ex3_ref[...]
        dt, a, acol, valid, dtr_ = _ssd_prep(dtr_ref[...], bias_ref, alog_ref, tri3_ref[...], c, seq_rows)
        arow = acol.T
        dtrow = dt.T
        alast = acol[CHUNK - 1:CHUNK, :]
        e_all = _dot01_r(jnp.exp(acol), ex3)
        wgt0 = jnp.exp(alast - acol)
        wgt = wgt0 * dt
        wx_all = _dot01_r(wgt, ex3)
        elast = jnp.exp(alast)
        dec_all = _dot01_r(_row16(elast), ex3)[0:1, :]
        causal = tri > 0.5
        upper = tri.T > 0.5
        lane_id = lax.broadcasted_iota(jnp.int32, (1, LANE), 1)
        sub_id = lax.broadcasted_iota(jnp.int32, (CHUNK, 1), 0)
        dacol = jnp.zeros((CHUNK, LANE), F32)
        darowf = jnp.zeros((CHUNK, LANE), F32)
        ddtrowf = jnp.zeros((CHUNK, LANE), F32)
        dwgt = jnp.zeros((CHUNK, LANE), F32)
        delast = jnp.zeros((1, LANE), F32)
        for g in range(SSM_GROUPS):
            gs = slice(g * gw, (g + 1) * gw)
            ext3_g = ext3_ref[g]
            bg = b_ref[:, g * SSM_STATE:(g + 1) * SSM_STATE]
            cg = c_ref[:, g * SSM_STATE:(g + 1) * SSM_STATE]
            bgb, cgb = bg.astype(BF16), cg.astype(BF16)
            xg = x_ref[:, gs]
            dyg = dy_ref[:, gs]
            hg = hp_ref[0, :, gs]
            dhg = dh_s[:, gs]
            hgb, dhgb = hg.astype(BF16), dhg.astype(BF16)
            gm = _dot(cgb, bgb, NT)
            gmt = _dot(bgb, cgb, NT)
            y_off = _dot(cgb, hgb) * e_all[:, gs]
            dy0 = (dyg * e_all[:, gs]).astype(BF16)
            dcg = _dot(dy0, hgb, NT)
            dh_in = _dot(cg.T.astype(BF16), dy0) + dhg * dec_all[:, gs]
            dacol = dacol + _dot01_r(dyg * y_off, ext3_g)
            xw = xg * wx_all[:, gs]
            dxw = _dot(bgb, dhgb)
            dx_state = dxw * wx_all[:, gs]
            dwgt = dwgt + _dot01_r(dxw * xg, ext3_g)
            dbt = _dot(dhgb, xw.astype(BF16), NT)
            hh = _colsum8(dhg * hg)
            hh16 = jnp.concatenate([hh, jnp.zeros_like(hh)], axis=0)
            delast = delast + jnp.sum(_dot01_r(hh16, ext3_g), axis=0, keepdims=True)
            dgm = jnp.zeros((CHUNK, CHUNK), F32)
            for r in range(hpg):
                hd = g * hpg + r
                cs = slice(r * SSM_HEAD_DIM, (r + 1) * SSM_HEAD_DIM)
                acol_r, arow_r = acol[:, hd:hd + 1], arow[hd:hd + 1, :]
                dtrow_r, dtcol_r = dtrow[hd:hd + 1, :], dt[:, hd:hd + 1]
                lm = jnp.where(causal, jnp.exp(jnp.where(causal, acol_r - arow_r, 0.0)), 0.0)
                lmt = jnp.where(upper, jnp.exp(jnp.where(upper, arow_r - acol_r, 0.0)), 0.0)
                wt = gmt * lmt * dtcol_r
                dy_r = dyg[:, cs].astype(BF16)
                dx_r = _dot(wt.astype(BF16), dy_r)
                dw = _dot(dy_r, xg[:, cs].astype(BF16), NT)
                t1 = dw * lm
                dgm = dgm + t1 * dtrow_r
                q1 = t1 * gm
                m = q1 * dtrow_r
                dacol = dacol + jnp.sum(m, axis=1, keepdims=True) * (lane_id == hd).astype(F32)
                darowf = darowf - (sub_id == hd).astype(F32) * jnp.sum(m, axis=0, keepdims=True)
                ddtrowf = ddtrowf + (sub_id == hd).astype(F32) * jnp.sum(q1, axis=0, keepdims=True)
                dact_ref[:, pl.ds(hd * SSM_HEAD_DIM, SSM_HEAD_DIM)] = (
                    dx_r + dx_state[:, cs] + dyg[:, cs] * dsk_ref[:, pl.ds(hd * SSM_HEAD_DIM, SSM_HEAD_DIM)])
            dgmb = dgm.astype(BF16)
            dact_ref[:, pl.ds(SSM_WIDTH + g * SSM_STATE, SSM_STATE)] = dbt.T + _dot(dgm.T.astype(BF16), cgb)
            dact_ref[:, pl.ds(SSM_WIDTH + 2 * SSM_STATE + g * SSM_STATE, SSM_STATE)] = dcg + _dot(dgmb, bgb)
            dh_s[:, gs] = dh_in
        t = dwgt * wgt
        dalast = jnp.sum(t, axis=0, keepdims=True) + delast * elast
        dacol_tot = dacol - t + darowf.T + (sub_id == CHUNK - 1).astype(F32) * dalast
        dda = _dot01_l(trit3_ref[...], dacol_tot)
        ddt = dwgt * wgt0 + ddtrowf.T + dda * a
        ddtr = jnp.where(valid, ddt * _sigmoid(dtr_), 0.0)
        ddtr_ref[...] = ddtr
        da_ref[...] += _colsum8(dda * dt) * a
        dbias_ref[...] += _colsum8(ddtr)

    rev = lambda c: nsteps - 1 - c
    rb = cps * CHUNK
    xs_spec = pl.BlockSpec((rb, SSM_WIDTH), lambda c: (rev(c), 0))
    dact, ddtr, da8, dbias8, *received = pl.pallas_call(
        body, name="ssd_bwd", grid=(nsteps,),
        in_specs=[xs_spec, xs_spec,
                  pl.BlockSpec((rb, 2 * SSM_STATE), lambda c: (rev(c), nb)),
                  pl.BlockSpec((rb, 2 * SSM_STATE), lambda c: (rev(c), nb + 1)),
                  pl.BlockSpec((rb, LANE), lambda c: (rev(c), 0)),
                  pl.BlockSpec((cps, SSM_STATE, SSM_WIDTH), lambda c: (rev(c), 0, 0)),
                  _fullspec((1, LANE)), _fullspec((1, LANE)), _fullspec((1, SSM_WIDTH)),
                  _fullspec((CHUNK, CHUNK)), _fullspec((CHUNK, 3 * CHUNK)), _fullspec((CHUNK, 3 * CHUNK)),
                  _fullspec((3 * LANE, SSM_WIDTH)), _fullspec((SSM_GROUPS, 3 * gw, LANE))] + [ANY] * na,
        out_specs=[pl.BlockSpec((rb, CONV_DIM), lambda c: (rev(c), 0)), pl.BlockSpec((rb, LANE), lambda c: (rev(c), 0)),
                   _fullspec((8, LANE)), _fullspec((8, LANE))] + [ANY] * na,
        out_shape=[_sds((lp, CONV_DIM), F32), _sds((lp, LANE), F32), _sds((8, LANE), F32), _sds((8, LANE), F32)]
        + [_sds(e.shape, e.dtype) for e in exchange],
        scratch_shapes=[pltpu.VMEM((SSM_STATE, SSM_WIDTH), F32)] + (_gather_scratch(na) if na else []),
        compiler_params=_cp(("arbitrary",)),
    )(dy, xact, xact, xact, dtr, hprev, dt_bias, a_log, dskip, _tri_mat(), _x3(_tri_mat(), 1), _x3(_tri_mat().T, 1),
      _x3(_expand_mat(), 0), jnp.stack([_x3(_expand_mat().T[g * gw:(g + 1) * gw], 0) for g in range(SSM_GROUPS)]),
      *exchange)
    return dact, ddtr, da8, dbias8, received


def _conv_bwd(dact, xbc, cw, cb):
    lp, c = xbc.shape
    t8 = ROWB // 8
    nb = lp // ROWB

    def body(d_ref, dnext_ref, x_ref, prev_ref, next_ref, w_ref, b_ref, dx_ref, dw_ref, db_ref, xb, dp):
        i = pl.program_id(0)
        last = i == nb - 1
        xb[pl.ds(0, 8), :] = jnp.where(i > 0, prev_ref[...], 0.0)
        xb[pl.ds(8, ROWB), :] = x_ref[...]
        xb[pl.ds(8 + ROWB, 8), :] = jnp.where(last, 0.0, next_ref[...])

        @pl.when(i == 0)
        def _():
            dw_ref[...] = jnp.zeros_like(dw_ref)
            db_ref[...] = jnp.zeros_like(db_ref)

        sub = lax.broadcasted_iota(jnp.int32, (8, 1), 0)
        x0 = 8 - (CONV_K - 1)

        def strip(s, carry):
            cs = pl.ds(pl.multiple_of(s * LANE, LANE), LANE)
            w, b = w_ref[:, cs], b_ref[:, cs]

            def dpre_rows(r0, n, d):
                xs = _shifted_rows(xb, r0, n, cs, [x0 + kk for kk in range(CONV_K)])
                pre = b + sum(w[kk:kk + 1, :] * xs[kk] for kk in range(CONV_K))
                sg = _sigmoid(pre)
                return d * (sg * (1.0 + pre * (1.0 - sg))), xs

            dws = [jnp.zeros((8, LANE), F32) for _ in range(CONV_K)]
            dbs = jnp.zeros((8, LANE), F32)
            for r0 in range(0, ROWB, CONV_ROWS):
                dpre, xs = dpre_rows(r0, CONV_ROWS, d_ref[pl.ds(r0, CONV_ROWS), cs])
                dp[pl.ds(r0, CONV_ROWS), cs] = dpre
                dbs = dbs + _colsum8(dpre)
                for kk in range(CONV_K):
                    dws[kk] = dws[kk] + _colsum8(dpre * xs[kk])
            dp[pl.ds(ROWB, 8), cs] = dpre_rows(ROWB, 8, jnp.where(last, 0.0, dnext_ref[:, cs]))[0]
            dwv = sum(jnp.where(sub == kk, jnp.sum(dws[kk], axis=0, keepdims=True), 0.0) for kk in range(CONV_K))
            dw_ref[:, cs] += dwv
            db_ref[:, cs] += dbs
            for r0 in range(0, ROWB, CONV_ROWS):
                ahead = _shifted_rows(dp, r0, CONV_ROWS, cs, [CONV_K - 1 - kk for kk in range(CONV_K)])
                dx = sum(w[kk:kk + 1, :] * ahead[kk] for kk in range(CONV_K))
                dx_ref[pl.ds(r0, CONV_ROWS), cs] = dx.astype(BF16)
            return carry

        lax.fori_loop(0, c // LANE, strip, 0)

    nxt = lambda i: (jnp.minimum((i + 1) * t8, lp // 8 - 1), 0)
    prv = lambda i: (jnp.maximum(i * t8 - 1, 0), 0)
    return pl.pallas_call(
        body, name="conv_bwd", grid=(nb,),
        in_specs=[_rowspec(ROWB, c), pl.BlockSpec((8, c), nxt), _rowspec(ROWB, c), pl.BlockSpec((8, c), prv),
                  pl.BlockSpec((8, c), nxt), _fullspec((8, c)), _fullspec((1, c))],
        out_specs=[_rowspec(ROWB, c), _fullspec((8, c)), _fullspec((8, c))],
        out_shape=[_sds((lp, c), BF16), _sds((8, c), F32), _sds((8, c), F32)],
        scratch_shapes=[pltpu.VMEM((ROWB + 16, c), F32), pltpu.VMEM((ROWB + 8, c), F32)],
        compiler_params=_cp(("arbitrary",)),
    )(dact, dact, xbc, xbc, xbc, cw, cb)


def _flash_bwd(q, k, v, datt, lse_row, delta_row, cos, sa, sb):
    lp = q.shape[0]
    nk = lp // ROWB

    def body(k_ref, v_ref, q_ref, do_ref, lse_ref, dl_ref, cos_ref, sa_ref, sb_ref, dq_ref, dk_ref, dv_ref,
             dq_acc, dk_acc, dv_acc):
        j = pl.program_id(1)

        @pl.when(j == 0)
        def _():
            dq_acc[...] = jnp.zeros_like(dq_acc)

        dk_acc[...] = jnp.zeros_like(dk_acc)
        dv_acc[...] = jnp.zeros_like(dv_acc)

        def tile(i, masked, key0=0, nkeys=ROWB):
            keys = pl.ds(key0, nkeys)
            kb, vb = k_ref[keys, :], v_ref[keys, :]
            off = pl.multiple_of(i * ROWB, ROWB)
            qb = q_ref[pl.ds(off, ROWB), :]
            dob = do_ref[pl.ds(off, ROWB), :]
            lse_r = lse_ref[0, :, pl.ds(off, ROWB)]
            dl_r = dl_ref[0, :, pl.ds(off, ROWB)]
            st = _dot(kb, qb, NT)
            if masked:
                krow = j * ROWB + key0 + lax.broadcasted_iota(jnp.int32, st.shape, 0)
                qrow = i * ROWB + lax.broadcasted_iota(jnp.int32, st.shape, 1)
                st = jnp.where(_att_ok(qrow, krow), st, NEG)
            pt = jnp.exp2(st - lse_r)
            dv_acc[keys, :] += _dot(pt.astype(BF16), dob)
            dpt = _dot(vb, dob, NT)
            dst = (pt * (dpt - dl_r)).astype(BF16)
            dk_acc[keys, :] += _dot(dst, qb)
            dq_acc[pl.ds(off, ROWB), :] += _dot(dst, kb, ((0,), (0,)))

        @pl.when(j == 0)
        def _():
            _pair_loop(0, nk, lambda i: tile(i, True, ROWB - META_KEYS, META_KEYS), (4, 2))

        @pl.when((j > 0) & (j < nk - 1))
        def _():
            tile(j, True)
            tile(j + 1, False)
            _pair_loop(j + 2, nk, lambda i: tile(i, False), (4, 2))

        @pl.when(j == nk - 1)
        def _():
            tile(j, True)

        dk_ref[...] = (dk_acc[...] * LN2).astype(BF16)
        dv_ref[...] = dv_acc[...].astype(BF16)
        dq = dq_acc[pl.ds(pl.multiple_of(j * ROWB, ROWB), ROWB), :] * ATT_SCALE
        half = HEADW // 2
        dq_ref[:, pl.ds(0, half)] = dq[:, :half].astype(BF16)
        dq_ref[:, pl.ds(half, half)] = _rope_t(dq[:, half:], cos_ref[...], sa_ref[...], sb_ref[...]).astype(BF16)

    stat = pl.BlockSpec((1, 1, lp), lambda h, j: (h, 0, 0))
    blk = pl.BlockSpec((ROWB, HEADW), lambda h, j: (j, h))
    tab = pl.BlockSpec((ROWB, HEADW // 2), lambda h, j: (j, 0))
    return pl.pallas_call(
        body, name="flash_bwd", grid=(ATT_HEADS, nk),
        in_specs=[blk, pl.BlockSpec((ROWB, V_HEAD), lambda h, j: (j, 2 * h)),
                  pl.BlockSpec((lp, HEADW), lambda h, j: (0, h)), pl.BlockSpec((lp, V_HEAD), lambda h, j: (0, h)),
                  stat, stat, tab, tab, tab],
        out_specs=[blk, blk, pl.BlockSpec((ROWB, V_HEAD), lambda h, j: (j, h))],
        out_shape=[_sds((lp, ATT_HEADS * HEADW), BF16), _sds((lp, ATT_HEADS * HEADW), BF16),
                   _sds((lp, ATT_HEADS * V_HEAD), BF16)],
        scratch_shapes=[pltpu.VMEM((lp, HEADW), F32), pltpu.VMEM((ROWB, HEADW), F32), pltpu.VMEM((ROWB, V_HEAD), F32)],
        compiler_params=_cp(("arbitrary", "arbitrary")),
    )(k, v, q, datt, lse_row, delta_row, cos, sa, sb)


def _qkv_bwd(dqp, dk, dv, cq, ckv, cos, sa, sb, gq, gkv, wq, wk, wv):
    lp = cq.shape[0]
    qw = ATT_HEADS * HEADW

    def body(dqp_ref, dk_ref, dv_ref, cq_ref, ckv_ref, cos_ref, sa_ref, sb_ref, gq_ref, gkv_ref, wq_ref, wk_ref, wv_ref,
             dcq_ref, dckv_ref, dkr_ref, dgq_ref, dgkv_ref):
        dcq, dgq = _rms_bwd(cq_ref[...], gq_ref[...], _dot(dqp_ref[...], wq_ref[...], NT))
        dcq_ref[...] = dcq.astype(BF16)
        dkb = dk_ref[...]
        half = HEADW // 2
        dksum = sum(dkb[:, hh * HEADW + half:(hh + 1) * HEADW].astype(F32) for hh in range(ATT_HEADS))
        dkr_ref[:, pl.ds(0, half)] = jnp.zeros((ROWB, half), BF16)
        dkr_ref[:, pl.ds(half, half)] = _rope_t(dksum, cos_ref[...], sa_ref[...], sb_ref[...]).astype(BF16)
        dckvn = _dot(dkb, wk_ref[...], NT) + _dot(dv_ref[...], wv_ref[...], NT)
        dckv, dgkv = _rms_bwd(ckv_ref[...], gkv_ref[...], dckvn)
        dckv_ref[...] = dckv.astype(BF16)
        _acc(dgq_ref, _colsum8(dgq))
        _acc(dgkv_ref, _colsum8(dgkv))

    return pl.pallas_call(
        body, name="qkv_bwd", grid=(lp // ROWB,),
        in_specs=[_rowspec(ROWB, qw), _rowspec(ROWB, qw), _rowspec(ROWB, ATT_HEADS * V_HEAD),
                  _rowspec(ROWB, Q_LORA), _rowspec(ROWB, KV_LORA)] + [_rowspec(ROWB, HEADW // 2)] * 3
        + [_fullspec((1, Q_LORA)), _fullspec((1, KV_LORA)), _fullspec((Q_LORA, qw)), _fullspec((KV_LORA, qw)),
           _fullspec((KV_LORA, ATT_HEADS * V_HEAD))],
        out_specs=[_rowspec(ROWB, Q_LORA), _rowspec(ROWB, KV_LORA), _rowspec(ROWB, HEADW),
                   _fullspec((8, Q_LORA)), _fullspec((8, KV_LORA))],
        out_shape=[_sds((lp, Q_LORA), BF16), _sds((lp, KV_LORA), BF16), _sds((lp, HEADW), BF16),
                   _sds((8, Q_LORA), F32), _sds((8, KV_LORA), F32)],
        compiler_params=_cp(("arbitrary",)),
    )(dqp, dk, dv, cq, ckv, cos, sa, sb, gq, gkv, wq, wk, wv)


def _in_bwd(pieces, head, x, dh1, g, w_all):
    lp = dh1.shape[0]
    npc = len(pieces)
    assert sum(pc.shape[1] for pc in pieces) == PROJ_W

    def body(*refs):
        head_ref, x_ref, dh1_ref, g_ref, w_ref, dx_ref, dhead_ref, dg_ref, dp_ref = refs[npc:]
        dp = jnp.concatenate([r[...].astype(BF16) for r in refs[:npc]], axis=1)
        dp_ref[...] = dp
        dx, dg = _rms_bwd(_h_block(head_ref, x_ref), g_ref[...], _dot(dp, w_ref[...], NT))
        dh = dh1_ref[...] + dx

        @pl.when(pl.program_id(0) == 0)
        def _():
            dhead_ref[...] = dh

        @pl.when(pl.program_id(0) > 0)
        def _():
            dx_ref[...] = dh

        _acc(dg_ref, _colsum8(dg))

    return pl.pallas_call(
        body, name="in_bwd", grid=(lp // ROWB,),
        in_specs=[_rowspec(ROWB, pc.shape[1]) for pc in pieces]
        + [_fullspec((ROWB, D_MODEL)), _xspec(), _rowspec(ROWB, D_MODEL), _fullspec((1, D_MODEL)),
           _fullspec((D_MODEL, PROJ_W))],
        out_specs=[_xspec(), _fullspec((ROWB, D_MODEL)), _fullspec((8, D_MODEL)), _rowspec(ROWB, PROJ_W)],
        out_shape=[_sds(x.shape, F32), _sds((ROWB, D_MODEL), F32), _sds((8, D_MODEL), F32), _sds((lp, PROJ_W), BF16)],
        compiler_params=_cp(("arbitrary",)),
    )(*pieces, head, x, dh1, g, w_all)


def _tile_of(n, cap=1024):
    return max(t for t in range(LANE, min(n, cap) + 1, LANE) if n % t == 0)


def _matmul_tn(name, a, b):
    rows, kd = a.shape
    nd = b.shape[1]
    tk, tn = _tile_of(kd), _tile_of(nd)
    rb = 3 * ROWB if rows % (3 * ROWB) == 0 else ROWB

    def body(a_ref, b_ref, o_ref):
        @pl.when(pl.program_id(2) == 0)
        def _():
            o_ref[...] = jnp.zeros_like(o_ref)

        o_ref[...] += _dot(a_ref[...], b_ref[...], ((0,), (0,)))

    return pl.pallas_call(
        body, name=name, grid=(kd // tk, nd // tn, rows // rb),
        in_specs=[pl.BlockSpec((rb, tk), lambda i, j, r: (r, i)), pl.BlockSpec((rb, tn), lambda i, j, r: (r, j))],
        out_specs=pl.BlockSpec((tk, tn), lambda i, j, r: (i, j)), out_shape=_sds((kd, nd), F32),
        compiler_params=_cp(("arbitrary", "arbitrary", "arbitrary")),
    )(a, b)


def _local_backward(head, x, f, exchange_late=False):
    p = f["p"]
    g = {}
    row = lambda v: _row1(v)
    s8 = lambda v: jnp.sum(v, axis=0)
    dh1, du, df, dgpre, dgpost = _mlp_bwd(f["dh2"], f["f"], f["h1"], f["u"], p["w_mlp_up"], p["w_mlp_down"],
                                          row(p["norm_mlp_pre"]), row(p["norm_mlp_post"]))
    g["norm_mlp_pre"], g["norm_mlp_post"] = s8(dgpre), s8(dgpost)
    g["w_mlp_up"] = _matmul_tn("dw_mlp_up", f["hn2"], du)
    g["w_mlp_down"] = _matmul_tn("dw_mlp_down", f["a"], df)
    dmix, datt, dssm, dgmp, delta = _out_bwd(dh1, f["mix"], f["att"], p["w_out"], row(p["norm_mix_post"]))
    g["norm_mix_post"] = s8(dgmp)
    g["w_out"] = jnp.concatenate([_matmul_tn("dw_out_att", f["att"], dmix), _matmul_tn("dw_out_ssm", f["ssm"], dmix)], axis=0)
    dy, dz, dgn, dd = _ssd_post_bwd(dssm, f["y"], f["xact"], f["z"], f["dskip"], row(p["ssm_norm"]))
    g["ssm_norm"] = s8(dgn)
    g["d_skip"] = s8(dd).reshape(SSM_HEADS, SSM_HEAD_DIM).sum(axis=1)
    dact, ddtr, da8, dbias8, received = _ssd_bwd(
        dy, f["xact"], f["dtr"], f["hprev"], f["dt_bias"], f["a_log"], f["dskip"], f["seq_rows"],
        exchange=[_to_chunks(n, g[n]).astype(BF16) for n in LATE] if exchange_late else ())
    g["a_log"], g["dt_bias"] = s8(da8)[:SSM_HEADS], s8(dbias8)[:SSM_HEADS]
    dxbc, dcw8, dcb8 = _conv_bwd(dact, f["xbc"], f["cw"], row(p["conv_b"]))
    g["conv_w"], g["conv_b"] = dcw8[:CONV_K], s8(dcb8)
    dqp, dkb, dv = _flash_bwd(f["q"], f["k"], f["v"], datt, f["lse"], delta, *f["rope"])
    dcq, dckv, dkr, dgq, dgkv = _qkv_bwd(dqp, dkb, dv, f["cq"], f["ckv"], *f["rope"], row(p["q_a_norm"]),
                                         row(p["kv_a_norm"]), f["wq"], f["wk"], f["wv"])
    g["q_a_norm"], g["kv_a_norm"] = s8(dgq), s8(dgkv)
    dwq = _matmul_tn("dw_q_up", f["cqn"], dqp).reshape(Q_LORA, ATT_HEADS, HEADW)
    g["w_q_up"] = dwq[:, :, :QK_NOPE + QK_ROPE].reshape(Q_LORA, -1)
    dwk = _matmul_tn("dw_k_up", f["ckvn"], dkb).reshape(KV_LORA, ATT_HEADS, HEADW)[:, :, :QK_NOPE]
    dwv = _matmul_tn("dw_v_up", f["ckvn"], dv).reshape(KV_LORA, ATT_HEADS, V_HEAD)
    g["w_kv_up"] = jnp.concatenate([dwk, dwv], axis=2).reshape(KV_LORA, -1)
    dx, dhead, dgin, dproj = _in_bwd([dcq, dckv, dkr, dz, dxbc, ddtr], head, x, dh1, row(p["norm_mix_pre"]), f["w_all"])
    g["norm_mix_pre"] = s8(dgin)
    g["meta_tokens"] = dhead[PADF:]
    dwa = _matmul_tn("dw_in", f["hn"], dproj)
    g["w_in"] = jnp.concatenate([dwa[:, PC_Q:PC_KR], dwa[:, PC_KR + QK_NOPE:PC_KR + QK_NOPE + QK_ROPE],
                                 dwa[:, PC_Z:PC_DT + SSM_HEADS]], axis=1)
    return dx, g, received


BIG = {"w_in": ((D_MODEL, IN_WIDTH), 1), "w_q_up": ((Q_LORA, ATT_HEADS * (QK_NOPE + QK_ROPE)), 1),
       "w_kv_up": ((KV_LORA, ATT_HEADS * (QK_NOPE + V_HEAD)), 1), "w_out": ((2 * D_MODEL, D_MODEL), 0),
       "w_mlp_up": ((D_MODEL, D_FF), 1), "w_mlp_down": ((D_FF, D_MODEL), 0), "conv_w": ((CONV_K, CONV_DIM), 1),
       "meta_tokens": ((N_META, D_MODEL), 1)}
SMALL = {"norm_mix_pre": D_MODEL, "q_a_norm": Q_LORA, "kv_a_norm": KV_LORA, "conv_b": CONV_DIM, "dt_bias": SSM_HEADS,
         "a_log": SSM_HEADS, "d_skip": SSM_HEADS, "ssm_norm": SSM_WIDTH, "norm_mix_post": D_MODEL,
         "norm_mlp_pre": D_MODEL, "norm_mlp_post": D_MODEL}
WEIGHT_ORDER = ("meta_tokens", "norm_mix_pre", "w_in", "q_a_norm", "w_q_up", "kv_a_norm", "w_kv_up", "conv_w", "conv_b",
                "dt_bias", "a_log", "d_skip", "ssm_norm", "w_out", "norm_mix_post", "norm_mlp_pre", "w_mlp_up",
                "w_mlp_down", "norm_mlp_post")
ADAM_ROWS = 256


def _shard_shape(name):
    shape, ax = BIG[name]
    return tuple(d // N_DEV if a == ax else d for a, d in enumerate(shape))


SMALL_ROWS = -(-sum(SMALL.values()) // (LANE * 8)) * 8


def _pack(flats, rows):
    v = jnp.concatenate([f.reshape(-1) for f in flats])
    return jnp.pad(v, (0, rows * LANE - v.shape[0])).reshape(rows, LANE)


def _unpack(packed, shapes):
    v = packed.reshape(-1)
    out, o = [], 0
    for s in shapes:
        n = math.prod(s)
        out.append(v[o:o + n].reshape(s))
        o += n
    return out


def _to_chunks(name, full):
    shape, ax = BIG[name]
    if ax == 0:
        return full.reshape((N_DEV,) + _shard_shape(name))
    k, n = shape
    return full.reshape(k, N_DEV, n // N_DEV).transpose(1, 0, 2)


def _from_shards(name, shards):
    shape, ax = BIG[name]
    if ax == 0:
        return shards.reshape(shape)
    return shards.transpose(1, 0, 2).reshape(shape)


def _peer(k):
    x, y, c = lax.axis_index("x"), lax.axis_index("y"), lax.axis_index("c")
    px = 1 - x if k & 4 else x
    py = 1 - y if k & 2 else y
    pc = 1 - c if k & 1 else c
    return (px, py, pc), 4 * px + 2 * py + pc


def _gather_ops(x_refs, out_refs, send_sems, recv_sems, local_sems):
    na = len(x_refs)
    chips = (4, 2, 6)

    def copy(a, n, block, to, src=None):
        return pltpu.make_async_remote_copy(
            src_ref=out_refs[a].at[block] if src is None else src, dst_ref=out_refs[a].at[block],
            send_sem=send_sems.at[7 * a + n], recv_sem=recv_sems.at[7 * a + n], device_id=to, device_id_type=MESH)

    def mine():
        me = _peer(0)[1]
        return [pltpu.make_async_copy(x_refs[a], out_refs[a].at[me], local_sems.at[a]) for a in range(na)]

    def first():
        me, sibling = _peer(0)[1], _peer(1)[0]
        out = [copy(a, 0, me, sibling, src=x_refs[a]) for a in range(na)]
        return out + [copy(a, 1 + n, me, _peer(k)[0], src=x_refs[a]) for n, k in enumerate(chips) for a in range(na)]

    def passed():
        sibling = _peer(1)[0]
        return [copy(a, 4 + n, _peer(k)[1], sibling) for n, k in enumerate(chips) for a in range(na)]

    def start():
        for cp in mine() + first():
            cp.start()

    def forward():
        sibling = _peer(1)[0]
        fwd = passed()
        for n, k in enumerate(chips):
            for a in range(na):
                copy(a, 1 + n, _peer(k)[1], sibling).wait_recv()
                fwd[n * na + a].start()

    def finish():
        sibling = _peer(1)[0]
        for a in range(na):
            copy(a, 0, _peer(1)[1], sibling).wait_recv()
        for n, k in enumerate(chips):
            for a in range(na):
                copy(a, 4 + n, _peer(k | 1)[1], sibling).wait_recv()
        for cp in first() + passed():
            cp.wait_send()
        for cp in mine():
            cp.wait()

    return start, forward, finish


def _gather_scratch(na):
    return [pltpu.SemaphoreType.DMA((7 * na,)), pltpu.SemaphoreType.DMA((7 * na,)), pltpu.SemaphoreType.DMA((na,))]


def _all_gather(shards):
    na = len(shards)

    def body(*refs):
        for step in _gather_ops(refs[:na], refs[na:2 * na], *refs[2 * na:]):
            step()

    return pl.pallas_call(
        body, name="all_gather_weights", out_shape=[_sds((N_DEV,) + s.shape, s.dtype) for s in shards],
        in_specs=[ANY] * na, out_specs=[ANY] * na, scratch_shapes=_gather_scratch(na),
    )(*shards)


def _exchange(chunks, small):
    na = len(chunks) + 1

    def body(*refs):
        for step in _exchange_ops(refs[:na], refs[na:2 * na], *refs[2 * na:], whole=(na - 1,)):
            step()

    arrays = list(chunks) + [small]
    return pl.pallas_call(
        body, name="exchange_grads",
        out_shape=[_sds(c.shape, c.dtype) for c in chunks] + [_sds((N_DEV,) + small.shape, small.dtype)],
        in_specs=[ANY] * na, out_specs=[ANY] * na, scratch_shapes=_gather_scratch(na),
    )(*arrays)


def _exchange_ops(in_refs, out_refs, send_sems, recv_sems, local_sems, whole=()):
    na = len(in_refs)

    def src(a, idx):
        return in_refs[a] if a in whole else in_refs[a].at[idx]

    def own():
        me = _peer(0)[1]
        return [pltpu.make_async_copy(src(a, me), out_refs[a].at[me], local_sems.at[a]) for a in range(na)]

    def copy(a, k, sending):
        me = _peer(0)[1]
        to, idx = _peer(k)
        return pltpu.make_async_remote_copy(
            src_ref=src(a, idx if sending else me), dst_ref=out_refs[a].at[me if sending else idx],
            send_sem=send_sems.at[7 * a + k - 1], recv_sem=recv_sems.at[7 * a + k - 1],
            device_id=to, device_id_type=MESH)

    def sent():
        return [copy(a, k, True) for k in range(1, N_DEV) for a in range(na)]

    def start():
        for cp in own() + sent():
            cp.start()

    def finish():
        for k in range(1, N_DEV):
            for a in range(na):
                copy(a, k, False).wait_recv()
        for cp in sent():
            cp.wait_send()
        for cp in own():
            cp.wait()

    return start, finish


def _reduce_adamw(name, recv, w, m, v):
    rows, cols = w.shape
    blk = ADAM_ROWS if rows % ADAM_ROWS == 0 else rows
    c1 = 1.0 - ADAM_B1 ** ADAM_STEP
    c2 = 1.0 - ADAM_B2 ** ADAM_STEP

    def body(r_ref, w_ref, m_ref, v_ref, g_ref, d_ref, nm_ref, nv_ref):
        g = r_ref[0].astype(F32)
        for s in range(1, N_DEV):
            g = g + r_ref[s].astype(F32)
        g_ref[...] = g
        m_ = ADAM_B1 * m_ref[...] + (1.0 - ADAM_B1) * g
        v_ = ADAM_B2 * v_ref[...] + (1.0 - ADAM_B2) * (g * g)
        nm_ref[...] = m_
        nv_ref[...] = v_
        d_ref[...] = -ADAM_LR * ((m_ / c1) / (jnp.sqrt(v_ / c2) + ADAM_EPS) + ADAM_WD * w_ref[...])

    spec = _rowspec(blk, cols)
    return pl.pallas_call(
        body, name="reduce_adamw_" + name, grid=(rows // blk,),
        in_specs=[pl.BlockSpec((N_DEV, blk, cols), lambda i: (0, i, 0)), spec, spec, spec],
        out_specs=[spec] * 4, out_shape=[_sds((rows, cols), F32)] * 4,
        compiler_params=_cp(("arbitrary",)),
    )(recv, w, m, v)


def kernel(x, meta_tokens, norm_mix_pre, w_in, q_a_norm, w_q_up, kv_a_norm, w_kv_up, conv_w, conv_b, dt_bias, a_log, d_skip, ssm_norm, w_out, norm_mix_post, norm_mlp_pre, w_mlp_up, w_mlp_down, norm_mlp_post, loss_target, m_meta_tokens, m_norm_mix_pre, m_w_in, m_q_a_norm, m_w_q_up, m_kv_a_norm, m_w_kv_up, m_conv_w, m_conv_b, m_dt_bias, m_a_log, m_d_skip, m_ssm_norm, m_w_out, m_norm_mix_post, m_norm_mlp_pre, m_w_mlp_up, m_w_mlp_down, m_norm_mlp_post, v_meta_tokens, v_norm_mix_pre, v_w_in, v_q_a_norm, v_w_q_up, v_kv_a_norm, v_w_kv_up, v_conv_w, v_conv_b, v_dt_bias, v_a_log, v_d_skip, v_ssm_norm, v_w_out, v_norm_mix_post, v_norm_mlp_pre, v_w_mlp_up, v_w_mlp_down, v_norm_mlp_post):
    w = dict(meta_tokens=meta_tokens, norm_mix_pre=norm_mix_pre, w_in=w_in, q_a_norm=q_a_norm, w_q_up=w_q_up,
             kv_a_norm=kv_a_norm, w_kv_up=w_kv_up, conv_w=conv_w, conv_b=conv_b, dt_bias=dt_bias, a_log=a_log,
             d_skip=d_skip, ssm_norm=ssm_norm, w_out=w_out, norm_mix_post=norm_mix_post, norm_mlp_pre=norm_mlp_pre,
             w_mlp_up=w_mlp_up, w_mlp_down=w_mlp_down, norm_mlp_post=norm_mlp_post)
    m = dict(meta_tokens=m_meta_tokens, norm_mix_pre=m_norm_mix_pre, w_in=m_w_in, q_a_norm=m_q_a_norm, w_q_up=m_w_q_up,
             kv_a_norm=m_kv_a_norm, w_kv_up=m_w_kv_up, conv_w=m_conv_w, conv_b=m_conv_b, dt_bias=m_dt_bias,
             a_log=m_a_log, d_skip=m_d_skip, ssm_norm=m_ssm_norm, w_out=m_w_out, norm_mix_post=m_norm_mix_post,
             norm_mlp_pre=m_norm_mlp_pre, w_mlp_up=m_w_mlp_up, w_mlp_down=m_w_mlp_down, norm_mlp_post=m_norm_mlp_post)
    v = dict(meta_tokens=v_meta_tokens, norm_mix_pre=v_norm_mix_pre, w_in=v_w_in, q_a_norm=v_q_a_norm, w_q_up=v_w_q_up,
             kv_a_norm=v_kv_a_norm, w_kv_up=v_w_kv_up, conv_w=v_conv_w, conv_b=v_conv_b, dt_bias=v_dt_bias,
             a_log=v_a_log, d_skip=v_d_skip, ssm_norm=v_ssm_norm, w_out=v_w_out, norm_mix_post=v_norm_mix_post,
             norm_mlp_pre=v_norm_mlp_pre, w_mlp_up=v_w_mlp_up, w_mlp_down=v_w_mlp_down, norm_mlp_post=v_norm_mlp_post)
    big_names = [n for n in WEIGHT_ORDER if n in BIG]
    small_names = [n for n in WEIGHT_ORDER if n in SMALL]
    shard = lambda d, n: d[n].reshape(_shard_shape(n))

    f32_names = ("conv_w", "meta_tokens")
    early = [n for n in big_names if n not in LATE]
    gathered = _all_gather([shard(w, n).astype(F32 if n in f32_names else BF16) for n in early])
    p = {n: w[n].reshape(-1) for n in small_names}
    p.update({n: _from_shards(n, s) for n, s in zip(early, gathered)})
    head = jnp.concatenate([jnp.zeros((PADF, D_MODEL), F32), p["meta_tokens"]], axis=0)
    f = _local_forward(head, x[0], loss_target[0], p, late={n: shard(w, n).astype(BF16) for n in LATE})
    dx, g, recv_late = _local_backward(head, x[0], f, exchange_late=True)
    grad_x = dx[None]
    loss = lax.psum(f["loss"], ("x", "y", "c"))

    small = _pack([g[n] for n in small_names], SMALL_ROWS)
    *recv_early, recv_small = _exchange([_to_chunks(n, g[n]).astype(BF16) for n in early], small)
    recv_of = {**dict(zip(early, recv_early)), **dict(zip(LATE, recv_late))}

    outs = {}
    kinds = ("grad", "delta", "new_m", "new_v")
    for n, recv in ((n, recv_of[n]) for n in big_names):
        for kind, arr in zip(kinds, _reduce_adamw(n, recv, shard(w, n), shard(m, n), shard(v, n))):
            outs[kind, n] = arr.reshape(w[n].shape)
    packed = [_pack([d[n] for n in small_names], SMALL_ROWS) for d in (w, m, v)]
    for kind, arr in zip(kinds, _reduce_adamw("small", recv_small, *packed)):
        for n, val in zip(small_names, _unpack(arr, [(SMALL[n],) for n in small_names])):
            outs[kind, n] = val.reshape(w[n].shape)
    return (loss, grad_x) + tuple(outs[kind, n] for kind in ("grad", "delta", "new_m", "new_v") for n in WEIGHT_ORDER)
```

```python
import math

import jax
import jax.numpy as jnp
import numpy as np
from jax import lax
from jax.experimental import pallas as pl
from jax.experimental.pallas import tpu as pltpu

F32 = jnp.float32
BF16 = jnp.bfloat16

D_MODEL = 1024
N_META = 16
EPS = 1e-6
ATT_HEADS = 8
Q_LORA = 384
KV_LORA = 256
QK_NOPE = 128
QK_ROPE = 64
V_HEAD = 128
ROPE_THETA = 10000.0
SSM_HEADS = 16
SSM_HEAD_DIM = 64
SSM_WIDTH = 1024
SSM_GROUPS = 2
SSM_STATE = 128
CONV_K = 4
CHUNK = 128
CONV_DIM = 1536
D_FF = 4096
IN_SPLITS = (Q_LORA, KV_LORA, QK_ROPE, SSM_WIDTH, CONV_DIM, SSM_HEADS)
IN_WIDTH = sum(IN_SPLITS)
ADAM_LR, ADAM_B1, ADAM_B2, ADAM_EPS, ADAM_WD, ADAM_STEP = 0.001, 0.9, 0.999, 1e-08, 0.01, 10

LANE = 128
ROWB = 512
PADF = ROWB - N_META
HEADW = 256
PC_Q, PC_KV, PC_KR, PC_Z, PC_XBC, PC_DT, PROJ_W = 0, 384, 640, 896, 1920, 3456, 3584
NEG = -1e30
N_DEV = 8
VMEM_LIMIT = 56 * 1024 * 1024
MESH = pl.DeviceIdType.MESH


def _cp(sem, vmem=VMEM_LIMIT, **kw):
    return pltpu.CompilerParams(dimension_semantics=sem, vmem_limit_bytes=vmem, **kw)


def _dot(a, b, dims=((1,), (0,))):
    return lax.dot_general(a, b, (dims, ((), ())), preferred_element_type=F32)


def _bdot(a, b, dims=((1,), (0,))):
    return _dot(a.astype(BF16), b.astype(BF16), dims)


NT = ((1,), (1,))


def _rms_fwd(x, w):
    r = lax.rsqrt(jnp.mean(x * x, axis=-1, keepdims=True) + EPS)
    return (x * r) * w


def _rms_bwd(x, w, dy):
    r = lax.rsqrt(jnp.mean(x * x, axis=-1, keepdims=True) + EPS)
    xh = x * r
    g = dy * w
    dx = r * (g - xh * jnp.mean(g * xh, axis=-1, keepdims=True))
    return dx, dy * xh


def _sigmoid(x):
    return 0.5 * jnp.tanh(0.5 * x) + 0.5


def _colsum8(x):
    t, c = x.shape
    return jnp.sum(x.reshape(t // 8, 8, c), axis=0)


def _rowspec(t, c, cb=0):
    return pl.BlockSpec((t, c), lambda i: (i, cb))


def _fullspec(shape):
    n = len(shape)
    return pl.BlockSpec(shape, lambda i: (0,) * n)


def _sds(shape, dt):
    return jax.ShapeDtypeStruct(shape, dt)


def _acc(ref, val):
    @pl.when(pl.program_id(0) == 0)
    def _():
        ref[...] = jnp.zeros_like(ref)

    ref[...] += val


def _xspec():
    return pl.BlockSpec((ROWB, D_MODEL), lambda i: (jnp.maximum(i - 1, 0), 0))


def _h_block(head_ref, x_ref):
    return jnp.where(pl.program_id(0) == 0, head_ref[...], x_ref[...])


def _norm_in_proj(head, x, g, w_all):
    lp = head.shape[0] + x.shape[0]

    def body(head_ref, x_ref, g_ref, w_ref, hn_ref, cq_ref, ckv_ref, kr_ref, z_ref, xbc_ref, dt_ref):
        hn = _rms_fwd(_h_block(head_ref, x_ref), g_ref[...]).astype(BF16)
        hn_ref[...] = hn
        p = _dot(hn, w_ref[...])
        cq_ref[...] = p[:, PC_Q:PC_KV]
        ckv_ref[...] = p[:, PC_KV:PC_KR]
        kr_ref[...] = p[:, PC_KR:PC_Z]
        z_ref[...] = p[:, PC_Z:PC_XBC]
        xbc_ref[...] = p[:, PC_XBC:PC_DT]
        dt_ref[...] = p[:, PC_DT:PROJ_W]

    widths = (Q_LORA, KV_LORA, HEADW, SSM_WIDTH, CONV_DIM, LANE)
    return pl.pallas_call(
        body, name="norm_in_proj", grid=(lp // ROWB,),
        in_specs=[_fullspec((ROWB, D_MODEL)), _xspec(), _fullspec((1, D_MODEL)), _fullspec((D_MODEL, PROJ_W))],
        out_specs=[_rowspec(ROWB, D_MODEL)] + [_rowspec(ROWB, w) for w in widths],
        out_shape=[_sds((lp, D_MODEL), BF16)] + [_sds((lp, w), F32) for w in widths],
        compiler_params=_cp(("arbitrary",)),
    )(head, x, g, w_all)


def _rope(x, cos, sa, sb):
    w = x.shape[1]
    return x * cos + pltpu.roll(x, w - 32, 1) * sa + pltpu.roll(x, 32, 1) * sb


def _rope_t(dy, cos, sa, sb):
    w = dy.shape[1]
    return dy * cos + pltpu.roll(dy * sa, 32, 1) + pltpu.roll(dy * sb, w - 32, 1)


def _qkv(cq, ckv, kr, cos, sa, sb, gq, gkv, wq, wk, wv):
    lp = cq.shape[0]
    qw = ATT_HEADS * HEADW
    half = HEADW // 2
    assert half == QK_NOPE == V_HEAD == LANE

    def body(cq_ref, ckv_ref, kr_ref, cos_ref, sa_ref, sb_ref, gq_ref, gkv_ref, wq_ref, wk_ref, wv_ref,
             q_ref, k_ref, v_ref, cqn_ref, ckvn_ref):
        tabs = [cos_ref[...], sa_ref[...], sb_ref[...]]
        cqn = _rms_fwd(cq_ref[...], gq_ref[...]).astype(BF16)
        ckvn = _rms_fwd(ckv_ref[...], gkv_ref[...]).astype(BF16)
        cqn_ref[...] = cqn
        ckvn_ref[...] = ckvn
        q = _dot(cqn, wq_ref[...])
        kn = _dot(ckvn, wk_ref[...])
        vv = _dot(ckvn, wv_ref[...])
        krope = _rope(kr_ref[:, pl.ds(half, half)], *tabs).astype(BF16)
        ones = jnp.ones((ROWB, half), BF16)
        for hh in range(ATT_HEADS):
            lo, hi, src = pl.ds(hh * HEADW, half), pl.ds(hh * HEADW + half, half), slice(hh * half, (hh + 1) * half)
            q_ref[:, lo] = (q[:, hh * HEADW:hh * HEADW + half] * Q_PRESCALE).astype(BF16)
            q_ref[:, hi] = (_rope(q[:, hh * HEADW + half:(hh + 1) * HEADW], *tabs) * Q_PRESCALE).astype(BF16)
            k_ref[:, lo] = kn[:, src].astype(BF16)
            k_ref[:, hi] = krope
            v_ref[:, lo] = vv[:, src].astype(BF16)
            v_ref[:, hi] = ones

    return pl.pallas_call(
        body, name="qkv", grid=(lp // ROWB,),
        in_specs=[_rowspec(ROWB, Q_LORA), _rowspec(ROWB, KV_LORA), _rowspec(ROWB, HEADW)]
        + [_rowspec(ROWB, half)] * 3
        + [_fullspec((1, Q_LORA)), _fullspec((1, KV_LORA)), _fullspec((Q_LORA, qw)),
           _fullspec((KV_LORA, ATT_HEADS * QK_NOPE)), _fullspec((KV_LORA, ATT_HEADS * V_HEAD))],
        out_specs=[_rowspec(ROWB, qw), _rowspec(ROWB, qw), _rowspec(ROWB, qw),
                   _rowspec(ROWB, Q_LORA), _rowspec(ROWB, KV_LORA)],
        out_shape=[_sds((lp, qw), BF16), _sds((lp, qw), BF16), _sds((lp, qw), BF16),
                   _sds((lp, Q_LORA), BF16), _sds((lp, KV_LORA), BF16)],
        compiler_params=_cp(("arbitrary",)),
    )(cq, ckv, kr, cos, sa, sb, gq, gkv, wq, wk, wv)


ATT_SCALE = (QK_NOPE + QK_ROPE) ** -0.5
LOG2E = 1.4426950408889634
LN2 = 0.6931471805599453
Q_PRESCALE = ATT_SCALE * LOG2E
KVB = 512
META_KEYS = LANE
assert N_META <= META_KEYS


def _att_ok(qrow, krow):
    return (krow <= qrow) & ((krow >= PADF) | (qrow < PADF))


def _lanes(x, n):
    return x if n == 1 else jnp.concatenate([x] * n, axis=1)


def _pair_loop(lo, hi, tile, unrolls=(2,)):
    for u in tuple(unrolls) + (1,):
        n = jnp.maximum(hi - lo, 0)
        trips = n // u

        def many(t, c, u=u, lo=lo):
            for d in range(u):
                tile(lo + u * t + d)
            return c

        lax.fori_loop(0, trips, many, 0)
        lo = lo + trips * u


def _flash_fwd(q, k, v):
    lp = q.shape[0]
    nq = lp // ROWB

    def body(q_ref, k_ref, v_ref, o_ref, lse_ref, acc, m_s):
        i = pl.program_id(1)
        qb = q_ref[...]
        m_s[...] = jnp.full_like(m_s, NEG)
        acc[...] = jnp.zeros_like(acc)

        def tile(j, masked, off=None, nkeys=KVB):
            off = pl.multiple_of(j * KVB, KVB) if off is None else off
            kb = k_ref[pl.ds(off, nkeys), :]
            vb = v_ref[pl.ds(off, nkeys), :]
            s = _dot(qb, kb, NT)
            if masked:
                qrow = i * ROWB + lax.broadcasted_iota(jnp.int32, s.shape, 0)
                krow = off + lax.broadcasted_iota(jnp.int32, s.shape, 1)
                s = jnp.where(_att_ok(qrow, krow), s, NEG)
            m_prev = m_s[...]
            m_new = jnp.maximum(m_prev, jnp.max(s, axis=1, keepdims=True))
            alpha = jnp.exp2(m_prev - m_new)
            p = jnp.exp2(s - _lanes(m_new, nkeys // LANE))
            acc[...] = _lanes(alpha, 2) * acc[...] + _dot(p.astype(BF16), vb)
            m_s[...] = m_new

        def first_tile():
            tile(0, True, off=ROWB - META_KEYS, nkeys=META_KEYS)

        @pl.when(i == 0)
        def _():
            first_tile()

        @pl.when(i > 0)
        def _():
            first_tile()
            tile(i, True)

        _pair_loop(1, i, lambda j: tile(j, False), (16, 8, 4, 2))
        l = acc[:, V_HEAD:]
        o_ref[...] = (acc[:, :V_HEAD] / l).astype(BF16)
        lse_ref[0] = (m_s[...] + jnp.log2(l)).T[0:1, :]

    return pl.pallas_call(
        body, name="flash_fwd", grid=(ATT_HEADS, nq),
        in_specs=[pl.BlockSpec((ROWB, HEADW), lambda h, i: (i, h)),
                  pl.BlockSpec((lp, HEADW), lambda h, i: (0, h)),
                  pl.BlockSpec((lp, HEADW), lambda h, i: (0, h))],
        out_specs=[pl.BlockSpec((ROWB, V_HEAD), lambda h, i: (i, h)),
                   pl.BlockSpec((1, 1, ROWB), lambda h, i: (h, 0, i))],
        out_shape=[_sds((lp, ATT_HEADS * V_HEAD), BF16), _sds((ATT_HEADS, 1, lp), F32)],
        scratch_shapes=[pltpu.VMEM((ROWB, HEADW), F32), pltpu.VMEM((ROWB, LANE), F32)],
        compiler_params=_cp(("arbitrary", "arbitrary")),
    )(q, k, v)


def _silu(x):
    return x * _sigmoid(x)


CONV_ROWS = 64


def _shifted_rows(ref, start, n, cols, offsets):
    win = ref[pl.ds(start, n + 8), cols]
    return [win[o:o + n] if o % 8 == 0 else pltpu.roll(win, n + 8 - o, 0)[0:n] for o in offsets]


def _conv_fwd(xbc, cw, cb):
    lp, c = xbc.shape
    t8 = ROWB // 8

    def body(x_ref, prev_ref, w_ref, b_ref, o_ref, buf):
        i = pl.program_id(0)
        buf[pl.ds(0, 8), :] = jnp.where(i > 0, prev_ref[...], 0.0)
        buf[pl.ds(8, ROWB), :] = x_ref[...]

        def strip(s, carry):
            cs = pl.ds(pl.multiple_of(s * LANE, LANE), LANE)
            w, b = w_ref[:, cs], b_ref[:, cs]
            for r0 in range(0, ROWB, CONV_ROWS):
                taps = _shifted_rows(buf, r0, CONV_ROWS, cs, [8 - (CONV_K - 1) + kk for kk in range(CONV_K)])
                pre = b + sum(w[kk:kk + 1, :] * taps[kk] for kk in range(CONV_K))
                o_ref[pl.ds(r0, CONV_ROWS), cs] = _silu(pre)
            return carry

        lax.fori_loop(0, c // LANE, strip, 0)

    return pl.pallas_call(
        body, name="conv_fwd", grid=(lp // ROWB,),
        in_specs=[_rowspec(ROWB, c), pl.BlockSpec((8, c), lambda i: (jnp.maximum(i * t8 - 1, 0), 0)),
                  _fullspec((8, c)), _fullspec((1, c))],
        out_specs=_rowspec(ROWB, c), out_shape=_sds((lp, c), F32),
        scratch_shapes=[pltpu.VMEM((ROWB + 8, c), F32)],
        compiler_params=_cp(("arbitrary",)),
    )(xbc, xbc, cw, cb)


def _expand_mat():
    r = np.arange(LANE)[:, None]
    c = np.arange(SSM_WIDTH)[None, :]
    return jnp.asarray((c // SSM_HEAD_DIM == r).astype(np.float32))


def _tri_mat():
    i = np.arange(CHUNK)
    return jnp.asarray((i[:, None] >= i[None, :]).astype(np.float32))


def _x3(m, axis):
    return jnp.concatenate([m.astype(BF16)] * 3, axis=axis)


def _split3(x):
    hi = x.astype(BF16)
    r = x - hi.astype(F32)
    mid = r.astype(BF16)
    return hi, mid, (r - mid.astype(F32)).astype(BF16)


def _dot01_r(x, m3):
    return _dot(jnp.concatenate(_split3(x), axis=1), m3)


def _dot01_l(m3, x):
    return _dot(m3, jnp.concatenate(_split3(x), axis=0))


def _ssd_prep(dt_raw, bias_ref, alog_ref, tri3, c, seq_rows):
    rows = c * CHUNK + lax.broadcasted_iota(jnp.int32, (CHUNK, LANE), 0)
    lanes = lax.broadcasted_iota(jnp.int32, (CHUNK, LANE), 1)
    valid = (rows >= PADF) & (rows < PADF + seq_rows) & (lanes < SSM_HEADS)
    dtr = dt_raw + bias_ref[...]
    sp = jnp.maximum(dtr, 0.0) + jnp.log(1.0 + jnp.exp(-jnp.abs(dtr)))
    dt = jnp.where(valid, sp, 0.0)
    a = -jnp.exp(alog_ref[...])
    acol = _dot01_l(tri3, dt * a)
    return dt, a, acol, valid, dtr


def _row16(v):
    return jnp.broadcast_to(v, (16, v.shape[1]))


def _ssd_fwd(xbc_act, dtr, dt_bias, a_log, seq_rows, gather=()):
    lp = xbc_act.shape[0]
    nc = lp // CHUNK
    cps = ROWB // CHUNK
    nsteps = nc // cps
    gw = SSM_WIDTH // SSM_GROUPS
    hpg = SSM_HEADS // SSM_GROUPS

    na = len(gather)

    def body(x_ref, b_ref, c_ref, dtr_ref, bias_ref, alog_ref, tri_ref, tri3_ref, ex3_ref, *rest):
        gin, (y_ref, hp_ref), gout, h_s, sems = rest[:na], rest[na:na + 2], rest[na + 2:2 * na + 2], rest[2 * na + 2], rest[2 * na + 3:]
        step = pl.program_id(0)

        @pl.when(step == 0)
        def _():
            h_s[...] = jnp.zeros_like(h_s)

        if na:
            g_start, g_forward, g_finish = _gather_ops(gin, gout, *sems)
            pl.when(step == 0)(g_start)
            pl.when(step == nsteps // 2)(g_forward)

        ex3 = ex3_ref[...]
        causal = tri_ref[...] > 0.5
        for cc in range(cps):
            rows = pl.ds(cc * CHUNK, CHUNK)
            dt, a, acol, _, _ = _ssd_prep(dtr_ref[rows, :], bias_ref, alog_ref, tri3_ref[...], step * cps + cc, seq_rows)
            arow = acol.T
            dtrow = dt.T
            alast = acol[CHUNK - 1:CHUNK, :]
            e_all = _dot01_r(jnp.exp(acol), ex3)
            wx_all = _dot01_r(jnp.exp(alast - acol) * dt, ex3)
            dec_all = _dot01_r(_row16(jnp.exp(alast)), ex3)[0:1, :]
            hp_ref[cc] = h_s[...]
            for g in range(SSM_GROUPS):
                gs = slice(g * gw, (g + 1) * gw)
                bg = b_ref[rows, g * SSM_STATE:(g + 1) * SSM_STATE]
                cg = c_ref[rows, g * SSM_STATE:(g + 1) * SSM_STATE].astype(BF16)
                xg = x_ref[rows, gs]
                hg = h_s[:, gs]
                gm = _bdot(cg, bg, NT)
                y_off = _bdot(cg, hg) * e_all[:, gs]
                for r in range(hpg):
                    hd = g * hpg + r
                    seg = acol[:, hd:hd + 1] - arow[hd:hd + 1, :]
                    lm = jnp.where(causal, jnp.exp(jnp.where(causal, seg, 0.0)), 0.0)
                    w = gm * lm * dtrow[hd:hd + 1, :]
                    cs = slice(r * SSM_HEAD_DIM, (r + 1) * SSM_HEAD_DIM)
                    y_ref[rows, pl.ds(hd * SSM_HEAD_DIM, SSM_HEAD_DIM)] = _bdot(w, xg[:, cs]) + y_off[:, cs]
                st = _bdot(bg.T, xg * wx_all[:, gs])
                h_s[:, gs] = hg * dec_all[:, gs] + st

        if na:
            pl.when(step == nsteps - 1)(g_finish)

    xs_spec = pl.BlockSpec((ROWB, SSM_WIDTH), lambda c: (c, 0))
    b_spec = pl.BlockSpec((ROWB, 2 * SSM_STATE), lambda c: (c, SSM_WIDTH // (2 * SSM_STATE)))
    c_spec = pl.BlockSpec((ROWB, 2 * SSM_STATE), lambda c: (c, SSM_WIDTH // (2 * SSM_STATE) + 1))
    y, hprev, *gathered = pl.pallas_call(
        body, name="ssd_fwd", grid=(nsteps,),
        in_specs=[xs_spec, b_spec, c_spec, pl.BlockSpec((ROWB, LANE), lambda c: (c, 0)),
                  _fullspec((1, LANE)), _fullspec((1, LANE)), _fullspec((CHUNK, CHUNK)), _fullspec((CHUNK, 3 * CHUNK)),
                  _fullspec((3 * LANE, SSM_WIDTH))] + [ANY] * na,
        out_specs=[xs_spec, pl.BlockSpec((cps, SSM_STATE, SSM_WIDTH), lambda c: (c, 0, 0))] + [ANY] * na,
        out_shape=[_sds((lp, SSM_WIDTH), F32), _sds((nc, SSM_STATE, SSM_WIDTH), F32)]
        + [_sds((N_DEV,) + s.shape, s.dtype) for s in gather],
        scratch_shapes=[pltpu.VMEM((SSM_STATE, SSM_WIDTH), F32)] + (_gather_scratch(na) if na else []),
        compiler_params=_cp(("arbitrary",)),
    )(xbc_act, xbc_act, xbc_act, dtr, dt_bias, a_log, _tri_mat(), _x3(_tri_mat(), 1), _x3(_expand_mat(), 0), *gather)
    return y, hprev, gathered


def _group_mean(x):
    gw = SSM_WIDTH // SSM_GROUPS
    parts = [jnp.broadcast_to(jnp.mean(x[:, g * gw:(g + 1) * gw], axis=-1, keepdims=True), (x.shape[0], gw))
             for g in range(SSM_GROUPS)]
    return jnp.concatenate(parts, axis=1)


def _ssd_post(y, xbc_act, z, dskip, gnorm):
    lp = y.shape[0]

    def body(y_ref, x_ref, z_ref, d_ref, g_ref, o_ref):
        z_ = z_ref[...]
        gt = (y_ref[...] + d_ref[...] * x_ref[...]) * _silu(z_)
        r = lax.rsqrt(_group_mean(gt * gt) + EPS)
        o_ref[...] = ((gt * r) * g_ref[...]).astype(BF16)

    return pl.pallas_call(
        body, name="ssd_post", grid=(lp // ROWB,),
        in_specs=[_rowspec(ROWB, SSM_WIDTH)] * 3 + [_fullspec((1, SSM_WIDTH))] * 2,
        out_specs=_rowspec(ROWB, SSM_WIDTH), out_shape=_sds((lp, SSM_WIDTH), BF16),
        compiler_params=_cp(("arbitrary",)),
    )(y, xbc_act, z, dskip, gnorm)


def _out_proj(att, ssm, head, x, w_out, g_post):
    lp = att.shape[0]

    def body(a_ref, s_ref, head_ref, x_ref, w_ref, g_ref, mix_ref, h1_ref):
        mix = _dot(a_ref[...], w_ref[pl.ds(0, 1024), :]) + _dot(s_ref[...], w_ref[pl.ds(1024, 1024), :])
        mix_ref[...] = mix
        h1_ref[...] = _h_block(head_ref, x_ref) + _rms_fwd(mix, g_ref[...])

    return pl.pallas_call(
        body, name="out_proj", grid=(lp // ROWB,),
        in_specs=[_rowspec(ROWB, 1024)] * 2 + [_fullspec((ROWB, D_MODEL)), _xspec(), _fullspec((2048, D_MODEL)),
                                               _fullspec((1, D_MODEL))],
        out_specs=[_rowspec(ROWB, D_MODEL)] * 2, out_shape=[_sds((lp, D_MODEL), F32)] * 2,
        compiler_params=_cp(("arbitrary",)),
    )(att, ssm, head, x, w_out, g_post)


def _resident(w_hbm, w_vmem, sem):
    @pl.when(pl.program_id(0) == 0)
    def _():
        cp = pltpu.make_async_copy(w_hbm, w_vmem, sem)
        cp.start()
        cp.wait()


ANY = pl.BlockSpec(memory_space=pl.ANY)


def _mlp_fwd(h1, tgt, w_up, w_down, g_pre, g_post, seq_rows):
    lp = h1.shape[0]

    def body(h1_ref, t_ref, wu_hbm, wd_hbm, gpre_ref, gpost_ref, hn2_ref, u_ref, a_ref, f_ref, dh2_ref, loss_ref,
             wu, wd, sems):
        _resident(wu_hbm, wu, sems.at[0])
        _resident(wd_hbm, wd, sems.at[1])
        i = pl.program_id(0)
        h1_ = h1_ref[...]
        hn2 = _rms_fwd(h1_, gpre_ref[...]).astype(BF16)
        hn2_ref[...] = hn2
        u = jnp.maximum(_dot(hn2, wu[...]), 0.0)
        u_ref[...] = u.astype(BF16)
        a = (u * u).astype(BF16)
        a_ref[...] = a
        f = _dot(a, wd[...])
        f_ref[...] = f
        h2 = h1_ + _rms_fwd(f, gpost_ref[...])
        rows = i * ROWB + lax.broadcasted_iota(jnp.int32, (ROWB, 1), 0)
        real = (rows >= PADF + N_META) & (rows < PADF + seq_rows)
        err = jnp.where(real, h2 - t_ref[...], 0.0)
        dh2_ref[...] = err * (1.0 / D_MODEL)
        _acc(loss_ref, _colsum8(err * err))

    return pl.pallas_call(
        body, name="mlp_fwd", grid=(lp // ROWB,),
        in_specs=[_rowspec(ROWB, D_MODEL), _xspec()] + [ANY, ANY] + [_fullspec((1, D_MODEL))] * 2,
        out_specs=[_rowspec(ROWB, D_MODEL), _rowspec(ROWB, D_FF), _rowspec(ROWB, D_FF)] + [_rowspec(ROWB, D_MODEL)] * 2
        + [_fullspec((8, D_MODEL))],
        out_shape=[_sds((lp, D_MODEL), BF16), _sds((lp, D_FF), BF16), _sds((lp, D_FF), BF16), _sds((lp, D_MODEL), F32),
                   _sds((lp, D_MODEL), F32), _sds((8, D_MODEL), F32)],
        scratch_shapes=[pltpu.VMEM((D_MODEL, D_FF), BF16), pltpu.VMEM((D_FF, D_MODEL), BF16), pltpu.SemaphoreType.DMA((2,))],
        compiler_params=_cp(("arbitrary",)),
    )(h1, tgt, w_up, w_down, g_pre, g_post)


def _pad_cols(w, width):
    return jnp.pad(w, ((0, 0), (0, width - w.shape[1])))


def _layout_weights(w_in, w_q_up, w_kv_up):
    o = np.cumsum((0,) + IN_SPLITS)
    pieces = [w_in[:, o[k]:o[k + 1]] for k in range(6)]
    kr = jnp.pad(pieces[2], ((0, 0), (QK_NOPE, HEADW - QK_NOPE - QK_ROPE)))
    w_all = jnp.concatenate([pieces[0], pieces[1], kr, pieces[3], pieces[4], _pad_cols(pieces[5], LANE)], axis=1)
    wq = jnp.pad(w_q_up.reshape(Q_LORA, ATT_HEADS, QK_NOPE + QK_ROPE), ((0, 0), (0, 0), (0, HEADW - QK_NOPE - QK_ROPE)))
    wkv = w_kv_up.reshape(KV_LORA, ATT_HEADS, QK_NOPE + V_HEAD)
    wk = jnp.pad(wkv[:, :, :QK_NOPE], ((0, 0), (0, 0), (0, HEADW - QK_NOPE)))
    wv = wkv[:, :, QK_NOPE:]
    return (w_all, wq.reshape(Q_LORA, -1), wk.reshape(KV_LORA, -1), wv.reshape(KV_LORA, -1),
            wkv[:, :, :QK_NOPE].reshape(KV_LORA, -1))


def _rope_tables(lp):
    pos = jnp.maximum(jnp.arange(lp, dtype=jnp.int32) - PADF, 0).astype(F32)
    inv_freq = ROPE_THETA ** (-jnp.arange(0, QK_ROPE, 2, dtype=F32) / QK_ROPE)
    ang = pos[:, None] * inv_freq[None, :]
    cos, sin = jnp.cos(ang), jnp.sin(ang)
    z32, z64 = jnp.zeros((lp, 32), F32), jnp.zeros((lp, 64), F32)
    cos_t = jnp.concatenate([cos, cos, jnp.ones((lp, 64), F32)], axis=1)
    sa = jnp.concatenate([-sin, z32, z64], axis=1)
    sb = jnp.concatenate([z32, sin, z64], axis=1)
    return cos_t, sa, sb


def _row1(v, width=None):
    v = v.reshape(1, -1).astype(F32)
    return v if width is None else _pad_cols(v, width)


LATE = ("w_out", "w_mlp_up", "w_mlp_down")
MID = ("w_in", "w_q_up", "w_kv_up", "conv_w")


def _local_forward(head, x, tgt, p, late=None):
    assert head.shape[0] == ROWB and x.shape[0] % ROWB == 0
    lp = ROWB + x.shape[0]
    seq_rows = N_META + x.shape[0]
    f = {"seq_rows": seq_rows}
    w_all, wq, wk, wv, wkn = _layout_weights(p["w_in"], p["w_q_up"], p["w_kv_up"])
    f.update(w_all=w_all, wq=wq, wk=wk, wv=wv)
    f["hn"], cq, ckv, kr, f["z"], f["xbc"], f["dtr"] = _norm_in_proj(head, x, _row1(p["norm_mix_pre"]), w_all)
    f.update(cq=cq, ckv=ckv)
    f["rope"] = _rope_tables(lp)
    f["q"], f["k"], f["v"], f["cqn"], f["ckvn"] = _qkv(cq, ckv, kr, *f["rope"], _row1(p["q_a_norm"]),
                                                   _row1(p["kv_a_norm"]), wq, wkn, wv)
    f["att"], f["lse"] = _flash_fwd(f["q"], f["k"], f["v"])
    f["cw"] = jnp.pad(p["conv_w"].astype(F32), ((0, 8 - CONV_K), (0, 0)))
    f["xact"] = _conv_fwd(f["xbc"], f["cw"], _row1(p["conv_b"]))
    f["dt_bias"], f["a_log"] = _row1(p["dt_bias"], LANE), _row1(p["a_log"], LANE)
    f["y"], f["hprev"], gathered = _ssd_fwd(f["xact"], f["dtr"], f["dt_bias"], f["a_log"], seq_rows,
                                            gather=[late[n] for n in LATE] if late else ())
    p = {**p, **{n: _from_shards(n, s) for n, s in zip(LATE, gathered)}}
    f["p"] = p
    f["dskip"] = jnp.repeat(p["d_skip"].reshape(-1).astype(F32), SSM_HEAD_DIM).reshape(1, SSM_WIDTH)
    f["ssm"] = _ssd_post(f["y"], f["xact"], f["z"], f["dskip"], _row1(p["ssm_norm"]))
    f["mix"], f["h1"] = _out_proj(f["att"], f["ssm"], head, x, p["w_out"], _row1(p["norm_mix_post"]))
    f["hn2"], f["u"], f["a"], f["f"], f["dh2"], loss8 = _mlp_fwd(
        f["h1"], tgt, p["w_mlp_up"], p["w_mlp_down"], _row1(p["norm_mlp_pre"]), _row1(p["norm_mlp_post"]), seq_rows)
    f["loss"] = 0.5 * jnp.sum(loss8) / D_MODEL
    return f


MLPB = 256


def _mlp_bwd(dh2, f, h1, u, w_up, w_down, g_pre, g_post):
    lp = h1.shape[0]

    def body(dh2_ref, f_ref, h1_ref, u_ref, wu_hbm, wd_hbm, gpre_ref, gpost_ref,
             dh1_ref, du_ref, df_ref, dgpre_ref, dgpost_ref, wu, wd, sems):
        _resident(wu_hbm, wu, sems.at[0])
        _resident(wd_hbm, wd, sems.at[1])
        dh2_ = dh2_ref[...]
        df, dgp = _rms_bwd(f_ref[...], gpost_ref[...], dh2_)
        dfb = df.astype(BF16)
        df_ref[...] = dfb
        da = _dot(dfb, wd[...], NT)
        du = (da * (2.0 * u_ref[...].astype(F32))).astype(BF16)
        du_ref[...] = du
        dhn2 = _dot(du, wu[...], NT)
        dx, dgq = _rms_bwd(h1_ref[...], gpre_ref[...], dhn2)
        dh1_ref[...] = dh2_ + dx
        _acc(dgpre_ref, _colsum8(dgq))
        _acc(dgpost_ref, _colsum8(dgp))

    return pl.pallas_call(
        body, name="mlp_bwd", grid=(lp // MLPB,),
        in_specs=[_rowspec(MLPB, D_MODEL)] * 3 + [_rowspec(MLPB, D_FF), ANY, ANY] + [_fullspec((1, D_MODEL))] * 2,
        out_specs=[_rowspec(MLPB, D_MODEL), _rowspec(MLPB, D_FF), _rowspec(MLPB, D_MODEL),
                   _fullspec((8, D_MODEL)), _fullspec((8, D_MODEL))],
        out_shape=[_sds((lp, D_MODEL), F32), _sds((lp, D_FF), BF16), _sds((lp, D_MODEL), BF16),
                   _sds((8, D_MODEL), F32), _sds((8, D_MODEL), F32)],
        scratch_shapes=[pltpu.VMEM((D_MODEL, D_FF), BF16), pltpu.VMEM((D_FF, D_MODEL), BF16), pltpu.SemaphoreType.DMA((2,))],
        compiler_params=_cp(("arbitrary",)),
    )(dh2, f, h1, u, w_up, w_down, g_pre, g_post)


def _out_bwd(dh1, mix, att, w_out, g_post):
    lp = dh1.shape[0]

    def body(dh1_ref, mix_ref, att_ref, w_ref, g_ref, dmix_ref, datt_ref, dssm_ref, dg_ref, dl_ref):
        dmix, dg = _rms_bwd(mix_ref[...], g_ref[...], dh1_ref[...])
        dmb = dmix.astype(BF16)
        dmix_ref[...] = dmb
        datt = _dot(dmb, w_ref[pl.ds(0, 1024), :], NT).astype(BF16)
        datt_ref[...] = datt
        dssm_ref[...] = _dot(dmb, w_ref[pl.ds(1024, 1024), :], NT)
        _acc(dg_ref, _colsum8(dg))
        prod = datt.astype(F32) * att_ref[...].astype(F32)
        for hh in range(ATT_HEADS):
            d = jnp.sum(prod[:, hh * V_HEAD:(hh + 1) * V_HEAD], axis=1, keepdims=True)
            dl_ref[hh] = jnp.broadcast_to(d, (ROWB, LANE)).T[0:1, :]

    return pl.pallas_call(
        body, name="out_bwd", grid=(lp // ROWB,),
        in_specs=[_rowspec(ROWB, D_MODEL)] * 3 + [_fullspec((2048, D_MODEL)), _fullspec((1, D_MODEL))],
        out_specs=[_rowspec(ROWB, D_MODEL)] * 3 + [_fullspec((8, D_MODEL)),
                                                   pl.BlockSpec((ATT_HEADS, 1, ROWB), lambda i: (0, 0, i))],
        out_shape=[_sds((lp, D_MODEL), BF16), _sds((lp, 1024), BF16), _sds((lp, 1024), F32), _sds((8, D_MODEL), F32),
                   _sds((ATT_HEADS, 1, lp), F32)],
        compiler_params=_cp(("arbitrary",)),
    )(dh1, mix, att, w_out, g_post)


def _ssd_post_bwd(dssm, y, xact, z, dskip, gnorm):
    lp = y.shape[0]

    def body(do_ref, y_ref, x_ref, z_ref, d_ref, g_ref, dy_ref, dz_ref, dg_ref, dd_ref):
        z_, x_ = z_ref[...], x_ref[...]
        sg = _sigmoid(z_)
        sz = z_ * sg
        y2 = y_ref[...] + d_ref[...] * x_
        gt = y2 * sz
        r = lax.rsqrt(_group_mean(gt * gt) + EPS)
        gh = gt * r
        do = do_ref[...]
        dgh = do * g_ref[...]
        dgt = r * (dgh - gh * _group_mean(dgh * gh))
        dy2 = dgt * sz
        dy_ref[...] = dy2
        dz_ref[...] = (dgt * y2 * (sg * (1.0 + z_ * (1.0 - sg)))).astype(BF16)
        _acc(dg_ref, _colsum8(do * gh))
        _acc(dd_ref, _colsum8(dy2 * x_))

    return pl.pallas_call(
        body, name="ssd_post_bwd", grid=(lp // ROWB,),
        in_specs=[_rowspec(ROWB, SSM_WIDTH)] * 4 + [_fullspec((1, SSM_WIDTH))] * 2,
        out_specs=[_rowspec(ROWB, SSM_WIDTH)] * 2 + [_fullspec((8, SSM_WIDTH))] * 2,
        out_shape=[_sds((lp, SSM_WIDTH), F32), _sds((lp, SSM_WIDTH), BF16), _sds((8, SSM_WIDTH), F32), _sds((8, SSM_WIDTH), F32)],
        compiler_params=_cp(("arbitrary",)),
    )(dssm, y, xact, z, dskip, gnorm)


def _ssd_bwd(dy, xact, dtr, hprev, dt_bias, a_log, dskip, seq_rows, exchange=()):
    lp = xact.shape[0]
    nc = lp // CHUNK
    gw = SSM_WIDTH // SSM_GROUPS
    hpg = SSM_HEADS // SSM_GROUPS
    nb = SSM_WIDTH // (2 * SSM_STATE)
    na = len(exchange)
    cps = ROWB // CHUNK
    nsteps = nc // cps

    def body(dy_ref, x_ref, b_ref, c_ref, dtr_ref, hp_ref, bias_ref, alog_ref, dsk_ref, tri_ref, tri3_ref, trit3_ref,
             ex3_ref, ext3_ref, *rest):
        xin, (dact_ref, ddtr_ref, da_ref, dbias_ref), xout = rest[:na], rest[na:na + 4], rest[na + 4:2 * na + 4]
        dh_s, sems = rest[2 * na + 4], rest[2 * na + 5:]
        step = pl.program_id(0)

        @pl.when(step == 0)
        def _():
            dh_s[...] = jnp.zeros_like(dh_s)
            da_ref[...] = jnp.zeros_like(da_ref)
            dbias_ref[...] = jnp.zeros_like(dbias_ref)

        if na:
            x_start, x_finish = _exchange_ops(xin, xout, *sems)
            pl.when(step == 0)(x_start)

        for lc in reversed(range(cps)):
            rows = pl.ds(lc * CHUNK, CHUNK)
            chunk((nsteps - 1 - step) * cps + lc, dy_ref.at[rows], x_ref.at[rows], b_ref.at[rows], c_ref.at[rows],
                  dtr_ref.at[rows], hp_ref.at[pl.ds(lc, 1)], bias_ref, alog_ref, dsk_ref, tri_ref, tri3_ref, trit3_ref,
                  ex3_ref, ext3_ref, dact_ref.at[rows], ddtr_ref.at[rows], da_ref, dbias_ref, dh_s)

        if na:
            pl.when(step == nsteps - 1)(x_finish)

    def chunk(c, dy_ref, x_ref, b_ref, c_ref, dtr_ref, hp_ref, bias_ref, alog_ref, dsk_ref, tri_ref, tri3_ref, trit3_ref,
              ex3_ref, ext3_ref, dact_ref, ddtr_ref, da_ref, dbias_ref, dh_s):
        tri = tri_ref[...]
        ex3 = ex3_ref[...]
        dt, a, acol, valid, dtr_ = _ssd_prep(dtr_ref[...], bias_ref, alog_ref, tri3_ref[...], c, seq_rows)
        arow = acol.T
        dtrow = dt.T
        alast = acol[CHUNK - 1:CHUNK, :]
        e_all = _dot01_r(jnp.exp(acol), ex3)
        wgt0 = jnp.exp(alast - acol)
        wgt = wgt0 * dt
        wx_all = _dot01_r(wgt, ex3)
        elast = jnp.exp(alast)
        dec_all = _dot01_r(_row16(elast), ex3)[0:1, :]
        causal = tri > 0.5
        upper = tri.T > 0.5
        lane_id = lax.broadcasted_iota(jnp.int32, (1, LANE), 1)
        sub_id = lax.broadcasted_iota(jnp.int32, (CHUNK, 1), 0)
        dacol = jnp.zeros((CHUNK, LANE), F32)
        darowf = jnp.zeros((CHUNK, LANE), F32)
        ddtrowf = jnp.zeros((CHUNK, LANE), F32)
        dwgt = jnp.zeros((CHUNK, LANE), F32)
        delast = jnp.zeros((1, LANE), F32)
        for g in range(SSM_GROUPS):
            gs = slice(g * gw, (g + 1) * gw)
            ext3_g = ext3_ref[g]
            bg = b_ref[:, g * SSM_STATE:(g + 1) * SSM_STATE]
            cg = c_ref[:, g * SSM_STATE:(g + 1) * SSM_STATE]
            bgb, cgb = bg.astype(BF16), cg.astype(BF16)
            xg = x_ref[:, gs]
            dyg = dy_ref[:, gs]
            hg = hp_ref[0, :, gs]
            dhg = dh_s[:, gs]
            hgb, dhgb = hg.astype(BF16), dhg.astype(BF16)
            gm = _dot(cgb, bgb, NT)
            gmt = _dot(bgb, cgb, NT)
            y_off = _dot(cgb, hgb) * e_all[:, gs]
            dy0 = (dyg * e_all[:, gs]).astype(BF16)
            dcg = _dot(dy0, hgb, NT)
            dh_in = _dot(cg.T.astype(BF16), dy0) + dhg * dec_all[:, gs]
            dacol = dacol + _dot01_r(dyg * y_off, ext3_g)
            xw = xg * wx_all[:, gs]
            dxw = _dot(bgb, dhgb)
            dx_state = dxw * wx_all[:, gs]
            dwgt = dwgt + _dot01_r(dxw * xg, ext3_g)
            dbt = _dot(dhgb, xw.astype(BF16), NT)
            hh = _colsum8(dhg * hg)
            hh16 = jnp.concatenate([hh, jnp.zeros_like(hh)], axis=0)
            delast = delast + jnp.sum(_dot01_r(hh16, ext3_g), axis=0, keepdims=True)
            dgm = jnp.zeros((CHUNK, CHUNK), F32)
            for r in range(hpg):
                hd = g * hpg + r
                cs = slice(r * SSM_HEAD_DIM, (r + 1) * SSM_HEAD_DIM)
                acol_r, arow_r = acol[:, hd:hd + 1], arow[hd:hd + 1, :]
                dtrow_r, dtcol_r = dtrow[hd:hd + 1, :], dt[:, hd:hd + 1]
                lm = jnp.where(causal, jnp.exp(jnp.where(causal, acol_r - arow_r, 0.0)), 0.0)
                lmt = jnp.where(upper, jnp.exp(jnp.where(upper, arow_r - acol_r, 0.0)), 0.0)
                wt = gmt * lmt * dtcol_r
                dy_r = dyg[:, cs].astype(BF16)
                dx_r = _dot(wt.astype(BF16), dy_r)
                dw = _dot(dy_r, xg[:, cs].astype(BF16), NT)
                t1 = dw * lm
                dgm = dgm + t1 * dtrow_r
                q1 = t1 * gm
                m = q1 * dtrow_r
                dacol = dacol + jnp.sum(m, axis=1, keepdims=True) * (lane_id == hd).astype(F32)
                darowf = darowf - (sub_id == hd).astype(F32) * jnp.sum(m, axis=0, keepdims=True)
                ddtrowf = ddtrowf + (sub_id == hd).astype(F32) * jnp.sum(q1, axis=0, keepdims=True)
                dact_ref[:, pl.ds(hd * SSM_HEAD_DIM, SSM_HEAD_DIM)] = (
                    dx_r + dx_state[:, cs] + dyg[:, cs] * dsk_ref[:, pl.ds(hd * SSM_HEAD_DIM, SSM_HEAD_DIM)])
            dgmb = dgm.astype(BF16)
            dact_ref[:, pl.ds(SSM_WIDTH + g * SSM_STATE, SSM_STATE)] = dbt.T + _dot(dgm.T.astype(BF16), cgb)
            dact_ref[:, pl.ds(SSM_WIDTH + 2 * SSM_STATE + g * SSM_STATE, SSM_STATE)] = dcg + _dot(dgmb, bgb)
            dh_s[:, gs] = dh_in
        t = dwgt * wgt
        dalast = jnp.sum(t, axis=0, keepdims=True) + delast * elast
        dacol_tot = dacol - t + darowf.T + (sub_id == CHUNK - 1).astype(F32) * dalast
        dda = _dot01_l(trit3_ref[...], dacol_tot)
        ddt = dwgt * wgt0 + ddtrowf.T + dda * a
        ddtr = jnp.where(valid, ddt * _sigmoid(dtr_), 0.0)
        ddtr_ref[...] = ddtr
        da_ref[...] += _colsum8(dda * dt) * a
        dbias_ref[...] += _colsum8(ddtr)

    rev = lambda c: nsteps - 1 - c
    rb = cps * CHUNK
    xs_spec = pl.BlockSpec((rb, SSM_WIDTH), lambda c: (rev(c), 0))
    dact, ddtr, da8, dbias8, *received = pl.pallas_call(
        body, name="ssd_bwd", grid=(nsteps,),
        in_specs=[xs_spec, xs_spec,
                  pl.BlockSpec((rb, 2 * SSM_STATE), lambda c: (rev(c), nb)),
                  pl.BlockSpec((rb, 2 * SSM_STATE), lambda c: (rev(c), nb + 1)),
                  pl.BlockSpec((rb, LANE), lambda c: (rev(c), 0)),
                  pl.BlockSpec((cps, SSM_STATE, SSM_WIDTH), lambda c: (rev(c), 0, 0)),
                  _fullspec((1, LANE)), _fullspec((1, LANE)), _fullspec((1, SSM_WIDTH)),
                  _fullspec((CHUNK, CHUNK)), _fullspec((CHUNK, 3 * CHUNK)), _fullspec((CHUNK, 3 * CHUNK)),
                  _fullspec((3 * LANE, SSM_WIDTH)), _fullspec((SSM_GROUPS, 3 * gw, LANE))] + [ANY] * na,
        out_specs=[pl.BlockSpec((rb, CONV_DIM), lambda c: (rev(c), 0)), pl.BlockSpec((rb, LANE), lambda c: (rev(c), 0)),
                   _fullspec((8, LANE)), _fullspec((8, LANE))] + [ANY] * na,
        out_shape=[_sds((lp, CONV_DIM), F32), _sds((lp, LANE), F32), _sds((8, LANE), F32), _sds((8, LANE), F32)]
        + [_sds(e.shape, e.dtype) for e in exchange],
        scratch_shapes=[pltpu.VMEM((SSM_STATE, SSM_WIDTH), F32)] + (_gather_scratch(na) if na else []),
        compiler_params=_cp(("arbitrary",)),
    )(dy, xact, xact, xact, dtr, hprev, dt_bias, a_log, dskip, _tri_mat(), _x3(_tri_mat(), 1), _x3(_tri_mat().T, 1),
      _x3(_expand_mat(), 0), jnp.stack([_x3(_expand_mat().T[g * gw:(g + 1) * gw], 0) for g in range(SSM_GROUPS)]),
      *exchange)
    return dact, ddtr, da8, dbias8, received


def _conv_bwd(dact, xbc, cw, cb):
    lp, c = xbc.shape
    t8 = ROWB // 8
    nb = lp // ROWB

    def body(d_ref, dnext_ref, x_ref, prev_ref, next_ref, w_ref, b_ref, dx_ref, dw_ref, db_ref, xb, dp):
        i = pl.program_id(0)
        last = i == nb - 1
        xb[pl.ds(0, 8), :] = jnp.where(i > 0, prev_ref[...], 0.0)
        xb[pl.ds(8, ROWB), :] = x_ref[...]
        xb[pl.ds(8 + ROWB, 8), :] = jnp.where(last, 0.0, next_ref[...])

        @pl.when(i == 0)
        def _():
            dw_ref[...] = jnp.zeros_like(dw_ref)
            db_ref[...] = jnp.zeros_like(db_ref)

        sub = lax.broadcasted_iota(jnp.int32, (8, 1), 0)
        x0 = 8 - (CONV_K - 1)

        def strip(s, carry):
            cs = pl.ds(pl.multiple_of(s * LANE, LANE), LANE)
            w, b = w_ref[:, cs], b_ref[:, cs]

            def dpre_rows(r0, n, d):
                xs = _shifted_rows(xb, r0, n, cs, [x0 + kk for kk in range(CONV_K)])
                pre = b + sum(w[kk:kk + 1, :] * xs[kk] for kk in range(CONV_K))
                sg = _sigmoid(pre)
                return d * (sg * (1.0 + pre * (1.0 - sg))), xs

            dws = [jnp.zeros((8, LANE), F32) for _ in range(CONV_K)]
            dbs = jnp.zeros((8, LANE), F32)
            for r0 in range(0, ROWB, CONV_ROWS):
                dpre, xs = dpre_rows(r0, CONV_ROWS, d_ref[pl.ds(r0, CONV_ROWS), cs])
                dp[pl.ds(r0, CONV_ROWS), cs] = dpre
                dbs = dbs + _colsum8(dpre)
                for kk in range(CONV_K):
                    dws[kk] = dws[kk] + _colsum8(dpre * xs[kk])
            dp[pl.ds(ROWB, 8), cs] = dpre_rows(ROWB, 8, jnp.where(last, 0.0, dnext_ref[:, cs]))[0]
            dwv = sum(jnp.where(sub == kk, jnp.sum(dws[kk], axis=0, keepdims=True), 0.0) for kk in range(CONV_K))
            dw_ref[:, cs] += dwv
            db_ref[:, cs] += dbs
            for r0 in range(0, ROWB, CONV_ROWS):
                ahead = _shifted_rows(dp, r0, CONV_ROWS, cs, [CONV_K - 1 - kk for kk in range(CONV_K)])
                dx = sum(w[kk:kk + 1, :] * ahead[kk] for kk in range(CONV_K))
                dx_ref[pl.ds(r0, CONV_ROWS), cs] = dx.astype(BF16)
            return carry

        lax.fori_loop(0, c // LANE, strip, 0)

    nxt = lambda i: (jnp.minimum((i + 1) * t8, lp // 8 - 1), 0)
    prv = lambda i: (jnp.maximum(i * t8 - 1, 0), 0)
    return pl.pallas_call(
        body, name="conv_bwd", grid=(nb,),
        in_specs=[_rowspec(ROWB, c), pl.BlockSpec((8, c), nxt), _rowspec(ROWB, c), pl.BlockSpec((8, c), prv),
                  pl.BlockSpec((8, c), nxt), _fullspec((8, c)), _fullspec((1, c))],
        out_specs=[_rowspec(ROWB, c), _fullspec((8, c)), _fullspec((8, c))],
        out_shape=[_sds((lp, c), BF16), _sds((8, c), F32), _sds((8, c), F32)],
        scratch_shapes=[pltpu.VMEM((ROWB + 16, c), F32), pltpu.VMEM((ROWB + 8, c), F32)],
        compiler_params=_cp(("arbitrary",)),
    )(dact, dact, xbc, xbc, xbc, cw, cb)


def _flash_bwd(q, k, v, datt, lse_row, delta_row, cos, sa, sb):
    lp = q.shape[0]
    nk = lp // ROWB

    def body(k_ref, v_ref, q_ref, do_ref, lse_ref, dl_ref, cos_ref, sa_ref, sb_ref, dq_ref, dk_ref, dv_ref,
             dq_acc, dk_acc, dv_acc):
        j = pl.program_id(1)

        @pl.when(j == 0)
        def _():
            dq_acc[...] = jnp.zeros_like(dq_acc)

        dk_acc[...] = jnp.zeros_like(dk_acc)
        dv_acc[...] = jnp.zeros_like(dv_acc)

        def tile(i, masked, key0=0, nkeys=ROWB):
            keys = pl.ds(key0, nkeys)
            kb, vb = k_ref[keys, :], v_ref[keys, :]
            off = pl.multiple_of(i * ROWB, ROWB)
            qb = q_ref[pl.ds(off, ROWB), :]
            dob = do_ref[pl.ds(off, ROWB), :]
            lse_r = lse_ref[0, :, pl.ds(off, ROWB)]
            dl_r = dl_ref[0, :, pl.ds(off, ROWB)]
            st = _dot(kb, qb, NT)
            if masked:
                krow = j * ROWB + key0 + lax.broadcasted_iota(jnp.int32, st.shape, 0)
                qrow = i * ROWB + lax.broadcasted_iota(jnp.int32, st.shape, 1)
                st = jnp.where(_att_ok(qrow, krow), st, NEG)
            pt = jnp.exp2(st - lse_r)
            dv_acc[keys, :] += _dot(pt.astype(BF16), dob)
            dpt = _dot(vb, dob, NT)
            dst = (pt * (dpt - dl_r)).astype(BF16)
            dk_acc[keys, :] += _dot(dst, qb)
            dq_acc[pl.ds(off, ROWB), :] += _dot(dst, kb, ((0,), (0,)))

        @pl.when(j == 0)
        def _():
            _pair_loop(0, nk, lambda i: tile(i, True, ROWB - META_KEYS, META_KEYS), (4, 2))

        @pl.when((j > 0) & (j < nk - 1))
        def _():
            tile(j, True)
            tile(j + 1, False)
            _pair_loop(j + 2, nk, lambda i: tile(i, False), (4, 2))

        @pl.when(j == nk - 1)
        def _():
            tile(j, True)

        dk_ref[...] = (dk_acc[...] * LN2).astype(BF16)
        dv_ref[...] = dv_acc[...].astype(BF16)
        dq = dq_acc[pl.ds(pl.multiple_of(j * ROWB, ROWB), ROWB), :] * ATT_SCALE
        half = HEADW // 2
        dq_ref[:, pl.ds(0, half)] = dq[:, :half].astype(BF16)
        dq_ref[:, pl.ds(half, half)] = _rope_t(dq[:, half:], cos_ref[...], sa_ref[...], sb_ref[...]).astype(BF16)

    stat = pl.BlockSpec((1, 1, lp), lambda h, j: (h, 0, 0))
    blk = pl.BlockSpec((ROWB, HEADW), lambda h, j: (j, h))
    tab = pl.BlockSpec((ROWB, HEADW // 2), lambda h, j: (j, 0))
    return pl.pallas_call(
        body, name="flash_bwd", grid=(ATT_HEADS, nk),
        in_specs=[blk, pl.BlockSpec((ROWB, V_HEAD), lambda h, j: (j, 2 * h)),
                  pl.BlockSpec((lp, HEADW), lambda h, j: (0, h)), pl.BlockSpec((lp, V_HEAD), lambda h, j: (0, h)),
                  stat, stat, tab, tab, tab],
        out_specs=[blk, blk, pl.BlockSpec((ROWB, V_HEAD), lambda h, j: (j, h))],
        out_shape=[_sds((lp, ATT_HEADS * HEADW), BF16), _sds((lp, ATT_HEADS * HEADW), BF16),
                   _sds((lp, ATT_HEADS * V_HEAD), BF16)],
        scratch_shapes=[pltpu.VMEM((lp, HEADW), F32), pltpu.VMEM((ROWB, HEADW), F32), pltpu.VMEM((ROWB, V_HEAD), F32)],
        compiler_params=_cp(("arbitrary", "arbitrary")),
    )(k, v, q, datt, lse_row, delta_row, cos, sa, sb)


def _qkv_bwd(dqp, dk, dv, cq, ckv, cos, sa, sb, gq, gkv, wq, wk, wv, dz, dxbc, ddtr):
    lp = cq.shape[0]
    qw = ATT_HEADS * HEADW

    def body(dqp_ref, dk_ref, dv_ref, cq_ref, ckv_ref, cos_ref, sa_ref, sb_ref, gq_ref, gkv_ref, wq_ref, wk_ref, wv_ref,
             dz_ref, dxbc_ref, ddtr_ref, dp_ref, dgq_ref, dgkv_ref):
        dcq, dgq = _rms_bwd(cq_ref[...], gq_ref[...], _dot(dqp_ref[...], wq_ref[...], NT))
        dp_ref[:, pl.ds(PC_Q, Q_LORA)] = dcq.astype(BF16)
        dkb = dk_ref[...]
        half = HEADW // 2
        dksum = sum(dkb[:, hh * HEADW + half:(hh + 1) * HEADW].astype(F32) for hh in range(ATT_HEADS))
        dp_ref[:, pl.ds(PC_KR, half)] = jnp.zeros((ROWB, half), BF16)
        dp_ref[:, pl.ds(PC_KR + half, half)] = _rope_t(dksum, cos_ref[...], sa_ref[...], sb_ref[...]).astype(BF16)
        dckvn = _dot(dkb, wk_ref[...], NT) + _dot(dv_ref[...], wv_ref[...], NT)
        dckv, dgkv = _rms_bwd(ckv_ref[...], gkv_ref[...], dckvn)
        dp_ref[:, pl.ds(PC_KV, KV_LORA)] = dckv.astype(BF16)
        dp_ref[:, pl.ds(PC_Z, SSM_WIDTH)] = dz_ref[...]
        dp_ref[:, pl.ds(PC_XBC, CONV_DIM)] = dxbc_ref[...]
        dp_ref[:, pl.ds(PC_DT, LANE)] = ddtr_ref[...].astype(BF16)
        _acc(dgq_ref, _colsum8(dgq))
        _acc(dgkv_ref, _colsum8(dgkv))

    return pl.pallas_call(
        body, name="qkv_bwd", grid=(lp // ROWB,),
        in_specs=[_rowspec(ROWB, qw), _rowspec(ROWB, qw), _rowspec(ROWB, ATT_HEADS * V_HEAD),
                  _rowspec(ROWB, Q_LORA), _rowspec(ROWB, KV_LORA)] + [_rowspec(ROWB, HEADW // 2)] * 3
        + [_fullspec((1, Q_LORA)), _fullspec((1, KV_LORA)), _fullspec((Q_LORA, qw)), _fullspec((KV_LORA, qw)),
           _fullspec((KV_LORA, ATT_HEADS * V_HEAD)), _rowspec(ROWB, SSM_WIDTH), _rowspec(ROWB, CONV_DIM),
           _rowspec(ROWB, LANE)],
        out_specs=[_rowspec(ROWB, PROJ_W), _fullspec((8, Q_LORA)), _fullspec((8, KV_LORA))],
        out_shape=[_sds((lp, PROJ_W), BF16), _sds((8, Q_LORA), F32), _sds((8, KV_LORA), F32)],
        compiler_params=_cp(("arbitrary",)),
    )(dqp, dk, dv, cq, ckv, cos, sa, sb, gq, gkv, wq, wk, wv, dz, dxbc, ddtr)


def _in_bwd(dproj, head, x, dh1, g, w_all, exchange=()):
    lp = dh1.shape[0]
    nsteps = lp // ROWB
    na = len(exchange)

    def body(dp_ref, head_ref, x_ref, dh1_ref, g_ref, w_ref, *rest):
        xin, (dx_ref, dhead_ref, dg_ref), xout, sems = rest[:na], rest[na:na + 3], rest[na + 3:2 * na + 3], rest[2 * na + 3:]
        step = pl.program_id(0)
        if na:
            x_start, x_finish = _exchange_ops(xin, xout, *sems)
            pl.when(step == 0)(x_start)
        dx, dg = _rms_bwd(_h_block(head_ref, x_ref), g_ref[...], _dot(dp_ref[...], w_ref[...], NT))
        dh = dh1_ref[...] + dx

        @pl.when(step == 0)
        def _():
            dhead_ref[...] = dh

        @pl.when(step > 0)
        def _():
            dx_ref[...] = dh

        _acc(dg_ref, _colsum8(dg))
        if na:
            pl.when(step == nsteps - 1)(x_finish)

    dx, dhead, dg8, *received = pl.pallas_call(
        body, name="in_bwd", grid=(nsteps,),
        in_specs=[_rowspec(ROWB, PROJ_W), _fullspec((ROWB, D_MODEL)), _xspec(), _rowspec(ROWB, D_MODEL),
                  _fullspec((1, D_MODEL)), _fullspec((D_MODEL, PROJ_W))] + [ANY] * na,
        out_specs=[_xspec(), _fullspec((ROWB, D_MODEL)), _fullspec((8, D_MODEL))] + [ANY] * na,
        out_shape=[_sds(x.shape, F32), _sds((ROWB, D_MODEL), F32), _sds((8, D_MODEL), F32)]
        + [_sds(e.shape, e.dtype) for e in exchange],
        scratch_shapes=_gather_scratch(na) if na else [],
        compiler_params=_cp(("arbitrary",)),
    )(dproj, head, x, dh1, g, w_all, *exchange)
    return dx, dhead, dg8, received


def _tile_of(n, cap=1024):
    return max(t for t in range(LANE, min(n, cap) + 1, LANE) if n % t == 0)


def _matmul_tn(name, a, b):
    rows, kd = a.shape
    nd = b.shape[1]
    tk, tn = _tile_of(kd), _tile_of(nd)
    rb = 3 * ROWB if rows % (3 * ROWB) == 0 else ROWB

    def body(a_ref, b_ref, o_ref):
        @pl.when(pl.program_id(2) == 0)
        def _():
            o_ref[...] = jnp.zeros_like(o_ref)

        o_ref[...] += _dot(a_ref[...], b_ref[...], ((0,), (0,)))

    return pl.pallas_call(
        body, name=name, grid=(kd // tk, nd // tn, rows // rb),
        in_specs=[pl.BlockSpec((rb, tk), lambda i, j, r: (r, i)), pl.BlockSpec((rb, tn), lambda i, j, r: (r, j))],
        out_specs=pl.BlockSpec((tk, tn), lambda i, j, r: (i, j)), out_shape=_sds((kd, nd), F32),
        compiler_params=_cp(("arbitrary", "arbitrary", "arbitrary")),
    )(a, b)


def _local_backward(head, x, f, exchange_late=False):
    p = f["p"]
    g = {}
    row = lambda v: _row1(v)
    s8 = lambda v: jnp.sum(v, axis=0)
    dh1, du, df, dgpre, dgpost = _mlp_bwd(f["dh2"], f["f"], f["h1"], f["u"], p["w_mlp_up"], p["w_mlp_down"],
                                          row(p["norm_mlp_pre"]), row(p["norm_mlp_post"]))
    g["norm_mlp_pre"], g["norm_mlp_post"] = s8(dgpre), s8(dgpost)
    g["w_mlp_up"] = _matmul_tn("dw_mlp_up", f["hn2"], du)
    g["w_mlp_down"] = _matmul_tn("dw_mlp_down", f["a"], df)
    dmix, datt, dssm, dgmp, delta = _out_bwd(dh1, f["mix"], f["att"], p["w_out"], row(p["norm_mix_post"]))
    g["norm_mix_post"] = s8(dgmp)
    g["w_out"] = jnp.concatenate([_matmul_tn("dw_out_att", f["att"], dmix), _matmul_tn("dw_out_ssm", f["ssm"], dmix)], axis=0)
    dy, dz, dgn, dd = _ssd_post_bwd(dssm, f["y"], f["xact"], f["z"], f["dskip"], row(p["ssm_norm"]))
    g["ssm_norm"] = s8(dgn)
    g["d_skip"] = s8(dd).reshape(SSM_HEADS, SSM_HEAD_DIM).sum(axis=1)
    dact, ddtr, da8, dbias8, received = _ssd_bwd(
        dy, f["xact"], f["dtr"], f["hprev"], f["dt_bias"], f["a_log"], f["dskip"], f["seq_rows"],
        exchange=[_to_chunks(n, g[n]).astype(BF16) for n in LATE] if exchange_late else ())
    g["a_log"], g["dt_bias"] = s8(da8)[:SSM_HEADS], s8(dbias8)[:SSM_HEADS]
    dxbc, dcw8, dcb8 = _conv_bwd(dact, f["xbc"], f["cw"], row(p["conv_b"]))
    g["conv_w"], g["conv_b"] = dcw8[:CONV_K], s8(dcb8)
    dqp, dkb, dv = _flash_bwd(f["q"], f["k"], f["v"], datt, f["lse"], delta, *f["rope"])
    dproj, dgq, dgkv = _qkv_bwd(dqp, dkb, dv, f["cq"], f["ckv"], *f["rope"], row(p["q_a_norm"]), row(p["kv_a_norm"]),
                                f["wq"], f["wk"], f["wv"], dz, dxbc, ddtr)
    g["q_a_norm"], g["kv_a_norm"] = s8(dgq), s8(dgkv)
    dwq = _matmul_tn("dw_q_up", f["cqn"], dqp).reshape(Q_LORA, ATT_HEADS, HEADW)
    g["w_q_up"] = dwq[:, :, :QK_NOPE + QK_ROPE].reshape(Q_LORA, -1)
    dwk = _matmul_tn("dw_k_up", f["ckvn"], dkb).reshape(KV_LORA, ATT_HEADS, HEADW)[:, :, :QK_NOPE]
    dwv = _matmul_tn("dw_v_up", f["ckvn"], dv).reshape(KV_LORA, ATT_HEADS, V_HEAD)
    g["w_kv_up"] = jnp.concatenate([dwk, dwv], axis=2).reshape(KV_LORA, -1)
    dwa = _matmul_tn("dw_in", f["hn"], dproj)
    g["w_in"] = jnp.concatenate([dwa[:, PC_Q:PC_KR], dwa[:, PC_KR + QK_NOPE:PC_KR + QK_NOPE + QK_ROPE],
                                 dwa[:, PC_Z:PC_DT + SSM_HEADS]], axis=1)
    dx, dhead, dgin, received_mid = _in_bwd(
        dproj, head, x, dh1, row(p["norm_mix_pre"]), f["w_all"],
        exchange=[_to_chunks(n, g[n]).astype(BF16) for n in MID] if exchange_late else ())
    g["norm_mix_pre"] = s8(dgin)
    g["meta_tokens"] = dhead[PADF:]
    return dx, g, dict(zip(LATE + MID, list(received) + list(received_mid)))


BIG = {"w_in": ((D_MODEL, IN_WIDTH), 1), "w_q_up": ((Q_LORA, ATT_HEADS * (QK_NOPE + QK_ROPE)), 1),
       "w_kv_up": ((KV_LORA, ATT_HEADS * (QK_NOPE + V_HEAD)), 1), "w_out": ((2 * D_MODEL, D_MODEL), 0),
       "w_mlp_up": ((D_MODEL, D_FF), 1), "w_mlp_down": ((D_FF, D_MODEL), 0), "conv_w": ((CONV_K, CONV_DIM), 1),
       "meta_tokens": ((N_META, D_MODEL), 1)}
SMALL = {"norm_mix_pre": D_MODEL, "q_a_norm": Q_LORA, "kv_a_norm": KV_LORA, "conv_b": CONV_DIM, "dt_bias": SSM_HEADS,
         "a_log": SSM_HEADS, "d_skip": SSM_HEADS, "ssm_norm": SSM_WIDTH, "norm_mix_post": D_MODEL,
         "norm_mlp_pre": D_MODEL, "norm_mlp_post": D_MODEL}
WEIGHT_ORDER = ("meta_tokens", "norm_mix_pre", "w_in", "q_a_norm", "w_q_up", "kv_a_norm", "w_kv_up", "conv_w", "conv_b",
                "dt_bias", "a_log", "d_skip", "ssm_norm", "w_out", "norm_mix_post", "norm_mlp_pre", "w_mlp_up",
                "w_mlp_down", "norm_mlp_post")
ADAM_ROWS = 256


def _shard_shape(name):
    shape, ax = BIG[name]
    return tuple(d // N_DEV if a == ax else d for a, d in enumerate(shape))


SMALL_ROWS = -(-sum(SMALL.values()) // (LANE * 8)) * 8


def _pack(flats, rows):
    v = jnp.concatenate([f.reshape(-1) for f in flats])
    return jnp.pad(v, (0, rows * LANE - v.shape[0])).reshape(rows, LANE)


def _unpack(packed, shapes):
    v = packed.reshape(-1)
    out, o = [], 0
    for s in shapes:
        n = math.prod(s)
        out.append(v[o:o + n].reshape(s))
        o += n
    return out


def _to_chunks(name, full):
    shape, ax = BIG[name]
    if ax == 0:
        return full.reshape((N_DEV,) + _shard_shape(name))
    k, n = shape
    return full.reshape(k, N_DEV, n // N_DEV).transpose(1, 0, 2)


def _from_shards(name, shards):
    shape, ax = BIG[name]
    if ax == 0:
        return shards.reshape(shape)
    return shards.transpose(1, 0, 2).reshape(shape)


def _peer(k):
    x, y, c = lax.axis_index("x"), lax.axis_index("y"), lax.axis_index("c")
    px = 1 - x if k & 4 else x
    py = 1 - y if k & 2 else y
    pc = 1 - c if k & 1 else c
    return (px, py, pc), 4 * px + 2 * py + pc


def _gather_ops(x_refs, out_refs, send_sems, recv_sems, local_sems):
    na = len(x_refs)
    chips = (4, 2, 6)

    def copy(a, n, block, to, src=None):
        return pltpu.make_async_remote_copy(
            src_ref=out_refs[a].at[block] if src is None else src, dst_ref=out_refs[a].at[block],
            send_sem=send_sems.at[7 * a + n], recv_sem=recv_sems.at[7 * a + n], device_id=to, device_id_type=MESH)

    def mine():
        me = _peer(0)[1]
        return [pltpu.make_async_copy(x_refs[a], out_refs[a].at[me], local_sems.at[a]) for a in range(na)]

    def first():
        me, sibling = _peer(0)[1], _peer(1)[0]
        out = [copy(a, 0, me, sibling, src=x_refs[a]) for a in range(na)]
        return out + [copy(a, 1 + n, me, _peer(k)[0], src=x_refs[a]) for n, k in enumerate(chips) for a in range(na)]

    def passed():
        sibling = _peer(1)[0]
        return [copy(a, 4 + n, _peer(k)[1], sibling) for n, k in enumerate(chips) for a in range(na)]

    def start():
        for cp in mine() + first():
            cp.start()

    def forward():
        sibling = _peer(1)[0]
        fwd = passed()
        for n, k in enumerate(chips):
            for a in range(na):
                copy(a, 1 + n, _peer(k)[1], sibling).wait_recv()
                fwd[n * na + a].start()

    def finish():
        sibling = _peer(1)[0]
        for a in range(na):
            copy(a, 0, _peer(1)[1], sibling).wait_recv()
        for n, k in enumerate(chips):
            for a in range(na):
                copy(a, 4 + n, _peer(k | 1)[1], sibling).wait_recv()
        for cp in first() + passed():
            cp.wait_send()
        for cp in mine():
            cp.wait()

    return start, forward, finish


def _gather_scratch(na):
    return [pltpu.SemaphoreType.DMA((7 * na,)), pltpu.SemaphoreType.DMA((7 * na,)), pltpu.SemaphoreType.DMA((na,))]


def _all_gather(shards):
    na = len(shards)

    def body(*refs):
        for step in _gather_ops(refs[:na], refs[na:2 * na], *refs[2 * na:]):
            step()

    return pl.pallas_call(
        body, name="all_gather_weights", out_shape=[_sds((N_DEV,) + s.shape, s.dtype) for s in shards],
        in_specs=[ANY] * na, out_specs=[ANY] * na, scratch_shapes=_gather_scratch(na),
    )(*shards)


def _exchange(chunks, small):
    na = len(chunks) + 1

    def body(*refs):
        for step in _exchange_ops(refs[:na], refs[na:2 * na], *refs[2 * na:], whole=(na - 1,)):
            step()

    arrays = list(chunks) + [small]
    return pl.pallas_call(
        body, name="exchange_grads",
        out_shape=[_sds(c.shape, c.dtype) for c in chunks] + [_sds((N_DEV,) + small.shape, small.dtype)],
        in_specs=[ANY] * na, out_specs=[ANY] * na, scratch_shapes=_gather_scratch(na),
    )(*arrays)


def _exchange_ops(in_refs, out_refs, send_sems, recv_sems, local_sems, whole=()):
    na = len(in_refs)

    def src(a, idx):
        return in_refs[a] if a in whole else in_refs[a].at[idx]

    def own():
        me = _peer(0)[1]
        return [pltpu.make_async_copy(src(a, me), out_refs[a].at[me], local_sems.at[a]) for a in range(na)]

    def copy(a, k, sending):
        me = _peer(0)[1]
        to, idx = _peer(k)
        return pltpu.make_async_remote_copy(
            src_ref=src(a, idx if sending else me), dst_ref=out_refs[a].at[me if sending else idx],
            send_sem=send_sems.at[7 * a + k - 1], recv_sem=recv_sems.at[7 * a + k - 1],
            device_id=to, device_id_type=MESH)

    def sent():
        return [copy(a, k, True) for k in range(1, N_DEV) for a in range(na)]

    def start():
        for cp in own() + sent():
            cp.start()

    def finish():
        for k in range(1, N_DEV):
            for a in range(na):
                copy(a, k, False).wait_recv()
        for cp in sent():
            cp.wait_send()
        for cp in own():
            cp.wait()

    return start, finish


def _reduce_adamw(name, recv, w, m, v):
    rows, cols = w.shape
    blk = ADAM_ROWS if rows % ADAM_ROWS == 0 else rows
    c1 = 1.0 - ADAM_B1 ** ADAM_STEP
    c2 = 1.0 - ADAM_B2 ** ADAM_STEP

    def body(r_ref, w_ref, m_ref, v_ref, g_ref, d_ref, nm_ref, nv_ref):
        g = r_ref[0].astype(F32)
        for s in range(1, N_DEV):
            g = g + r_ref[s].astype(F32)
        g_ref[...] = g
        m_ = ADAM_B1 * m_ref[...] + (1.0 - ADAM_B1) * g
        v_ = ADAM_B2 * v_ref[...] + (1.0 - ADAM_B2) * (g * g)
        nm_ref[...] = m_
        nv_ref[...] = v_
        d_ref[...] = -ADAM_LR * ((m_ / c1) / (jnp.sqrt(v_ / c2) + ADAM_EPS) + ADAM_WD * w_ref[...])

    spec = _rowspec(blk, cols)
    return pl.pallas_call(
        body, name="reduce_adamw_" + name, grid=(rows // blk,),
        in_specs=[pl.BlockSpec((N_DEV, blk, cols), lambda i: (0, i, 0)), spec, spec, spec],
        out_specs=[spec] * 4, out_shape=[_sds((rows, cols), F32)] * 4,
        compiler_params=_cp(("arbitrary",)),
    )(recv, w, m, v)


def kernel(x, meta_tokens, norm_mix_pre, w_in, q_a_norm, w_q_up, kv_a_norm, w_kv_up, conv_w, conv_b, dt_bias, a_log, d_skip, ssm_norm, w_out, norm_mix_post, norm_mlp_pre, w_mlp_up, w_mlp_down, norm_mlp_post, loss_target, m_meta_tokens, m_norm_mix_pre, m_w_in, m_q_a_norm, m_w_q_up, m_kv_a_norm, m_w_kv_up, m_conv_w, m_conv_b, m_dt_bias, m_a_log, m_d_skip, m_ssm_norm, m_w_out, m_norm_mix_post, m_norm_mlp_pre, m_w_mlp_up, m_w_mlp_down, m_norm_mlp_post, v_meta_tokens, v_norm_mix_pre, v_w_in, v_q_a_norm, v_w_q_up, v_kv_a_norm, v_w_kv_up, v_conv_w, v_conv_b, v_dt_bias, v_a_log, v_d_skip, v_ssm_norm, v_w_out, v_norm_mix_post, v_norm_mlp_pre, v_w_mlp_up, v_w_mlp_down, v_norm_mlp_post):
    w = dict(meta_tokens=meta_tokens, norm_mix_pre=norm_mix_pre, w_in=w_in, q_a_norm=q_a_norm, w_q_up=w_q_up,
             kv_a_norm=kv_a_norm, w_kv_up=w_kv_up, conv_w=conv_w, conv_b=conv_b, dt_bias=dt_bias, a_log=a_log,
             d_skip=d_skip, ssm_norm=ssm_norm, w_out=w_out, norm_mix_post=norm_mix_post, norm_mlp_pre=norm_mlp_pre,
             w_mlp_up=w_mlp_up, w_mlp_down=w_mlp_down, norm_mlp_post=norm_mlp_post)
    m = dict(meta_tokens=m_meta_tokens, norm_mix_pre=m_norm_mix_pre, w_in=m_w_in, q_a_norm=m_q_a_norm, w_q_up=m_w_q_up,
             kv_a_norm=m_kv_a_norm, w_kv_up=m_w_kv_up, conv_w=m_conv_w, conv_b=m_conv_b, dt_bias=m_dt_bias,
             a_log=m_a_log, d_skip=m_d_skip, ssm_norm=m_ssm_norm, w_out=m_w_out, norm_mix_post=m_norm_mix_post,
             norm_mlp_pre=m_norm_mlp_pre, w_mlp_up=m_w_mlp_up, w_mlp_down=m_w_mlp_down, norm_mlp_post=m_norm_mlp_post)
    v = dict(meta_tokens=v_meta_tokens, norm_mix_pre=v_norm_mix_pre, w_in=v_w_in, q_a_norm=v_q_a_norm, w_q_up=v_w_q_up,
             kv_a_norm=v_kv_a_norm, w_kv_up=v_w_kv_up, conv_w=v_conv_w, conv_b=v_conv_b, dt_bias=v_dt_bias,
             a_log=v_a_log, d_skip=v_d_skip, ssm_norm=v_ssm_norm, w_out=v_w_out, norm_mix_post=v_norm_mix_post,
             norm_mlp_pre=v_norm_mlp_pre, w_mlp_up=v_w_mlp_up, w_mlp_down=v_w_mlp_down, norm_mlp_post=v_norm_mlp_post)
    big_names = [n for n in WEIGHT_ORDER if n in BIG]
    small_names = [n for n in WEIGHT_ORDER if n in SMALL]
    shard = lambda d, n: d[n].reshape(_shard_shape(n))

    f32_names = ("conv_w", "meta_tokens")
    early = [n for n in big_names if n not in LATE]
    gathered = _all_gather([shard(w, n).astype(F32 if n in f32_names else BF16) for n in early])
    p = {n: w[n].reshape(-1) for n in small_names}
    p.update({n: _from_shards(n, s) for n, s in zip(early, gathered)})
    head = jnp.concatenate([jnp.zeros((PADF, D_MODEL), F32), p["meta_tokens"]], axis=0)
    f = _local_forward(head, x[0], loss_target[0], p, late={n: shard(w, n).astype(BF16) for n in LATE})
    dx, g, recv_of = _local_backward(head, x[0], f, exchange_late=True)
    grad_x = dx[None]
    loss = lax.psum(f["loss"], ("x", "y", "c"))

    rest = [n for n in big_names if n not in LATE + MID]
    small = _pack([g[n] for n in small_names], SMALL_ROWS)
    *recv_rest, recv_small = _exchange([_to_chunks(n, g[n]) for n in rest], small)
    recv_of.update(zip(rest, recv_rest))

    outs = {}
    kinds = ("grad", "delta", "new_m", "new_v")
    for n, recv in ((n, recv_of[n]) for n in big_names):
        for kind, arr in zip(kinds, _reduce_adamw(n, recv, shard(w, n), shard(m, n), shard(v, n))):
            outs[kind, n] = arr.reshape(w[n].shape)
    packed = [_pack([d[n] for n in small_names], SMALL_ROWS) for d in (w, m, v)]
    for kind, arr in zip(kinds, _reduce_adamw("small", recv_small, *packed)):
        for n, val in zip(small_names, _unpack(arr, [(SMALL[n],) for n in small_names])):
            outs[kind, n] = val.reshape(w[n].shape)
    return (loss, grad_x) + tuple(outs[kind, n] for kind in ("grad", "delta", "new_m", "new_v") for n in WEIGHT_ORDER)
```

```python
import math

import jax
import jax.numpy as jnp
import numpy as np
from jax import lax
from jax.experimental import pallas as pl
from jax.experimental.pallas import tpu as pltpu

F32 = jnp.float32
BF16 = jnp.bfloat16

D_MODEL = 1024
N_META = 16
EPS = 1e-6
ATT_HEADS = 8
Q_LORA = 384
KV_LORA = 256
QK_NOPE = 128
QK_ROPE = 64
V_HEAD = 128
ROPE_THETA = 10000.0
SSM_HEADS = 16
SSM_HEAD_DIM = 64
SSM_WIDTH = 1024
SSM_GROUPS = 2
SSM_STATE = 128
CONV_K = 4
CHUNK = 128
CONV_DIM = 1536
D_FF = 4096
IN_SPLITS = (Q_LORA, KV_LORA, QK_ROPE, SSM_WIDTH, CONV_DIM, SSM_HEADS)
IN_WIDTH = sum(IN_SPLITS)
ADAM_LR, ADAM_B1, ADAM_B2, ADAM_EPS, ADAM_WD, ADAM_STEP = 0.001, 0.9, 0.999, 1e-08, 0.01, 10

LANE = 128
ROWB = 512
PADF = ROWB - N_META
HEADW = 256
PC_Q, PC_KV, PC_KR, PC_Z, PC_XBC, PC_DT, PROJ_W = 0, 384, 640, 896, 1920, 3456, 3584
NEG = -1e30
N_DEV = 8
VMEM_LIMIT = 56 * 1024 * 1024
MESH = pl.DeviceIdType.MESH


def _cp(sem, vmem=VMEM_LIMIT, **kw):
    return pltpu.CompilerParams(dimension_semantics=sem, vmem_limit_bytes=vmem, **kw)


def _dot(a, b, dims=((1,), (0,))):
    return lax.dot_general(a, b, (dims, ((), ())), preferred_element_type=F32)


def _bdot(a, b, dims=((1,), (0,))):
    return _dot(a.astype(BF16), b.astype(BF16), dims)


NT = ((1,), (1,))


def _rms_fwd(x, w):
    r = lax.rsqrt(jnp.mean(x * x, axis=-1, keepdims=True) + EPS)
    return (x * r) * w


def _rms_bwd(x, w, dy):
    r = lax.rsqrt(jnp.mean(x * x, axis=-1, keepdims=True) + EPS)
    xh = x * r
    g = dy * w
    dx = r * (g - xh * jnp.mean(g * xh, axis=-1, keepdims=True))
    return dx, dy * xh


def _sigmoid(x):
    return 0.5 * jnp.tanh(0.5 * x) + 0.5


def _colsum8(x):
    t, c = x.shape
    return jnp.sum(x.reshape(t // 8, 8, c), axis=0)


def _rowspec(t, c, cb=0):
    return pl.BlockSpec((t, c), lambda i: (i, cb))


def _fullspec(shape):
    n = len(shape)
    return pl.BlockSpec(shape, lambda i: (0,) * n)


def _sds(shape, dt):
    return jax.ShapeDtypeStruct(shape, dt)


def _acc(ref, val):
    @pl.when(pl.program_id(0) == 0)
    def _():
        ref[...] = jnp.zeros_like(ref)

    ref[...] += val


def _xspec():
    return pl.BlockSpec((ROWB, D_MODEL), lambda i: (jnp.maximum(i - 1, 0), 0))


def _h_block(head_ref, x_ref):
    return jnp.where(pl.program_id(0) == 0, head_ref[...], x_ref[...])


def _norm_in_proj(head, x, g, w_all):
    lp = head.shape[0] + x.shape[0]

    def body(head_ref, x_ref, g_ref, w_ref, hn_ref, cq_ref, ckv_ref, kr_ref, z_ref, xbc_ref, dt_ref):
        hn = _rms_fwd(_h_block(head_ref, x_ref), g_ref[...]).astype(BF16)
        hn_ref[...] = hn
        p = _dot(hn, w_ref[...])
        cq_ref[...] = p[:, PC_Q:PC_KV]
        ckv_ref[...] = p[:, PC_KV:PC_KR]
        kr_ref[...] = p[:, PC_KR:PC_Z]
        z_ref[...] = p[:, PC_Z:PC_XBC]
        xbc_ref[...] = p[:, PC_XBC:PC_DT]
        dt_ref[...] = p[:, PC_DT:PROJ_W]

    widths = (Q_LORA, KV_LORA, HEADW, SSM_WIDTH, CONV_DIM, LANE)
    return pl.pallas_call(
        body, name="norm_in_proj", grid=(lp // ROWB,),
        in_specs=[_fullspec((ROWB, D_MODEL)), _xspec(), _fullspec((1, D_MODEL)), _fullspec((D_MODEL, PROJ_W))],
        out_specs=[_rowspec(ROWB, D_MODEL)] + [_rowspec(ROWB, w) for w in widths],
        out_shape=[_sds((lp, D_MODEL), BF16)] + [_sds((lp, w), F32) for w in widths],
        compiler_params=_cp(("arbitrary",)),
    )(head, x, g, w_all)


def _rope(x, cos, sa, sb):
    w = x.shape[1]
    return x * cos + pltpu.roll(x, w - 32, 1) * sa + pltpu.roll(x, 32, 1) * sb


def _rope_t(dy, cos, sa, sb):
    w = dy.shape[1]
    return dy * cos + pltpu.roll(dy * sa, 32, 1) + pltpu.roll(dy * sb, w - 32, 1)


def _qkv(cq, ckv, kr, cos, sa, sb, gq, gkv, wq, wk, wv):
    lp = cq.shape[0]
    qw = ATT_HEADS * HEADW
    half = HEADW // 2
    assert half == QK_NOPE == V_HEAD == LANE

    def body(cq_ref, ckv_ref, kr_ref, cos_ref, sa_ref, sb_ref, gq_ref, gkv_ref, wq_ref, wk_ref, wv_ref,
             q_ref, k_ref, v_ref, cqn_ref, ckvn_ref):
        tabs = [cos_ref[...], sa_ref[...], sb_ref[...]]
        cqn = _rms_fwd(cq_ref[...], gq_ref[...]).astype(BF16)
        ckvn = _rms_fwd(ckv_ref[...], gkv_ref[...]).astype(BF16)
        cqn_ref[...] = cqn
        ckvn_ref[...] = ckvn
        q = _dot(cqn, wq_ref[...])
        kn = _dot(ckvn, wk_ref[...])
        vv = _dot(ckvn, wv_ref[...])
        krope = _rope(kr_ref[:, pl.ds(half, half)], *tabs).astype(BF16)
        ones = jnp.ones((ROWB, half), BF16)
        for hh in range(ATT_HEADS):
            lo, hi, src = pl.ds(hh * HEADW, half), pl.ds(hh * HEADW + half, half), slice(hh * half, (hh + 1) * half)
            q_ref[:, lo] = (q[:, hh * HEADW:hh * HEADW + half] * Q_PRESCALE).astype(BF16)
            q_ref[:, hi] = (_rope(q[:, hh * HEADW + half:(hh + 1) * HEADW], *tabs) * Q_PRESCALE).astype(BF16)
            k_ref[:, lo] = kn[:, src].astype(BF16)
            k_ref[:, hi] = krope
            v_ref[:, lo] = vv[:, src].astype(BF16)
            v_ref[:, hi] = ones

    return pl.pallas_call(
        body, name="qkv", grid=(lp // ROWB,),
        in_specs=[_rowspec(ROWB, Q_LORA), _rowspec(ROWB, KV_LORA), _rowspec(ROWB, HEADW)]
        + [_rowspec(ROWB, half)] * 3
        + [_fullspec((1, Q_LORA)), _fullspec((1, KV_LORA)), _fullspec((Q_LORA, qw)),
           _fullspec((KV_LORA, ATT_HEADS * QK_NOPE)), _fullspec((KV_LORA, ATT_HEADS * V_HEAD))],
        out_specs=[_rowspec(ROWB, qw), _rowspec(ROWB, qw), _rowspec(ROWB, qw),
                   _rowspec(ROWB, Q_LORA), _rowspec(ROWB, KV_LORA)],
        out_shape=[_sds((lp, qw), BF16), _sds((lp, qw), BF16), _sds((lp, qw), BF16),
                   _sds((lp, Q_LORA), BF16), _sds((lp, KV_LORA), BF16)],
        compiler_params=_cp(("arbitrary",)),
    )(cq, ckv, kr, cos, sa, sb, gq, gkv, wq, wk, wv)


ATT_SCALE = (QK_NOPE + QK_ROPE) ** -0.5
LOG2E = 1.4426950408889634
LN2 = 0.6931471805599453
Q_PRESCALE = ATT_SCALE * LOG2E
KVB = 512
META_KEYS = LANE
assert N_META <= META_KEYS


def _att_ok(qrow, krow):
    return (krow <= qrow) & ((krow >= PADF) | (qrow < PADF))


def _lanes(x, n):
    return x if n == 1 else jnp.concatenate([x] * n, axis=1)


def _pair_loop(lo, hi, tile, unrolls=(2,)):
    for u in tuple(unrolls) + (1,):
        n = jnp.maximum(hi - lo, 0)
        trips = n // u

        def many(t, c, u=u, lo=lo):
            for d in range(u):
                tile(lo + u * t + d)
            return c

        lax.fori_loop(0, trips, many, 0)
        lo = lo + trips * u


def _flash_fwd(q, k, v):
    lp = q.shape[0]
    nq = lp // ROWB

    def body(q_ref, k_ref, v_ref, o_ref, lse_ref, acc, m_s):
        i = pl.program_id(1)
        qb = q_ref[...]
        m_s[...] = jnp.full_like(m_s, NEG)
        acc[...] = jnp.zeros_like(acc)

        def tile(j, masked, off=None, nkeys=KVB):
            off = pl.multiple_of(j * KVB, KVB) if off is None else off
            kb = k_ref[pl.ds(off, nkeys), :]
            vb = v_ref[pl.ds(off, nkeys), :]
            s = _dot(qb, kb, NT)
            if masked:
                qrow = i * ROWB + lax.broadcasted_iota(jnp.int32, s.shape, 0)
                krow = off + lax.broadcasted_iota(jnp.int32, s.shape, 1)
                s = jnp.where(_att_ok(qrow, krow), s, NEG)
            m_prev = m_s[...]
            m_new = jnp.maximum(m_prev, jnp.max(s, axis=1, keepdims=True))
            alpha = jnp.exp2(m_prev - m_new)
            p = jnp.exp2(s - _lanes(m_new, nkeys // LANE))
            acc[...] = _lanes(alpha, 2) * acc[...] + _dot(p.astype(BF16), vb)
            m_s[...] = m_new

        def first_tile():
            tile(0, True, off=ROWB - META_KEYS, nkeys=META_KEYS)

        @pl.when(i == 0)
        def _():
            first_tile()

        odd = jnp.bitwise_and(jnp.maximum(i - 1, 0), 3)
        for r in range(4):
            @pl.when((i > 0) & (odd == r))
            def _(r=r):
                first_tile()
                tile(i, True)
                for d in range(r):
                    tile(1 + d, False)

        _pair_loop(1 + odd, i, lambda j: tile(j, False), (16, 8, 4))
        l = acc[:, V_HEAD:]
        o_ref[...] = (acc[:, :V_HEAD] / l).astype(BF16)
        lse_ref[0] = (m_s[...] + jnp.log2(l)).T[0:1, :]

    return pl.pallas_call(
        body, name="flash_fwd", grid=(ATT_HEADS, nq),
        in_specs=[pl.BlockSpec((ROWB, HEADW), lambda h, i: (i, h)),
                  pl.BlockSpec((lp, HEADW), lambda h, i: (0, h)),
                  pl.BlockSpec((lp, HEADW), lambda h, i: (0, h))],
        out_specs=[pl.BlockSpec((ROWB, V_HEAD), lambda h, i: (i, h)),
                   pl.BlockSpec((1, 1, ROWB), lambda h, i: (h, 0, i))],
        out_shape=[_sds((lp, ATT_HEADS * V_HEAD), BF16), _sds((ATT_HEADS, 1, lp), F32)],
        scratch_shapes=[pltpu.VMEM((ROWB, HEADW), F32), pltpu.VMEM((ROWB, LANE), F32)],
        compiler_params=_cp(("arbitrary", "arbitrary")),
    )(q, k, v)


def _silu(x):
    return x * _sigmoid(x)


CONV_ROWS = 64


def _shifted_rows(ref, start, n, cols, offsets):
    win = ref[pl.ds(start, n + 8), cols]
    return [win[o:o + n] if o % 8 == 0 else pltpu.roll(win, n + 8 - o, 0)[0:n] for o in offsets]


def _conv_fwd(xbc, cw, cb):
    lp, c = xbc.shape
    t8 = ROWB // 8

    def body(x_ref, prev_ref, w_ref, b_ref, o_ref, buf):
        i = pl.program_id(0)
        buf[pl.ds(0, 8), :] = jnp.where(i > 0, prev_ref[...], 0.0)
        buf[pl.ds(8, ROWB), :] = x_ref[...]

        def strip(s, carry):
            cs = pl.ds(pl.multiple_of(s * LANE, LANE), LANE)
            w, b = w_ref[:, cs], b_ref[:, cs]
            for r0 in range(0, ROWB, CONV_ROWS):
                taps = _shifted_rows(buf, r0, CONV_ROWS, cs, [8 - (CONV_K - 1) + kk for kk in range(CONV_K)])
                pre = b + sum(w[kk:kk + 1, :] * taps[kk] for kk in range(CONV_K))
                o_ref[pl.ds(r0, CONV_ROWS), cs] = _silu(pre)
            return carry

        lax.fori_loop(0, c // LANE, strip, 0)

    return pl.pallas_call(
        body, name="conv_fwd", grid=(lp // ROWB,),
        in_specs=[_rowspec(ROWB, c), pl.BlockSpec((8, c), lambda i: (jnp.maximum(i * t8 - 1, 0), 0)),
                  _fullspec((8, c)), _fullspec((1, c))],
        out_specs=_rowspec(ROWB, c), out_shape=_sds((lp, c), F32),
        scratch_shapes=[pltpu.VMEM((ROWB + 8, c), F32)],
        compiler_params=_cp(("arbitrary",)),
    )(xbc, xbc, cw, cb)


def _expand_mat():
    r = np.arange(LANE)[:, None]
    c = np.arange(SSM_WIDTH)[None, :]
    return jnp.asarray((c // SSM_HEAD_DIM == r).astype(np.float32))


def _tri_mat():
    i = np.arange(CHUNK)
    return jnp.asarray((i[:, None] >= i[None, :]).astype(np.float32))


def _x3(m, axis):
    return jnp.concatenate([m.astype(BF16)] * 3, axis=axis)


def _split3(x):
    hi = x.astype(BF16)
    r = x - hi.astype(F32)
    mid = r.astype(BF16)
    return hi, mid, (r - mid.astype(F32)).astype(BF16)


def _dot01_r(x, m3):
    return _dot(jnp.concatenate(_split3(x), axis=1), m3)


def _dot01_l(m3, x):
    return _dot(m3, jnp.concatenate(_split3(x), axis=0))


def _ssd_prep(dt_raw, bias_ref, alog_ref, tri3, c, seq_rows):
    rows = c * CHUNK + lax.broadcasted_iota(jnp.int32, (CHUNK, LANE), 0)
    lanes = lax.broadcasted_iota(jnp.int32, (CHUNK, LANE), 1)
    valid = (rows >= PADF) & (rows < PADF + seq_rows) & (lanes < SSM_HEADS)
    dtr = dt_raw + bias_ref[...]
    sp = jnp.maximum(dtr, 0.0) + jnp.log(1.0 + jnp.exp(-jnp.abs(dtr)))
    dt = jnp.where(valid, sp, 0.0)
    a = -jnp.exp(alog_ref[...])
    acol = _dot01_l(tri3, dt * a)
    return dt, a, acol, valid, dtr


def _row16(v):
    return jnp.broadcast_to(v, (16, v.shape[1]))


def _ssd_fwd(xbc_act, dtr, dt_bias, a_log, seq_rows, gather=()):
    lp = xbc_act.shape[0]
    nc = lp // CHUNK
    cps = ROWB // CHUNK
    nsteps = nc // cps
    gw = SSM_WIDTH // SSM_GROUPS
    hpg = SSM_HEADS // SSM_GROUPS

    na = len(gather)

    def body(x_ref, b_ref, c_ref, dtr_ref, bias_ref, alog_ref, tri_ref, tri3_ref, ex3_ref, *rest):
        gin, (y_ref, hp_ref), gout, h_s, sems = rest[:na], rest[na:na + 2], rest[na + 2:2 * na + 2], rest[2 * na + 2], rest[2 * na + 3:]
        step = pl.program_id(0)

        @pl.when(step == 0)
        def _():
            h_s[...] = jnp.zeros_like(h_s)

        if na:
            g_start, g_forward, g_finish = _gather_ops(gin, gout, *sems)
            pl.when(step == 0)(g_start)
            pl.when(step == nsteps // 2)(g_forward)

        ex3 = ex3_ref[...]
        causal = tri_ref[...] > 0.5
        for cc in range(cps):
            rows = pl.ds(cc * CHUNK, CHUNK)
            dt, a, acol, _, _ = _ssd_prep(dtr_ref[rows, :], bias_ref, alog_ref, tri3_ref[...], step * cps + cc, seq_rows)
            arow = acol.T
            dtrow = dt.T
            alast = acol[CHUNK - 1:CHUNK, :]
            e_all = _dot01_r(jnp.exp(acol), ex3)
            wx_all = _dot01_r(jnp.exp(alast - acol) * dt, ex3)
            dec_all = _dot01_r(_row16(jnp.exp(alast)), ex3)[0:1, :]
            hp_ref[cc] = h_s[...]
            for g in range(SSM_GROUPS):
                gs = slice(g * gw, (g + 1) * gw)
                bg = b_ref[rows, g * SSM_STATE:(g + 1) * SSM_STATE]
                cg = c_ref[rows, g * SSM_STATE:(g + 1) * SSM_STATE].astype(BF16)
                xg = x_ref[rows, gs]
                hg = h_s[:, gs]
                gm = _bdot(cg, bg, NT)
                y_off = _bdot(cg, hg) * e_all[:, gs]
                for r in range(hpg):
                    hd = g * hpg + r
                    seg = acol[:, hd:hd + 1] - arow[hd:hd + 1, :]
                    lm = jnp.where(causal, jnp.exp(jnp.where(causal, seg, 0.0)), 0.0)
                    w = gm * lm * dtrow[hd:hd + 1, :]
                    cs = slice(r * SSM_HEAD_DIM, (r + 1) * SSM_HEAD_DIM)
                    y_ref[rows, pl.ds(hd * SSM_HEAD_DIM, SSM_HEAD_DIM)] = _bdot(w, xg[:, cs]) + y_off[:, cs]
                st = _bdot(bg.T, xg * wx_all[:, gs])
                h_s[:, gs] = hg * dec_all[:, gs] + st

        if na:
            pl.when(step == nsteps - 1)(g_finish)

    xs_spec = pl.BlockSpec((ROWB, SSM_WIDTH), lambda c: (c, 0))
    b_spec = pl.BlockSpec((ROWB, 2 * SSM_STATE), lambda c: (c, SSM_WIDTH // (2 * SSM_STATE)))
    c_spec = pl.BlockSpec((ROWB, 2 * SSM_STATE), lambda c: (c, SSM_WIDTH // (2 * SSM_STATE) + 1))
    y, hprev, *gathered = pl.pallas_call(
        body, name="ssd_fwd", grid=(nsteps,),
        in_specs=[xs_spec, b_spec, c_spec, pl.BlockSpec((ROWB, LANE), lambda c: (c, 0)),
                  _fullspec((1, LANE)), _fullspec((1, LANE)), _fullspec((CHUNK, CHUNK)), _fullspec((CHUNK, 3 * CHUNK)),
                  _fullspec((3 * LANE, SSM_WIDTH))] + [ANY] * na,
        out_specs=[xs_spec, pl.BlockSpec((cps, SSM_STATE, SSM_WIDTH), lambda c: (c, 0, 0))] + [ANY] * na,
        out_shape=[_sds((lp, SSM_WIDTH), F32), _sds((nc, SSM_STATE, SSM_WIDTH), F32)]
        + [_sds((N_DEV,) + s.shape, s.dtype) for s in gather],
        scratch_shapes=[pltpu.VMEM((SSM_STATE, SSM_WIDTH), F32)] + (_gather_scratch(na) if na else []),
        compiler_params=_cp(("arbitrary",)),
    )(xbc_act, xbc_act, xbc_act, dtr, dt_bias, a_log, _tri_mat(), _x3(_tri_mat(), 1), _x3(_expand_mat(), 0), *gather)
    return y, hprev, gathered


def _group_mean(x):
    gw = SSM_WIDTH // SSM_GROUPS
    parts = [jnp.broadcast_to(jnp.mean(x[:, g * gw:(g + 1) * gw], axis=-1, keepdims=True), (x.shape[0], gw))
             for g in range(SSM_GROUPS)]
    return jnp.concatenate(parts, axis=1)


def _ssd_post(y, xbc_act, z, dskip, gnorm):
    lp = y.shape[0]

    def body(y_ref, x_ref, z_ref, d_ref, g_ref, o_ref):
        z_ = z_ref[...]
        gt = (y_ref[...] + d_ref[...] * x_ref[...]) * _silu(z_)
        r = lax.rsqrt(_group_mean(gt * gt) + EPS)
        o_ref[...] = ((gt * r) * g_ref[...]).astype(BF16)

    return pl.pallas_call(
        body, name="ssd_post", grid=(lp // ROWB,),
        in_specs=[_rowspec(ROWB, SSM_WIDTH)] * 3 + [_fullspec((1, SSM_WIDTH))] * 2,
        out_specs=_rowspec(ROWB, SSM_WIDTH), out_shape=_sds((lp, SSM_WIDTH), BF16),
        compiler_params=_cp(("arbitrary",)),
    )(y, xbc_act, z, dskip, gnorm)


def _out_proj(att, ssm, head, x, w_out, g_post):
    lp = att.shape[0]

    def body(a_ref, s_ref, head_ref, x_ref, w_ref, g_ref, mix_ref, h1_ref):
        mix = _dot(a_ref[...], w_ref[pl.ds(0, 1024), :]) + _dot(s_ref[...], w_ref[pl.ds(1024, 1024), :])
        mix_ref[...] = mix
        h1_ref[...] = _h_block(head_ref, x_ref) + _rms_fwd(mix, g_ref[...])

    return pl.pallas_call(
        body, name="out_proj", grid=(lp // ROWB,),
        in_specs=[_rowspec(ROWB, 1024)] * 2 + [_fullspec((ROWB, D_MODEL)), _xspec(), _fullspec((2048, D_MODEL)),
                                               _fullspec((1, D_MODEL))],
        out_specs=[_rowspec(ROWB, D_MODEL)] * 2, out_shape=[_sds((lp, D_MODEL), F32)] * 2,
        compiler_params=_cp(("arbitrary",)),
    )(att, ssm, head, x, w_out, g_post)


def _resident(w_hbm, w_vmem, sem):
    @pl.when(pl.program_id(0) == 0)
    def _():
        cp = pltpu.make_async_copy(w_hbm, w_vmem, sem)
        cp.start()
        cp.wait()


ANY = pl.BlockSpec(memory_space=pl.ANY)


def _mlp_fwd(h1, tgt, w_up, w_down, g_pre, g_post, seq_rows):
    lp = h1.shape[0]

    def body(h1_ref, t_ref, wu_hbm, wd_hbm, gpre_ref, gpost_ref, hn2_ref, u_ref, a_ref, f_ref, dh2_ref, loss_ref,
             wu, wd, sems):
        _resident(wu_hbm, wu, sems.at[0])
        _resident(wd_hbm, wd, sems.at[1])
        i = pl.program_id(0)
        h1_ = h1_ref[...]
        hn2 = _rms_fwd(h1_, gpre_ref[...]).astype(BF16)
        hn2_ref[...] = hn2
        u = jnp.maximum(_dot(hn2, wu[...]), 0.0)
        u_ref[...] = u.astype(BF16)
        a = (u * u).astype(BF16)
        a_ref[...] = a
        f = _dot(a, wd[...])
        f_ref[...] = f
        h2 = h1_ + _rms_fwd(f, gpost_ref[...])
        rows = i * ROWB + lax.broadcasted_iota(jnp.int32, (ROWB, 1), 0)
        real = (rows >= PADF + N_META) & (rows < PADF + seq_rows)
        err = jnp.where(real, h2 - t_ref[...], 0.0)
        dh2_ref[...] = err * (1.0 / D_MODEL)
        _acc(loss_ref, _colsum8(err * err))

    return pl.pallas_call(
        body, name="mlp_fwd", grid=(lp // ROWB,),
        in_specs=[_rowspec(ROWB, D_MODEL), _xspec()] + [ANY, ANY] + [_fullspec((1, D_MODEL))] * 2,
        out_specs=[_rowspec(ROWB, D_MODEL), _rowspec(ROWB, D_FF), _rowspec(ROWB, D_FF)] + [_rowspec(ROWB, D_MODEL)] * 2
        + [_fullspec((8, D_MODEL))],
        out_shape=[_sds((lp, D_MODEL), BF16), _sds((lp, D_FF), BF16), _sds((lp, D_FF), BF16), _sds((lp, D_MODEL), F32),
                   _sds((lp, D_MODEL), F32), _sds((8, D_MODEL), F32)],
        scratch_shapes=[pltpu.VMEM((D_MODEL, D_FF), BF16), pltpu.VMEM((D_FF, D_MODEL), BF16), pltpu.SemaphoreType.DMA((2,))],
        compiler_params=_cp(("arbitrary",)),
    )(h1, tgt, w_up, w_down, g_pre, g_post)


def _pad_cols(w, width):
    return jnp.pad(w, ((0, 0), (0, width - w.shape[1])))


def _layout_weights(w_in, w_q_up, w_kv_up):
    o = np.cumsum((0,) + IN_SPLITS)
    pieces = [w_in[:, o[k]:o[k + 1]] for k in range(6)]
    kr = jnp.pad(pieces[2], ((0, 0), (QK_NOPE, HEADW - QK_NOPE - QK_ROPE)))
    w_all = jnp.concatenate([pieces[0], pieces[1], kr, pieces[3], pieces[4], _pad_cols(pieces[5], LANE)], axis=1)
    wq = jnp.pad(w_q_up.reshape(Q_LORA, ATT_HEADS, QK_NOPE + QK_ROPE), ((0, 0), (0, 0), (0, HEADW - QK_NOPE - QK_ROPE)))
    wkv = w_kv_up.reshape(KV_LORA, ATT_HEADS, QK_NOPE + V_HEAD)
    wk = jnp.pad(wkv[:, :, :QK_NOPE], ((0, 0), (0, 0), (0, HEADW - QK_NOPE)))
    wv = wkv[:, :, QK_NOPE:]
    return (w_all, wq.reshape(Q_LORA, -1), wk.reshape(KV_LORA, -1), wv.reshape(KV_LORA, -1),
            wkv[:, :, :QK_NOPE].reshape(KV_LORA, -1))


def _rope_tables(lp):
    pos = jnp.maximum(jnp.arange(lp, dtype=jnp.int32) - PADF, 0).astype(F32)
    inv_freq = ROPE_THETA ** (-jnp.arange(0, QK_ROPE, 2, dtype=F32) / QK_ROPE)
    ang = pos[:, None] * inv_freq[None, :]
    cos, sin = jnp.cos(ang), jnp.sin(ang)
    z32, z64 = jnp.zeros((lp, 32), F32), jnp.zeros((lp, 64), F32)
    cos_t = jnp.concatenate([cos, cos, jnp.ones((lp, 64), F32)], axis=1)
    sa = jnp.concatenate([-sin, z32, z64], axis=1)
    sb = jnp.concatenate([z32, sin, z64], axis=1)
    return cos_t, sa, sb


def _row1(v, width=None):
    v = v.reshape(1, -1).astype(F32)
    return v if width is None else _pad_cols(v, width)


LATE = ("w_out", "w_mlp_up", "w_mlp_down")
MID = ("w_in", "w_q_up", "w_kv_up", "conv_w")


def _local_forward(head, x, tgt, p, late=None):
    assert head.shape[0] == ROWB and x.shape[0] % ROWB == 0
    lp = ROWB + x.shape[0]
    seq_rows = N_META + x.shape[0]
    f = {"seq_rows": seq_rows}
    w_all, wq, wk, wv, wkn = _layout_weights(p["w_in"], p["w_q_up"], p["w_kv_up"])
    f.update(w_all=w_all, wq=wq, wk=wk, wv=wv)
    f["hn"], cq, ckv, kr, f["z"], f["xbc"], f["dtr"] = _norm_in_proj(head, x, _row1(p["norm_mix_pre"]), w_all)
    f.update(cq=cq, ckv=ckv)
    f["rope"] = _rope_tables(lp)
    f["q"], f["k"], f["v"], f["cqn"], f["ckvn"] = _qkv(cq, ckv, kr, *f["rope"], _row1(p["q_a_norm"]),
                                                   _row1(p["kv_a_norm"]), wq, wkn, wv)
    f["att"], f["lse"] = _flash_fwd(f["q"], f["k"], f["v"])
    f["cw"] = jnp.pad(p["conv_w"].astype(F32), ((0, 8 - CONV_K), (0, 0)))
    f["xact"] = _conv_fwd(f["xbc"], f["cw"], _row1(p["conv_b"]))
    f["dt_bias"], f["a_log"] = _row1(p["dt_bias"], LANE), _row1(p["a_log"], LANE)
    f["y"], f["hprev"], gathered = _ssd_fwd(f["xact"], f["dtr"], f["dt_bias"], f["a_log"], seq_rows,
                                            gather=[late[n] for n in LATE] if late else ())
    p = {**p, **{n: _from_shards(n, s) for n, s in zip(LATE, gathered)}}
    f["p"] = p
    f["dskip"] = jnp.repeat(p["d_skip"].reshape(-1).astype(F32), SSM_HEAD_DIM).reshape(1, SSM_WIDTH)
    f["ssm"] = _ssd_post(f["y"], f["xact"], f["z"], f["dskip"], _row1(p["ssm_norm"]))
    f["mix"], f["h1"] = _out_proj(f["att"], f["ssm"], head, x, p["w_out"], _row1(p["norm_mix_post"]))
    f["hn2"], f["u"], f["a"], f["f"], f["dh2"], loss8 = _mlp_fwd(
        f["h1"], tgt, p["w_mlp_up"], p["w_mlp_down"], _row1(p["norm_mlp_pre"]), _row1(p["norm_mlp_post"]), seq_rows)
    f["loss"] = 0.5 * jnp.sum(loss8) / D_MODEL
    return f


MLPB = 256


def _mlp_bwd(dh2, f, h1, u, w_up, w_down, g_pre, g_post):
    lp = h1.shape[0]

    def body(dh2_ref, f_ref, h1_ref, u_ref, wu_hbm, wd_hbm, gpre_ref, gpost_ref,
             dh1_ref, du_ref, df_ref, dgpre_ref, dgpost_ref, wu, wd, sems):
        _resident(wu_hbm, wu, sems.at[0])
        _resident(wd_hbm, wd, sems.at[1])
        dh2_ = dh2_ref[...]
        df, dgp = _rms_bwd(f_ref[...], gpost_ref[...], dh2_)
        dfb = df.astype(BF16)
        df_ref[...] = dfb
        da = _dot(dfb, wd[...], NT)
        du = (da * (2.0 * u_ref[...].astype(F32))).astype(BF16)
        du_ref[...] = du
        dhn2 = _dot(du, wu[...], NT)
        dx, dgq = _rms_bwd(h1_ref[...], gpre_ref[...], dhn2)
        dh1_ref[...] = dh2_ + dx
        _acc(dgpre_ref, _colsum8(dgq))
        _acc(dgpost_ref, _colsum8(dgp))

    return pl.pallas_call(
        body, name="mlp_bwd", grid=(lp // MLPB,),
        in_specs=[_rowspec(MLPB, D_MODEL)] * 3 + [_rowspec(MLPB, D_FF), ANY, ANY] + [_fullspec((1, D_MODEL))] * 2,
        out_specs=[_rowspec(MLPB, D_MODEL), _rowspec(MLPB, D_FF), _rowspec(MLPB, D_MODEL),
                   _fullspec((8, D_MODEL)), _fullspec((8, D_MODEL))],
        out_shape=[_sds((lp, D_MODEL), F32), _sds((lp, D_FF), BF16), _sds((lp, D_MODEL), BF16),
                   _sds((8, D_MODEL), F32), _sds((8, D_MODEL), F32)],
        scratch_shapes=[pltpu.VMEM((D_MODEL, D_FF), BF16), pltpu.VMEM((D_FF, D_MODEL), BF16), pltpu.SemaphoreType.DMA((2,))],
        compiler_params=_cp(("arbitrary",)),
    )(dh2, f, h1, u, w_up, w_down, g_pre, g_post)


def _out_bwd(dh1, mix, att, w_out, g_post):
    lp = dh1.shape[0]

    def body(dh1_ref, mix_ref, att_ref, w_ref, g_ref, dmix_ref, datt_ref, dssm_ref, dg_ref, dl_ref):
        dmix, dg = _rms_bwd(mix_ref[...], g_ref[...], dh1_ref[...])
        dmb = dmix.astype(BF16)
        dmix_ref[...] = dmb
        datt = _dot(dmb, w_ref[pl.ds(0, 1024), :], NT).astype(BF16)
        datt_ref[...] = datt
        dssm_ref[...] = _dot(dmb, w_ref[pl.ds(1024, 1024), :], NT)
        _acc(dg_ref, _colsum8(dg))
        prod = datt.astype(F32) * att_ref[...].astype(F32)
        for hh in range(ATT_HEADS):
            d = jnp.sum(prod[:, hh * V_HEAD:(hh + 1) * V_HEAD], axis=1, keepdims=True)
            dl_ref[hh] = jnp.broadcast_to(d, (ROWB, LANE)).T[0:1, :]

    return pl.pallas_call(
        body, name="out_bwd", grid=(lp // ROWB,),
        in_specs=[_rowspec(ROWB, D_MODEL)] * 3 + [_fullspec((2048, D_MODEL)), _fullspec((1, D_MODEL))],
        out_specs=[_rowspec(ROWB, D_MODEL)] * 3 + [_fullspec((8, D_MODEL)),
                                                   pl.BlockSpec((ATT_HEADS, 1, ROWB), lambda i: (0, 0, i))],
        out_shape=[_sds((lp, D_MODEL), BF16), _sds((lp, 1024), BF16), _sds((lp, 1024), F32), _sds((8, D_MODEL), F32),
                   _sds((ATT_HEADS, 1, lp), F32)],
        compiler_params=_cp(("arbitrary",)),
    )(dh1, mix, att, w_out, g_post)


def _ssd_post_bwd(dssm, y, xact, z, dskip, gnorm):
    lp = y.shape[0]

    def body(do_ref, y_ref, x_ref, z_ref, d_ref, g_ref, dy_ref, dz_ref, dg_ref, dd_ref):
        z_, x_ = z_ref[...], x_ref[...]
        sg = _sigmoid(z_)
        sz = z_ * sg
        y2 = y_ref[...] + d_ref[...] * x_
        gt = y2 * sz
        r = lax.rsqrt(_group_mean(gt * gt) + EPS)
        gh = gt * r
        do = do_ref[...]
        dgh = do * g_ref[...]
        dgt = r * (dgh - gh * _group_mean(dgh * gh))
        dy2 = dgt * sz
        dy_ref[...] = dy2
        dz_ref[...] = (dgt * y2 * (sg * (1.0 + z_ * (1.0 - sg)))).astype(BF16)
        _acc(dg_ref, _colsum8(do * gh))
        _acc(dd_ref, _colsum8(dy2 * x_))

    return pl.pallas_call(
        body, name="ssd_post_bwd", grid=(lp // ROWB,),
        in_specs=[_rowspec(ROWB, SSM_WIDTH)] * 4 + [_fullspec((1, SSM_WIDTH))] * 2,
        out_specs=[_rowspec(ROWB, SSM_WIDTH)] * 2 + [_fullspec((8, SSM_WIDTH))] * 2,
        out_shape=[_sds((lp, SSM_WIDTH), F32), _sds((lp, SSM_WIDTH), BF16), _sds((8, SSM_WIDTH), F32), _sds((8, SSM_WIDTH), F32)],
        compiler_params=_cp(("arbitrary",)),
    )(dssm, y, xact, z, dskip, gnorm)


def _ssd_bwd(dy, xact, dtr, hprev, dt_bias, a_log, dskip, seq_rows, exchange=()):
    lp = xact.shape[0]
    nc = lp // CHUNK
    gw = SSM_WIDTH // SSM_GROUPS
    hpg = SSM_HEADS // SSM_GROUPS
    nb = SSM_WIDTH // (2 * SSM_STATE)
    na = len(exchange)
    cps = ROWB // CHUNK
    nsteps = nc // cps

    def body(dy_ref, x_ref, b_ref, c_ref, dtr_ref, hp_ref, bias_ref, alog_ref, dsk_ref, tri_ref, tri3_ref, trit3_ref,
             ex3_ref, ext3_ref, *rest):
        xin, (dact_ref, ddtr_ref, da_ref, dbias_ref), xout = rest[:na], rest[na:na + 4], rest[na + 4:2 * na + 4]
        dh_s, sems = rest[2 * na + 4], rest[2 * na + 5:]
        step = pl.program_id(0)

        @pl.when(step == 0)
        def _():
            dh_s[...] = jnp.zeros_like(dh_s)
            da_ref[...] = jnp.zeros_like(da_ref)
            dbias_ref[...] = jnp.zeros_like(dbias_ref)

        if na:
            x_start, x_finish = _exchange_ops(xin, xout, *sems)
            pl.when(step == 0)(x_start)

        for lc in reversed(range(cps)):
            rows = pl.ds(lc * CHUNK, CHUNK)
            chunk((nsteps - 1 - step) * cps + lc, dy_ref.at[rows], x_ref.at[rows], b_ref.at[rows], c_ref.at[rows],
                  dtr_ref.at[rows], hp_ref.at[pl.ds(lc, 1)], bias_ref, alog_ref, dsk_ref, tri_ref, tri3_ref, trit3_ref,
                  ex3_ref, ext3_ref, dact_ref.at[rows], ddtr_ref.at[rows], da_ref, dbias_ref, dh_s)

        if na:
            pl.when(step == nsteps - 1)(x_finish)

    def chunk(c, dy_ref, x_ref, b_ref, c_ref, dtr_ref, hp_ref, bias_ref, alog_ref, dsk_ref, tri_ref, tri3_ref, trit3_ref,
              ex3_ref, ext3_ref, dact_ref, ddtr_ref, da_ref, dbias_ref, dh_s):
        tri = tri_ref[...]
        ex3 = ex3_ref[...]
        dt, a, acol, valid, dtr_ = _ssd_prep(dtr_ref[...], bias_ref, alog_ref, tri3_ref[...], c, seq_rows)
        arow = acol.T
        dtrow = dt.T
        alast = acol[CHUNK - 1:CHUNK, :]
        e_all = _dot01_r(jnp.exp(acol), ex3)
        wgt0 = jnp.exp(alast - acol)
        wgt = wgt0 * dt
        wx_all = _dot01_r(wgt, ex3)
        elast = jnp.exp(alast)
        dec_all = _dot01_r(_row16(elast), ex3)[0:1, :]
        causal = tri > 0.5
        upper = tri.T > 0.5
        lane_id = lax.broadcasted_iota(jnp.int32, (1, LANE), 1)
        sub_id = lax.broadcasted_iota(jnp.int32, (CHUNK, 1), 0)
        dacol = jnp.zeros((CHUNK, LANE), F32)
        darowf = jnp.zeros((CHUNK, LANE), F32)
        ddtrowf = jnp.zeros((CHUNK, LANE), F32)
        dwgt = jnp.zeros((CHUNK, LANE), F32)
        delast = jnp.zeros((1, LANE), F32)
        for g in range(SSM_GROUPS):
            gs = slice(g * gw, (g + 1) * gw)
            ext3_g = ext3_ref[g]
            bg = b_ref[:, g * SSM_STATE:(g + 1) * SSM_STATE]
            cg = c_ref[:, g * SSM_STATE:(g + 1) * SSM_STATE]
            bgb, cgb = bg.astype(BF16), cg.astype(BF16)
            xg = x_ref[:, gs]
            dyg = dy_ref[:, gs]
            hg = hp_ref[0, :, gs]
            dhg = dh_s[:, gs]
            hgb, dhgb = hg.astype(BF16), dhg.astype(BF16)
            gm = _dot(cgb, bgb, NT)
            gmt = _dot(bgb, cgb, NT)
            y_off = _dot(cgb, hgb) * e_all[:, gs]
            dy0 = (dyg * e_all[:, gs]).astype(BF16)
            dcg = _dot(dy0, hgb, NT)
            dh_in = _dot(cg.T.astype(BF16), dy0) + dhg * dec_all[:, gs]
            dacol = dacol + _dot01_r(dyg * y_off, ext3_g)
            xw = xg * wx_all[:, gs]
            dxw = _dot(bgb, dhgb)
            dx_state = dxw * wx_all[:, gs]
            dwgt = dwgt + _dot01_r(dxw * xg, ext3_g)
            dbt = _dot(dhgb, xw.astype(BF16), NT)
            hh = _colsum8(dhg * hg)
            hh16 = jnp.concatenate([hh, jnp.zeros_like(hh)], axis=0)
            delast = delast + jnp.sum(_dot01_r(hh16, ext3_g), axis=0, keepdims=True)
            dgm = jnp.zeros((CHUNK, CHUNK), F32)
            for r in range(hpg):
                hd = g * hpg + r
                cs = slice(r * SSM_HEAD_DIM, (r + 1) * SSM_HEAD_DIM)
                acol_r, arow_r = acol[:, hd:hd + 1], arow[hd:hd + 1, :]
                dtrow_r, dtcol_r = dtrow[hd:hd + 1, :], dt[:, hd:hd + 1]
                lm = jnp.where(causal, jnp.exp(jnp.where(causal, acol_r - arow_r, 0.0)), 0.0)
                lmt = jnp.where(upper, jnp.exp(jnp.where(upper, arow_r - acol_r, 0.0)), 0.0)
                wt = gmt * lmt * dtcol_r
                dy_r = dyg[:, cs].astype(BF16)
                dx_r = _dot(wt.astype(BF16), dy_r)
                dw = _dot(dy_r, xg[:, cs].astype(BF16), NT)
                t1 = dw * lm
                dgm = dgm + t1 * dtrow_r
                q1 = t1 * gm
                m = q1 * dtrow_r
                dacol = dacol + jnp.sum(m, axis=1, keepdims=True) * (lane_id == hd).astype(F32)
                darowf = darowf - (sub_id == hd).astype(F32) * jnp.sum(m, axis=0, keepdims=True)
                ddtrowf = ddtrowf + (sub_id == hd).astype(F32) * jnp.sum(q1, axis=0, keepdims=True)
                dact_ref[:, pl.ds(hd * SSM_HEAD_DIM, SSM_HEAD_DIM)] = (
                    dx_r + dx_state[:, cs] + dyg[:, cs] * dsk_ref[:, pl.ds(hd * SSM_HEAD_DIM, SSM_HEAD_DIM)])
            dgmb = dgm.astype(BF16)
            dact_ref[:, pl.ds(SSM_WIDTH + g * SSM_STATE, SSM_STATE)] = dbt.T + _dot(dgm.T.astype(BF16), cgb)
            dact_ref[:, pl.ds(SSM_WIDTH + 2 * SSM_STATE + g * SSM_STATE, SSM_STATE)] = dcg + _dot(dgmb, bgb)
            dh_s[:, gs] = dh_in
        t = dwgt * wgt
        dalast = jnp.sum(t, axis=0, keepdims=True) + delast * elast
        dacol_tot = dacol - t + darowf.T + (sub_id == CHUNK - 1).astype(F32) * dalast
        dda = _dot01_l(trit3_ref[...], dacol_tot)
        ddt = dwgt * wgt0 + ddtrowf.T + dda * a
        ddtr = jnp.where(valid, ddt * _sigmoid(dtr_), 0.0)
        ddtr_ref[...] = ddtr
        da_ref[...] += _colsum8(dda * dt) * a
        dbias_ref[...] += _colsum8(ddtr)

    rev = lambda c: nsteps - 1 - c
    rb = cps * CHUNK
    xs_spec = pl.BlockSpec((rb, SSM_WIDTH), lambda c: (rev(c), 0))
    dact, ddtr, da8, dbias8, *received = pl.pallas_call(
        body, name="ssd_bwd", grid=(nsteps,),
        in_specs=[xs_spec, xs_spec,
                  pl.BlockSpec((rb, 2 * SSM_STATE), lambda c: (rev(c), nb)),
                  pl.BlockSpec((rb, 2 * SSM_STATE), lambda c: (rev(c), nb + 1)),
                  pl.BlockSpec((rb, LANE), lambda c: (rev(c), 0)),
                  pl.BlockSpec((cps, SSM_STATE, SSM_WIDTH), lambda c: (rev(c), 0, 0)),
                  _fullspec((1, LANE)), _fullspec((1, LANE)), _fullspec((1, SSM_WIDTH)),
                  _fullspec((CHUNK, CHUNK)), _fullspec((CHUNK, 3 * CHUNK)), _fullspec((CHUNK, 3 * CHUNK)),
                  _fullspec((3 * LANE, SSM_WIDTH)), _fullspec((SSM_GROUPS, 3 * gw, LANE))] + [ANY] * na,
        out_specs=[pl.BlockSpec((rb, CONV_DIM), lambda c: (rev(c), 0)), pl.BlockSpec((rb, LANE), lambda c: (rev(c), 0)),
                   _fullspec((8, LANE)), _fullspec((8, LANE))] + [ANY] * na,
        out_shape=[_sds((lp, CONV_DIM), F32), _sds((lp, LANE), F32), _sds((8, LANE), F32), _sds((8, LANE), F32)]
        + [_sds(e.shape, e.dtype) for e in exchange],
        scratch_shapes=[pltpu.VMEM((SSM_STATE, SSM_WIDTH), F32)] + (_gather_scratch(na) if na else []),
        compiler_params=_cp(("arbitrary",)),
    )(dy, xact, xact, xact, dtr, hprev, dt_bias, a_log, dskip, _tri_mat(), _x3(_tri_mat(), 1), _x3(_tri_mat().T, 1),
      _x3(_expand_mat(), 0), jnp.stack([_x3(_expand_mat().T[g * gw:(g + 1) * gw], 0) for g in range(SSM_GROUPS)]),
      *exchange)
    return dact, ddtr, da8, dbias8, received


def _conv_bwd(dact, xbc, cw, cb):
    lp, c = xbc.shape
    t8 = ROWB // 8
    nb = lp // ROWB

    def body(d_ref, dnext_ref, x_ref, prev_ref, next_ref, w_ref, b_ref, dx_ref, dw_ref, db_ref, xb, dp):
        i = pl.program_id(0)
        last = i == nb - 1
        xb[pl.ds(0, 8), :] = jnp.where(i > 0, prev_ref[...], 0.0)
        xb[pl.ds(8, ROWB), :] = x_ref[...]
        xb[pl.ds(8 + ROWB, 8), :] = jnp.where(last, 0.0, next_ref[...])

        @pl.when(i == 0)
        def _():
            dw_ref[...] = jnp.zeros_like(dw_ref)
            db_ref[...] = jnp.zeros_like(db_ref)

        sub = lax.broadcasted_iota(jnp.int32, (8, 1), 0)
        x0 = 8 - (CONV_K - 1)

        def strip(s, carry):
            cs = pl.ds(pl.multiple_of(s * LANE, LANE), LANE)
            w, b = w_ref[:, cs], b_ref[:, cs]

            def dpre_rows(r0, n, d):
                xs = _shifted_rows(xb, r0, n, cs, [x0 + kk for kk in range(CONV_K)])
                pre = b + sum(w[kk:kk + 1, :] * xs[kk] for kk in range(CONV_K))
                sg = _sigmoid(pre)
                return d * (sg * (1.0 + pre * (1.0 - sg))), xs

            dws = [jnp.zeros((8, LANE), F32) for _ in range(CONV_K)]
            dbs = jnp.zeros((8, LANE), F32)
            for r0 in range(0, ROWB, CONV_ROWS):
                dpre, xs = dpre_rows(r0, CONV_ROWS, d_ref[pl.ds(r0, CONV_ROWS), cs])
                dp[pl.ds(r0, CONV_ROWS), cs] = dpre
                dbs = dbs + _colsum8(dpre)
                for kk in range(CONV_K):
                    dws[kk] = dws[kk] + _colsum8(dpre * xs[kk])
            dp[pl.ds(ROWB, 8), cs] = dpre_rows(ROWB, 8, jnp.where(last, 0.0, dnext_ref[:, cs]))[0]
            dwv = sum(jnp.where(sub == kk, jnp.sum(dws[kk], axis=0, keepdims=True), 0.0) for kk in range(CONV_K))
            dw_ref[:, cs] += dwv
            db_ref[:, cs] += dbs
            for r0 in range(0, ROWB, CONV_ROWS):
                ahead = _shifted_rows(dp, r0, CONV_ROWS, cs, [CONV_K - 1 - kk for kk in range(CONV_K)])
                dx = sum(w[kk:kk + 1, :] * ahead[kk] for kk in range(CONV_K))
                dx_ref[pl.ds(r0, CONV_ROWS), cs] = dx.astype(BF16)
            return carry

        lax.fori_loop(0, c // LANE, strip, 0)

    nxt = lambda i: (jnp.minimum((i + 1) * t8, lp // 8 - 1), 0)
    prv = lambda i: (jnp.maximum(i * t8 - 1, 0), 0)
    return pl.pallas_call(
        body, name="conv_bwd", grid=(nb,),
        in_specs=[_rowspec(ROWB, c), pl.BlockSpec((8, c), nxt), _rowspec(ROWB, c), pl.BlockSpec((8, c), prv),
                  pl.BlockSpec((8, c), nxt), _fullspec((8, c)), _fullspec((1, c))],
        out_specs=[_rowspec(ROWB, c), _fullspec((8, c)), _fullspec((8, c))],
        out_shape=[_sds((lp, c), BF16), _sds((8, c), F32), _sds((8, c), F32)],
        scratch_shapes=[pltpu.VMEM((ROWB + 16, c), F32), pltpu.VMEM((ROWB + 8, c), F32)],
        compiler_params=_cp(("arbitrary",)),
    )(dact, dact, xbc, xbc, xbc, cw, cb)


def _flash_bwd(q, k, v, datt, lse_row, delta_row, cos, sa, sb):
    lp = q.shape[0]
    nk = lp // ROWB

    def body(k_ref, v_ref, q_ref, do_ref, lse_ref, dl_ref, cos_ref, sa_ref, sb_ref, dq_ref, dk_ref, dv_ref,
             dq_acc, dk_acc, dv_acc):
        j = pl.program_id(1)

        @pl.when(j == 0)
        def _():
            dq_acc[...] = jnp.zeros_like(dq_acc)

        dk_acc[...] = jnp.zeros_like(dk_acc)
        dv_acc[...] = jnp.zeros_like(dv_acc)

        def tile(i, masked, key0=0, nkeys=ROWB):
            keys = pl.ds(key0, nkeys)
            kb, vb = k_ref[keys, :], v_ref[keys, :]
            off = pl.multiple_of(i * ROWB, ROWB)
            qb = q_ref[pl.ds(off, ROWB), :]
            dob = do_ref[pl.ds(off, ROWB), :]
            lse_r = lse_ref[0, :, pl.ds(off, ROWB)]
            dl_r = dl_ref[0, :, pl.ds(off, ROWB)]
            st = _dot(kb, qb, NT)
            if masked:
                krow = j * ROWB + key0 + lax.broadcasted_iota(jnp.int32, st.shape, 0)
                qrow = i * ROWB + lax.broadcasted_iota(jnp.int32, st.shape, 1)
                st = jnp.where(_att_ok(qrow, krow), st, NEG)
            pt = jnp.exp2(st - lse_r)
            dv_acc[keys, :] += _dot(pt.astype(BF16), dob)
            dpt = _dot(vb, dob, NT)
            dst = (pt * (dpt - dl_r)).astype(BF16)
            dk_acc[keys, :] += _dot(dst, qb)
            dq_acc[pl.ds(off, ROWB), :] += _dot(dst, kb, ((0,), (0,)))

        @pl.when(j == 0)
        def _():
            _pair_loop(0, nk, lambda i: tile(i, True, ROWB - META_KEYS, META_KEYS), (4, 2))

        odd = jnp.bitwise_and(nk - 1 - j, 3)
        for r in range(4):
            @pl.when((j > 0) & (odd == r))
            def _(r=r):
                tile(j, True)
                for d in range(r):
                    tile(j + 1 + d, False)

        @pl.when(j > 0)
        def _():
            _pair_loop(j + 1 + odd, nk, lambda i: tile(i, False), (4,))

        dk_ref[...] = (dk_acc[...] * LN2).astype(BF16)
        dv_ref[...] = dv_acc[...].astype(BF16)
        dq = dq_acc[pl.ds(pl.multiple_of(j * ROWB, ROWB), ROWB), :] * ATT_SCALE
        half = HEADW // 2
        dq_ref[:, pl.ds(0, half)] = dq[:, :half].astype(BF16)
        dq_ref[:, pl.ds(half, half)] = _rope_t(dq[:, half:], cos_ref[...], sa_ref[...], sb_ref[...]).astype(BF16)

    stat = pl.BlockSpec((1, 1, lp), lambda h, j: (h, 0, 0))
    blk = pl.BlockSpec((ROWB, HEADW), lambda h, j: (j, h))
    tab = pl.BlockSpec((ROWB, HEADW // 2), lambda h, j: (j, 0))
    return pl.pallas_call(
        body, name="flash_bwd", grid=(ATT_HEADS, nk),
        in_specs=[blk, pl.BlockSpec((ROWB, V_HEAD), lambda h, j: (j, 2 * h)),
                  pl.BlockSpec((lp, HEADW), lambda h, j: (0, h)), pl.BlockSpec((lp, V_HEAD), lambda h, j: (0, h)),
                  stat, stat, tab, tab, tab],
        out_specs=[blk, blk, pl.BlockSpec((ROWB, V_HEAD), lambda h, j: (j, h))],
        out_shape=[_sds((lp, ATT_HEADS * HEADW), BF16), _sds((lp, ATT_HEADS * HEADW), BF16),
                   _sds((lp, ATT_HEADS * V_HEAD), BF16)],
        scratch_shapes=[pltpu.VMEM((lp, HEADW), F32), pltpu.VMEM((ROWB, HEADW), F32), pltpu.VMEM((ROWB, V_HEAD), F32)],
        compiler_params=_cp(("arbitrary", "arbitrary")),
    )(k, v, q, datt, lse_row, delta_row, cos, sa, sb)


def _qkv_bwd(dqp, dk, dv, cq, ckv, cos, sa, sb, gq, gkv, wq, wk, wv, dz, dxbc, ddtr):
    lp = cq.shape[0]
    qw = ATT_HEADS * HEADW

    def body(dqp_ref, dk_ref, dv_ref, cq_ref, ckv_ref, cos_ref, sa_ref, sb_ref, gq_ref, gkv_ref, wq_ref, wk_ref, wv_ref,
             dz_ref, dxbc_ref, ddtr_ref, dp_ref, dgq_ref, dgkv_ref):
        dcq, dgq = _rms_bwd(cq_ref[...], gq_ref[...], _dot(dqp_ref[...], wq_ref[...], NT))
        dp_ref[:, pl.ds(PC_Q, Q_LORA)] = dcq.astype(BF16)
        dkb = dk_ref[...]
        half = HEADW // 2
        dksum = sum(dkb[:, hh * HEADW + half:(hh + 1) * HEADW].astype(F32) for hh in range(ATT_HEADS))
        dp_ref[:, pl.ds(PC_KR, half)] = jnp.zeros((ROWB, half), BF16)
        dp_ref[:, pl.ds(PC_KR + half, half)] = _rope_t(dksum, cos_ref[...], sa_ref[...], sb_ref[...]).astype(BF16)
        dckvn = _dot(dkb, wk_ref[...], NT) + _dot(dv_ref[...], wv_ref[...], NT)
        dckv, dgkv = _rms_bwd(ckv_ref[...], gkv_ref[...], dckvn)
        dp_ref[:, pl.ds(PC_KV, KV_LORA)] = dckv.astype(BF16)
        dp_ref[:, pl.ds(PC_Z, SSM_WIDTH)] = dz_ref[...]
        dp_ref[:, pl.ds(PC_XBC, CONV_DIM)] = dxbc_ref[...]
        dp_ref[:, pl.ds(PC_DT, LANE)] = ddtr_ref[...].astype(BF16)
        _acc(dgq_ref, _colsum8(dgq))
        _acc(dgkv_ref, _colsum8(dgkv))

    return pl.pallas_call(
        body, name="qkv_bwd", grid=(lp // ROWB,),
        in_specs=[_rowspec(ROWB, qw), _rowspec(ROWB, qw), _rowspec(ROWB, ATT_HEADS * V_HEAD),
                  _rowspec(ROWB, Q_LORA), _rowspec(ROWB, KV_LORA)] + [_rowspec(ROWB, HEADW // 2)] * 3
        + [_fullspec((1, Q_LORA)), _fullspec((1, KV_LORA)), _fullspec((Q_LORA, qw)), _fullspec((KV_LORA, qw)),
           _fullspec((KV_LORA, ATT_HEADS * V_HEAD)), _rowspec(ROWB, SSM_WIDTH), _rowspec(ROWB, CONV_DIM),
           _rowspec(ROWB, LANE)],
        out_specs=[_rowspec(ROWB, PROJ_W), _fullspec((8, Q_LORA)), _fullspec((8, KV_LORA))],
        out_shape=[_sds((lp, PROJ_W), BF16), _sds((8, Q_LORA), F32), _sds((8, KV_LORA), F32)],
        compiler_params=_cp(("arbitrary",)),
    )(dqp, dk, dv, cq, ckv, cos, sa, sb, gq, gkv, wq, wk, wv, dz, dxbc, ddtr)


def _in_bwd(dproj, head, x, dh1, g, w_all, exchange=()):
    lp = dh1.shape[0]
    nsteps = lp // ROWB
    na = len(exchange)

    def body(dp_ref, head_ref, x_ref, dh1_ref, g_ref, w_ref, *rest):
        xin, (dx_ref, dhead_ref, dg_ref), xout, sems = rest[:na], rest[na:na + 3], rest[na + 3:2 * na + 3], rest[2 * na + 3:]
        step = pl.program_id(0)
        if na:
            x_start, x_finish = _exchange_ops(xin, xout, *sems)
            pl.when(step == 0)(x_start)
        dx, dg = _rms_bwd(_h_block(head_ref, x_ref), g_ref[...], _dot(dp_ref[...], w_ref[...], NT))
        dh = dh1_ref[...] + dx

        @pl.when(step == 0)
        def _():
            dhead_ref[...] = dh

        @pl.when(step > 0)
        def _():
            dx_ref[...] = dh

        _acc(dg_ref, _colsum8(dg))
        if na:
            pl.when(step == nsteps - 1)(x_finish)

    dx, dhead, dg8, *received = pl.pallas_call(
        body, name="in_bwd", grid=(nsteps,),
        in_specs=[_rowspec(ROWB, PROJ_W), _fullspec((ROWB, D_MODEL)), _xspec(), _rowspec(ROWB, D_MODEL),
                  _fullspec((1, D_MODEL)), _fullspec((D_MODEL, PROJ_W))] + [ANY] * na,
        out_specs=[_xspec(), _fullspec((ROWB, D_MODEL)), _fullspec((8, D_MODEL))] + [ANY] * na,
        out_shape=[_sds(x.shape, F32), _sds((ROWB, D_MODEL), F32), _sds((8, D_MODEL), F32)]
        + [_sds(e.shape, e.dtype) for e in exchange],
        scratch_shapes=_gather_scratch(na) if na else [],
        compiler_params=_cp(("arbitrary",)),
    )(dproj, head, x, dh1, g, w_all, *exchange)
    return dx, dhead, dg8, received


def _tile_of(n, cap=1024):
    return max(t for t in range(LANE, min(n, cap) + 1, LANE) if n % t == 0)


def _matmul_tn(name, a, b):
    rows, kd = a.shape
    nd = b.shape[1]
    tk, tn = _tile_of(kd), _tile_of(nd)
    rb = 3 * ROWB if rows % (3 * ROWB) == 0 else ROWB

    def body(a_ref, b_ref, o_ref):
        @pl.when(pl.program_id(2) == 0)
        def _():
            o_ref[...] = jnp.zeros_like(o_ref)

        o_ref[...] += _dot(a_ref[...], b_ref[...], ((0,), (0,)))

    return pl.pallas_call(
        body, name=name, grid=(kd // tk, nd // tn, rows // rb),
        in_specs=[pl.BlockSpec((rb, tk), lambda i, j, r: (r, i)), pl.BlockSpec((rb, tn), lambda i, j, r: (r, j))],
        out_specs=pl.BlockSpec((tk, tn), lambda i, j, r: (i, j)), out_shape=_sds((kd, nd), F32),
        compiler_params=_cp(("arbitrary", "arbitrary", "arbitrary")),
    )(a, b)


def _local_backward(head, x, f, exchange_late=False):
    p = f["p"]
    g = {}
    row = lambda v: _row1(v)
    s8 = lambda v: jnp.sum(v, axis=0)
    dh1, du, df, dgpre, dgpost = _mlp_bwd(f["dh2"], f["f"], f["h1"], f["u"], p["w_mlp_up"], p["w_mlp_down"],
                                          row(p["norm_mlp_pre"]), row(p["norm_mlp_post"]))
    g["norm_mlp_pre"], g["norm_mlp_post"] = s8(dgpre), s8(dgpost)
    g["w_mlp_up"] = _matmul_tn("dw_mlp_up", f["hn2"], du)
    g["w_mlp_down"] = _matmul_tn("dw_mlp_down", f["a"], df)
    dmix, datt, dssm, dgmp, delta = _out_bwd(dh1, f["mix"], f["att"], p["w_out"], row(p["norm_mix_post"]))
    g["norm_mix_post"] = s8(dgmp)
    g["w_out"] = jnp.concatenate([_matmul_tn("dw_out_att", f["att"], dmix), _matmul_tn("dw_out_ssm", f["ssm"], dmix)], axis=0)
    dy, dz, dgn, dd = _ssd_post_bwd(dssm, f["y"], f["xact"], f["z"], f["dskip"], row(p["ssm_norm"]))
    g["ssm_norm"] = s8(dgn)
    g["d_skip"] = s8(dd).reshape(SSM_HEADS, SSM_HEAD_DIM).sum(axis=1)
    dact, ddtr, da8, dbias8, received = _ssd_bwd(
        dy, f["xact"], f["dtr"], f["hprev"], f["dt_bias"], f["a_log"], f["dskip"], f["seq_rows"],
        exchange=[_to_chunks(n, g[n]).astype(BF16) for n in LATE] if exchange_late else ())
    g["a_log"], g["dt_bias"] = s8(da8)[:SSM_HEADS], s8(dbias8)[:SSM_HEADS]
    dxbc, dcw8, dcb8 = _conv_bwd(dact, f["xbc"], f["cw"], row(p["conv_b"]))
    g["conv_w"], g["conv_b"] = dcw8[:CONV_K], s8(dcb8)
    dqp, dkb, dv = _flash_bwd(f["q"], f["k"], f["v"], datt, f["lse"], delta, *f["rope"])
    dproj, dgq, dgkv = _qkv_bwd(dqp, dkb, dv, f["cq"], f["ckv"], *f["rope"], row(p["q_a_norm"]), row(p["kv_a_norm"]),
                                f["wq"], f["wk"], f["wv"], dz, dxbc, ddtr)
    g["q_a_norm"], g["kv_a_norm"] = s8(dgq), s8(dgkv)
    dwq = _matmul_tn("dw_q_up", f["cqn"], dqp).reshape(Q_LORA, ATT_HEADS, HEADW)
    g["w_q_up"] = dwq[:, :, :QK_NOPE + QK_ROPE].reshape(Q_LORA, -1)
    dwk = _matmul_tn("dw_k_up", f["ckvn"], dkb).reshape(KV_LORA, ATT_HEADS, HEADW)[:, :, :QK_NOPE]
    dwv = _matmul_tn("dw_v_up", f["ckvn"], dv).reshape(KV_LORA, ATT_HEADS, V_HEAD)
    g["w_kv_up"] = jnp.concatenate([dwk, dwv], axis=2).reshape(KV_LORA, -1)
    dwa = _matmul_tn("dw_in", f["hn"], dproj)
    g["w_in"] = jnp.concatenate([dwa[:, PC_Q:PC_KR], dwa[:, PC_KR + QK_NOPE:PC_KR + QK_NOPE + QK_ROPE],
                                 dwa[:, PC_Z:PC_DT + SSM_HEADS]], axis=1)
    dx, dhead, dgin, received_mid = _in_bwd(
        dproj, head, x, dh1, row(p["norm_mix_pre"]), f["w_all"],
        exchange=[_to_chunks(n, g[n]).astype(BF16) for n in MID] if exchange_late else ())
    g["norm_mix_pre"] = s8(dgin)
    g["meta_tokens"] = dhead[PADF:]
    return dx, g, dict(zip(LATE + MID, list(received) + list(received_mid)))


BIG = {"w_in": ((D_MODEL, IN_WIDTH), 1), "w_q_up": ((Q_LORA, ATT_HEADS * (QK_NOPE + QK_ROPE)), 1),
       "w_kv_up": ((KV_LORA, ATT_HEADS * (QK_NOPE + V_HEAD)), 1), "w_out": ((2 * D_MODEL, D_MODEL), 0),
       "w_mlp_up": ((D_MODEL, D_FF), 1), "w_mlp_down": ((D_FF, D_MODEL), 0), "conv_w": ((CONV_K, CONV_DIM), 1),
       "meta_tokens": ((N_META, D_MODEL), 1)}
SMALL = {"norm_mix_pre": D_MODEL, "q_a_norm": Q_LORA, "kv_a_norm": KV_LORA, "conv_b": CONV_DIM, "dt_bias": SSM_HEADS,
         "a_log": SSM_HEADS, "d_skip": SSM_HEADS, "ssm_norm": SSM_WIDTH, "norm_mix_post": D_MODEL,
         "norm_mlp_pre": D_MODEL, "norm_mlp_post": D_MODEL}
WEIGHT_ORDER = ("meta_tokens", "norm_mix_pre", "w_in", "q_a_norm", "w_q_up", "kv_a_norm", "w_kv_up", "conv_w", "conv_b",
                "dt_bias", "a_log", "d_skip", "ssm_norm", "w_out", "norm_mix_post", "norm_mlp_pre", "w_mlp_up",
                "w_mlp_down", "norm_mlp_post")
ADAM_ROWS = 256


def _shard_shape(name):
    shape, ax = BIG[name]
    return tuple(d // N_DEV if a == ax else d for a, d in enumerate(shape))


SMALL_ROWS = -(-sum(SMALL.values()) // (LANE * 8)) * 8


def _pack(flats, rows):
    v = jnp.concatenate([f.reshape(-1) for f in flats])
    return jnp.pad(v, (0, rows * LANE - v.shape[0])).reshape(rows, LANE)


def _unpack(packed, shapes):
    v = packed.reshape(-1)
    out, o = [], 0
    for s in shapes:
        n = math.prod(s)
        out.append(v[o:o + n].reshape(s))
        o += n
    return out


def _to_chunks(name, full):
    shape, ax = BIG[name]
    if ax == 0:
        return full.reshape((N_DEV,) + _shard_shape(name))
    k, n = shape
    return full.reshape(k, N_DEV, n // N_DEV).transpose(1, 0, 2)


def _from_shards(name, shards):
    shape, ax = BIG[name]
    if ax == 0:
        return shards.reshape(shape)
    return shards.transpose(1, 0, 2).reshape(shape)


def _peer(k):
    x, y, c = lax.axis_index("x"), lax.axis_index("y"), lax.axis_index("c")
    px = 1 - x if k & 4 else x
    py = 1 - y if k & 2 else y
    pc = 1 - c if k & 1 else c
    return (px, py, pc), 4 * px + 2 * py + pc


def _gather_ops(x_refs, out_refs, send_sems, recv_sems, local_sems):
    na = len(x_refs)
    chips = (4, 2, 6)

    def copy(a, n, block, to, src=None):
        return pltpu.make_async_remote_copy(
            src_ref=out_refs[a].at[block] if src is None else src, dst_ref=out_refs[a].at[block],
            send_sem=send_sems.at[7 * a + n], recv_sem=recv_sems.at[7 * a + n], device_id=to, device_id_type=MESH)

    def mine():
        me = _peer(0)[1]
        return [pltpu.make_async_copy(x_refs[a], out_refs[a].at[me], local_sems.at[a]) for a in range(na)]

    def first():
        me, sibling = _peer(0)[1], _peer(1)[0]
        out = [copy(a, 0, me, sibling, src=x_refs[a]) for a in range(na)]
        return out + [copy(a, 1 + n, me, _peer(k)[0], src=x_refs[a]) for n, k in enumerate(chips) for a in range(na)]

    def passed():
        sibling = _peer(1)[0]
        return [copy(a, 4 + n, _peer(k)[1], sibling) for n, k in enumerate(chips) for a in range(na)]

    def start():
        for cp in mine() + first():
            cp.start()

    def forward():
        sibling = _peer(1)[0]
        fwd = passed()
        for n, k in enumerate(chips):
            for a in range(na):
                copy(a, 1 + n, _peer(k)[1], sibling).wait_recv()
                fwd[n * na + a].start()

    def finish():
        sibling = _peer(1)[0]
        for a in range(na):
            copy(a, 0, _peer(1)[1], sibling).wait_recv()
        for n, k in enumerate(chips):
            for a in range(na):
                copy(a, 4 + n, _peer(k | 1)[1], sibling).wait_recv()
        for cp in first() + passed():
            cp.wait_send()
        for cp in mine():
            cp.wait()

    return start, forward, finish


def _gather_scratch(na):
    return [pltpu.SemaphoreType.DMA((7 * na,)), pltpu.SemaphoreType.DMA((7 * na,)), pltpu.SemaphoreType.DMA((na,))]


def _all_gather(shards):
    na = len(shards)

    def body(*refs):
        for step in _gather_ops(refs[:na], refs[na:2 * na], *refs[2 * na:]):
            step()

    return pl.pallas_call(
        body, name="all_gather_weights", out_shape=[_sds((N_DEV,) + s.shape, s.dtype) for s in shards],
        in_specs=[ANY] * na, out_specs=[ANY] * na, scratch_shapes=_gather_scratch(na),
    )(*shards)


def _exchange(chunks, small):
    na = len(chunks) + 1

    def body(*refs):
        for step in _exchange_ops(refs[:na], refs[na:2 * na], *refs[2 * na:], whole=(na - 1,)):
            step()

    arrays = list(chunks) + [small]
    return pl.pallas_call(
        body, name="exchange_grads",
        out_shape=[_sds(c.shape, c.dtype) for c in chunks] + [_sds((N_DEV,) + small.shape, small.dtype)],
        in_specs=[ANY] * na, out_specs=[ANY] * na, scratch_shapes=_gather_scratch(na),
    )(*arrays)


def _exchange_ops(in_refs, out_refs, send_sems, recv_sems, local_sems, whole=()):
    na = len(in_refs)

    def src(a, idx):
        return in_refs[a] if a in whole else in_refs[a].at[idx]

    def own():
        me = _peer(0)[1]
        return [pltpu.make_async_copy(src(a, me), out_refs[a].at[me], local_sems.at[a]) for a in range(na)]

    def copy(a, k, sending):
        me = _peer(0)[1]
        to, idx = _peer(k)
        return pltpu.make_async_remote_copy(
            src_ref=src(a, idx if sending else me), dst_ref=out_refs[a].at[me if sending else idx],
            send_sem=send_sems.at[7 * a + k - 1], recv_sem=recv_sems.at[7 * a + k - 1],
            device_id=to, device_id_type=MESH)

    def sent():
        return [copy(a, k, True) for k in range(1, N_DEV) for a in range(na)]

    def start():
        for cp in own() + sent():
            cp.start()

    def finish():
        for k in range(1, N_DEV):
            for a in range(na):
                copy(a, k, False).wait_recv()
        for cp in sent():
            cp.wait_send()
        for cp in own():
            cp.wait()

    return start, finish


def _reduce_adamw(name, recv, w, m, v):
    rows, cols = w.shape
    blk = ADAM_ROWS if rows % ADAM_ROWS == 0 else rows
    c1 = 1.0 - ADAM_B1 ** ADAM_STEP
    c2 = 1.0 - ADAM_B2 ** ADAM_STEP

    def body(r_ref, w_ref, m_ref, v_ref, g_ref, d_ref, nm_ref, nv_ref):
        g = r_ref[0].astype(F32)
        for s in range(1, N_DEV):
            g = g + r_ref[s].astype(F32)
        g_ref[...] = g
        m_ = ADAM_B1 * m_ref[...] + (1.0 - ADAM_B1) * g
        v_ = ADAM_B2 * v_ref[...] + (1.0 - ADAM_B2) * (g * g)
        nm_ref[...] = m_
        nv_ref[...] = v_
        d_ref[...] = -ADAM_LR * ((m_ / c1) / (jnp.sqrt(v_ / c2) + ADAM_EPS) + ADAM_WD * w_ref[...])

    spec = _rowspec(blk, cols)
    return pl.pallas_call(
        body, name="reduce_adamw_" + name, grid=(rows // blk,),
        in_specs=[pl.BlockSpec((N_DEV, blk, cols), lambda i: (0, i, 0)), spec, spec, spec],
        out_specs=[spec] * 4, out_shape=[_sds((rows, cols), F32)] * 4,
        compiler_params=_cp(("arbitrary",)),
    )(recv, w, m, v)


def kernel(x, meta_tokens, norm_mix_pre, w_in, q_a_norm, w_q_up, kv_a_norm, w_kv_up, conv_w, conv_b, dt_bias, a_log, d_skip, ssm_norm, w_out, norm_mix_post, norm_mlp_pre, w_mlp_up, w_mlp_down, norm_mlp_post, loss_target, m_meta_tokens, m_norm_mix_pre, m_w_in, m_q_a_norm, m_w_q_up, m_kv_a_norm, m_w_kv_up, m_conv_w, m_conv_b, m_dt_bias, m_a_log, m_d_skip, m_ssm_norm, m_w_out, m_norm_mix_post, m_norm_mlp_pre, m_w_mlp_up, m_w_mlp_down, m_norm_mlp_post, v_meta_tokens, v_norm_mix_pre, v_w_in, v_q_a_norm, v_w_q_up, v_kv_a_norm, v_w_kv_up, v_conv_w, v_conv_b, v_dt_bias, v_a_log, v_d_skip, v_ssm_norm, v_w_out, v_norm_mix_post, v_norm_mlp_pre, v_w_mlp_up, v_w_mlp_down, v_norm_mlp_post):
    w = dict(meta_tokens=meta_tokens, norm_mix_pre=norm_mix_pre, w_in=w_in, q_a_norm=q_a_norm, w_q_up=w_q_up,
             kv_a_norm=kv_a_norm, w_kv_up=w_kv_up, conv_w=conv_w, conv_b=conv_b, dt_bias=dt_bias, a_log=a_log,
             d_skip=d_skip, ssm_norm=ssm_norm, w_out=w_out, norm_mix_post=norm_mix_post, norm_mlp_pre=norm_mlp_pre,
             w_mlp_up=w_mlp_up, w_mlp_down=w_mlp_down, norm_mlp_post=norm_mlp_post)
    m = dict(meta_tokens=m_meta_tokens, norm_mix_pre=m_norm_mix_pre, w_in=m_w_in, q_a_norm=m_q_a_norm, w_q_up=m_w_q_up,
             kv_a_norm=m_kv_a_norm, w_kv_up=m_w_kv_up, conv_w=m_conv_w, conv_b=m_conv_b, dt_bias=m_dt_bias,
             a_log=m_a_log, d_skip=m_d_skip, ssm_norm=m_ssm_norm, w_out=m_w_out, norm_mix_post=m_norm_mix_post,
             norm_mlp_pre=m_norm_mlp_pre, w_mlp_up=m_w_mlp_up, w_mlp_down=m_w_mlp_down, norm_mlp_post=m_norm_mlp_post)
    v = dict(meta_tokens=v_meta_tokens, norm_mix_pre=v_norm_mix_pre, w_in=v_w_in, q_a_norm=v_q_a_norm, w_q_up=v_w_q_up,
             kv_a_norm=v_kv_a_norm, w_kv_up=v_w_kv_up, conv_w=v_conv_w, conv_b=v_conv_b, dt_bias=v_dt_bias,
             a_log=v_a_log, d_skip=v_d_skip, ssm_norm=v_ssm_norm, w_out=v_w_out, norm_mix_post=v_norm_mix_post,
             norm_mlp_pre=v_norm_mlp_pre, w_mlp_up=v_w_mlp_up, w_mlp_down=v_w_mlp_down, norm_mlp_post=v_norm_mlp_post)
    big_names = [n for n in WEIGHT_ORDER if n in BIG]
    small_names = [n for n in WEIGHT_ORDER if n in SMALL]
    shard = lambda d, n: d[n].reshape(_shard_shape(n))

    f32_names = ("conv_w", "meta_tokens")
    early = [n for n in big_names if n not in LATE]
    gathered = _all_gather([shard(w, n).astype(F32 if n in f32_names else BF16) for n in early])
    p = {n: w[n].reshape(-1) for n in small_names}
    p.update({n: _from_shards(n, s) for n, s in zip(early, gathered)})
    head = jnp.concatenate([jnp.zeros((PADF, D_MODEL), F32), p["meta_tokens"]], axis=0)
    f = _local_forward(head, x[0], loss_target[0], p, late={n: shard(w, n).astype(BF16) for n in LATE})
    dx, g, recv_of = _local_backward(head, x[0], f, exchange_late=True)
    grad_x = dx[None]
    loss = lax.psum(f["loss"], ("x", "y", "c"))

    rest = [n for n in big_names if n not in LATE + MID]
    small = _pack([g[n] for n in small_names], SMALL_ROWS)
    *recv_rest, recv_small = _exchange([_to_chunks(n, g[n]) for n in rest], small)
    recv_of.update(zip(rest, recv_rest))

    outs = {}
    kinds = ("grad", "delta", "new_m", "new_v")
    for n, recv in ((n, recv_of[n]) for n in big_names):
        for kind, arr in zip(kinds, _reduce_adamw(n, recv, shard(w, n), shard(m, n), shard(v, n))):
            outs[kind, n] = arr.reshape(w[n].shape)
    packed = [_pack([d[n] for n in small_names], SMALL_ROWS) for d in (w, m, v)]
    for kind, arr in zip(kinds, _reduce_adamw("small", recv_small, *packed)):
        for n, val in zip(small_names, _unpack(arr, [(SMALL[n],) for n in small_names])):
            outs[kind, n] = val.reshape(w[n].shape)
    return (loss, grad_x) + tuple(outs[kind, n] for kind in ("grad", "delta", "new_m", "new_v") for n in WEIGHT_ORDER)
```

```python
import math

import jax
import jax.numpy as jnp
import numpy as np
from jax import lax
from jax.experimental import pallas as pl
from jax.experimental.pallas import tpu as pltpu

F32 = jnp.float32
BF16 = jnp.bfloat16

D_MODEL = 1024
N_META = 16
EPS = 1e-6
ATT_HEADS = 8
Q_LORA = 384
KV_LORA = 256
QK_NOPE = 128
QK_ROPE = 64
V_HEAD = 128
ROPE_THETA = 10000.0
SSM_HEADS = 16
SSM_HEAD_DIM = 64
SSM_WIDTH = 1024
SSM_GROUPS = 2
SSM_STATE = 128
CONV_K = 4
CHUNK = 128
CONV_DIM = 1536
D_FF = 4096
IN_SPLITS = (Q_LORA, KV_LORA, QK_ROPE, SSM_WIDTH, CONV_DIM, SSM_HEADS)
IN_WIDTH = sum(IN_SPLITS)
ADAM_LR, ADAM_B1, ADAM_B2, ADAM_EPS, ADAM_WD, ADAM_STEP = 0.001, 0.9, 0.999, 1e-08, 0.01, 10

LANE = 128
ROWB = 512
PADF = ROWB - N_META
HEADW = 256
PC_Q, PC_KV, PC_KR, PC_Z, PC_XBC, PC_DT, PROJ_W = 0, 384, 640, 896, 1920, 3456, 3584
NEG = -1e30
N_DEV = 8
VMEM_LIMIT = 56 * 1024 * 1024
MESH = pl.DeviceIdType.MESH


def _cp(sem, vmem=VMEM_LIMIT, **kw):
    return pltpu.CompilerParams(dimension_semantics=sem, vmem_limit_bytes=vmem, **kw)


def _dot(a, b, dims=((1,), (0,))):
    return lax.dot_general(a, b, (dims, ((), ())), preferred_element_type=F32)


def _bdot(a, b, dims=((1,), (0,))):
    return _dot(a.astype(BF16), b.astype(BF16), dims)


NT = ((1,), (1,))


def _rms_fwd(x, w):
    r = lax.rsqrt(jnp.mean(x * x, axis=-1, keepdims=True) + EPS)
    return (x * r) * w


def _rms_bwd(x, w, dy):
    r = lax.rsqrt(jnp.mean(x * x, axis=-1, keepdims=True) + EPS)
    xh = x * r
    g = dy * w
    dx = r * (g - xh * jnp.mean(g * xh, axis=-1, keepdims=True))
    return dx, dy * xh


def _sigmoid(x):
    return 0.5 * jnp.tanh(0.5 * x) + 0.5


def _colsum8(x):
    t, c = x.shape
    return jnp.sum(x.reshape(t // 8, 8, c), axis=0)


def _rowspec(t, c, cb=0):
    return pl.BlockSpec((t, c), lambda i: (i, cb))


def _fullspec(shape):
    n = len(shape)
    return pl.BlockSpec(shape, lambda i: (0,) * n)


def _sds(shape, dt):
    return jax.ShapeDtypeStruct(shape, dt)


def _acc(ref, val):
    @pl.when(pl.program_id(0) == 0)
    def _():
        ref[...] = jnp.zeros_like(ref)

    ref[...] += val


def _xspec():
    return pl.BlockSpec((ROWB, D_MODEL), lambda i: (jnp.maximum(i - 1, 0), 0))


def _h_block(head_ref, x_ref):
    return jnp.where(pl.program_id(0) == 0, head_ref[...], x_ref[...])


def _norm_in_proj(head, x, g, w_all):
    lp = head.shape[0] + x.shape[0]

    def body(head_ref, x_ref, g_ref, w_ref, hn_ref, cq_ref, ckv_ref, kr_ref, z_ref, xbc_ref, dt_ref):
        hn = _rms_fwd(_h_block(head_ref, x_ref), g_ref[...]).astype(BF16)
        hn_ref[...] = hn
        p = _dot(hn, w_ref[...])
        cq_ref[...] = p[:, PC_Q:PC_KV]
        ckv_ref[...] = p[:, PC_KV:PC_KR]
        kr_ref[...] = p[:, PC_KR:PC_Z]
        z_ref[...] = p[:, PC_Z:PC_XBC]
        xbc_ref[...] = p[:, PC_XBC:PC_DT]
        dt_ref[...] = p[:, PC_DT:PROJ_W]

    widths = (Q_LORA, KV_LORA, HEADW, SSM_WIDTH, CONV_DIM, LANE)
    return pl.pallas_call(
        body, name="norm_in_proj", grid=(lp // ROWB,),
        in_specs=[_fullspec((ROWB, D_MODEL)), _xspec(), _fullspec((1, D_MODEL)), _fullspec((D_MODEL, PROJ_W))],
        out_specs=[_rowspec(ROWB, D_MODEL)] + [_rowspec(ROWB, w) for w in widths],
        out_shape=[_sds((lp, D_MODEL), BF16)] + [_sds((lp, w), F32) for w in widths],
        compiler_params=_cp(("arbitrary",)),
    )(head, x, g, w_all)


def _rope(x, cos, sa, sb):
    w = x.shape[1]
    return x * cos + pltpu.roll(x, w - 32, 1) * sa + pltpu.roll(x, 32, 1) * sb


def _rope_t(dy, cos, sa, sb):
    w = dy.shape[1]
    return dy * cos + pltpu.roll(dy * sa, 32, 1) + pltpu.roll(dy * sb, w - 32, 1)


def _qkv(cq, ckv, kr, cos, sa, sb, gq, gkv, wq, wk, wv):
    lp = cq.shape[0]
    qw = ATT_HEADS * HEADW
    half = HEADW // 2
    assert half == QK_NOPE == V_HEAD == LANE

    def body(cq_ref, ckv_ref, kr_ref, cos_ref, sa_ref, sb_ref, gq_ref, gkv_ref, wq_ref, wk_ref, wv_ref,
             q_ref, k_ref, v_ref, cqn_ref, ckvn_ref):
        tabs = [cos_ref[...], sa_ref[...], sb_ref[...]]
        cqn = _rms_fwd(cq_ref[...], gq_ref[...]).astype(BF16)
        ckvn = _rms_fwd(ckv_ref[...], gkv_ref[...]).astype(BF16)
        cqn_ref[...] = cqn
        ckvn_ref[...] = ckvn
        q = _dot(cqn, wq_ref[...])
        kn = _dot(ckvn, wk_ref[...])
        vv = _dot(ckvn, wv_ref[...])
        krope = _rope(kr_ref[:, pl.ds(half, half)], *tabs).astype(BF16)
        ones = jnp.ones((ROWB, half), BF16)
        for hh in range(ATT_HEADS):
            lo, hi, src = pl.ds(hh * HEADW, half), pl.ds(hh * HEADW + half, half), slice(hh * half, (hh + 1) * half)
            q_ref[:, lo] = (q[:, hh * HEADW:hh * HEADW + half] * Q_PRESCALE).astype(BF16)
            q_ref[:, hi] = (_rope(q[:, hh * HEADW + half:(hh + 1) * HEADW], *tabs) * Q_PRESCALE).astype(BF16)
            k_ref[:, lo] = kn[:, src].astype(BF16)
            k_ref[:, hi] = krope
            v_ref[:, lo] = vv[:, src].astype(BF16)
            v_ref[:, hi] = ones

    return pl.pallas_call(
        body, name="qkv", grid=(lp // ROWB,),
        in_specs=[_rowspec(ROWB, Q_LORA), _rowspec(ROWB, KV_LORA), _rowspec(ROWB, HEADW)]
        + [_rowspec(ROWB, half)] * 3
        + [_fullspec((1, Q_LORA)), _fullspec((1, KV_LORA)), _fullspec((Q_LORA, qw)),
           _fullspec((KV_LORA, ATT_HEADS * QK_NOPE)), _fullspec((KV_LORA, ATT_HEADS * V_HEAD))],
        out_specs=[_rowspec(ROWB, qw), _rowspec(ROWB, qw), _rowspec(ROWB, qw),
                   _rowspec(ROWB, Q_LORA), _rowspec(ROWB, KV_LORA)],
        out_shape=[_sds((lp, qw), BF16), _sds((lp, qw), BF16), _sds((lp, qw), BF16),
                   _sds((lp, Q_LORA), BF16), _sds((lp, KV_LORA), BF16)],
        compiler_params=_cp(("arbitrary",)),
    )(cq, ckv, kr, cos, sa, sb, gq, gkv, wq, wk, wv)


ATT_SCALE = (QK_NOPE + QK_ROPE) ** -0.5
LOG2E = 1.4426950408889634
LN2 = 0.6931471805599453
Q_PRESCALE = ATT_SCALE * LOG2E
KVB = 512
META_KEYS = LANE
assert N_META <= META_KEYS


def _att_ok(qrow, krow):
    return (krow <= qrow) & ((krow >= PADF) | (qrow < PADF))


def _lanes(x, n):
    return x if n == 1 else jnp.concatenate([x] * n, axis=1)


def _pair_loop(lo, hi, tile, unrolls=(2,)):
    for u in tuple(unrolls) + (1,):
        n = jnp.maximum(hi - lo, 0)
        trips = n // u

        def many(t, c, u=u, lo=lo):
            for d in range(u):
                tile(lo + u * t + d)
            return c

        lax.fori_loop(0, trips, many, 0)
        lo = lo + trips * u


def _flash_fwd(q, k, v):
    lp = q.shape[0]
    nq = lp // ROWB

    def body(q_ref, k_ref, v_ref, o_ref, lse_ref, acc, m_s):
        i = pl.program_id(1)
        qb = q_ref[...]
        m_s[...] = jnp.full_like(m_s, NEG)
        acc[...] = jnp.zeros_like(acc)

        def tile(j, masked, off=None, nkeys=KVB):
            off = pl.multiple_of(j * KVB, KVB) if off is None else off
            kb = k_ref[pl.ds(off, nkeys), :]
            vb = v_ref[pl.ds(off, nkeys), :]
            s = _dot(qb, kb, NT)
            if masked:
                qrow = i * ROWB + lax.broadcasted_iota(jnp.int32, s.shape, 0)
                krow = off + lax.broadcasted_iota(jnp.int32, s.shape, 1)
                s = jnp.where(_att_ok(qrow, krow), s, NEG)
            m_prev = m_s[...]
            m_new = jnp.maximum(m_prev, jnp.max(s, axis=1, keepdims=True))
            alpha = jnp.exp2(m_prev - m_new)
            p = jnp.exp2(s - _lanes(m_new, nkeys // LANE))
            acc[...] = _lanes(alpha, 2) * acc[...] + _dot(p.astype(BF16), vb)
            m_s[...] = m_new

        def first_tile():
            tile(0, True, off=ROWB - META_KEYS, nkeys=META_KEYS)

        @pl.when(i == 0)
        def _():
            first_tile()

        odd = jnp.bitwise_and(jnp.maximum(i - 1, 0), 7)
        for r in range(8):
            @pl.when((i > 0) & (odd == r))
            def _(r=r):
                first_tile()
                tile(i, True)
                for d in range(r):
                    tile(1 + d, False)

        _pair_loop(1 + odd, i, lambda j: tile(j, False), (16, 8))
        l = acc[:, V_HEAD:]
        o_ref[...] = (acc[:, :V_HEAD] / l).astype(BF16)
        lse_ref[0] = (m_s[...] + jnp.log2(l)).T[0:1, :]

    return pl.pallas_call(
        body, name="flash_fwd", grid=(ATT_HEADS, nq),
        in_specs=[pl.BlockSpec((ROWB, HEADW), lambda h, i: (i, h)),
                  pl.BlockSpec((lp, HEADW), lambda h, i: (0, h)),
                  pl.BlockSpec((lp, HEADW), lambda h, i: (0, h))],
        out_specs=[pl.BlockSpec((ROWB, V_HEAD), lambda h, i: (i, h)),
                   pl.BlockSpec((1, 1, ROWB), lambda h, i: (h, 0, i))],
        out_shape=[_sds((lp, ATT_HEADS * V_HEAD), BF16), _sds((ATT_HEADS, 1, lp), F32)],
        scratch_shapes=[pltpu.VMEM((ROWB, HEADW), F32), pltpu.VMEM((ROWB, LANE), F32)],
        compiler_params=_cp(("arbitrary", "arbitrary")),
    )(q, k, v)


def _silu(x):
    return x * _sigmoid(x)


CONV_ROWS = 64


def _shifted_rows(ref, start, n, cols, offsets):
    win = ref[pl.ds(start, n + 8), cols]
    return [win[o:o + n] if o % 8 == 0 else pltpu.roll(win, n + 8 - o, 0)[0:n] for o in offsets]


def _conv_fwd(xbc, cw, cb):
    lp, c = xbc.shape
    t8 = ROWB // 8

    def body(x_ref, prev_ref, w_ref, b_ref, o_ref, buf):
        i = pl.program_id(0)
        buf[pl.ds(0, 8), :] = jnp.where(i > 0, prev_ref[...], 0.0)
        buf[pl.ds(8, ROWB), :] = x_ref[...]

        def strip(s, carry):
            cs = pl.ds(pl.multiple_of(s * LANE, LANE), LANE)
            w, b = w_ref[:, cs], b_ref[:, cs]
            for r0 in range(0, ROWB, CONV_ROWS):
                taps = _shifted_rows(buf, r0, CONV_ROWS, cs, [8 - (CONV_K - 1) + kk for kk in range(CONV_K)])
                pre = b + sum(w[kk:kk + 1, :] * taps[kk] for kk in range(CONV_K))
                o_ref[pl.ds(r0, CONV_ROWS), cs] = _silu(pre)
            return carry

        lax.fori_loop(0, c // LANE, strip, 0)

    return pl.pallas_call(
        body, name="conv_fwd", grid=(lp // ROWB,),
        in_specs=[_rowspec(ROWB, c), pl.BlockSpec((8, c), lambda i: (jnp.maximum(i * t8 - 1, 0), 0)),
                  _fullspec((8, c)), _fullspec((1, c))],
        out_specs=_rowspec(ROWB, c), out_shape=_sds((lp, c), F32),
        scratch_shapes=[pltpu.VMEM((ROWB + 8, c), F32)],
        compiler_params=_cp(("arbitrary",)),
    )(xbc, xbc, cw, cb)


def _expand_mat():
    r = np.arange(LANE)[:, None]
    c = np.arange(SSM_WIDTH)[None, :]
    return jnp.asarray((c // SSM_HEAD_DIM == r).astype(np.float32))


def _tri_mat():
    i = np.arange(CHUNK)
    return jnp.asarray((i[:, None] >= i[None, :]).astype(np.float32))


def _x3(m, axis):
    return jnp.concatenate([m.astype(BF16)] * 3, axis=axis)


def _split3(x):
    hi = x.astype(BF16)
    r = x - hi.astype(F32)
    mid = r.astype(BF16)
    return hi, mid, (r - mid.astype(F32)).astype(BF16)


def _dot01_r(x, m3):
    return _dot(jnp.concatenate(_split3(x), axis=1), m3)


def _dot01_l(m3, x):
    return _dot(m3, jnp.concatenate(_split3(x), axis=0))


def _ssd_prep(dt_raw, bias_ref, alog_ref, tri3, c, seq_rows):
    rows = c * CHUNK + lax.broadcasted_iota(jnp.int32, (CHUNK, LANE), 0)
    lanes = lax.broadcasted_iota(jnp.int32, (CHUNK, LANE), 1)
    valid = (rows >= PADF) & (rows < PADF + seq_rows) & (lanes < SSM_HEADS)
    dtr = dt_raw + bias_ref[...]
    sp = jnp.maximum(dtr, 0.0) + jnp.log(1.0 + jnp.exp(-jnp.abs(dtr)))
    dt = jnp.where(valid, sp, 0.0)
    a = -jnp.exp(alog_ref[...])
    acol = _dot01_l(tri3, dt * a)
    return dt, a, acol, valid, dtr


def _row16(v):
    return jnp.broadcast_to(v, (16, v.shape[1]))


def _ssd_fwd(xbc_act, dtr, dt_bias, a_log, seq_rows, gather=()):
    lp = xbc_act.shape[0]
    nc = lp // CHUNK
    cps = ROWB // CHUNK
    nsteps = nc // cps
    gw = SSM_WIDTH // SSM_GROUPS
    hpg = SSM_HEADS // SSM_GROUPS

    na = len(gather)

    def body(x_ref, b_ref, c_ref, dtr_ref, bias_ref, alog_ref, tri_ref, tri3_ref, ex3_ref, *rest):
        gin, (y_ref, hp_ref), gout, h_s, sems = rest[:na], rest[na:na + 2], rest[na + 2:2 * na + 2], rest[2 * na + 2], rest[2 * na + 3:]
        step = pl.program_id(0)

        @pl.when(step == 0)
        def _():
            h_s[...] = jnp.zeros_like(h_s)

        if na:
            g_start, g_forward, g_finish = _gather_ops(gin, gout, *sems)
            pl.when(step == 0)(g_start)
            pl.when(step == nsteps // 2)(g_forward)

        ex3 = ex3_ref[...]
        causal = tri_ref[...] > 0.5
        for cc in range(cps):
            rows = pl.ds(cc * CHUNK, CHUNK)
            dt, a, acol, _, _ = _ssd_prep(dtr_ref[rows, :], bias_ref, alog_ref, tri3_ref[...], step * cps + cc, seq_rows)
            arow = acol.T
            dtrow = dt.T
            alast = acol[CHUNK - 1:CHUNK, :]
            e_all = _dot01_r(jnp.exp(acol), ex3)
            wx_all = _dot01_r(jnp.exp(alast - acol) * dt, ex3)
            dec_all = _dot01_r(_row16(jnp.exp(alast)), ex3)[0:1, :]
            hp_ref[cc] = h_s[...]
            for g in range(SSM_GROUPS):
                gs = slice(g * gw, (g + 1) * gw)
                bg = b_ref[rows, g * SSM_STATE:(g + 1) * SSM_STATE]
                cg = c_ref[rows, g * SSM_STATE:(g + 1) * SSM_STATE].astype(BF16)
                xg = x_ref[rows, gs]
                hg = h_s[:, gs]
                gm = _bdot(cg, bg, NT)
                y_off = _bdot(cg, hg) * e_all[:, gs]
                for r in range(hpg):
                    hd = g * hpg + r
                    seg = acol[:, hd:hd + 1] - arow[hd:hd + 1, :]
                    lm = jnp.where(causal, jnp.exp(jnp.where(causal, seg, 0.0)), 0.0)
                    w = gm * lm * dtrow[hd:hd + 1, :]
                    cs = slice(r * SSM_HEAD_DIM, (r + 1) * SSM_HEAD_DIM)
                    y_ref[rows, pl.ds(hd * SSM_HEAD_DIM, SSM_HEAD_DIM)] = _bdot(w, xg[:, cs]) + y_off[:, cs]
                st = _bdot(bg.T, xg * wx_all[:, gs])
                h_s[:, gs] = hg * dec_all[:, gs] + st

        if na:
            pl.when(step == nsteps - 1)(g_finish)

    xs_spec = pl.BlockSpec((ROWB, SSM_WIDTH), lambda c: (c, 0))
    b_spec = pl.BlockSpec((ROWB, 2 * SSM_STATE), lambda c: (c, SSM_WIDTH // (2 * SSM_STATE)))
    c_spec = pl.BlockSpec((ROWB, 2 * SSM_STATE), lambda c: (c, SSM_WIDTH // (2 * SSM_STATE) + 1))
    y, hprev, *gathered = pl.pallas_call(
        body, name="ssd_fwd", grid=(nsteps,),
        in_specs=[xs_spec, b_spec, c_spec, pl.BlockSpec((ROWB, LANE), lambda c: (c, 0)),
                  _fullspec((1, LANE)), _fullspec((1, LANE)), _fullspec((CHUNK, CHUNK)), _fullspec((CHUNK, 3 * CHUNK)),
                  _fullspec((3 * LANE, SSM_WIDTH))] + [ANY] * na,
        out_specs=[xs_spec, pl.BlockSpec((cps, SSM_STATE, SSM_WIDTH), lambda c: (c, 0, 0))] + [ANY] * na,
        out_shape=[_sds((lp, SSM_WIDTH), F32), _sds((nc, SSM_STATE, SSM_WIDTH), F32)]
        + [_sds((N_DEV,) + s.shape, s.dtype) for s in gather],
        scratch_shapes=[pltpu.VMEM((SSM_STATE, SSM_WIDTH), F32)] + (_gather_scratch(na) if na else []),
        compiler_params=_cp(("arbitrary",)),
    )(xbc_act, xbc_act, xbc_act, dtr, dt_bias, a_log, _tri_mat(), _x3(_tri_mat(), 1), _x3(_expand_mat(), 0), *gather)
    return y, hprev, gathered


def _group_mean(x):
    gw = SSM_WIDTH // SSM_GROUPS
    parts = [jnp.broadcast_to(jnp.mean(x[:, g * gw:(g + 1) * gw], axis=-1, keepdims=True), (x.shape[0], gw))
             for g in range(SSM_GROUPS)]
    return jnp.concatenate(parts, axis=1)


def _ssd_post(y, xbc_act, z, dskip, gnorm):
    lp = y.shape[0]

    def body(y_ref, x_ref, z_ref, d_ref, g_ref, o_ref):
        z_ = z_ref[...]
        gt = (y_ref[...] + d_ref[...] * x_ref[...]) * _silu(z_)
        r = lax.rsqrt(_group_mean(gt * gt) + EPS)
        o_ref[...] = ((gt * r) * g_ref[...]).astype(BF16)

    return pl.pallas_call(
        body, name="ssd_post", grid=(lp // ROWB,),
        in_specs=[_rowspec(ROWB, SSM_WIDTH)] * 3 + [_fullspec((1, SSM_WIDTH))] * 2,
        out_specs=_rowspec(ROWB, SSM_WIDTH), out_shape=_sds((lp, SSM_WIDTH), BF16),
        compiler_params=_cp(("arbitrary",)),
    )(y, xbc_act, z, dskip, gnorm)


def _out_proj(att, ssm, head, x, w_out, g_post):
    lp = att.shape[0]

    def body(a_ref, s_ref, head_ref, x_ref, w_ref, g_ref, mix_ref, h1_ref):
        mix = _dot(a_ref[...], w_ref[pl.ds(0, 1024), :]) + _dot(s_ref[...], w_ref[pl.ds(1024, 1024), :])
        mix_ref[...] = mix
        h1_ref[...] = _h_block(head_ref, x_ref) + _rms_fwd(mix, g_ref[...])

    return pl.pallas_call(
        body, name="out_proj", grid=(lp // ROWB,),
        in_specs=[_rowspec(ROWB, 1024)] * 2 + [_fullspec((ROWB, D_MODEL)), _xspec(), _fullspec((2048, D_MODEL)),
                                               _fullspec((1, D_MODEL))],
        out_specs=[_rowspec(ROWB, D_MODEL)] * 2, out_shape=[_sds((lp, D_MODEL), F32)] * 2,
        compiler_params=_cp(("arbitrary",)),
    )(att, ssm, head, x, w_out, g_post)


def _resident(w_hbm, w_vmem, sem):
    @pl.when(pl.program_id(0) == 0)
    def _():
        cp = pltpu.make_async_copy(w_hbm, w_vmem, sem)
        cp.start()
        cp.wait()


ANY = pl.BlockSpec(memory_space=pl.ANY)


def _mlp_fwd(h1, tgt, w_up, w_down, g_pre, g_post, seq_rows):
    lp = h1.shape[0]

    def body(h1_ref, t_ref, wu_hbm, wd_hbm, gpre_ref, gpost_ref, hn2_ref, u_ref, a_ref, f_ref, dh2_ref, loss_ref,
             wu, wd, sems):
        _resident(wu_hbm, wu, sems.at[0])
        _resident(wd_hbm, wd, sems.at[1])
        i = pl.program_id(0)
        h1_ = h1_ref[...]
        hn2 = _rms_fwd(h1_, gpre_ref[...]).astype(BF16)
        hn2_ref[...] = hn2
        u = jnp.maximum(_dot(hn2, wu[...]), 0.0)
        u_ref[...] = u.astype(BF16)
        a = (u * u).astype(BF16)
        a_ref[...] = a
        f = _dot(a, wd[...])
        f_ref[...] = f
        h2 = h1_ + _rms_fwd(f, gpost_ref[...])
        rows = i * ROWB + lax.broadcasted_iota(jnp.int32, (ROWB, 1), 0)
        real = (rows >= PADF + N_META) & (rows < PADF + seq_rows)
        err = jnp.where(real, h2 - t_ref[...], 0.0)
        dh2_ref[...] = err * (1.0 / D_MODEL)
        _acc(loss_ref, _colsum8(err * err))

    return pl.pallas_call(
        body, name="mlp_fwd", grid=(lp // ROWB,),
        in_specs=[_rowspec(ROWB, D_MODEL), _xspec()] + [ANY, ANY] + [_fullspec((1, D_MODEL))] * 2,
        out_specs=[_rowspec(ROWB, D_MODEL), _rowspec(ROWB, D_FF), _rowspec(ROWB, D_FF)] + [_rowspec(ROWB, D_MODEL)] * 2
        + [_fullspec((8, D_MODEL))],
        out_shape=[_sds((lp, D_MODEL), BF16), _sds((lp, D_FF), BF16), _sds((lp, D_FF), BF16), _sds((lp, D_MODEL), F32),
                   _sds((lp, D_MODEL), F32), _sds((8, D_MODEL), F32)],
        scratch_shapes=[pltpu.VMEM((D_MODEL, D_FF), BF16), pltpu.VMEM((D_FF, D_MODEL), BF16), pltpu.SemaphoreType.DMA((2,))],
        compiler_params=_cp(("arbitrary",)),
    )(h1, tgt, w_up, w_down, g_pre, g_post)


def _pad_cols(w, width):
    return jnp.pad(w, ((0, 0), (0, width - w.shape[1])))


def _layout_weights(w_in, w_q_up, w_kv_up):
    o = np.cumsum((0,) + IN_SPLITS)
    pieces = [w_in[:, o[k]:o[k + 1]] for k in range(6)]
    kr = jnp.pad(pieces[2], ((0, 0), (QK_NOPE, HEADW - QK_NOPE - QK_ROPE)))
    w_all = jnp.concatenate([pieces[0], pieces[1], kr, pieces[3], pieces[4], _pad_cols(pieces[5], LANE)], axis=1)
    wq = jnp.pad(w_q_up.reshape(Q_LORA, ATT_HEADS, QK_NOPE + QK_ROPE), ((0, 0), (0, 0), (0, HEADW - QK_NOPE - QK_ROPE)))
    wkv = w_kv_up.reshape(KV_LORA, ATT_HEADS, QK_NOPE + V_HEAD)
    wk = jnp.pad(wkv[:, :, :QK_NOPE], ((0, 0), (0, 0), (0, HEADW - QK_NOPE)))
    wv = wkv[:, :, QK_NOPE:]
    return (w_all, wq.reshape(Q_LORA, -1), wk.reshape(KV_LORA, -1), wv.reshape(KV_LORA, -1),
            wkv[:, :, :QK_NOPE].reshape(KV_LORA, -1))


def _rope_tables(lp):
    pos = jnp.maximum(jnp.arange(lp, dtype=jnp.int32) - PADF, 0).astype(F32)
    inv_freq = ROPE_THETA ** (-jnp.arange(0, QK_ROPE, 2, dtype=F32) / QK_ROPE)
    ang = pos[:, None] * inv_freq[None, :]
    cos, sin = jnp.cos(ang), jnp.sin(ang)
    z32, z64 = jnp.zeros((lp, 32), F32), jnp.zeros((lp, 64), F32)
    cos_t = jnp.concatenate([cos, cos, jnp.ones((lp, 64), F32)], axis=1)
    sa = jnp.concatenate([-sin, z32, z64], axis=1)
    sb = jnp.concatenate([z32, sin, z64], axis=1)
    return cos_t, sa, sb


def _row1(v, width=None):
    v = v.reshape(1, -1).astype(F32)
    return v if width is None else _pad_cols(v, width)


LATE = ("w_out", "w_mlp_up", "w_mlp_down")
MID = ("w_in", "w_q_up", "w_kv_up", "conv_w")


def _local_forward(head, x, tgt, p, late=None):
    assert head.shape[0] == ROWB and x.shape[0] % ROWB == 0
    lp = ROWB + x.shape[0]
    seq_rows = N_META + x.shape[0]
    f = {"seq_rows": seq_rows}
    w_all, wq, wk, wv, wkn = _layout_weights(p["w_in"], p["w_q_up"], p["w_kv_up"])
    f.update(w_all=w_all, wq=wq, wk=wk, wv=wv)
    f["hn"], cq, ckv, kr, f["z"], f["xbc"], f["dtr"] = _norm_in_proj(head, x, _row1(p["norm_mix_pre"]), w_all)
    f.update(cq=cq, ckv=ckv)
    f["rope"] = _rope_tables(lp)
    f["q"], f["k"], f["v"], f["cqn"], f["ckvn"] = _qkv(cq, ckv, kr, *f["rope"], _row1(p["q_a_norm"]),
                                                   _row1(p["kv_a_norm"]), wq, wkn, wv)
    f["att"], f["lse"] = _flash_fwd(f["q"], f["k"], f["v"])
    f["cw"] = jnp.pad(p["conv_w"].astype(F32), ((0, 8 - CONV_K), (0, 0)))
    f["xact"] = _conv_fwd(f["xbc"], f["cw"], _row1(p["conv_b"]))
    f["dt_bias"], f["a_log"] = _row1(p["dt_bias"], LANE), _row1(p["a_log"], LANE)
    f["y"], f["hprev"], gathered = _ssd_fwd(f["xact"], f["dtr"], f["dt_bias"], f["a_log"], seq_rows,
                                            gather=[late[n] for n in LATE] if late else ())
    p = {**p, **{n: _from_shards(n, s) for n, s in zip(LATE, gathered)}}
    f["p"] = p
    f["dskip"] = jnp.repeat(p["d_skip"].reshape(-1).astype(F32), SSM_HEAD_DIM).reshape(1, SSM_WIDTH)
    f["ssm"] = _ssd_post(f["y"], f["xact"], f["z"], f["dskip"], _row1(p["ssm_norm"]))
    f["mix"], f["h1"] = _out_proj(f["att"], f["ssm"], head, x, p["w_out"], _row1(p["norm_mix_post"]))
    f["hn2"], f["u"], f["a"], f["f"], f["dh2"], loss8 = _mlp_fwd(
        f["h1"], tgt, p["w_mlp_up"], p["w_mlp_down"], _row1(p["norm_mlp_pre"]), _row1(p["norm_mlp_post"]), seq_rows)
    f["loss"] = 0.5 * jnp.sum(loss8) / D_MODEL
    return f


MLPB = 256


def _mlp_bwd(dh2, f, h1, u, w_up, w_down, g_pre, g_post):
    lp = h1.shape[0]

    def body(dh2_ref, f_ref, h1_ref, u_ref, wu_hbm, wd_hbm, gpre_ref, gpost_ref,
             dh1_ref, du_ref, df_ref, dgpre_ref, dgpost_ref, wu, wd, sems):
        _resident(wu_hbm, wu, sems.at[0])
        _resident(wd_hbm, wd, sems.at[1])
        dh2_ = dh2_ref[...]
        df, dgp = _rms_bwd(f_ref[...], gpost_ref[...], dh2_)
        dfb = df.astype(BF16)
        df_ref[...] = dfb
        da = _dot(dfb, wd[...], NT)
        du = (da * (2.0 * u_ref[...].astype(F32))).astype(BF16)
        du_ref[...] = du
        dhn2 = _dot(du, wu[...], NT)
        dx, dgq = _rms_bwd(h1_ref[...], gpre_ref[...], dhn2)
        dh1_ref[...] = dh2_ + dx
        _acc(dgpre_ref, _colsum8(dgq))
        _acc(dgpost_ref, _colsum8(dgp))

    return pl.pallas_call(
        body, name="mlp_bwd", grid=(lp // MLPB,),
        in_specs=[_rowspec(MLPB, D_MODEL)] * 3 + [_rowspec(MLPB, D_FF), ANY, ANY] + [_fullspec((1, D_MODEL))] * 2,
        out_specs=[_rowspec(MLPB, D_MODEL), _rowspec(MLPB, D_FF), _rowspec(MLPB, D_MODEL),
                   _fullspec((8, D_MODEL)), _fullspec((8, D_MODEL))],
        out_shape=[_sds((lp, D_MODEL), F32), _sds((lp, D_FF), BF16), _sds((lp, D_MODEL), BF16),
                   _sds((8, D_MODEL), F32), _sds((8, D_MODEL), F32)],
        scratch_shapes=[pltpu.VMEM((D_MODEL, D_FF), BF16), pltpu.VMEM((D_FF, D_MODEL), BF16), pltpu.SemaphoreType.DMA((2,))],
        compiler_params=_cp(("arbitrary",)),
    )(dh2, f, h1, u, w_up, w_down, g_pre, g_post)


def _out_bwd(dh1, mix, att, w_out, g_post):
    lp = dh1.shape[0]

    def body(dh1_ref, mix_ref, att_ref, w_ref, g_ref, dmix_ref, datt_ref, dssm_ref, dg_ref, dl_ref):
        dmix, dg = _rms_bwd(mix_ref[...], g_ref[...], dh1_ref[...])
        dmb = dmix.astype(BF16)
        dmix_ref[...] = dmb
        datt = _dot(dmb, w_ref[pl.ds(0, 1024), :], NT).astype(BF16)
        datt_ref[...] = datt
        dssm_ref[...] = _dot(dmb, w_ref[pl.ds(1024, 1024), :], NT)
        _acc(dg_ref, _colsum8(dg))
        prod = datt.astype(F32) * att_ref[...].astype(F32)
        for hh in range(ATT_HEADS):
            d = jnp.sum(prod[:, hh * V_HEAD:(hh + 1) * V_HEAD], axis=1, keepdims=True)
            dl_ref[hh] = jnp.broadcast_to(d, (ROWB, LANE)).T[0:1, :]

    return pl.pallas_call(
        body, name="out_bwd", grid=(lp // ROWB,),
        in_specs=[_rowspec(ROWB, D_MODEL)] * 3 + [_fullspec((2048, D_MODEL)), _fullspec((1, D_MODEL))],
        out_specs=[_rowspec(ROWB, D_MODEL)] * 3 + [_fullspec((8, D_MODEL)),
                                                   pl.BlockSpec((ATT_HEADS, 1, ROWB), lambda i: (0, 0, i))],
        out_shape=[_sds((lp, D_MODEL), BF16), _sds((lp, 1024), BF16), _sds((lp, 1024), F32), _sds((8, D_MODEL), F32),
                   _sds((ATT_HEADS, 1, lp), F32)],
        compiler_params=_cp(("arbitrary",)),
    )(dh1, mix, att, w_out, g_post)


def _ssd_post_bwd(dssm, y, xact, z, dskip, gnorm):
    lp = y.shape[0]

    def body(do_ref, y_ref, x_ref, z_ref, d_ref, g_ref, dy_ref, dz_ref, dg_ref, dd_ref):
        z_, x_ = z_ref[...], x_ref[...]
        sg = _sigmoid(z_)
        sz = z_ * sg
        y2 = y_ref[...] + d_ref[...] * x_
        gt = y2 * sz
        r = lax.rsqrt(_group_mean(gt * gt) + EPS)
        gh = gt * r
        do = do_ref[...]
        dgh = do * g_ref[...]
        dgt = r * (dgh - gh * _group_mean(dgh * gh))
        dy2 = dgt * sz
        dy_ref[...] = dy2
        dz_ref[...] = (dgt * y2 * (sg * (1.0 + z_ * (1.0 - sg)))).astype(BF16)
        _acc(dg_ref, _colsum8(do * gh))
        _acc(dd_ref, _colsum8(dy2 * x_))

    return pl.pallas_call(
        body, name="ssd_post_bwd", grid=(lp // ROWB,),
        in_specs=[_rowspec(ROWB, SSM_WIDTH)] * 4 + [_fullspec((1, SSM_WIDTH))] * 2,
        out_specs=[_rowspec(ROWB, SSM_WIDTH)] * 2 + [_fullspec((8, SSM_WIDTH))] * 2,
        out_shape=[_sds((lp, SSM_WIDTH), F32), _sds((lp, SSM_WIDTH), BF16), _sds((8, SSM_WIDTH), F32), _sds((8, SSM_WIDTH), F32)],
        compiler_params=_cp(("arbitrary",)),
    )(dssm, y, xact, z, dskip, gnorm)


def _ssd_bwd(dy, xact, dtr, hprev, dt_bias, a_log, dskip, seq_rows, exchange=()):
    lp = xact.shape[0]
    nc = lp // CHUNK
    gw = SSM_WIDTH // SSM_GROUPS
    hpg = SSM_HEADS // SSM_GROUPS
    nb = SSM_WIDTH // (2 * SSM_STATE)
    na = len(exchange)
    cps = ROWB // CHUNK
    nsteps = nc // cps

    def body(dy_ref, x_ref, b_ref, c_ref, dtr_ref, hp_ref, bias_ref, alog_ref, dsk_ref, tri_ref, tri3_ref, trit3_ref,
             ex3_ref, ext3_ref, *rest):
        xin, (dact_ref, ddtr_ref, da_ref, dbias_ref), xout = rest[:na], rest[na:na + 4], rest[na + 4:2 * na + 4]
        dh_s, sems = rest[2 * na + 4], rest[2 * na + 5:]
        step = pl.program_id(0)

        @pl.when(step == 0)
        def _():
            dh_s[...] = jnp.zeros_like(dh_s)
            da_ref[...] = jnp.zeros_like(da_ref)
            dbias_ref[...] = jnp.zeros_like(dbias_ref)

        if na:
            x_start, x_finish = _exchange_ops(xin, xout, *sems)
            pl.when(step == 0)(x_start)

        for lc in reversed(range(cps)):
            rows = pl.ds(lc * CHUNK, CHUNK)
            chunk((nsteps - 1 - step) * cps + lc, dy_ref.at[rows], x_ref.at[rows], b_ref.at[rows], c_ref.at[rows],
                  dtr_ref.at[rows], hp_ref.at[pl.ds(lc, 1)], bias_ref, alog_ref, dsk_ref, tri_ref, tri3_ref, trit3_ref,
                  ex3_ref, ext3_ref, dact_ref.at[rows], ddtr_ref.at[rows], da_ref, dbias_ref, dh_s)

        if na:
            pl.when(step == nsteps - 1)(x_finish)

    def chunk(c, dy_ref, x_ref, b_ref, c_ref, dtr_ref, hp_ref, bias_ref, alog_ref, dsk_ref, tri_ref, tri3_ref, trit3_ref,
              ex3_ref, ext3_ref, dact_ref, ddtr_ref, da_ref, dbias_ref, dh_s):
        tri = tri_ref[...]
        ex3 = ex3_ref[...]
        dt, a, acol, valid, dtr_ = _ssd_prep(dtr_ref[...], bias_ref, alog_ref, tri3_ref[...], c, seq_rows)
        arow = acol.T
        dtrow = dt.T
        alast = acol[CHUNK - 1:CHUNK, :]
        e_all = _dot01_r(jnp.exp(acol), ex3)
        wgt0 = jnp.exp(alast - acol)
        wgt = wgt0 * dt
        wx_all = _dot01_r(wgt, ex3)
        elast = jnp.exp(alast)
        dec_all = _dot01_r(_row16(elast), ex3)[0:1, :]
        causal = tri > 0.5
        upper = tri.T > 0.5
        lane_id = lax.broadcasted_iota(jnp.int32, (1, LANE), 1)
        sub_id = lax.broadcasted_iota(jnp.int32, (CHUNK, 1), 0)
        dacol = jnp.zeros((CHUNK, LANE), F32)
        darowf = jnp.zeros((CHUNK, LANE), F32)
        ddtrowf = jnp.zeros((CHUNK, LANE), F32)
        dwgt = jnp.zeros((CHUNK, LANE), F32)
        delast = jnp.zeros((1, LANE), F32)
        for g in range(SSM_GROUPS):
            gs = slice(g * gw, (g + 1) * gw)
            ext3_g = ext3_ref[g]
            bg = b_ref[:, g * SSM_STATE:(g + 1) * SSM_STATE]
            cg = c_ref[:, g * SSM_STATE:(g + 1) * SSM_STATE]
            bgb, cgb = bg.astype(BF16), cg.astype(BF16)
            xg = x_ref[:, gs]
            dyg = dy_ref[:, gs]
            hg = hp_ref[0, :, gs]
            dhg = dh_s[:, gs]
            hgb, dhgb = hg.astype(BF16), dhg.astype(BF16)
            gm = _dot(cgb, bgb, NT)
            gmt = _dot(bgb, cgb, NT)
            y_off = _dot(cgb, hgb) * e_all[:, gs]
            dy0 = (dyg * e_all[:, gs]).astype(BF16)
            dcg = _dot(dy0, hgb, NT)
            dh_in = _dot(cg.T.astype(BF16), dy0) + dhg * dec_all[:, gs]
            dacol = dacol + _dot01_r(dyg * y_off, ext3_g)
            xw = xg * wx_all[:, gs]
            dxw = _dot(bgb, dhgb)
            dx_state = dxw * wx_all[:, gs]
            dwgt = dwgt + _dot01_r(dxw * xg, ext3_g)
            dbt = _dot(dhgb, xw.astype(BF16), NT)
            hh = _colsum8(dhg * hg)
            hh16 = jnp.concatenate([hh, jnp.zeros_like(hh)], axis=0)
            delast = delast + jnp.sum(_dot01_r(hh16, ext3_g), axis=0, keepdims=True)
            dgm = jnp.zeros((CHUNK, CHUNK), F32)
            for r in range(hpg):
                hd = g * hpg + r
                cs = slice(r * SSM_HEAD_DIM, (r + 1) * SSM_HEAD_DIM)
                acol_r, arow_r = acol[:, hd:hd + 1], arow[hd:hd + 1, :]
                dtrow_r, dtcol_r = dtrow[hd:hd + 1, :], dt[:, hd:hd + 1]
                lm = jnp.where(causal, jnp.exp(jnp.where(causal, acol_r - arow_r, 0.0)), 0.0)
                lmt = jnp.where(upper, jnp.exp(jnp.where(upper, arow_r - acol_r, 0.0)), 0.0)
                wt = gmt * lmt * dtcol_r
                dy_r = dyg[:, cs].astype(BF16)
                dx_r = _dot(wt.astype(BF16), dy_r)
                dw = _dot(dy_r, xg[:, cs].astype(BF16), NT)
                t1 = dw * lm
                dgm = dgm + t1 * dtrow_r
                q1 = t1 * gm
                m = q1 * dtrow_r
                dacol = dacol + jnp.sum(m, axis=1, keepdims=True) * (lane_id == hd).astype(F32)
                darowf = darowf - (sub_id == hd).astype(F32) * jnp.sum(m, axis=0, keepdims=True)
                ddtrowf = ddtrowf + (sub_id == hd).astype(F32) * jnp.sum(q1, axis=0, keepdims=True)
                dact_ref[:, pl.ds(hd * SSM_HEAD_DIM, SSM_HEAD_DIM)] = (
                    dx_r + dx_state[:, cs] + dyg[:, cs] * dsk_ref[:, pl.ds(hd * SSM_HEAD_DIM, SSM_HEAD_DIM)])
            dgmb = dgm.astype(BF16)
            dact_ref[:, pl.ds(SSM_WIDTH + g * SSM_STATE, SSM_STATE)] = dbt.T + _dot(dgm.T.astype(BF16), cgb)
            dact_ref[:, pl.ds(SSM_WIDTH + 2 * SSM_STATE + g * SSM_STATE, SSM_STATE)] = dcg + _dot(dgmb, bgb)
            dh_s[:, gs] = dh_in
        t = dwgt * wgt
        dalast = jnp.sum(t, axis=0, keepdims=True) + delast * elast
        dacol_tot = dacol - t + darowf.T + (sub_id == CHUNK - 1).astype(F32) * dalast
        dda = _dot01_l(trit3_ref[...], dacol_tot)
        ddt = dwgt * wgt0 + ddtrowf.T + dda * a
        ddtr = jnp.where(valid, ddt * _sigmoid(dtr_), 0.0)
        ddtr_ref[...] = ddtr
        da_ref[...] += _colsum8(dda * dt) * a
        dbias_ref[...] += _colsum8(ddtr)

    rev = lambda c: nsteps - 1 - c
    rb = cps * CHUNK
    xs_spec = pl.BlockSpec((rb, SSM_WIDTH), lambda c: (rev(c), 0))
    dact, ddtr, da8, dbias8, *received = pl.pallas_call(
        body, name="ssd_bwd", grid=(nsteps,),
        in_specs=[xs_spec, xs_spec,
                  pl.BlockSpec((rb, 2 * SSM_STATE), lambda c: (rev(c), nb)),
                  pl.BlockSpec((rb, 2 * SSM_STATE), lambda c: (rev(c), nb + 1)),
                  pl.BlockSpec((rb, LANE), lambda c: (rev(c), 0)),
                  pl.BlockSpec((cps, SSM_STATE, SSM_WIDTH), lambda c: (rev(c), 0, 0)),
                  _fullspec((1, LANE)), _fullspec((1, LANE)), _fullspec((1, SSM_WIDTH)),
                  _fullspec((CHUNK, CHUNK)), _fullspec((CHUNK, 3 * CHUNK)), _fullspec((CHUNK, 3 * CHUNK)),
                  _fullspec((3 * LANE, SSM_WIDTH)), _fullspec((SSM_GROUPS, 3 * gw, LANE))] + [ANY] * na,
        out_specs=[pl.BlockSpec((rb, CONV_DIM), lambda c: (rev(c), 0)), pl.BlockSpec((rb, LANE), lambda c: (rev(c), 0)),
                   _fullspec((8, LANE)), _fullspec((8, LANE))] + [ANY] * na,
        out_shape=[_sds((lp, CONV_DIM), F32), _sds((lp, LANE), F32), _sds((8, LANE), F32), _sds((8, LANE), F32)]
        + [_sds(e.shape, e.dtype) for e in exchange],
        scratch_shapes=[pltpu.VMEM((SSM_STATE, SSM_WIDTH), F32)] + (_gather_scratch(na) if na else []),
        compiler_params=_cp(("arbitrary",)),
    )(dy, xact, xact, xact, dtr, hprev, dt_bias, a_log, dskip, _tri_mat(), _x3(_tri_mat(), 1), _x3(_tri_mat().T, 1),
      _x3(_expand_mat(), 0), jnp.stack([_x3(_expand_mat().T[g * gw:(g + 1) * gw], 0) for g in range(SSM_GROUPS)]),
      *exchange)
    return dact, ddtr, da8, dbias8, received


def _conv_bwd(dact, xbc, cw, cb):
    lp, c = xbc.shape
    t8 = ROWB // 8
    nb = lp // ROWB

    def body(d_ref, dnext_ref, x_ref, prev_ref, next_ref, w_ref, b_ref, dx_ref, dw_ref, db_ref, xb, dp):
        i = pl.program_id(0)
        last = i == nb - 1
        xb[pl.ds(0, 8), :] = jnp.where(i > 0, prev_ref[...], 0.0)
        xb[pl.ds(8, ROWB), :] = x_ref[...]
        xb[pl.ds(8 + ROWB, 8), :] = jnp.where(last, 0.0, next_ref[...])

        @pl.when(i == 0)
        def _():
            dw_ref[...] = jnp.zeros_like(dw_ref)
            db_ref[...] = jnp.zeros_like(db_ref)

        sub = lax.broadcasted_iota(jnp.int32, (8, 1), 0)
        x0 = 8 - (CONV_K - 1)

        def strip(s, carry):
            cs = pl.ds(pl.multiple_of(s * LANE, LANE), LANE)
            w, b = w_ref[:, cs], b_ref[:, cs]

            def dpre_rows(r0, n, d):
                xs = _shifted_rows(xb, r0, n, cs, [x0 + kk for kk in range(CONV_K)])
                pre = b + sum(w[kk:kk + 1, :] * xs[kk] for kk in range(CONV_K))
                sg = _sigmoid(pre)
                return d * (sg * (1.0 + pre * (1.0 - sg))), xs

            dws = [jnp.zeros((8, LANE), F32) for _ in range(CONV_K)]
            dbs = jnp.zeros((8, LANE), F32)
            for r0 in range(0, ROWB, CONV_ROWS):
                dpre, xs = dpre_rows(r0, CONV_ROWS, d_ref[pl.ds(r0, CONV_ROWS), cs])
                dp[pl.ds(r0, CONV_ROWS), cs] = dpre
                dbs = dbs + _colsum8(dpre)
                for kk in range(CONV_K):
                    dws[kk] = dws[kk] + _colsum8(dpre * xs[kk])
            dp[pl.ds(ROWB, 8), cs] = dpre_rows(ROWB, 8, jnp.where(last, 0.0, dnext_ref[:, cs]))[0]
            dwv = sum(jnp.where(sub == kk, jnp.sum(dws[kk], axis=0, keepdims=True), 0.0) for kk in range(CONV_K))
            dw_ref[:, cs] += dwv
            db_ref[:, cs] += dbs
            for r0 in range(0, ROWB, CONV_ROWS):
                ahead = _shifted_rows(dp, r0, CONV_ROWS, cs, [CONV_K - 1 - kk for kk in range(CONV_K)])
                dx = sum(w[kk:kk + 1, :] * ahead[kk] for kk in range(CONV_K))
                dx_ref[pl.ds(r0, CONV_ROWS), cs] = dx.astype(BF16)
            return carry

        lax.fori_loop(0, c // LANE, strip, 0)

    nxt = lambda i: (jnp.minimum((i + 1) * t8, lp // 8 - 1), 0)
    prv = lambda i: (jnp.maximum(i * t8 - 1, 0), 0)
    return pl.pallas_call(
        body, name="conv_bwd", grid=(nb,),
        in_specs=[_rowspec(ROWB, c), pl.BlockSpec((8, c), nxt), _rowspec(ROWB, c), pl.BlockSpec((8, c), prv),
                  pl.BlockSpec((8, c), nxt), _fullspec((8, c)), _fullspec((1, c))],
        out_specs=[_rowspec(ROWB, c), _fullspec((8, c)), _fullspec((8, c))],
        out_shape=[_sds((lp, c), BF16), _sds((8, c), F32), _sds((8, c), F32)],
        scratch_shapes=[pltpu.VMEM((ROWB + 16, c), F32), pltpu.VMEM((ROWB + 8, c), F32)],
        compiler_params=_cp(("arbitrary",)),
    )(dact, dact, xbc, xbc, xbc, cw, cb)


def _flash_bwd(q, k, v, datt, lse_row, delta_row, cos, sa, sb):
    lp = q.shape[0]
    nk = lp // ROWB

    def body(k_ref, v_ref, q_ref, do_ref, lse_ref, dl_ref, cos_ref, sa_ref, sb_ref, dq_ref, dk_ref, dv_ref,
             dq_acc, dk_acc, dv_acc):
        j = pl.program_id(1)

        @pl.when(j == 0)
        def _():
            dq_acc[...] = jnp.zeros_like(dq_acc)

        dk_acc[...] = jnp.zeros_like(dk_acc)
        dv_acc[...] = jnp.zeros_like(dv_acc)

        def tile(i, masked, key0=0, nkeys=ROWB):
            keys = pl.ds(key0, nkeys)
            kb, vb = k_ref[keys, :], v_ref[keys, :]
            off = pl.multiple_of(i * ROWB, ROWB)
            qb = q_ref[pl.ds(off, ROWB), :]
            dob = do_ref[pl.ds(off, ROWB), :]
            lse_r = lse_ref[0, :, pl.ds(off, ROWB)]
            dl_r = dl_ref[0, :, pl.ds(off, ROWB)]
            st = _dot(kb, qb, NT)
            if masked:
                krow = j * ROWB + key0 + lax.broadcasted_iota(jnp.int32, st.shape, 0)
                qrow = i * ROWB + lax.broadcasted_iota(jnp.int32, st.shape, 1)
                st = jnp.where(_att_ok(qrow, krow), st, NEG)
            pt = jnp.exp2(st - lse_r)
            dv_acc[keys, :] += _dot(pt.astype(BF16), dob)
            dpt = _dot(vb, dob, NT)
            dst = (pt * (dpt - dl_r)).astype(BF16)
            dk_acc[keys, :] += _dot(dst, qb)
            dq_acc[pl.ds(off, ROWB), :] += _dot(dst, kb, ((0,), (0,)))

        @pl.when(j == 0)
        def _():
            _pair_loop(0, nk, lambda i: tile(i, True, ROWB - META_KEYS, META_KEYS), (4, 2))

        odd = jnp.bitwise_and(nk - 1 - j, 3)
        for r in range(4):
            @pl.when((j > 0) & (odd == r))
            def _(r=r):
                tile(j, True)
                for d in range(r):
                    tile(j + 1 + d, False)

        @pl.when(j > 0)
        def _():
            _pair_loop(j + 1 + odd, nk, lambda i: tile(i, False), (4,))

        dk_ref[...] = (dk_acc[...] * LN2).astype(BF16)
        dv_ref[...] = dv_acc[...].astype(BF16)
        dq = dq_acc[pl.ds(pl.multiple_of(j * ROWB, ROWB), ROWB), :] * ATT_SCALE
        half = HEADW // 2
        dq_ref[:, pl.ds(0, half)] = dq[:, :half].astype(BF16)
        dq_ref[:, pl.ds(half, half)] = _rope_t(dq[:, half:], cos_ref[...], sa_ref[...], sb_ref[...]).astype(BF16)

    stat = pl.BlockSpec((1, 1, lp), lambda h, j: (h, 0, 0))
    blk = pl.BlockSpec((ROWB, HEADW), lambda h, j: (j, h))
    tab = pl.BlockSpec((ROWB, HEADW // 2), lambda h, j: (j, 0))
    return pl.pallas_call(
        body, name="flash_bwd", grid=(ATT_HEADS, nk),
        in_specs=[blk, pl.BlockSpec((ROWB, V_HEAD), lambda h, j: (j, 2 * h)),
                  pl.BlockSpec((lp, HEADW), lambda h, j: (0, h)), pl.BlockSpec((lp, V_HEAD), lambda h, j: (0, h)),
                  stat, stat, tab, tab, tab],
        out_specs=[blk, blk, pl.BlockSpec((ROWB, V_HEAD), lambda h, j: (j, h))],
        out_shape=[_sds((lp, ATT_HEADS * HEADW), BF16), _sds((lp, ATT_HEADS * HEADW), BF16),
                   _sds((lp, ATT_HEADS * V_HEAD), BF16)],
        scratch_shapes=[pltpu.VMEM((lp, HEADW), F32), pltpu.VMEM((ROWB, HEADW), F32), pltpu.VMEM((ROWB, V_HEAD), F32)],
        compiler_params=_cp(("arbitrary", "arbitrary")),
    )(k, v, q, datt, lse_row, delta_row, cos, sa, sb)


def _qkv_bwd(dqp, dk, dv, cq, ckv, cos, sa, sb, gq, gkv, wq, wk, wv, dz, dxbc, ddtr):
    lp = cq.shape[0]
    qw = ATT_HEADS * HEADW

    def body(dqp_ref, dk_ref, dv_ref, cq_ref, ckv_ref, cos_ref, sa_ref, sb_ref, gq_ref, gkv_ref, wq_ref, wk_ref, wv_ref,
             dz_ref, dxbc_ref, ddtr_ref, dp_ref, dgq_ref, dgkv_ref):
        dcq, dgq = _rms_bwd(cq_ref[...], gq_ref[...], _dot(dqp_ref[...], wq_ref[...], NT))
        dp_ref[:, pl.ds(PC_Q, Q_LORA)] = dcq.astype(BF16)
        dkb = dk_ref[...]
        half = HEADW // 2
        dksum = sum(dkb[:, hh * HEADW + half:(hh + 1) * HEADW].astype(F32) for hh in range(ATT_HEADS))
        dp_ref[:, pl.ds(PC_KR, half)] = jnp.zeros((ROWB, half), BF16)
        dp_ref[:, pl.ds(PC_KR + half, half)] = _rope_t(dksum, cos_ref[...], sa_ref[...], sb_ref[...]).astype(BF16)
        dckvn = _dot(dkb, wk_ref[...], NT) + _dot(dv_ref[...], wv_ref[...], NT)
        dckv, dgkv = _rms_bwd(ckv_ref[...], gkv_ref[...], dckvn)
        dp_ref[:, pl.ds(PC_KV, KV_LORA)] = dckv.astype(BF16)
        dp_ref[:, pl.ds(PC_Z, SSM_WIDTH)] = dz_ref[...]
        dp_ref[:, pl.ds(PC_XBC, CONV_DIM)] = dxbc_ref[...]
        dp_ref[:, pl.ds(PC_DT, LANE)] = ddtr_ref[...].astype(BF16)
        _acc(dgq_ref, _colsum8(dgq))
        _acc(dgkv_ref, _colsum8(dgkv))

    return pl.pallas_call(
        body, name="qkv_bwd", grid=(lp // ROWB,),
        in_specs=[_rowspec(ROWB, qw), _rowspec(ROWB, qw), _rowspec(ROWB, ATT_HEADS * V_HEAD),
                  _rowspec(ROWB, Q_LORA), _rowspec(ROWB, KV_LORA)] + [_rowspec(ROWB, HEADW // 2)] * 3
        + [_fullspec((1, Q_LORA)), _fullspec((1, KV_LORA)), _fullspec((Q_LORA, qw)), _fullspec((KV_LORA, qw)),
           _fullspec((KV_LORA, ATT_HEADS * V_HEAD)), _rowspec(ROWB, SSM_WIDTH), _rowspec(ROWB, CONV_DIM),
           _rowspec(ROWB, LANE)],
        out_specs=[_rowspec(ROWB, PROJ_W), _fullspec((8, Q_LORA)), _fullspec((8, KV_LORA))],
        out_shape=[_sds((lp, PROJ_W), BF16), _sds((8, Q_LORA), F32), _sds((8, KV_LORA), F32)],
        compiler_params=_cp(("arbitrary",)),
    )(dqp, dk, dv, cq, ckv, cos, sa, sb, gq, gkv, wq, wk, wv, dz, dxbc, ddtr)


def _in_bwd(dproj, head, x, dh1, g, w_all, exchange=()):
    lp = dh1.shape[0]
    nsteps = lp // ROWB
    na = len(exchange)

    def body(dp_ref, head_ref, x_ref, dh1_ref, g_ref, w_ref, *rest):
        xin, (dx_ref, dhead_ref, dg_ref), xout, sems = rest[:na], rest[na:na + 3], rest[na + 3:2 * na + 3], rest[2 * na + 3:]
        step = pl.program_id(0)
        if na:
            x_start, x_finish = _exchange_ops(xin, xout, *sems)
            pl.when(step == 0)(x_start)
        dx, dg = _rms_bwd(_h_block(head_ref, x_ref), g_ref[...], _dot(dp_ref[...], w_ref[...], NT))
        dh = dh1_ref[...] + dx

        @pl.when(step == 0)
        def _():
            dhead_ref[...] = dh

        @pl.when(step > 0)
        def _():
            dx_ref[...] = dh

        _acc(dg_ref, _colsum8(dg))
        if na:
            pl.when(step == nsteps - 1)(x_finish)

    dx, dhead, dg8, *received = pl.pallas_call(
        body, name="in_bwd", grid=(nsteps,),
        in_specs=[_rowspec(ROWB, PROJ_W), _fullspec((ROWB, D_MODEL)), _xspec(), _rowspec(ROWB, D_MODEL),
                  _fullspec((1, D_MODEL)), _fullspec((D_MODEL, PROJ_W))] + [ANY] * na,
        out_specs=[_xspec(), _fullspec((ROWB, D_MODEL)), _fullspec((8, D_MODEL))] + [ANY] * na,
        out_shape=[_sds(x.shape, F32), _sds((ROWB, D_MODEL), F32), _sds((8, D_MODEL), F32)]
        + [_sds(e.shape, e.dtype) for e in exchange],
        scratch_shapes=_gather_scratch(na) if na else [],
        compiler_params=_cp(("arbitrary",)),
    )(dproj, head, x, dh1, g, w_all, *exchange)
    return dx, dhead, dg8, received


def _tile_of(n, cap=1024):
    return max(t for t in range(LANE, min(n, cap) + 1, LANE) if n % t == 0)


def _matmul_tn(name, a, b):
    rows, kd = a.shape
    nd = b.shape[1]
    tk, tn = _tile_of(kd), _tile_of(nd)
    rb = 3 * ROWB if rows % (3 * ROWB) == 0 else ROWB

    def body(a_ref, b_ref, o_ref):
        @pl.when(pl.program_id(2) == 0)
        def _():
            o_ref[...] = jnp.zeros_like(o_ref)

        o_ref[...] += _dot(a_ref[...], b_ref[...], ((0,), (0,)))

    return pl.pallas_call(
        body, name=name, grid=(kd // tk, nd // tn, rows // rb),
        in_specs=[pl.BlockSpec((rb, tk), lambda i, j, r: (r, i)), pl.BlockSpec((rb, tn), lambda i, j, r: (r, j))],
        out_specs=pl.BlockSpec((tk, tn), lambda i, j, r: (i, j)), out_shape=_sds((kd, nd), F32),
        compiler_params=_cp(("arbitrary", "arbitrary", "arbitrary")),
    )(a, b)


def _local_backward(head, x, f, exchange_late=False):
    p = f["p"]
    g = {}
    row = lambda v: _row1(v)
    s8 = lambda v: jnp.sum(v, axis=0)
    dh1, du, df, dgpre, dgpost = _mlp_bwd(f["dh2"], f["f"], f["h1"], f["u"], p["w_mlp_up"], p["w_mlp_down"],
                                          row(p["norm_mlp_pre"]), row(p["norm_mlp_post"]))
    g["norm_mlp_pre"], g["norm_mlp_post"] = s8(dgpre), s8(dgpost)
    g["w_mlp_up"] = _matmul_tn("dw_mlp_up", f["hn2"], du)
    g["w_mlp_down"] = _matmul_tn("dw_mlp_down", f["a"], df)
    dmix, datt, dssm, dgmp, delta = _out_bwd(dh1, f["mix"], f["att"], p["w_out"], row(p["norm_mix_post"]))
    g["norm_mix_post"] = s8(dgmp)
    g["w_out"] = jnp.concatenate([_matmul_tn("dw_out_att", f["att"], dmix), _matmul_tn("dw_out_ssm", f["ssm"], dmix)], axis=0)
    dy, dz, dgn, dd = _ssd_post_bwd(dssm, f["y"], f["xact"], f["z"], f["dskip"], row(p["ssm_norm"]))
    g["ssm_norm"] = s8(dgn)
    g["d_skip"] = s8(dd).reshape(SSM_HEADS, SSM_HEAD_DIM).sum(axis=1)
    dact, ddtr, da8, dbias8, received = _ssd_bwd(
        dy, f["xact"], f["dtr"], f["hprev"], f["dt_bias"], f["a_log"], f["dskip"], f["seq_rows"],
        exchange=[_to_chunks(n, g[n]).astype(BF16) for n in LATE] if exchange_late else ())
    g["a_log"], g["dt_bias"] = s8(da8)[:SSM_HEADS], s8(dbias8)[:SSM_HEADS]
    dxbc, dcw8, dcb8 = _conv_bwd(dact, f["xbc"], f["cw"], row(p["conv_b"]))
    g["conv_w"], g["conv_b"] = dcw8[:CONV_K], s8(dcb8)
    dqp, dkb, dv = _flash_bwd(f["q"], f["k"], f["v"], datt, f["lse"], delta, *f["rope"])
    dproj, dgq, dgkv = _qkv_bwd(dqp, dkb, dv, f["cq"], f["ckv"], *f["rope"], row(p["q_a_norm"]), row(p["kv_a_norm"]),
                                f["wq"], f["wk"], f["wv"], dz, dxbc, ddtr)
    g["q_a_norm"], g["kv_a_norm"] = s8(dgq), s8(dgkv)
    dwq = _matmul_tn("dw_q_up", f["cqn"], dqp).reshape(Q_LORA, ATT_HEADS, HEADW)
    g["w_q_up"] = dwq[:, :, :QK_NOPE + QK_ROPE].reshape(Q_LORA, -1)
    dwk = _matmul_tn("dw_k_up", f["ckvn"], dkb).reshape(KV_LORA, ATT_HEADS, HEADW)[:, :, :QK_NOPE]
    dwv = _matmul_tn("dw_v_up", f["ckvn"], dv).reshape(KV_LORA, ATT_HEADS, V_HEAD)
    g["w_kv_up"] = jnp.concatenate([dwk, dwv], axis=2).reshape(KV_LORA, -1)
    dwa = _matmul_tn("dw_in", f["hn"], dproj)
    g["w_in"] = jnp.concatenate([dwa[:, PC_Q:PC_KR], dwa[:, PC_KR + QK_NOPE:PC_KR + QK_NOPE + QK_ROPE],
                                 dwa[:, PC_Z:PC_DT + SSM_HEADS]], axis=1)
    dx, dhead, dgin, received_mid = _in_bwd(
        dproj, head, x, dh1, row(p["norm_mix_pre"]), f["w_all"],
        exchange=[_to_chunks(n, g[n]).astype(BF16) for n in MID] if exchange_late else ())
    g["norm_mix_pre"] = s8(dgin)
    g["meta_tokens"] = dhead[PADF:]
    return dx, g, dict(zip(LATE + MID, list(received) + list(received_mid)))


BIG = {"w_in": ((D_MODEL, IN_WIDTH), 1), "w_q_up": ((Q_LORA, ATT_HEADS * (QK_NOPE + QK_ROPE)), 1),
       "w_kv_up": ((KV_LORA, ATT_HEADS * (QK_NOPE + V_HEAD)), 1), "w_out": ((2 * D_MODEL, D_MODEL), 0),
       "w_mlp_up": ((D_MODEL, D_FF), 1), "w_mlp_down": ((D_FF, D_MODEL), 0), "conv_w": ((CONV_K, CONV_DIM), 1),
       "meta_tokens": ((N_META, D_MODEL), 1)}
SMALL = {"norm_mix_pre": D_MODEL, "q_a_norm": Q_LORA, "kv_a_norm": KV_LORA, "conv_b": CONV_DIM, "dt_bias": SSM_HEADS,
         "a_log": SSM_HEADS, "d_skip": SSM_HEADS, "ssm_norm": SSM_WIDTH, "norm_mix_post": D_MODEL,
         "norm_mlp_pre": D_MODEL, "norm_mlp_post": D_MODEL}
WEIGHT_ORDER = ("meta_tokens", "norm_mix_pre", "w_in", "q_a_norm", "w_q_up", "kv_a_norm", "w_kv_up", "conv_w", "conv_b",
                "dt_bias", "a_log", "d_skip", "ssm_norm", "w_out", "norm_mix_post", "norm_mlp_pre", "w_mlp_up",
                "w_mlp_down", "norm_mlp_post")
ADAM_ROWS = 256


def _shard_shape(name):
    shape, ax = BIG[name]
    return tuple(d // N_DEV if a == ax else d for a, d in enumerate(shape))


SMALL_ROWS = -(-sum(SMALL.values()) // (LANE * 8)) * 8


def _pack(flats, rows):
    v = jnp.concatenate([f.reshape(-1) for f in flats])
    return jnp.pad(v, (0, rows * LANE - v.shape[0])).reshape(rows, LANE)


def _unpack(packed, shapes):
    v = packed.reshape(-1)
    out, o = [], 0
    for s in shapes:
        n = math.prod(s)
        out.append(v[o:o + n].reshape(s))
        o += n
    return out


def _to_chunks(name, full):
    shape, ax = BIG[name]
    if ax == 0:
        return full.reshape((N_DEV,) + _shard_shape(name))
    k, n = shape
    return full.reshape(k, N_DEV, n // N_DEV).transpose(1, 0, 2)


def _from_shards(name, shards):
    shape, ax = BIG[name]
    if ax == 0:
        return shards.reshape(shape)
    return shards.transpose(1, 0, 2).reshape(shape)


def _peer(k):
    x, y, c = lax.axis_index("x"), lax.axis_index("y"), lax.axis_index("c")
    px = 1 - x if k & 4 else x
    py = 1 - y if k & 2 else y
    pc = 1 - c if k & 1 else c
    return (px, py, pc), 4 * px + 2 * py + pc


def _gather_ops(x_refs, out_refs, send_sems, recv_sems, local_sems):
    na = len(x_refs)
    chips = (4, 2, 6)

    def copy(a, n, block, to, src=None):
        return pltpu.make_async_remote_copy(
            src_ref=out_refs[a].at[block] if src is None else src, dst_ref=out_refs[a].at[block],
            send_sem=send_sems.at[7 * a + n], recv_sem=recv_sems.at[7 * a + n], device_id=to, device_id_type=MESH)

    def mine():
        me = _peer(0)[1]
        return [pltpu.make_async_copy(x_refs[a], out_refs[a].at[me], local_sems.at[a]) for a in range(na)]

    def first():
        me, sibling = _peer(0)[1], _peer(1)[0]
        out = [copy(a, 0, me, sibling, src=x_refs[a]) for a in range(na)]
        return out + [copy(a, 1 + n, me, _peer(k)[0], src=x_refs[a]) for n, k in enumerate(chips) for a in range(na)]

    def passed():
        sibling = _peer(1)[0]
        return [copy(a, 4 + n, _peer(k)[1], sibling) for n, k in enumerate(chips) for a in range(na)]

    def start():
        for cp in mine() + first():
            cp.start()

    def forward():
        sibling = _peer(1)[0]
        fwd = passed()
        for n, k in enumerate(chips):
            for a in range(na):
                copy(a, 1 + n, _peer(k)[1], sibling).wait_recv()
                fwd[n * na + a].start()

    def finish():
        sibling = _peer(1)[0]
        for a in range(na):
            copy(a, 0, _peer(1)[1], sibling).wait_recv()
        for n, k in enumerate(chips):
            for a in range(na):
                copy(a, 4 + n, _peer(k | 1)[1], sibling).wait_recv()
        for cp in first() + passed():
            cp.wait_send()
        for cp in mine():
            cp.wait()

    return start, forward, finish


def _gather_scratch(na):
    return [pltpu.SemaphoreType.DMA((7 * na,)), pltpu.SemaphoreType.DMA((7 * na,)), pltpu.SemaphoreType.DMA((na,))]


def _all_gather(shards):
    na = len(shards)

    def body(*refs):
        for step in _gather_ops(refs[:na], refs[na:2 * na], *refs[2 * na:]):
            step()

    return pl.pallas_call(
        body, name="all_gather_weights", out_shape=[_sds((N_DEV,) + s.shape, s.dtype) for s in shards],
        in_specs=[ANY] * na, out_specs=[ANY] * na, scratch_shapes=_gather_scratch(na),
    )(*shards)


def _exchange(chunks, small):
    na = len(chunks) + 1

    def body(*refs):
        for step in _exchange_ops(refs[:na], refs[na:2 * na], *refs[2 * na:], whole=(na - 1,)):
            step()

    arrays = list(chunks) + [small]
    return pl.pallas_call(
        body, name="exchange_grads",
        out_shape=[_sds(c.shape, c.dtype) for c in chunks] + [_sds((N_DEV,) + small.shape, small.dtype)],
        in_specs=[ANY] * na, out_specs=[ANY] * na, scratch_shapes=_gather_scratch(na),
    )(*arrays)


def _exchange_ops(in_refs, out_refs, send_sems, recv_sems, local_sems, whole=()):
    na = len(in_refs)

    def src(a, idx):
        return in_refs[a] if a in whole else in_refs[a].at[idx]

    def own():
        me = _peer(0)[1]
        return [pltpu.make_async_copy(src(a, me), out_refs[a].at[me], local_sems.at[a]) for a in range(na)]

    def copy(a, k, sending):
        me = _peer(0)[1]
        to, idx = _peer(k)
        return pltpu.make_async_remote_copy(
            src_ref=src(a, idx if sending else me), dst_ref=out_refs[a].at[me if sending else idx],
            send_sem=send_sems.at[7 * a + k - 1], recv_sem=recv_sems.at[7 * a + k - 1],
            device_id=to, device_id_type=MESH)

    def sent():
        return [copy(a, k, True) for k in range(1, N_DEV) for a in range(na)]

    def start():
        for cp in own() + sent():
            cp.start()

    def finish():
        for k in range(1, N_DEV):
            for a in range(na):
                copy(a, k, False).wait_recv()
        for cp in sent():
            cp.wait_send()
        for cp in own():
            cp.wait()

    return start, finish


def _reduce_adamw(name, recv, w, m, v):
    rows, cols = w.shape
    blk = ADAM_ROWS if rows % ADAM_ROWS == 0 else rows
    c1 = 1.0 - ADAM_B1 ** ADAM_STEP
    c2 = 1.0 - ADAM_B2 ** ADAM_STEP

    def body(r_ref, w_ref, m_ref, v_ref, g_ref, d_ref, nm_ref, nv_ref):
        g = r_ref[0].astype(F32)
        for s in range(1, N_DEV):
            g = g + r_ref[s].astype(F32)
        g_ref[...] = g
        m_ = ADAM_B1 * m_ref[...] + (1.0 - ADAM_B1) * g
        v_ = ADAM_B2 * v_ref[...] + (1.0 - ADAM_B2) * (g * g)
        nm_ref[...] = m_
        nv_ref[...] = v_
        d_ref[...] = -ADAM_LR * ((m_ / c1) / (jnp.sqrt(v_ / c2) + ADAM_EPS) + ADAM_WD * w_ref[...])

    spec = _rowspec(blk, cols)
    return pl.pallas_call(
        body, name="reduce_adamw_" + name, grid=(rows // blk,),
        in_specs=[pl.BlockSpec((N_DEV, blk, cols), lambda i: (0, i, 0)), spec, spec, spec],
        out_specs=[spec] * 4, out_shape=[_sds((rows, cols), F32)] * 4,
        compiler_params=_cp(("arbitrary",)),
    )(recv, w, m, v)


def kernel(x, meta_tokens, norm_mix_pre, w_in, q_a_norm, w_q_up, kv_a_norm, w_kv_up, conv_w, conv_b, dt_bias, a_log, d_skip, ssm_norm, w_out, norm_mix_post, norm_mlp_pre, w_mlp_up, w_mlp_down, norm_mlp_post, loss_target, m_meta_tokens, m_norm_mix_pre, m_w_in, m_q_a_norm, m_w_q_up, m_kv_a_norm, m_w_kv_up, m_conv_w, m_conv_b, m_dt_bias, m_a_log, m_d_skip, m_ssm_norm, m_w_out, m_norm_mix_post, m_norm_mlp_pre, m_w_mlp_up, m_w_mlp_down, m_norm_mlp_post, v_meta_tokens, v_norm_mix_pre, v_w_in, v_q_a_norm, v_w_q_up, v_kv_a_norm, v_w_kv_up, v_conv_w, v_conv_b, v_dt_bias, v_a_log, v_d_skip, v_ssm_norm, v_w_out, v_norm_mix_post, v_norm_mlp_pre, v_w_mlp_up, v_w_mlp_down, v_norm_mlp_post):
    w = dict(meta_tokens=meta_tokens, norm_mix_pre=norm_mix_pre, w_in=w_in, q_a_norm=q_a_norm, w_q_up=w_q_up,
             kv_a_norm=kv_a_norm, w_kv_up=w_kv_up, conv_w=conv_w, conv_b=conv_b, dt_bias=dt_bias, a_log=a_log,
             d_skip=d_skip, ssm_norm=ssm_norm, w_out=w_out, norm_mix_post=norm_mix_post, norm_mlp_pre=norm_mlp_pre,
             w_mlp_up=w_mlp_up, w_mlp_down=w_mlp_down, norm_mlp_post=norm_mlp_post)
    m = dict(meta_tokens=m_meta_tokens, norm_mix_pre=m_norm_mix_pre, w_in=m_w_in, q_a_norm=m_q_a_norm, w_q_up=m_w_q_up,
             kv_a_norm=m_kv_a_norm, w_kv_up=m_w_kv_up, conv_w=m_conv_w, conv_b=m_conv_b, dt_bias=m_dt_bias,
             a_log=m_a_log, d_skip=m_d_skip, ssm_norm=m_ssm_norm, w_out=m_w_out, norm_mix_post=m_norm_mix_post,
             norm_mlp_pre=m_norm_mlp_pre, w_mlp_up=m_w_mlp_up, w_mlp_down=m_w_mlp_down, norm_mlp_post=m_norm_mlp_post)
    v = dict(meta_tokens=v_meta_tokens, norm_mix_pre=v_norm_mix_pre, w_in=v_w_in, q_a_norm=v_q_a_norm, w_q_up=v_w_q_up,
             kv_a_norm=v_kv_a_norm, w_kv_up=v_w_kv_up, conv_w=v_conv_w, conv_b=v_conv_b, dt_bias=v_dt_bias,
             a_log=v_a_log, d_skip=v_d_skip, ssm_norm=v_ssm_norm, w_out=v_w_out, norm_mix_post=v_norm_mix_post,
             norm_mlp_pre=v_norm_mlp_pre, w_mlp_up=v_w_mlp_up, w_mlp_down=v_w_mlp_down, norm_mlp_post=v_norm_mlp_post)
    big_names = [n for n in WEIGHT_ORDER if n in BIG]
    small_names = [n for n in WEIGHT_ORDER if n in SMALL]
    shard = lambda d, n: d[n].reshape(_shard_shape(n))

    f32_names = ("conv_w", "meta_tokens")
    early = [n for n in big_names if n not in LATE]
    gathered = _all_gather([shard(w, n).astype(F32 if n in f32_names else BF16) for n in early])
    p = {n: w[n].reshape(-1) for n in small_names}
    p.update({n: _from_shards(n, s) for n, s in zip(early, gathered)})
    head = jnp.concatenate([jnp.zeros((PADF, D_MODEL), F32), p["meta_tokens"]], axis=0)
    f = _local_forward(head, x[0], loss_target[0], p, late={n: shard(w, n).astype(BF16) for n in LATE})
    dx, g, recv_of = _local_backward(head, x[0], f, exchange_late=True)
    grad_x = dx[None]
    loss = lax.psum(f["loss"], ("x", "y", "c"))

    rest = [n for n in big_names if n not in LATE + MID]
    small = _pack([g[n] for n in small_names], SMALL_ROWS)
    *recv_rest, recv_small = _exchange([_to_chunks(n, g[n]) for n in rest], small)
    recv_of.update(zip(rest, recv_rest))

    outs = {}
    kinds = ("grad", "delta", "new_m", "new_v")
    for n, recv in ((n, recv_of[n]) for n in big_names):
        for kind, arr in zip(kinds, _reduce_adamw(n, recv, shard(w, n), shard(m, n), shard(v, n))):
            outs[kind, n] = arr.reshape(w[n].shape)
    packed = [_pack([d[n] for n in small_names], SMALL_ROWS) for d in (w, m, v)]
    for kind, arr in zip(kinds, _reduce_adamw("small", recv_small, *packed)):
        for n, val in zip(small_names, _unpack(arr, [(SMALL[n],) for n in small_names])):
            outs[kind, n] = val.reshape(w[n].shape)
    return (loss, grad_x) + tuple(outs[kind, n] for kind in ("grad", "delta", "new_m", "new_v") for n in WEIGHT_ORDER)
```

```python
import math

import jax
import jax.numpy as jnp
import numpy as np
from jax import lax
from jax.experimental import pallas as pl
from jax.experimental.pallas import tpu as pltpu

F32 = jnp.float32
BF16 = jnp.bfloat16

D_MODEL = 1024
N_META = 16
EPS = 1e-6
ATT_HEADS = 8
Q_LORA = 384
KV_LORA = 256
QK_NOPE = 128
QK_ROPE = 64
V_HEAD = 128
ROPE_THETA = 10000.0
SSM_HEADS = 16
SSM_HEAD_DIM = 64
SSM_WIDTH = 1024
SSM_GROUPS = 2
SSM_STATE = 128
CONV_K = 4
CHUNK = 128
CONV_DIM = 1536
D_FF = 4096
IN_SPLITS = (Q_LORA, KV_LORA, QK_ROPE, SSM_WIDTH, CONV_DIM, SSM_HEADS)
IN_WIDTH = sum(IN_SPLITS)
ADAM_LR, ADAM_B1, ADAM_B2, ADAM_EPS, ADAM_WD, ADAM_STEP = 0.001, 0.9, 0.999, 1e-08, 0.01, 10

LANE = 128
ROWB = 512
PADF = ROWB - N_META
HEADW = 256
PC_Q, PC_KV, PC_KR, PC_Z, PC_XBC, PC_DT, PROJ_W = 0, 384, 640, 896, 1920, 3456, 3584
NEG = -1e30
N_DEV = 8
VMEM_LIMIT = 56 * 1024 * 1024
MESH = pl.DeviceIdType.MESH


def _cp(sem, vmem=VMEM_LIMIT, **kw):
    return pltpu.CompilerParams(dimension_semantics=sem, vmem_limit_bytes=vmem, **kw)


def _dot(a, b, dims=((1,), (0,))):
    return lax.dot_general(a, b, (dims, ((), ())), preferred_element_type=F32)


def _bdot(a, b, dims=((1,), (0,))):
    return _dot(a.astype(BF16), b.astype(BF16), dims)


NT = ((1,), (1,))


def _rms_fwd(x, w):
    r = lax.rsqrt(jnp.mean(x * x, axis=-1, keepdims=True) + EPS)
    return (x * r) * w


def _rms_bwd(x, w, dy):
    r = lax.rsqrt(jnp.mean(x * x, axis=-1, keepdims=True) + EPS)
    xh = x * r
    g = dy * w
    dx = r * (g - xh * jnp.mean(g * xh, axis=-1, keepdims=True))
    return dx, dy * xh


def _sigmoid(x):
    return 0.5 * jnp.tanh(0.5 * x) + 0.5


def _colsum8(x):
    t, c = x.shape
    return jnp.sum(x.reshape(t // 8, 8, c), axis=0)


def _rowspec(t, c, cb=0):
    return pl.BlockSpec((t, c), lambda i: (i, cb))


def _fullspec(shape):
    n = len(shape)
    return pl.BlockSpec(shape, lambda i: (0,) * n)


def _sds(shape, dt):
    return jax.ShapeDtypeStruct(shape, dt)


def _acc(ref, val):
    @pl.when(pl.program_id(0) == 0)
    def _():
        ref[...] = jnp.zeros_like(ref)

    ref[...] += val


def _xspec():
    return pl.BlockSpec((ROWB, D_MODEL), lambda i: (jnp.maximum(i - 1, 0), 0))


def _h_block(head_ref, x_ref):
    return jnp.where(pl.program_id(0) == 0, head_ref[...], x_ref[...])


def _norm_in_proj(head, x, g, w_all):
    lp = head.shape[0] + x.shape[0]

    def body(head_ref, x_ref, g_ref, w_ref, hn_ref, cq_ref, ckv_ref, kr_ref, z_ref, xbc_ref, dt_ref):
        hn = _rms_fwd(_h_block(head_ref, x_ref), g_ref[...]).astype(BF16)
        hn_ref[...] = hn
        p = _dot(hn, w_ref[...])
        cq_ref[...] = p[:, PC_Q:PC_KV]
        ckv_ref[...] = p[:, PC_KV:PC_KR]
        kr_ref[...] = p[:, PC_KR:PC_Z]
        z_ref[...] = p[:, PC_Z:PC_XBC]
        xbc_ref[...] = p[:, PC_XBC:PC_DT]
        dt_ref[...] = p[:, PC_DT:PROJ_W]

    widths = (Q_LORA, KV_LORA, HEADW, SSM_WIDTH, CONV_DIM, LANE)
    return pl.pallas_call(
        body, name="norm_in_proj", grid=(lp // ROWB,),
        in_specs=[_fullspec((ROWB, D_MODEL)), _xspec(), _fullspec((1, D_MODEL)), _fullspec((D_MODEL, PROJ_W))],
        out_specs=[_rowspec(ROWB, D_MODEL)] + [_rowspec(ROWB, w) for w in widths],
        out_shape=[_sds((lp, D_MODEL), BF16)] + [_sds((lp, w), F32) for w in widths],
        compiler_params=_cp(("arbitrary",)),
    )(head, x, g, w_all)


def _rope(x, cos, sa, sb):
    w = x.shape[1]
    return x * cos + pltpu.roll(x, w - 32, 1) * sa + pltpu.roll(x, 32, 1) * sb


def _rope_t(dy, cos, sa, sb):
    w = dy.shape[1]
    return dy * cos + pltpu.roll(dy * sa, 32, 1) + pltpu.roll(dy * sb, w - 32, 1)


def _qkv(cq, ckv, kr, cos, sa, sb, gq, gkv, wq, wk, wv):
    lp = cq.shape[0]
    qw = ATT_HEADS * HEADW
    half = HEADW // 2
    assert half == QK_NOPE == V_HEAD == LANE

    def body(cq_ref, ckv_ref, kr_ref, cos_ref, sa_ref, sb_ref, gq_ref, gkv_ref, wq_ref, wk_ref, wv_ref,
             q_ref, k_ref, v_ref, cqn_ref, ckvn_ref):
        tabs = [cos_ref[...], sa_ref[...], sb_ref[...]]
        cqn = _rms_fwd(cq_ref[...], gq_ref[...]).astype(BF16)
        ckvn = _rms_fwd(ckv_ref[...], gkv_ref[...]).astype(BF16)
        cqn_ref[...] = cqn
        ckvn_ref[...] = ckvn
        q = _dot(cqn, wq_ref[...])
        kn = _dot(ckvn, wk_ref[...])
        vv = _dot(ckvn, wv_ref[...])
        krope = _rope(kr_ref[:, pl.ds(half, half)], *tabs).astype(BF16)
        ones = jnp.ones((ROWB, half), BF16)
        for hh in range(ATT_HEADS):
            lo, hi, src = pl.ds(hh * HEADW, half), pl.ds(hh * HEADW + half, half), slice(hh * half, (hh + 1) * half)
            q_ref[:, lo] = (q[:, hh * HEADW:hh * HEADW + half] * Q_PRESCALE).astype(BF16)
            q_ref[:, hi] = (_rope(q[:, hh * HEADW + half:(hh + 1) * HEADW], *tabs) * Q_PRESCALE).astype(BF16)
            k_ref[:, lo] = kn[:, src].astype(BF16)
            k_ref[:, hi] = krope
            v_ref[:, lo] = vv[:, src].astype(BF16)
            v_ref[:, hi] = ones

    return pl.pallas_call(
        body, name="qkv", grid=(lp // ROWB,),
        in_specs=[_rowspec(ROWB, Q_LORA), _rowspec(ROWB, KV_LORA), _rowspec(ROWB, HEADW)]
        + [_rowspec(ROWB, half)] * 3
        + [_fullspec((1, Q_LORA)), _fullspec((1, KV_LORA)), _fullspec((Q_LORA, qw)),
           _fullspec((KV_LORA, ATT_HEADS * QK_NOPE)), _fullspec((KV_LORA, ATT_HEADS * V_HEAD))],
        out_specs=[_rowspec(ROWB, qw), _rowspec(ROWB, qw), _rowspec(ROWB, qw),
                   _rowspec(ROWB, Q_LORA), _rowspec(ROWB, KV_LORA)],
        out_shape=[_sds((lp, qw), BF16), _sds((lp, qw), BF16), _sds((lp, qw), BF16),
                   _sds((lp, Q_LORA), BF16), _sds((lp, KV_LORA), BF16)],
        compiler_params=_cp(("arbitrary",)),
    )(cq, ckv, kr, cos, sa, sb, gq, gkv, wq, wk, wv)


ATT_SCALE = (QK_NOPE + QK_ROPE) ** -0.5
LOG2E = 1.4426950408889634
LN2 = 0.6931471805599453
Q_PRESCALE = ATT_SCALE * LOG2E
KVB = 512
META_KEYS = LANE
assert N_META <= META_KEYS


def _att_ok(qrow, krow):
    return (krow <= qrow) & ((krow >= PADF) | (qrow < PADF))


def _lanes(x, n):
    return x if n == 1 else jnp.concatenate([x] * n, axis=1)


def _pair_loop(lo, hi, tile, unrolls=(2,)):
    for u in tuple(unrolls) + (1,):
        n = jnp.maximum(hi - lo, 0)
        trips = n // u

        def many(t, c, u=u, lo=lo):
            for d in range(u):
                tile(lo + u * t + d)
            return c

        lax.fori_loop(0, trips, many, 0)
        lo = lo + trips * u


def _flash_fwd(q, k, v):
    lp = q.shape[0]
    nq = lp // ROWB

    def body(q_ref, k_ref, v_ref, o_ref, lse_ref, acc, m_s):
        i = pl.program_id(1)
        qb = q_ref[...]
        m_s[...] = jnp.full_like(m_s, NEG)
        acc[...] = jnp.zeros_like(acc)

        def tile(j, masked, off=None, nkeys=KVB):
            off = pl.multiple_of(j * KVB, KVB) if off is None else off
            kb = k_ref[pl.ds(off, nkeys), :]
            vb = v_ref[pl.ds(off, nkeys), :]
            s = _dot(qb, kb, NT)
            if masked:
                qrow = i * ROWB + lax.broadcasted_iota(jnp.int32, s.shape, 0)
                krow = off + lax.broadcasted_iota(jnp.int32, s.shape, 1)
                s = jnp.where(_att_ok(qrow, krow), s, NEG)
            m_prev = m_s[...]
            m_new = jnp.maximum(m_prev, jnp.max(s, axis=1, keepdims=True))
            alpha = jnp.exp2(m_prev - m_new)
            p = jnp.exp2(s - _lanes(m_new, nkeys // LANE))
            acc[...] = _lanes(alpha, 2) * acc[...] + _dot(p.astype(BF16), vb)
            m_s[...] = m_new

        def first_tile():
            tile(0, True, off=ROWB - META_KEYS, nkeys=META_KEYS)

        @pl.when(i == 0)
        def _():
            first_tile()

        odd = jnp.bitwise_and(jnp.maximum(i - 1, 0), 7)
        for r in range(8):
            @pl.when((i > 0) & (odd == r))
            def _(r=r):
                first_tile()
                tile(i, True)
                for d in range(r):
                    tile(1 + d, False)

        _pair_loop(1 + odd, i, lambda j: tile(j, False), (16, 8))
        l = acc[:, V_HEAD:]
        o_ref[...] = (acc[:, :V_HEAD] / l).astype(BF16)
        lse_ref[0] = (m_s[...] + jnp.log2(l)).T[0:1, :]

    return pl.pallas_call(
        body, name="flash_fwd", grid=(ATT_HEADS, nq),
        in_specs=[pl.BlockSpec((ROWB, HEADW), lambda h, i: (i, h)),
                  pl.BlockSpec((lp, HEADW), lambda h, i: (0, h)),
                  pl.BlockSpec((lp, HEADW), lambda h, i: (0, h))],
        out_specs=[pl.BlockSpec((ROWB, V_HEAD), lambda h, i: (i, h)),
                   pl.BlockSpec((1, 1, ROWB), lambda h, i: (h, 0, i))],
        out_shape=[_sds((lp, ATT_HEADS * V_HEAD), BF16), _sds((ATT_HEADS, 1, lp), F32)],
        scratch_shapes=[pltpu.VMEM((ROWB, HEADW), F32), pltpu.VMEM((ROWB, LANE), F32)],
        compiler_params=_cp(("arbitrary", "arbitrary")),
    )(q, k, v)


def _silu(x):
    return x * _sigmoid(x)


CONV_ROWS = 64


def _shifted_rows(ref, start, n, cols, offsets):
    win = ref[pl.ds(start, n + 8), cols]
    return [win[o:o + n] if o % 8 == 0 else pltpu.roll(win, n + 8 - o, 0)[0:n] for o in offsets]


def _conv_fwd(xbc, cw, cb):
    lp, c = xbc.shape
    t8 = ROWB // 8

    def body(x_ref, prev_ref, w_ref, b_ref, o_ref, buf):
        i = pl.program_id(0)
        buf[pl.ds(0, 8), :] = jnp.where(i > 0, prev_ref[...], 0.0)
        buf[pl.ds(8, ROWB), :] = x_ref[...]

        def strip(s, carry):
            cs = pl.ds(pl.multiple_of(s * LANE, LANE), LANE)
            w, b = w_ref[:, cs], b_ref[:, cs]
            for r0 in range(0, ROWB, CONV_ROWS):
                taps = _shifted_rows(buf, r0, CONV_ROWS, cs, [8 - (CONV_K - 1) + kk for kk in range(CONV_K)])
                pre = b + sum(w[kk:kk + 1, :] * taps[kk] for kk in range(CONV_K))
                o_ref[pl.ds(r0, CONV_ROWS), cs] = _silu(pre)
            return carry

        lax.fori_loop(0, c // LANE, strip, 0)

    return pl.pallas_call(
        body, name="conv_fwd", grid=(lp // ROWB,),
        in_specs=[_rowspec(ROWB, c), pl.BlockSpec((8, c), lambda i: (jnp.maximum(i * t8 - 1, 0), 0)),
                  _fullspec((8, c)), _fullspec((1, c))],
        out_specs=_rowspec(ROWB, c), out_shape=_sds((lp, c), F32),
        scratch_shapes=[pltpu.VMEM((ROWB + 8, c), F32)],
        compiler_params=_cp(("arbitrary",)),
    )(xbc, xbc, cw, cb)


def _expand_mat():
    r = np.arange(LANE)[:, None]
    c = np.arange(SSM_WIDTH)[None, :]
    return jnp.asarray((c // SSM_HEAD_DIM == r).astype(np.float32))


def _tri_mat():
    i = np.arange(CHUNK)
    return jnp.asarray((i[:, None] >= i[None, :]).astype(np.float32))


def _x3(m, axis):
    return jnp.concatenate([m.astype(BF16)] * 3, axis=axis)


def _split3(x):
    hi = x.astype(BF16)
    r = x - hi.astype(F32)
    mid = r.astype(BF16)
    return hi, mid, (r - mid.astype(F32)).astype(BF16)


def _dot01_r(x, m3):
    return _dot(jnp.concatenate(_split3(x), axis=1), m3)


def _dot01_l(m3, x):
    return _dot(m3, jnp.concatenate(_split3(x), axis=0))


def _ssd_prep(dt_raw, bias_ref, alog_ref, tri3, c, seq_rows):
    rows = c * CHUNK + lax.broadcasted_iota(jnp.int32, (CHUNK, LANE), 0)
    lanes = lax.broadcasted_iota(jnp.int32, (CHUNK, LANE), 1)
    valid = (rows >= PADF) & (rows < PADF + seq_rows) & (lanes < SSM_HEADS)
    dtr = dt_raw + bias_ref[...]
    sp = jnp.maximum(dtr, 0.0) + jnp.log(1.0 + jnp.exp(-jnp.abs(dtr)))
    dt = jnp.where(valid, sp, 0.0)
    a = -jnp.exp(alog_ref[...])
    acol = _dot01_l(tri3, dt * a)
    return dt, a, acol, valid, dtr


def _row16(v):
    return jnp.broadcast_to(v, (16, v.shape[1]))


def _ssd_fwd(xbc_act, dtr, dt_bias, a_log, seq_rows, gather=()):
    lp = xbc_act.shape[0]
    nc = lp // CHUNK
    cps = ROWB // CHUNK
    nsteps = nc // cps
    gw = SSM_WIDTH // SSM_GROUPS
    hpg = SSM_HEADS // SSM_GROUPS

    na = len(gather)

    def body(x_ref, b_ref, c_ref, dtr_ref, bias_ref, alog_ref, tri_ref, tri3_ref, ex3_ref, *rest):
        gin, (y_ref, hp_ref), gout, h_s, sems = rest[:na], rest[na:na + 2], rest[na + 2:2 * na + 2], rest[2 * na + 2], rest[2 * na + 3:]
        step = pl.program_id(0)

        @pl.when(step == 0)
        def _():
            h_s[...] = jnp.zeros_like(h_s)

        if na:
            g_start, g_forward, g_finish = _gather_ops(gin, gout, *sems)
            pl.when(step == 0)(g_start)
            pl.when(step == nsteps // 2)(g_forward)

        ex3 = ex3_ref[...]
        causal = tri_ref[...] > 0.5
        for cc in range(cps):
            rows = pl.ds(cc * CHUNK, CHUNK)
            dt, a, acol, _, _ = _ssd_prep(dtr_ref[rows, :], bias_ref, alog_ref, tri3_ref[...], step * cps + cc, seq_rows)
            arow = acol.T
            dtrow = dt.T
            alast = acol[CHUNK - 1:CHUNK, :]
            e_all = _dot01_r(jnp.exp(acol), ex3)
            wx_all = _dot01_r(jnp.exp(alast - acol) * dt, ex3)
            dec_all = _dot01_r(_row16(jnp.exp(alast)), ex3)[0:1, :]
            hp_ref[cc] = h_s[...]
            for g in range(SSM_GROUPS):
                gs = slice(g * gw, (g + 1) * gw)
                bg = b_ref[rows, g * SSM_STATE:(g + 1) * SSM_STATE]
                cg = c_ref[rows, g * SSM_STATE:(g + 1) * SSM_STATE].astype(BF16)
                xg = x_ref[rows, gs]
                hg = h_s[:, gs]
                gm = _bdot(cg, bg, NT)
                y_off = _bdot(cg, hg) * e_all[:, gs]
                for r in range(hpg):
                    hd = g * hpg + r
                    seg = acol[:, hd:hd + 1] - arow[hd:hd + 1, :]
                    lm = jnp.where(causal, jnp.exp(jnp.where(causal, seg, 0.0)), 0.0)
                    w = gm * lm * dtrow[hd:hd + 1, :]
                    cs = slice(r * SSM_HEAD_DIM, (r + 1) * SSM_HEAD_DIM)
                    y_ref[rows, pl.ds(hd * SSM_HEAD_DIM, SSM_HEAD_DIM)] = _bdot(w, xg[:, cs]) + y_off[:, cs]
                st = _bdot(bg.T, xg * wx_all[:, gs])
                h_s[:, gs] = hg * dec_all[:, gs] + st

        if na:
            pl.when(step == nsteps - 1)(g_finish)

    xs_spec = pl.BlockSpec((ROWB, SSM_WIDTH), lambda c: (c, 0))
    b_spec = pl.BlockSpec((ROWB, 2 * SSM_STATE), lambda c: (c, SSM_WIDTH // (2 * SSM_STATE)))
    c_spec = pl.BlockSpec((ROWB, 2 * SSM_STATE), lambda c: (c, SSM_WIDTH // (2 * SSM_STATE) + 1))
    y, hprev, *gathered = pl.pallas_call(
        body, name="ssd_fwd", grid=(nsteps,),
        in_specs=[xs_spec, b_spec, c_spec, pl.BlockSpec((ROWB, LANE), lambda c: (c, 0)),
                  _fullspec((1, LANE)), _fullspec((1, LANE)), _fullspec((CHUNK, CHUNK)), _fullspec((CHUNK, 3 * CHUNK)),
                  _fullspec((3 * LANE, SSM_WIDTH))] + [ANY] * na,
        out_specs=[xs_spec, pl.BlockSpec((cps, SSM_STATE, SSM_WIDTH), lambda c: (c, 0, 0))] + [ANY] * na,
        out_shape=[_sds((lp, SSM_WIDTH), F32), _sds((nc, SSM_STATE, SSM_WIDTH), F32)]
        + [_sds((N_DEV,) + s.shape, s.dtype) for s in gather],
        scratch_shapes=[pltpu.VMEM((SSM_STATE, SSM_WIDTH), F32)] + (_gather_scratch(na) if na else []),
        compiler_params=_cp(("arbitrary",)),
    )(xbc_act, xbc_act, xbc_act, dtr, dt_bias, a_log, _tri_mat(), _x3(_tri_mat(), 1), _x3(_expand_mat(), 0), *gather)
    return y, hprev, gathered


def _group_mean(x):
    gw = SSM_WIDTH // SSM_GROUPS
    parts = [jnp.broadcast_to(jnp.mean(x[:, g * gw:(g + 1) * gw], axis=-1, keepdims=True), (x.shape[0], gw))
             for g in range(SSM_GROUPS)]
    return jnp.concatenate(parts, axis=1)


def _ssd_post(y, xbc_act, z, dskip, gnorm):
    lp = y.shape[0]

    def body(y_ref, x_ref, z_ref, d_ref, g_ref, o_ref):
        z_ = z_ref[...]
        gt = (y_ref[...] + d_ref[...] * x_ref[...]) * _silu(z_)
        r = lax.rsqrt(_group_mean(gt * gt) + EPS)
        o_ref[...] = ((gt * r) * g_ref[...]).astype(BF16)

    return pl.pallas_call(
        body, name="ssd_post", grid=(lp // ROWB,),
        in_specs=[_rowspec(ROWB, SSM_WIDTH)] * 3 + [_fullspec((1, SSM_WIDTH))] * 2,
        out_specs=_rowspec(ROWB, SSM_WIDTH), out_shape=_sds((lp, SSM_WIDTH), BF16),
        compiler_params=_cp(("arbitrary",)),
    )(y, xbc_act, z, dskip, gnorm)


def _out_proj(att, ssm, head, x, w_out, g_post):
    lp = att.shape[0]

    def body(a_ref, s_ref, head_ref, x_ref, w_ref, g_ref, mix_ref, h1_ref):
        mix = _dot(a_ref[...], w_ref[pl.ds(0, 1024), :]) + _dot(s_ref[...], w_ref[pl.ds(1024, 1024), :])
        mix_ref[...] = mix
        h1_ref[...] = _h_block(head_ref, x_ref) + _rms_fwd(mix, g_ref[...])

    return pl.pallas_call(
        body, name="out_proj", grid=(lp // ROWB,),
        in_specs=[_rowspec(ROWB, 1024)] * 2 + [_fullspec((ROWB, D_MODEL)), _xspec(), _fullspec((2048, D_MODEL)),
                                               _fullspec((1, D_MODEL))],
        out_specs=[_rowspec(ROWB, D_MODEL)] * 2, out_shape=[_sds((lp, D_MODEL), F32)] * 2,
        compiler_params=_cp(("arbitrary",)),
    )(att, ssm, head, x, w_out, g_post)


def _resident(w_hbm, w_vmem, sem):
    @pl.when(pl.program_id(0) == 0)
    def _():
        cp = pltpu.make_async_copy(w_hbm, w_vmem, sem)
        cp.start()
        cp.wait()


ANY = pl.BlockSpec(memory_space=pl.ANY)


def _mlp_fwd(h1, tgt, w_up, w_down, g_pre, g_post, seq_rows):
    lp = h1.shape[0]

    def body(h1_ref, t_ref, wu_hbm, wd_hbm, gpre_ref, gpost_ref, hn2_ref, u_ref, a_ref, f_ref, dh2_ref, loss_ref,
             wu, wd, sems):
        _resident(wu_hbm, wu, sems.at[0])
        _resident(wd_hbm, wd, sems.at[1])
        i = pl.program_id(0)
        h1_ = h1_ref[...]
        hn2 = _rms_fwd(h1_, gpre_ref[...]).astype(BF16)
        hn2_ref[...] = hn2
        u = jnp.maximum(_dot(hn2, wu[...]), 0.0)
        u_ref[...] = u.astype(BF16)
        a = (u * u).astype(BF16)
        a_ref[...] = a
        f = _dot(a, wd[...])
        f_ref[...] = f
        h2 = h1_ + _rms_fwd(f, gpost_ref[...])
        rows = i * ROWB + lax.broadcasted_iota(jnp.int32, (ROWB, 1), 0)
        real = (rows >= PADF + N_META) & (rows < PADF + seq_rows)
        err = jnp.where(real, h2 - t_ref[...], 0.0)
        dh2_ref[...] = err * (1.0 / D_MODEL)
        _acc(loss_ref, _colsum8(err * err))

    return pl.pallas_call(
        body, name="mlp_fwd", grid=(lp // ROWB,),
        in_specs=[_rowspec(ROWB, D_MODEL), _xspec()] + [ANY, ANY] + [_fullspec((1, D_MODEL))] * 2,
        out_specs=[_rowspec(ROWB, D_MODEL), _rowspec(ROWB, D_FF), _rowspec(ROWB, D_FF)] + [_rowspec(ROWB, D_MODEL)] * 2
        + [_fullspec((8, D_MODEL))],
        out_shape=[_sds((lp, D_MODEL), BF16), _sds((lp, D_FF), BF16), _sds((lp, D_FF), BF16), _sds((lp, D_MODEL), F32),
                   _sds((lp, D_MODEL), F32), _sds((8, D_MODEL), F32)],
        scratch_shapes=[pltpu.VMEM((D_MODEL, D_FF), BF16), pltpu.VMEM((D_FF, D_MODEL), BF16), pltpu.SemaphoreType.DMA((2,))],
        compiler_params=_cp(("arbitrary",)),
    )(h1, tgt, w_up, w_down, g_pre, g_post)


def _pad_cols(w, width):
    return jnp.pad(w, ((0, 0), (0, width - w.shape[1])))


def _layout_weights(w_in, w_q_up, w_kv_up):
    o = np.cumsum((0,) + IN_SPLITS)
    pieces = [w_in[:, o[k]:o[k + 1]] for k in range(6)]
    kr = jnp.pad(pieces[2], ((0, 0), (QK_NOPE, HEADW - QK_NOPE - QK_ROPE)))
    w_all = jnp.concatenate([pieces[0], pieces[1], kr, pieces[3], pieces[4], _pad_cols(pieces[5], LANE)], axis=1)
    wq = jnp.pad(w_q_up.reshape(Q_LORA, ATT_HEADS, QK_NOPE + QK_ROPE), ((0, 0), (0, 0), (0, HEADW - QK_NOPE - QK_ROPE)))
    wkv = w_kv_up.reshape(KV_LORA, ATT_HEADS, QK_NOPE + V_HEAD)
    wk = jnp.pad(wkv[:, :, :QK_NOPE], ((0, 0), (0, 0), (0, HEADW - QK_NOPE)))
    wv = wkv[:, :, QK_NOPE:]
    return (w_all, wq.reshape(Q_LORA, -1), wk.reshape(KV_LORA, -1), wv.reshape(KV_LORA, -1),
            wkv[:, :, :QK_NOPE].reshape(KV_LORA, -1))


def _rope_tables(lp):
    pos = jnp.maximum(jnp.arange(lp, dtype=jnp.int32) - PADF, 0).astype(F32)
    inv_freq = ROPE_THETA ** (-jnp.arange(0, QK_ROPE, 2, dtype=F32) / QK_ROPE)
    ang = pos[:, None] * inv_freq[None, :]
    cos, sin = jnp.cos(ang), jnp.sin(ang)
    z32, z64 = jnp.zeros((lp, 32), F32), jnp.zeros((lp, 64), F32)
    cos_t = jnp.concatenate([cos, cos, jnp.ones((lp, 64), F32)], axis=1)
    sa = jnp.concatenate([-sin, z32, z64], axis=1)
    sb = jnp.concatenate([z32, sin, z64], axis=1)
    return cos_t, sa, sb


def _row1(v, width=None):
    v = v.reshape(1, -1).astype(F32)
    return v if width is None else _pad_cols(v, width)


LATE = ("w_out", "w_mlp_up", "w_mlp_down")
MID = ("w_in", "w_q_up", "w_kv_up", "conv_w")


def _local_forward(head, x, tgt, p, late=None):
    assert head.shape[0] == ROWB and x.shape[0] % ROWB == 0
    lp = ROWB + x.shape[0]
    seq_rows = N_META + x.shape[0]
    f = {"seq_rows": seq_rows}
    w_all, wq, wk, wv, wkn = _layout_weights(p["w_in"], p["w_q_up"], p["w_kv_up"])
    f.update(w_all=w_all, wq=wq, wk=wk, wv=wv)
    f["hn"], cq, ckv, kr, f["z"], f["xbc"], f["dtr"] = _norm_in_proj(head, x, _row1(p["norm_mix_pre"]), w_all)
    f.update(cq=cq, ckv=ckv)
    f["rope"] = _rope_tables(lp)
    f["q"], f["k"], f["v"], f["cqn"], f["ckvn"] = _qkv(cq, ckv, kr, *f["rope"], _row1(p["q_a_norm"]),
                                                   _row1(p["kv_a_norm"]), wq, wkn, wv)
    f["att"], f["lse"] = _flash_fwd(f["q"], f["k"], f["v"])
    f["cw"] = jnp.pad(p["conv_w"].astype(F32), ((0, 8 - CONV_K), (0, 0)))
    f["xact"] = _conv_fwd(f["xbc"], f["cw"], _row1(p["conv_b"]))
    f["dt_bias"], f["a_log"] = _row1(p["dt_bias"], LANE), _row1(p["a_log"], LANE)
    f["y"], f["hprev"], gathered = _ssd_fwd(f["xact"], f["dtr"], f["dt_bias"], f["a_log"], seq_rows,
                                            gather=[late[n] for n in LATE] if late else ())
    p = {**p, **{n: _from_shards(n, s) for n, s in zip(LATE, gathered)}}
    f["p"] = p
    f["dskip"] = jnp.repeat(p["d_skip"].reshape(-1).astype(F32), SSM_HEAD_DIM).reshape(1, SSM_WIDTH)
    f["ssm"] = _ssd_post(f["y"], f["xact"], f["z"], f["dskip"], _row1(p["ssm_norm"]))
    f["mix"], f["h1"] = _out_proj(f["att"], f["ssm"], head, x, p["w_out"], _row1(p["norm_mix_post"]))
    f["hn2"], f["u"], f["a"], f["f"], f["dh2"], loss8 = _mlp_fwd(
        f["h1"], tgt, p["w_mlp_up"], p["w_mlp_down"], _row1(p["norm_mlp_pre"]), _row1(p["norm_mlp_post"]), seq_rows)
    f["loss"] = 0.5 * jnp.sum(loss8) / D_MODEL
    return f


MLPB = 256


def _mlp_bwd(dh2, f, h1, u, w_up, w_down, g_pre, g_post):
    lp = h1.shape[0]

    def body(dh2_ref, f_ref, h1_ref, u_ref, wu_hbm, wd_hbm, gpre_ref, gpost_ref,
             dh1_ref, du_ref, df_ref, dgpre_ref, dgpost_ref, wu, wd, sems):
        _resident(wu_hbm, wu, sems.at[0])
        _resident(wd_hbm, wd, sems.at[1])
        dh2_ = dh2_ref[...]
        df, dgp = _rms_bwd(f_ref[...], gpost_ref[...], dh2_)
        dfb = df.astype(BF16)
        df_ref[...] = dfb
        da = _dot(dfb, wd[...], NT)
        du = (da * (2.0 * u_ref[...].astype(F32))).astype(BF16)
        du_ref[...] = du
        dhn2 = _dot(du, wu[...], NT)
        dx, dgq = _rms_bwd(h1_ref[...], gpre_ref[...], dhn2)
        dh1_ref[...] = dh2_ + dx
        _acc(dgpre_ref, _colsum8(dgq))
        _acc(dgpost_ref, _colsum8(dgp))

    return pl.pallas_call(
        body, name="mlp_bwd", grid=(lp // MLPB,),
        in_specs=[_rowspec(MLPB, D_MODEL)] * 3 + [_rowspec(MLPB, D_FF), ANY, ANY] + [_fullspec((1, D_MODEL))] * 2,
        out_specs=[_rowspec(MLPB, D_MODEL), _rowspec(MLPB, D_FF), _rowspec(MLPB, D_MODEL),
                   _fullspec((8, D_MODEL)), _fullspec((8, D_MODEL))],
        out_shape=[_sds((lp, D_MODEL), F32), _sds((lp, D_FF), BF16), _sds((lp, D_MODEL), BF16),
                   _sds((8, D_MODEL), F32), _sds((8, D_MODEL), F32)],
        scratch_shapes=[pltpu.VMEM((D_MODEL, D_FF), BF16), pltpu.VMEM((D_FF, D_MODEL), BF16), pltpu.SemaphoreType.DMA((2,))],
        compiler_params=_cp(("arbitrary",)),
    )(dh2, f, h1, u, w_up, w_down, g_pre, g_post)


def _out_bwd(dh1, mix, att, w_out, g_post):
    lp = dh1.shape[0]

    def body(dh1_ref, mix_ref, att_ref, w_ref, g_ref, dmix_ref, datt_ref, dssm_ref, dg_ref, dl_ref):
        dmix, dg = _rms_bwd(mix_ref[...], g_ref[...], dh1_ref[...])
        dmb = dmix.astype(BF16)
        dmix_ref[...] = dmb
        datt = _dot(dmb, w_ref[pl.ds(0, 1024), :], NT).astype(BF16)
        datt_ref[...] = datt
        dssm_ref[...] = _dot(dmb, w_ref[pl.ds(1024, 1024), :], NT)
        _acc(dg_ref, _colsum8(dg))
        prod = datt.astype(F32) * att_ref[...].astype(F32)
        for hh in range(ATT_HEADS):
            d = jnp.sum(prod[:, hh * V_HEAD:(hh + 1) * V_HEAD], axis=1, keepdims=True)
            dl_ref[hh] = jnp.broadcast_to(d, (ROWB, LANE)).T[0:1, :]

    return pl.pallas_call(
        body, name="out_bwd", grid=(lp // ROWB,),
        in_specs=[_rowspec(ROWB, D_MODEL)] * 3 + [_fullspec((2048, D_MODEL)), _fullspec((1, D_MODEL))],
        out_specs=[_rowspec(ROWB, D_MODEL)] * 3 + [_fullspec((8, D_MODEL)),
                                                   pl.BlockSpec((ATT_HEADS, 1, ROWB), lambda i: (0, 0, i))],
        out_shape=[_sds((lp, D_MODEL), BF16), _sds((lp, 1024), BF16), _sds((lp, 1024), F32), _sds((8, D_MODEL), F32),
                   _sds((ATT_HEADS, 1, lp), F32)],
        compiler_params=_cp(("arbitrary",)),
    )(dh1, mix, att, w_out, g_post)


def _ssd_post_bwd(dssm, y, xact, z, dskip, gnorm):
    lp = y.shape[0]

    def body(do_ref, y_ref, x_ref, z_ref, d_ref, g_ref, dy_ref, dz_ref, dg_ref, dd_ref):
        z_, x_ = z_ref[...], x_ref[...]
        sg = _sigmoid(z_)
        sz = z_ * sg
        y2 = y_ref[...] + d_ref[...] * x_
        gt = y2 * sz
        r = lax.rsqrt(_group_mean(gt * gt) + EPS)
        gh = gt * r
        do = do_ref[...]
        dgh = do * g_ref[...]
        dgt = r * (dgh - gh * _group_mean(dgh * gh))
        dy2 = dgt * sz
        dy_ref[...] = dy2
        dz_ref[...] = (dgt * y2 * (sg * (1.0 + z_ * (1.0 - sg)))).astype(BF16)
        _acc(dg_ref, _colsum8(do * gh))
        _acc(dd_ref, _colsum8(dy2 * x_))

    return pl.pallas_call(
        body, name="ssd_post_bwd", grid=(lp // ROWB,),
        in_specs=[_rowspec(ROWB, SSM_WIDTH)] * 4 + [_fullspec((1, SSM_WIDTH))] * 2,
        out_specs=[_rowspec(ROWB, SSM_WIDTH)] * 2 + [_fullspec((8, SSM_WIDTH))] * 2,
        out_shape=[_sds((lp, SSM_WIDTH), F32), _sds((lp, SSM_WIDTH), BF16), _sds((8, SSM_WIDTH), F32), _sds((8, SSM_WIDTH), F32)],
        compiler_params=_cp(("arbitrary",)),
    )(dssm, y, xact, z, dskip, gnorm)


def _ssd_bwd(dy, xact, dtr, hprev, dt_bias, a_log, dskip, seq_rows, exchange=()):
    lp = xact.shape[0]
    nc = lp // CHUNK
    gw = SSM_WIDTH // SSM_GROUPS
    hpg = SSM_HEADS // SSM_GROUPS
    nb = SSM_WIDTH // (2 * SSM_STATE)
    na = len(exchange)
    cps = ROWB // CHUNK
    nsteps = nc // cps

    def body(dy_ref, x_ref, b_ref, c_ref, dtr_ref, hp_ref, bias_ref, alog_ref, dsk_ref, tri_ref, tri3_ref, trit3_ref,
             ex3_ref, ext3_ref, *rest):
        xin, (dact_ref, ddtr_ref, da_ref, dbias_ref), xout = rest[:na], rest[na:na + 4], rest[na + 4:2 * na + 4]
        dh_s, sems = rest[2 * na + 4], rest[2 * na + 5:]
        step = pl.program_id(0)

        @pl.when(step == 0)
        def _():
            dh_s[...] = jnp.zeros_like(dh_s)
            da_ref[...] = jnp.zeros_like(da_ref)
            dbias_ref[...] = jnp.zeros_like(dbias_ref)

        if na:
            x_start, x_finish = _exchange_ops(xin, xout, *sems)
            pl.when(step == 0)(x_start)

        for lc in reversed(range(cps)):
            rows = pl.ds(lc * CHUNK, CHUNK)
            chunk((nsteps - 1 - step) * cps + lc, dy_ref.at[rows], x_ref.at[rows], b_ref.at[rows], c_ref.at[rows],
                  dtr_ref.at[rows], hp_ref.at[pl.ds(lc, 1)], bias_ref, alog_ref, dsk_ref, tri_ref, tri3_ref, trit3_ref,
                  ex3_ref, ext3_ref, dact_ref.at[rows], ddtr_ref.at[rows], da_ref, dbias_ref, dh_s)

        if na:
            pl.when(step == nsteps - 1)(x_finish)

    def chunk(c, dy_ref, x_ref, b_ref, c_ref, dtr_ref, hp_ref, bias_ref, alog_ref, dsk_ref, tri_ref, tri3_ref, trit3_ref,
              ex3_ref, ext3_ref, dact_ref, ddtr_ref, da_ref, dbias_ref, dh_s):
        tri = tri_ref[...]
        ex3 = ex3_ref[...]
        dt, a, acol, valid, dtr_ = _ssd_prep(dtr_ref[...], bias_ref, alog_ref, tri3_ref[...], c, seq_rows)
        arow = acol.T
        dtrow = dt.T
        alast = acol[CHUNK - 1:CHUNK, :]
        e_all = _dot01_r(jnp.exp(acol), ex3)
        wgt0 = jnp.exp(alast - acol)
        wgt = wgt0 * dt
        wx_all = _dot01_r(wgt, ex3)
        elast = jnp.exp(alast)
        dec_all = _dot01_r(_row16(elast), ex3)[0:1, :]
        causal = tri > 0.5
        upper = tri.T > 0.5
        lane_id = lax.broadcasted_iota(jnp.int32, (1, LANE), 1)
        sub_id = lax.broadcasted_iota(jnp.int32, (CHUNK, 1), 0)
        dacol = jnp.zeros((CHUNK, LANE), F32)
        darowf = jnp.zeros((CHUNK, LANE), F32)
        ddtrowf = jnp.zeros((CHUNK, LANE), F32)
        dwgt = jnp.zeros((CHUNK, LANE), F32)
        delast = jnp.zeros((1, LANE), F32)
        for g in range(SSM_GROUPS):
            gs = slice(g * gw, (g + 1) * gw)
            ext3_g = ext3_ref[g]
            bg = b_ref[:, g * SSM_STATE:(g + 1) * SSM_STATE]
            cg = c_ref[:, g * SSM_STATE:(g + 1) * SSM_STATE]
            bgb, cgb = bg.astype(BF16), cg.astype(BF16)
            xg = x_ref[:, gs]
            dyg = dy_ref[:, gs]
            hg = hp_ref[0, :, gs]
            dhg = dh_s[:, gs]
            hgb, dhgb = hg.astype(BF16), dhg.astype(BF16)
            gm = _dot(cgb, bgb, NT)
            gmt = _dot(bgb, cgb, NT)
            y_off = _dot(cgb, hgb) * e_all[:, gs]
            dy0 = (dyg * e_all[:, gs]).astype(BF16)
            dcg = _dot(dy0, hgb, NT)
            dh_in = _dot(cg.T.astype(BF16), dy0) + dhg * dec_all[:, gs]
            dacol = dacol + _dot01_r(dyg * y_off, ext3_g)
            xw = xg * wx_all[:, gs]
            dxw = _dot(bgb, dhgb)
            dx_state = dxw * wx_all[:, gs]
            dwgt = dwgt + _dot01_r(dxw * xg, ext3_g)
            dbt = _dot(dhgb, xw.astype(BF16), NT)
            hh = _colsum8(dhg * hg)
            hh16 = jnp.concatenate([hh, jnp.zeros_like(hh)], axis=0)
            delast = delast + jnp.sum(_dot01_r(hh16, ext3_g), axis=0, keepdims=True)
            dgm = jnp.zeros((CHUNK, CHUNK), F32)
            for r in range(hpg):
                hd = g * hpg + r
                cs = slice(r * SSM_HEAD_DIM, (r + 1) * SSM_HEAD_DIM)
                acol_r, arow_r = acol[:, hd:hd + 1], arow[hd:hd + 1, :]
                dtrow_r, dtcol_r = dtrow[hd:hd + 1, :], dt[:, hd:hd + 1]
                lm = jnp.where(causal, jnp.exp(jnp.where(causal, acol_r - arow_r, 0.0)), 0.0)
                lmt = jnp.where(upper, jnp.exp(jnp.where(upper, arow_r - acol_r, 0.0)), 0.0)
                wt = gmt * lmt * dtcol_r
                dy_r = dyg[:, cs].astype(BF16)
                dx_r = _dot(wt.astype(BF16), dy_r)
                dw = _dot(dy_r, xg[:, cs].astype(BF16), NT)
                t1 = dw * lm
                dgm = dgm + t1 * dtrow_r
                q1 = t1 * gm
                m = q1 * dtrow_r
                dacol = dacol + jnp.sum(m, axis=1, keepdims=True) * (lane_id == hd).astype(F32)
                darowf = darowf - (sub_id == hd).astype(F32) * jnp.sum(m, axis=0, keepdims=True)
                ddtrowf = ddtrowf + (sub_id == hd).astype(F32) * jnp.sum(q1, axis=0, keepdims=True)
                dact_ref[:, pl.ds(hd * SSM_HEAD_DIM, SSM_HEAD_DIM)] = (
                    dx_r + dx_state[:, cs] + dyg[:, cs] * dsk_ref[:, pl.ds(hd * SSM_HEAD_DIM, SSM_HEAD_DIM)])
            dgmb = dgm.astype(BF16)
            dact_ref[:, pl.ds(SSM_WIDTH + g * SSM_STATE, SSM_STATE)] = dbt.T + _dot(dgm.T.astype(BF16), cgb)
            dact_ref[:, pl.ds(SSM_WIDTH + 2 * SSM_STATE + g * SSM_STATE, SSM_STATE)] = dcg + _dot(dgmb, bgb)
            dh_s[:, gs] = dh_in
        t = dwgt * wgt
        dalast = jnp.sum(t, axis=0, keepdims=True) + delast * elast
        dacol_tot = dacol - t + darowf.T + (sub_id == CHUNK - 1).astype(F32) * dalast
        dda = _dot01_l(trit3_ref[...], dacol_tot)
        ddt = dwgt * wgt0 + ddtrowf.T + dda * a
        ddtr = jnp.where(valid, ddt * _sigmoid(dtr_), 0.0)
        ddtr_ref[...] = ddtr
        da_ref[...] += _colsum8(dda * dt) * a
        dbias_ref[...] += _colsum8(ddtr)

    rev = lambda c: nsteps - 1 - c
    rb = cps * CHUNK
    xs_spec = pl.BlockSpec((rb, SSM_WIDTH), lambda c: (rev(c), 0))
    dact, ddtr, da8, dbias8, *received = pl.pallas_call(
        body, name="ssd_bwd", grid=(nsteps,),
        in_specs=[xs_spec, xs_spec,
                  pl.BlockSpec((rb, 2 * SSM_STATE), lambda c: (rev(c), nb)),
                  pl.BlockSpec((rb, 2 * SSM_STATE), lambda c: (rev(c), nb + 1)),
                  pl.BlockSpec((rb, LANE), lambda c: (rev(c), 0)),
                  pl.BlockSpec((cps, SSM_STATE, SSM_WIDTH), lambda c: (rev(c), 0, 0)),
                  _fullspec((1, LANE)), _fullspec((1, LANE)), _fullspec((1, SSM_WIDTH)),
                  _fullspec((CHUNK, CHUNK)), _fullspec((CHUNK, 3 * CHUNK)), _fullspec((CHUNK, 3 * CHUNK)),
                  _fullspec((3 * LANE, SSM_WIDTH)), _fullspec((SSM_GROUPS, 3 * gw, LANE))] + [ANY] * na,
        out_specs=[pl.BlockSpec((rb, CONV_DIM), lambda c: (rev(c), 0)), pl.BlockSpec((rb, LANE), lambda c: (rev(c), 0)),
                   _fullspec((8, LANE)), _fullspec((8, LANE))] + [ANY] * na,
        out_shape=[_sds((lp, CONV_DIM), F32), _sds((lp, LANE), F32), _sds((8, LANE), F32), _sds((8, LANE), F32)]
        + [_sds(e.shape, e.dtype) for e in exchange],
        scratch_shapes=[pltpu.VMEM((SSM_STATE, SSM_WIDTH), F32)] + (_gather_scratch(na) if na else []),
        compiler_params=_cp(("arbitrary",)),
    )(dy, xact, xact, xact, dtr, hprev, dt_bias, a_log, dskip, _tri_mat(), _x3(_tri_mat(), 1), _x3(_tri_mat().T, 1),
      _x3(_expand_mat(), 0), jnp.stack([_x3(_expand_mat().T[g * gw:(g + 1) * gw], 0) for g in range(SSM_GROUPS)]),
      *exchange)
    return dact, ddtr, da8, dbias8, received


def _conv_bwd(dact, xbc, cw, cb):
    lp, c = xbc.shape
    t8 = ROWB // 8
    nb = lp // ROWB

    def body(d_ref, dnext_ref, x_ref, prev_ref, next_ref, w_ref, b_ref, dx_ref, dw_ref, db_ref, xb, dp):
        i = pl.program_id(0)
        last = i == nb - 1
        xb[pl.ds(0, 8), :] = jnp.where(i > 0, prev_ref[...], 0.0)
        xb[pl.ds(8, ROWB), :] = x_ref[...]
        xb[pl.ds(8 + ROWB, 8), :] = jnp.where(last, 0.0, next_ref[...])

        @pl.when(i == 0)
        def _():
            dw_ref[...] = jnp.zeros_like(dw_ref)
            db_ref[...] = jnp.zeros_like(db_ref)

        sub = lax.broadcasted_iota(jnp.int32, (8, 1), 0)
        x0 = 8 - (CONV_K - 1)

        def strip(s, carry):
            cs = pl.ds(pl.multiple_of(s * LANE, LANE), LANE)
            w, b = w_ref[:, cs], b_ref[:, cs]

            def dpre_rows(r0, n, d):
                xs = _shifted_rows(xb, r0, n, cs, [x0 + kk for kk in range(CONV_K)])
                pre = b + sum(w[kk:kk + 1, :] * xs[kk] for kk in range(CONV_K))
                sg = _sigmoid(pre)
                return d * (sg * (1.0 + pre * (1.0 - sg))), xs

            dws = [jnp.zeros((8, LANE), F32) for _ in range(CONV_K)]
            dbs = jnp.zeros((8, LANE), F32)
            for r0 in range(0, ROWB, CONV_ROWS):
                dpre, xs = dpre_rows(r0, CONV_ROWS, d_ref[pl.ds(r0, CONV_ROWS), cs])
                dp[pl.ds(r0, CONV_ROWS), cs] = dpre
                dbs = dbs + _colsum8(dpre)
                for kk in range(CONV_K):
                    dws[kk] = dws[kk] + _colsum8(dpre * xs[kk])
            dp[pl.ds(ROWB, 8), cs] = dpre_rows(ROWB, 8, jnp.where(last, 0.0, dnext_ref[:, cs]))[0]
            dwv = sum(jnp.where(sub == kk, jnp.sum(dws[kk], axis=0, keepdims=True), 0.0) for kk in range(CONV_K))
            dw_ref[:, cs] += dwv
            db_ref[:, cs] += dbs
            for r0 in range(0, ROWB, CONV_ROWS):
                ahead = _shifted_rows(dp, r0, CONV_ROWS, cs, [CONV_K - 1 - kk for kk in range(CONV_K)])
                dx = sum(w[kk:kk + 1, :] * ahead[kk] for kk in range(CONV_K))
                dx_ref[pl.ds(r0, CONV_ROWS), cs] = dx.astype(BF16)
            return carry

        lax.fori_loop(0, c // LANE, strip, 0)

    nxt = lambda i: (jnp.minimum((i + 1) * t8, lp // 8 - 1), 0)
    prv = lambda i: (jnp.maximum(i * t8 - 1, 0), 0)
    return pl.pallas_call(
        body, name="conv_bwd", grid=(nb,),
        in_specs=[_rowspec(ROWB, c), pl.BlockSpec((8, c), nxt), _rowspec(ROWB, c), pl.BlockSpec((8, c), prv),
                  pl.BlockSpec((8, c), nxt), _fullspec((8, c)), _fullspec((1, c))],
        out_specs=[_rowspec(ROWB, c), _fullspec((8, c)), _fullspec((8, c))],
        out_shape=[_sds((lp, c), BF16), _sds((8, c), F32), _sds((8, c), F32)],
        scratch_shapes=[pltpu.VMEM((ROWB + 16, c), F32), pltpu.VMEM((ROWB + 8, c), F32)],
        compiler_params=_cp(("arbitrary",)),
    )(dact, dact, xbc, xbc, xbc, cw, cb)


def _flash_bwd(q, k, v, datt, lse_row, delta_row, cos, sa, sb):
    lp = q.shape[0]
    nk = lp // ROWB

    def body(k_ref, v_ref, q_ref, do_ref, lse_ref, dl_ref, cos_ref, sa_ref, sb_ref, dq_ref, dk_ref, dv_ref,
             dq_acc, dk_acc, dv_acc):
        j = pl.program_id(1)

        @pl.when(j == 0)
        def _():
            dq_acc[...] = jnp.zeros_like(dq_acc)

        dk_acc[...] = jnp.zeros_like(dk_acc)
        dv_acc[...] = jnp.zeros_like(dv_acc)

        def tile(i, masked, key0=0, nkeys=ROWB):
            keys = pl.ds(key0, nkeys)
            kb, vb = k_ref[keys, :], v_ref[keys, :]
            off = pl.multiple_of(i * ROWB, ROWB)
            qb = q_ref[pl.ds(off, ROWB), :]
            dob = do_ref[pl.ds(off, ROWB), :]
            lse_r = lse_ref[0, :, pl.ds(off, ROWB)]
            dl_r = dl_ref[0, :, pl.ds(off, ROWB)]
            st = _dot(kb, qb, NT)
            if masked:
                krow = j * ROWB + key0 + lax.broadcasted_iota(jnp.int32, st.shape, 0)
                qrow = i * ROWB + lax.broadcasted_iota(jnp.int32, st.shape, 1)
                st = jnp.where(_att_ok(qrow, krow), st, NEG)
            pt = jnp.exp2(st - lse_r)
            dv_acc[keys, :] += _dot(pt.astype(BF16), dob)
            dpt = _dot(vb, dob, NT)
            dst = (pt * (dpt - dl_r)).astype(BF16)
            dk_acc[keys, :] += _dot(dst, qb)
            dq_acc[pl.ds(off, ROWB), :] += _dot(dst, kb, ((0,), (0,)))

        @pl.when(j == 0)
        def _():
            _pair_loop(0, nk, lambda i: tile(i, True, ROWB - META_KEYS, META_KEYS), (4, 2))

        odd = jnp.bitwise_and(nk - 1 - j, 3)
        for r in range(4):
            @pl.when((j > 0) & (odd == r))
            def _(r=r):
                tile(j, True)
                for d in range(r):
                    tile(j + 1 + d, False)

        @pl.when(j > 0)
        def _():
            _pair_loop(j + 1 + odd, nk, lambda i: tile(i, False), (16, 8, 4))

        dk_ref[...] = (dk_acc[...] * LN2).astype(BF16)
        dv_ref[...] = dv_acc[...].astype(BF16)
        dq = dq_acc[pl.ds(pl.multiple_of(j * ROWB, ROWB), ROWB), :] * ATT_SCALE
        half = HEADW // 2
        dq_ref[:, pl.ds(0, half)] = dq[:, :half].astype(BF16)
        dq_ref[:, pl.ds(half, half)] = _rope_t(dq[:, half:], cos_ref[...], sa_ref[...], sb_ref[...]).astype(BF16)

    stat = pl.BlockSpec((1, 1, lp), lambda h, j: (h, 0, 0))
    blk = pl.BlockSpec((ROWB, HEADW), lambda h, j: (j, h))
    tab = pl.BlockSpec((ROWB, HEADW // 2), lambda h, j: (j, 0))
    return pl.pallas_call(
        body, name="flash_bwd", grid=(ATT_HEADS, nk),
        in_specs=[blk, pl.BlockSpec((ROWB, V_HEAD), lambda h, j: (j, 2 * h)),
                  pl.BlockSpec((lp, HEADW), lambda h, j: (0, h)), pl.BlockSpec((lp, V_HEAD), lambda h, j: (0, h)),
                  stat, stat, tab, tab, tab],
        out_specs=[blk, blk, pl.BlockSpec((ROWB, V_HEAD), lambda h, j: (j, h))],
        out_shape=[_sds((lp, ATT_HEADS * HEADW), BF16), _sds((lp, ATT_HEADS * HEADW), BF16),
                   _sds((lp, ATT_HEADS * V_HEAD), BF16)],
        scratch_shapes=[pltpu.VMEM((lp, HEADW), F32), pltpu.VMEM((ROWB, HEADW), F32), pltpu.VMEM((ROWB, V_HEAD), F32)],
        compiler_params=_cp(("arbitrary", "arbitrary")),
    )(k, v, q, datt, lse_row, delta_row, cos, sa, sb)


def _qkv_bwd(dqp, dk, dv, cq, ckv, cos, sa, sb, gq, gkv, wq, wk, wv, dz, dxbc, ddtr):
    lp = cq.shape[0]
    qw = ATT_HEADS * HEADW

    def body(dqp_ref, dk_ref, dv_ref, cq_ref, ckv_ref, cos_ref, sa_ref, sb_ref, gq_ref, gkv_ref, wq_ref, wk_ref, wv_ref,
             dz_ref, dxbc_ref, ddtr_ref, dp_ref, dgq_ref, dgkv_ref):
        dcq, dgq = _rms_bwd(cq_ref[...], gq_ref[...], _dot(dqp_ref[...], wq_ref[...], NT))
        dp_ref[:, pl.ds(PC_Q, Q_LORA)] = dcq.astype(BF16)
        dkb = dk_ref[...]
        half = HEADW // 2
        dksum = sum(dkb[:, hh * HEADW + half:(hh + 1) * HEADW].astype(F32) for hh in range(ATT_HEADS))
        dp_ref[:, pl.ds(PC_KR, half)] = jnp.zeros((ROWB, half), BF16)
        dp_ref[:, pl.ds(PC_KR + half, half)] = _rope_t(dksum, cos_ref[...], sa_ref[...], sb_ref[...]).astype(BF16)
        dckvn = _dot(dkb, wk_ref[...], NT) + _dot(dv_ref[...], wv_ref[...], NT)
        dckv, dgkv = _rms_bwd(ckv_ref[...], gkv_ref[...], dckvn)
        dp_ref[:, pl.ds(PC_KV, KV_LORA)] = dckv.astype(BF16)
        dp_ref[:, pl.ds(PC_Z, SSM_WIDTH)] = dz_ref[...]
        dp_ref[:, pl.ds(PC_XBC, CONV_DIM)] = dxbc_ref[...]
        dp_ref[:, pl.ds(PC_DT, LANE)] = ddtr_ref[...].astype(BF16)
        _acc(dgq_ref, _colsum8(dgq))
        _acc(dgkv_ref, _colsum8(dgkv))

    return pl.pallas_call(
        body, name="qkv_bwd", grid=(lp // ROWB,),
        in_specs=[_rowspec(ROWB, qw), _rowspec(ROWB, qw), _rowspec(ROWB, ATT_HEADS * V_HEAD),
                  _rowspec(ROWB, Q_LORA), _rowspec(ROWB, KV_LORA)] + [_rowspec(ROWB, HEADW // 2)] * 3
        + [_fullspec((1, Q_LORA)), _fullspec((1, KV_LORA)), _fullspec((Q_LORA, qw)), _fullspec((KV_LORA, qw)),
           _fullspec((KV_LORA, ATT_HEADS * V_HEAD)), _rowspec(ROWB, SSM_WIDTH), _rowspec(ROWB, CONV_DIM),
           _rowspec(ROWB, LANE)],
        out_specs=[_rowspec(ROWB, PROJ_W), _fullspec((8, Q_LORA)), _fullspec((8, KV_LORA))],
        out_shape=[_sds((lp, PROJ_W), BF16), _sds((8, Q_LORA), F32), _sds((8, KV_LORA), F32)],
        compiler_params=_cp(("arbitrary",)),
    )(dqp, dk, dv, cq, ckv, cos, sa, sb, gq, gkv, wq, wk, wv, dz, dxbc, ddtr)


def _in_bwd(dproj, head, x, dh1, g, w_all, exchange=()):
    lp = dh1.shape[0]
    nsteps = lp // ROWB
    na = len(exchange)

    def body(dp_ref, head_ref, x_ref, dh1_ref, g_ref, w_ref, *rest):
        xin, (dx_ref, dhead_ref, dg_ref), xout, sems = rest[:na], rest[na:na + 3], rest[na + 3:2 * na + 3], rest[2 * na + 3:]
        step = pl.program_id(0)
        if na:
            x_start, x_finish = _exchange_ops(xin, xout, *sems)
            pl.when(step == 0)(x_start)
        dx, dg = _rms_bwd(_h_block(head_ref, x_ref), g_ref[...], _dot(dp_ref[...], w_ref[...], NT))
        dh = dh1_ref[...] + dx

        @pl.when(step == 0)
        def _():
            dhead_ref[...] = dh

        @pl.when(step > 0)
        def _():
            dx_ref[...] = dh

        _acc(dg_ref, _colsum8(dg))
        if na:
            pl.when(step == nsteps - 1)(x_finish)

    dx, dhead, dg8, *received = pl.pallas_call(
        body, name="in_bwd", grid=(nsteps,),
        in_specs=[_rowspec(ROWB, PROJ_W), _fullspec((ROWB, D_MODEL)), _xspec(), _rowspec(ROWB, D_MODEL),
                  _fullspec((1, D_MODEL)), _fullspec((D_MODEL, PROJ_W))] + [ANY] * na,
        out_specs=[_xspec(), _fullspec((ROWB, D_MODEL)), _fullspec((8, D_MODEL))] + [ANY] * na,
        out_shape=[_sds(x.shape, F32), _sds((ROWB, D_MODEL), F32), _sds((8, D_MODEL), F32)]
        + [_sds(e.shape, e.dtype) for e in exchange],
        scratch_shapes=_gather_scratch(na) if na else [],
        compiler_params=_cp(("arbitrary",)),
    )(dproj, head, x, dh1, g, w_all, *exchange)
    return dx, dhead, dg8, received


def _tile_of(n, cap=1024):
    return max(t for t in range(LANE, min(n, cap) + 1, LANE) if n % t == 0)


def _matmul_tn(name, a, b):
    rows, kd = a.shape
    nd = b.shape[1]
    tk, tn = _tile_of(kd), _tile_of(nd)
    rb = 3 * ROWB if rows % (3 * ROWB) == 0 else ROWB

    def body(a_ref, b_ref, o_ref):
        @pl.when(pl.program_id(2) == 0)
        def _():
            o_ref[...] = jnp.zeros_like(o_ref)

        o_ref[...] += _dot(a_ref[...], b_ref[...], ((0,), (0,)))

    return pl.pallas_call(
        body, name=name, grid=(kd // tk, nd // tn, rows // rb),
        in_specs=[pl.BlockSpec((rb, tk), lambda i, j, r: (r, i)), pl.BlockSpec((rb, tn), lambda i, j, r: (r, j))],
        out_specs=pl.BlockSpec((tk, tn), lambda i, j, r: (i, j)), out_shape=_sds((kd, nd), F32),
        compiler_params=_cp(("arbitrary", "arbitrary", "arbitrary")),
    )(a, b)


def _local_backward(head, x, f, exchange_late=False):
    p = f["p"]
    g = {}
    row = lambda v: _row1(v)
    s8 = lambda v: jnp.sum(v, axis=0)
    dh1, du, df, dgpre, dgpost = _mlp_bwd(f["dh2"], f["f"], f["h1"], f["u"], p["w_mlp_up"], p["w_mlp_down"],
                                          row(p["norm_mlp_pre"]), row(p["norm_mlp_post"]))
    g["norm_mlp_pre"], g["norm_mlp_post"] = s8(dgpre), s8(dgpost)
    g["w_mlp_up"] = _matmul_tn("dw_mlp_up", f["hn2"], du)
    g["w_mlp_down"] = _matmul_tn("dw_mlp_down", f["a"], df)
    dmix, datt, dssm, dgmp, delta = _out_bwd(dh1, f["mix"], f["att"], p["w_out"], row(p["norm_mix_post"]))
    g["norm_mix_post"] = s8(dgmp)
    g["w_out"] = jnp.concatenate([_matmul_tn("dw_out_att", f["att"], dmix), _matmul_tn("dw_out_ssm", f["ssm"], dmix)], axis=0)
    dy, dz, dgn, dd = _ssd_post_bwd(dssm, f["y"], f["xact"], f["z"], f["dskip"], row(p["ssm_norm"]))
    g["ssm_norm"] = s8(dgn)
    g["d_skip"] = s8(dd).reshape(SSM_HEADS, SSM_HEAD_DIM).sum(axis=1)
    dact, ddtr, da8, dbias8, received = _ssd_bwd(
        dy, f["xact"], f["dtr"], f["hprev"], f["dt_bias"], f["a_log"], f["dskip"], f["seq_rows"],
        exchange=[_to_chunks(n, g[n]).astype(BF16) for n in LATE] if exchange_late else ())
    g["a_log"], g["dt_bias"] = s8(da8)[:SSM_HEADS], s8(dbias8)[:SSM_HEADS]
    dxbc, dcw8, dcb8 = _conv_bwd(dact, f["xbc"], f["cw"], row(p["conv_b"]))
    g["conv_w"], g["conv_b"] = dcw8[:CONV_K], s8(dcb8)
    dqp, dkb, dv = _flash_bwd(f["q"], f["k"], f["v"], datt, f["lse"], delta, *f["rope"])
    dproj, dgq, dgkv = _qkv_bwd(dqp, dkb, dv, f["cq"], f["ckv"], *f["rope"], row(p["q_a_norm"]), row(p["kv_a_norm"]),
                                f["wq"], f["wk"], f["wv"], dz, dxbc, ddtr)
    g["q_a_norm"], g["kv_a_norm"] = s8(dgq), s8(dgkv)
    dwq = _matmul_tn("dw_q_up", f["cqn"], dqp).reshape(Q_LORA, ATT_HEADS, HEADW)
    g["w_q_up"] = dwq[:, :, :QK_NOPE + QK_ROPE].reshape(Q_LORA, -1)
    dwk = _matmul_tn("dw_k_up", f["ckvn"], dkb).reshape(KV_LORA, ATT_HEADS, HEADW)[:, :, :QK_NOPE]
    dwv = _matmul_tn("dw_v_up", f["ckvn"], dv).reshape(KV_LORA, ATT_HEADS, V_HEAD)
    g["w_kv_up"] = jnp.concatenate([dwk, dwv], axis=2).reshape(KV_LORA, -1)
    dwa = _matmul_tn("dw_in", f["hn"], dproj)
    g["w_in"] = jnp.concatenate([dwa[:, PC_Q:PC_KR], dwa[:, PC_KR + QK_NOPE:PC_KR + QK_NOPE + QK_ROPE],
                                 dwa[:, PC_Z:PC_DT + SSM_HEADS]], axis=1)
    dx, dhead, dgin, received_mid = _in_bwd(
        dproj, head, x, dh1, row(p["norm_mix_pre"]), f["w_all"],
        exchange=[_to_chunks(n, g[n]).astype(BF16) for n in MID] if exchange_late else ())
    g["norm_mix_pre"] = s8(dgin)
    g["meta_tokens"] = dhead[PADF:]
    return dx, g, dict(zip(LATE + MID, list(received) + list(received_mid)))


BIG = {"w_in": ((D_MODEL, IN_WIDTH), 1), "w_q_up": ((Q_LORA, ATT_HEADS * (QK_NOPE + QK_ROPE)), 1),
       "w_kv_up": ((KV_LORA, ATT_HEADS * (QK_NOPE + V_HEAD)), 1), "w_out": ((2 * D_MODEL, D_MODEL), 0),
       "w_mlp_up": ((D_MODEL, D_FF), 1), "w_mlp_down": ((D_FF, D_MODEL), 0), "conv_w": ((CONV_K, CONV_DIM), 1),
       "meta_tokens": ((N_META, D_MODEL), 1)}
SMALL = {"norm_mix_pre": D_MODEL, "q_a_norm": Q_LORA, "kv_a_norm": KV_LORA, "conv_b": CONV_DIM, "dt_bias": SSM_HEADS,
         "a_log": SSM_HEADS, "d_skip": SSM_HEADS, "ssm_norm": SSM_WIDTH, "norm_mix_post": D_MODEL,
         "norm_mlp_pre": D_MODEL, "norm_mlp_post": D_MODEL}
WEIGHT_ORDER = ("meta_tokens", "norm_mix_pre", "w_in", "q_a_norm", "w_q_up", "kv_a_norm", "w_kv_up", "conv_w", "conv_b",
                "dt_bias", "a_log", "d_skip", "ssm_norm", "w_out", "norm_mix_post", "norm_mlp_pre", "w_mlp_up",
                "w_mlp_down", "norm_mlp_post")
ADAM_ROWS = 256


def _shard_shape(name):
    shape, ax = BIG[name]
    return tuple(d // N_DEV if a == ax else d for a, d in enumerate(shape))


SMALL_ROWS = -(-sum(SMALL.values()) // (LANE * 8)) * 8


def _pack(flats, rows):
    v = jnp.concatenate([f.reshape(-1) for f in flats])
    return jnp.pad(v, (0, rows * LANE - v.shape[0])).reshape(rows, LANE)


def _unpack(packed, shapes):
    v = packed.reshape(-1)
    out, o = [], 0
    for s in shapes:
        n = math.prod(s)
        out.append(v[o:o + n].reshape(s))
        o += n
    return out


def _to_chunks(name, full):
    shape, ax = BIG[name]
    if ax == 0:
        return full.reshape((N_DEV,) + _shard_shape(name))
    k, n = shape
    return full.reshape(k, N_DEV, n // N_DEV).transpose(1, 0, 2)


def _from_shards(name, shards):
    shape, ax = BIG[name]
    if ax == 0:
        return shards.reshape(shape)
    return shards.transpose(1, 0, 2).reshape(shape)


def _peer(k):
    x, y, c = lax.axis_index("x"), lax.axis_index("y"), lax.axis_index("c")
    px = 1 - x if k & 4 else x
    py = 1 - y if k & 2 else y
    pc = 1 - c if k & 1 else c
    return (px, py, pc), 4 * px + 2 * py + pc


def _gather_ops(x_refs, out_refs, send_sems, recv_sems, local_sems):
    na = len(x_refs)
    chips = (4, 2, 6)

    def copy(a, n, block, to, src=None):
        return pltpu.make_async_remote_copy(
            src_ref=out_refs[a].at[block] if src is None else src, dst_ref=out_refs[a].at[block],
            send_sem=send_sems.at[7 * a + n], recv_sem=recv_sems.at[7 * a + n], device_id=to, device_id_type=MESH)

    def mine():
        me = _peer(0)[1]
        return [pltpu.make_async_copy(x_refs[a], out_refs[a].at[me], local_sems.at[a]) for a in range(na)]

    def first():
        me, sibling = _peer(0)[1], _peer(1)[0]
        out = [copy(a, 0, me, sibling, src=x_refs[a]) for a in range(na)]
        return out + [copy(a, 1 + n, me, _peer(k)[0], src=x_refs[a]) for n, k in enumerate(chips) for a in range(na)]

    def passed():
        sibling = _peer(1)[0]
        return [copy(a, 4 + n, _peer(k)[1], sibling) for n, k in enumerate(chips) for a in range(na)]

    def start():
        for cp in mine() + first():
            cp.start()

    def forward():
        sibling = _peer(1)[0]
        fwd = passed()
        for n, k in enumerate(chips):
            for a in range(na):
                copy(a, 1 + n, _peer(k)[1], sibling).wait_recv()
                fwd[n * na + a].start()

    def finish():
        sibling = _peer(1)[0]
        for a in range(na):
            copy(a, 0, _peer(1)[1], sibling).wait_recv()
        for n, k in enumerate(chips):
            for a in range(na):
                copy(a, 4 + n, _peer(k | 1)[1], sibling).wait_recv()
        for cp in first() + passed():
            cp.wait_send()
        for cp in mine():
            cp.wait()

    return start, forward, finish


def _gather_scratch(na):
    return [pltpu.SemaphoreType.DMA((7 * na,)), pltpu.SemaphoreType.DMA((7 * na,)), pltpu.SemaphoreType.DMA((na,))]


def _all_gather(shards):
    na = len(shards)

    def body(*refs):
        for step in _gather_ops(refs[:na], refs[na:2 * na], *refs[2 * na:]):
            step()

    return pl.pallas_call(
        body, name="all_gather_weights", out_shape=[_sds((N_DEV,) + s.shape, s.dtype) for s in shards],
        in_specs=[ANY] * na, out_specs=[ANY] * na, scratch_shapes=_gather_scratch(na),
    )(*shards)


def _exchange(chunks, small):
    na = len(chunks) + 1

    def body(*refs):
        for step in _exchange_ops(refs[:na], refs[na:2 * na], *refs[2 * na:], whole=(na - 1,)):
            step()

    arrays = list(chunks) + [small]
    return pl.pallas_call(
        body, name="exchange_grads",
        out_shape=[_sds(c.shape, c.dtype) for c in chunks] + [_sds((N_DEV,) + small.shape, small.dtype)],
        in_specs=[ANY] * na, out_specs=[ANY] * na, scratch_shapes=_gather_scratch(na),
    )(*arrays)


def _exchange_ops(in_refs, out_refs, send_sems, recv_sems, local_sems, whole=()):
    na = len(in_refs)

    def src(a, idx):
        return in_refs[a] if a in whole else in_refs[a].at[idx]

    def own():
        me = _peer(0)[1]
        return [pltpu.make_async_copy(src(a, me), out_refs[a].at[me], local_sems.at[a]) for a in range(na)]

    def copy(a, k, sending):
        me = _peer(0)[1]
        to, idx = _peer(k)
        return pltpu.make_async_remote_copy(
            src_ref=src(a, idx if sending else me), dst_ref=out_refs[a].at[me if sending else idx],
            send_sem=send_sems.at[7 * a + k - 1], recv_sem=recv_sems.at[7 * a + k - 1],
            device_id=to, device_id_type=MESH)

    def sent():
        return [copy(a, k, True) for k in range(1, N_DEV) for a in range(na)]

    def start():
        for cp in own() + sent():
            cp.start()

    def finish():
        for k in range(1, N_DEV):
            for a in range(na):
                copy(a, k, False).wait_recv()
        for cp in sent():
            cp.wait_send()
        for cp in own():
            cp.wait()

    return start, finish


def _reduce_adamw(name, recv, w, m, v):
    rows, cols = w.shape
    blk = ADAM_ROWS if rows % ADAM_ROWS == 0 else rows
    c1 = 1.0 - ADAM_B1 ** ADAM_STEP
    c2 = 1.0 - ADAM_B2 ** ADAM_STEP

    def body(r_ref, w_ref, m_ref, v_ref, g_ref, d_ref, nm_ref, nv_ref):
        g = r_ref[0].astype(F32)
        for s in range(1, N_DEV):
            g = g + r_ref[s].astype(F32)
        g_ref[...] = g
        m_ = ADAM_B1 * m_ref[...] + (1.0 - ADAM_B1) * g
        v_ = ADAM_B2 * v_ref[...] + (1.0 - ADAM_B2) * (g * g)
        nm_ref[...] = m_
        nv_ref[...] = v_
        d_ref[...] = -ADAM_LR * ((m_ / c1) / (jnp.sqrt(v_ / c2) + ADAM_EPS) + ADAM_WD * w_ref[...])

    spec = _rowspec(blk, cols)
    return pl.pallas_call(
        body, name="reduce_adamw_" + name, grid=(rows // blk,),
        in_specs=[pl.BlockSpec((N_DEV, blk, cols), lambda i: (0, i, 0)), spec, spec, spec],
        out_specs=[spec] * 4, out_shape=[_sds((rows, cols), F32)] * 4,
        compiler_params=_cp(("arbitrary",)),
    )(recv, w, m, v)


def kernel(x, meta_tokens, norm_mix_pre, w_in, q_a_norm, w_q_up, kv_a_norm, w_kv_up, conv_w, conv_b, dt_bias, a_log, d_skip, ssm_norm, w_out, norm_mix_post, norm_mlp_pre, w_mlp_up, w_mlp_down, norm_mlp_post, loss_target, m_meta_tokens, m_norm_mix_pre, m_w_in, m_q_a_norm, m_w_q_up, m_kv_a_norm, m_w_kv_up, m_conv_w, m_conv_b, m_dt_bias, m_a_log, m_d_skip, m_ssm_norm, m_w_out, m_norm_mix_post, m_norm_mlp_pre, m_w_mlp_up, m_w_mlp_down, m_norm_mlp_post, v_meta_tokens, v_norm_mix_pre, v_w_in, v_q_a_norm, v_w_q_up, v_kv_a_norm, v_w_kv_up, v_conv_w, v_conv_b, v_dt_bias, v_a_log, v_d_skip, v_ssm_norm, v_w_out, v_norm_mix_post, v_norm_mlp_pre, v_w_mlp_up, v_w_mlp_down, v_norm_mlp_post):
    w = dict(meta_tokens=meta_tokens, norm_mix_pre=norm_mix_pre, w_in=w_in, q_a_norm=q_a_norm, w_q_up=w_q_up,
             kv_a_norm=kv_a_norm, w_kv_up=w_kv_up, conv_w=conv_w, conv_b=conv_b, dt_bias=dt_bias, a_log=a_log,
             d_skip=d_skip, ssm_norm=ssm_norm, w_out=w_out, norm_mix_post=norm_mix_post, norm_mlp_pre=norm_mlp_pre,
             w_mlp_up=w_mlp_up, w_mlp_down=w_mlp_down, norm_mlp_post=norm_mlp_post)
    m = dict(meta_tokens=m_meta_tokens, norm_mix_pre=m_norm_mix_pre, w_in=m_w_in, q_a_norm=m_q_a_norm, w_q_up=m_w_q_up,
             kv_a_norm=m_kv_a_norm, w_kv_up=m_w_kv_up, conv_w=m_conv_w, conv_b=m_conv_b, dt_bias=m_dt_bias,
             a_log=m_a_log, d_skip=m_d_skip, ssm_norm=m_ssm_norm, w_out=m_w_out, norm_mix_post=m_norm_mix_post,
             norm_mlp_pre=m_norm_mlp_pre, w_mlp_up=m_w_mlp_up, w_mlp_down=m_w_mlp_down, norm_mlp_post=m_norm_mlp_post)
    v = dict(meta_tokens=v_meta_tokens, norm_mix_pre=v_norm_mix_pre, w_in=v_w_in, q_a_norm=v_q_a_norm, w_q_up=v_w_q_up,
             kv_a_norm=v_kv_a_norm, w_kv_up=v_w_kv_up, conv_w=v_conv_w, conv_b=v_conv_b, dt_bias=v_dt_bias,
             a_log=v_a_log, d_skip=v_d_skip, ssm_norm=v_ssm_norm, w_out=v_w_out, norm_mix_post=v_norm_mix_post,
             norm_mlp_pre=v_norm_mlp_pre, w_mlp_up=v_w_mlp_up, w_mlp_down=v_w_mlp_down, norm_mlp_post=v_norm_mlp_post)
    big_names = [n for n in WEIGHT_ORDER if n in BIG]
    small_names = [n for n in WEIGHT_ORDER if n in SMALL]
    shard = lambda d, n: d[n].reshape(_shard_shape(n))

    f32_names = ("conv_w", "meta_tokens")
    early = [n for n in big_names if n not in LATE]
    gathered = _all_gather([shard(w, n).astype(F32 if n in f32_names else BF16) for n in early])
    p = {n: w[n].reshape(-1) for n in small_names}
    p.update({n: _from_shards(n, s) for n, s in zip(early, gathered)})
    head = jnp.concatenate([jnp.zeros((PADF, D_MODEL), F32), p["meta_tokens"]], axis=0)
    f = _local_forward(head, x[0], loss_target[0], p, late={n: shard(w, n).astype(BF16) for n in LATE})
    dx, g, recv_of = _local_backward(head, x[0], f, exchange_late=True)
    grad_x = dx[None]
    loss = lax.psum(f["loss"], ("x", "y", "c"))

    rest = [n for n in big_names if n not in LATE + MID]
    small = _pack([g[n] for n in small_names], SMALL_ROWS)
    *recv_rest, recv_small = _exchange([_to_chunks(n, g[n]) for n in rest], small)
    recv_of.update(zip(rest, recv_rest))

    outs = {}
    kinds = ("grad", "delta", "new_m", "new_v")
    for n, recv in ((n, recv_of[n]) for n in big_names):
        for kind, arr in zip(kinds, _reduce_adamw(n, recv, shard(w, n), shard(m, n), shard(v, n))):
            outs[kind, n] = arr.reshape(w[n].shape)
    packed = [_pack([d[n] for n in small_names], SMALL_ROWS) for d in (w, m, v)]
    for kind, arr in zip(kinds, _reduce_adamw("small", recv_small, *packed)):
        for n, val in zip(small_names, _unpack(arr, [(SMALL[n],) for n in small_names])):
            outs[kind, n] = val.reshape(w[n].shape)
    return (loss, grad_x) + tuple(outs[kind, n] for kind in ("grad", "delta", "new_m", "new_v") for n in WEIGHT_ORDER)
```

```python
import math

import jax
import jax.numpy as jnp
import numpy as np
from jax import lax
from jax.experimental import pallas as pl
from jax.experimental.pallas import tpu as pltpu

F32 = jnp.float32
BF16 = jnp.bfloat16

D_MODEL = 1024
N_META = 16
EPS = 1e-6
ATT_HEADS = 8
Q_LORA = 384
KV_LORA = 256
QK_NOPE = 128
QK_ROPE = 64
V_HEAD = 128
ROPE_THETA = 10000.0
SSM_HEADS = 16
SSM_HEAD_DIM = 64
SSM_WIDTH = 1024
SSM_GROUPS = 2
SSM_STATE = 128
CONV_K = 4
CHUNK = 128
CONV_DIM = 1536
D_FF = 4096
IN_SPLITS = (Q_LORA, KV_LORA, QK_ROPE, SSM_WIDTH, CONV_DIM, SSM_HEADS)
IN_WIDTH = sum(IN_SPLITS)
ADAM_LR, ADAM_B1, ADAM_B2, ADAM_EPS, ADAM_WD, ADAM_STEP = 0.001, 0.9, 0.999, 1e-08, 0.01, 10

LANE = 128
ROWB = 512
PADF = ROWB - N_META
HEADW = 256
PC_Q, PC_KV, PC_KR, PC_Z, PC_XBC, PC_DT, PROJ_W = 0, 384, 640, 896, 1920, 3456, 3584
NEG = -1e30
N_DEV = 8
VMEM_LIMIT = 56 * 1024 * 1024
MESH = pl.DeviceIdType.MESH


def _cp(sem, vmem=VMEM_LIMIT, **kw):
    return pltpu.CompilerParams(dimension_semantics=sem, vmem_limit_bytes=vmem, **kw)


def _dot(a, b, dims=((1,), (0,))):
    return lax.dot_general(a, b, (dims, ((), ())), preferred_element_type=F32)


def _bdot(a, b, dims=((1,), (0,))):
    return _dot(a.astype(BF16), b.astype(BF16), dims)


NT = ((1,), (1,))


def _rms_fwd(x, w):
    r = lax.rsqrt(jnp.mean(x * x, axis=-1, keepdims=True) + EPS)
    return (x * r) * w


def _rms_bwd(x, w, dy):
    r = lax.rsqrt(jnp.mean(x * x, axis=-1, keepdims=True) + EPS)
    xh = x * r
    g = dy * w
    dx = r * (g - xh * jnp.mean(g * xh, axis=-1, keepdims=True))
    return dx, dy * xh


def _sigmoid(x):
    return 0.5 * jnp.tanh(0.5 * x) + 0.5


def _colsum8(x):
    t, c = x.shape
    return jnp.sum(x.reshape(t // 8, 8, c), axis=0)


def _rowspec(t, c, cb=0):
    return pl.BlockSpec((t, c), lambda i: (i, cb))


def _fullspec(shape):
    n = len(shape)
    return pl.BlockSpec(shape, lambda i: (0,) * n)


def _sds(shape, dt):
    return jax.ShapeDtypeStruct(shape, dt)


def _acc(ref, val):
    @pl.when(pl.program_id(0) == 0)
    def _():
        ref[...] = jnp.zeros_like(ref)

    ref[...] += val


def _xspec():
    return pl.BlockSpec((ROWB, D_MODEL), lambda i: (jnp.maximum(i - 1, 0), 0))


def _h_block(head_ref, x_ref):
    return jnp.where(pl.program_id(0) == 0, head_ref[...], x_ref[...])


def _norm_in_proj(head, x, g, w_all):
    lp = head.shape[0] + x.shape[0]

    def body(head_ref, x_ref, g_ref, w_ref, hn_ref, cq_ref, ckv_ref, kr_ref, z_ref, xbc_ref, dt_ref):
        hn = _rms_fwd(_h_block(head_ref, x_ref), g_ref[...]).astype(BF16)
        hn_ref[...] = hn
        p = _dot(hn, w_ref[...])
        cq_ref[...] = p[:, PC_Q:PC_KV]
        ckv_ref[...] = p[:, PC_KV:PC_KR]
        kr_ref[...] = p[:, PC_KR:PC_Z]
        z_ref[...] = p[:, PC_Z:PC_XBC]
        xbc_ref[...] = p[:, PC_XBC:PC_DT]
        dt_ref[...] = p[:, PC_DT:PROJ_W]

    widths = (Q_LORA, KV_LORA, HEADW, SSM_WIDTH, CONV_DIM, LANE)
    return pl.pallas_call(
        body, name="norm_in_proj", grid=(lp // ROWB,),
        in_specs=[_fullspec((ROWB, D_MODEL)), _xspec(), _fullspec((1, D_MODEL)), _fullspec((D_MODEL, PROJ_W))],
        out_specs=[_rowspec(ROWB, D_MODEL)] + [_rowspec(ROWB, w) for w in widths],
        out_shape=[_sds((lp, D_MODEL), BF16)] + [_sds((lp, w), F32) for w in widths],
        compiler_params=_cp(("arbitrary",)),
    )(head, x, g, w_all)


def _rope(x, cos, sa, sb):
    w = x.shape[1]
    return x * cos + pltpu.roll(x, w - 32, 1) * sa + pltpu.roll(x, 32, 1) * sb


def _rope_t(dy, cos, sa, sb):
    w = dy.shape[1]
    return dy * cos + pltpu.roll(dy * sa, 32, 1) + pltpu.roll(dy * sb, w - 32, 1)


def _qkv(cq, ckv, kr, cos, sa, sb, gq, gkv, wq, wk, wv):
    lp = cq.shape[0]
    qw = ATT_HEADS * HEADW
    half = HEADW // 2
    assert half == QK_NOPE == V_HEAD == LANE

    def body(cq_ref, ckv_ref, kr_ref, cos_ref, sa_ref, sb_ref, gq_ref, gkv_ref, wq_ref, wk_ref, wv_ref,
             q_ref, k_ref, v_ref, cqn_ref, ckvn_ref):
        tabs = [cos_ref[...], sa_ref[...], sb_ref[...]]
        cqn = _rms_fwd(cq_ref[...], gq_ref[...]).astype(BF16)
        ckvn = _rms_fwd(ckv_ref[...], gkv_ref[...]).astype(BF16)
        cqn_ref[...] = cqn
        ckvn_ref[...] = ckvn
        q = _dot(cqn, wq_ref[...])
        kn = _dot(ckvn, wk_ref[...])
        vv = _dot(ckvn, wv_ref[...])
        krope = _rope(kr_ref[:, pl.ds(half, half)], *tabs).astype(BF16)
        ones = jnp.ones((ROWB, half), BF16)
        for hh in range(ATT_HEADS):
            lo, hi, src = pl.ds(hh * HEADW, half), pl.ds(hh * HEADW + half, half), slice(hh * half, (hh + 1) * half)
            q_ref[:, lo] = (q[:, hh * HEADW:hh * HEADW + half] * Q_PRESCALE).astype(BF16)
            q_ref[:, hi] = (_rope(q[:, hh * HEADW + half:(hh + 1) * HEADW], *tabs) * Q_PRESCALE).astype(BF16)
            k_ref[:, lo] = kn[:, src].astype(BF16)
            k_ref[:, hi] = krope
            v_ref[:, lo] = vv[:, src].astype(BF16)
            v_ref[:, hi] = ones

    return pl.pallas_call(
        body, name="qkv", grid=(lp // ROWB,),
        in_specs=[_rowspec(ROWB, Q_LORA), _rowspec(ROWB, KV_LORA), _rowspec(ROWB, HEADW)]
        + [_rowspec(ROWB, half)] * 3
        + [_fullspec((1, Q_LORA)), _fullspec((1, KV_LORA)), _fullspec((Q_LORA, qw)),
           _fullspec((KV_LORA, ATT_HEADS * QK_NOPE)), _fullspec((KV_LORA, ATT_HEADS * V_HEAD))],
        out_specs=[_rowspec(ROWB, qw), _rowspec(ROWB, qw), _rowspec(ROWB, qw),
                   _rowspec(ROWB, Q_LORA), _rowspec(ROWB, KV_LORA)],
        out_shape=[_sds((lp, qw), BF16), _sds((lp, qw), BF16), _sds((lp, qw), BF16),
                   _sds((lp, Q_LORA), BF16), _sds((lp, KV_LORA), BF16)],
        compiler_params=_cp(("arbitrary",)),
    )(cq, ckv, kr, cos, sa, sb, gq, gkv, wq, wk, wv)


ATT_SCALE = (QK_NOPE + QK_ROPE) ** -0.5
LOG2E = 1.4426950408889634
LN2 = 0.6931471805599453
Q_PRESCALE = ATT_SCALE * LOG2E
KVB = 512
META_KEYS = LANE
assert N_META <= META_KEYS


def _att_ok(qrow, krow):
    return (krow <= qrow) & ((krow >= PADF) | (qrow < PADF))


def _lanes(x, n):
    return x if n == 1 else jnp.concatenate([x] * n, axis=1)


def _pair_loop(lo, hi, tile, unrolls=(2,)):
    for u in tuple(unrolls) + (1,):
        n = jnp.maximum(hi - lo, 0)
        trips = n // u

        def many(t, c, u=u, lo=lo):
            for d in range(u):
                tile(lo + u * t + d)
            return c

        lax.fori_loop(0, trips, many, 0)
        lo = lo + trips * u


def _flash_fwd(q, k, v):
    lp = q.shape[0]
    nq = lp // ROWB

    def body(q_ref, k_ref, v_ref, o_ref, lse_ref, acc, m_s):
        i = pl.program_id(1)
        qb = q_ref[...]
        m_s[...] = jnp.full_like(m_s, NEG)
        acc[...] = jnp.zeros_like(acc)

        def tile(j, masked, off=None, nkeys=KVB):
            off = pl.multiple_of(j * KVB, KVB) if off is None else off
            kb = k_ref[pl.ds(off, nkeys), :]
            vb = v_ref[pl.ds(off, nkeys), :]
            s = _dot(qb, kb, NT)
            if masked:
                qrow = i * ROWB + lax.broadcasted_iota(jnp.int32, s.shape, 0)
                krow = off + lax.broadcasted_iota(jnp.int32, s.shape, 1)
                s = jnp.where(_att_ok(qrow, krow), s, NEG)
            m_prev = m_s[...]
            m_new = jnp.maximum(m_prev, jnp.max(s, axis=1, keepdims=True))
            alpha = jnp.exp2(m_prev - m_new)
            p = jnp.exp2(s - _lanes(m_new, nkeys // LANE))
            acc[...] = _lanes(alpha, 2) * acc[...] + _dot(p.astype(BF16), vb)
            m_s[...] = m_new

        def first_tile():
            tile(0, True, off=ROWB - META_KEYS, nkeys=META_KEYS)

        @pl.when(i == 0)
        def _():
            first_tile()

        odd = jnp.bitwise_and(jnp.maximum(i - 1, 0), 7)
        for r in range(8):
            @pl.when((i > 0) & (odd == r))
            def _(r=r):
                first_tile()
                tile(i, True)
                for d in range(r):
                    tile(1 + d, False)

        _pair_loop(1 + odd, i, lambda j: tile(j, False), (16, 8))
        l = acc[:, V_HEAD:]
        o_ref[...] = (acc[:, :V_HEAD] / l).astype(BF16)
        lse_ref[0] = (m_s[...] + jnp.log2(l)).T[0:1, :]

    return pl.pallas_call(
        body, name="flash_fwd", grid=(ATT_HEADS, nq),
        in_specs=[pl.BlockSpec((ROWB, HEADW), lambda h, i: (i, h)),
                  pl.BlockSpec((lp, HEADW), lambda h, i: (0, h)),
                  pl.BlockSpec((lp, HEADW), lambda h, i: (0, h))],
        out_specs=[pl.BlockSpec((ROWB, V_HEAD), lambda h, i: (i, h)),
                   pl.BlockSpec((1, 1, ROWB), lambda h, i: (h, 0, i))],
        out_shape=[_sds((lp, ATT_HEADS * V_HEAD), BF16), _sds((ATT_HEADS, 1, lp), F32)],
        scratch_shapes=[pltpu.VMEM((ROWB, HEADW), F32), pltpu.VMEM((ROWB, LANE), F32)],
        compiler_params=_cp(("arbitrary", "arbitrary")),
    )(q, k, v)


def _silu(x):
    return x * _sigmoid(x)


CONV_ROWS = 64


def _shifted_rows(ref, start, n, cols, offsets):
    win = ref[pl.ds(start, n + 8), cols]
    return [win[o:o + n] if o % 8 == 0 else pltpu.roll(win, n + 8 - o, 0)[0:n] for o in offsets]


def _conv_fwd(xbc, cw, cb):
    lp, c = xbc.shape
    t8 = ROWB // 8

    def body(x_ref, prev_ref, w_ref, b_ref, o_ref, buf):
        i = pl.program_id(0)
        buf[pl.ds(0, 8), :] = jnp.where(i > 0, prev_ref[...], 0.0)
        buf[pl.ds(8, ROWB), :] = x_ref[...]

        def strip(s, carry):
            cs = pl.ds(pl.multiple_of(s * LANE, LANE), LANE)
            w, b = w_ref[:, cs], b_ref[:, cs]
            for r0 in range(0, ROWB, CONV_ROWS):
                taps = _shifted_rows(buf, r0, CONV_ROWS, cs, [8 - (CONV_K - 1) + kk for kk in range(CONV_K)])
                pre = b + sum(w[kk:kk + 1, :] * taps[kk] for kk in range(CONV_K))
                o_ref[pl.ds(r0, CONV_ROWS), cs] = _silu(pre)
            return carry

        lax.fori_loop(0, c // LANE, strip, 0)

    return pl.pallas_call(
        body, name="conv_fwd", grid=(lp // ROWB,),
        in_specs=[_rowspec(ROWB, c), pl.BlockSpec((8, c), lambda i: (jnp.maximum(i * t8 - 1, 0), 0)),
                  _fullspec((8, c)), _fullspec((1, c))],
        out_specs=_rowspec(ROWB, c), out_shape=_sds((lp, c), F32),
        scratch_shapes=[pltpu.VMEM((ROWB + 8, c), F32)],
        compiler_params=_cp(("arbitrary",)),
    )(xbc, xbc, cw, cb)


def _expand_mat():
    r = np.arange(LANE)[:, None]
    c = np.arange(SSM_WIDTH)[None, :]
    return jnp.asarray((c // SSM_HEAD_DIM == r).astype(np.float32))


def _tri_mat():
    i = np.arange(CHUNK)
    return jnp.asarray((i[:, None] >= i[None, :]).astype(np.float32))


def _x3(m, axis):
    return jnp.concatenate([m.astype(BF16)] * 3, axis=axis)


def _split3(x):
    hi = x.astype(BF16)
    r = x - hi.astype(F32)
    mid = r.astype(BF16)
    return hi, mid, (r - mid.astype(F32)).astype(BF16)


def _dot01_r(x, m3):
    return _dot(jnp.concatenate(_split3(x), axis=1), m3)


def _dot01_l(m3, x):
    return _dot(m3, jnp.concatenate(_split3(x), axis=0))


def _ssd_prep(dt_raw, bias_ref, alog_ref, tri3, c, seq_rows):
    rows = c * CHUNK + lax.broadcasted_iota(jnp.int32, (CHUNK, LANE), 0)
    lanes = lax.broadcasted_iota(jnp.int32, (CHUNK, LANE), 1)
    valid = (rows >= PADF) & (rows < PADF + seq_rows) & (lanes < SSM_HEADS)
    dtr = dt_raw + bias_ref[...]
    sp = jnp.maximum(dtr, 0.0) + jnp.log(1.0 + jnp.exp(-jnp.abs(dtr)))
    dt = jnp.where(valid, sp, 0.0)
    a = -jnp.exp(alog_ref[...])
    acol = _dot01_l(tri3, dt * a)
    return dt, a, acol, valid, dtr


def _row16(v):
    return jnp.broadcast_to(v, (16, v.shape[1]))


def _ssd_fwd(xbc_act, dtr, dt_bias, a_log, seq_rows, gather=()):
    lp = xbc_act.shape[0]
    nc = lp // CHUNK
    cps = ROWB // CHUNK
    nsteps = nc // cps
    gw = SSM_WIDTH // SSM_GROUPS
    hpg = SSM_HEADS // SSM_GROUPS

    na = len(gather)

    def body(x_ref, b_ref, c_ref, dtr_ref, bias_ref, alog_ref, tri_ref, tri3_ref, ex3_ref, *rest):
        gin, (y_ref, hp_ref), gout, h_s, sems = rest[:na], rest[na:na + 2], rest[na + 2:2 * na + 2], rest[2 * na + 2], rest[2 * na + 3:]
        step = pl.program_id(0)

        @pl.when(step == 0)
        def _():
            h_s[...] = jnp.zeros_like(h_s)

        if na:
            g_start, g_forward, g_finish = _gather_ops(gin, gout, *sems)
            pl.when(step == 0)(g_start)
            pl.when(step == nsteps // 2)(g_forward)

        ex3 = ex3_ref[...]
        causal = tri_ref[...] > 0.5
        for cc in range(cps):
            rows = pl.ds(cc * CHUNK, CHUNK)
            dt, a, acol, _, _ = _ssd_prep(dtr_ref[rows, :], bias_ref, alog_ref, tri3_ref[...], step * cps + cc, seq_rows)
            arow = acol.T
            dtrow = dt.T
            alast = acol[CHUNK - 1:CHUNK, :]
            e_all = _dot01_r(jnp.exp(acol), ex3)
            wx_all = _dot01_r(jnp.exp(alast - acol) * dt, ex3)
            dec_all = _dot01_r(_row16(jnp.exp(alast)), ex3)[0:1, :]
            hp_ref[cc] = h_s[...]
            for g in range(SSM_GROUPS):
                gs = slice(g * gw, (g + 1) * gw)
                bg = b_ref[rows, g * SSM_STATE:(g + 1) * SSM_STATE]
                cg = c_ref[rows, g * SSM_STATE:(g + 1) * SSM_STATE].astype(BF16)
                xg = x_ref[rows, gs]
                hg = h_s[:, gs]
                gm = _bdot(cg, bg, NT)
                y_off = _bdot(cg, hg) * e_all[:, gs]
                for r in range(hpg):
                    hd = g * hpg + r
                    seg = acol[:, hd:hd + 1] - arow[hd:hd + 1, :]
                    lm = jnp.where(causal, jnp.exp(jnp.where(causal, seg, 0.0)), 0.0)
                    w = gm * lm * dtrow[hd:hd + 1, :]
                    cs = slice(r * SSM_HEAD_DIM, (r + 1) * SSM_HEAD_DIM)
                    y_ref[rows, pl.ds(hd * SSM_HEAD_DIM, SSM_HEAD_DIM)] = _bdot(w, xg[:, cs]) + y_off[:, cs]
                st = _bdot(bg.T, xg * wx_all[:, gs])
                h_s[:, gs] = hg * dec_all[:, gs] + st

        if na:
            pl.when(step == nsteps - 1)(g_finish)

    xs_spec = pl.BlockSpec((ROWB, SSM_WIDTH), lambda c: (c, 0))
    b_spec = pl.BlockSpec((ROWB, 2 * SSM_STATE), lambda c: (c, SSM_WIDTH // (2 * SSM_STATE)))
    c_spec = pl.BlockSpec((ROWB, 2 * SSM_STATE), lambda c: (c, SSM_WIDTH // (2 * SSM_STATE) + 1))
    y, hprev, *gathered = pl.pallas_call(
        body, name="ssd_fwd", grid=(nsteps,),
        in_specs=[xs_spec, b_spec, c_spec, pl.BlockSpec((ROWB, LANE), lambda c: (c, 0)),
                  _fullspec((1, LANE)), _fullspec((1, LANE)), _fullspec((CHUNK, CHUNK)), _fullspec((CHUNK, 3 * CHUNK)),
                  _fullspec((3 * LANE, SSM_WIDTH))] + [ANY] * na,
        out_specs=[xs_spec, pl.BlockSpec((cps, SSM_STATE, SSM_WIDTH), lambda c: (c, 0, 0))] + [ANY] * na,
        out_shape=[_sds((lp, SSM_WIDTH), F32), _sds((nc, SSM_STATE, SSM_WIDTH), F32)]
        + [_sds((N_DEV,) + s.shape, s.dtype) for s in gather],
        scratch_shapes=[pltpu.VMEM((SSM_STATE, SSM_WIDTH), F32)] + (_gather_scratch(na) if na else []),
        compiler_params=_cp(("arbitrary",)),
    )(xbc_act, xbc_act, xbc_act, dtr, dt_bias, a_log, _tri_mat(), _x3(_tri_mat(), 1), _x3(_expand_mat(), 0), *gather)
    return y, hprev, gathered


def _group_mean(x):
    gw = SSM_WIDTH // SSM_GROUPS
    parts = [jnp.broadcast_to(jnp.mean(x[:, g * gw:(g + 1) * gw], axis=-1, keepdims=True), (x.shape[0], gw))
             for g in range(SSM_GROUPS)]
    return jnp.concatenate(parts, axis=1)


def _out_proj(att, y, xbc_act, z, dskip, gnorm, head, x, w_out, g_post):
    lp = att.shape[0]

    def body(a_ref, y_ref, xs_ref, z_ref, d_ref, gn_ref, head_ref, x_ref, w_ref, g_ref, ssm_ref, mix_ref, h1_ref):
        gt = (y_ref[...] + d_ref[...] * xs_ref[...]) * _silu(z_ref[...])
        r = lax.rsqrt(_group_mean(gt * gt) + EPS)
        ssm = ((gt * r) * gn_ref[...]).astype(BF16)
        ssm_ref[...] = ssm
        mix = _dot(a_ref[...], w_ref[pl.ds(0, 1024), :]) + _dot(ssm, w_ref[pl.ds(1024, 1024), :])
        mix_ref[...] = mix
        h1_ref[...] = _h_block(head_ref, x_ref) + _rms_fwd(mix, g_ref[...])

    return pl.pallas_call(
        body, name="out_proj", grid=(lp // ROWB,),
        in_specs=[_rowspec(ROWB, 1024)] * 4 + [_fullspec((1, SSM_WIDTH))] * 2
        + [_fullspec((ROWB, D_MODEL)), _xspec(), _fullspec((2048, D_MODEL)), _fullspec((1, D_MODEL))],
        out_specs=[_rowspec(ROWB, SSM_WIDTH)] + [_rowspec(ROWB, D_MODEL)] * 2,
        out_shape=[_sds((lp, SSM_WIDTH), BF16)] + [_sds((lp, D_MODEL), F32)] * 2,
        compiler_params=_cp(("arbitrary",)),
    )(att, y, xbc_act, z, dskip, gnorm, head, x, w_out, g_post)


def _resident(w_hbm, w_vmem, sem):
    @pl.when(pl.program_id(0) == 0)
    def _():
        cp = pltpu.make_async_copy(w_hbm, w_vmem, sem)
        cp.start()
        cp.wait()


ANY = pl.BlockSpec(memory_space=pl.ANY)


def _mlp_fwd(h1, tgt, w_up, w_down, g_pre, g_post, seq_rows):
    lp = h1.shape[0]

    def body(h1_ref, t_ref, wu_hbm, wd_hbm, gpre_ref, gpost_ref, hn2_ref, u_ref, a_ref, f_ref, dh2_ref, loss_ref,
             wu, wd, sems):
        _resident(wu_hbm, wu, sems.at[0])
        _resident(wd_hbm, wd, sems.at[1])
        i = pl.program_id(0)
        h1_ = h1_ref[...]
        hn2 = _rms_fwd(h1_, gpre_ref[...]).astype(BF16)
        hn2_ref[...] = hn2
        u = jnp.maximum(_dot(hn2, wu[...]), 0.0)
        u_ref[...] = u.astype(BF16)
        a = (u * u).astype(BF16)
        a_ref[...] = a
        f = _dot(a, wd[...])
        f_ref[...] = f
        h2 = h1_ + _rms_fwd(f, gpost_ref[...])
        rows = i * ROWB + lax.broadcasted_iota(jnp.int32, (ROWB, 1), 0)
        real = (rows >= PADF + N_META) & (rows < PADF + seq_rows)
        err = jnp.where(real, h2 - t_ref[...], 0.0)
        dh2_ref[...] = err * (1.0 / D_MODEL)
        _acc(loss_ref, _colsum8(err * err))

    return pl.pallas_call(
        body, name="mlp_fwd", grid=(lp // ROWB,),
        in_specs=[_rowspec(ROWB, D_MODEL), _xspec()] + [ANY, ANY] + [_fullspec((1, D_MODEL))] * 2,
        out_specs=[_rowspec(ROWB, D_MODEL), _rowspec(ROWB, D_FF), _rowspec(ROWB, D_FF)] + [_rowspec(ROWB, D_MODEL)] * 2
        + [_fullspec((8, D_MODEL))],
        out_shape=[_sds((lp, D_MODEL), BF16), _sds((lp, D_FF), BF16), _sds((lp, D_FF), BF16), _sds((lp, D_MODEL), F32),
                   _sds((lp, D_MODEL), F32), _sds((8, D_MODEL), F32)],
        scratch_shapes=[pltpu.VMEM((D_MODEL, D_FF), BF16), pltpu.VMEM((D_FF, D_MODEL), BF16), pltpu.SemaphoreType.DMA((2,))],
        compiler_params=_cp(("arbitrary",)),
    )(h1, tgt, w_up, w_down, g_pre, g_post)


def _pad_cols(w, width):
    return jnp.pad(w, ((0, 0), (0, width - w.shape[1])))


def _layout_weights(w_in, w_q_up, w_kv_up):
    o = np.cumsum((0,) + IN_SPLITS)
    pieces = [w_in[:, o[k]:o[k + 1]] for k in range(6)]
    kr = jnp.pad(pieces[2], ((0, 0), (QK_NOPE, HEADW - QK_NOPE - QK_ROPE)))
    w_all = jnp.concatenate([pieces[0], pieces[1], kr, pieces[3], pieces[4], _pad_cols(pieces[5], LANE)], axis=1)
    wq = jnp.pad(w_q_up.reshape(Q_LORA, ATT_HEADS, QK_NOPE + QK_ROPE), ((0, 0), (0, 0), (0, HEADW - QK_NOPE - QK_ROPE)))
    wkv = w_kv_up.reshape(KV_LORA, ATT_HEADS, QK_NOPE + V_HEAD)
    wk = jnp.pad(wkv[:, :, :QK_NOPE], ((0, 0), (0, 0), (0, HEADW - QK_NOPE)))
    wv = wkv[:, :, QK_NOPE:]
    return (w_all, wq.reshape(Q_LORA, -1), wk.reshape(KV_LORA, -1), wv.reshape(KV_LORA, -1),
            wkv[:, :, :QK_NOPE].reshape(KV_LORA, -1))


def _rope_tables(lp):
    pos = jnp.maximum(jnp.arange(lp, dtype=jnp.int32) - PADF, 0).astype(F32)
    inv_freq = ROPE_THETA ** (-jnp.arange(0, QK_ROPE, 2, dtype=F32) / QK_ROPE)
    ang = pos[:, None] * inv_freq[None, :]
    cos, sin = jnp.cos(ang), jnp.sin(ang)
    z32, z64 = jnp.zeros((lp, 32), F32), jnp.zeros((lp, 64), F32)
    cos_t = jnp.concatenate([cos, cos, jnp.ones((lp, 64), F32)], axis=1)
    sa = jnp.concatenate([-sin, z32, z64], axis=1)
    sb = jnp.concatenate([z32, sin, z64], axis=1)
    return cos_t, sa, sb


def _row1(v, width=None):
    v = v.reshape(1, -1).astype(F32)
    return v if width is None else _pad_cols(v, width)


LATE = ("w_out", "w_mlp_up", "w_mlp_down")
MID = ("w_in", "w_q_up", "w_kv_up", "conv_w")


def _local_forward(head, x, tgt, p, late=None):
    assert head.shape[0] == ROWB and x.shape[0] % ROWB == 0
    lp = ROWB + x.shape[0]
    seq_rows = N_META + x.shape[0]
    f = {"seq_rows": seq_rows}
    w_all, wq, wk, wv, wkn = _layout_weights(p["w_in"], p["w_q_up"], p["w_kv_up"])
    f.update(w_all=w_all, wq=wq, wk=wk, wv=wv)
    f["hn"], cq, ckv, kr, f["z"], f["xbc"], f["dtr"] = _norm_in_proj(head, x, _row1(p["norm_mix_pre"]), w_all)
    f.update(cq=cq, ckv=ckv)
    f["rope"] = _rope_tables(lp)
    f["q"], f["k"], f["v"], f["cqn"], f["ckvn"] = _qkv(cq, ckv, kr, *f["rope"], _row1(p["q_a_norm"]),
                                                   _row1(p["kv_a_norm"]), wq, wkn, wv)
    f["att"], f["lse"] = _flash_fwd(f["q"], f["k"], f["v"])
    f["cw"] = jnp.pad(p["conv_w"].astype(F32), ((0, 8 - CONV_K), (0, 0)))
    f["xact"] = _conv_fwd(f["xbc"], f["cw"], _row1(p["conv_b"]))
    f["dt_bias"], f["a_log"] = _row1(p["dt_bias"], LANE), _row1(p["a_log"], LANE)
    f["y"], f["hprev"], gathered = _ssd_fwd(f["xact"], f["dtr"], f["dt_bias"], f["a_log"], seq_rows,
                                            gather=[late[n] for n in LATE] if late else ())
    p = {**p, **{n: _from_shards(n, s) for n, s in zip(LATE, gathered)}}
    f["p"] = p
    f["dskip"] = jnp.repeat(p["d_skip"].reshape(-1).astype(F32), SSM_HEAD_DIM).reshape(1, SSM_WIDTH)
    f["ssm"], f["mix"], f["h1"] = _out_proj(f["att"], f["y"], f["xact"], f["z"], f["dskip"], _row1(p["ssm_norm"]),
                                            head, x, p["w_out"], _row1(p["norm_mix_post"]))
    f["hn2"], f["u"], f["a"], f["f"], f["dh2"], loss8 = _mlp_fwd(
        f["h1"], tgt, p["w_mlp_up"], p["w_mlp_down"], _row1(p["norm_mlp_pre"]), _row1(p["norm_mlp_post"]), seq_rows)
    f["loss"] = 0.5 * jnp.sum(loss8) / D_MODEL
    return f


MLPB = 256


def _mlp_bwd(dh2, f, h1, u, w_up, w_down, g_pre, g_post):
    lp = h1.shape[0]

    def body(dh2_ref, f_ref, h1_ref, u_ref, wu_hbm, wd_hbm, gpre_ref, gpost_ref,
             dh1_ref, du_ref, df_ref, dgpre_ref, dgpost_ref, wu, wd, sems):
        _resident(wu_hbm, wu, sems.at[0])
        _resident(wd_hbm, wd, sems.at[1])
        dh2_ = dh2_ref[...]
        df, dgp = _rms_bwd(f_ref[...], gpost_ref[...], dh2_)
        dfb = df.astype(BF16)
        df_ref[...] = dfb
        da = _dot(dfb, wd[...], NT)
        du = (da * (2.0 * u_ref[...].astype(F32))).astype(BF16)
        du_ref[...] = du
        dhn2 = _dot(du, wu[...], NT)
        dx, dgq = _rms_bwd(h1_ref[...], gpre_ref[...], dhn2)
        dh1_ref[...] = dh2_ + dx
        _acc(dgpre_ref, _colsum8(dgq))
        _acc(dgpost_ref, _colsum8(dgp))

    return pl.pallas_call(
        body, name="mlp_bwd", grid=(lp // MLPB,),
        in_specs=[_rowspec(MLPB, D_MODEL)] * 3 + [_rowspec(MLPB, D_FF), ANY, ANY] + [_fullspec((1, D_MODEL))] * 2,
        out_specs=[_rowspec(MLPB, D_MODEL), _rowspec(MLPB, D_FF), _rowspec(MLPB, D_MODEL),
                   _fullspec((8, D_MODEL)), _fullspec((8, D_MODEL))],
        out_shape=[_sds((lp, D_MODEL), F32), _sds((lp, D_FF), BF16), _sds((lp, D_MODEL), BF16),
                   _sds((8, D_MODEL), F32), _sds((8, D_MODEL), F32)],
        scratch_shapes=[pltpu.VMEM((D_MODEL, D_FF), BF16), pltpu.VMEM((D_FF, D_MODEL), BF16), pltpu.SemaphoreType.DMA((2,))],
        compiler_params=_cp(("arbitrary",)),
    )(dh2, f, h1, u, w_up, w_down, g_pre, g_post)


def _out_bwd(dh1, mix, att, w_out, g_post):
    lp = dh1.shape[0]

    def body(dh1_ref, mix_ref, att_ref, w_ref, g_ref, dmix_ref, datt_ref, dssm_ref, dg_ref, dl_ref):
        dmix, dg = _rms_bwd(mix_ref[...], g_ref[...], dh1_ref[...])
        dmb = dmix.astype(BF16)
        dmix_ref[...] = dmb
        datt = _dot(dmb, w_ref[pl.ds(0, 1024), :], NT).astype(BF16)
        datt_ref[...] = datt
        dssm_ref[...] = _dot(dmb, w_ref[pl.ds(1024, 1024), :], NT)
        _acc(dg_ref, _colsum8(dg))
        prod = datt.astype(F32) * att_ref[...].astype(F32)
        for hh in range(ATT_HEADS):
            d = jnp.sum(prod[:, hh * V_HEAD:(hh + 1) * V_HEAD], axis=1, keepdims=True)
            dl_ref[hh] = jnp.broadcast_to(d, (ROWB, LANE)).T[0:1, :]

    return pl.pallas_call(
        body, name="out_bwd", grid=(lp // ROWB,),
        in_specs=[_rowspec(ROWB, D_MODEL)] * 3 + [_fullspec((2048, D_MODEL)), _fullspec((1, D_MODEL))],
        out_specs=[_rowspec(ROWB, D_MODEL)] * 3 + [_fullspec((8, D_MODEL)),
                                                   pl.BlockSpec((ATT_HEADS, 1, ROWB), lambda i: (0, 0, i))],
        out_shape=[_sds((lp, D_MODEL), BF16), _sds((lp, 1024), BF16), _sds((lp, 1024), F32), _sds((8, D_MODEL), F32),
                   _sds((ATT_HEADS, 1, lp), F32)],
        compiler_params=_cp(("arbitrary",)),
    )(dh1, mix, att, w_out, g_post)


def _ssd_post_bwd(dssm, y, xact, z, dskip, gnorm):
    lp = y.shape[0]

    def body(do_ref, y_ref, x_ref, z_ref, d_ref, g_ref, dy_ref, dz_ref, dg_ref, dd_ref):
        z_, x_ = z_ref[...], x_ref[...]
        sg = _sigmoid(z_)
        sz = z_ * sg
        y2 = y_ref[...] + d_ref[...] * x_
        gt = y2 * sz
        r = lax.rsqrt(_group_mean(gt * gt) + EPS)
        gh = gt * r
        do = do_ref[...]
        dgh = do * g_ref[...]
        dgt = r * (dgh - gh * _group_mean(dgh * gh))
        dy2 = dgt * sz
        dy_ref[...] = dy2
        dz_ref[...] = (dgt * y2 * (sg * (1.0 + z_ * (1.0 - sg)))).astype(BF16)
        _acc(dg_ref, _colsum8(do * gh))
        _acc(dd_ref, _colsum8(dy2 * x_))

    return pl.pallas_call(
        body, name="ssd_post_bwd", grid=(lp // ROWB,),
        in_specs=[_rowspec(ROWB, SSM_WIDTH)] * 4 + [_fullspec((1, SSM_WIDTH))] * 2,
        out_specs=[_rowspec(ROWB, SSM_WIDTH)] * 2 + [_fullspec((8, SSM_WIDTH))] * 2,
        out_shape=[_sds((lp, SSM_WIDTH), F32), _sds((lp, SSM_WIDTH), BF16), _sds((8, SSM_WIDTH), F32), _sds((8, SSM_WIDTH), F32)],
        compiler_params=_cp(("arbitrary",)),
    )(dssm, y, xact, z, dskip, gnorm)


def _ssd_bwd(dy, xact, dtr, hprev, dt_bias, a_log, dskip, seq_rows, exchange=()):
    lp = xact.shape[0]
    nc = lp // CHUNK
    gw = SSM_WIDTH // SSM_GROUPS
    hpg = SSM_HEADS // SSM_GROUPS
    nb = SSM_WIDTH // (2 * SSM_STATE)
    na = len(exchange)
    cps = ROWB // CHUNK
    nsteps = nc // cps

    def body(dy_ref, x_ref, b_ref, c_ref, dtr_ref, hp_ref, bias_ref, alog_ref, dsk_ref, tri_ref, tri3_ref, trit3_ref,
             ex3_ref, ext3_ref, *rest):
        xin, (dact_ref, ddtr_ref, da_ref, dbias_ref), xout = rest[:na], rest[na:na + 4], rest[na + 4:2 * na + 4]
        dh_s, sems = rest[2 * na + 4], rest[2 * na + 5:]
        step = pl.program_id(0)

        @pl.when(step == 0)
        def _():
            dh_s[...] = jnp.zeros_like(dh_s)
            da_ref[...] = jnp.zeros_like(da_ref)
            dbias_ref[...] = jnp.zeros_like(dbias_ref)

        if na:
            x_start, x_finish = _exchange_ops(xin, xout, *sems)
            pl.when(step == 0)(x_start)

        for lc in reversed(range(cps)):
            rows = pl.ds(lc * CHUNK, CHUNK)
            chunk((nsteps - 1 - step) * cps + lc, dy_ref.at[rows], x_ref.at[rows], b_ref.at[rows], c_ref.at[rows],
                  dtr_ref.at[rows], hp_ref.at[pl.ds(lc, 1)], bias_ref, alog_ref, dsk_ref, tri_ref, tri3_ref, trit3_ref,
                  ex3_ref, ext3_ref, dact_ref.at[rows], ddtr_ref.at[rows], da_ref, dbias_ref, dh_s)

        if na:
            pl.when(step == nsteps - 1)(x_finish)

    def chunk(c, dy_ref, x_ref, b_ref, c_ref, dtr_ref, hp_ref, bias_ref, alog_ref, dsk_ref, tri_ref, tri3_ref, trit3_ref,
              ex3_ref, ext3_ref, dact_ref, ddtr_ref, da_ref, dbias_ref, dh_s):
        tri = tri_ref[...]
        ex3 = ex3_ref[...]
        dt, a, acol, valid, dtr_ = _ssd_prep(dtr_ref[...], bias_ref, alog_ref, tri3_ref[...], c, seq_rows)
        arow = acol.T
        dtrow = dt.T
        alast = acol[CHUNK - 1:CHUNK, :]
        e_all = _dot01_r(jnp.exp(acol), ex3)
        wgt0 = jnp.exp(alast - acol)
        wgt = wgt0 * dt
        wx_all = _dot01_r(wgt, ex3)
        elast = jnp.exp(alast)
        dec_all = _dot01_r(_row16(elast), ex3)[0:1, :]
        causal = tri > 0.5
        upper = tri.T > 0.5
        lane_id = lax.broadcasted_iota(jnp.int32, (1, LANE), 1)
        sub_id = lax.broadcasted_iota(jnp.int32, (CHUNK, 1), 0)
        dacol = jnp.zeros((CHUNK, LANE), F32)
        darowf = jnp.zeros((CHUNK, LANE), F32)
        ddtrowf = jnp.zeros((CHUNK, LANE), F32)
        dwgt = jnp.zeros((CHUNK, LANE), F32)
        delast = jnp.zeros((1, LANE), F32)
        for g in range(SSM_GROUPS):
            gs = slice(g * gw, (g + 1) * gw)
            ext3_g = ext3_ref[g]
            bg = b_ref[:, g * SSM_STATE:(g + 1) * SSM_STATE]
            cg = c_ref[:, g * SSM_STATE:(g + 1) * SSM_STATE]
            bgb, cgb = bg.astype(BF16), cg.astype(BF16)
            xg = x_ref[:, gs]
            dyg = dy_ref[:, gs]
            hg = hp_ref[0, :, gs]
            dhg = dh_s[:, gs]
            hgb, dhgb = hg.astype(BF16), dhg.astype(BF16)
            gm = _dot(cgb, bgb, NT)
            gmt = _dot(bgb, cgb, NT)
            y_off = _dot(cgb, hgb) * e_all[:, gs]
            dy0 = (dyg * e_all[:, gs]).astype(BF16)
            dcg = _dot(dy0, hgb, NT)
            dh_in = _dot(cg.T.astype(BF16), dy0) + dhg * dec_all[:, gs]
            dacol = dacol + _dot01_r(dyg * y_off, ext3_g)
            xw = xg * wx_all[:, gs]
            dxw = _dot(bgb, dhgb)
            dx_state = dxw * wx_all[:, gs]
            dwgt = dwgt + _dot01_r(dxw * xg, ext3_g)
            dbt = _dot(dhgb, xw.astype(BF16), NT)
            hh = _colsum8(dhg * hg)
            hh16 = jnp.concatenate([hh, jnp.zeros_like(hh)], axis=0)
            delast = delast + jnp.sum(_dot01_r(hh16, ext3_g), axis=0, keepdims=True)
            dgm = jnp.zeros((CHUNK, CHUNK), F32)
            for r in range(hpg):
                hd = g * hpg + r
                cs = slice(r * SSM_HEAD_DIM, (r + 1) * SSM_HEAD_DIM)
                acol_r, arow_r = acol[:, hd:hd + 1], arow[hd:hd + 1, :]
                dtrow_r, dtcol_r = dtrow[hd:hd + 1, :], dt[:, hd:hd + 1]
                lm = jnp.where(causal, jnp.exp(jnp.where(causal, acol_r - arow_r, 0.0)), 0.0)
                lmt = jnp.where(upper, jnp.exp(jnp.where(upper, arow_r - acol_r, 0.0)), 0.0)
                wt = gmt * lmt * dtcol_r
                dy_r = dyg[:, cs].astype(BF16)
                dx_r = _dot(wt.astype(BF16), dy_r)
                dw = _dot(dy_r, xg[:, cs].astype(BF16), NT)
                t1 = dw * lm
                dgm = dgm + t1 * dtrow_r
                q1 = t1 * gm
                m = q1 * dtrow_r
                dacol = dacol + jnp.sum(m, axis=1, keepdims=True) * (lane_id == hd).astype(F32)
                darowf = darowf - (sub_id == hd).astype(F32) * jnp.sum(m, axis=0, keepdims=True)
                ddtrowf = ddtrowf + (sub_id == hd).astype(F32) * jnp.sum(q1, axis=0, keepdims=True)
                dact_ref[:, pl.ds(hd * SSM_HEAD_DIM, SSM_HEAD_DIM)] = (
                    dx_r + dx_state[:, cs] + dyg[:, cs] * dsk_ref[:, pl.ds(hd * SSM_HEAD_DIM, SSM_HEAD_DIM)])
            dgmb = dgm.astype(BF16)
            dact_ref[:, pl.ds(SSM_WIDTH + g * SSM_STATE, SSM_STATE)] = dbt.T + _dot(dgm.T.astype(BF16), cgb)
            dact_ref[:, pl.ds(SSM_WIDTH + 2 * SSM_STATE + g * SSM_STATE, SSM_STATE)] = dcg + _dot(dgmb, bgb)
            dh_s[:, gs] = dh_in
        t = dwgt * wgt
        dalast = jnp.sum(t, axis=0, keepdims=True) + delast * elast
        dacol_tot = dacol - t + darowf.T + (sub_id == CHUNK - 1).astype(F32) * dalast
        dda = _dot01_l(trit3_ref[...], dacol_tot)
        ddt = dwgt * wgt0 + ddtrowf.T + dda * a
        ddtr = jnp.where(valid, ddt * _sigmoid(dtr_), 0.0)
        ddtr_ref[...] = ddtr
        da_ref[...] += _colsum8(dda * dt) * a
        dbias_ref[...] += _colsum8(ddtr)

    rev = lambda c: nsteps - 1 - c
    rb = cps * CHUNK
    xs_spec = pl.BlockSpec((rb, SSM_WIDTH), lambda c: (rev(c), 0))
    dact, ddtr, da8, dbias8, *received = pl.pallas_call(
        body, name="ssd_bwd", grid=(nsteps,),
        in_specs=[xs_spec, xs_spec,
                  pl.BlockSpec((rb, 2 * SSM_STATE), lambda c: (rev(c), nb)),
                  pl.BlockSpec((rb, 2 * SSM_STATE), lambda c: (rev(c), nb + 1)),
                  pl.BlockSpec((rb, LANE), lambda c: (rev(c), 0)),
                  pl.BlockSpec((cps, SSM_STATE, SSM_WIDTH), lambda c: (rev(c), 0, 0)),
                  _fullspec((1, LANE)), _fullspec((1, LANE)), _fullspec((1, SSM_WIDTH)),
                  _fullspec((CHUNK, CHUNK)), _fullspec((CHUNK, 3 * CHUNK)), _fullspec((CHUNK, 3 * CHUNK)),
                  _fullspec((3 * LANE, SSM_WIDTH)), _fullspec((SSM_GROUPS, 3 * gw, LANE))] + [ANY] * na,
        out_specs=[pl.BlockSpec((rb, CONV_DIM), lambda c: (rev(c), 0)), pl.BlockSpec((rb, LANE), lambda c: (rev(c), 0)),
                   _fullspec((8, LANE)), _fullspec((8, LANE))] + [ANY] * na,
        out_shape=[_sds((lp, CONV_DIM), F32), _sds((lp, LANE), F32), _sds((8, LANE), F32), _sds((8, LANE), F32)]
        + [_sds(e.shape, e.dtype) for e in exchange],
        scratch_shapes=[pltpu.VMEM((SSM_STATE, SSM_WIDTH), F32)] + (_gather_scratch(na) if na else []),
        compiler_params=_cp(("arbitrary",)),
    )(dy, xact, xact, xact, dtr, hprev, dt_bias, a_log, dskip, _tri_mat(), _x3(_tri_mat(), 1), _x3(_tri_mat().T, 1),
      _x3(_expand_mat(), 0), jnp.stack([_x3(_expand_mat().T[g * gw:(g + 1) * gw], 0) for g in range(SSM_GROUPS)]),
      *exchange)
    return dact, ddtr, da8, dbias8, received


def _conv_bwd(dact, xbc, cw, cb):
    lp, c = xbc.shape
    t8 = ROWB // 8
    nb = lp // ROWB

    def body(d_ref, dnext_ref, x_ref, prev_ref, next_ref, w_ref, b_ref, dx_ref, dw_ref, db_ref, xb, dp):
        i = pl.program_id(0)
        last = i == nb - 1
        xb[pl.ds(0, 8), :] = jnp.where(i > 0, prev_ref[...], 0.0)
        xb[pl.ds(8, ROWB), :] = x_ref[...]
        xb[pl.ds(8 + ROWB, 8), :] = jnp.where(last, 0.0, next_ref[...])

        @pl.when(i == 0)
        def _():
            dw_ref[...] = jnp.zeros_like(dw_ref)
            db_ref[...] = jnp.zeros_like(db_ref)

        sub = lax.broadcasted_iota(jnp.int32, (8, 1), 0)
        x0 = 8 - (CONV_K - 1)

        def strip(s, carry):
            cs = pl.ds(pl.multiple_of(s * LANE, LANE), LANE)
            w, b = w_ref[:, cs], b_ref[:, cs]

            def dpre_rows(r0, n, d):
                xs = _shifted_rows(xb, r0, n, cs, [x0 + kk for kk in range(CONV_K)])
                pre = b + sum(w[kk:kk + 1, :] * xs[kk] for kk in range(CONV_K))
                sg = _sigmoid(pre)
                return d * (sg * (1.0 + pre * (1.0 - sg))), xs

            dws = [jnp.zeros((8, LANE), F32) for _ in range(CONV_K)]
            dbs = jnp.zeros((8, LANE), F32)
            for r0 in range(0, ROWB, CONV_ROWS):
                dpre, xs = dpre_rows(r0, CONV_ROWS, d_ref[pl.ds(r0, CONV_ROWS), cs])
                dp[pl.ds(r0, CONV_ROWS), cs] = dpre
                dbs = dbs + _colsum8(dpre)
                for kk in range(CONV_K):
                    dws[kk] = dws[kk] + _colsum8(dpre * xs[kk])
            dp[pl.ds(ROWB, 8), cs] = dpre_rows(ROWB, 8, jnp.where(last, 0.0, dnext_ref[:, cs]))[0]
            dwv = sum(jnp.where(sub == kk, jnp.sum(dws[kk], axis=0, keepdims=True), 0.0) for kk in range(CONV_K))
            dw_ref[:, cs] += dwv
            db_ref[:, cs] += dbs
            for r0 in range(0, ROWB, CONV_ROWS):
                ahead = _shifted_rows(dp, r0, CONV_ROWS, cs, [CONV_K - 1 - kk for kk in range(CONV_K)])
                dx = sum(w[kk:kk + 1, :] * ahead[kk] for kk in range(CONV_K))
                dx_ref[pl.ds(r0, CONV_ROWS), cs] = dx.astype(BF16)
            return carry

        lax.fori_loop(0, c // LANE, strip, 0)

    nxt = lambda i: (jnp.minimum((i + 1) * t8, lp // 8 - 1), 0)
    prv = lambda i: (jnp.maximum(i * t8 - 1, 0), 0)
    return pl.pallas_call(
        body, name="conv_bwd", grid=(nb,),
        in_specs=[_rowspec(ROWB, c), pl.BlockSpec((8, c), nxt), _rowspec(ROWB, c), pl.BlockSpec((8, c), prv),
                  pl.BlockSpec((8, c), nxt), _fullspec((8, c)), _fullspec((1, c))],
        out_specs=[_rowspec(ROWB, c), _fullspec((8, c)), _fullspec((8, c))],
        out_shape=[_sds((lp, c), BF16), _sds((8, c), F32), _sds((8, c), F32)],
        scratch_shapes=[pltpu.VMEM((ROWB + 16, c), F32), pltpu.VMEM((ROWB + 8, c), F32)],
        compiler_params=_cp(("arbitrary",)),
    )(dact, dact, xbc, xbc, xbc, cw, cb)


def _flash_bwd(q, k, v, datt, lse_row, delta_row, cos, sa, sb):
    lp = q.shape[0]
    nk = lp // ROWB

    def body(k_ref, v_ref, q_ref, do_ref, lse_ref, dl_ref, cos_ref, sa_ref, sb_ref, dq_ref, dk_ref, dv_ref,
             dq_acc, dk_acc, dv_acc):
        j = pl.program_id(1)

        @pl.when(j == 0)
        def _():
            dq_acc[...] = jnp.zeros_like(dq_acc)

        dk_acc[...] = jnp.zeros_like(dk_acc)
        dv_acc[...] = jnp.zeros_like(dv_acc)

        def tile(i, masked, key0=0, nkeys=ROWB):
            keys = pl.ds(key0, nkeys)
            kb, vb = k_ref[keys, :], v_ref[keys, :]
            off = pl.multiple_of(i * ROWB, ROWB)
            qb = q_ref[pl.ds(off, ROWB), :]
            dob = do_ref[pl.ds(off, ROWB), :]
            lse_r = lse_ref[0, :, pl.ds(off, ROWB)]
            dl_r = dl_ref[0, :, pl.ds(off, ROWB)]
            st = _dot(kb, qb, NT)
            if masked:
                krow = j * ROWB + key0 + lax.broadcasted_iota(jnp.int32, st.shape, 0)
                qrow = i * ROWB + lax.broadcasted_iota(jnp.int32, st.shape, 1)
                st = jnp.where(_att_ok(qrow, krow), st, NEG)
            pt = jnp.exp2(st - lse_r)
            dv_acc[keys, :] += _dot(pt.astype(BF16), dob)
            dpt = _dot(vb, dob, NT)
            dst = (pt * (dpt - dl_r)).astype(BF16)
            dk_acc[keys, :] += _dot(dst, qb)
            dq_acc[pl.ds(off, ROWB), :] += _dot(dst, kb, ((0,), (0,)))

        @pl.when(j == 0)
        def _():
            _pair_loop(0, nk, lambda i: tile(i, True, ROWB - META_KEYS, META_KEYS), (4, 2))

        odd = jnp.bitwise_and(nk - 1 - j, 3)
        for r in range(4):
            @pl.when((j > 0) & (odd == r))
            def _(r=r):
                tile(j, True)
                for d in range(r):
                    tile(j + 1 + d, False)

        @pl.when(j > 0)
        def _():
            _pair_loop(j + 1 + odd, nk, lambda i: tile(i, False), (16, 8, 4))

        dk_ref[...] = (dk_acc[...] * LN2).astype(BF16)
        dv_ref[...] = dv_acc[...].astype(BF16)
        dq = dq_acc[pl.ds(pl.multiple_of(j * ROWB, ROWB), ROWB), :] * ATT_SCALE
        half = HEADW // 2
        dq_ref[:, pl.ds(0, half)] = dq[:, :half].astype(BF16)
        dq_ref[:, pl.ds(half, half)] = _rope_t(dq[:, half:], cos_ref[...], sa_ref[...], sb_ref[...]).astype(BF16)

    stat = pl.BlockSpec((1, 1, lp), lambda h, j: (h, 0, 0))
    blk = pl.BlockSpec((ROWB, HEADW), lambda h, j: (j, h))
    tab = pl.BlockSpec((ROWB, HEADW // 2), lambda h, j: (j, 0))
    return pl.pallas_call(
        body, name="flash_bwd", grid=(ATT_HEADS, nk),
        in_specs=[blk, pl.BlockSpec((ROWB, V_HEAD), lambda h, j: (j, 2 * h)),
                  pl.BlockSpec((lp, HEADW), lambda h, j: (0, h)), pl.BlockSpec((lp, V_HEAD), lambda h, j: (0, h)),
                  stat, stat, tab, tab, tab],
        out_specs=[blk, blk, pl.BlockSpec((ROWB, V_HEAD), lambda h, j: (j, h))],
        out_shape=[_sds((lp, ATT_HEADS * HEADW), BF16), _sds((lp, ATT_HEADS * HEADW), BF16),
                   _sds((lp, ATT_HEADS * V_HEAD), BF16)],
        scratch_shapes=[pltpu.VMEM((lp, HEADW), F32), pltpu.VMEM((ROWB, HEADW), F32), pltpu.VMEM((ROWB, V_HEAD), F32)],
        compiler_params=_cp(("arbitrary", "arbitrary")),
    )(k, v, q, datt, lse_row, delta_row, cos, sa, sb)


def _qkv_bwd(dqp, dk, dv, cq, ckv, cos, sa, sb, gq, gkv, wq, wk, wv, dz, dxbc, ddtr):
    lp = cq.shape[0]
    qw = ATT_HEADS * HEADW

    def body(dqp_ref, dk_ref, dv_ref, cq_ref, ckv_ref, cos_ref, sa_ref, sb_ref, gq_ref, gkv_ref, wq_ref, wk_ref, wv_ref,
             dz_ref, dxbc_ref, ddtr_ref, dp_ref, dgq_ref, dgkv_ref):
        dcq, dgq = _rms_bwd(cq_ref[...], gq_ref[...], _dot(dqp_ref[...], wq_ref[...], NT))
        dp_ref[:, pl.ds(PC_Q, Q_LORA)] = dcq.astype(BF16)
        dkb = dk_ref[...]
        half = HEADW // 2
        dksum = sum(dkb[:, hh * HEADW + half:(hh + 1) * HEADW].astype(F32) for hh in range(ATT_HEADS))
        dp_ref[:, pl.ds(PC_KR, half)] = jnp.zeros((ROWB, half), BF16)
        dp_ref[:, pl.ds(PC_KR + half, half)] = _rope_t(dksum, cos_ref[...], sa_ref[...], sb_ref[...]).astype(BF16)
        dckvn = _dot(dkb, wk_ref[...], NT) + _dot(dv_ref[...], wv_ref[...], NT)
        dckv, dgkv = _rms_bwd(ckv_ref[...], gkv_ref[...], dckvn)
        dp_ref[:, pl.ds(PC_KV, KV_LORA)] = dckv.astype(BF16)
        dp_ref[:, pl.ds(PC_Z, SSM_WIDTH)] = dz_ref[...]
        dp_ref[:, pl.ds(PC_XBC, CONV_DIM)] = dxbc_ref[...]
        dp_ref[:, pl.ds(PC_DT, LANE)] = ddtr_ref[...].astype(BF16)
        _acc(dgq_ref, _colsum8(dgq))
        _acc(dgkv_ref, _colsum8(dgkv))

    return pl.pallas_call(
        body, name="qkv_bwd", grid=(lp // ROWB,),
        in_specs=[_rowspec(ROWB, qw), _rowspec(ROWB, qw), _rowspec(ROWB, ATT_HEADS * V_HEAD),
                  _rowspec(ROWB, Q_LORA), _rowspec(ROWB, KV_LORA)] + [_rowspec(ROWB, HEADW // 2)] * 3
        + [_fullspec((1, Q_LORA)), _fullspec((1, KV_LORA)), _fullspec((Q_LORA, qw)), _fullspec((KV_LORA, qw)),
           _fullspec((KV_LORA, ATT_HEADS * V_HEAD)), _rowspec(ROWB, SSM_WIDTH), _rowspec(ROWB, CONV_DIM),
           _rowspec(ROWB, LANE)],
        out_specs=[_rowspec(ROWB, PROJ_W), _fullspec((8, Q_LORA)), _fullspec((8, KV_LORA))],
        out_shape=[_sds((lp, PROJ_W), BF16), _sds((8, Q_LORA), F32), _sds((8, KV_LORA), F32)],
        compiler_params=_cp(("arbitrary",)),
    )(dqp, dk, dv, cq, ckv, cos, sa, sb, gq, gkv, wq, wk, wv, dz, dxbc, ddtr)


def _in_bwd(dproj, head, x, dh1, g, w_all, exchange=()):
    lp = dh1.shape[0]
    nsteps = lp // ROWB
    na = len(exchange)

    def body(dp_ref, head_ref, x_ref, dh1_ref, g_ref, w_ref, *rest):
        xin, (dx_ref, dhead_ref, dg_ref), xout, sems = rest[:na], rest[na:na + 3], rest[na + 3:2 * na + 3], rest[2 * na + 3:]
        step = pl.program_id(0)
        if na:
            x_start, x_finish = _exchange_ops(xin, xout, *sems)
            pl.when(step == 0)(x_start)
        dx, dg = _rms_bwd(_h_block(head_ref, x_ref), g_ref[...], _dot(dp_ref[...], w_ref[...], NT))
        dh = dh1_ref[...] + dx

        @pl.when(step == 0)
        def _():
            dhead_ref[...] = dh

        @pl.when(step > 0)
        def _():
            dx_ref[...] = dh

        _acc(dg_ref, _colsum8(dg))
        if na:
            pl.when(step == nsteps - 1)(x_finish)

    dx, dhead, dg8, *received = pl.pallas_call(
        body, name="in_bwd", grid=(nsteps,),
        in_specs=[_rowspec(ROWB, PROJ_W), _fullspec((ROWB, D_MODEL)), _xspec(), _rowspec(ROWB, D_MODEL),
                  _fullspec((1, D_MODEL)), _fullspec((D_MODEL, PROJ_W))] + [ANY] * na,
        out_specs=[_xspec(), _fullspec((ROWB, D_MODEL)), _fullspec((8, D_MODEL))] + [ANY] * na,
        out_shape=[_sds(x.shape, F32), _sds((ROWB, D_MODEL), F32), _sds((8, D_MODEL), F32)]
        + [_sds(e.shape, e.dtype) for e in exchange],
        scratch_shapes=_gather_scratch(na) if na else [],
        compiler_params=_cp(("arbitrary",)),
    )(dproj, head, x, dh1, g, w_all, *exchange)
    return dx, dhead, dg8, received


def _tile_of(n, cap=1024):
    return max(t for t in range(LANE, min(n, cap) + 1, LANE) if n % t == 0)


def _matmul_tn(name, a, b):
    rows, kd = a.shape
    nd = b.shape[1]
    tk, tn = _tile_of(kd), _tile_of(nd)
    rb = 3 * ROWB if rows % (3 * ROWB) == 0 else ROWB

    def body(a_ref, b_ref, o_ref):
        @pl.when(pl.program_id(2) == 0)
        def _():
            o_ref[...] = jnp.zeros_like(o_ref)

        o_ref[...] += _dot(a_ref[...], b_ref[...], ((0,), (0,)))

    return pl.pallas_call(
        body, name=name, grid=(kd // tk, nd // tn, rows // rb),
        in_specs=[pl.BlockSpec((rb, tk), lambda i, j, r: (r, i)), pl.BlockSpec((rb, tn), lambda i, j, r: (r, j))],
        out_specs=pl.BlockSpec((tk, tn), lambda i, j, r: (i, j)), out_shape=_sds((kd, nd), F32),
        compiler_params=_cp(("arbitrary", "arbitrary", "arbitrary")),
    )(a, b)


def _local_backward(head, x, f, exchange_late=False):
    p = f["p"]
    g = {}
    row = lambda v: _row1(v)
    s8 = lambda v: jnp.sum(v, axis=0)
    dh1, du, df, dgpre, dgpost = _mlp_bwd(f["dh2"], f["f"], f["h1"], f["u"], p["w_mlp_up"], p["w_mlp_down"],
                                          row(p["norm_mlp_pre"]), row(p["norm_mlp_post"]))
    g["norm_mlp_pre"], g["norm_mlp_post"] = s8(dgpre), s8(dgpost)
    g["w_mlp_up"] = _matmul_tn("dw_mlp_up", f["hn2"], du)
    g["w_mlp_down"] = _matmul_tn("dw_mlp_down", f["a"], df)
    dmix, datt, dssm, dgmp, delta = _out_bwd(dh1, f["mix"], f["att"], p["w_out"], row(p["norm_mix_post"]))
    g["norm_mix_post"] = s8(dgmp)
    g["w_out"] = jnp.concatenate([_matmul_tn("dw_out_att", f["att"], dmix), _matmul_tn("dw_out_ssm", f["ssm"], dmix)], axis=0)
    dy, dz, dgn, dd = _ssd_post_bwd(dssm, f["y"], f["xact"], f["z"], f["dskip"], row(p["ssm_norm"]))
    g["ssm_norm"] = s8(dgn)
    g["d_skip"] = s8(dd).reshape(SSM_HEADS, SSM_HEAD_DIM).sum(axis=1)
    dact, ddtr, da8, dbias8, received = _ssd_bwd(
        dy, f["xact"], f["dtr"], f["hprev"], f["dt_bias"], f["a_log"], f["dskip"], f["seq_rows"],
        exchange=[_to_chunks(n, g[n]).astype(BF16) for n in LATE] if exchange_late else ())
    g["a_log"], g["dt_bias"] = s8(da8)[:SSM_HEADS], s8(dbias8)[:SSM_HEADS]
    dxbc, dcw8, dcb8 = _conv_bwd(dact, f["xbc"], f["cw"], row(p["conv_b"]))
    g["conv_w"], g["conv_b"] = dcw8[:CONV_K], s8(dcb8)
    dqp, dkb, dv = _flash_bwd(f["q"], f["k"], f["v"], datt, f["lse"], delta, *f["rope"])
    dproj, dgq, dgkv = _qkv_bwd(dqp, dkb, dv, f["cq"], f["ckv"], *f["rope"], row(p["q_a_norm"]), row(p["kv_a_norm"]),
                                f["wq"], f["wk"], f["wv"], dz, dxbc, ddtr)
    g["q_a_norm"], g["kv_a_norm"] = s8(dgq), s8(dgkv)
    dwq = _matmul_tn("dw_q_up", f["cqn"], dqp).reshape(Q_LORA, ATT_HEADS, HEADW)
    g["w_q_up"] = dwq[:, :, :QK_NOPE + QK_ROPE].reshape(Q_LORA, -1)
    dwk = _matmul_tn("dw_k_up", f["ckvn"], dkb).reshape(KV_LORA, ATT_HEADS, HEADW)[:, :, :QK_NOPE]
    dwv = _matmul_tn("dw_v_up", f["ckvn"], dv).reshape(KV_LORA, ATT_HEADS, V_HEAD)
    g["w_kv_up"] = jnp.concatenate([dwk, dwv], axis=2).reshape(KV_LORA, -1)
    dwa = _matmul_tn("dw_in", f["hn"], dproj)
    g["w_in"] = jnp.concatenate([dwa[:, PC_Q:PC_KR], dwa[:, PC_KR + QK_NOPE:PC_KR + QK_NOPE + QK_ROPE],
                                 dwa[:, PC_Z:PC_DT + SSM_HEADS]], axis=1)
    dx, dhead, dgin, received_mid = _in_bwd(
        dproj, head, x, dh1, row(p["norm_mix_pre"]), f["w_all"],
        exchange=[_to_chunks(n, g[n]).astype(BF16) for n in MID] if exchange_late else ())
    g["norm_mix_pre"] = s8(dgin)
    g["meta_tokens"] = dhead[PADF:]
    return dx, g, dict(zip(LATE + MID, list(received) + list(received_mid)))


BIG = {"w_in": ((D_MODEL, IN_WIDTH), 1), "w_q_up": ((Q_LORA, ATT_HEADS * (QK_NOPE + QK_ROPE)), 1),
       "w_kv_up": ((KV_LORA, ATT_HEADS * (QK_NOPE + V_HEAD)), 1), "w_out": ((2 * D_MODEL, D_MODEL), 0),
       "w_mlp_up": ((D_MODEL, D_FF), 1), "w_mlp_down": ((D_FF, D_MODEL), 0), "conv_w": ((CONV_K, CONV_DIM), 1),
       "meta_tokens": ((N_META, D_MODEL), 1)}
SMALL = {"norm_mix_pre": D_MODEL, "q_a_norm": Q_LORA, "kv_a_norm": KV_LORA, "conv_b": CONV_DIM, "dt_bias": SSM_HEADS,
         "a_log": SSM_HEADS, "d_skip": SSM_HEADS, "ssm_norm": SSM_WIDTH, "norm_mix_post": D_MODEL,
         "norm_mlp_pre": D_MODEL, "norm_mlp_post": D_MODEL}
WEIGHT_ORDER = ("meta_tokens", "norm_mix_pre", "w_in", "q_a_norm", "w_q_up", "kv_a_norm", "w_kv_up", "conv_w", "conv_b",
                "dt_bias", "a_log", "d_skip", "ssm_norm", "w_out", "norm_mix_post", "norm_mlp_pre", "w_mlp_up",
                "w_mlp_down", "norm_mlp_post")
ADAM_ROWS = 256


def _shard_shape(name):
    shape, ax = BIG[name]
    return tuple(d // N_DEV if a == ax else d for a, d in enumerate(shape))


SMALL_ROWS = -(-sum(SMALL.values()) // (LANE * 8)) * 8


def _pack(flats, rows):
    v = jnp.concatenate([f.reshape(-1) for f in flats])
    return jnp.pad(v, (0, rows * LANE - v.shape[0])).reshape(rows, LANE)


def _unpack(packed, shapes):
    v = packed.reshape(-1)
    out, o = [], 0
    for s in shapes:
        n = math.prod(s)
        out.append(v[o:o + n].reshape(s))
        o += n
    return out


def _to_chunks(name, full):
    shape, ax = BIG[name]
    if ax == 0:
        return full.reshape((N_DEV,) + _shard_shape(name))
    k, n = shape
    return full.reshape(k, N_DEV, n // N_DEV).transpose(1, 0, 2)


def _from_shards(name, shards):
    shape, ax = BIG[name]
    if ax == 0:
        return shards.reshape(shape)
    return shards.transpose(1, 0, 2).reshape(shape)


def _peer(k):
    x, y, c = lax.axis_index("x"), lax.axis_index("y"), lax.axis_index("c")
    px = 1 - x if k & 4 else x
    py = 1 - y if k & 2 else y
    pc = 1 - c if k & 1 else c
    return (px, py, pc), 4 * px + 2 * py + pc


def _gather_ops(x_refs, out_refs, send_sems, recv_sems, local_sems):
    na = len(x_refs)
    chips = (4, 2, 6)

    def copy(a, n, block, to, src=None):
        return pltpu.make_async_remote_copy(
            src_ref=out_refs[a].at[block] if src is None else src, dst_ref=out_refs[a].at[block],
            send_sem=send_sems.at[7 * a + n], recv_sem=recv_sems.at[7 * a + n], device_id=to, device_id_type=MESH)

    def mine():
        me = _peer(0)[1]
        return [pltpu.make_async_copy(x_refs[a], out_refs[a].at[me], local_sems.at[a]) for a in range(na)]

    def first():
        me, sibling = _peer(0)[1], _peer(1)[0]
        out = [copy(a, 0, me, sibling, src=x_refs[a]) for a in range(na)]
        return out + [copy(a, 1 + n, me, _peer(k)[0], src=x_refs[a]) for n, k in enumerate(chips) for a in range(na)]

    def passed():
        sibling = _peer(1)[0]
        return [copy(a, 4 + n, _peer(k)[1], sibling) for n, k in enumerate(chips) for a in range(na)]

    def start():
        for cp in mine() + first():
            cp.start()

    def forward():
        sibling = _peer(1)[0]
        fwd = passed()
        for n, k in enumerate(chips):
            for a in range(na):
                copy(a, 1 + n, _peer(k)[1], sibling).wait_recv()
                fwd[n * na + a].start()

    def finish():
        sibling = _peer(1)[0]
        for a in range(na):
            copy(a, 0, _peer(1)[1], sibling).wait_recv()
        for n, k in enumerate(chips):
            for a in range(na):
                copy(a, 4 + n, _peer(k | 1)[1], sibling).wait_recv()
        for cp in first() + passed():
            cp.wait_send()
        for cp in mine():
            cp.wait()

    return start, forward, finish


def _gather_scratch(na):
    return [pltpu.SemaphoreType.DMA((7 * na,)), pltpu.SemaphoreType.DMA((7 * na,)), pltpu.SemaphoreType.DMA((na,))]


def _all_gather(shards):
    na = len(shards)

    def body(*refs):
        for step in _gather_ops(refs[:na], refs[na:2 * na], *refs[2 * na:]):
            step()

    return pl.pallas_call(
        body, name="all_gather_weights", out_shape=[_sds((N_DEV,) + s.shape, s.dtype) for s in shards],
        in_specs=[ANY] * na, out_specs=[ANY] * na, scratch_shapes=_gather_scratch(na),
    )(*shards)


def _exchange(chunks, small):
    na = len(chunks) + 1

    def body(*refs):
        for step in _exchange_ops(refs[:na], refs[na:2 * na], *refs[2 * na:], whole=(na - 1,)):
            step()

    arrays = list(chunks) + [small]
    return pl.pallas_call(
        body, name="exchange_grads",
        out_shape=[_sds(c.shape, c.dtype) for c in chunks] + [_sds((N_DEV,) + small.shape, small.dtype)],
        in_specs=[ANY] * na, out_specs=[ANY] * na, scratch_shapes=_gather_scratch(na),
    )(*arrays)


def _exchange_ops(in_refs, out_refs, send_sems, recv_sems, local_sems, whole=()):
    na = len(in_refs)

    def src(a, idx):
        return in_refs[a] if a in whole else in_refs[a].at[idx]

    def own():
        me = _peer(0)[1]
        return [pltpu.make_async_copy(src(a, me), out_refs[a].at[me], local_sems.at[a]) for a in range(na)]

    def copy(a, k, sending):
        me = _peer(0)[1]
        to, idx = _peer(k)
        return pltpu.make_async_remote_copy(
            src_ref=src(a, idx if sending else me), dst_ref=out_refs[a].at[me if sending else idx],
            send_sem=send_sems.at[7 * a + k - 1], recv_sem=recv_sems.at[7 * a + k - 1],
            device_id=to, device_id_type=MESH)

    def sent():
        return [copy(a, k, True) for k in range(1, N_DEV) for a in range(na)]

    def start():
        for cp in own() + sent():
            cp.start()

    def finish():
        for k in range(1, N_DEV):
            for a in range(na):
                copy(a, k, False).wait_recv()
        for cp in sent():
            cp.wait_send()
        for cp in own():
            cp.wait()

    return start, finish


def _reduce_adamw(name, recv, w, m, v):
    rows, cols = w.shape
    blk = ADAM_ROWS if rows % ADAM_ROWS == 0 else rows
    c1 = 1.0 - ADAM_B1 ** ADAM_STEP
    c2 = 1.0 - ADAM_B2 ** ADAM_STEP

    def body(r_ref, w_ref, m_ref, v_ref, g_ref, d_ref, nm_ref, nv_ref):
        g = r_ref[0].astype(F32)
        for s in range(1, N_DEV):
            g = g + r_ref[s].astype(F32)
        g_ref[...] = g
        m_ = ADAM_B1 * m_ref[...] + (1.0 - ADAM_B1) * g
        v_ = ADAM_B2 * v_ref[...] + (1.0 - ADAM_B2) * (g * g)
        nm_ref[...] = m_
        nv_ref[...] = v_
        d_ref[...] = -ADAM_LR * ((m_ / c1) / (jnp.sqrt(v_ / c2) + ADAM_EPS) + ADAM_WD * w_ref[...])

    spec = _rowspec(blk, cols)
    return pl.pallas_call(
        body, name="reduce_adamw_" + name, grid=(rows // blk,),
        in_specs=[pl.BlockSpec((N_DEV, blk, cols), lambda i: (0, i, 0)), spec, spec, spec],
        out_specs=[spec] * 4, out_shape=[_sds((rows, cols), F32)] * 4,
        compiler_params=_cp(("arbitrary",)),
    )(recv, w, m, v)


def kernel(x, meta_tokens, norm_mix_pre, w_in, q_a_norm, w_q_up, kv_a_norm, w_kv_up, conv_w, conv_b, dt_bias, a_log, d_skip, ssm_norm, w_out, norm_mix_post, norm_mlp_pre, w_mlp_up, w_mlp_down, norm_mlp_post, loss_target, m_meta_tokens, m_norm_mix_pre, m_w_in, m_q_a_norm, m_w_q_up, m_kv_a_norm, m_w_kv_up, m_conv_w, m_conv_b, m_dt_bias, m_a_log, m_d_skip, m_ssm_norm, m_w_out, m_norm_mix_post, m_norm_mlp_pre, m_w_mlp_up, m_w_mlp_down, m_norm_mlp_post, v_meta_tokens, v_norm_mix_pre, v_w_in, v_q_a_norm, v_w_q_up, v_kv_a_norm, v_w_kv_up, v_conv_w, v_conv_b, v_dt_bias, v_a_log, v_d_skip, v_ssm_norm, v_w_out, v_norm_mix_post, v_norm_mlp_pre, v_w_mlp_up, v_w_mlp_down, v_norm_mlp_post):
    w = dict(meta_tokens=meta_tokens, norm_mix_pre=norm_mix_pre, w_in=w_in, q_a_norm=q_a_norm, w_q_up=w_q_up,
             kv_a_norm=kv_a_norm, w_kv_up=w_kv_up, conv_w=conv_w, conv_b=conv_b, dt_bias=dt_bias, a_log=a_log,
             d_skip=d_skip, ssm_norm=ssm_norm, w_out=w_out, norm_mix_post=norm_mix_post, norm_mlp_pre=norm_mlp_pre,
             w_mlp_up=w_mlp_up, w_mlp_down=w_mlp_down, norm_mlp_post=norm_mlp_post)
    m = dict(meta_tokens=m_meta_tokens, norm_mix_pre=m_norm_mix_pre, w_in=m_w_in, q_a_norm=m_q_a_norm, w_q_up=m_w_q_up,
             kv_a_norm=m_kv_a_norm, w_kv_up=m_w_kv_up, conv_w=m_conv_w, conv_b=m_conv_b, dt_bias=m_dt_bias,
             a_log=m_a_log, d_skip=m_d_skip, ssm_norm=m_ssm_norm, w_out=m_w_out, norm_mix_post=m_norm_mix_post,
             norm_mlp_pre=m_norm_mlp_pre, w_mlp_up=m_w_mlp_up, w_mlp_down=m_w_mlp_down, norm_mlp_post=m_norm_mlp_post)
    v = dict(meta_tokens=v_meta_tokens, norm_mix_pre=v_norm_mix_pre, w_in=v_w_in, q_a_norm=v_q_a_norm, w_q_up=v_w_q_up,
             kv_a_norm=v_kv_a_norm, w_kv_up=v_w_kv_up, conv_w=v_conv_w, conv_b=v_conv_b, dt_bias=v_dt_bias,
             a_log=v_a_log, d_skip=v_d_skip, ssm_norm=v_ssm_norm, w_out=v_w_out, norm_mix_post=v_norm_mix_post,
             norm_mlp_pre=v_norm_mlp_pre, w_mlp_up=v_w_mlp_up, w_mlp_down=v_w_mlp_down, norm_mlp_post=v_norm_mlp_post)
    big_names = [n for n in WEIGHT_ORDER if n in BIG]
    small_names = [n for n in WEIGHT_ORDER if n in SMALL]
    shard = lambda d, n: d[n].reshape(_shard_shape(n))

    f32_names = ("conv_w", "meta_tokens")
    early = [n for n in big_names if n not in LATE]
    gathered = _all_gather([shard(w, n).astype(F32 if n in f32_names else BF16) for n in early])
    p = {n: w[n].reshape(-1) for n in small_names}
    p.update({n: _from_shards(n, s) for n, s in zip(early, gathered)})
    head = jnp.concatenate([jnp.zeros((PADF, D_MODEL), F32), p["meta_tokens"]], axis=0)
    f = _local_forward(head, x[0], loss_target[0], p, late={n: shard(w, n).astype(BF16) for n in LATE})
    dx, g, recv_of = _local_backward(head, x[0], f, exchange_late=True)
    grad_x = dx[None]
    loss = lax.psum(f["loss"], ("x", "y", "c"))

    rest = [n for n in big_names if n not in LATE + MID]
    small = _pack([g[n] for n in small_names], SMALL_ROWS)
    *recv_rest, recv_small = _exchange([_to_chunks(n, g[n]) for n in rest], small)
    recv_of.update(zip(rest, recv_rest))

    outs = {}
    kinds = ("grad", "delta", "new_m", "new_v")
    for n, recv in ((n, recv_of[n]) for n in big_names):
        for kind, arr in zip(kinds, _reduce_adamw(n, recv, shard(w, n), shard(m, n), shard(v, n))):
            outs[kind, n] = arr.reshape(w[n].shape)
    packed = [_pack([d[n] for n in small_names], SMALL_ROWS) for d in (w, m, v)]
    for kind, arr in zip(kinds, _reduce_adamw("small", recv_small, *packed)):
        for n, val in zip(small_names, _unpack(arr, [(SMALL[n],) for n in small_names])):
            outs[kind, n] = val.reshape(w[n].shape)
    return (loss, grad_x) + tuple(outs[kind, n] for kind in ("grad", "delta", "new_m", "new_v") for n in WEIGHT_ORDER)
```

```python
import math

import jax
import jax.numpy as jnp
import numpy as np
from jax import lax
from jax.experimental import pallas as pl
from jax.experimental.pallas import tpu as pltpu

F32 = jnp.float32
BF16 = jnp.bfloat16

D_MODEL = 1024
N_META = 16
EPS = 1e-6
ATT_HEADS = 8
Q_LORA = 384
KV_LORA = 256
QK_NOPE = 128
QK_ROPE = 64
V_HEAD = 128
ROPE_THETA = 10000.0
SSM_HEADS = 16
SSM_HEAD_DIM = 64
SSM_WIDTH = 1024
SSM_GROUPS = 2
SSM_STATE = 128
CONV_K = 4
CHUNK = 128
CONV_DIM = 1536
D_FF = 4096
IN_SPLITS = (Q_LORA, KV_LORA, QK_ROPE, SSM_WIDTH, CONV_DIM, SSM_HEADS)
IN_WIDTH = sum(IN_SPLITS)
ADAM_LR, ADAM_B1, ADAM_B2, ADAM_EPS, ADAM_WD, ADAM_STEP = 0.001, 0.9, 0.999, 1e-08, 0.01, 10

LANE = 128
ROWB = 512
PADF = ROWB - N_META
HEADW = 256
PC_Q, PC_KV, PC_KR, PC_Z, PC_XBC, PC_DT, PROJ_W = 0, 384, 640, 896, 1920, 3456, 3584
NEG = -1e30
N_DEV = 8
VMEM_LIMIT = 56 * 1024 * 1024
MESH = pl.DeviceIdType.MESH


def _cp(sem, vmem=VMEM_LIMIT, **kw):
    return pltpu.CompilerParams(dimension_semantics=sem, vmem_limit_bytes=vmem, **kw)


def _dot(a, b, dims=((1,), (0,))):
    return lax.dot_general(a, b, (dims, ((), ())), preferred_element_type=F32)


def _bdot(a, b, dims=((1,), (0,))):
    return _dot(a.astype(BF16), b.astype(BF16), dims)


NT = ((1,), (1,))


def _rms_fwd(x, w):
    r = lax.rsqrt(jnp.mean(x * x, axis=-1, keepdims=True) + EPS)
    return (x * r) * w


def _rms_bwd(x, w, dy):
    r = lax.rsqrt(jnp.mean(x * x, axis=-1, keepdims=True) + EPS)
    xh = x * r
    g = dy * w
    dx = r * (g - xh * jnp.mean(g * xh, axis=-1, keepdims=True))
    return dx, dy * xh


def _sigmoid(x):
    return 0.5 * jnp.tanh(0.5 * x) + 0.5


def _colsum8(x):
    t, c = x.shape
    return jnp.sum(x.reshape(t // 8, 8, c), axis=0)


def _rowspec(t, c, cb=0):
    return pl.BlockSpec((t, c), lambda i: (i, cb))


def _fullspec(shape):
    n = len(shape)
    return pl.BlockSpec(shape, lambda i: (0,) * n)


def _sds(shape, dt):
    return jax.ShapeDtypeStruct(shape, dt)


def _acc(ref, val):
    @pl.when(pl.program_id(0) == 0)
    def _():
        ref[...] = jnp.zeros_like(ref)

    ref[...] += val


def _xspec():
    return pl.BlockSpec((ROWB, D_MODEL), lambda i: (jnp.maximum(i - 1, 0), 0))


def _h_block(head_ref, x_ref):
    return jnp.where(pl.program_id(0) == 0, head_ref[...], x_ref[...])


def _norm_in_proj(head, x, g, w_all):
    lp = head.shape[0] + x.shape[0]

    def body(head_ref, x_ref, g_ref, w_ref, hn_ref, cq_ref, ckv_ref, kr_ref, z_ref, xbc_ref, dt_ref):
        hn = _rms_fwd(_h_block(head_ref, x_ref), g_ref[...]).astype(BF16)
        hn_ref[...] = hn
        p = _dot(hn, w_ref[...])
        cq_ref[...] = p[:, PC_Q:PC_KV]
        ckv_ref[...] = p[:, PC_KV:PC_KR]
        kr_ref[...] = p[:, PC_KR:PC_Z]
        z_ref[...] = p[:, PC_Z:PC_XBC]
        xbc_ref[...] = p[:, PC_XBC:PC_DT]
        dt_ref[...] = p[:, PC_DT:PROJ_W]

    widths = (Q_LORA, KV_LORA, HEADW, SSM_WIDTH, CONV_DIM, LANE)
    return pl.pallas_call(
        body, name="norm_in_proj", grid=(lp // ROWB,),
        in_specs=[_fullspec((ROWB, D_MODEL)), _xspec(), _fullspec((1, D_MODEL)), _fullspec((D_MODEL, PROJ_W))],
        out_specs=[_rowspec(ROWB, D_MODEL)] + [_rowspec(ROWB, w) for w in widths],
        out_shape=[_sds((lp, D_MODEL), BF16)] + [_sds((lp, w), F32) for w in widths],
        compiler_params=_cp(("arbitrary",)),
    )(head, x, g, w_all)


def _rope(x, cos, sa, sb):
    w = x.shape[1]
    return x * cos + pltpu.roll(x, w - 32, 1) * sa + pltpu.roll(x, 32, 1) * sb


def _rope_t(dy, cos, sa, sb):
    w = dy.shape[1]
    return dy * cos + pltpu.roll(dy * sa, 32, 1) + pltpu.roll(dy * sb, w - 32, 1)


def _qkv(cq, ckv, kr, cos, sa, sb, gq, gkv, wq, wk, wv):
    lp = cq.shape[0]
    qw = ATT_HEADS * HEADW
    half = HEADW // 2
    assert half == QK_NOPE == V_HEAD == LANE

    def body(cq_ref, ckv_ref, kr_ref, cos_ref, sa_ref, sb_ref, gq_ref, gkv_ref, wq_ref, wk_ref, wv_ref,
             q_ref, k_ref, v_ref, cqn_ref, ckvn_ref):
        tabs = [cos_ref[...], sa_ref[...], sb_ref[...]]
        cqn = _rms_fwd(cq_ref[...], gq_ref[...]).astype(BF16)
        ckvn = _rms_fwd(ckv_ref[...], gkv_ref[...]).astype(BF16)
        cqn_ref[...] = cqn
        ckvn_ref[...] = ckvn
        q = _dot(cqn, wq_ref[...])
        kn = _dot(ckvn, wk_ref[...])
        vv = _dot(ckvn, wv_ref[...])
        krope = _rope(kr_ref[:, pl.ds(half, half)], *tabs).astype(BF16)
        ones = jnp.ones((ROWB, half), BF16)
        for hh in range(ATT_HEADS):
            lo, hi, src = pl.ds(hh * HEADW, half), pl.ds(hh * HEADW + half, half), slice(hh * half, (hh + 1) * half)
            q_ref[:, lo] = (q[:, hh * HEADW:hh * HEADW + half] * Q_PRESCALE).astype(BF16)
            q_ref[:, hi] = (_rope(q[:, hh * HEADW + half:(hh + 1) * HEADW], *tabs) * Q_PRESCALE).astype(BF16)
            k_ref[:, lo] = kn[:, src].astype(BF16)
            k_ref[:, hi] = krope
            v_ref[:, lo] = vv[:, src].astype(BF16)
            v_ref[:, hi] = ones

    return pl.pallas_call(
        body, name="qkv", grid=(lp // ROWB,),
        in_specs=[_rowspec(ROWB, Q_LORA), _rowspec(ROWB, KV_LORA), _rowspec(ROWB, HEADW)]
        + [_rowspec(ROWB, half)] * 3
        + [_fullspec((1, Q_LORA)), _fullspec((1, KV_LORA)), _fullspec((Q_LORA, qw)),
           _fullspec((KV_LORA, ATT_HEADS * QK_NOPE)), _fullspec((KV_LORA, ATT_HEADS * V_HEAD))],
        out_specs=[_rowspec(ROWB, qw), _rowspec(ROWB, qw), _rowspec(ROWB, qw),
                   _rowspec(ROWB, Q_LORA), _rowspec(ROWB, KV_LORA)],
        out_shape=[_sds((lp, qw), BF16), _sds((lp, qw), BF16), _sds((lp, qw), BF16),
                   _sds((lp, Q_LORA), BF16), _sds((lp, KV_LORA), BF16)],
        compiler_params=_cp(("arbitrary",)),
    )(cq, ckv, kr, cos, sa, sb, gq, gkv, wq, wk, wv)


ATT_SCALE = (QK_NOPE + QK_ROPE) ** -0.5
LOG2E = 1.4426950408889634
LN2 = 0.6931471805599453
Q_PRESCALE = ATT_SCALE * LOG2E
KVB = 512
META_KEYS = LANE
assert N_META <= META_KEYS


def _att_ok(qrow, krow):
    return (krow <= qrow) & ((krow >= PADF) | (qrow < PADF))


def _lanes(x, n):
    return x if n == 1 else jnp.concatenate([x] * n, axis=1)


def _pair_loop(lo, hi, tile, unrolls=(2,)):
    for u in tuple(unrolls) + (1,):
        n = jnp.maximum(hi - lo, 0)
        trips = n // u

        def many(t, c, u=u, lo=lo):
            for d in range(u):
                tile(lo + u * t + d)
            return c

        lax.fori_loop(0, trips, many, 0)
        lo = lo + trips * u


def _flash_fwd(q, k, v):
    lp = q.shape[0]
    nq = lp // ROWB

    def body(q_ref, k_ref, v_ref, o_ref, lse_ref, acc, m_s):
        i = pl.program_id(1)
        qb = q_ref[...]
        m_s[...] = jnp.full_like(m_s, NEG)
        acc[...] = jnp.zeros_like(acc)

        def tile(j, masked, off=None, nkeys=KVB):
            off = pl.multiple_of(j * KVB, KVB) if off is None else off
            kb = k_ref[pl.ds(off, nkeys), :]
            vb = v_ref[pl.ds(off, nkeys), :]
            s = _dot(qb, kb, NT)
            if masked:
                qrow = i * ROWB + lax.broadcasted_iota(jnp.int32, s.shape, 0)
                krow = off + lax.broadcasted_iota(jnp.int32, s.shape, 1)
                s = jnp.where(_att_ok(qrow, krow), s, NEG)
            m_prev = m_s[...]
            m_new = jnp.maximum(m_prev, jnp.max(s, axis=1, keepdims=True))
            alpha = jnp.exp2(m_prev - m_new)
            p = jnp.exp2(s - _lanes(m_new, nkeys // LANE))
            acc[...] = _lanes(alpha, 2) * acc[...] + _dot(p.astype(BF16), vb)
            m_s[...] = m_new

        def first_tile():
            tile(0, True, off=ROWB - META_KEYS, nkeys=META_KEYS)

        @pl.when(i == 0)
        def _():
            first_tile()

        odd = jnp.bitwise_and(jnp.maximum(i - 1, 0), 7)
        for r in range(8):
            @pl.when((i > 0) & (odd == r))
            def _(r=r):
                first_tile()
                tile(i, True)
                for d in range(r):
                    tile(1 + d, False)

        _pair_loop(1 + odd, i, lambda j: tile(j, False), (16, 8))
        l = acc[:, V_HEAD:]
        o_ref[...] = (acc[:, :V_HEAD] / l).astype(BF16)
        lse_ref[0] = (m_s[...] + jnp.log2(l)).T[0:1, :]

    return pl.pallas_call(
        body, name="flash_fwd", grid=(ATT_HEADS, nq),
        in_specs=[pl.BlockSpec((ROWB, HEADW), lambda h, i: (i, h)),
                  pl.BlockSpec((lp, HEADW), lambda h, i: (0, h)),
                  pl.BlockSpec((lp, HEADW), lambda h, i: (0, h))],
        out_specs=[pl.BlockSpec((ROWB, V_HEAD), lambda h, i: (i, h)),
                   pl.BlockSpec((1, 1, ROWB), lambda h, i: (h, 0, i))],
        out_shape=[_sds((lp, ATT_HEADS * V_HEAD), BF16), _sds((ATT_HEADS, 1, lp), F32)],
        scratch_shapes=[pltpu.VMEM((ROWB, HEADW), F32), pltpu.VMEM((ROWB, LANE), F32)],
        compiler_params=_cp(("arbitrary", "arbitrary")),
    )(q, k, v)


def _silu(x):
    return x * _sigmoid(x)


CONV_ROWS = 64


def _shifted_rows(ref, start, n, cols, offsets):
    win = ref[pl.ds(start, n + 8), cols]
    return [win[o:o + n] if o % 8 == 0 else pltpu.roll(win, n + 8 - o, 0)[0:n] for o in offsets]


def _conv_fwd(xbc, cw, cb):
    lp, c = xbc.shape
    t8 = ROWB // 8

    def body(x_ref, prev_ref, w_ref, b_ref, o_ref, buf):
        i = pl.program_id(0)
        buf[pl.ds(0, 8), :] = jnp.where(i > 0, prev_ref[...], 0.0)
        buf[pl.ds(8, ROWB), :] = x_ref[...]

        def strip(s, carry):
            cs = pl.ds(pl.multiple_of(s * LANE, LANE), LANE)
            w, b = w_ref[:, cs], b_ref[:, cs]
            for r0 in range(0, ROWB, CONV_ROWS):
                taps = _shifted_rows(buf, r0, CONV_ROWS, cs, [8 - (CONV_K - 1) + kk for kk in range(CONV_K)])
                pre = b + sum(w[kk:kk + 1, :] * taps[kk] for kk in range(CONV_K))
                o_ref[pl.ds(r0, CONV_ROWS), cs] = _silu(pre)
            return carry

        lax.fori_loop(0, c // LANE, strip, 0)

    return pl.pallas_call(
        body, name="conv_fwd", grid=(lp // ROWB,),
        in_specs=[_rowspec(ROWB, c), pl.BlockSpec((8, c), lambda i: (jnp.maximum(i * t8 - 1, 0), 0)),
                  _fullspec((8, c)), _fullspec((1, c))],
        out_specs=_rowspec(ROWB, c), out_shape=_sds((lp, c), F32),
        scratch_shapes=[pltpu.VMEM((ROWB + 8, c), F32)],
        compiler_params=_cp(("arbitrary",)),
    )(xbc, xbc, cw, cb)


def _expand_mat():
    r = np.arange(LANE)[:, None]
    c = np.arange(SSM_WIDTH)[None, :]
    return jnp.asarray((c // SSM_HEAD_DIM == r).astype(np.float32))


def _tri_mat():
    i = np.arange(CHUNK)
    return jnp.asarray((i[:, None] >= i[None, :]).astype(np.float32))


def _x3(m, axis):
    return jnp.concatenate([m.astype(BF16)] * 3, axis=axis)


def _split3(x):
    hi = x.astype(BF16)
    r = x - hi.astype(F32)
    mid = r.astype(BF16)
    return hi, mid, (r - mid.astype(F32)).astype(BF16)


def _dot01_r(x, m3):
    return _dot(jnp.concatenate(_split3(x), axis=1), m3)


def _dot01_l(m3, x):
    return _dot(m3, jnp.concatenate(_split3(x), axis=0))


def _ssd_prep(dt_raw, bias_ref, alog_ref, tri3, c, seq_rows):
    rows = c * CHUNK + lax.broadcasted_iota(jnp.int32, (CHUNK, LANE), 0)
    lanes = lax.broadcasted_iota(jnp.int32, (CHUNK, LANE), 1)
    valid = (rows >= PADF) & (rows < PADF + seq_rows) & (lanes < SSM_HEADS)
    dtr = dt_raw + bias_ref[...]
    sp = jnp.maximum(dtr, 0.0) + jnp.log(1.0 + jnp.exp(-jnp.abs(dtr)))
    dt = jnp.where(valid, sp, 0.0)
    a = -jnp.exp(alog_ref[...])
    acol = _dot01_l(tri3, dt * a)
    return dt, a, acol, valid, dtr


def _row16(v):
    return jnp.broadcast_to(v, (16, v.shape[1]))


def _ssd_fwd(xbc_act, dtr, dt_bias, a_log, seq_rows, gather=()):
    lp = xbc_act.shape[0]
    nc = lp // CHUNK
    cps = ROWB // CHUNK
    nsteps = nc // cps
    gw = SSM_WIDTH // SSM_GROUPS
    hpg = SSM_HEADS // SSM_GROUPS

    na = len(gather)

    def body(x_ref, b_ref, c_ref, dtr_ref, bias_ref, alog_ref, tri_ref, tri3_ref, ex3_ref, *rest):
        gin, (y_ref, hp_ref), gout, h_s, sems = rest[:na], rest[na:na + 2], rest[na + 2:2 * na + 2], rest[2 * na + 2], rest[2 * na + 3:]
        step = pl.program_id(0)

        @pl.when(step == 0)
        def _():
            h_s[...] = jnp.zeros_like(h_s)

        if na:
            g_start, g_forward, g_finish = _gather_ops(gin, gout, *sems)
            pl.when(step == 0)(g_start)
            pl.when(step == nsteps // 2)(g_forward)

        ex3 = ex3_ref[...]
        causal = tri_ref[...] > 0.5
        for cc in range(cps):
            rows = pl.ds(cc * CHUNK, CHUNK)
            dt, a, acol, _, _ = _ssd_prep(dtr_ref[rows, :], bias_ref, alog_ref, tri3_ref[...], step * cps + cc, seq_rows)
            arow = acol.T
            dtrow = dt.T
            alast = acol[CHUNK - 1:CHUNK, :]
            e_all = _dot01_r(jnp.exp(acol), ex3)
            wx_all = _dot01_r(jnp.exp(alast - acol) * dt, ex3)
            dec_all = _dot01_r(_row16(jnp.exp(alast)), ex3)[0:1, :]
            hp_ref[cc] = h_s[...]
            for g in range(SSM_GROUPS):
                gs = slice(g * gw, (g + 1) * gw)
                bg = b_ref[rows, g * SSM_STATE:(g + 1) * SSM_STATE]
                cg = c_ref[rows, g * SSM_STATE:(g + 1) * SSM_STATE].astype(BF16)
                xg = x_ref[rows, gs]
                hg = h_s[:, gs]
                gm = _bdot(cg, bg, NT)
                y_off = _bdot(cg, hg) * e_all[:, gs]
                for r in range(hpg):
                    hd = g * hpg + r
                    seg = acol[:, hd:hd + 1] - arow[hd:hd + 1, :]
                    lm = jnp.where(causal, jnp.exp(jnp.where(causal, seg, 0.0)), 0.0)
                    w = gm * lm * dtrow[hd:hd + 1, :]
                    cs = slice(r * SSM_HEAD_DIM, (r + 1) * SSM_HEAD_DIM)
                    y_ref[rows, pl.ds(hd * SSM_HEAD_DIM, SSM_HEAD_DIM)] = _bdot(w, xg[:, cs]) + y_off[:, cs]
                st = _bdot(bg.T, xg * wx_all[:, gs])
                h_s[:, gs] = hg * dec_all[:, gs] + st

        if na:
            pl.when(step == nsteps - 1)(g_finish)

    xs_spec = pl.BlockSpec((ROWB, SSM_WIDTH), lambda c: (c, 0))
    b_spec = pl.BlockSpec((ROWB, 2 * SSM_STATE), lambda c: (c, SSM_WIDTH // (2 * SSM_STATE)))
    c_spec = pl.BlockSpec((ROWB, 2 * SSM_STATE), lambda c: (c, SSM_WIDTH // (2 * SSM_STATE) + 1))
    y, hprev, *gathered = pl.pallas_call(
        body, name="ssd_fwd", grid=(nsteps,),
        in_specs=[xs_spec, b_spec, c_spec, pl.BlockSpec((ROWB, LANE), lambda c: (c, 0)),
                  _fullspec((1, LANE)), _fullspec((1, LANE)), _fullspec((CHUNK, CHUNK)), _fullspec((CHUNK, 3 * CHUNK)),
                  _fullspec((3 * LANE, SSM_WIDTH))] + [ANY] * na,
        out_specs=[xs_spec, pl.BlockSpec((cps, SSM_STATE, SSM_WIDTH), lambda c: (c, 0, 0))] + [ANY] * na,
        out_shape=[_sds((lp, SSM_WIDTH), F32), _sds((nc, SSM_STATE, SSM_WIDTH), F32)]
        + [_sds((N_DEV,) + s.shape, s.dtype) for s in gather],
        scratch_shapes=[pltpu.VMEM((SSM_STATE, SSM_WIDTH), F32)] + (_gather_scratch(na) if na else []),
        compiler_params=_cp(("arbitrary",)),
    )(xbc_act, xbc_act, xbc_act, dtr, dt_bias, a_log, _tri_mat(), _x3(_tri_mat(), 1), _x3(_expand_mat(), 0), *gather)
    return y, hprev, gathered


def _group_mean(x):
    gw = SSM_WIDTH // SSM_GROUPS
    parts = [jnp.broadcast_to(jnp.mean(x[:, g * gw:(g + 1) * gw], axis=-1, keepdims=True), (x.shape[0], gw))
             for g in range(SSM_GROUPS)]
    return jnp.concatenate(parts, axis=1)


def _out_proj(att, y, xbc_act, z, dskip, gnorm, head, x, w_out, g_post):
    lp = att.shape[0]

    def body(a_ref, y_ref, xs_ref, z_ref, d_ref, gn_ref, head_ref, x_ref, w_ref, g_ref, ssm_ref, mix_ref, h1_ref):
        gt = (y_ref[...] + d_ref[...] * xs_ref[...]) * _silu(z_ref[...])
        r = lax.rsqrt(_group_mean(gt * gt) + EPS)
        ssm = ((gt * r) * gn_ref[...]).astype(BF16)
        ssm_ref[...] = ssm
        mix = _dot(a_ref[...], w_ref[pl.ds(0, 1024), :]) + _dot(ssm, w_ref[pl.ds(1024, 1024), :])
        mix_ref[...] = mix
        h1_ref[...] = _h_block(head_ref, x_ref) + _rms_fwd(mix, g_ref[...])

    return pl.pallas_call(
        body, name="out_proj", grid=(lp // ROWB,),
        in_specs=[_rowspec(ROWB, 1024)] * 4 + [_fullspec((1, SSM_WIDTH))] * 2
        + [_fullspec((ROWB, D_MODEL)), _xspec(), _fullspec((2048, D_MODEL)), _fullspec((1, D_MODEL))],
        out_specs=[_rowspec(ROWB, SSM_WIDTH)] + [_rowspec(ROWB, D_MODEL)] * 2,
        out_shape=[_sds((lp, SSM_WIDTH), BF16)] + [_sds((lp, D_MODEL), F32)] * 2,
        compiler_params=_cp(("arbitrary",)),
    )(att, y, xbc_act, z, dskip, gnorm, head, x, w_out, g_post)


def _resident(w_hbm, w_vmem, sem):
    @pl.when(pl.program_id(0) == 0)
    def _():
        cp = pltpu.make_async_copy(w_hbm, w_vmem, sem)
        cp.start()
        cp.wait()


ANY = pl.BlockSpec(memory_space=pl.ANY)


def _mlp_fwd(h1, tgt, w_up, w_down, g_pre, g_post, seq_rows):
    lp = h1.shape[0]

    def body(h1_ref, t_ref, wu_hbm, wd_hbm, gpre_ref, gpost_ref, hn2_ref, u_ref, a_ref, f_ref, dh2_ref, loss_ref,
             wu, wd, sems):
        _resident(wu_hbm, wu, sems.at[0])
        _resident(wd_hbm, wd, sems.at[1])
        i = pl.program_id(0)
        h1_ = h1_ref[...]
        hn2 = _rms_fwd(h1_, gpre_ref[...]).astype(BF16)
        hn2_ref[...] = hn2
        u = jnp.maximum(_dot(hn2, wu[...]), 0.0)
        u_ref[...] = u.astype(BF16)
        a = (u * u).astype(BF16)
        a_ref[...] = a
        f = _dot(a, wd[...])
        f_ref[...] = f
        h2 = h1_ + _rms_fwd(f, gpost_ref[...])
        rows = i * ROWB + lax.broadcasted_iota(jnp.int32, (ROWB, 1), 0)
        real = (rows >= PADF + N_META) & (rows < PADF + seq_rows)
        err = jnp.where(real, h2 - t_ref[...], 0.0)
        dh2_ref[...] = err * (1.0 / D_MODEL)
        _acc(loss_ref, _colsum8(err * err))

    return pl.pallas_call(
        body, name="mlp_fwd", grid=(lp // ROWB,),
        in_specs=[_rowspec(ROWB, D_MODEL), _xspec()] + [ANY, ANY] + [_fullspec((1, D_MODEL))] * 2,
        out_specs=[_rowspec(ROWB, D_MODEL), _rowspec(ROWB, D_FF), _rowspec(ROWB, D_FF)] + [_rowspec(ROWB, D_MODEL)] * 2
        + [_fullspec((8, D_MODEL))],
        out_shape=[_sds((lp, D_MODEL), BF16), _sds((lp, D_FF), BF16), _sds((lp, D_FF), BF16), _sds((lp, D_MODEL), F32),
                   _sds((lp, D_MODEL), F32), _sds((8, D_MODEL), F32)],
        scratch_shapes=[pltpu.VMEM((D_MODEL, D_FF), BF16), pltpu.VMEM((D_FF, D_MODEL), BF16), pltpu.SemaphoreType.DMA((2,))],
        compiler_params=_cp(("arbitrary",)),
    )(h1, tgt, w_up, w_down, g_pre, g_post)


def _pad_cols(w, width):
    return jnp.pad(w, ((0, 0), (0, width - w.shape[1])))


def _layout_weights(w_in, w_q_up, w_kv_up):
    o = np.cumsum((0,) + IN_SPLITS)
    pieces = [w_in[:, o[k]:o[k + 1]] for k in range(6)]
    kr = jnp.pad(pieces[2], ((0, 0), (QK_NOPE, HEADW - QK_NOPE - QK_ROPE)))
    w_all = jnp.concatenate([pieces[0], pieces[1], kr, pieces[3], pieces[4], _pad_cols(pieces[5], LANE)], axis=1)
    wq = jnp.pad(w_q_up.reshape(Q_LORA, ATT_HEADS, QK_NOPE + QK_ROPE), ((0, 0), (0, 0), (0, HEADW - QK_NOPE - QK_ROPE)))
    wkv = w_kv_up.reshape(KV_LORA, ATT_HEADS, QK_NOPE + V_HEAD)
    wk = jnp.pad(wkv[:, :, :QK_NOPE], ((0, 0), (0, 0), (0, HEADW - QK_NOPE)))
    wv = wkv[:, :, QK_NOPE:]
    return (w_all, wq.reshape(Q_LORA, -1), wk.reshape(KV_LORA, -1), wv.reshape(KV_LORA, -1),
            wkv[:, :, :QK_NOPE].reshape(KV_LORA, -1))


def _rope_tables(lp):
    pos = jnp.maximum(jnp.arange(lp, dtype=jnp.int32) - PADF, 0).astype(F32)
    inv_freq = ROPE_THETA ** (-jnp.arange(0, QK_ROPE, 2, dtype=F32) / QK_ROPE)
    ang = pos[:, None] * inv_freq[None, :]
    cos, sin = jnp.cos(ang), jnp.sin(ang)
    z32, z64 = jnp.zeros((lp, 32), F32), jnp.zeros((lp, 64), F32)
    cos_t = jnp.concatenate([cos, cos, jnp.ones((lp, 64), F32)], axis=1)
    sa = jnp.concatenate([-sin, z32, z64], axis=1)
    sb = jnp.concatenate([z32, sin, z64], axis=1)
    return cos_t, sa, sb


def _row1(v, width=None):
    v = v.reshape(1, -1).astype(F32)
    return v if width is None else _pad_cols(v, width)


LATE = ("w_out", "w_mlp_up", "w_mlp_down")
MID = ("w_in", "w_q_up", "w_kv_up", "conv_w")


def _local_forward(head, x, tgt, p, late=None):
    assert head.shape[0] == ROWB and x.shape[0] % ROWB == 0
    lp = ROWB + x.shape[0]
    seq_rows = N_META + x.shape[0]
    f = {"seq_rows": seq_rows}
    w_all, wq, wk, wv, wkn = _layout_weights(p["w_in"], p["w_q_up"], p["w_kv_up"])
    f.update(w_all=w_all, wq=wq, wk=wk, wv=wv)
    f["hn"], cq, ckv, kr, f["z"], f["xbc"], f["dtr"] = _norm_in_proj(head, x, _row1(p["norm_mix_pre"]), w_all)
    f.update(cq=cq, ckv=ckv)
    f["rope"] = _rope_tables(lp)
    f["q"], f["k"], f["v"], f["cqn"], f["ckvn"] = _qkv(cq, ckv, kr, *f["rope"], _row1(p["q_a_norm"]),
                                                   _row1(p["kv_a_norm"]), wq, wkn, wv)
    f["att"], f["lse"] = _flash_fwd(f["q"], f["k"], f["v"])
    f["cw"] = jnp.pad(p["conv_w"].astype(F32), ((0, 8 - CONV_K), (0, 0)))
    f["xact"] = _conv_fwd(f["xbc"], f["cw"], _row1(p["conv_b"]))
    f["dt_bias"], f["a_log"] = _row1(p["dt_bias"], LANE), _row1(p["a_log"], LANE)
    f["y"], f["hprev"], gathered = _ssd_fwd(f["xact"], f["dtr"], f["dt_bias"], f["a_log"], seq_rows,
                                            gather=[late[n] for n in LATE] if late else ())
    p = {**p, **{n: _from_shards(n, s) for n, s in zip(LATE, gathered)}}
    f["p"] = p
    f["dskip"] = jnp.repeat(p["d_skip"].reshape(-1).astype(F32), SSM_HEAD_DIM).reshape(1, SSM_WIDTH)
    f["ssm"], f["mix"], f["h1"] = _out_proj(f["att"], f["y"], f["xact"], f["z"], f["dskip"], _row1(p["ssm_norm"]),
                                            head, x, p["w_out"], _row1(p["norm_mix_post"]))
    f["hn2"], f["u"], f["a"], f["f"], f["dh2"], loss8 = _mlp_fwd(
        f["h1"], tgt, p["w_mlp_up"], p["w_mlp_down"], _row1(p["norm_mlp_pre"]), _row1(p["norm_mlp_post"]), seq_rows)
    f["loss"] = 0.5 * jnp.sum(loss8) / D_MODEL
    return f


MLPB = 256


def _mlp_bwd(dh2, f, h1, u, w_up, w_down, g_pre, g_post):
    lp = h1.shape[0]

    def body(dh2_ref, f_ref, h1_ref, u_ref, wu_hbm, wd_hbm, gpre_ref, gpost_ref,
             dh1_ref, du_ref, df_ref, dgpre_ref, dgpost_ref, wu, wd, sems):
        _resident(wu_hbm, wu, sems.at[0])
        _resident(wd_hbm, wd, sems.at[1])
        dh2_ = dh2_ref[...]
        df, dgp = _rms_bwd(f_ref[...], gpost_ref[...], dh2_)
        dfb = df.astype(BF16)
        df_ref[...] = dfb
        da = _dot(dfb, wd[...], NT)
        du = (da * (2.0 * u_ref[...].astype(F32))).astype(BF16)
        du_ref[...] = du
        dhn2 = _dot(du, wu[...], NT)
        dx, dgq = _rms_bwd(h1_ref[...], gpre_ref[...], dhn2)
        dh1_ref[...] = dh2_ + dx
        _acc(dgpre_ref, _colsum8(dgq))
        _acc(dgpost_ref, _colsum8(dgp))

    return pl.pallas_call(
        body, name="mlp_bwd", grid=(lp // MLPB,),
        in_specs=[_rowspec(MLPB, D_MODEL)] * 3 + [_rowspec(MLPB, D_FF), ANY, ANY] + [_fullspec((1, D_MODEL))] * 2,
        out_specs=[_rowspec(MLPB, D_MODEL), _rowspec(MLPB, D_FF), _rowspec(MLPB, D_MODEL),
                   _fullspec((8, D_MODEL)), _fullspec((8, D_MODEL))],
        out_shape=[_sds((lp, D_MODEL), F32), _sds((lp, D_FF), BF16), _sds((lp, D_MODEL), BF16),
                   _sds((8, D_MODEL), F32), _sds((8, D_MODEL), F32)],
        scratch_shapes=[pltpu.VMEM((D_MODEL, D_FF), BF16), pltpu.VMEM((D_FF, D_MODEL), BF16), pltpu.SemaphoreType.DMA((2,))],
        compiler_params=_cp(("arbitrary",)),
    )(dh2, f, h1, u, w_up, w_down, g_pre, g_post)


def _out_bwd(dh1, mix, att, y, xact, z, dskip, gnorm, w_out, g_post):
    lp = dh1.shape[0]

    def body(dh1_ref, mix_ref, att_ref, y_ref, x_ref, z_ref, d_ref, gn_ref, w_ref, g_ref,
             dmix_ref, datt_ref, dy_ref, dz_ref, dg_ref, dl_ref, dgn_ref, dd_ref):
        dmix, dg = _rms_bwd(mix_ref[...], g_ref[...], dh1_ref[...])
        dmb = dmix.astype(BF16)
        dmix_ref[...] = dmb
        datt = _dot(dmb, w_ref[pl.ds(0, 1024), :], NT).astype(BF16)
        datt_ref[...] = datt
        _acc(dg_ref, _colsum8(dg))
        prod = datt.astype(F32) * att_ref[...].astype(F32)
        for hh in range(ATT_HEADS):
            d = jnp.sum(prod[:, hh * V_HEAD:(hh + 1) * V_HEAD], axis=1, keepdims=True)
            dl_ref[hh] = jnp.broadcast_to(d, (ROWB, LANE)).T[0:1, :]
        do = _dot(dmb, w_ref[pl.ds(1024, 1024), :], NT)
        z_, x_ = z_ref[...], x_ref[...]
        sg = _sigmoid(z_)
        sz = z_ * sg
        y2 = y_ref[...] + d_ref[...] * x_
        gt = y2 * sz
        r = lax.rsqrt(_group_mean(gt * gt) + EPS)
        gh = gt * r
        dgh = do * gn_ref[...]
        dgt = r * (dgh - gh * _group_mean(dgh * gh))
        dy2 = dgt * sz
        dy_ref[...] = dy2
        dz_ref[...] = (dgt * y2 * (sg * (1.0 + z_ * (1.0 - sg)))).astype(BF16)
        _acc(dgn_ref, _colsum8(do * gh))
        _acc(dd_ref, _colsum8(dy2 * x_))

    return pl.pallas_call(
        body, name="out_bwd", grid=(lp // ROWB,),
        in_specs=[_rowspec(ROWB, D_MODEL)] * 6 + [_fullspec((1, SSM_WIDTH))] * 2
        + [_fullspec((2048, D_MODEL)), _fullspec((1, D_MODEL))],
        out_specs=[_rowspec(ROWB, D_MODEL)] * 4 + [_fullspec((8, D_MODEL)),
                                                   pl.BlockSpec((ATT_HEADS, 1, ROWB), lambda i: (0, 0, i)),
                                                   _fullspec((8, SSM_WIDTH)), _fullspec((8, SSM_WIDTH))],
        out_shape=[_sds((lp, D_MODEL), BF16), _sds((lp, 1024), BF16), _sds((lp, SSM_WIDTH), F32),
                   _sds((lp, SSM_WIDTH), BF16), _sds((8, D_MODEL), F32), _sds((ATT_HEADS, 1, lp), F32),
                   _sds((8, SSM_WIDTH), F32), _sds((8, SSM_WIDTH), F32)],
        compiler_params=_cp(("arbitrary",)),
    )(dh1, mix, att, y, xact, z, dskip, gnorm, w_out, g_post)


def _ssd_bwd(dy, xact, dtr, hprev, dt_bias, a_log, dskip, seq_rows, exchange=()):
    lp = xact.shape[0]
    nc = lp // CHUNK
    gw = SSM_WIDTH // SSM_GROUPS
    hpg = SSM_HEADS // SSM_GROUPS
    nb = SSM_WIDTH // (2 * SSM_STATE)
    na = len(exchange)
    cps = ROWB // CHUNK
    nsteps = nc // cps

    def body(dy_ref, x_ref, b_ref, c_ref, dtr_ref, hp_ref, bias_ref, alog_ref, dsk_ref, tri_ref, tri3_ref, trit3_ref,
             ex3_ref, ext3_ref, *rest):
        xin, (dact_ref, ddtr_ref, da_ref, dbias_ref), xout = rest[:na], rest[na:na + 4], rest[na + 4:2 * na + 4]
        dh_s, sems = rest[2 * na + 4], rest[2 * na + 5:]
        step = pl.program_id(0)

        @pl.when(step == 0)
        def _():
            dh_s[...] = jnp.zeros_like(dh_s)
            da_ref[...] = jnp.zeros_like(da_ref)
            dbias_ref[...] = jnp.zeros_like(dbias_ref)

        if na:
            x_start, x_finish = _exchange_ops(xin, xout, *sems)
            pl.when(step == 0)(x_start)

        for lc in reversed(range(cps)):
            rows = pl.ds(lc * CHUNK, CHUNK)
            chunk((nsteps - 1 - step) * cps + lc, dy_ref.at[rows], x_ref.at[rows], b_ref.at[rows], c_ref.at[rows],
                  dtr_ref.at[rows], hp_ref.at[pl.ds(lc, 1)], bias_ref, alog_ref, dsk_ref, tri_ref, tri3_ref, trit3_ref,
                  ex3_ref, ext3_ref, dact_ref.at[rows], ddtr_ref.at[rows], da_ref, dbias_ref, dh_s)

        if na:
            pl.when(step == nsteps - 1)(x_finish)

    def chunk(c, dy_ref, x_ref, b_ref, c_ref, dtr_ref, hp_ref, bias_ref, alog_ref, dsk_ref, tri_ref, tri3_ref, trit3_ref,
              ex3_ref, ext3_ref, dact_ref, ddtr_ref, da_ref, dbias_ref, dh_s):
        tri = tri_ref[...]
        ex3 = ex3_ref[...]
        dt, a, acol, valid, dtr_ = _ssd_prep(dtr_ref[...], bias_ref, alog_ref, tri3_ref[...], c, seq_rows)
        arow = acol.T
        dtrow = dt.T
        alast = acol[CHUNK - 1:CHUNK, :]
        e_all = _dot01_r(jnp.exp(acol), ex3)
        wgt0 = jnp.exp(alast - acol)
        wgt = wgt0 * dt
        wx_all = _dot01_r(wgt, ex3)
        elast = jnp.exp(alast)
        dec_all = _dot01_r(_row16(elast), ex3)[0:1, :]
        causal = tri > 0.5
        upper = tri.T > 0.5
        lane_id = lax.broadcasted_iota(jnp.int32, (1, LANE), 1)
        sub_id = lax.broadcasted_iota(jnp.int32, (CHUNK, 1), 0)
        dacol = jnp.zeros((CHUNK, LANE), F32)
        darowf = jnp.zeros((CHUNK, LANE), F32)
        ddtrowf = jnp.zeros((CHUNK, LANE), F32)
        dwgt = jnp.zeros((CHUNK, LANE), F32)
        delast = jnp.zeros((1, LANE), F32)
        for g in range(SSM_GROUPS):
            gs = slice(g * gw, (g + 1) * gw)
            ext3_g = ext3_ref[g]
            bg = b_ref[:, g * SSM_STATE:(g + 1) * SSM_STATE]
            cg = c_ref[:, g * SSM_STATE:(g + 1) * SSM_STATE]
            bgb, cgb = bg.astype(BF16), cg.astype(BF16)
            xg = x_ref[:, gs]
            dyg = dy_ref[:, gs]
            hg = hp_ref[0, :, gs]
            dhg = dh_s[:, gs]
            hgb, dhgb = hg.astype(BF16), dhg.astype(BF16)
            gm = _dot(cgb, bgb, NT)
            gmt = _dot(bgb, cgb, NT)
            y_off = _dot(cgb, hgb) * e_all[:, gs]
            dy0 = (dyg * e_all[:, gs]).astype(BF16)
            dcg = _dot(dy0, hgb, NT)
            dh_in = _dot(cg.T.astype(BF16), dy0) + dhg * dec_all[:, gs]
            dacol = dacol + _dot01_r(dyg * y_off, ext3_g)
            xw = xg * wx_all[:, gs]
            dxw = _dot(bgb, dhgb)
            dx_state = dxw * wx_all[:, gs]
            dwgt = dwgt + _dot01_r(dxw * xg, ext3_g)
            dbt = _dot(dhgb, xw.astype(BF16), NT)
            hh = _colsum8(dhg * hg)
            hh16 = jnp.concatenate([hh, jnp.zeros_like(hh)], axis=0)
            delast = delast + jnp.sum(_dot01_r(hh16, ext3_g), axis=0, keepdims=True)
            dgm = jnp.zeros((CHUNK, CHUNK), F32)
            for r in range(hpg):
                hd = g * hpg + r
                cs = slice(r * SSM_HEAD_DIM, (r + 1) * SSM_HEAD_DIM)
                acol_r, arow_r = acol[:, hd:hd + 1], arow[hd:hd + 1, :]
                dtrow_r, dtcol_r = dtrow[hd:hd + 1, :], dt[:, hd:hd + 1]
                lm = jnp.where(causal, jnp.exp(jnp.where(causal, acol_r - arow_r, 0.0)), 0.0)
                lmt = jnp.where(upper, jnp.exp(jnp.where(upper, arow_r - acol_r, 0.0)), 0.0)
                wt = gmt * lmt * dtcol_r
                dy_r = dyg[:, cs].astype(BF16)
                dx_r = _dot(wt.astype(BF16), dy_r)
                dw = _dot(dy_r, xg[:, cs].astype(BF16), NT)
                t1 = dw * lm
                dgm = dgm + t1 * dtrow_r
                q1 = t1 * gm
                m = q1 * dtrow_r
                dacol = dacol + jnp.sum(m, axis=1, keepdims=True) * (lane_id == hd).astype(F32)
                darowf = darowf - (sub_id == hd).astype(F32) * jnp.sum(m, axis=0, keepdims=True)
                ddtrowf = ddtrowf + (sub_id == hd).astype(F32) * jnp.sum(q1, axis=0, keepdims=True)
                dact_ref[:, pl.ds(hd * SSM_HEAD_DIM, SSM_HEAD_DIM)] = (
                    dx_r + dx_state[:, cs] + dyg[:, cs] * dsk_ref[:, pl.ds(hd * SSM_HEAD_DIM, SSM_HEAD_DIM)])
            dgmb = dgm.astype(BF16)
            dact_ref[:, pl.ds(SSM_WIDTH + g * SSM_STATE, SSM_STATE)] = dbt.T + _dot(dgm.T.astype(BF16), cgb)
            dact_ref[:, pl.ds(SSM_WIDTH + 2 * SSM_STATE + g * SSM_STATE, SSM_STATE)] = dcg + _dot(dgmb, bgb)
            dh_s[:, gs] = dh_in
        t = dwgt * wgt
        dalast = jnp.sum(t, axis=0, keepdims=True) + delast * elast
        dacol_tot = dacol - t + darowf.T + (sub_id == CHUNK - 1).astype(F32) * dalast
        dda = _dot01_l(trit3_ref[...], dacol_tot)
        ddt = dwgt * wgt0 + ddtrowf.T + dda * a
        ddtr = jnp.where(valid, ddt * _sigmoid(dtr_), 0.0)
        ddtr_ref[...] = ddtr
        da_ref[...] += _colsum8(dda * dt) * a
        dbias_ref[...] += _colsum8(ddtr)

    rev = lambda c: nsteps - 1 - c
    rb = cps * CHUNK
    xs_spec = pl.BlockSpec((rb, SSM_WIDTH), lambda c: (rev(c), 0))
    dact, ddtr, da8, dbias8, *received = pl.pallas_call(
        body, name="ssd_bwd", grid=(nsteps,),
        in_specs=[xs_spec, xs_spec,
                  pl.BlockSpec((rb, 2 * SSM_STATE), lambda c: (rev(c), nb)),
                  pl.BlockSpec((rb, 2 * SSM_STATE), lambda c: (rev(c), nb + 1)),
                  pl.BlockSpec((rb, LANE), lambda c: (rev(c), 0)),
                  pl.BlockSpec((cps, SSM_STATE, SSM_WIDTH), lambda c: (rev(c), 0, 0)),
                  _fullspec((1, LANE)), _fullspec((1, LANE)), _fullspec((1, SSM_WIDTH)),
                  _fullspec((CHUNK, CHUNK)), _fullspec((CHUNK, 3 * CHUNK)), _fullspec((CHUNK, 3 * CHUNK)),
                  _fullspec((3 * LANE, SSM_WIDTH)), _fullspec((SSM_GROUPS, 3 * gw, LANE))] + [ANY] * na,
        out_specs=[pl.BlockSpec((rb, CONV_DIM), lambda c: (rev(c), 0)), pl.BlockSpec((rb, LANE), lambda c: (rev(c), 0)),
                   _fullspec((8, LANE)), _fullspec((8, LANE))] + [ANY] * na,
        out_shape=[_sds((lp, CONV_DIM), F32), _sds((lp, LANE), F32), _sds((8, LANE), F32), _sds((8, LANE), F32)]
        + [_sds(e.shape, e.dtype) for e in exchange],
        scratch_shapes=[pltpu.VMEM((SSM_STATE, SSM_WIDTH), F32)] + (_gather_scratch(na) if na else []),
        compiler_params=_cp(("arbitrary",)),
    )(dy, xact, xact, xact, dtr, hprev, dt_bias, a_log, dskip, _tri_mat(), _x3(_tri_mat(), 1), _x3(_tri_mat().T, 1),
      _x3(_expand_mat(), 0), jnp.stack([_x3(_expand_mat().T[g * gw:(g + 1) * gw], 0) for g in range(SSM_GROUPS)]),
      *exchange)
    return dact, ddtr, da8, dbias8, received


def _conv_bwd(dact, xbc, cw, cb):
    lp, c = xbc.shape
    t8 = ROWB // 8
    nb = lp // ROWB

    def body(d_ref, dnext_ref, x_ref, prev_ref, next_ref, w_ref, b_ref, dx_ref, dw_ref, db_ref, xb, dp):
        i = pl.program_id(0)
        last = i == nb - 1
        xb[pl.ds(0, 8), :] = jnp.where(i > 0, prev_ref[...], 0.0)
        xb[pl.ds(8, ROWB), :] = x_ref[...]
        xb[pl.ds(8 + ROWB, 8), :] = jnp.where(last, 0.0, next_ref[...])

        @pl.when(i == 0)
        def _():
            dw_ref[...] = jnp.zeros_like(dw_ref)
            db_ref[...] = jnp.zeros_like(db_ref)

        sub = lax.broadcasted_iota(jnp.int32, (8, 1), 0)
        x0 = 8 - (CONV_K - 1)

        def strip(s, carry):
            cs = pl.ds(pl.multiple_of(s * LANE, LANE), LANE)
            w, b = w_ref[:, cs], b_ref[:, cs]

            def dpre_rows(r0, n, d):
                xs = _shifted_rows(xb, r0, n, cs, [x0 + kk for kk in range(CONV_K)])
                pre = b + sum(w[kk:kk + 1, :] * xs[kk] for kk in range(CONV_K))
                sg = _sigmoid(pre)
                return d * (sg * (1.0 + pre * (1.0 - sg))), xs

            dws = [jnp.zeros((8, LANE), F32) for _ in range(CONV_K)]
            dbs = jnp.zeros((8, LANE), F32)
            for r0 in range(0, ROWB, CONV_ROWS):
                dpre, xs = dpre_rows(r0, CONV_ROWS, d_ref[pl.ds(r0, CONV_ROWS), cs])
                dp[pl.ds(r0, CONV_ROWS), cs] = dpre
                dbs = dbs + _colsum8(dpre)
                for kk in range(CONV_K):
                    dws[kk] = dws[kk] + _colsum8(dpre * xs[kk])
            dp[pl.ds(ROWB, 8), cs] = dpre_rows(ROWB, 8, jnp.where(last, 0.0, dnext_ref[:, cs]))[0]
            dwv = sum(jnp.where(sub == kk, jnp.sum(dws[kk], axis=0, keepdims=True), 0.0) for kk in range(CONV_K))
            dw_ref[:, cs] += dwv
            db_ref[:, cs] += dbs
            for r0 in range(0, ROWB, CONV_ROWS):
                ahead = _shifted_rows(dp, r0, CONV_ROWS, cs, [CONV_K - 1 - kk for kk in range(CONV_K)])
                dx = sum(w[kk:kk + 1, :] * ahead[kk] for kk in range(CONV_K))
                dx_ref[pl.ds(r0, CONV_ROWS), cs] = dx.astype(BF16)
            return carry

        lax.fori_loop(0, c // LANE, strip, 0)

    nxt = lambda i: (jnp.minimum((i + 1) * t8, lp // 8 - 1), 0)
    prv = lambda i: (jnp.maximum(i * t8 - 1, 0), 0)
    return pl.pallas_call(
        body, name="conv_bwd", grid=(nb,),
        in_specs=[_rowspec(ROWB, c), pl.BlockSpec((8, c), nxt), _rowspec(ROWB, c), pl.BlockSpec((8, c), prv),
                  pl.BlockSpec((8, c), nxt), _fullspec((8, c)), _fullspec((1, c))],
        out_specs=[_rowspec(ROWB, c), _fullspec((8, c)), _fullspec((8, c))],
        out_shape=[_sds((lp, c), BF16), _sds((8, c), F32), _sds((8, c), F32)],
        scratch_shapes=[pltpu.VMEM((ROWB + 16, c), F32), pltpu.VMEM((ROWB + 8, c), F32)],
        compiler_params=_cp(("arbitrary",)),
    )(dact, dact, xbc, xbc, xbc, cw, cb)


def _flash_bwd(q, k, v, datt, lse_row, delta_row, cos, sa, sb):
    lp = q.shape[0]
    nk = lp // ROWB

    def body(k_ref, v_ref, q_ref, do_ref, lse_ref, dl_ref, cos_ref, sa_ref, sb_ref, dq_ref, dk_ref, dv_ref,
             dq_acc, dk_acc, dv_acc):
        j = pl.program_id(1)

        @pl.when(j == 0)
        def _():
            dq_acc[...] = jnp.zeros_like(dq_acc)

        dk_acc[...] = jnp.zeros_like(dk_acc)
        dv_acc[...] = jnp.zeros_like(dv_acc)

        def tile(i, masked, key0=0, nkeys=ROWB):
            keys = pl.ds(key0, nkeys)
            kb, vb = k_ref[keys, :], v_ref[keys, :]
            off = pl.multiple_of(i * ROWB, ROWB)
            qb = q_ref[pl.ds(off, ROWB), :]
            dob = do_ref[pl.ds(off, ROWB), :]
            lse_r = lse_ref[0, :, pl.ds(off, ROWB)]
            dl_r = dl_ref[0, :, pl.ds(off, ROWB)]
            st = _dot(kb, qb, NT)
            if masked:
                krow = j * ROWB + key0 + lax.broadcasted_iota(jnp.int32, st.shape, 0)
                qrow = i * ROWB + lax.broadcasted_iota(jnp.int32, st.shape, 1)
                st = jnp.where(_att_ok(qrow, krow), st, NEG)
            pt = jnp.exp2(st - lse_r)
            dv_acc[keys, :] += _dot(pt.astype(BF16), dob)
            dpt = _dot(vb, dob, NT)
            dst = (pt * (dpt - dl_r)).astype(BF16)
            dk_acc[keys, :] += _dot(dst, qb)
            dq_acc[pl.ds(off, ROWB), :] += _dot(dst, kb, ((0,), (0,)))

        @pl.when(j == 0)
        def _():
            _pair_loop(0, nk, lambda i: tile(i, True, ROWB - META_KEYS, META_KEYS), (4, 2))

        odd = jnp.bitwise_and(nk - 1 - j, 3)
        for r in range(4):
            @pl.when((j > 0) & (odd == r))
            def _(r=r):
                tile(j, True)
                for d in range(r):
                    tile(j + 1 + d, False)

        @pl.when(j > 0)
        def _():
            _pair_loop(j + 1 + odd, nk, lambda i: tile(i, False), (16, 8, 4))

        dk_ref[...] = (dk_acc[...] * LN2).astype(BF16)
        dv_ref[...] = dv_acc[...].astype(BF16)
        dq = dq_acc[pl.ds(pl.multiple_of(j * ROWB, ROWB), ROWB), :] * ATT_SCALE
        half = HEADW // 2
        dq_ref[:, pl.ds(0, half)] = dq[:, :half].astype(BF16)
        dq_ref[:, pl.ds(half, half)] = _rope_t(dq[:, half:], cos_ref[...], sa_ref[...], sb_ref[...]).astype(BF16)

    stat = pl.BlockSpec((1, 1, lp), lambda h, j: (h, 0, 0))
    blk = pl.BlockSpec((ROWB, HEADW), lambda h, j: (j, h))
    tab = pl.BlockSpec((ROWB, HEADW // 2), lambda h, j: (j, 0))
    return pl.pallas_call(
        body, name="flash_bwd", grid=(ATT_HEADS, nk),
        in_specs=[blk, pl.BlockSpec((ROWB, V_HEAD), lambda h, j: (j, 2 * h)),
                  pl.BlockSpec((lp, HEADW), lambda h, j: (0, h)), pl.BlockSpec((lp, V_HEAD), lambda h, j: (0, h)),
                  stat, stat, tab, tab, tab],
        out_specs=[blk, blk, pl.BlockSpec((ROWB, V_HEAD), lambda h, j: (j, h))],
        out_shape=[_sds((lp, ATT_HEADS * HEADW), BF16), _sds((lp, ATT_HEADS * HEADW), BF16),
                   _sds((lp, ATT_HEADS * V_HEAD), BF16)],
        scratch_shapes=[pltpu.VMEM((lp, HEADW), F32), pltpu.VMEM((ROWB, HEADW), F32), pltpu.VMEM((ROWB, V_HEAD), F32)],
        compiler_params=_cp(("arbitrary", "arbitrary")),
    )(k, v, q, datt, lse_row, delta_row, cos, sa, sb)


def _qkv_bwd(dqp, dk, dv, cq, ckv, cos, sa, sb, gq, gkv, wq, wk, wv, dz, dxbc, ddtr):
    lp = cq.shape[0]
    qw = ATT_HEADS * HEADW

    def body(dqp_ref, dk_ref, dv_ref, cq_ref, ckv_ref, cos_ref, sa_ref, sb_ref, gq_ref, gkv_ref, wq_ref, wk_ref, wv_ref,
             dz_ref, dxbc_ref, ddtr_ref, dp_ref, dgq_ref, dgkv_ref):
        dcq, dgq = _rms_bwd(cq_ref[...], gq_ref[...], _dot(dqp_ref[...], wq_ref[...], NT))
        dp_ref[:, pl.ds(PC_Q, Q_LORA)] = dcq.astype(BF16)
        dkb = dk_ref[...]
        half = HEADW // 2
        dksum = sum(dkb[:, hh * HEADW + half:(hh + 1) * HEADW].astype(F32) for hh in range(ATT_HEADS))
        dp_ref[:, pl.ds(PC_KR, half)] = jnp.zeros((ROWB, half), BF16)
        dp_ref[:, pl.ds(PC_KR + half, half)] = _rope_t(dksum, cos_ref[...], sa_ref[...], sb_ref[...]).astype(BF16)
        dckvn = _dot(dkb, wk_ref[...], NT) + _dot(dv_ref[...], wv_ref[...], NT)
        dckv, dgkv = _rms_bwd(ckv_ref[...], gkv_ref[...], dckvn)
        dp_ref[:, pl.ds(PC_KV, KV_LORA)] = dckv.astype(BF16)
        dp_ref[:, pl.ds(PC_Z, SSM_WIDTH)] = dz_ref[...]
        dp_ref[:, pl.ds(PC_XBC, CONV_DIM)] = dxbc_ref[...]
        dp_ref[:, pl.ds(PC_DT, LANE)] = ddtr_ref[...].astype(BF16)
        _acc(dgq_ref, _colsum8(dgq))
        _acc(dgkv_ref, _colsum8(dgkv))

    return pl.pallas_call(
        body, name="qkv_bwd", grid=(lp // ROWB,),
        in_specs=[_rowspec(ROWB, qw), _rowspec(ROWB, qw), _rowspec(ROWB, ATT_HEADS * V_HEAD),
                  _rowspec(ROWB, Q_LORA), _rowspec(ROWB, KV_LORA)] + [_rowspec(ROWB, HEADW // 2)] * 3
        + [_fullspec((1, Q_LORA)), _fullspec((1, KV_LORA)), _fullspec((Q_LORA, qw)), _fullspec((KV_LORA, qw)),
           _fullspec((KV_LORA, ATT_HEADS * V_HEAD)), _rowspec(ROWB, SSM_WIDTH), _rowspec(ROWB, CONV_DIM),
           _rowspec(ROWB, LANE)],
        out_specs=[_rowspec(ROWB, PROJ_W), _fullspec((8, Q_LORA)), _fullspec((8, KV_LORA))],
        out_shape=[_sds((lp, PROJ_W), BF16), _sds((8, Q_LORA), F32), _sds((8, KV_LORA), F32)],
        compiler_params=_cp(("arbitrary",)),
    )(dqp, dk, dv, cq, ckv, cos, sa, sb, gq, gkv, wq, wk, wv, dz, dxbc, ddtr)


def _in_bwd(dproj, head, x, dh1, g, w_all, exchange=()):
    lp = dh1.shape[0]
    nsteps = lp // ROWB
    na = len(exchange)

    def body(dp_ref, head_ref, x_ref, dh1_ref, g_ref, w_ref, *rest):
        xin, (dx_ref, dhead_ref, dg_ref), xout, sems = rest[:na], rest[na:na + 3], rest[na + 3:2 * na + 3], rest[2 * na + 3:]
        step = pl.program_id(0)
        if na:
            x_start, x_finish = _exchange_ops(xin, xout, *sems)
            pl.when(step == 0)(x_start)
        dx, dg = _rms_bwd(_h_block(head_ref, x_ref), g_ref[...], _dot(dp_ref[...], w_ref[...], NT))
        dh = dh1_ref[...] + dx

        @pl.when(step == 0)
        def _():
            dhead_ref[...] = dh

        @pl.when(step > 0)
        def _():
            dx_ref[...] = dh

        _acc(dg_ref, _colsum8(dg))
        if na:
            pl.when(step == nsteps - 1)(x_finish)

    dx, dhead, dg8, *received = pl.pallas_call(
        body, name="in_bwd", grid=(nsteps,),
        in_specs=[_rowspec(ROWB, PROJ_W), _fullspec((ROWB, D_MODEL)), _xspec(), _rowspec(ROWB, D_MODEL),
                  _fullspec((1, D_MODEL)), _fullspec((D_MODEL, PROJ_W))] + [ANY] * na,
        out_specs=[_xspec(), _fullspec((ROWB, D_MODEL)), _fullspec((8, D_MODEL))] + [ANY] * na,
        out_shape=[_sds(x.shape, F32), _sds((ROWB, D_MODEL), F32), _sds((8, D_MODEL), F32)]
        + [_sds(e.shape, e.dtype) for e in exchange],
        scratch_shapes=_gather_scratch(na) if na else [],
        compiler_params=_cp(("arbitrary",)),
    )(dproj, head, x, dh1, g, w_all, *exchange)
    return dx, dhead, dg8, received


def _tile_of(n, cap=1024):
    return max(t for t in range(LANE, min(n, cap) + 1, LANE) if n % t == 0)


def _matmul_tn(name, a, b):
    rows, kd = a.shape
    nd = b.shape[1]
    tk, tn = _tile_of(kd), _tile_of(nd)
    rb = 3 * ROWB if rows % (3 * ROWB) == 0 else ROWB

    def body(a_ref, b_ref, o_ref):
        @pl.when(pl.program_id(2) == 0)
        def _():
            o_ref[...] = jnp.zeros_like(o_ref)

        o_ref[...] += _dot(a_ref[...], b_ref[...], ((0,), (0,)))

    return pl.pallas_call(
        body, name=name, grid=(kd // tk, nd // tn, rows // rb),
        in_specs=[pl.BlockSpec((rb, tk), lambda i, j, r: (r, i)), pl.BlockSpec((rb, tn), lambda i, j, r: (r, j))],
        out_specs=pl.BlockSpec((tk, tn), lambda i, j, r: (i, j)), out_shape=_sds((kd, nd), F32),
        compiler_params=_cp(("arbitrary", "arbitrary", "arbitrary")),
    )(a, b)


def _local_backward(head, x, f, exchange_late=False):
    p = f["p"]
    g = {}
    row = lambda v: _row1(v)
    s8 = lambda v: jnp.sum(v, axis=0)
    dh1, du, df, dgpre, dgpost = _mlp_bwd(f["dh2"], f["f"], f["h1"], f["u"], p["w_mlp_up"], p["w_mlp_down"],
                                          row(p["norm_mlp_pre"]), row(p["norm_mlp_post"]))
    g["norm_mlp_pre"], g["norm_mlp_post"] = s8(dgpre), s8(dgpost)
    g["w_mlp_up"] = _matmul_tn("dw_mlp_up", f["hn2"], du)
    g["w_mlp_down"] = _matmul_tn("dw_mlp_down", f["a"], df)
    dmix, datt, dy, dz, dgmp, delta, dgn, dd = _out_bwd(dh1, f["mix"], f["att"], f["y"], f["xact"], f["z"], f["dskip"],
                                                        row(p["ssm_norm"]), p["w_out"], row(p["norm_mix_post"]))
    g["norm_mix_post"] = s8(dgmp)
    g["w_out"] = jnp.concatenate([_matmul_tn("dw_out_att", f["att"], dmix), _matmul_tn("dw_out_ssm", f["ssm"], dmix)], axis=0)
    g["ssm_norm"] = s8(dgn)
    g["d_skip"] = s8(dd).reshape(SSM_HEADS, SSM_HEAD_DIM).sum(axis=1)
    dact, ddtr, da8, dbias8, received = _ssd_bwd(
        dy, f["xact"], f["dtr"], f["hprev"], f["dt_bias"], f["a_log"], f["dskip"], f["seq_rows"],
        exchange=[_to_chunks(n, g[n]).astype(BF16) for n in LATE] if exchange_late else ())
    g["a_log"], g["dt_bias"] = s8(da8)[:SSM_HEADS], s8(dbias8)[:SSM_HEADS]
    dxbc, dcw8, dcb8 = _conv_bwd(dact, f["xbc"], f["cw"], row(p["conv_b"]))
    g["conv_w"], g["conv_b"] = dcw8[:CONV_K], s8(dcb8)
    dqp, dkb, dv = _flash_bwd(f["q"], f["k"], f["v"], datt, f["lse"], delta, *f["rope"])
    dproj, dgq, dgkv = _qkv_bwd(dqp, dkb, dv, f["cq"], f["ckv"], *f["rope"], row(p["q_a_norm"]), row(p["kv_a_norm"]),
                                f["wq"], f["wk"], f["wv"], dz, dxbc, ddtr)
    g["q_a_norm"], g["kv_a_norm"] = s8(dgq), s8(dgkv)
    dwq = _matmul_tn("dw_q_up", f["cqn"], dqp).reshape(Q_LORA, ATT_HEADS, HEADW)
    g["w_q_up"] = dwq[:, :, :QK_NOPE + QK_ROPE].reshape(Q_LORA, -1)
    dwk = _matmul_tn("dw_k_up", f["ckvn"], dkb).reshape(KV_LORA, ATT_HEADS, HEADW)[:, :, :QK_NOPE]
    dwv = _matmul_tn("dw_v_up", f["ckvn"], dv).reshape(KV_LORA, ATT_HEADS, V_HEAD)
    g["w_kv_up"] = jnp.concatenate([dwk, dwv], axis=2).reshape(KV_LORA, -1)
    dwa = _matmul_tn("dw_in", f["hn"], dproj)
    g["w_in"] = jnp.concatenate([dwa[:, PC_Q:PC_KR], dwa[:, PC_KR + QK_NOPE:PC_KR + QK_NOPE + QK_ROPE],
                                 dwa[:, PC_Z:PC_DT + SSM_HEADS]], axis=1)
    dx, dhead, dgin, received_mid = _in_bwd(
        dproj, head, x, dh1, row(p["norm_mix_pre"]), f["w_all"],
        exchange=[_to_chunks(n, g[n]).astype(BF16) for n in MID] if exchange_late else ())
    g["norm_mix_pre"] = s8(dgin)
    g["meta_tokens"] = dhead[PADF:]
    return dx, g, dict(zip(LATE + MID, list(received) + list(received_mid)))


BIG = {"w_in": ((D_MODEL, IN_WIDTH), 1), "w_q_up": ((Q_LORA, ATT_HEADS * (QK_NOPE + QK_ROPE)), 1),
       "w_kv_up": ((KV_LORA, ATT_HEADS * (QK_NOPE + V_HEAD)), 1), "w_out": ((2 * D_MODEL, D_MODEL), 0),
       "w_mlp_up": ((D_MODEL, D_FF), 1), "w_mlp_down": ((D_FF, D_MODEL), 0), "conv_w": ((CONV_K, CONV_DIM), 1),
       "meta_tokens": ((N_META, D_MODEL), 1)}
SMALL = {"norm_mix_pre": D_MODEL, "q_a_norm": Q_LORA, "kv_a_norm": KV_LORA, "conv_b": CONV_DIM, "dt_bias": SSM_HEADS,
         "a_log": SSM_HEADS, "d_skip": SSM_HEADS, "ssm_norm": SSM_WIDTH, "norm_mix_post": D_MODEL,
         "norm_mlp_pre": D_MODEL, "norm_mlp_post": D_MODEL}
WEIGHT_ORDER = ("meta_tokens", "norm_mix_pre", "w_in", "q_a_norm", "w_q_up", "kv_a_norm", "w_kv_up", "conv_w", "conv_b",
                "dt_bias", "a_log", "d_skip", "ssm_norm", "w_out", "norm_mix_post", "norm_mlp_pre", "w_mlp_up",
                "w_mlp_down", "norm_mlp_post")
ADAM_ROWS = 256


def _shard_shape(name):
    shape, ax = BIG[name]
    return tuple(d // N_DEV if a == ax else d for a, d in enumerate(shape))


SMALL_ROWS = -(-sum(SMALL.values()) // (LANE * 8)) * 8


def _pack(flats, rows):
    v = jnp.concatenate([f.reshape(-1) for f in flats])
    return jnp.pad(v, (0, rows * LANE - v.shape[0])).reshape(rows, LANE)


def _unpack(packed, shapes):
    v = packed.reshape(-1)
    out, o = [], 0
    for s in shapes:
        n = math.prod(s)
        out.append(v[o:o + n].reshape(s))
        o += n
    return out


def _to_chunks(name, full):
    shape, ax = BIG[name]
    if ax == 0:
        return full.reshape((N_DEV,) + _shard_shape(name))
    k, n = shape
    return full.reshape(k, N_DEV, n // N_DEV).transpose(1, 0, 2)


def _from_shards(name, shards):
    shape, ax = BIG[name]
    if ax == 0:
        return shards.reshape(shape)
    return shards.transpose(1, 0, 2).reshape(shape)


def _peer(k):
    x, y, c = lax.axis_index("x"), lax.axis_index("y"), lax.axis_index("c")
    px = 1 - x if k & 4 else x
    py = 1 - y if k & 2 else y
    pc = 1 - c if k & 1 else c
    return (px, py, pc), 4 * px + 2 * py + pc


def _gather_ops(x_refs, out_refs, send_sems, recv_sems, local_sems):
    na = len(x_refs)
    chips = (4, 2, 6)

    def copy(a, n, block, to, src=None):
        return pltpu.make_async_remote_copy(
            src_ref=out_refs[a].at[block] if src is None else src, dst_ref=out_refs[a].at[block],
            send_sem=send_sems.at[7 * a + n], recv_sem=recv_sems.at[7 * a + n], device_id=to, device_id_type=MESH)

    def mine():
        me = _peer(0)[1]
        return [pltpu.make_async_copy(x_refs[a], out_refs[a].at[me], local_sems.at[a]) for a in range(na)]

    def first():
        me, sibling = _peer(0)[1], _peer(1)[0]
        out = [copy(a, 0, me, sibling, src=x_refs[a]) for a in range(na)]
        return out + [copy(a, 1 + n, me, _peer(k)[0], src=x_refs[a]) for n, k in enumerate(chips) for a in range(na)]

    def passed():
        sibling = _peer(1)[0]
        return [copy(a, 4 + n, _peer(k)[1], sibling) for n, k in enumerate(chips) for a in range(na)]

    def start():
        for cp in mine() + first():
            cp.start()

    def forward():
        sibling = _peer(1)[0]
        fwd = passed()
        for n, k in enumerate(chips):
            for a in range(na):
                copy(a, 1 + n, _peer(k)[1], sibling).wait_recv()
                fwd[n * na + a].start()

    def finish():
        sibling = _peer(1)[0]
        for a in range(na):
            copy(a, 0, _peer(1)[1], sibling).wait_recv()
        for n, k in enumerate(chips):
            for a in range(na):
                copy(a, 4 + n, _peer(k | 1)[1], sibling).wait_recv()
        for cp in first() + passed():
            cp.wait_send()
        for cp in mine():
            cp.wait()

    return start, forward, finish


def _gather_scratch(na):
    return [pltpu.SemaphoreType.DMA((7 * na,)), pltpu.SemaphoreType.DMA((7 * na,)), pltpu.SemaphoreType.DMA((na,))]


def _all_gather(shards):
    na = len(shards)

    def body(*refs):
        for step in _gather_ops(refs[:na], refs[na:2 * na], *refs[2 * na:]):
            step()

    return pl.pallas_call(
        body, name="all_gather_weights", out_shape=[_sds((N_DEV,) + s.shape, s.dtype) for s in shards],
        in_specs=[ANY] * na, out_specs=[ANY] * na, scratch_shapes=_gather_scratch(na),
    )(*shards)


def _exchange(chunks, small):
    na = len(chunks) + 1

    def body(*refs):
        for step in _exchange_ops(refs[:na], refs[na:2 * na], *refs[2 * na:], whole=(na - 1,)):
            step()

    arrays = list(chunks) + [small]
    return pl.pallas_call(
        body, name="exchange_grads",
        out_shape=[_sds(c.shape, c.dtype) for c in chunks] + [_sds((N_DEV,) + small.shape, small.dtype)],
        in_specs=[ANY] * na, out_specs=[ANY] * na, scratch_shapes=_gather_scratch(na),
    )(*arrays)


def _exchange_ops(in_refs, out_refs, send_sems, recv_sems, local_sems, whole=()):
    na = len(in_refs)

    def src(a, idx):
        return in_refs[a] if a in whole else in_refs[a].at[idx]

    def own():
        me = _peer(0)[1]
        return [pltpu.make_async_copy(src(a, me), out_refs[a].at[me], local_sems.at[a]) for a in range(na)]

    def copy(a, k, sending):
        me = _peer(0)[1]
        to, idx = _peer(k)
        return pltpu.make_async_remote_copy(
            src_ref=src(a, idx if sending else me), dst_ref=out_refs[a].at[me if sending else idx],
            send_sem=send_sems.at[7 * a + k - 1], recv_sem=recv_sems.at[7 * a + k - 1],
            device_id=to, device_id_type=MESH)

    def sent():
        return [copy(a, k, True) for k in range(1, N_DEV) for a in range(na)]

    def start():
        for cp in own() + sent():
            cp.start()

    def finish():
        for k in range(1, N_DEV):
            for a in range(na):
                copy(a, k, False).wait_recv()
        for cp in sent():
            cp.wait_send()
        for cp in own():
            cp.wait()

    return start, finish


def _reduce_adamw(name, recv, w, m, v):
    rows, cols = w.shape
    blk = ADAM_ROWS if rows % ADAM_ROWS == 0 else rows
    c1 = 1.0 - ADAM_B1 ** ADAM_STEP
    c2 = 1.0 - ADAM_B2 ** ADAM_STEP

    def body(r_ref, w_ref, m_ref, v_ref, g_ref, d_ref, nm_ref, nv_ref):
        g = r_ref[0].astype(F32)
        for s in range(1, N_DEV):
            g = g + r_ref[s].astype(F32)
        g_ref[...] = g
        m_ = ADAM_B1 * m_ref[...] + (1.0 - ADAM_B1) * g
        v_ = ADAM_B2 * v_ref[...] + (1.0 - ADAM_B2) * (g * g)
        nm_ref[...] = m_
        nv_ref[...] = v_
        d_ref[...] = -ADAM_LR * ((m_ / c1) / (jnp.sqrt(v_ / c2) + ADAM_EPS) + ADAM_WD * w_ref[...])

    spec = _rowspec(blk, cols)
    return pl.pallas_call(
        body, name="reduce_adamw_" + name, grid=(rows // blk,),
        in_specs=[pl.BlockSpec((N_DEV, blk, cols), lambda i: (0, i, 0)), spec, spec, spec],
        out_specs=[spec] * 4, out_shape=[_sds((rows, cols), F32)] * 4,
        compiler_params=_cp(("arbitrary",)),
    )(recv, w, m, v)


def kernel(x, meta_tokens, norm_mix_pre, w_in, q_a_norm, w_q_up, kv_a_norm, w_kv_up, conv_w, conv_b, dt_bias, a_log, d_skip, ssm_norm, w_out, norm_mix_post, norm_mlp_pre, w_mlp_up, w_mlp_down, norm_mlp_post, loss_target, m_meta_tokens, m_norm_mix_pre, m_w_in, m_q_a_norm, m_w_q_up, m_kv_a_norm, m_w_kv_up, m_conv_w, m_conv_b, m_dt_bias, m_a_log, m_d_skip, m_ssm_norm, m_w_out, m_norm_mix_post, m_norm_mlp_pre, m_w_mlp_up, m_w_mlp_down, m_norm_mlp_post, v_meta_tokens, v_norm_mix_pre, v_w_in, v_q_a_norm, v_w_q_up, v_kv_a_norm, v_w_kv_up, v_conv_w, v_conv_b, v_dt_bias, v_a_log, v_d_skip, v_ssm_norm, v_w_out, v_norm_mix_post, v_norm_mlp_pre, v_w_mlp_up, v_w_mlp_down, v_norm_mlp_post):
    w = dict(meta_tokens=meta_tokens, norm_mix_pre=norm_mix_pre, w_in=w_in, q_a_norm=q_a_norm, w_q_up=w_q_up,
             kv_a_norm=kv_a_norm, w_kv_up=w_kv_up, conv_w=conv_w, conv_b=conv_b, dt_bias=dt_bias, a_log=a_log,
             d_skip=d_skip, ssm_norm=ssm_norm, w_out=w_out, norm_mix_post=norm_mix_post, norm_mlp_pre=norm_mlp_pre,
             w_mlp_up=w_mlp_up, w_mlp_down=w_mlp_down, norm_mlp_post=norm_mlp_post)
    m = dict(meta_tokens=m_meta_tokens, norm_mix_pre=m_norm_mix_pre, w_in=m_w_in, q_a_norm=m_q_a_norm, w_q_up=m_w_q_up,
             kv_a_norm=m_kv_a_norm, w_kv_up=m_w_kv_up, conv_w=m_conv_w, conv_b=m_conv_b, dt_bias=m_dt_bias,
             a_log=m_a_log, d_skip=m_d_skip, ssm_norm=m_ssm_norm, w_out=m_w_out, norm_mix_post=m_norm_mix_post,
             norm_mlp_pre=m_norm_mlp_pre, w_mlp_up=m_w_mlp_up, w_mlp_down=m_w_mlp_down, norm_mlp_post=m_norm_mlp_post)
    v = dict(meta_tokens=v_meta_tokens, norm_mix_pre=v_norm_mix_pre, w_in=v_w_in, q_a_norm=v_q_a_norm, w_q_up=v_w_q_up,
             kv_a_norm=v_kv_a_norm, w_kv_up=v_w_kv_up, conv_w=v_conv_w, conv_b=v_conv_b, dt_bias=v_dt_bias,
             a_log=v_a_log, d_skip=v_d_skip, ssm_norm=v_ssm_norm, w_out=v_w_out, norm_mix_post=v_norm_mix_post,
             norm_mlp_pre=v_norm_mlp_pre, w_mlp_up=v_w_mlp_up, w_mlp_down=v_w_mlp_down, norm_mlp_post=v_norm_mlp_post)
    big_names = [n for n in WEIGHT_ORDER if n in BIG]
    small_names = [n for n in WEIGHT_ORDER if n in SMALL]
    shard = lambda d, n: d[n].reshape(_shard_shape(n))

    f32_names = ("conv_w", "meta_tokens")
    early = [n for n in big_names if n not in LATE]
    gathered = _all_gather([shard(w, n).astype(F32 if n in f32_names else BF16) for n in early])
    p = {n: w[n].reshape(-1) for n in small_names}
    p.update({n: _from_shards(n, s) for n, s in zip(early, gathered)})
    head = jnp.concatenate([jnp.zeros((PADF, D_MODEL), F32), p["meta_tokens"]], axis=0)
    f = _local_forward(head, x[0], loss_target[0], p, late={n: shard(w, n).astype(BF16) for n in LATE})
    dx, g, recv_of = _local_backward(head, x[0], f, exchange_late=True)
    grad_x = dx[None]
    loss = lax.psum(f["loss"], ("x", "y", "c"))

    rest = [n for n in big_names if n not in LATE + MID]
    small = _pack([g[n] for n in small_names], SMALL_ROWS)
    *recv_rest, recv_small = _exchange([_to_chunks(n, g[n]) for n in rest], small)
    recv_of.update(zip(rest, recv_rest))

    outs = {}
    kinds = ("grad", "delta", "new_m", "new_v")
    for n, recv in ((n, recv_of[n]) for n in big_names):
        for kind, arr in zip(kinds, _reduce_adamw(n, recv, shard(w, n), shard(m, n), shard(v, n))):
            outs[kind, n] = arr.reshape(w[n].shape)
    packed = [_pack([d[n] for n in small_names], SMALL_ROWS) for d in (w, m, v)]
    for kind, arr in zip(kinds, _reduce_adamw("small", recv_small, *packed)):
        for n, val in zip(small_names, _unpack(arr, [(SMALL[n],) for n in small_names])):
            outs[kind, n] = val.reshape(w[n].shape)
    return (loss, grad_x) + tuple(outs[kind, n] for kind in ("grad", "delta", "new_m", "new_v") for n in WEIGHT_ORDER)
```

```python
import math

import jax
import jax.numpy as jnp
import numpy as np
from jax import lax
from jax.experimental import pallas as pl
from jax.experimental.pallas import tpu as pltpu

F32 = jnp.float32
BF16 = jnp.bfloat16

D_MODEL = 1024
N_META = 16
EPS = 1e-6
ATT_HEADS = 8
Q_LORA = 384
KV_LORA = 256
QK_NOPE = 128
QK_ROPE = 64
V_HEAD = 128
ROPE_THETA = 10000.0
SSM_HEADS = 16
SSM_HEAD_DIM = 64
SSM_WIDTH = 1024
SSM_GROUPS = 2
SSM_STATE = 128
CONV_K = 4
CHUNK = 128
CONV_DIM = 1536
D_FF = 4096
IN_SPLITS = (Q_LORA, KV_LORA, QK_ROPE, SSM_WIDTH, CONV_DIM, SSM_HEADS)
IN_WIDTH = sum(IN_SPLITS)
ADAM_LR, ADAM_B1, ADAM_B2, ADAM_EPS, ADAM_WD, ADAM_STEP = 0.001, 0.9, 0.999, 1e-08, 0.01, 10

LANE = 128
ROWB = 512
PADF = ROWB - N_META
HEADW = 256
PC_Q, PC_KV, PC_KR, PC_Z, PC_XBC, PC_DT, PROJ_W = 0, 384, 640, 896, 1920, 3456, 3584
NEG = -1e30
N_DEV = 8
VMEM_LIMIT = 56 * 1024 * 1024
MESH = pl.DeviceIdType.MESH


def _cp(sem, vmem=VMEM_LIMIT, **kw):
    return pltpu.CompilerParams(dimension_semantics=sem, vmem_limit_bytes=vmem, **kw)


def _dot(a, b, dims=((1,), (0,))):
    return lax.dot_general(a, b, (dims, ((), ())), preferred_element_type=F32)


def _bdot(a, b, dims=((1,), (0,))):
    return _dot(a.astype(BF16), b.astype(BF16), dims)


NT = ((1,), (1,))


def _rms_fwd(x, w):
    r = lax.rsqrt(jnp.mean(x * x, axis=-1, keepdims=True) + EPS)
    return (x * r) * w


def _rms_bwd(x, w, dy):
    r = lax.rsqrt(jnp.mean(x * x, axis=-1, keepdims=True) + EPS)
    xh = x * r
    g = dy * w
    dx = r * (g - xh * jnp.mean(g * xh, axis=-1, keepdims=True))
    return dx, dy * xh


def _sigmoid(x):
    return 0.5 * jnp.tanh(0.5 * x) + 0.5


def _colsum8(x):
    t, c = x.shape
    return jnp.sum(x.reshape(t // 8, 8, c), axis=0)


def _rowspec(t, c, cb=0):
    return pl.BlockSpec((t, c), lambda i: (i, cb))


def _fullspec(shape):
    n = len(shape)
    return pl.BlockSpec(shape, lambda i: (0,) * n)


def _sds(shape, dt):
    return jax.ShapeDtypeStruct(shape, dt)


def _acc(ref, val):
    @pl.when(pl.program_id(0) == 0)
    def _():
        ref[...] = jnp.zeros_like(ref)

    ref[...] += val


def _xspec():
    return pl.BlockSpec((ROWB, D_MODEL), lambda i: (jnp.maximum(i - 1, 0), 0))


def _h_block(head_ref, x_ref):
    return jnp.where(pl.program_id(0) == 0, head_ref[...], x_ref[...])


def _norm_in_proj(head, x, g, w_all):
    lp = head.shape[0] + x.shape[0]

    def body(head_ref, x_ref, g_ref, w_ref, hn_ref, cq_ref, ckv_ref, kr_ref, z_ref, xbc_ref, dt_ref):
        hn = _rms_fwd(_h_block(head_ref, x_ref), g_ref[...]).astype(BF16)
        hn_ref[...] = hn
        p = _dot(hn, w_ref[...])
        cq_ref[...] = p[:, PC_Q:PC_KV]
        ckv_ref[...] = p[:, PC_KV:PC_KR]
        kr_ref[...] = p[:, PC_KR:PC_Z]
        z_ref[...] = p[:, PC_Z:PC_XBC]
        xbc_ref[...] = p[:, PC_XBC:PC_DT]
        dt_ref[...] = p[:, PC_DT:PROJ_W]

    widths = (Q_LORA, KV_LORA, HEADW, SSM_WIDTH, CONV_DIM, LANE)
    return pl.pallas_call(
        body, name="norm_in_proj", grid=(lp // ROWB,),
        in_specs=[_fullspec((ROWB, D_MODEL)), _xspec(), _fullspec((1, D_MODEL)), _fullspec((D_MODEL, PROJ_W))],
        out_specs=[_rowspec(ROWB, D_MODEL)] + [_rowspec(ROWB, w) for w in widths],
        out_shape=[_sds((lp, D_MODEL), BF16)] + [_sds((lp, w), F32) for w in widths],
        compiler_params=_cp(("arbitrary",)),
    )(head, x, g, w_all)


def _rope(x, cos, sa, sb):
    w = x.shape[1]
    return x * cos + pltpu.roll(x, w - 32, 1) * sa + pltpu.roll(x, 32, 1) * sb


def _rope_t(dy, cos, sa, sb):
    w = dy.shape[1]
    return dy * cos + pltpu.roll(dy * sa, 32, 1) + pltpu.roll(dy * sb, w - 32, 1)


def _qkv(cq, ckv, kr, cos, sa, sb, gq, gkv, wq, wk, wv):
    lp = cq.shape[0]
    qw = ATT_HEADS * HEADW
    half = HEADW // 2
    assert half == QK_NOPE == V_HEAD == LANE

    def body(cq_ref, ckv_ref, kr_ref, cos_ref, sa_ref, sb_ref, gq_ref, gkv_ref, wq_ref, wk_ref, wv_ref,
             q_ref, k_ref, v_ref, cqn_ref, ckvn_ref):
        tabs = [cos_ref[...], sa_ref[...], sb_ref[...]]
        cqn = _rms_fwd(cq_ref[...], gq_ref[...]).astype(BF16)
        ckvn = _rms_fwd(ckv_ref[...], gkv_ref[...]).astype(BF16)
        cqn_ref[...] = cqn
        ckvn_ref[...] = ckvn
        q = _dot(cqn, wq_ref[...])
        kn = _dot(ckvn, wk_ref[...])
        vv = _dot(ckvn, wv_ref[...])
        krope = _rope(kr_ref[:, pl.ds(half, half)], *tabs).astype(BF16)
        ones = jnp.ones((ROWB, half), BF16)
        for hh in range(ATT_HEADS):
            lo, hi, src = pl.ds(hh * HEADW, half), pl.ds(hh * HEADW + half, half), slice(hh * half, (hh + 1) * half)
            q_ref[:, lo] = (q[:, hh * HEADW:hh * HEADW + half] * Q_PRESCALE).astype(BF16)
            q_ref[:, hi] = (_rope(q[:, hh * HEADW + half:(hh + 1) * HEADW], *tabs) * Q_PRESCALE).astype(BF16)
            k_ref[:, lo] = kn[:, src].astype(BF16)
            k_ref[:, hi] = krope
            v_ref[:, lo] = vv[:, src].astype(BF16)
            v_ref[:, hi] = ones

    return pl.pallas_call(
        body, name="qkv", grid=(lp // ROWB,),
        in_specs=[_rowspec(ROWB, Q_LORA), _rowspec(ROWB, KV_LORA), _rowspec(ROWB, HEADW)]
        + [_rowspec(ROWB, half)] * 3
        + [_fullspec((1, Q_LORA)), _fullspec((1, KV_LORA)), _fullspec((Q_LORA, qw)),
           _fullspec((KV_LORA, ATT_HEADS * QK_NOPE)), _fullspec((KV_LORA, ATT_HEADS * V_HEAD))],
        out_specs=[_rowspec(ROWB, qw), _rowspec(ROWB, qw), _rowspec(ROWB, qw),
                   _rowspec(ROWB, Q_LORA), _rowspec(ROWB, KV_LORA)],
        out_shape=[_sds((lp, qw), BF16), _sds((lp, qw), BF16), _sds((lp, qw), BF16),
                   _sds((lp, Q_LORA), BF16), _sds((lp, KV_LORA), BF16)],
        compiler_params=_cp(("arbitrary",)),
    )(cq, ckv, kr, cos, sa, sb, gq, gkv, wq, wk, wv)


ATT_SCALE = (QK_NOPE + QK_ROPE) ** -0.5
LOG2E = 1.4426950408889634
LN2 = 0.6931471805599453
Q_PRESCALE = ATT_SCALE * LOG2E
KVB = 512
META_KEYS = LANE
assert N_META <= META_KEYS


def _att_ok(qrow, krow):
    return (krow <= qrow) & ((krow >= PADF) | (qrow < PADF))


def _lanes(x, n):
    return x if n == 1 else jnp.concatenate([x] * n, axis=1)


def _pair_loop(lo, hi, tile, unrolls=(2,)):
    for u in tuple(unrolls) + (1,):
        n = jnp.maximum(hi - lo, 0)
        trips = n // u

        def many(t, c, u=u, lo=lo):
            for d in range(u):
                tile(lo + u * t + d)
            return c

        lax.fori_loop(0, trips, many, 0)
        lo = lo + trips * u


def _flash_fwd(q, k, v):
    lp = q.shape[0]
    nq = lp // ROWB

    def body(q_ref, k_ref, v_ref, o_ref, lse_ref, acc, m_s):
        i = pl.program_id(1)
        qb = q_ref[...]
        m_s[...] = jnp.full_like(m_s, NEG)
        acc[...] = jnp.zeros_like(acc)

        def tile(j, masked, off=None, nkeys=KVB):
            off = pl.multiple_of(j * KVB, KVB) if off is None else off
            kb = k_ref[pl.ds(off, nkeys), :]
            vb = v_ref[pl.ds(off, nkeys), :]
            s = _dot(qb, kb, NT)
            if masked:
                qrow = i * ROWB + lax.broadcasted_iota(jnp.int32, s.shape, 0)
                krow = off + lax.broadcasted_iota(jnp.int32, s.shape, 1)
                s = jnp.where(_att_ok(qrow, krow), s, NEG)
            m_prev = m_s[...]
            m_new = jnp.maximum(m_prev, jnp.max(s, axis=1, keepdims=True))
            alpha = jnp.exp2(m_prev - m_new)
            p = jnp.exp2(s - _lanes(m_new, nkeys // LANE))
            acc[...] = _lanes(alpha, 2) * acc[...] + _dot(p.astype(BF16), vb)
            m_s[...] = m_new

        def first_tile():
            tile(0, True, off=ROWB - META_KEYS, nkeys=META_KEYS)

        @pl.when(i == 0)
        def _():
            first_tile()

        odd = jnp.bitwise_and(jnp.maximum(i - 1, 0), 7)
        for r in range(8):
            @pl.when((i > 0) & (odd == r))
            def _(r=r):
                first_tile()
                tile(i, True)
                for d in range(r):
                    tile(1 + d, False)

        _pair_loop(1 + odd, i, lambda j: tile(j, False), (16, 8))
        l = acc[:, V_HEAD:]
        o_ref[...] = (acc[:, :V_HEAD] / l).astype(BF16)
        lse_ref[0] = (m_s[...] + jnp.log2(l)).T[0:1, :]

    return pl.pallas_call(
        body, name="flash_fwd", grid=(ATT_HEADS, nq),
        in_specs=[pl.BlockSpec((ROWB, HEADW), lambda h, i: (i, h)),
                  pl.BlockSpec((lp, HEADW), lambda h, i: (0, h)),
                  pl.BlockSpec((lp, HEADW), lambda h, i: (0, h))],
        out_specs=[pl.BlockSpec((ROWB, V_HEAD), lambda h, i: (i, h)),
                   pl.BlockSpec((1, 1, ROWB), lambda h, i: (h, 0, i))],
        out_shape=[_sds((lp, ATT_HEADS * V_HEAD), BF16), _sds((ATT_HEADS, 1, lp), F32)],
        scratch_shapes=[pltpu.VMEM((ROWB, HEADW), F32), pltpu.VMEM((ROWB, LANE), F32)],
        compiler_params=_cp(("arbitrary", "arbitrary")),
    )(q, k, v)


def _silu(x):
    return x * _sigmoid(x)


CONV_ROWS = 64


def _window_rows(win, n, offsets):
    return [win[o:o + n] if o % 8 == 0 else pltpu.roll(win, n + 8 - o, 0)[0:n] for o in offsets]


def _shifted_rows(ref, start, n, cols, offsets):
    return _window_rows(ref[pl.ds(start, n + 8), cols], n, offsets)


def _block_window(x_ref, halo, r0, n, cols):
    if r0 == 0:
        return jnp.concatenate([halo, x_ref[pl.ds(0, n), cols]], axis=0)
    return x_ref[pl.ds(r0 - 8, n + 8), cols]


def _conv_fwd(xbc, cw, cb):
    lp, c = xbc.shape
    t8 = ROWB // 8

    def body(x_ref, prev_ref, w_ref, b_ref, o_ref):
        i = pl.program_id(0)

        def strip(s, carry):
            cs = pl.ds(pl.multiple_of(s * LANE, LANE), LANE)
            w, b = w_ref[:, cs], b_ref[:, cs]
            halo = jnp.where(i > 0, prev_ref[:, cs], 0.0)
            for r0 in range(0, ROWB, CONV_ROWS):
                win = _block_window(x_ref, halo, r0, CONV_ROWS, cs)
                taps = _window_rows(win, CONV_ROWS, [8 - (CONV_K - 1) + kk for kk in range(CONV_K)])
                pre = b + sum(w[kk:kk + 1, :] * taps[kk] for kk in range(CONV_K))
                o_ref[pl.ds(r0, CONV_ROWS), cs] = _silu(pre)
            return carry

        lax.fori_loop(0, c // LANE, strip, 0)

    return pl.pallas_call(
        body, name="conv_fwd", grid=(lp // ROWB,),
        in_specs=[_rowspec(ROWB, c), pl.BlockSpec((8, c), lambda i: (jnp.maximum(i * t8 - 1, 0), 0)),
                  _fullspec((8, c)), _fullspec((1, c))],
        out_specs=_rowspec(ROWB, c), out_shape=_sds((lp, c), F32),
        compiler_params=_cp(("arbitrary",)),
    )(xbc, xbc, cw, cb)


def _expand_mat():
    r = np.arange(LANE)[:, None]
    c = np.arange(SSM_WIDTH)[None, :]
    return jnp.asarray((c // SSM_HEAD_DIM == r).astype(np.float32))


def _tri_mat():
    i = np.arange(CHUNK)
    return jnp.asarray((i[:, None] >= i[None, :]).astype(np.float32))


def _x3(m, axis):
    return jnp.concatenate([m.astype(BF16)] * 3, axis=axis)


def _split3(x):
    hi = x.astype(BF16)
    r = x - hi.astype(F32)
    mid = r.astype(BF16)
    return hi, mid, (r - mid.astype(F32)).astype(BF16)


def _dot01_r(x, m3):
    return _dot(jnp.concatenate(_split3(x), axis=1), m3)


def _dot01_l(m3, x):
    return _dot(m3, jnp.concatenate(_split3(x), axis=0))


def _ssd_prep(dt_raw, bias_ref, alog_ref, tri3, c, seq_rows):
    rows = c * CHUNK + lax.broadcasted_iota(jnp.int32, (CHUNK, LANE), 0)
    lanes = lax.broadcasted_iota(jnp.int32, (CHUNK, LANE), 1)
    valid = (rows >= PADF) & (rows < PADF + seq_rows) & (lanes < SSM_HEADS)
    dtr = dt_raw + bias_ref[...]
    sp = jnp.maximum(dtr, 0.0) + jnp.log(1.0 + jnp.exp(-jnp.abs(dtr)))
    dt = jnp.where(valid, sp, 0.0)
    a = -jnp.exp(alog_ref[...])
    acol = _dot01_l(tri3, dt * a)
    return dt, a, acol, valid, dtr


def _row16(v):
    return jnp.broadcast_to(v, (16, v.shape[1]))


def _ssd_fwd(xbc_act, dtr, dt_bias, a_log, seq_rows, gather=()):
    lp = xbc_act.shape[0]
    nc = lp // CHUNK
    cps = ROWB // CHUNK
    nsteps = nc // cps
    gw = SSM_WIDTH // SSM_GROUPS
    hpg = SSM_HEADS // SSM_GROUPS

    na = len(gather)

    def body(x_ref, b_ref, c_ref, dtr_ref, bias_ref, alog_ref, tri_ref, tri3_ref, ex3_ref, *rest):
        gin, (y_ref, hp_ref), gout, h_s, sems = rest[:na], rest[na:na + 2], rest[na + 2:2 * na + 2], rest[2 * na + 2], rest[2 * na + 3:]
        step = pl.program_id(0)

        @pl.when(step == 0)
        def _():
            h_s[...] = jnp.zeros_like(h_s)

        if na:
            g_start, g_forward, g_finish = _gather_ops(gin, gout, *sems)
            pl.when(step == 0)(g_start)
            pl.when(step == nsteps // 2)(g_forward)

        ex3 = ex3_ref[...]
        causal = tri_ref[...] > 0.5
        for cc in range(cps):
            rows = pl.ds(cc * CHUNK, CHUNK)
            dt, a, acol, _, _ = _ssd_prep(dtr_ref[rows, :], bias_ref, alog_ref, tri3_ref[...], step * cps + cc, seq_rows)
            arow = acol.T
            dtrow = dt.T
            alast = acol[CHUNK - 1:CHUNK, :]
            e_all = _dot01_r(jnp.exp(acol), ex3)
            wx_all = _dot01_r(jnp.exp(alast - acol) * dt, ex3)
            dec_all = _dot01_r(_row16(jnp.exp(alast)), ex3)[0:1, :]
            hp_ref[cc] = h_s[...]
            for g in range(SSM_GROUPS):
                gs = slice(g * gw, (g + 1) * gw)
                bg = b_ref[rows, g * SSM_STATE:(g + 1) * SSM_STATE]
                cg = c_ref[rows, g * SSM_STATE:(g + 1) * SSM_STATE].astype(BF16)
                xg = x_ref[rows, gs]
                hg = h_s[:, gs]
                gm = _bdot(cg, bg, NT)
                y_off = _bdot(cg, hg) * e_all[:, gs]
                for r in range(hpg):
                    hd = g * hpg + r
                    seg = acol[:, hd:hd + 1] - arow[hd:hd + 1, :]
                    lm = jnp.where(causal, jnp.exp(jnp.where(causal, seg, 0.0)), 0.0)
                    w = gm * lm * dtrow[hd:hd + 1, :]
                    cs = slice(r * SSM_HEAD_DIM, (r + 1) * SSM_HEAD_DIM)
                    y_ref[rows, pl.ds(hd * SSM_HEAD_DIM, SSM_HEAD_DIM)] = _bdot(w, xg[:, cs]) + y_off[:, cs]
                st = _bdot(bg.T, xg * wx_all[:, gs])
                h_s[:, gs] = hg * dec_all[:, gs] + st

        if na:
            pl.when(step == nsteps - 1)(g_finish)

    xs_spec = pl.BlockSpec((ROWB, SSM_WIDTH), lambda c: (c, 0))
    b_spec = pl.BlockSpec((ROWB, 2 * SSM_STATE), lambda c: (c, SSM_WIDTH // (2 * SSM_STATE)))
    c_spec = pl.BlockSpec((ROWB, 2 * SSM_STATE), lambda c: (c, SSM_WIDTH // (2 * SSM_STATE) + 1))
    y, hprev, *gathered = pl.pallas_call(
        body, name="ssd_fwd", grid=(nsteps,),
        in_specs=[xs_spec, b_spec, c_spec, pl.BlockSpec((ROWB, LANE), lambda c: (c, 0)),
                  _fullspec((1, LANE)), _fullspec((1, LANE)), _fullspec((CHUNK, CHUNK)), _fullspec((CHUNK, 3 * CHUNK)),
                  _fullspec((3 * LANE, SSM_WIDTH))] + [ANY] * na,
        out_specs=[xs_spec, pl.BlockSpec((cps, SSM_STATE, SSM_WIDTH), lambda c: (c, 0, 0))] + [ANY] * na,
        out_shape=[_sds((lp, SSM_WIDTH), F32), _sds((nc, SSM_STATE, SSM_WIDTH), F32)]
        + [_sds((N_DEV,) + s.shape, s.dtype) for s in gather],
        scratch_shapes=[pltpu.VMEM((SSM_STATE, SSM_WIDTH), F32)] + (_gather_scratch(na) if na else []),
        compiler_params=_cp(("arbitrary",)),
    )(xbc_act, xbc_act, xbc_act, dtr, dt_bias, a_log, _tri_mat(), _x3(_tri_mat(), 1), _x3(_expand_mat(), 0), *gather)
    return y, hprev, gathered


def _group_mean(x):
    gw = SSM_WIDTH // SSM_GROUPS
    parts = [jnp.broadcast_to(jnp.mean(x[:, g * gw:(g + 1) * gw], axis=-1, keepdims=True), (x.shape[0], gw))
             for g in range(SSM_GROUPS)]
    return jnp.concatenate(parts, axis=1)


def _out_proj(att, y, xbc_act, z, dskip, gnorm, head, x, w_out, g_post):
    lp = att.shape[0]

    def body(a_ref, y_ref, xs_ref, z_ref, d_ref, gn_ref, head_ref, x_ref, w_ref, g_ref, ssm_ref, mix_ref, h1_ref):
        gt = (y_ref[...] + d_ref[...] * xs_ref[...]) * _silu(z_ref[...])
        r = lax.rsqrt(_group_mean(gt * gt) + EPS)
        ssm = ((gt * r) * gn_ref[...]).astype(BF16)
        ssm_ref[...] = ssm
        mix = _dot(a_ref[...], w_ref[pl.ds(0, 1024), :]) + _dot(ssm, w_ref[pl.ds(1024, 1024), :])
        mix_ref[...] = mix
        h1_ref[...] = _h_block(head_ref, x_ref) + _rms_fwd(mix, g_ref[...])

    return pl.pallas_call(
        body, name="out_proj", grid=(lp // ROWB,),
        in_specs=[_rowspec(ROWB, 1024)] * 4 + [_fullspec((1, SSM_WIDTH))] * 2
        + [_fullspec((ROWB, D_MODEL)), _xspec(), _fullspec((2048, D_MODEL)), _fullspec((1, D_MODEL))],
        out_specs=[_rowspec(ROWB, SSM_WIDTH)] + [_rowspec(ROWB, D_MODEL)] * 2,
        out_shape=[_sds((lp, SSM_WIDTH), BF16)] + [_sds((lp, D_MODEL), F32)] * 2,
        compiler_params=_cp(("arbitrary",)),
    )(att, y, xbc_act, z, dskip, gnorm, head, x, w_out, g_post)


def _resident(w_hbm, w_vmem, sem):
    @pl.when(pl.program_id(0) == 0)
    def _():
        cp = pltpu.make_async_copy(w_hbm, w_vmem, sem)
        cp.start()
        cp.wait()


ANY = pl.BlockSpec(memory_space=pl.ANY)


def _mlp_fwd(h1, tgt, w_up, w_down, g_pre, g_post, seq_rows):
    lp = h1.shape[0]

    def body(h1_ref, t_ref, wu_hbm, wd_hbm, gpre_ref, gpost_ref, hn2_ref, u_ref, a_ref, f_ref, dh2_ref, loss_ref,
             wu, wd, sems):
        _resident(wu_hbm, wu, sems.at[0])
        _resident(wd_hbm, wd, sems.at[1])
        i = pl.program_id(0)
        h1_ = h1_ref[...]
        hn2 = _rms_fwd(h1_, gpre_ref[...]).astype(BF16)
        hn2_ref[...] = hn2
        u = jnp.maximum(_dot(hn2, wu[...]), 0.0)
        u_ref[...] = u.astype(BF16)
        a = (u * u).astype(BF16)
        a_ref[...] = a
        f = _dot(a, wd[...])
        f_ref[...] = f
        h2 = h1_ + _rms_fwd(f, gpost_ref[...])
        rows = i * ROWB + lax.broadcasted_iota(jnp.int32, (ROWB, 1), 0)
        real = (rows >= PADF + N_META) & (rows < PADF + seq_rows)
        err = jnp.where(real, h2 - t_ref[...], 0.0)
        dh2_ref[...] = err * (1.0 / D_MODEL)
        _acc(loss_ref, _colsum8(err * err))

    return pl.pallas_call(
        body, name="mlp_fwd", grid=(lp // ROWB,),
        in_specs=[_rowspec(ROWB, D_MODEL), _xspec()] + [ANY, ANY] + [_fullspec((1, D_MODEL))] * 2,
        out_specs=[_rowspec(ROWB, D_MODEL), _rowspec(ROWB, D_FF), _rowspec(ROWB, D_FF)] + [_rowspec(ROWB, D_MODEL)] * 2
        + [_fullspec((8, D_MODEL))],
        out_shape=[_sds((lp, D_MODEL), BF16), _sds((lp, D_FF), BF16), _sds((lp, D_FF), BF16), _sds((lp, D_MODEL), F32),
                   _sds((lp, D_MODEL), F32), _sds((8, D_MODEL), F32)],
        scratch_shapes=[pltpu.VMEM((D_MODEL, D_FF), BF16), pltpu.VMEM((D_FF, D_MODEL), BF16), pltpu.SemaphoreType.DMA((2,))],
        compiler_params=_cp(("arbitrary",)),
    )(h1, tgt, w_up, w_down, g_pre, g_post)


def _pad_cols(w, width):
    return jnp.pad(w, ((0, 0), (0, width - w.shape[1])))


def _layout_weights(w_in, w_q_up, w_kv_up):
    o = np.cumsum((0,) + IN_SPLITS)
    pieces = [w_in[:, o[k]:o[k + 1]] for k in range(6)]
    kr = jnp.pad(pieces[2], ((0, 0), (QK_NOPE, HEADW - QK_NOPE - QK_ROPE)))
    w_all = jnp.concatenate([pieces[0], pieces[1], kr, pieces[3], pieces[4], _pad_cols(pieces[5], LANE)], axis=1)
    wq = jnp.pad(w_q_up.reshape(Q_LORA, ATT_HEADS, QK_NOPE + QK_ROPE), ((0, 0), (0, 0), (0, HEADW - QK_NOPE - QK_ROPE)))
    wkv = w_kv_up.reshape(KV_LORA, ATT_HEADS, QK_NOPE + V_HEAD)
    wk = jnp.pad(wkv[:, :, :QK_NOPE], ((0, 0), (0, 0), (0, HEADW - QK_NOPE)))
    wv = wkv[:, :, QK_NOPE:]
    return (w_all, wq.reshape(Q_LORA, -1), wk.reshape(KV_LORA, -1), wv.reshape(KV_LORA, -1),
            wkv[:, :, :QK_NOPE].reshape(KV_LORA, -1))


def _rope_tables(lp):
    pos = jnp.maximum(jnp.arange(lp, dtype=jnp.int32) - PADF, 0).astype(F32)
    inv_freq = ROPE_THETA ** (-jnp.arange(0, QK_ROPE, 2, dtype=F32) / QK_ROPE)
    ang = pos[:, None] * inv_freq[None, :]
    cos, sin = jnp.cos(ang), jnp.sin(ang)
    z32, z64 = jnp.zeros((lp, 32), F32), jnp.zeros((lp, 64), F32)
    cos_t = jnp.concatenate([cos, cos, jnp.ones((lp, 64), F32)], axis=1)
    sa = jnp.concatenate([-sin, z32, z64], axis=1)
    sb = jnp.concatenate([z32, sin, z64], axis=1)
    return cos_t, sa, sb


def _row1(v, width=None):
    v = v.reshape(1, -1).astype(F32)
    return v if width is None else _pad_cols(v, width)


LATE = ("w_out", "w_mlp_up", "w_mlp_down")
MID = ("w_in", "w_q_up", "w_kv_up", "conv_w")


def _local_forward(head, x, tgt, p, late=None):
    assert head.shape[0] == ROWB and x.shape[0] % ROWB == 0
    lp = ROWB + x.shape[0]
    seq_rows = N_META + x.shape[0]
    f = {"seq_rows": seq_rows}
    w_all, wq, wk, wv, wkn = _layout_weights(p["w_in"], p["w_q_up"], p["w_kv_up"])
    f.update(w_all=w_all, wq=wq, wk=wk, wv=wv)
    f["hn"], cq, ckv, kr, f["z"], f["xbc"], f["dtr"] = _norm_in_proj(head, x, _row1(p["norm_mix_pre"]), w_all)
    f.update(cq=cq, ckv=ckv)
    f["rope"] = _rope_tables(lp)
    f["q"], f["k"], f["v"], f["cqn"], f["ckvn"] = _qkv(cq, ckv, kr, *f["rope"], _row1(p["q_a_norm"]),
                                                   _row1(p["kv_a_norm"]), wq, wkn, wv)
    f["att"], f["lse"] = _flash_fwd(f["q"], f["k"], f["v"])
    f["cw"] = jnp.pad(p["conv_w"].astype(F32), ((0, 8 - CONV_K), (0, 0)))
    f["xact"] = _conv_fwd(f["xbc"], f["cw"], _row1(p["conv_b"]))
    f["dt_bias"], f["a_log"] = _row1(p["dt_bias"], LANE), _row1(p["a_log"], LANE)
    f["y"], f["hprev"], gathered = _ssd_fwd(f["xact"], f["dtr"], f["dt_bias"], f["a_log"], seq_rows,
                                            gather=[late[n] for n in LATE] if late else ())
    p = {**p, **{n: _from_shards(n, s) for n, s in zip(LATE, gathered)}}
    f["p"] = p
    f["dskip"] = jnp.repeat(p["d_skip"].reshape(-1).astype(F32), SSM_HEAD_DIM).reshape(1, SSM_WIDTH)
    f["ssm"], f["mix"], f["h1"] = _out_proj(f["att"], f["y"], f["xact"], f["z"], f["dskip"], _row1(p["ssm_norm"]),
                                            head, x, p["w_out"], _row1(p["norm_mix_post"]))
    f["hn2"], f["u"], f["a"], f["f"], f["dh2"], loss8 = _mlp_fwd(
        f["h1"], tgt, p["w_mlp_up"], p["w_mlp_down"], _row1(p["norm_mlp_pre"]), _row1(p["norm_mlp_post"]), seq_rows)
    f["loss"] = 0.5 * jnp.sum(loss8) / D_MODEL
    return f


MLPB = 256


def _mlp_bwd(dh2, f, h1, u, w_up, w_down, g_pre, g_post):
    lp = h1.shape[0]

    def body(dh2_ref, f_ref, h1_ref, u_ref, wu_hbm, wd_hbm, gpre_ref, gpost_ref,
             dh1_ref, du_ref, df_ref, dgpre_ref, dgpost_ref, wu, wd, sems):
        _resident(wu_hbm, wu, sems.at[0])
        _resident(wd_hbm, wd, sems.at[1])
        dh2_ = dh2_ref[...]
        df, dgp = _rms_bwd(f_ref[...], gpost_ref[...], dh2_)
        dfb = df.astype(BF16)
        df_ref[...] = dfb
        da = _dot(dfb, wd[...], NT)
        du = (da * (2.0 * u_ref[...].astype(F32))).astype(BF16)
        du_ref[...] = du
        dhn2 = _dot(du, wu[...], NT)
        dx, dgq = _rms_bwd(h1_ref[...], gpre_ref[...], dhn2)
        dh1_ref[...] = dh2_ + dx
        _acc(dgpre_ref, _colsum8(dgq))
        _acc(dgpost_ref, _colsum8(dgp))

    return pl.pallas_call(
        body, name="mlp_bwd", grid=(lp // MLPB,),
        in_specs=[_rowspec(MLPB, D_MODEL)] * 3 + [_rowspec(MLPB, D_FF), ANY, ANY] + [_fullspec((1, D_MODEL))] * 2,
        out_specs=[_rowspec(MLPB, D_MODEL), _rowspec(MLPB, D_FF), _rowspec(MLPB, D_MODEL),
                   _fullspec((8, D_MODEL)), _fullspec((8, D_MODEL))],
        out_shape=[_sds((lp, D_MODEL), F32), _sds((lp, D_FF), BF16), _sds((lp, D_MODEL), BF16),
                   _sds((8, D_MODEL), F32), _sds((8, D_MODEL), F32)],
        scratch_shapes=[pltpu.VMEM((D_MODEL, D_FF), BF16), pltpu.VMEM((D_FF, D_MODEL), BF16), pltpu.SemaphoreType.DMA((2,))],
        compiler_params=_cp(("arbitrary",)),
    )(dh2, f, h1, u, w_up, w_down, g_pre, g_post)


def _out_bwd(dh1, mix, att, y, xact, z, dskip, gnorm, w_out, g_post):
    lp = dh1.shape[0]

    def body(dh1_ref, mix_ref, att_ref, y_ref, x_ref, z_ref, d_ref, gn_ref, w_ref, g_ref,
             dmix_ref, datt_ref, dy_ref, dz_ref, dg_ref, dl_ref, dgn_ref, dd_ref):
        dmix, dg = _rms_bwd(mix_ref[...], g_ref[...], dh1_ref[...])
        dmb = dmix.astype(BF16)
        dmix_ref[...] = dmb
        datt = _dot(dmb, w_ref[pl.ds(0, 1024), :], NT).astype(BF16)
        datt_ref[...] = datt
        _acc(dg_ref, _colsum8(dg))
        prod = datt.astype(F32) * att_ref[...].astype(F32)
        for hh in range(ATT_HEADS):
            d = jnp.sum(prod[:, hh * V_HEAD:(hh + 1) * V_HEAD], axis=1, keepdims=True)
            dl_ref[hh] = jnp.broadcast_to(d, (ROWB, LANE)).T[0:1, :]
        do = _dot(dmb, w_ref[pl.ds(1024, 1024), :], NT)
        z_, x_ = z_ref[...], x_ref[...]
        sg = _sigmoid(z_)
        sz = z_ * sg
        y2 = y_ref[...] + d_ref[...] * x_
        gt = y2 * sz
        r = lax.rsqrt(_group_mean(gt * gt) + EPS)
        gh = gt * r
        dgh = do * gn_ref[...]
        dgt = r * (dgh - gh * _group_mean(dgh * gh))
        dy2 = dgt * sz
        dy_ref[...] = dy2
        dz_ref[...] = (dgt * y2 * (sg * (1.0 + z_ * (1.0 - sg)))).astype(BF16)
        _acc(dgn_ref, _colsum8(do * gh))
        _acc(dd_ref, _colsum8(dy2 * x_))

    return pl.pallas_call(
        body, name="out_bwd", grid=(lp // ROWB,),
        in_specs=[_rowspec(ROWB, D_MODEL)] * 6 + [_fullspec((1, SSM_WIDTH))] * 2
        + [_fullspec((2048, D_MODEL)), _fullspec((1, D_MODEL))],
        out_specs=[_rowspec(ROWB, D_MODEL)] * 4 + [_fullspec((8, D_MODEL)),
                                                   pl.BlockSpec((ATT_HEADS, 1, ROWB), lambda i: (0, 0, i)),
                                                   _fullspec((8, SSM_WIDTH)), _fullspec((8, SSM_WIDTH))],
        out_shape=[_sds((lp, D_MODEL), BF16), _sds((lp, 1024), BF16), _sds((lp, SSM_WIDTH), F32),
                   _sds((lp, SSM_WIDTH), BF16), _sds((8, D_MODEL), F32), _sds((ATT_HEADS, 1, lp), F32),
                   _sds((8, SSM_WIDTH), F32), _sds((8, SSM_WIDTH), F32)],
        compiler_params=_cp(("arbitrary",)),
    )(dh1, mix, att, y, xact, z, dskip, gnorm, w_out, g_post)


def _ssd_bwd(dy, xact, dtr, hprev, dt_bias, a_log, dskip, seq_rows, exchange=()):
    lp = xact.shape[0]
    nc = lp // CHUNK
    gw = SSM_WIDTH // SSM_GROUPS
    hpg = SSM_HEADS // SSM_GROUPS
    nb = SSM_WIDTH // (2 * SSM_STATE)
    na = len(exchange)
    cps = ROWB // CHUNK
    nsteps = nc // cps

    def body(dy_ref, x_ref, b_ref, c_ref, dtr_ref, hp_ref, bias_ref, alog_ref, dsk_ref, tri_ref, tri3_ref, trit3_ref,
             ex3_ref, ext3_ref, *rest):
        xin, (dact_ref, ddtr_ref, da_ref, dbias_ref), xout = rest[:na], rest[na:na + 4], rest[na + 4:2 * na + 4]
        dh_s, sems = rest[2 * na + 4], rest[2 * na + 5:]
        step = pl.program_id(0)

        @pl.when(step == 0)
        def _():
            dh_s[...] = jnp.zeros_like(dh_s)
            da_ref[...] = jnp.zeros_like(da_ref)
            dbias_ref[...] = jnp.zeros_like(dbias_ref)

        if na:
            x_start, x_finish = _exchange_ops(xin, xout, *sems)
            pl.when(step == 0)(x_start)

        for lc in reversed(range(cps)):
            rows = pl.ds(lc * CHUNK, CHUNK)
            chunk((nsteps - 1 - step) * cps + lc, dy_ref.at[rows], x_ref.at[rows], b_ref.at[rows], c_ref.at[rows],
                  dtr_ref.at[rows], hp_ref.at[pl.ds(lc, 1)], bias_ref, alog_ref, dsk_ref, tri_ref, tri3_ref, trit3_ref,
                  ex3_ref, ext3_ref, dact_ref.at[rows], ddtr_ref.at[rows], da_ref, dbias_ref, dh_s)

        if na:
            pl.when(step == nsteps - 1)(x_finish)

    def chunk(c, dy_ref, x_ref, b_ref, c_ref, dtr_ref, hp_ref, bias_ref, alog_ref, dsk_ref, tri_ref, tri3_ref, trit3_ref,
              ex3_ref, ext3_ref, dact_ref, ddtr_ref, da_ref, dbias_ref, dh_s):
        tri = tri_ref[...]
        ex3 = ex3_ref[...]
        dt, a, acol, valid, dtr_ = _ssd_prep(dtr_ref[...], bias_ref, alog_ref, tri3_ref[...], c, seq_rows)
        arow = acol.T
        dtrow = dt.T
        alast = acol[CHUNK - 1:CHUNK, :]
        e_all = _dot01_r(jnp.exp(acol), ex3)
        wgt0 = jnp.exp(alast - acol)
        wgt = wgt0 * dt
        wx_all = _dot01_r(wgt, ex3)
        elast = jnp.exp(alast)
        dec_all = _dot01_r(_row16(elast), ex3)[0:1, :]
        causal = tri > 0.5
        upper = tri.T > 0.5
        lane_id = lax.broadcasted_iota(jnp.int32, (1, LANE), 1)
        sub_id = lax.broadcasted_iota(jnp.int32, (CHUNK, 1), 0)
        dacol = jnp.zeros((CHUNK, LANE), F32)
        darowf = jnp.zeros((CHUNK, LANE), F32)
        ddtrowf = jnp.zeros((CHUNK, LANE), F32)
        dwgt = jnp.zeros((CHUNK, LANE), F32)
        delast = jnp.zeros((1, LANE), F32)
        for g in range(SSM_GROUPS):
            gs = slice(g * gw, (g + 1) * gw)
            ext3_g = ext3_ref[g]
            bg = b_ref[:, g * SSM_STATE:(g + 1) * SSM_STATE]
            cg = c_ref[:, g * SSM_STATE:(g + 1) * SSM_STATE]
            bgb, cgb = bg.astype(BF16), cg.astype(BF16)
            xg = x_ref[:, gs]
            dyg = dy_ref[:, gs]
            hg = hp_ref[0, :, gs]
            dhg = dh_s[:, gs]
            hgb, dhgb = hg.astype(BF16), dhg.astype(BF16)
            gm = _dot(cgb, bgb, NT)
            gmt = _dot(bgb, cgb, NT)
            y_off = _dot(cgb, hgb) * e_all[:, gs]
            dy0 = (dyg * e_all[:, gs]).astype(BF16)
            dcg = _dot(dy0, hgb, NT)
            dh_in = _dot(cg.T.astype(BF16), dy0) + dhg * dec_all[:, gs]
            dacol = dacol + _dot01_r(dyg * y_off, ext3_g)
            xw = xg * wx_all[:, gs]
            dxw = _dot(bgb, dhgb)
            dx_state = dxw * wx_all[:, gs]
            dwgt = dwgt + _dot01_r(dxw * xg, ext3_g)
            dbt = _dot(dhgb, xw.astype(BF16), NT)
            hh = _colsum8(dhg * hg)
            hh16 = jnp.concatenate([hh, jnp.zeros_like(hh)], axis=0)
            delast = delast + jnp.sum(_dot01_r(hh16, ext3_g), axis=0, keepdims=True)
            dgm = jnp.zeros((CHUNK, CHUNK), F32)
            for r in range(hpg):
                hd = g * hpg + r
                cs = slice(r * SSM_HEAD_DIM, (r + 1) * SSM_HEAD_DIM)
                acol_r, arow_r = acol[:, hd:hd + 1], arow[hd:hd + 1, :]
                dtrow_r, dtcol_r = dtrow[hd:hd + 1, :], dt[:, hd:hd + 1]
                lm = jnp.where(causal, jnp.exp(jnp.where(causal, acol_r - arow_r, 0.0)), 0.0)
                lmt = jnp.where(upper, jnp.exp(jnp.where(upper, arow_r - acol_r, 0.0)), 0.0)
                wt = gmt * lmt * dtcol_r
                dy_r = dyg[:, cs].astype(BF16)
                dx_r = _dot(wt.astype(BF16), dy_r)
                dw = _dot(dy_r, xg[:, cs].astype(BF16), NT)
                t1 = dw * lm
                dgm = dgm + t1 * dtrow_r
                q1 = t1 * gm
                m = q1 * dtrow_r
                dacol = dacol + jnp.sum(m, axis=1, keepdims=True) * (lane_id == hd).astype(F32)
                darowf = darowf - (sub_id == hd).astype(F32) * jnp.sum(m, axis=0, keepdims=True)
                ddtrowf = ddtrowf + (sub_id == hd).astype(F32) * jnp.sum(q1, axis=0, keepdims=True)
                dact_ref[:, pl.ds(hd * SSM_HEAD_DIM, SSM_HEAD_DIM)] = (
                    dx_r + dx_state[:, cs] + dyg[:, cs] * dsk_ref[:, pl.ds(hd * SSM_HEAD_DIM, SSM_HEAD_DIM)])
            dgmb = dgm.astype(BF16)
            dact_ref[:, pl.ds(SSM_WIDTH + g * SSM_STATE, SSM_STATE)] = dbt.T + _dot(dgm.T.astype(BF16), cgb)
            dact_ref[:, pl.ds(SSM_WIDTH + 2 * SSM_STATE + g * SSM_STATE, SSM_STATE)] = dcg + _dot(dgmb, bgb)
            dh_s[:, gs] = dh_in
        t = dwgt * wgt
        dalast = jnp.sum(t, axis=0, keepdims=True) + delast * elast
        dacol_tot = dacol - t + darowf.T + (sub_id == CHUNK - 1).astype(F32) * dalast
        dda = _dot01_l(trit3_ref[...], dacol_tot)
        ddt = dwgt * wgt0 + ddtrowf.T + dda * a
        ddtr = jnp.where(valid, ddt * _sigmoid(dtr_), 0.0)
        ddtr_ref[...] = ddtr
        da_ref[...] += _colsum8(dda * dt) * a
        dbias_ref[...] += _colsum8(ddtr)

    rev = lambda c: nsteps - 1 - c
    rb = cps * CHUNK
    xs_spec = pl.BlockSpec((rb, SSM_WIDTH), lambda c: (rev(c), 0))
    dact, ddtr, da8, dbias8, *received = pl.pallas_call(
        body, name="ssd_bwd", grid=(nsteps,),
        in_specs=[xs_spec, xs_spec,
                  pl.BlockSpec((rb, 2 * SSM_STATE), lambda c: (rev(c), nb)),
                  pl.BlockSpec((rb, 2 * SSM_STATE), lambda c: (rev(c), nb + 1)),
                  pl.BlockSpec((rb, LANE), lambda c: (rev(c), 0)),
                  pl.BlockSpec((cps, SSM_STATE, SSM_WIDTH), lambda c: (rev(c), 0, 0)),
                  _fullspec((1, LANE)), _fullspec((1, LANE)), _fullspec((1, SSM_WIDTH)),
                  _fullspec((CHUNK, CHUNK)), _fullspec((CHUNK, 3 * CHUNK)), _fullspec((CHUNK, 3 * CHUNK)),
                  _fullspec((3 * LANE, SSM_WIDTH)), _fullspec((SSM_GROUPS, 3 * gw, LANE))] + [ANY] * na,
        out_specs=[pl.BlockSpec((rb, CONV_DIM), lambda c: (rev(c), 0)), pl.BlockSpec((rb, LANE), lambda c: (rev(c), 0)),
                   _fullspec((8, LANE)), _fullspec((8, LANE))] + [ANY] * na,
        out_shape=[_sds((lp, CONV_DIM), F32), _sds((lp, LANE), F32), _sds((8, LANE), F32), _sds((8, LANE), F32)]
        + [_sds(e.shape, e.dtype) for e in exchange],
        scratch_shapes=[pltpu.VMEM((SSM_STATE, SSM_WIDTH), F32)] + (_gather_scratch(na) if na else []),
        compiler_params=_cp(("arbitrary",)),
    )(dy, xact, xact, xact, dtr, hprev, dt_bias, a_log, dskip, _tri_mat(), _x3(_tri_mat(), 1), _x3(_tri_mat().T, 1),
      _x3(_expand_mat(), 0), jnp.stack([_x3(_expand_mat().T[g * gw:(g + 1) * gw], 0) for g in range(SSM_GROUPS)]),
      *exchange)
    return dact, ddtr, da8, dbias8, received


def _conv_bwd(dact, xbc, cw, cb):
    lp, c = xbc.shape
    t8 = ROWB // 8
    nb = lp // ROWB

    def body(d_ref, dnext_ref, x_ref, prev_ref, next_ref, w_ref, b_ref, dx_ref, dw_ref, db_ref, dp):
        i = pl.program_id(0)
        last = i == nb - 1

        @pl.when(i == 0)
        def _():
            dw_ref[...] = jnp.zeros_like(dw_ref)
            db_ref[...] = jnp.zeros_like(db_ref)

        sub = lax.broadcasted_iota(jnp.int32, (8, 1), 0)
        x0 = 8 - (CONV_K - 1)

        def strip(s, carry):
            cs = pl.ds(pl.multiple_of(s * LANE, LANE), LANE)
            w, b = w_ref[:, cs], b_ref[:, cs]

            halo = jnp.where(i > 0, prev_ref[:, cs], 0.0)
            after = jnp.where(last, 0.0, next_ref[:, cs])

            def dpre_rows(r0, n, d):
                win = (jnp.concatenate([x_ref[pl.ds(ROWB - 8, 8), cs], after], axis=0) if r0 == ROWB
                       else _block_window(x_ref, halo, r0, n, cs))
                xs = _window_rows(win, n, [x0 + kk for kk in range(CONV_K)])
                pre = b + sum(w[kk:kk + 1, :] * xs[kk] for kk in range(CONV_K))
                sg = _sigmoid(pre)
                return d * (sg * (1.0 + pre * (1.0 - sg))), xs

            dws = [jnp.zeros((8, LANE), F32) for _ in range(CONV_K)]
            dbs = jnp.zeros((8, LANE), F32)
            for r0 in range(0, ROWB, CONV_ROWS):
                dpre, xs = dpre_rows(r0, CONV_ROWS, d_ref[pl.ds(r0, CONV_ROWS), cs])
                dp[pl.ds(r0, CONV_ROWS), cs] = dpre
                dbs = dbs + _colsum8(dpre)
                for kk in range(CONV_K):
                    dws[kk] = dws[kk] + _colsum8(dpre * xs[kk])
            dp[pl.ds(ROWB, 8), cs] = dpre_rows(ROWB, 8, jnp.where(last, 0.0, dnext_ref[:, cs]))[0]
            dwv = sum(jnp.where(sub == kk, jnp.sum(dws[kk], axis=0, keepdims=True), 0.0) for kk in range(CONV_K))
            dw_ref[:, cs] += dwv
            db_ref[:, cs] += dbs
            for r0 in range(0, ROWB, CONV_ROWS):
                ahead = _shifted_rows(dp, r0, CONV_ROWS, cs, [CONV_K - 1 - kk for kk in range(CONV_K)])
                dx = sum(w[kk:kk + 1, :] * ahead[kk] for kk in range(CONV_K))
                dx_ref[pl.ds(r0, CONV_ROWS), cs] = dx.astype(BF16)
            return carry

        lax.fori_loop(0, c // LANE, strip, 0)

    nxt = lambda i: (jnp.minimum((i + 1) * t8, lp // 8 - 1), 0)
    prv = lambda i: (jnp.maximum(i * t8 - 1, 0), 0)
    return pl.pallas_call(
        body, name="conv_bwd", grid=(nb,),
        in_specs=[_rowspec(ROWB, c), pl.BlockSpec((8, c), nxt), _rowspec(ROWB, c), pl.BlockSpec((8, c), prv),
                  pl.BlockSpec((8, c), nxt), _fullspec((8, c)), _fullspec((1, c))],
        out_specs=[_rowspec(ROWB, c), _fullspec((8, c)), _fullspec((8, c))],
        out_shape=[_sds((lp, c), BF16), _sds((8, c), F32), _sds((8, c), F32)],
        scratch_shapes=[pltpu.VMEM((ROWB + 8, c), F32)],
        compiler_params=_cp(("arbitrary",)),
    )(dact, dact, xbc, xbc, xbc, cw, cb)


def _flash_bwd(q, k, v, datt, lse_row, delta_row, cos, sa, sb):
    lp = q.shape[0]
    nk = lp // ROWB

    def body(k_ref, v_ref, q_ref, do_ref, lse_ref, dl_ref, cos_ref, sa_ref, sb_ref, dq_ref, dk_ref, dv_ref,
             dq_acc, dk_acc, dv_acc):
        j = pl.program_id(1)

        @pl.when(j == 0)
        def _():
            dq_acc[...] = jnp.zeros_like(dq_acc)

        dk_acc[...] = jnp.zeros_like(dk_acc)
        dv_acc[...] = jnp.zeros_like(dv_acc)

        def tile(i, masked, key0=0, nkeys=ROWB):
            keys = pl.ds(key0, nkeys)
            kb, vb = k_ref[keys, :], v_ref[keys, :]
            off = pl.multiple_of(i * ROWB, ROWB)
            qb = q_ref[pl.ds(off, ROWB), :]
            dob = do_ref[pl.ds(off, ROWB), :]
            lse_r = lse_ref[0, :, pl.ds(off, ROWB)]
            dl_r = dl_ref[0, :, pl.ds(off, ROWB)]
            st = _dot(kb, qb, NT)
            if masked:
                krow = j * ROWB + key0 + lax.broadcasted_iota(jnp.int32, st.shape, 0)
                qrow = i * ROWB + lax.broadcasted_iota(jnp.int32, st.shape, 1)
                st = jnp.where(_att_ok(qrow, krow), st, NEG)
            pt = jnp.exp2(st - lse_r)
            dv_acc[keys, :] += _dot(pt.astype(BF16), dob)
            dpt = _dot(vb, dob, NT)
            dst = (pt * (dpt - dl_r)).astype(BF16)
            dk_acc[keys, :] += _dot(dst, qb)
            dq_acc[pl.ds(off, ROWB), :] += _dot(dst, kb, ((0,), (0,)))

        @pl.when(j == 0)
        def _():
            _pair_loop(0, nk, lambda i: tile(i, True, ROWB - META_KEYS, META_KEYS), (4, 2))

        odd = jnp.bitwise_and(nk - 1 - j, 3)
        for r in range(4):
            @pl.when((j > 0) & (odd == r))
            def _(r=r):
                tile(j, True)
                for d in range(r):
                    tile(j + 1 + d, False)

        @pl.when(j > 0)
        def _():
            _pair_loop(j + 1 + odd, nk, lambda i: tile(i, False), (16, 8, 4))

        dk_ref[...] = (dk_acc[...] * LN2).astype(BF16)
        dv_ref[...] = dv_acc[...].astype(BF16)
        dq = dq_acc[pl.ds(pl.multiple_of(j * ROWB, ROWB), ROWB), :] * ATT_SCALE
        half = HEADW // 2
        dq_ref[:, pl.ds(0, half)] = dq[:, :half].astype(BF16)
        dq_ref[:, pl.ds(half, half)] = _rope_t(dq[:, half:], cos_ref[...], sa_ref[...], sb_ref[...]).astype(BF16)

    stat = pl.BlockSpec((1, 1, lp), lambda h, j: (h, 0, 0))
    blk = pl.BlockSpec((ROWB, HEADW), lambda h, j: (j, h))
    tab = pl.BlockSpec((ROWB, HEADW // 2), lambda h, j: (j, 0))
    return pl.pallas_call(
        body, name="flash_bwd", grid=(ATT_HEADS, nk),
        in_specs=[blk, pl.BlockSpec((ROWB, V_HEAD), lambda h, j: (j, 2 * h)),
                  pl.BlockSpec((lp, HEADW), lambda h, j: (0, h)), pl.BlockSpec((lp, V_HEAD), lambda h, j: (0, h)),
                  stat, stat, tab, tab, tab],
        out_specs=[blk, blk, pl.BlockSpec((ROWB, V_HEAD), lambda h, j: (j, h))],
        out_shape=[_sds((lp, ATT_HEADS * HEADW), BF16), _sds((lp, ATT_HEADS * HEADW), BF16),
                   _sds((lp, ATT_HEADS * V_HEAD), BF16)],
        scratch_shapes=[pltpu.VMEM((lp, HEADW), F32), pltpu.VMEM((ROWB, HEADW), F32), pltpu.VMEM((ROWB, V_HEAD), F32)],
        compiler_params=_cp(("arbitrary", "arbitrary")),
    )(k, v, q, datt, lse_row, delta_row, cos, sa, sb)


def _qkv_bwd(dqp, dk, dv, cq, ckv, cos, sa, sb, gq, gkv, wq, wk, wv, dz, dxbc, ddtr):
    lp = cq.shape[0]
    qw = ATT_HEADS * HEADW

    def body(dqp_ref, dk_ref, dv_ref, cq_ref, ckv_ref, cos_ref, sa_ref, sb_ref, gq_ref, gkv_ref, wq_ref, wk_ref, wv_ref,
             dz_ref, dxbc_ref, ddtr_ref, dp_ref, dgq_ref, dgkv_ref):
        dcq, dgq = _rms_bwd(cq_ref[...], gq_ref[...], _dot(dqp_ref[...], wq_ref[...], NT))
        dp_ref[:, pl.ds(PC_Q, Q_LORA)] = dcq.astype(BF16)
        dkb = dk_ref[...]
        half = HEADW // 2
        dksum = sum(dkb[:, hh * HEADW + half:(hh + 1) * HEADW].astype(F32) for hh in range(ATT_HEADS))
        dp_ref[:, pl.ds(PC_KR, half)] = jnp.zeros((ROWB, half), BF16)
        dp_ref[:, pl.ds(PC_KR + half, half)] = _rope_t(dksum, cos_ref[...], sa_ref[...], sb_ref[...]).astype(BF16)
        dckvn = _dot(dkb, wk_ref[...], NT) + _dot(dv_ref[...], wv_ref[...], NT)
        dckv, dgkv = _rms_bwd(ckv_ref[...], gkv_ref[...], dckvn)
        dp_ref[:, pl.ds(PC_KV, KV_LORA)] = dckv.astype(BF16)
        dp_ref[:, pl.ds(PC_Z, SSM_WIDTH)] = dz_ref[...]
        dp_ref[:, pl.ds(PC_XBC, CONV_DIM)] = dxbc_ref[...]
        dp_ref[:, pl.ds(PC_DT, LANE)] = ddtr_ref[...].astype(BF16)
        _acc(dgq_ref, _colsum8(dgq))
        _acc(dgkv_ref, _colsum8(dgkv))

    return pl.pallas_call(
        body, name="qkv_bwd", grid=(lp // ROWB,),
        in_specs=[_rowspec(ROWB, qw), _rowspec(ROWB, qw), _rowspec(ROWB, ATT_HEADS * V_HEAD),
                  _rowspec(ROWB, Q_LORA), _rowspec(ROWB, KV_LORA)] + [_rowspec(ROWB, HEADW // 2)] * 3
        + [_fullspec((1, Q_LORA)), _fullspec((1, KV_LORA)), _fullspec((Q_LORA, qw)), _fullspec((KV_LORA, qw)),
           _fullspec((KV_LORA, ATT_HEADS * V_HEAD)), _rowspec(ROWB, SSM_WIDTH), _rowspec(ROWB, CONV_DIM),
           _rowspec(ROWB, LANE)],
        out_specs=[_rowspec(ROWB, PROJ_W), _fullspec((8, Q_LORA)), _fullspec((8, KV_LORA))],
        out_shape=[_sds((lp, PROJ_W), BF16), _sds((8, Q_LORA), F32), _sds((8, KV_LORA), F32)],
        compiler_params=_cp(("arbitrary",)),
    )(dqp, dk, dv, cq, ckv, cos, sa, sb, gq, gkv, wq, wk, wv, dz, dxbc, ddtr)


def _in_bwd(dproj, head, x, dh1, g, w_all, exchange=()):
    lp = dh1.shape[0]
    nsteps = lp // ROWB
    na = len(exchange)

    def body(dp_ref, head_ref, x_ref, dh1_ref, g_ref, w_ref, *rest):
        xin, (dx_ref, dhead_ref, dg_ref), xout, sems = rest[:na], rest[na:na + 3], rest[na + 3:2 * na + 3], rest[2 * na + 3:]
        step = pl.program_id(0)
        if na:
            x_start, x_finish = _exchange_ops(xin, xout, *sems)
            pl.when(step == 0)(x_start)
        dx, dg = _rms_bwd(_h_block(head_ref, x_ref), g_ref[...], _dot(dp_ref[...], w_ref[...], NT))
        dh = dh1_ref[...] + dx

        @pl.when(step == 0)
        def _():
            dhead_ref[...] = dh

        @pl.when(step > 0)
        def _():
            dx_ref[...] = dh

        _acc(dg_ref, _colsum8(dg))
        if na:
            pl.when(step == nsteps - 1)(x_finish)

    dx, dhead, dg8, *received = pl.pallas_call(
        body, name="in_bwd", grid=(nsteps,),
        in_specs=[_rowspec(ROWB, PROJ_W), _fullspec((ROWB, D_MODEL)), _xspec(), _rowspec(ROWB, D_MODEL),
                  _fullspec((1, D_MODEL)), _fullspec((D_MODEL, PROJ_W))] + [ANY] * na,
        out_specs=[_xspec(), _fullspec((ROWB, D_MODEL)), _fullspec((8, D_MODEL))] + [ANY] * na,
        out_shape=[_sds(x.shape, F32), _sds((ROWB, D_MODEL), F32), _sds((8, D_MODEL), F32)]
        + [_sds(e.shape, e.dtype) for e in exchange],
        scratch_shapes=_gather_scratch(na) if na else [],
        compiler_params=_cp(("arbitrary",)),
    )(dproj, head, x, dh1, g, w_all, *exchange)
    return dx, dhead, dg8, received


def _tile_of(n, cap=1024):
    return max(t for t in range(LANE, min(n, cap) + 1, LANE) if n % t == 0)


def _matmul_tn(name, a, b):
    rows, kd = a.shape
    nd = b.shape[1]
    tk, tn = _tile_of(kd), _tile_of(nd)
    rb = 3 * ROWB if rows % (3 * ROWB) == 0 else ROWB

    def body(a_ref, b_ref, o_ref):
        @pl.when(pl.program_id(2) == 0)
        def _():
            o_ref[...] = jnp.zeros_like(o_ref)

        o_ref[...] += _dot(a_ref[...], b_ref[...], ((0,), (0,)))

    return pl.pallas_call(
        body, name=name, grid=(kd // tk, nd // tn, rows // rb),
        in_specs=[pl.BlockSpec((rb, tk), lambda i, j, r: (r, i)), pl.BlockSpec((rb, tn), lambda i, j, r: (r, j))],
        out_specs=pl.BlockSpec((tk, tn), lambda i, j, r: (i, j)), out_shape=_sds((kd, nd), F32),
        compiler_params=_cp(("arbitrary", "arbitrary", "arbitrary")),
    )(a, b)


def _local_backward(head, x, f, exchange_late=False):
    p = f["p"]
    g = {}
    row = lambda v: _row1(v)
    s8 = lambda v: jnp.sum(v, axis=0)
    dh1, du, df, dgpre, dgpost = _mlp_bwd(f["dh2"], f["f"], f["h1"], f["u"], p["w_mlp_up"], p["w_mlp_down"],
                                          row(p["norm_mlp_pre"]), row(p["norm_mlp_post"]))
    g["norm_mlp_pre"], g["norm_mlp_post"] = s8(dgpre), s8(dgpost)
    g["w_mlp_up"] = _matmul_tn("dw_mlp_up", f["hn2"], du)
    g["w_mlp_down"] = _matmul_tn("dw_mlp_down", f["a"], df)
    dmix, datt, dy, dz, dgmp, delta, dgn, dd = _out_bwd(dh1, f["mix"], f["att"], f["y"], f["xact"], f["z"], f["dskip"],
                                                        row(p["ssm_norm"]), p["w_out"], row(p["norm_mix_post"]))
    g["norm_mix_post"] = s8(dgmp)
    g["w_out"] = jnp.concatenate([_matmul_tn("dw_out_att", f["att"], dmix), _matmul_tn("dw_out_ssm", f["ssm"], dmix)], axis=0)
    g["ssm_norm"] = s8(dgn)
    g["d_skip"] = s8(dd).reshape(SSM_HEADS, SSM_HEAD_DIM).sum(axis=1)
    dact, ddtr, da8, dbias8, received = _ssd_bwd(
        dy, f["xact"], f["dtr"], f["hprev"], f["dt_bias"], f["a_log"], f["dskip"], f["seq_rows"],
        exchange=[_to_chunks(n, g[n]).astype(BF16) for n in LATE] if exchange_late else ())
    g["a_log"], g["dt_bias"] = s8(da8)[:SSM_HEADS], s8(dbias8)[:SSM_HEADS]
    dxbc, dcw8, dcb8 = _conv_bwd(dact, f["xbc"], f["cw"], row(p["conv_b"]))
    g["conv_w"], g["conv_b"] = dcw8[:CONV_K], s8(dcb8)
    dqp, dkb, dv = _flash_bwd(f["q"], f["k"], f["v"], datt, f["lse"], delta, *f["rope"])
    dproj, dgq, dgkv = _qkv_bwd(dqp, dkb, dv, f["cq"], f["ckv"], *f["rope"], row(p["q_a_norm"]), row(p["kv_a_norm"]),
                                f["wq"], f["wk"], f["wv"], dz, dxbc, ddtr)
    g["q_a_norm"], g["kv_a_norm"] = s8(dgq), s8(dgkv)
    dwq = _matmul_tn("dw_q_up", f["cqn"], dqp).reshape(Q_LORA, ATT_HEADS, HEADW)
    g["w_q_up"] = dwq[:, :, :QK_NOPE + QK_ROPE].reshape(Q_LORA, -1)
    dwk = _matmul_tn("dw_k_up", f["ckvn"], dkb).reshape(KV_LORA, ATT_HEADS, HEADW)[:, :, :QK_NOPE]
    dwv = _matmul_tn("dw_v_up", f["ckvn"], dv).reshape(KV_LORA, ATT_HEADS, V_HEAD)
    g["w_kv_up"] = jnp.concatenate([dwk, dwv], axis=2).reshape(KV_LORA, -1)
    dwa = _matmul_tn("dw_in", f["hn"], dproj)
    g["w_in"] = jnp.concatenate([dwa[:, PC_Q:PC_KR], dwa[:, PC_KR + QK_NOPE:PC_KR + QK_NOPE + QK_ROPE],
                                 dwa[:, PC_Z:PC_DT + SSM_HEADS]], axis=1)
    dx, dhead, dgin, received_mid = _in_bwd(
        dproj, head, x, dh1, row(p["norm_mix_pre"]), f["w_all"],
        exchange=[_to_chunks(n, g[n]).astype(BF16) for n in MID] if exchange_late else ())
    g["norm_mix_pre"] = s8(dgin)
    g["meta_tokens"] = dhead[PADF:]
    return dx, g, dict(zip(LATE + MID, list(received) + list(received_mid)))


BIG = {"w_in": ((D_MODEL, IN_WIDTH), 1), "w_q_up": ((Q_LORA, ATT_HEADS * (QK_NOPE + QK_ROPE)), 1),
       "w_kv_up": ((KV_LORA, ATT_HEADS * (QK_NOPE + V_HEAD)), 1), "w_out": ((2 * D_MODEL, D_MODEL), 0),
       "w_mlp_up": ((D_MODEL, D_FF), 1), "w_mlp_down": ((D_FF, D_MODEL), 0), "conv_w": ((CONV_K, CONV_DIM), 1),
       "meta_tokens": ((N_META, D_MODEL), 1)}
SMALL = {"norm_mix_pre": D_MODEL, "q_a_norm": Q_LORA, "kv_a_norm": KV_LORA, "conv_b": CONV_DIM, "dt_bias": SSM_HEADS,
         "a_log": SSM_HEADS, "d_skip": SSM_HEADS, "ssm_norm": SSM_WIDTH, "norm_mix_post": D_MODEL,
         "norm_mlp_pre": D_MODEL, "norm_mlp_post": D_MODEL}
WEIGHT_ORDER = ("meta_tokens", "norm_mix_pre", "w_in", "q_a_norm", "w_q_up", "kv_a_norm", "w_kv_up", "conv_w", "conv_b",
                "dt_bias", "a_log", "d_skip", "ssm_norm", "w_out", "norm_mix_post", "norm_mlp_pre", "w_mlp_up",
                "w_mlp_down", "norm_mlp_post")
ADAM_ROWS = 256


def _shard_shape(name):
    shape, ax = BIG[name]
    return tuple(d // N_DEV if a == ax else d for a, d in enumerate(shape))


SMALL_ROWS = -(-sum(SMALL.values()) // (LANE * 8)) * 8


def _pack(flats, rows):
    v = jnp.concatenate([f.reshape(-1) for f in flats])
    return jnp.pad(v, (0, rows * LANE - v.shape[0])).reshape(rows, LANE)


def _unpack(packed, shapes):
    v = packed.reshape(-1)
    out, o = [], 0
    for s in shapes:
        n = math.prod(s)
        out.append(v[o:o + n].reshape(s))
        o += n
    return out


def _to_chunks(name, full):
    shape, ax = BIG[name]
    if ax == 0:
        return full.reshape((N_DEV,) + _shard_shape(name))
    k, n = shape
    return full.reshape(k, N_DEV, n // N_DEV).transpose(1, 0, 2)


def _from_shards(name, shards):
    shape, ax = BIG[name]
    if ax == 0:
        return shards.reshape(shape)
    return shards.transpose(1, 0, 2).reshape(shape)


def _peer(k):
    x, y, c = lax.axis_index("x"), lax.axis_index("y"), lax.axis_index("c")
    px = 1 - x if k & 4 else x
    py = 1 - y if k & 2 else y
    pc = 1 - c if k & 1 else c
    return (px, py, pc), 4 * px + 2 * py + pc


def _gather_ops(x_refs, out_refs, send_sems, recv_sems, local_sems):
    na = len(x_refs)
    chips = (4, 2, 6)

    def copy(a, n, block, to, src=None):
        return pltpu.make_async_remote_copy(
            src_ref=out_refs[a].at[block] if src is None else src, dst_ref=out_refs[a].at[block],
            send_sem=send_sems.at[7 * a + n], recv_sem=recv_sems.at[7 * a + n], device_id=to, device_id_type=MESH)

    def mine():
        me = _peer(0)[1]
        return [pltpu.make_async_copy(x_refs[a], out_refs[a].at[me], local_sems.at[a]) for a in range(na)]

    def first():
        me, sibling = _peer(0)[1], _peer(1)[0]
        out = [copy(a, 0, me, sibling, src=x_refs[a]) for a in range(na)]
        return out + [copy(a, 1 + n, me, _peer(k)[0], src=x_refs[a]) for n, k in enumerate(chips) for a in range(na)]

    def passed():
        sibling = _peer(1)[0]
        return [copy(a, 4 + n, _peer(k)[1], sibling) for n, k in enumerate(chips) for a in range(na)]

    def start():
        for cp in mine() + first():
            cp.start()

    def forward():
        sibling = _peer(1)[0]
        fwd = passed()
        for n, k in enumerate(chips):
            for a in range(na):
                copy(a, 1 + n, _peer(k)[1], sibling).wait_recv()
                fwd[n * na + a].start()

    def finish():
        sibling = _peer(1)[0]
        for a in range(na):
            copy(a, 0, _peer(1)[1], sibling).wait_recv()
        for n, k in enumerate(chips):
            for a in range(na):
                copy(a, 4 + n, _peer(k | 1)[1], sibling).wait_recv()
        for cp in first() + passed():
            cp.wait_send()
        for cp in mine():
            cp.wait()

    return start, forward, finish


def _gather_scratch(na):
    return [pltpu.SemaphoreType.DMA((7 * na,)), pltpu.SemaphoreType.DMA((7 * na,)), pltpu.SemaphoreType.DMA((na,))]


def _all_gather(shards):
    na = len(shards)

    def body(*refs):
        for step in _gather_ops(refs[:na], refs[na:2 * na], *refs[2 * na:]):
            step()

    return pl.pallas_call(
        body, name="all_gather_weights", out_shape=[_sds((N_DEV,) + s.shape, s.dtype) for s in shards],
        in_specs=[ANY] * na, out_specs=[ANY] * na, scratch_shapes=_gather_scratch(na),
    )(*shards)


def _exchange(chunks, small):
    na = len(chunks) + 1

    def body(*refs):
        for step in _exchange_ops(refs[:na], refs[na:2 * na], *refs[2 * na:], whole=(na - 1,)):
            step()

    arrays = list(chunks) + [small]
    return pl.pallas_call(
        body, name="exchange_grads",
        out_shape=[_sds(c.shape, c.dtype) for c in chunks] + [_sds((N_DEV,) + small.shape, small.dtype)],
        in_specs=[ANY] * na, out_specs=[ANY] * na, scratch_shapes=_gather_scratch(na),
    )(*arrays)


def _exchange_ops(in_refs, out_refs, send_sems, recv_sems, local_sems, whole=()):
    na = len(in_refs)

    def src(a, idx):
        return in_refs[a] if a in whole else in_refs[a].at[idx]

    def own():
        me = _peer(0)[1]
        return [pltpu.make_async_copy(src(a, me), out_refs[a].at[me], local_sems.at[a]) for a in range(na)]

    def copy(a, k, sending):
        me = _peer(0)[1]
        to, idx = _peer(k)
        return pltpu.make_async_remote_copy(
            src_ref=src(a, idx if sending else me), dst_ref=out_refs[a].at[me if sending else idx],
            send_sem=send_sems.at[7 * a + k - 1], recv_sem=recv_sems.at[7 * a + k - 1],
            device_id=to, device_id_type=MESH)

    def sent():
        return [copy(a, k, True) for k in range(1, N_DEV) for a in range(na)]

    def start():
        for cp in own() + sent():
            cp.start()

    def finish():
        for k in range(1, N_DEV):
            for a in range(na):
                copy(a, k, False).wait_recv()
        for cp in sent():
            cp.wait_send()
        for cp in own():
            cp.wait()

    return start, finish


def _reduce_adamw(name, recv, w, m, v):
    rows, cols = w.shape
    blk = ADAM_ROWS if rows % ADAM_ROWS == 0 else rows
    c1 = 1.0 - ADAM_B1 ** ADAM_STEP
    c2 = 1.0 - ADAM_B2 ** ADAM_STEP

    def body(r_ref, w_ref, m_ref, v_ref, g_ref, d_ref, nm_ref, nv_ref):
        g = r_ref[0].astype(F32)
        for s in range(1, N_DEV):
            g = g + r_ref[s].astype(F32)
        g_ref[...] = g
        m_ = ADAM_B1 * m_ref[...] + (1.0 - ADAM_B1) * g
        v_ = ADAM_B2 * v_ref[...] + (1.0 - ADAM_B2) * (g * g)
        nm_ref[...] = m_
        nv_ref[...] = v_
        d_ref[...] = -ADAM_LR * ((m_ / c1) / (jnp.sqrt(v_ / c2) + ADAM_EPS) + ADAM_WD * w_ref[...])

    spec = _rowspec(blk, cols)
    return pl.pallas_call(
        body, name="reduce_adamw_" + name, grid=(rows // blk,),
        in_specs=[pl.BlockSpec((N_DEV, blk, cols), lambda i: (0, i, 0)), spec, spec, spec],
        out_specs=[spec] * 4, out_shape=[_sds((rows, cols), F32)] * 4,
        compiler_params=_cp(("arbitrary",)),
    )(recv, w, m, v)


def kernel(x, meta_tokens, norm_mix_pre, w_in, q_a_norm, w_q_up, kv_a_norm, w_kv_up, conv_w, conv_b, dt_bias, a_log, d_skip, ssm_norm, w_out, norm_mix_post, norm_mlp_pre, w_mlp_up, w_mlp_down, norm_mlp_post, loss_target, m_meta_tokens, m_norm_mix_pre, m_w_in, m_q_a_norm, m_w_q_up, m_kv_a_norm, m_w_kv_up, m_conv_w, m_conv_b, m_dt_bias, m_a_log, m_d_skip, m_ssm_norm, m_w_out, m_norm_mix_post, m_norm_mlp_pre, m_w_mlp_up, m_w_mlp_down, m_norm_mlp_post, v_meta_tokens, v_norm_mix_pre, v_w_in, v_q_a_norm, v_w_q_up, v_kv_a_norm, v_w_kv_up, v_conv_w, v_conv_b, v_dt_bias, v_a_log, v_d_skip, v_ssm_norm, v_w_out, v_norm_mix_post, v_norm_mlp_pre, v_w_mlp_up, v_w_mlp_down, v_norm_mlp_post):
    w = dict(meta_tokens=meta_tokens, norm_mix_pre=norm_mix_pre, w_in=w_in, q_a_norm=q_a_norm, w_q_up=w_q_up,
             kv_a_norm=kv_a_norm, w_kv_up=w_kv_up, conv_w=conv_w, conv_b=conv_b, dt_bias=dt_bias, a_log=a_log,
             d_skip=d_skip, ssm_norm=ssm_norm, w_out=w_out, norm_mix_post=norm_mix_post, norm_mlp_pre=norm_mlp_pre,
             w_mlp_up=w_mlp_up, w_mlp_down=w_mlp_down, norm_mlp_post=norm_mlp_post)
    m = dict(meta_tokens=m_meta_tokens, norm_mix_pre=m_norm_mix_pre, w_in=m_w_in, q_a_norm=m_q_a_norm, w_q_up=m_w_q_up,
             kv_a_norm=m_kv_a_norm, w_kv_up=m_w_kv_up, conv_w=m_conv_w, conv_b=m_conv_b, dt_bias=m_dt_bias,
             a_log=m_a_log, d_skip=m_d_skip, ssm_norm=m_ssm_norm, w_out=m_w_out, norm_mix_post=m_norm_mix_post,
             norm_mlp_pre=m_norm_mlp_pre, w_mlp_up=m_w_mlp_up, w_mlp_down=m_w_mlp_down, norm_mlp_post=m_norm_mlp_post)
    v = dict(meta_tokens=v_meta_tokens, norm_mix_pre=v_norm_mix_pre, w_in=v_w_in, q_a_norm=v_q_a_norm, w_q_up=v_w_q_up,
             kv_a_norm=v_kv_a_norm, w_kv_up=v_w_kv_up, conv_w=v_conv_w, conv_b=v_conv_b, dt_bias=v_dt_bias,
             a_log=v_a_log, d_skip=v_d_skip, ssm_norm=v_ssm_norm, w_out=v_w_out, norm_mix_post=v_norm_mix_post,
             norm_mlp_pre=v_norm_mlp_pre, w_mlp_up=v_w_mlp_up, w_mlp_down=v_w_mlp_down, norm_mlp_post=v_norm_mlp_post)
    big_names = [n for n in WEIGHT_ORDER if n in BIG]
    small_names = [n for n in WEIGHT_ORDER if n in SMALL]
    shard = lambda d, n: d[n].reshape(_shard_shape(n))

    f32_names = ("conv_w", "meta_tokens")
    early = [n for n in big_names if n not in LATE]
    gathered = _all_gather([shard(w, n).astype(F32 if n in f32_names else BF16) for n in early])
    p = {n: w[n].reshape(-1) for n in small_names}
    p.update({n: _from_shards(n, s) for n, s in zip(early, gathered)})
    head = jnp.concatenate([jnp.zeros((PADF, D_MODEL), F32), p["meta_tokens"]], axis=0)
    f = _local_forward(head, x[0], loss_target[0], p, late={n: shard(w, n).astype(BF16) for n in LATE})
    dx, g, recv_of = _local_backward(head, x[0], f, exchange_late=True)
    grad_x = dx[None]
    loss = lax.psum(f["loss"], ("x", "y", "c"))

    rest = [n for n in big_names if n not in LATE + MID]
    small = _pack([g[n] for n in small_names], SMALL_ROWS)
    *recv_rest, recv_small = _exchange([_to_chunks(n, g[n]) for n in rest], small)
    recv_of.update(zip(rest, recv_rest))

    outs = {}
    kinds = ("grad", "delta", "new_m", "new_v")
    for n, recv in ((n, recv_of[n]) for n in big_names):
        for kind, arr in zip(kinds, _reduce_adamw(n, recv, shard(w, n), shard(m, n), shard(v, n))):
            outs[kind, n] = arr.reshape(w[n].shape)
    packed = [_pack([d[n] for n in small_names], SMALL_ROWS) for d in (w, m, v)]
    for kind, arr in zip(kinds, _reduce_adamw("small", recv_small, *packed)):
        for n, val in zip(small_names, _unpack(arr, [(SMALL[n],) for n in small_names])):
            outs[kind, n] = val.reshape(w[n].shape)
    return (loss, grad_x) + tuple(outs[kind, n] for kind in ("grad", "delta", "new_m", "new_v") for n in WEIGHT_ORDER)
```

```python
import math

import jax
import jax.numpy as jnp
import numpy as np
from jax import lax
from jax.experimental import pallas as pl
from jax.experimental.pallas import tpu as pltpu

F32 = jnp.float32
BF16 = jnp.bfloat16

D_MODEL = 1024
N_META = 16
EPS = 1e-6
ATT_HEADS = 8
Q_LORA = 384
KV_LORA = 256
QK_NOPE = 128
QK_ROPE = 64
V_HEAD = 128
ROPE_THETA = 10000.0
SSM_HEADS = 16
SSM_HEAD_DIM = 64
SSM_WIDTH = 1024
SSM_GROUPS = 2
SSM_STATE = 128
CONV_K = 4
CHUNK = 128
CONV_DIM = 1536
D_FF = 4096
IN_SPLITS = (Q_LORA, KV_LORA, QK_ROPE, SSM_WIDTH, CONV_DIM, SSM_HEADS)
IN_WIDTH = sum(IN_SPLITS)
ADAM_LR, ADAM_B1, ADAM_B2, ADAM_EPS, ADAM_WD, ADAM_STEP = 0.001, 0.9, 0.999, 1e-08, 0.01, 10

LANE = 128
ROWB = 512
PADF = ROWB - N_META
HEADW = 256
PC_Q, PC_KV, PC_KR, PC_Z, PC_XBC, PC_DT, PROJ_W = 0, 384, 640, 896, 1920, 3456, 3584
NEG = -1e30
N_DEV = 8
VMEM_LIMIT = 56 * 1024 * 1024
MESH = pl.DeviceIdType.MESH


def _cp(sem, vmem=VMEM_LIMIT, **kw):
    return pltpu.CompilerParams(dimension_semantics=sem, vmem_limit_bytes=vmem, **kw)


def _dot(a, b, dims=((1,), (0,))):
    return lax.dot_general(a, b, (dims, ((), ())), preferred_element_type=F32)


def _bdot(a, b, dims=((1,), (0,))):
    return _dot(a.astype(BF16), b.astype(BF16), dims)


NT = ((1,), (1,))


def _rms_fwd(x, w):
    r = lax.rsqrt(jnp.mean(x * x, axis=-1, keepdims=True) + EPS)
    return (x * r) * w


def _rms_bwd(x, w, dy):
    r = lax.rsqrt(jnp.mean(x * x, axis=-1, keepdims=True) + EPS)
    xh = x * r
    g = dy * w
    dx = r * (g - xh * jnp.mean(g * xh, axis=-1, keepdims=True))
    return dx, dy * xh


def _sigmoid(x):
    return 0.5 * jnp.tanh(0.5 * x) + 0.5


def _colsum8(x):
    t, c = x.shape
    return jnp.sum(x.reshape(t // 8, 8, c), axis=0)


def _rowspec(t, c, cb=0):
    return pl.BlockSpec((t, c), lambda i: (i, cb))


def _fullspec(shape):
    n = len(shape)
    return pl.BlockSpec(shape, lambda i: (0,) * n)


def _sds(shape, dt):
    return jax.ShapeDtypeStruct(shape, dt)


def _acc(ref, val):
    @pl.when(pl.program_id(0) == 0)
    def _():
        ref[...] = jnp.zeros_like(ref)

    ref[...] += val


def _xspec():
    return pl.BlockSpec((ROWB, D_MODEL), lambda i: (jnp.maximum(i - 1, 0), 0))


def _h_block(head_ref, x_ref):
    return jnp.where(pl.program_id(0) == 0, head_ref[...], x_ref[...])


def _norm_in_proj(head, x, g, w_all):
    lp = head.shape[0] + x.shape[0]

    def body(head_ref, x_ref, g_ref, w_ref, hn_ref, cq_ref, ckv_ref, kr_ref, z_ref, xbc_ref, dt_ref):
        hn = _rms_fwd(_h_block(head_ref, x_ref), g_ref[...]).astype(BF16)
        hn_ref[...] = hn
        p = _dot(hn, w_ref[...])
        cq_ref[...] = p[:, PC_Q:PC_KV]
        ckv_ref[...] = p[:, PC_KV:PC_KR]
        kr_ref[...] = p[:, PC_KR:PC_Z]
        z_ref[...] = p[:, PC_Z:PC_XBC]
        xbc_ref[...] = p[:, PC_XBC:PC_DT]
        dt_ref[...] = p[:, PC_DT:PROJ_W]

    widths = (Q_LORA, KV_LORA, HEADW, SSM_WIDTH, CONV_DIM, LANE)
    return pl.pallas_call(
        body, name="norm_in_proj", grid=(lp // ROWB,),
        in_specs=[_fullspec((ROWB, D_MODEL)), _xspec(), _fullspec((1, D_MODEL)), _fullspec((D_MODEL, PROJ_W))],
        out_specs=[_rowspec(ROWB, D_MODEL)] + [_rowspec(ROWB, w) for w in widths],
        out_shape=[_sds((lp, D_MODEL), BF16)] + [_sds((lp, w), F32) for w in widths],
        compiler_params=_cp(("arbitrary",)),
    )(head, x, g, w_all)


def _rope(x, cos, sa, sb):
    w = x.shape[1]
    return x * cos + pltpu.roll(x, w - 32, 1) * sa + pltpu.roll(x, 32, 1) * sb


def _rope_t(dy, cos, sa, sb):
    w = dy.shape[1]
    return dy * cos + pltpu.roll(dy * sa, 32, 1) + pltpu.roll(dy * sb, w - 32, 1)


def _qkv(cq, ckv, kr, cos, sa, sb, gq, gkv, wq, wk, wv):
    lp = cq.shape[0]
    qw = ATT_HEADS * HEADW
    half = HEADW // 2
    assert half == QK_NOPE == V_HEAD == LANE

    def body(cq_ref, ckv_ref, kr_ref, cos_ref, sa_ref, sb_ref, gq_ref, gkv_ref, wq_ref, wk_ref, wv_ref,
             q_ref, k_ref, v_ref, cqn_ref, ckvn_ref):
        tabs = [cos_ref[...], sa_ref[...], sb_ref[...]]
        cqn = _rms_fwd(cq_ref[...], gq_ref[...]).astype(BF16)
        ckvn = _rms_fwd(ckv_ref[...], gkv_ref[...]).astype(BF16)
        cqn_ref[...] = cqn
        ckvn_ref[...] = ckvn
        q = _dot(cqn, wq_ref[...])
        kn = _dot(ckvn, wk_ref[...])
        vv = _dot(ckvn, wv_ref[...])
        krope = _rope(kr_ref[:, pl.ds(half, half)], *tabs).astype(BF16)
        ones = jnp.ones((ROWB, half), BF16)
        for hh in range(ATT_HEADS):
            lo, hi, src = pl.ds(hh * HEADW, half), pl.ds(hh * HEADW + half, half), slice(hh * half, (hh + 1) * half)
            q_ref[:, lo] = (q[:, hh * HEADW:hh * HEADW + half] * Q_PRESCALE).astype(BF16)
            q_ref[:, hi] = (_rope(q[:, hh * HEADW + half:(hh + 1) * HEADW], *tabs) * Q_PRESCALE).astype(BF16)
            k_ref[:, lo] = kn[:, src].astype(BF16)
            k_ref[:, hi] = krope
            v_ref[:, lo] = vv[:, src].astype(BF16)
            v_ref[:, hi] = ones

    return pl.pallas_call(
        body, name="qkv", grid=(lp // ROWB,),
        in_specs=[_rowspec(ROWB, Q_LORA), _rowspec(ROWB, KV_LORA), _rowspec(ROWB, HEADW)]
        + [_rowspec(ROWB, half)] * 3
        + [_fullspec((1, Q_LORA)), _fullspec((1, KV_LORA)), _fullspec((Q_LORA, qw)),
           _fullspec((KV_LORA, ATT_HEADS * QK_NOPE)), _fullspec((KV_LORA, ATT_HEADS * V_HEAD))],
        out_specs=[_rowspec(ROWB, qw), _rowspec(ROWB, qw), _rowspec(ROWB, qw),
                   _rowspec(ROWB, Q_LORA), _rowspec(ROWB, KV_LORA)],
        out_shape=[_sds((lp, qw), BF16), _sds((lp, qw), BF16), _sds((lp, qw), BF16),
                   _sds((lp, Q_LORA), BF16), _sds((lp, KV_LORA), BF16)],
        compiler_params=_cp(("arbitrary",)),
    )(cq, ckv, kr, cos, sa, sb, gq, gkv, wq, wk, wv)


ATT_SCALE = (QK_NOPE + QK_ROPE) ** -0.5
LOG2E = 1.4426950408889634
LN2 = 0.6931471805599453
Q_PRESCALE = ATT_SCALE * LOG2E
KVB = 512
META_KEYS = LANE
assert N_META <= META_KEYS


def _att_ok(qrow, krow):
    return (krow <= qrow) & ((krow >= PADF) | (qrow < PADF))


def _lanes(x, n):
    return x if n == 1 else jnp.concatenate([x] * n, axis=1)


def _pair_loop(lo, hi, tile, unrolls=(2,)):
    for u in tuple(unrolls) + (1,):
        n = jnp.maximum(hi - lo, 0)
        trips = n // u

        def many(t, c, u=u, lo=lo):
            for d in range(u):
                tile(lo + u * t + d)
            return c

        lax.fori_loop(0, trips, many, 0)
        lo = lo + trips * u


def _flash_fwd(q, k, v):
    lp = q.shape[0]
    nq = lp // ROWB

    def body(q_ref, k_ref, v_ref, o_ref, lse_ref, acc, m_s):
        i = pl.program_id(1)
        qb = q_ref[...]
        m_s[...] = jnp.full_like(m_s, NEG)
        acc[...] = jnp.zeros_like(acc)

        def tile(j, masked, off=None, nkeys=KVB):
            off = pl.multiple_of(j * KVB, KVB) if off is None else off
            kb = k_ref[pl.ds(off, nkeys), :]
            vb = v_ref[pl.ds(off, nkeys), :]
            s = _dot(qb, kb, NT)
            if masked:
                qrow = i * ROWB + lax.broadcasted_iota(jnp.int32, s.shape, 0)
                krow = off + lax.broadcasted_iota(jnp.int32, s.shape, 1)
                s = jnp.where(_att_ok(qrow, krow), s, NEG)
            m_prev = m_s[...]
            m_new = jnp.maximum(m_prev, jnp.max(s, axis=1, keepdims=True))
            alpha = jnp.exp2(m_prev - m_new)
            p = jnp.exp2(s - _lanes(m_new, nkeys // LANE))
            acc[...] = _lanes(alpha, 2) * acc[...] + _dot(p.astype(BF16), vb)
            m_s[...] = m_new

        def first_tile():
            tile(0, True, off=ROWB - META_KEYS, nkeys=META_KEYS)

        @pl.when(i == 0)
        def _():
            first_tile()

        odd = jnp.bitwise_and(jnp.maximum(i - 1, 0), 7)
        for r in range(8):
            @pl.when((i > 0) & (odd == r))
            def _(r=r):
                first_tile()
                tile(i, True)
                for d in range(r):
                    tile(1 + d, False)

        _pair_loop(1 + odd, i, lambda j: tile(j, False), (16, 8))
        l = acc[:, V_HEAD:]
        o_ref[...] = (acc[:, :V_HEAD] / l).astype(BF16)
        lse_ref[0] = (m_s[...] + jnp.log2(l)).T[0:1, :]

    return pl.pallas_call(
        body, name="flash_fwd", grid=(ATT_HEADS, nq),
        in_specs=[pl.BlockSpec((ROWB, HEADW), lambda h, i: (i, h)),
                  pl.BlockSpec((lp, HEADW), lambda h, i: (0, h)),
                  pl.BlockSpec((lp, HEADW), lambda h, i: (0, h))],
        out_specs=[pl.BlockSpec((ROWB, V_HEAD), lambda h, i: (i, h)),
                   pl.BlockSpec((1, 1, ROWB), lambda h, i: (h, 0, i))],
        out_shape=[_sds((lp, ATT_HEADS * V_HEAD), BF16), _sds((ATT_HEADS, 1, lp), F32)],
        scratch_shapes=[pltpu.VMEM((ROWB, HEADW), F32), pltpu.VMEM((ROWB, LANE), F32)],
        compiler_params=_cp(("arbitrary", "arbitrary")),
    )(q, k, v)


def _silu(x):
    return x * _sigmoid(x)


CONV_ROWS = 64


def _window_rows(win, n, offsets):
    return [win[o:o + n] if o % 8 == 0 else pltpu.roll(win, n + 8 - o, 0)[0:n] for o in offsets]


def _shifted_rows(ref, start, n, cols, offsets):
    return _window_rows(ref[pl.ds(start, n + 8), cols], n, offsets)


def _block_window(x_ref, halo, r0, n, cols):
    if r0 == 0:
        return jnp.concatenate([halo, x_ref[pl.ds(0, n), cols]], axis=0)
    return x_ref[pl.ds(r0 - 8, n + 8), cols]


def _conv_fwd(xbc, cw, cb):
    lp, c = xbc.shape
    t8 = ROWB // 8

    def body(x_ref, prev_ref, w_ref, b_ref, o_ref):
        i = pl.program_id(0)

        def strip(s, carry):
            cs = pl.ds(pl.multiple_of(s * LANE, LANE), LANE)
            w, b = w_ref[:, cs], b_ref[:, cs]
            halo = jnp.where(i > 0, prev_ref[:, cs], 0.0)
            for r0 in range(0, ROWB, CONV_ROWS):
                win = _block_window(x_ref, halo, r0, CONV_ROWS, cs)
                taps = _window_rows(win, CONV_ROWS, [8 - (CONV_K - 1) + kk for kk in range(CONV_K)])
                pre = b + sum(w[kk:kk + 1, :] * taps[kk] for kk in range(CONV_K))
                o_ref[pl.ds(r0, CONV_ROWS), cs] = _silu(pre)
            return carry

        lax.fori_loop(0, c // LANE, strip, 0)

    return pl.pallas_call(
        body, name="conv_fwd", grid=(lp // ROWB,),
        in_specs=[_rowspec(ROWB, c), pl.BlockSpec((8, c), lambda i: (jnp.maximum(i * t8 - 1, 0), 0)),
                  _fullspec((8, c)), _fullspec((1, c))],
        out_specs=_rowspec(ROWB, c), out_shape=_sds((lp, c), F32),
        compiler_params=_cp(("arbitrary",)),
    )(xbc, xbc, cw, cb)


def _expand_mat():
    r = np.arange(LANE)[:, None]
    c = np.arange(SSM_WIDTH)[None, :]
    return jnp.asarray((c // SSM_HEAD_DIM == r).astype(np.float32))


def _tri_mat():
    i = np.arange(CHUNK)
    return jnp.asarray((i[:, None] >= i[None, :]).astype(np.float32))


def _x3(m, axis):
    return jnp.concatenate([m.astype(BF16)] * 3, axis=axis)


def _split3(x):
    hi = x.astype(BF16)
    r = x - hi.astype(F32)
    mid = r.astype(BF16)
    return hi, mid, (r - mid.astype(F32)).astype(BF16)


def _dot01_r(x, m3):
    return _dot(jnp.concatenate(_split3(x), axis=1), m3)


def _dot01_l(m3, x):
    return _dot(m3, jnp.concatenate(_split3(x), axis=0))


def _ssd_prep(dt_raw, bias_ref, alog_ref, tri3, c, seq_rows):
    rows = c * CHUNK + lax.broadcasted_iota(jnp.int32, (CHUNK, LANE), 0)
    lanes = lax.broadcasted_iota(jnp.int32, (CHUNK, LANE), 1)
    valid = (rows >= PADF) & (rows < PADF + seq_rows) & (lanes < SSM_HEADS)
    dtr = dt_raw + bias_ref[...]
    sp = jnp.maximum(dtr, 0.0) + jnp.log(1.0 + jnp.exp(-jnp.abs(dtr)))
    dt = jnp.where(valid, sp, 0.0)
    a = -jnp.exp(alog_ref[...])
    acol = _dot01_l(tri3, dt * a)
    return dt, a, acol, valid, dtr


def _row16(v):
    return jnp.broadcast_to(v, (16, v.shape[1]))


def _ssd_fwd(xbc_act, dtr, dt_bias, a_log, seq_rows, gather=()):
    lp = xbc_act.shape[0]
    nc = lp // CHUNK
    cps = ROWB // CHUNK
    nsteps = nc // cps
    gw = SSM_WIDTH // SSM_GROUPS
    hpg = SSM_HEADS // SSM_GROUPS

    na = len(gather)

    def body(x_ref, b_ref, c_ref, dtr_ref, bias_ref, alog_ref, tri_ref, tri3_ref, ex3_ref, *rest):
        gin, (y_ref, hp_ref), gout, h_s, sems = rest[:na], rest[na:na + 2], rest[na + 2:2 * na + 2], rest[2 * na + 2], rest[2 * na + 3:]
        step = pl.program_id(0)

        @pl.when(step == 0)
        def _():
            h_s[...] = jnp.zeros_like(h_s)

        if na:
            g_start, g_forward, g_finish = _gather_ops(gin, gout, *sems)
            pl.when(step == 0)(g_start)
            pl.when(step == nsteps // 2)(g_forward)

        ex3 = ex3_ref[...]
        causal = tri_ref[...] > 0.5
        for cc in range(cps):
            rows = pl.ds(cc * CHUNK, CHUNK)
            dt, a, acol, _, _ = _ssd_prep(dtr_ref[rows, :], bias_ref, alog_ref, tri3_ref[...], step * cps + cc, seq_rows)
            arow = acol.T
            dtrow = dt.T
            alast = acol[CHUNK - 1:CHUNK, :]
            e_all = _dot01_r(jnp.exp(acol), ex3)
            wx_all = _dot01_r(jnp.exp(alast - acol) * dt, ex3)
            dec_all = _dot01_r(_row16(jnp.exp(alast)), ex3)[0:1, :]
            hp_ref[cc] = h_s[...]
            for g in range(SSM_GROUPS):
                gs = slice(g * gw, (g + 1) * gw)
                bg = b_ref[rows, g * SSM_STATE:(g + 1) * SSM_STATE]
                cg = c_ref[rows, g * SSM_STATE:(g + 1) * SSM_STATE].astype(BF16)
                xg = x_ref[rows, gs]
                hg = h_s[:, gs]
                gm = _bdot(cg, bg, NT)
                y_off = _bdot(cg, hg) * e_all[:, gs]
                for r in range(hpg):
                    hd = g * hpg + r
                    seg = acol[:, hd:hd + 1] - arow[hd:hd + 1, :]
                    lm = jnp.where(causal, jnp.exp(jnp.where(causal, seg, 0.0)), 0.0)
                    w = gm * lm * dtrow[hd:hd + 1, :]
                    cs = slice(r * SSM_HEAD_DIM, (r + 1) * SSM_HEAD_DIM)
                    y_ref[rows, pl.ds(hd * SSM_HEAD_DIM, SSM_HEAD_DIM)] = _bdot(w, xg[:, cs]) + y_off[:, cs]
                st = _bdot(bg.T, xg * wx_all[:, gs])
                h_s[:, gs] = hg * dec_all[:, gs] + st

        if na:
            pl.when(step == nsteps - 1)(g_finish)

    xs_spec = pl.BlockSpec((ROWB, SSM_WIDTH), lambda c: (c, 0))
    b_spec = pl.BlockSpec((ROWB, 2 * SSM_STATE), lambda c: (c, SSM_WIDTH // (2 * SSM_STATE)))
    c_spec = pl.BlockSpec((ROWB, 2 * SSM_STATE), lambda c: (c, SSM_WIDTH // (2 * SSM_STATE) + 1))
    y, hprev, *gathered = pl.pallas_call(
        body, name="ssd_fwd", grid=(nsteps,),
        in_specs=[xs_spec, b_spec, c_spec, pl.BlockSpec((ROWB, LANE), lambda c: (c, 0)),
                  _fullspec((1, LANE)), _fullspec((1, LANE)), _fullspec((CHUNK, CHUNK)), _fullspec((CHUNK, 3 * CHUNK)),
                  _fullspec((3 * LANE, SSM_WIDTH))] + [ANY] * na,
        out_specs=[xs_spec, pl.BlockSpec((cps, SSM_STATE, SSM_WIDTH), lambda c: (c, 0, 0))] + [ANY] * na,
        out_shape=[_sds((lp, SSM_WIDTH), F32), _sds((nc, SSM_STATE, SSM_WIDTH), F32)]
        + [_sds((N_DEV,) + s.shape, s.dtype) for s in gather],
        scratch_shapes=[pltpu.VMEM((SSM_STATE, SSM_WIDTH), F32)] + (_gather_scratch(na) if na else []),
        compiler_params=_cp(("arbitrary",)),
    )(xbc_act, xbc_act, xbc_act, dtr, dt_bias, a_log, _tri_mat(), _x3(_tri_mat(), 1), _x3(_expand_mat(), 0), *gather)
    return y, hprev, gathered


def _group_mean(x):
    gw = SSM_WIDTH // SSM_GROUPS
    parts = [jnp.broadcast_to(jnp.mean(x[:, g * gw:(g + 1) * gw], axis=-1, keepdims=True), (x.shape[0], gw))
             for g in range(SSM_GROUPS)]
    return jnp.concatenate(parts, axis=1)


def _out_proj(att, y, xbc_act, z, dskip, gnorm, head, x, w_out, g_post):
    lp = att.shape[0]

    def body(a_ref, y_ref, xs_ref, z_ref, d_ref, gn_ref, head_ref, x_ref, w_ref, g_ref, ssm_ref, mix_ref, h1_ref):
        gt = (y_ref[...] + d_ref[...] * xs_ref[...]) * _silu(z_ref[...])
        r = lax.rsqrt(_group_mean(gt * gt) + EPS)
        ssm = ((gt * r) * gn_ref[...]).astype(BF16)
        ssm_ref[...] = ssm
        mix = _dot(a_ref[...], w_ref[pl.ds(0, 1024), :]) + _dot(ssm, w_ref[pl.ds(1024, 1024), :])
        mix_ref[...] = mix
        h1_ref[...] = _h_block(head_ref, x_ref) + _rms_fwd(mix, g_ref[...])

    return pl.pallas_call(
        body, name="out_proj", grid=(lp // ROWB,),
        in_specs=[_rowspec(ROWB, 1024)] * 4 + [_fullspec((1, SSM_WIDTH))] * 2
        + [_fullspec((ROWB, D_MODEL)), _xspec(), _fullspec((2048, D_MODEL)), _fullspec((1, D_MODEL))],
        out_specs=[_rowspec(ROWB, SSM_WIDTH)] + [_rowspec(ROWB, D_MODEL)] * 2,
        out_shape=[_sds((lp, SSM_WIDTH), BF16)] + [_sds((lp, D_MODEL), F32)] * 2,
        compiler_params=_cp(("arbitrary",)),
    )(att, y, xbc_act, z, dskip, gnorm, head, x, w_out, g_post)


def _resident(w_hbm, w_vmem, sem):
    @pl.when(pl.program_id(0) == 0)
    def _():
        cp = pltpu.make_async_copy(w_hbm, w_vmem, sem)
        cp.start()
        cp.wait()


ANY = pl.BlockSpec(memory_space=pl.ANY)


def _mlp_fwd(h1, tgt, w_up, w_down, g_pre, g_post, seq_rows):
    lp = h1.shape[0]

    def body(h1_ref, t_ref, wu_hbm, wd_hbm, gpre_ref, gpost_ref, hn2_ref, u_ref, a_ref, f_ref, dh2_ref, loss_ref,
             wu, wd, sems):
        _resident(wu_hbm, wu, sems.at[0])
        _resident(wd_hbm, wd, sems.at[1])
        i = pl.program_id(0)
        h1_ = h1_ref[...]
        hn2 = _rms_fwd(h1_, gpre_ref[...]).astype(BF16)
        hn2_ref[...] = hn2
        u = jnp.maximum(_dot(hn2, wu[...]), 0.0)
        u_ref[...] = u.astype(BF16)
        a = (u * u).astype(BF16)
        a_ref[...] = a
        f = _dot(a, wd[...])
        f_ref[...] = f
        h2 = h1_ + _rms_fwd(f, gpost_ref[...])
        rows = i * ROWB + lax.broadcasted_iota(jnp.int32, (ROWB, 1), 0)
        real = (rows >= PADF + N_META) & (rows < PADF + seq_rows)
        err = jnp.where(real, h2 - t_ref[...], 0.0)
        dh2_ref[...] = err * (1.0 / D_MODEL)
        _acc(loss_ref, _colsum8(err * err))

    return pl.pallas_call(
        body, name="mlp_fwd", grid=(lp // ROWB,),
        in_specs=[_rowspec(ROWB, D_MODEL), _xspec()] + [ANY, ANY] + [_fullspec((1, D_MODEL))] * 2,
        out_specs=[_rowspec(ROWB, D_MODEL), _rowspec(ROWB, D_FF), _rowspec(ROWB, D_FF)] + [_rowspec(ROWB, D_MODEL)] * 2
        + [_fullspec((8, D_MODEL))],
        out_shape=[_sds((lp, D_MODEL), BF16), _sds((lp, D_FF), BF16), _sds((lp, D_FF), BF16), _sds((lp, D_MODEL), F32),
                   _sds((lp, D_MODEL), F32), _sds((8, D_MODEL), F32)],
        scratch_shapes=[pltpu.VMEM((D_MODEL, D_FF), BF16), pltpu.VMEM((D_FF, D_MODEL), BF16), pltpu.SemaphoreType.DMA((2,))],
        compiler_params=_cp(("arbitrary",)),
    )(h1, tgt, w_up, w_down, g_pre, g_post)


def _pad_cols(w, width):
    return jnp.pad(w, ((0, 0), (0, width - w.shape[1])))


def _layout_weights(w_in, w_q_up, w_kv_up):
    o = np.cumsum((0,) + IN_SPLITS)
    pieces = [w_in[:, o[k]:o[k + 1]] for k in range(6)]
    kr = jnp.pad(pieces[2], ((0, 0), (QK_NOPE, HEADW - QK_NOPE - QK_ROPE)))
    w_all = jnp.concatenate([pieces[0], pieces[1], kr, pieces[3], pieces[4], _pad_cols(pieces[5], LANE)], axis=1)
    wq = jnp.pad(w_q_up.reshape(Q_LORA, ATT_HEADS, QK_NOPE + QK_ROPE), ((0, 0), (0, 0), (0, HEADW - QK_NOPE - QK_ROPE)))
    wkv = w_kv_up.reshape(KV_LORA, ATT_HEADS, QK_NOPE + V_HEAD)
    wk = jnp.pad(wkv[:, :, :QK_NOPE], ((0, 0), (0, 0), (0, HEADW - QK_NOPE)))
    wv = wkv[:, :, QK_NOPE:]
    return (w_all, wq.reshape(Q_LORA, -1), wk.reshape(KV_LORA, -1), wv.reshape(KV_LORA, -1),
            wkv[:, :, :QK_NOPE].reshape(KV_LORA, -1))


def _rope_tables(lp):
    pos = jnp.maximum(jnp.arange(lp, dtype=jnp.int32) - PADF, 0).astype(F32)
    inv_freq = ROPE_THETA ** (-jnp.arange(0, QK_ROPE, 2, dtype=F32) / QK_ROPE)
    ang = pos[:, None] * inv_freq[None, :]
    cos, sin = jnp.cos(ang), jnp.sin(ang)
    z32, z64 = jnp.zeros((lp, 32), F32), jnp.zeros((lp, 64), F32)
    cos_t = jnp.concatenate([cos, cos, jnp.ones((lp, 64), F32)], axis=1)
    sa = jnp.concatenate([-sin, z32, z64], axis=1)
    sb = jnp.concatenate([z32, sin, z64], axis=1)
    return cos_t, sa, sb


def _row1(v, width=None):
    v = v.reshape(1, -1).astype(F32)
    return v if width is None else _pad_cols(v, width)


LATE = ("w_out", "w_mlp_up", "w_mlp_down")
MID = ("w_in", "w_q_up", "w_kv_up", "conv_w")


def _local_forward(head, x, tgt, p, late=None):
    assert head.shape[0] == ROWB and x.shape[0] % ROWB == 0
    lp = ROWB + x.shape[0]
    seq_rows = N_META + x.shape[0]
    f = {"seq_rows": seq_rows}
    w_all, wq, wk, wv, wkn = _layout_weights(p["w_in"], p["w_q_up"], p["w_kv_up"])
    f.update(w_all=w_all, wq=wq, wk=wk, wv=wv)
    f["hn"], cq, ckv, kr, f["z"], f["xbc"], f["dtr"] = _norm_in_proj(head, x, _row1(p["norm_mix_pre"]), w_all)
    f.update(cq=cq, ckv=ckv)
    f["rope"] = _rope_tables(lp)
    f["q"], f["k"], f["v"], f["cqn"], f["ckvn"] = _qkv(cq, ckv, kr, *f["rope"], _row1(p["q_a_norm"]),
                                                   _row1(p["kv_a_norm"]), wq, wkn, wv)
    f["att"], f["lse"] = _flash_fwd(f["q"], f["k"], f["v"])
    f["cw"] = jnp.pad(p["conv_w"].astype(F32), ((0, 8 - CONV_K), (0, 0)))
    f["xact"] = _conv_fwd(f["xbc"], f["cw"], _row1(p["conv_b"]))
    f["dt_bias"], f["a_log"] = _row1(p["dt_bias"], LANE), _row1(p["a_log"], LANE)
    f["y"], f["hprev"], gathered = _ssd_fwd(f["xact"], f["dtr"], f["dt_bias"], f["a_log"], seq_rows,
                                            gather=[late[n] for n in LATE] if late else ())
    p = {**p, **{n: _from_shards(n, s) for n, s in zip(LATE, gathered)}}
    f["p"] = p
    f["dskip"] = jnp.repeat(p["d_skip"].reshape(-1).astype(F32), SSM_HEAD_DIM).reshape(1, SSM_WIDTH)
    f["ssm"], f["mix"], f["h1"] = _out_proj(f["att"], f["y"], f["xact"], f["z"], f["dskip"], _row1(p["ssm_norm"]),
                                            head, x, p["w_out"], _row1(p["norm_mix_post"]))
    f["hn2"], f["u"], f["a"], f["f"], f["dh2"], loss8 = _mlp_fwd(
        f["h1"], tgt, p["w_mlp_up"], p["w_mlp_down"], _row1(p["norm_mlp_pre"]), _row1(p["norm_mlp_post"]), seq_rows)
    f["loss"] = 0.5 * jnp.sum(loss8) / D_MODEL
    return f


MLPB = 256


def _mlp_bwd(dh2, f, h1, u, w_up, w_down, g_pre, g_post):
    lp = h1.shape[0]

    def body(dh2_ref, f_ref, h1_ref, u_ref, wu_hbm, wd_hbm, gpre_ref, gpost_ref,
             dh1_ref, du_ref, df_ref, dgpre_ref, dgpost_ref, wu, wd, sems):
        _resident(wu_hbm, wu, sems.at[0])
        _resident(wd_hbm, wd, sems.at[1])
        dh2_ = dh2_ref[...]
        df, dgp = _rms_bwd(f_ref[...], gpost_ref[...], dh2_)
        dfb = df.astype(BF16)
        df_ref[...] = dfb
        da = _dot(dfb, wd[...], NT)
        du = (da * (2.0 * u_ref[...].astype(F32))).astype(BF16)
        du_ref[...] = du
        dhn2 = _dot(du, wu[...], NT)
        dx, dgq = _rms_bwd(h1_ref[...], gpre_ref[...], dhn2)
        dh1_ref[...] = dh2_ + dx
        _acc(dgpre_ref, _colsum8(dgq))
        _acc(dgpost_ref, _colsum8(dgp))

    return pl.pallas_call(
        body, name="mlp_bwd", grid=(lp // MLPB,),
        in_specs=[_rowspec(MLPB, D_MODEL)] * 3 + [_rowspec(MLPB, D_FF), ANY, ANY] + [_fullspec((1, D_MODEL))] * 2,
        out_specs=[_rowspec(MLPB, D_MODEL), _rowspec(MLPB, D_FF), _rowspec(MLPB, D_MODEL),
                   _fullspec((8, D_MODEL)), _fullspec((8, D_MODEL))],
        out_shape=[_sds((lp, D_MODEL), F32), _sds((lp, D_FF), BF16), _sds((lp, D_MODEL), BF16),
                   _sds((8, D_MODEL), F32), _sds((8, D_MODEL), F32)],
        scratch_shapes=[pltpu.VMEM((D_MODEL, D_FF), BF16), pltpu.VMEM((D_FF, D_MODEL), BF16), pltpu.SemaphoreType.DMA((2,))],
        compiler_params=_cp(("arbitrary",)),
    )(dh2, f, h1, u, w_up, w_down, g_pre, g_post)


def _out_bwd(dh1, mix, att, y, xact, z, dskip, gnorm, w_out, g_post):
    lp = dh1.shape[0]

    def body(dh1_ref, mix_ref, att_ref, y_ref, x_ref, z_ref, d_ref, gn_ref, w_ref, g_ref,
             dmix_ref, datt_ref, dy_ref, dz_ref, dg_ref, dl_ref, dgn_ref, dd_ref):
        dmix, dg = _rms_bwd(mix_ref[...], g_ref[...], dh1_ref[...])
        dmb = dmix.astype(BF16)
        dmix_ref[...] = dmb
        datt = _dot(dmb, w_ref[pl.ds(0, 1024), :], NT).astype(BF16)
        datt_ref[...] = datt
        _acc(dg_ref, _colsum8(dg))
        prod = datt.astype(F32) * att_ref[...].astype(F32)
        for hh in range(ATT_HEADS):
            d = jnp.sum(prod[:, hh * V_HEAD:(hh + 1) * V_HEAD], axis=1, keepdims=True)
            dl_ref[hh] = jnp.broadcast_to(d, (ROWB, LANE)).T[0:1, :]
        do = _dot(dmb, w_ref[pl.ds(1024, 1024), :], NT)
        z_, x_ = z_ref[...], x_ref[...]
        sg = _sigmoid(z_)
        sz = z_ * sg
        y2 = y_ref[...] + d_ref[...] * x_
        gt = y2 * sz
        r = lax.rsqrt(_group_mean(gt * gt) + EPS)
        gh = gt * r
        dgh = do * gn_ref[...]
        dgt = r * (dgh - gh * _group_mean(dgh * gh))
        dy2 = dgt * sz
        dy_ref[...] = dy2
        dz_ref[...] = (dgt * y2 * (sg * (1.0 + z_ * (1.0 - sg)))).astype(BF16)
        _acc(dgn_ref, _colsum8(do * gh))
        _acc(dd_ref, _colsum8(dy2 * x_))

    return pl.pallas_call(
        body, name="out_bwd", grid=(lp // ROWB,),
        in_specs=[_rowspec(ROWB, D_MODEL)] * 6 + [_fullspec((1, SSM_WIDTH))] * 2
        + [_fullspec((2048, D_MODEL)), _fullspec((1, D_MODEL))],
        out_specs=[_rowspec(ROWB, D_MODEL)] * 4 + [_fullspec((8, D_MODEL)),
                                                   pl.BlockSpec((ATT_HEADS, 1, ROWB), lambda i: (0, 0, i)),
                                                   _fullspec((8, SSM_WIDTH)), _fullspec((8, SSM_WIDTH))],
        out_shape=[_sds((lp, D_MODEL), BF16), _sds((lp, 1024), BF16), _sds((lp, SSM_WIDTH), F32),
                   _sds((lp, SSM_WIDTH), BF16), _sds((8, D_MODEL), F32), _sds((ATT_HEADS, 1, lp), F32),
                   _sds((8, SSM_WIDTH), F32), _sds((8, SSM_WIDTH), F32)],
        compiler_params=_cp(("arbitrary",)),
    )(dh1, mix, att, y, xact, z, dskip, gnorm, w_out, g_post)


def _ssd_bwd(dy, xact, dtr, hprev, dt_bias, a_log, dskip, seq_rows, exchange=()):
    lp = xact.shape[0]
    nc = lp // CHUNK
    gw = SSM_WIDTH // SSM_GROUPS
    hpg = SSM_HEADS // SSM_GROUPS
    nb = SSM_WIDTH // (2 * SSM_STATE)
    na = len(exchange)
    cps = ROWB // CHUNK
    nsteps = nc // cps

    def body(dy_ref, x_ref, b_ref, c_ref, dtr_ref, hp_ref, bias_ref, alog_ref, dsk_ref, tri_ref, tri3_ref, trit3_ref,
             ex3_ref, ext3_ref, *rest):
        xin, (dact_ref, ddtr_ref, da_ref, dbias_ref), xout = rest[:na], rest[na:na + 4], rest[na + 4:2 * na + 4]
        dh_s, sems = rest[2 * na + 4], rest[2 * na + 5:]
        step = pl.program_id(0)

        @pl.when(step == 0)
        def _():
            dh_s[...] = jnp.zeros_like(dh_s)
            da_ref[...] = jnp.zeros_like(da_ref)
            dbias_ref[...] = jnp.zeros_like(dbias_ref)

        if na:
            x_start, x_finish = _exchange_ops(xin, xout, *sems)
            pl.when(step == 0)(x_start)

        for lc in reversed(range(cps)):
            rows = pl.ds(lc * CHUNK, CHUNK)
            chunk((nsteps - 1 - step) * cps + lc, dy_ref.at[rows], x_ref.at[rows], b_ref.at[rows], c_ref.at[rows],
                  dtr_ref.at[rows], hp_ref.at[pl.ds(lc, 1)], bias_ref, alog_ref, dsk_ref, tri_ref, tri3_ref, trit3_ref,
                  ex3_ref, ext3_ref, dact_ref.at[rows], ddtr_ref.at[rows], da_ref, dbias_ref, dh_s)

        if na:
            pl.when(step == nsteps - 1)(x_finish)

    def chunk(c, dy_ref, x_ref, b_ref, c_ref, dtr_ref, hp_ref, bias_ref, alog_ref, dsk_ref, tri_ref, tri3_ref, trit3_ref,
              ex3_ref, ext3_ref, dact_ref, ddtr_ref, da_ref, dbias_ref, dh_s):
        tri = tri_ref[...]
        ex3 = ex3_ref[...]
        dt, a, acol, valid, dtr_ = _ssd_prep(dtr_ref[...], bias_ref, alog_ref, tri3_ref[...], c, seq_rows)
        arow = acol.T
        dtrow = dt.T
        alast = acol[CHUNK - 1:CHUNK, :]
        e_all = _dot01_r(jnp.exp(acol), ex3)
        wgt0 = jnp.exp(alast - acol)
        wgt = wgt0 * dt
        wx_all = _dot01_r(wgt, ex3)
        elast = jnp.exp(alast)
        dec_all = _dot01_r(_row16(elast), ex3)[0:1, :]
        causal = tri > 0.5
        upper = tri.T > 0.5
        lane_id = lax.broadcasted_iota(jnp.int32, (1, LANE), 1)
        sub_id = lax.broadcasted_iota(jnp.int32, (CHUNK, 1), 0)
        dacol = jnp.zeros((CHUNK, LANE), F32)
        darowf = jnp.zeros((CHUNK, LANE), F32)
        ddtrowf = jnp.zeros((CHUNK, LANE), F32)
        dwgt = jnp.zeros((CHUNK, LANE), F32)
        delast = jnp.zeros((1, LANE), F32)
        for g in range(SSM_GROUPS):
            gs = slice(g * gw, (g + 1) * gw)
            ext3_g = ext3_ref[g]
            bg = b_ref[:, g * SSM_STATE:(g + 1) * SSM_STATE]
            cg = c_ref[:, g * SSM_STATE:(g + 1) * SSM_STATE]
            bgb, cgb = bg.astype(BF16), cg.astype(BF16)
            xg = x_ref[:, gs]
            dyg = dy_ref[:, gs]
            hg = hp_ref[0, :, gs]
            dhg = dh_s[:, gs]
            hgb, dhgb = hg.astype(BF16), dhg.astype(BF16)
            gm = _dot(cgb, bgb, NT)
            gmt = _dot(bgb, cgb, NT)
            y_off = _dot(cgb, hgb) * e_all[:, gs]
            dy0 = (dyg * e_all[:, gs]).astype(BF16)
            dcg = _dot(dy0, hgb, NT)
            dh_in = _dot(cg.T.astype(BF16), dy0) + dhg * dec_all[:, gs]
            dacol = dacol + _dot01_r(dyg * y_off, ext3_g)
            xw = xg * wx_all[:, gs]
            dxw = _dot(bgb, dhgb)
            dx_state = dxw * wx_all[:, gs]
            dwgt = dwgt + _dot01_r(dxw * xg, ext3_g)
            dbt = _dot(dhgb, xw.astype(BF16), NT)
            hh = _colsum8(dhg * hg)
            hh16 = jnp.concatenate([hh, jnp.zeros_like(hh)], axis=0)
            delast = delast + jnp.sum(_dot01_r(hh16, ext3_g), axis=0, keepdims=True)
            dgm = jnp.zeros((CHUNK, CHUNK), F32)
            for r in range(hpg):
                hd = g * hpg + r
                cs = slice(r * SSM_HEAD_DIM, (r + 1) * SSM_HEAD_DIM)
                acol_r, arow_r = acol[:, hd:hd + 1], arow[hd:hd + 1, :]
                dtrow_r, dtcol_r = dtrow[hd:hd + 1, :], dt[:, hd:hd + 1]
                lm = jnp.where(causal, jnp.exp(jnp.where(causal, acol_r - arow_r, 0.0)), 0.0)
                lmt = jnp.where(upper, jnp.exp(jnp.where(upper, arow_r - acol_r, 0.0)), 0.0)
                wt = gmt * lmt * dtcol_r
                dy_r = dyg[:, cs].astype(BF16)
                dx_r = _dot(wt.astype(BF16), dy_r)
                dw = _dot(dy_r, xg[:, cs].astype(BF16), NT)
                t1 = dw * lm
                dgm = dgm + t1 * dtrow_r
                q1 = t1 * gm
                m = q1 * dtrow_r
                dacol = dacol + jnp.sum(m, axis=1, keepdims=True) * (lane_id == hd).astype(F32)
                darowf = darowf - (sub_id == hd).astype(F32) * jnp.sum(m, axis=0, keepdims=True)
                ddtrowf = ddtrowf + (sub_id == hd).astype(F32) * jnp.sum(q1, axis=0, keepdims=True)
                dact_ref[:, pl.ds(hd * SSM_HEAD_DIM, SSM_HEAD_DIM)] = (
                    dx_r + dx_state[:, cs] + dyg[:, cs] * dsk_ref[:, pl.ds(hd * SSM_HEAD_DIM, SSM_HEAD_DIM)])
            dgmb = dgm.astype(BF16)
            dact_ref[:, pl.ds(SSM_WIDTH + g * SSM_STATE, SSM_STATE)] = dbt.T + _dot(dgm.T.astype(BF16), cgb)
            dact_ref[:, pl.ds(SSM_WIDTH + 2 * SSM_STATE + g * SSM_STATE, SSM_STATE)] = dcg + _dot(dgmb, bgb)
            dh_s[:, gs] = dh_in
        t = dwgt * wgt
        dalast = jnp.sum(t, axis=0, keepdims=True) + delast * elast
        dacol_tot = dacol - t + darowf.T + (sub_id == CHUNK - 1).astype(F32) * dalast
        dda = _dot01_l(trit3_ref[...], dacol_tot)
        ddt = dwgt * wgt0 + ddtrowf.T + dda * a
        ddtr = jnp.where(valid, ddt * _sigmoid(dtr_), 0.0)
        ddtr_ref[...] = ddtr
        da_ref[...] += _colsum8(dda * dt) * a
        dbias_ref[...] += _colsum8(ddtr)

    rev = lambda c: nsteps - 1 - c
    rb = cps * CHUNK
    xs_spec = pl.BlockSpec((rb, SSM_WIDTH), lambda c: (rev(c), 0))
    dact, ddtr, da8, dbias8, *received = pl.pallas_call(
        body, name="ssd_bwd", grid=(nsteps,),
        in_specs=[xs_spec, xs_spec,
                  pl.BlockSpec((rb, 2 * SSM_STATE), lambda c: (rev(c), nb)),
                  pl.BlockSpec((rb, 2 * SSM_STATE), lambda c: (rev(c), nb + 1)),
                  pl.BlockSpec((rb, LANE), lambda c: (rev(c), 0)),
                  pl.BlockSpec((cps, SSM_STATE, SSM_WIDTH), lambda c: (rev(c), 0, 0)),
                  _fullspec((1, LANE)), _fullspec((1, LANE)), _fullspec((1, SSM_WIDTH)),
                  _fullspec((CHUNK, CHUNK)), _fullspec((CHUNK, 3 * CHUNK)), _fullspec((CHUNK, 3 * CHUNK)),
                  _fullspec((3 * LANE, SSM_WIDTH)), _fullspec((SSM_GROUPS, 3 * gw, LANE))] + [ANY] * na,
        out_specs=[pl.BlockSpec((rb, CONV_DIM), lambda c: (rev(c), 0)), pl.BlockSpec((rb, LANE), lambda c: (rev(c), 0)),
                   _fullspec((8, LANE)), _fullspec((8, LANE))] + [ANY] * na,
        out_shape=[_sds((lp, CONV_DIM), F32), _sds((lp, LANE), F32), _sds((8, LANE), F32), _sds((8, LANE), F32)]
        + [_sds(e.shape, e.dtype) for e in exchange],
        scratch_shapes=[pltpu.VMEM((SSM_STATE, SSM_WIDTH), F32)] + (_gather_scratch(na) if na else []),
        compiler_params=_cp(("arbitrary",)),
    )(dy, xact, xact, xact, dtr, hprev, dt_bias, a_log, dskip, _tri_mat(), _x3(_tri_mat(), 1), _x3(_tri_mat().T, 1),
      _x3(_expand_mat(), 0), jnp.stack([_x3(_expand_mat().T[g * gw:(g + 1) * gw], 0) for g in range(SSM_GROUPS)]),
      *exchange)
    return dact, ddtr, da8, dbias8, received


def _conv_bwd(dact, xbc, cw, cb):
    lp, c = xbc.shape
    t8 = ROWB // 8
    nb = lp // ROWB

    def body(d_ref, dnext_ref, x_ref, prev_ref, next_ref, w_ref, b_ref, dx_ref, dw_ref, db_ref, dp):
        i = pl.program_id(0)
        last = i == nb - 1

        @pl.when(i == 0)
        def _():
            dw_ref[...] = jnp.zeros_like(dw_ref)
            db_ref[...] = jnp.zeros_like(db_ref)

        sub = lax.broadcasted_iota(jnp.int32, (8, 1), 0)
        x0 = 8 - (CONV_K - 1)

        def strip(s, carry):
            cs = pl.ds(pl.multiple_of(s * LANE, LANE), LANE)
            w, b = w_ref[:, cs], b_ref[:, cs]

            halo = jnp.where(i > 0, prev_ref[:, cs], 0.0)
            after = jnp.where(last, 0.0, next_ref[:, cs])

            def dpre_rows(r0, n, d):
                win = (jnp.concatenate([x_ref[pl.ds(ROWB - 8, 8), cs], after], axis=0) if r0 == ROWB
                       else _block_window(x_ref, halo, r0, n, cs))
                xs = _window_rows(win, n, [x0 + kk for kk in range(CONV_K)])
                pre = b + sum(w[kk:kk + 1, :] * xs[kk] for kk in range(CONV_K))
                sg = _sigmoid(pre)
                return d * (sg * (1.0 + pre * (1.0 - sg))), xs

            dws = [jnp.zeros((8, LANE), F32) for _ in range(CONV_K)]
            dbs = jnp.zeros((8, LANE), F32)
            for r0 in range(0, ROWB, CONV_ROWS):
                dpre, xs = dpre_rows(r0, CONV_ROWS, d_ref[pl.ds(r0, CONV_ROWS), cs])
                dp[pl.ds(r0, CONV_ROWS), cs] = dpre
                dbs = dbs + _colsum8(dpre)
                for kk in range(CONV_K):
                    dws[kk] = dws[kk] + _colsum8(dpre * xs[kk])
            dp[pl.ds(ROWB, 8), cs] = dpre_rows(ROWB, 8, jnp.where(last, 0.0, dnext_ref[:, cs]))[0]
            dwv = sum(jnp.where(sub == kk, jnp.sum(dws[kk], axis=0, keepdims=True), 0.0) for kk in range(CONV_K))
            dw_ref[:, cs] += dwv
            db_ref[:, cs] += dbs
            for r0 in range(0, ROWB, CONV_ROWS):
                ahead = _shifted_rows(dp, r0, CONV_ROWS, cs, [CONV_K - 1 - kk for kk in range(CONV_K)])
                dx = sum(w[kk:kk + 1, :] * ahead[kk] for kk in range(CONV_K))
                dx_ref[pl.ds(r0, CONV_ROWS), cs] = dx.astype(BF16)
            return carry

        lax.fori_loop(0, c // LANE, strip, 0)

    nxt = lambda i: (jnp.minimum((i + 1) * t8, lp // 8 - 1), 0)
    prv = lambda i: (jnp.maximum(i * t8 - 1, 0), 0)
    return pl.pallas_call(
        body, name="conv_bwd", grid=(nb,),
        in_specs=[_rowspec(ROWB, c), pl.BlockSpec((8, c), nxt), _rowspec(ROWB, c), pl.BlockSpec((8, c), prv),
                  pl.BlockSpec((8, c), nxt), _fullspec((8, c)), _fullspec((1, c))],
        out_specs=[_rowspec(ROWB, c), _fullspec((8, c)), _fullspec((8, c))],
        out_shape=[_sds((lp, c), BF16), _sds((8, c), F32), _sds((8, c), F32)],
        scratch_shapes=[pltpu.VMEM((ROWB + 8, c), F32)],
        compiler_params=_cp(("arbitrary",)),
    )(dact, dact, xbc, xbc, xbc, cw, cb)


def _flash_bwd(q, k, v, datt, lse_row, delta_row, cos, sa, sb):
    lp = q.shape[0]
    nk = lp // ROWB

    def body(k_ref, v_ref, q_ref, do_ref, lse_ref, dl_ref, cos_ref, sa_ref, sb_ref, dq_ref, dk_ref, dv_ref,
             dq_acc, dk_acc, dv_acc):
        j = pl.program_id(1)

        @pl.when(j == 0)
        def _():
            dq_acc[...] = jnp.zeros_like(dq_acc)

        dk_acc[...] = jnp.zeros_like(dk_acc)
        dv_acc[...] = jnp.zeros_like(dv_acc)

        def tile(i, masked, key0=0, nkeys=ROWB):
            keys = pl.ds(key0, nkeys)
            kb, vb = k_ref[keys, :], v_ref[keys, :]
            off = pl.multiple_of(i * ROWB, ROWB)
            qb = q_ref[pl.ds(off, ROWB), :]
            dob = do_ref[pl.ds(off, ROWB), :]
            lse_r = lse_ref[0, :, pl.ds(off, ROWB)]
            dl_r = dl_ref[0, :, pl.ds(off, ROWB)]
            st = _dot(kb, qb, NT)
            if masked:
                krow = j * ROWB + key0 + lax.broadcasted_iota(jnp.int32, st.shape, 0)
                qrow = i * ROWB + lax.broadcasted_iota(jnp.int32, st.shape, 1)
                st = jnp.where(_att_ok(qrow, krow), st, NEG)
            pt = jnp.exp2(st - lse_r)
            dv_acc[keys, :] += _dot(pt.astype(BF16), dob)
            dpt = _dot(vb, dob, NT)
            dst = (pt * (dpt - dl_r)).astype(BF16)
            dk_acc[keys, :] += _dot(dst, qb)
            dq_acc[pl.ds(off, ROWB), :] += _dot(dst, kb, ((0,), (0,)))

        @pl.when(j == 0)
        def _():
            _pair_loop(0, nk, lambda i: tile(i, True, ROWB - META_KEYS, META_KEYS), (16, 8, 4, 2))

        odd = jnp.bitwise_and(nk - 1 - j, 3)
        for r in range(4):
            @pl.when((j > 0) & (odd == r))
            def _(r=r):
                tile(j, True)
                for d in range(r):
                    tile(j + 1 + d, False)

        @pl.when(j > 0)
        def _():
            _pair_loop(j + 1 + odd, nk, lambda i: tile(i, False), (16, 8, 4))

        dk_ref[...] = (dk_acc[...] * LN2).astype(BF16)
        dv_ref[...] = dv_acc[...].astype(BF16)
        dq = dq_acc[pl.ds(pl.multiple_of(j * ROWB, ROWB), ROWB), :] * ATT_SCALE
        half = HEADW // 2
        dq_ref[:, pl.ds(0, half)] = dq[:, :half].astype(BF16)
        dq_ref[:, pl.ds(half, half)] = _rope_t(dq[:, half:], cos_ref[...], sa_ref[...], sb_ref[...]).astype(BF16)

    stat = pl.BlockSpec((1, 1, lp), lambda h, j: (h, 0, 0))
    blk = pl.BlockSpec((ROWB, HEADW), lambda h, j: (j, h))
    tab = pl.BlockSpec((ROWB, HEADW // 2), lambda h, j: (j, 0))
    return pl.pallas_call(
        body, name="flash_bwd", grid=(ATT_HEADS, nk),
        in_specs=[blk, pl.BlockSpec((ROWB, V_HEAD), lambda h, j: (j, 2 * h)),
                  pl.BlockSpec((lp, HEADW), lambda h, j: (0, h)), pl.BlockSpec((lp, V_HEAD), lambda h, j: (0, h)),
                  stat, stat, tab, tab, tab],
        out_specs=[blk, blk, pl.BlockSpec((ROWB, V_HEAD), lambda h, j: (j, h))],
        out_shape=[_sds((lp, ATT_HEADS * HEADW), BF16), _sds((lp, ATT_HEADS * HEADW), BF16),
                   _sds((lp, ATT_HEADS * V_HEAD), BF16)],
        scratch_shapes=[pltpu.VMEM((lp, HEADW), F32), pltpu.VMEM((ROWB, HEADW), F32), pltpu.VMEM((ROWB, V_HEAD), F32)],
        compiler_params=_cp(("arbitrary", "arbitrary")),
    )(k, v, q, datt, lse_row, delta_row, cos, sa, sb)


def _qkv_bwd(dqp, dk, dv, cq, ckv, cos, sa, sb, gq, gkv, wq, wk, wv, dz, dxbc, ddtr):
    lp = cq.shape[0]
    qw = ATT_HEADS * HEADW

    def body(dqp_ref, dk_ref, dv_ref, cq_ref, ckv_ref, cos_ref, sa_ref, sb_ref, gq_ref, gkv_ref, wq_ref, wk_ref, wv_ref,
             dz_ref, dxbc_ref, ddtr_ref, dp_ref, dgq_ref, dgkv_ref):
        dcq, dgq = _rms_bwd(cq_ref[...], gq_ref[...], _dot(dqp_ref[...], wq_ref[...], NT))
        dp_ref[:, pl.ds(PC_Q, Q_LORA)] = dcq.astype(BF16)
        dkb = dk_ref[...]
        half = HEADW // 2
        dksum = sum(dkb[:, hh * HEADW + half:(hh + 1) * HEADW].astype(F32) for hh in range(ATT_HEADS))
        dp_ref[:, pl.ds(PC_KR, half)] = jnp.zeros((ROWB, half), BF16)
        dp_ref[:, pl.ds(PC_KR + half, half)] = _rope_t(dksum, cos_ref[...], sa_ref[...], sb_ref[...]).astype(BF16)
        dckvn = _dot(dkb, wk_ref[...], NT) + _dot(dv_ref[...], wv_ref[...], NT)
        dckv, dgkv = _rms_bwd(ckv_ref[...], gkv_ref[...], dckvn)
        dp_ref[:, pl.ds(PC_KV, KV_LORA)] = dckv.astype(BF16)
        dp_ref[:, pl.ds(PC_Z, SSM_WIDTH)] = dz_ref[...]
        dp_ref[:, pl.ds(PC_XBC, CONV_DIM)] = dxbc_ref[...]
        dp_ref[:, pl.ds(PC_DT, LANE)] = ddtr_ref[...].astype(BF16)
        _acc(dgq_ref, _colsum8(dgq))
        _acc(dgkv_ref, _colsum8(dgkv))

    return pl.pallas_call(
        body, name="qkv_bwd", grid=(lp // ROWB,),
        in_specs=[_rowspec(ROWB, qw), _rowspec(ROWB, qw), _rowspec(ROWB, ATT_HEADS * V_HEAD),
                  _rowspec(ROWB, Q_LORA), _rowspec(ROWB, KV_LORA)] + [_rowspec(ROWB, HEADW // 2)] * 3
        + [_fullspec((1, Q_LORA)), _fullspec((1, KV_LORA)), _fullspec((Q_LORA, qw)), _fullspec((KV_LORA, qw)),
           _fullspec((KV_LORA, ATT_HEADS * V_HEAD)), _rowspec(ROWB, SSM_WIDTH), _rowspec(ROWB, CONV_DIM),
           _rowspec(ROWB, LANE)],
        out_specs=[_rowspec(ROWB, PROJ_W), _fullspec((8, Q_LORA)), _fullspec((8, KV_LORA))],
        out_shape=[_sds((lp, PROJ_W), BF16), _sds((8, Q_LORA), F32), _sds((8, KV_LORA), F32)],
        compiler_params=_cp(("arbitrary",)),
    )(dqp, dk, dv, cq, ckv, cos, sa, sb, gq, gkv, wq, wk, wv, dz, dxbc, ddtr)


def _in_bwd(dproj, head, x, dh1, g, w_all, exchange=()):
    lp = dh1.shape[0]
    nsteps = lp // ROWB
    na = len(exchange)

    def body(dp_ref, head_ref, x_ref, dh1_ref, g_ref, w_ref, *rest):
        xin, (dx_ref, dhead_ref, dg_ref), xout, sems = rest[:na], rest[na:na + 3], rest[na + 3:2 * na + 3], rest[2 * na + 3:]
        step = pl.program_id(0)
        if na:
            x_start, x_finish = _exchange_ops(xin, xout, *sems)
            pl.when(step == 0)(x_start)
        dx, dg = _rms_bwd(_h_block(head_ref, x_ref), g_ref[...], _dot(dp_ref[...], w_ref[...], NT))
        dh = dh1_ref[...] + dx

        @pl.when(step == 0)
        def _():
            dhead_ref[...] = dh

        @pl.when(step > 0)
        def _():
            dx_ref[...] = dh

        _acc(dg_ref, _colsum8(dg))
        if na:
            pl.when(step == nsteps - 1)(x_finish)

    dx, dhead, dg8, *received = pl.pallas_call(
        body, name="in_bwd", grid=(nsteps,),
        in_specs=[_rowspec(ROWB, PROJ_W), _fullspec((ROWB, D_MODEL)), _xspec(), _rowspec(ROWB, D_MODEL),
                  _fullspec((1, D_MODEL)), _fullspec((D_MODEL, PROJ_W))] + [ANY] * na,
        out_specs=[_xspec(), _fullspec((ROWB, D_MODEL)), _fullspec((8, D_MODEL))] + [ANY] * na,
        out_shape=[_sds(x.shape, F32), _sds((ROWB, D_MODEL), F32), _sds((8, D_MODEL), F32)]
        + [_sds(e.shape, e.dtype) for e in exchange],
        scratch_shapes=_gather_scratch(na) if na else [],
        compiler_params=_cp(("arbitrary",)),
    )(dproj, head, x, dh1, g, w_all, *exchange)
    return dx, dhead, dg8, received


def _tile_of(n, cap=1024):
    return max(t for t in range(LANE, min(n, cap) + 1, LANE) if n % t == 0)


def _matmul_tn(name, a, b):
    rows, kd = a.shape
    nd = b.shape[1]
    tk, tn = _tile_of(kd), _tile_of(nd)
    rb = 3 * ROWB if rows % (3 * ROWB) == 0 else ROWB

    def body(a_ref, b_ref, o_ref):
        @pl.when(pl.program_id(2) == 0)
        def _():
            o_ref[...] = jnp.zeros_like(o_ref)

        o_ref[...] += _dot(a_ref[...], b_ref[...], ((0,), (0,)))

    return pl.pallas_call(
        body, name=name, grid=(kd // tk, nd // tn, rows // rb),
        in_specs=[pl.BlockSpec((rb, tk), lambda i, j, r: (r, i)), pl.BlockSpec((rb, tn), lambda i, j, r: (r, j))],
        out_specs=pl.BlockSpec((tk, tn), lambda i, j, r: (i, j)), out_shape=_sds((kd, nd), F32),
        compiler_params=_cp(("arbitrary", "arbitrary", "arbitrary")),
    )(a, b)


def _local_backward(head, x, f, exchange_late=False):
    p = f["p"]
    g = {}
    row = lambda v: _row1(v)
    s8 = lambda v: jnp.sum(v, axis=0)
    dh1, du, df, dgpre, dgpost = _mlp_bwd(f["dh2"], f["f"], f["h1"], f["u"], p["w_mlp_up"], p["w_mlp_down"],
                                          row(p["norm_mlp_pre"]), row(p["norm_mlp_post"]))
    g["norm_mlp_pre"], g["norm_mlp_post"] = s8(dgpre), s8(dgpost)
    g["w_mlp_up"] = _matmul_tn("dw_mlp_up", f["hn2"], du)
    g["w_mlp_down"] = _matmul_tn("dw_mlp_down", f["a"], df)
    dmix, datt, dy, dz, dgmp, delta, dgn, dd = _out_bwd(dh1, f["mix"], f["att"], f["y"], f["xact"], f["z"], f["dskip"],
                                                        row(p["ssm_norm"]), p["w_out"], row(p["norm_mix_post"]))
    g["norm_mix_post"] = s8(dgmp)
    g["w_out"] = jnp.concatenate([_matmul_tn("dw_out_att", f["att"], dmix), _matmul_tn("dw_out_ssm", f["ssm"], dmix)], axis=0)
    g["ssm_norm"] = s8(dgn)
    g["d_skip"] = s8(dd).reshape(SSM_HEADS, SSM_HEAD_DIM).sum(axis=1)
    dact, ddtr, da8, dbias8, received = _ssd_bwd(
        dy, f["xact"], f["dtr"], f["hprev"], f["dt_bias"], f["a_log"], f["dskip"], f["seq_rows"],
        exchange=[_to_chunks(n, g[n]).astype(BF16) for n in LATE] if exchange_late else ())
    g["a_log"], g["dt_bias"] = s8(da8)[:SSM_HEADS], s8(dbias8)[:SSM_HEADS]
    dxbc, dcw8, dcb8 = _conv_bwd(dact, f["xbc"], f["cw"], row(p["conv_b"]))
    g["conv_w"], g["conv_b"] = dcw8[:CONV_K], s8(dcb8)
    dqp, dkb, dv = _flash_bwd(f["q"], f["k"], f["v"], datt, f["lse"], delta, *f["rope"])
    dproj, dgq, dgkv = _qkv_bwd(dqp, dkb, dv, f["cq"], f["ckv"], *f["rope"], row(p["q_a_norm"]), row(p["kv_a_norm"]),
                                f["wq"], f["wk"], f["wv"], dz, dxbc, ddtr)
    g["q_a_norm"], g["kv_a_norm"] = s8(dgq), s8(dgkv)
    dwq = _matmul_tn("dw_q_up", f["cqn"], dqp).reshape(Q_LORA, ATT_HEADS, HEADW)
    g["w_q_up"] = dwq[:, :, :QK_NOPE + QK_ROPE].reshape(Q_LORA, -1)
    dwk = _matmul_tn("dw_k_up", f["ckvn"], dkb).reshape(KV_LORA, ATT_HEADS, HEADW)[:, :, :QK_NOPE]
    dwv = _matmul_tn("dw_v_up", f["ckvn"], dv).reshape(KV_LORA, ATT_HEADS, V_HEAD)
    g["w_kv_up"] = jnp.concatenate([dwk, dwv], axis=2).reshape(KV_LORA, -1)
    dwa = _matmul_tn("dw_in", f["hn"], dproj)
    g["w_in"] = jnp.concatenate([dwa[:, PC_Q:PC_KR], dwa[:, PC_KR + QK_NOPE:PC_KR + QK_NOPE + QK_ROPE],
                                 dwa[:, PC_Z:PC_DT + SSM_HEADS]], axis=1)
    dx, dhead, dgin, received_mid = _in_bwd(
        dproj, head, x, dh1, row(p["norm_mix_pre"]), f["w_all"],
        exchange=[_to_chunks(n, g[n]).astype(BF16) for n in MID] if exchange_late else ())
    g["norm_mix_pre"] = s8(dgin)
    g["meta_tokens"] = dhead[PADF:]
    return dx, g, dict(zip(LATE + MID, list(received) + list(received_mid)))


BIG = {"w_in": ((D_MODEL, IN_WIDTH), 1), "w_q_up": ((Q_LORA, ATT_HEADS * (QK_NOPE + QK_ROPE)), 1),
       "w_kv_up": ((KV_LORA, ATT_HEADS * (QK_NOPE + V_HEAD)), 1), "w_out": ((2 * D_MODEL, D_MODEL), 0),
       "w_mlp_up": ((D_MODEL, D_FF), 1), "w_mlp_down": ((D_FF, D_MODEL), 0), "conv_w": ((CONV_K, CONV_DIM), 1),
       "meta_tokens": ((N_META, D_MODEL), 1)}
SMALL = {"norm_mix_pre": D_MODEL, "q_a_norm": Q_LORA, "kv_a_norm": KV_LORA, "conv_b": CONV_DIM, "dt_bias": SSM_HEADS,
         "a_log": SSM_HEADS, "d_skip": SSM_HEADS, "ssm_norm": SSM_WIDTH, "norm_mix_post": D_MODEL,
         "norm_mlp_pre": D_MODEL, "norm_mlp_post": D_MODEL}
WEIGHT_ORDER = ("meta_tokens", "norm_mix_pre", "w_in", "q_a_norm", "w_q_up", "kv_a_norm", "w_kv_up", "conv_w", "conv_b",
                "dt_bias", "a_log", "d_skip", "ssm_norm", "w_out", "norm_mix_post", "norm_mlp_pre", "w_mlp_up",
                "w_mlp_down", "norm_mlp_post")
ADAM_ROWS = 256


def _shard_shape(name):
    shape, ax = BIG[name]
    return tuple(d // N_DEV if a == ax else d for a, d in enumerate(shape))


SMALL_ROWS = -(-sum(SMALL.values()) // (LANE * 8)) * 8


def _pack(flats, rows):
    v = jnp.concatenate([f.reshape(-1) for f in flats])
    return jnp.pad(v, (0, rows * LANE - v.shape[0])).reshape(rows, LANE)


def _unpack(packed, shapes):
    v = packed.reshape(-1)
    out, o = [], 0
    for s in shapes:
        n = math.prod(s)
        out.append(v[o:o + n].reshape(s))
        o += n
    return out


def _to_chunks(name, full):
    shape, ax = BIG[name]
    if ax == 0:
        return full.reshape((N_DEV,) + _shard_shape(name))
    k, n = shape
    return full.reshape(k, N_DEV, n // N_DEV).transpose(1, 0, 2)


def _from_shards(name, shards):
    shape, ax = BIG[name]
    if ax == 0:
        return shards.reshape(shape)
    return shards.transpose(1, 0, 2).reshape(shape)


def _peer(k):
    x, y, c = lax.axis_index("x"), lax.axis_index("y"), lax.axis_index("c")
    px = 1 - x if k & 4 else x
    py = 1 - y if k & 2 else y
    pc = 1 - c if k & 1 else c
    return (px, py, pc), 4 * px + 2 * py + pc


def _gather_ops(x_refs, out_refs, send_sems, recv_sems, local_sems):
    na = len(x_refs)
    chips = (4, 2, 6)

    def copy(a, n, block, to, src=None):
        return pltpu.make_async_remote_copy(
            src_ref=out_refs[a].at[block] if src is None else src, dst_ref=out_refs[a].at[block],
            send_sem=send_sems.at[7 * a + n], recv_sem=recv_sems.at[7 * a + n], device_id=to, device_id_type=MESH)

    def mine():
        me = _peer(0)[1]
        return [pltpu.make_async_copy(x_refs[a], out_refs[a].at[me], local_sems.at[a]) for a in range(na)]

    def first():
        me, sibling = _peer(0)[1], _peer(1)[0]
        out = [copy(a, 0, me, sibling, src=x_refs[a]) for a in range(na)]
        return out + [copy(a, 1 + n, me, _peer(k)[0], src=x_refs[a]) for n, k in enumerate(chips) for a in range(na)]

    def passed():
        sibling = _peer(1)[0]
        return [copy(a, 4 + n, _peer(k)[1], sibling) for n, k in enumerate(chips) for a in range(na)]

    def start():
        for cp in mine() + first():
            cp.start()

    def forward():
        sibling = _peer(1)[0]
        fwd = passed()
        for n, k in enumerate(chips):
            for a in range(na):
                copy(a, 1 + n, _peer(k)[1], sibling).wait_recv()
                fwd[n * na + a].start()

    def finish():
        sibling = _peer(1)[0]
        for a in range(na):
            copy(a, 0, _peer(1)[1], sibling).wait_recv()
        for n, k in enumerate(chips):
            for a in range(na):
                copy(a, 4 + n, _peer(k | 1)[1], sibling).wait_recv()
        for cp in first() + passed():
            cp.wait_send()
        for cp in mine():
            cp.wait()

    return start, forward, finish


def _gather_scratch(na):
    return [pltpu.SemaphoreType.DMA((7 * na,)), pltpu.SemaphoreType.DMA((7 * na,)), pltpu.SemaphoreType.DMA((na,))]


def _all_gather(shards):
    na = len(shards)

    def body(*refs):
        for step in _gather_ops(refs[:na], refs[na:2 * na], *refs[2 * na:]):
            step()

    return pl.pallas_call(
        body, name="all_gather_weights", out_shape=[_sds((N_DEV,) + s.shape, s.dtype) for s in shards],
        in_specs=[ANY] * na, out_specs=[ANY] * na, scratch_shapes=_gather_scratch(na),
    )(*shards)


def _exchange(chunks, small):
    na = len(chunks) + 1

    def body(*refs):
        for step in _exchange_ops(refs[:na], refs[na:2 * na], *refs[2 * na:], whole=(na - 1,)):
            step()

    arrays = list(chunks) + [small]
    return pl.pallas_call(
        body, name="exchange_grads",
        out_shape=[_sds(c.shape, c.dtype) for c in chunks] + [_sds((N_DEV,) + small.shape, small.dtype)],
        in_specs=[ANY] * na, out_specs=[ANY] * na, scratch_shapes=_gather_scratch(na),
    )(*arrays)


def _exchange_ops(in_refs, out_refs, send_sems, recv_sems, local_sems, whole=()):
    na = len(in_refs)

    def src(a, idx):
        return in_refs[a] if a in whole else in_refs[a].at[idx]

    def own():
        me = _peer(0)[1]
        return [pltpu.make_async_copy(src(a, me), out_refs[a].at[me], local_sems.at[a]) for a in range(na)]

    def copy(a, k, sending):
        me = _peer(0)[1]
        to, idx = _peer(k)
        return pltpu.make_async_remote_copy(
            src_ref=src(a, idx if sending else me), dst_ref=out_refs[a].at[me if sending else idx],
            send_sem=send_sems.at[7 * a + k - 1], recv_sem=recv_sems.at[7 * a + k - 1],
            device_id=to, device_id_type=MESH)

    def sent():
        return [copy(a, k, True) for k in range(1, N_DEV) for a in range(na)]

    def start():
        for cp in own() + sent():
            cp.start()

    def finish():
        for k in range(1, N_DEV):
            for a in range(na):
                copy(a, k, False).wait_recv()
        for cp in sent():
            cp.wait_send()
        for cp in own():
            cp.wait()

    return start, finish


def _reduce_adamw(name, recv, w, m, v):
    rows, cols = w.shape
    blk = ADAM_ROWS if rows % ADAM_ROWS == 0 else rows
    c1 = 1.0 - ADAM_B1 ** ADAM_STEP
    c2 = 1.0 - ADAM_B2 ** ADAM_STEP

    def body(r_ref, w_ref, m_ref, v_ref, g_ref, d_ref, nm_ref, nv_ref):
        g = r_ref[0].astype(F32)
        for s in range(1, N_DEV):
            g = g + r_ref[s].astype(F32)
        g_ref[...] = g
        m_ = ADAM_B1 * m_ref[...] + (1.0 - ADAM_B1) * g
        v_ = ADAM_B2 * v_ref[...] + (1.0 - ADAM_B2) * (g * g)
        nm_ref[...] = m_
        nv_ref[...] = v_
        d_ref[...] = -ADAM_LR * ((m_ / c1) / (jnp.sqrt(v_ / c2) + ADAM_EPS) + ADAM_WD * w_ref[...])

    spec = _rowspec(blk, cols)
    return pl.pallas_call(
        body, name="reduce_adamw_" + name, grid=(rows // blk,),
        in_specs=[pl.BlockSpec((N_DEV, blk, cols), lambda i: (0, i, 0)), spec, spec, spec],
        out_specs=[spec] * 4, out_shape=[_sds((rows, cols), F32)] * 4,
        compiler_params=_cp(("arbitrary",)),
    )(recv, w, m, v)


def kernel(x, meta_tokens, norm_mix_pre, w_in, q_a_norm, w_q_up, kv_a_norm, w_kv_up, conv_w, conv_b, dt_bias, a_log, d_skip, ssm_norm, w_out, norm_mix_post, norm_mlp_pre, w_mlp_up, w_mlp_down, norm_mlp_post, loss_target, m_meta_tokens, m_norm_mix_pre, m_w_in, m_q_a_norm, m_w_q_up, m_kv_a_norm, m_w_kv_up, m_conv_w, m_conv_b, m_dt_bias, m_a_log, m_d_skip, m_ssm_norm, m_w_out, m_norm_mix_post, m_norm_mlp_pre, m_w_mlp_up, m_w_mlp_down, m_norm_mlp_post, v_meta_tokens, v_norm_mix_pre, v_w_in, v_q_a_norm, v_w_q_up, v_kv_a_norm, v_w_kv_up, v_conv_w, v_conv_b, v_dt_bias, v_a_log, v_d_skip, v_ssm_norm, v_w_out, v_norm_mix_post, v_norm_mlp_pre, v_w_mlp_up, v_w_mlp_down, v_norm_mlp_post):
    w = dict(meta_tokens=meta_tokens, norm_mix_pre=norm_mix_pre, w_in=w_in, q_a_norm=q_a_norm, w_q_up=w_q_up,
             kv_a_norm=kv_a_norm, w_kv_up=w_kv_up, conv_w=conv_w, conv_b=conv_b, dt_bias=dt_bias, a_log=a_log,
             d_skip=d_skip, ssm_norm=ssm_norm, w_out=w_out, norm_mix_post=norm_mix_post, norm_mlp_pre=norm_mlp_pre,
             w_mlp_up=w_mlp_up, w_mlp_down=w_mlp_down, norm_mlp_post=norm_mlp_post)
    m = dict(meta_tokens=m_meta_tokens, norm_mix_pre=m_norm_mix_pre, w_in=m_w_in, q_a_norm=m_q_a_norm, w_q_up=m_w_q_up,
             kv_a_norm=m_kv_a_norm, w_kv_up=m_w_kv_up, conv_w=m_conv_w, conv_b=m_conv_b, dt_bias=m_dt_bias,
             a_log=m_a_log, d_skip=m_d_skip, ssm_norm=m_ssm_norm, w_out=m_w_out, norm_mix_post=m_norm_mix_post,
             norm_mlp_pre=m_norm_mlp_pre, w_mlp_up=m_w_mlp_up, w_mlp_down=m_w_mlp_down, norm_mlp_post=m_norm_mlp_post)
    v = dict(meta_tokens=v_meta_tokens, norm_mix_pre=v_norm_mix_pre, w_in=v_w_in, q_a_norm=v_q_a_norm, w_q_up=v_w_q_up,
             kv_a_norm=v_kv_a_norm, w_kv_up=v_w_kv_up, conv_w=v_conv_w, conv_b=v_conv_b, dt_bias=v_dt_bias,
             a_log=v_a_log, d_skip=v_d_skip, ssm_norm=v_ssm_norm, w_out=v_w_out, norm_mix_post=v_norm_mix_post,
             norm_mlp_pre=v_norm_mlp_pre, w_mlp_up=v_w_mlp_up, w_mlp_down=v_w_mlp_down, norm_mlp_post=v_norm_mlp_post)
    big_names = [n for n in WEIGHT_ORDER if n in BIG]
    small_names = [n for n in WEIGHT_ORDER if n in SMALL]
    shard = lambda d, n: d[n].reshape(_shard_shape(n))

    f32_names = ("conv_w", "meta_tokens")
    early = [n for n in big_names if n not in LATE]
    gathered = _all_gather([shard(w, n).astype(F32 if n in f32_names else BF16) for n in early])
    p = {n: w[n].reshape(-1) for n in small_names}
    p.update({n: _from_shards(n, s) for n, s in zip(early, gathered)})
    head = jnp.concatenate([jnp.zeros((PADF, D_MODEL), F32), p["meta_tokens"]], axis=0)
    f = _local_forward(head, x[0], loss_target[0], p, late={n: shard(w, n).astype(BF16) for n in LATE})
    dx, g, recv_of = _local_backward(head, x[0], f, exchange_late=True)
    grad_x = dx[None]
    loss = lax.psum(f["loss"], ("x", "y", "c"))

    rest = [n for n in big_names if n not in LATE + MID]
    small = _pack([g[n] for n in small_names], SMALL_ROWS)
    *recv_rest, recv_small = _exchange([_to_chunks(n, g[n]) for n in rest], small)
    recv_of.update(zip(rest, recv_rest))

    outs = {}
    kinds = ("grad", "delta", "new_m", "new_v")
    for n, recv in ((n, recv_of[n]) for n in big_names):
        for kind, arr in zip(kinds, _reduce_adamw(n, recv, shard(w, n), shard(m, n), shard(v, n))):
            outs[kind, n] = arr.reshape(w[n].shape)
    packed = [_pack([d[n] for n in small_names], SMALL_ROWS) for d in (w, m, v)]
    for kind, arr in zip(kinds, _reduce_adamw("small", recv_small, *packed)):
        for n, val in zip(small_names, _unpack(arr, [(SMALL[n],) for n in small_names])):
            outs[kind, n] = val.reshape(w[n].shape)
    return (loss, grad_x) + tuple(outs[kind, n] for kind in ("grad", "delta", "new_m", "new_v") for n in WEIGHT_ORDER)
```
